```python
import jax, jax.numpy as jnp
from jax import lax
import numpy as np

D_MODEL = 1024
BATCH = 8
SEQ = 2048
DEPTH = 4

D_HGRN = D_MODEL
HGRN_EXPAND = 128
HGRN_HEADS = D_HGRN // HGRN_EXPAND
HEAD_K = HGRN_EXPAND
HEAD_V = D_HGRN // HGRN_HEADS
CHUNK = 64
D_CONV = D_MODEL
CONV_WIDTH = 31
FFN_HIDDEN = -(-8 * D_MODEL // (3 * 256)) * 256
ALPHA = (2 * DEPTH) ** 0.25
BETA = (8 * DEPTH) ** -0.25
LN_EPS = 1e-5
RMS_EPS = 1e-6
F_MIN = 1e-30
IN_SPLITS = (D_HGRN, D_HGRN, D_HGRN, D_HGRN, D_CONV, D_CONV, D_MODEL, D_MODEL)
D_IN = sum(IN_SPLITS)

kernel_name = "hybrid_hgrn2_conformer_deepnorm"


def layer_norm(x, g, b):
    xf = x.astype(jnp.float32)
    mu = jnp.mean(xf, axis=-1, keepdims=True)
    var = jnp.mean(jnp.square(xf - mu), axis=-1, keepdims=True)
    y = (xf - mu) * lax.rsqrt(var + LN_EPS) * g.astype(jnp.float32) + b.astype(jnp.float32)
    return y.astype(x.dtype)


def hgrn2_mixer(q_raw, f_raw, i_raw, g_raw, lb, g_norm_w):
    B, T, _ = q_raw.shape
    n_chunks = T // CHUNK
    q = jax.nn.silu(q_raw.astype(jnp.float32))
    z = f_raw.astype(jnp.float32)
    lb = lb.astype(jnp.float32)
    f = lb + (1.0 - lb) * jax.nn.sigmoid(z)
    log_f = jnp.log(jnp.maximum(f, F_MIN))
    k = (1.0 - lb) * jax.nn.sigmoid(-z)
    v = i_raw.astype(jnp.float32)

    def to_chunks(t, dh):
        return t.reshape(B, n_chunks, CHUNK, HGRN_HEADS, dh).transpose(1, 0, 3, 2, 4)

    qc, kc, gc = to_chunks(q, HEAD_K), to_chunks(k, HEAD_K), to_chunks(log_f, HEAD_K)
    vc = to_chunks(v, HEAD_V)
    causal = jnp.tril(jnp.ones((CHUNK, CHUNK), dtype=bool))[:, :, None]

    def chunk_step(S, inp):
        q_, k_, v_, g_ = inp
        G = jnp.cumsum(g_, axis=2)
        o_inter = jnp.einsum('bhtk,bhkv->bhtv', q_ * jnp.exp(G), S)
        diff = G[:, :, :, None, :] - G[:, :, None, :, :]
        decay = jnp.where(causal, jnp.exp(jnp.minimum(diff, 0.0)), 0.0)
        scores = jnp.einsum('bhtk,bhsk,bhtsk->bhts', q_, k_, decay)
        o_intra = jnp.einsum('bhts,bhsv->bhtv', scores, v_)
        G_last = G[:, :, -1:, :]
        S_new = jnp.exp(G_last[:, :, 0, :])[..., None] * S + jnp.einsum(
            'bhsk,bhsv->bhkv', k_ * jnp.exp(G_last - G), v_)
        return S_new, o_inter + o_intra

    S0 = jnp.zeros((B, HGRN_HEADS, HEAD_K, HEAD_V), jnp.float32)
    _, o = lax.scan(chunk_step, S0, (qc, kc, vc, gc))
    o = o.transpose(1, 0, 3, 2, 4).reshape(B, T, HGRN_HEADS, HEAD_V)
    o = o * lax.rsqrt(jnp.mean(jnp.square(o), axis=-1, keepdims=True) + RMS_EPS)
    o = o * g_norm_w.astype(jnp.float32)
    o = o.reshape(B, T, D_HGRN) * jax.nn.silu(g_raw.astype(jnp.float32))
    return o.astype(q_raw.dtype)


def conformer_conv_mixer(a, b, w_dw, b_dw, ln_g, ln_b):
    u = a * jax.nn.sigmoid(b)
    y = lax.conv_general_dilated(
        u, w_dw[:, None, :], window_strides=(1,), padding=[(CONV_WIDTH - 1, 0)],
        dimension_numbers=('NWC', 'WIO', 'NWC'), feature_group_count=D_CONV)
    y = layer_norm(y + b_dw, ln_g, ln_b)
    return jax.nn.silu(y)


def _fwd_setup_inputs(seed: int = 0) -> dict:
    key = jax.random.key(seed)
    ks = jax.random.split(key, 24)
    f32 = jnp.float32

    def nrm(k, shape, scale):
        return jax.random.normal(k, shape, f32) * scale

    return {
        "x": nrm(ks[0], (BATCH, SEQ, D_MODEL), 1.0),
        "ln0_g": 1.0 + nrm(ks[1], (D_MODEL,), 0.02),
        "ln0_b": nrm(ks[2], (D_MODEL,), 0.02),
        "w_in": nrm(ks[3], (DEPTH, D_MODEL, D_IN), D_MODEL ** -0.5),
        "b_in": nrm(ks[4], (DEPTH, D_IN), 0.02),
        "lb_logits": nrm(ks[5], (DEPTH, D_HGRN), 0.1),
        "g_norm_w": 1.0 + nrm(ks[6], (DEPTH, HEAD_V), 0.02),
        "w_a": nrm(ks[7], (DEPTH, D_HGRN, D_MODEL), BETA * D_HGRN ** -0.5),
        "w_dw": nrm(ks[8], (DEPTH, CONV_WIDTH, D_CONV), CONV_WIDTH ** -0.5),
        "b_dw": nrm(ks[9], (DEPTH, D_CONV), 0.02),
        "conv_ln_g": 1.0 + nrm(ks[10], (DEPTH, D_CONV), 0.02),
        "conv_ln_b": nrm(ks[11], (DEPTH, D_CONV), 0.02),
        "w_b": nrm(ks[12], (DEPTH, D_CONV, D_MODEL), BETA * D_CONV ** -0.5),
        "b_b": nrm(ks[13], (DEPTH, D_MODEL), 0.02),
        "w_o": nrm(ks[14], (DEPTH, D_MODEL, D_MODEL), BETA * D_MODEL ** -0.5),
        "ln1_g": 1.0 + nrm(ks[15], (DEPTH, D_MODEL), 0.02),
        "ln1_b": nrm(ks[16], (DEPTH, D_MODEL), 0.02),
        "w_up": nrm(ks[17], (DEPTH, D_MODEL, 2 * FFN_HIDDEN), D_MODEL ** -0.5),
        "w_down": nrm(ks[18], (DEPTH, FFN_HIDDEN, D_MODEL), BETA * FFN_HIDDEN ** -0.5),
        "ln2_g": 1.0 + nrm(ks[19], (DEPTH, D_MODEL), 0.02),
        "ln2_b": nrm(ks[20], (DEPTH, D_MODEL), 0.02),
    }


def _fwd_reference(x, ln0_g, ln0_b, w_in, b_in, lb_logits, g_norm_w, w_a, w_dw, b_dw,
              conv_ln_g, conv_ln_b, w_b, b_b, w_o, ln1_g, ln1_b, w_up, w_down,
              ln2_g, ln2_b):
    split_idx = list(np.cumsum(IN_SPLITS)[:-1])
    p = jax.nn.softmax(lb_logits.astype(jnp.float32), axis=0)
    lower_bounds = jnp.cumsum(p, axis=0) - p[0:1]

    x = layer_norm(x, ln0_g, ln0_b)
    for l in range(DEPTH):
        h = jnp.einsum('btd,de->bte', x, w_in[l]) + b_in[l]
        q_r, f_r, i_r, g_r, glu_a, glu_b, gate_h, gate_c = jnp.split(h, split_idx, axis=-1)
        y_h = hgrn2_mixer(q_r, f_r, i_r, g_r, lower_bounds[l], g_norm_w[l])
        y_h = jnp.einsum('bte,ed->btd', y_h, w_a[l])
        y_c = conformer_conv_mixer(glu_a, glu_b, w_dw[l], b_dw[l], conv_ln_g[l], conv_ln_b[l])
        y_c = jnp.einsum('bte,ed->btd', y_c, w_b[l]) + b_b[l]
        merged = jax.nn.sigmoid(gate_h) * y_h + jax.nn.sigmoid(gate_c) * y_c
        mix_out = jnp.einsum('btd,de->bte', merged, w_o[l])
        x = layer_norm(ALPHA * x + mix_out, ln1_g[l], ln1_b[l])
        up = jnp.einsum('btd,df->btf', x, w_up[l])
        u_gate, u_val = jnp.split(up, 2, axis=-1)
        ffn_out = jnp.einsum('btf,fd->btd', jax.nn.silu(u_gate) * u_val, w_down[l])
        x = layer_norm(ALPHA * x + ffn_out, ln2_g[l], ln2_b[l])
    return x


import jax as _jax
import jax.numpy as _jnp

TWIN_FORMAT = 'train_step'
FWD_PARAMS = ['x', 'ln0_g', 'ln0_b', 'w_in', 'b_in', 'lb_logits', 'g_norm_w', 'w_a', 'w_dw', 'b_dw', 'conv_ln_g', 'conv_ln_b', 'w_b', 'b_b', 'w_o', 'ln1_g', 'ln1_b', 'w_up', 'w_down', 'ln2_g', 'ln2_b']
TWIN_WEIGHTS = ['ln0_g', 'ln0_b', 'w_in', 'b_in', 'lb_logits', 'g_norm_w', 'w_a', 'w_dw', 'b_dw', 'conv_ln_g', 'conv_ln_b', 'w_b', 'b_b', 'w_o', 'ln1_g', 'ln1_b', 'w_up', 'w_down', 'ln2_g', 'ln2_b']
TWIN_DIFF_INPUT = 'x'
TWIN_INPUTS = ['x', 'ln0_g', 'ln0_b', 'w_in', 'b_in', 'lb_logits', 'g_norm_w', 'w_a', 'w_dw', 'b_dw', 'conv_ln_g', 'conv_ln_b', 'w_b', 'b_b', 'w_o', 'ln1_g', 'ln1_b', 'w_up', 'w_down', 'ln2_g', 'ln2_b', 'loss_target', 'm_ln0_g', 'm_ln0_b', 'm_w_in', 'm_b_in', 'm_lb_logits', 'm_g_norm_w', 'm_w_a', 'm_w_dw', 'm_b_dw', 'm_conv_ln_g', 'm_conv_ln_b', 'm_w_b', 'm_b_b', 'm_w_o', 'm_ln1_g', 'm_ln1_b', 'm_w_up', 'm_w_down', 'm_ln2_g', 'm_ln2_b', 'v_ln0_g', 'v_ln0_b', 'v_w_in', 'v_b_in', 'v_lb_logits', 'v_g_norm_w', 'v_w_a', 'v_w_dw', 'v_b_dw', 'v_conv_ln_g', 'v_conv_ln_b', 'v_w_b', 'v_b_b', 'v_w_o', 'v_ln1_g', 'v_ln1_b', 'v_w_up', 'v_w_down', 'v_ln2_g', 'v_ln2_b']
TWIN_OUTPUTS = ['loss', 'grad_x', 'grad_ln0_g', 'grad_ln0_b', 'grad_w_in', 'grad_b_in', 'grad_lb_logits', 'grad_g_norm_w', 'grad_w_a', 'grad_w_dw', 'grad_b_dw', 'grad_conv_ln_g', 'grad_conv_ln_b', 'grad_w_b', 'grad_b_b', 'grad_w_o', 'grad_ln1_g', 'grad_ln1_b', 'grad_w_up', 'grad_w_down', 'grad_ln2_g', 'grad_ln2_b', 'delta_ln0_g', 'delta_ln0_b', 'delta_w_in', 'delta_b_in', 'delta_lb_logits', 'delta_g_norm_w', 'delta_w_a', 'delta_w_dw', 'delta_b_dw', 'delta_conv_ln_g', 'delta_conv_ln_b', 'delta_w_b', 'delta_b_b', 'delta_w_o', 'delta_ln1_g', 'delta_ln1_b', 'delta_w_up', 'delta_w_down', 'delta_ln2_g', 'delta_ln2_b', 'new_m_ln0_g', 'new_m_ln0_b', 'new_m_w_in', 'new_m_b_in', 'new_m_lb_logits', 'new_m_g_norm_w', 'new_m_w_a', 'new_m_w_dw', 'new_m_b_dw', 'new_m_conv_ln_g', 'new_m_conv_ln_b', 'new_m_w_b', 'new_m_b_b', 'new_m_w_o', 'new_m_ln1_g', 'new_m_ln1_b', 'new_m_w_up', 'new_m_w_down', 'new_m_ln2_g', 'new_m_ln2_b', 'new_v_ln0_g', 'new_v_ln0_b', 'new_v_w_in', 'new_v_b_in', 'new_v_lb_logits', 'new_v_g_norm_w', 'new_v_w_a', 'new_v_w_dw', 'new_v_b_dw', 'new_v_conv_ln_g', 'new_v_conv_ln_b', 'new_v_w_b', 'new_v_b_b', 'new_v_w_o', 'new_v_ln1_g', 'new_v_ln1_b', 'new_v_w_up', 'new_v_w_down', 'new_v_ln2_g', 'new_v_ln2_b']
TWIN_LEAF_KINDS = {'loss': 'loss', 'grad_x': 'grad_x', 'grad_ln0_g': 'grad_w', 'grad_ln0_b': 'grad_w', 'grad_w_in': 'grad_w', 'grad_b_in': 'grad_w', 'grad_lb_logits': 'grad_w', 'grad_g_norm_w': 'grad_w', 'grad_w_a': 'grad_w', 'grad_w_dw': 'grad_w', 'grad_b_dw': 'grad_w', 'grad_conv_ln_g': 'grad_w', 'grad_conv_ln_b': 'grad_w', 'grad_w_b': 'grad_w', 'grad_b_b': 'grad_w', 'grad_w_o': 'grad_w', 'grad_ln1_g': 'grad_w', 'grad_ln1_b': 'grad_w', 'grad_w_up': 'grad_w', 'grad_w_down': 'grad_w', 'grad_ln2_g': 'grad_w', 'grad_ln2_b': 'grad_w', 'delta_ln0_g': 'delta_w', 'delta_ln0_b': 'delta_w', 'delta_w_in': 'delta_w', 'delta_b_in': 'delta_w', 'delta_lb_logits': 'delta_w', 'delta_g_norm_w': 'delta_w', 'delta_w_a': 'delta_w', 'delta_w_dw': 'delta_w', 'delta_b_dw': 'delta_w', 'delta_conv_ln_g': 'delta_w', 'delta_conv_ln_b': 'delta_w', 'delta_w_b': 'delta_w', 'delta_b_b': 'delta_w', 'delta_w_o': 'delta_w', 'delta_ln1_g': 'delta_w', 'delta_ln1_b': 'delta_w', 'delta_w_up': 'delta_w', 'delta_w_down': 'delta_w', 'delta_ln2_g': 'delta_w', 'delta_ln2_b': 'delta_w', 'new_m_ln0_g': 'new_m', 'new_m_ln0_b': 'new_m', 'new_m_w_in': 'new_m', 'new_m_b_in': 'new_m', 'new_m_lb_logits': 'new_m', 'new_m_g_norm_w': 'new_m', 'new_m_w_a': 'new_m', 'new_m_w_dw': 'new_m', 'new_m_b_dw': 'new_m', 'new_m_conv_ln_g': 'new_m', 'new_m_conv_ln_b': 'new_m', 'new_m_w_b': 'new_m', 'new_m_b_b': 'new_m', 'new_m_w_o': 'new_m', 'new_m_ln1_g': 'new_m', 'new_m_ln1_b': 'new_m', 'new_m_w_up': 'new_m', 'new_m_w_down': 'new_m', 'new_m_ln2_g': 'new_m', 'new_m_ln2_b': 'new_m', 'new_v_ln0_g': 'new_v', 'new_v_ln0_b': 'new_v', 'new_v_w_in': 'new_v', 'new_v_b_in': 'new_v', 'new_v_lb_logits': 'new_v', 'new_v_g_norm_w': 'new_v', 'new_v_w_a': 'new_v', 'new_v_w_dw': 'new_v', 'new_v_b_dw': 'new_v', 'new_v_conv_ln_g': 'new_v', 'new_v_conv_ln_b': 'new_v', 'new_v_w_b': 'new_v', 'new_v_b_b': 'new_v', 'new_v_w_o': 'new_v', 'new_v_ln1_g': 'new_v', 'new_v_ln1_b': 'new_v', 'new_v_w_up': 'new_v', 'new_v_w_down': 'new_v', 'new_v_ln2_g': 'new_v', 'new_v_ln2_b': 'new_v'}


def _forward(args):
    return _fwd_reference(*[args[k] for k in FWD_PARAMS])


def _output_shape():
    out = _jax.eval_shape(lambda: _forward(_fwd_setup_inputs(0)))
    return out.shape, out.dtype

N_MICROBATCH = 1
ADAM_LR = 0.001
ADAM_B1 = 0.9
ADAM_B2 = 0.999
ADAM_EPS = 1e-08
ADAM_WD = 0.01
ADAM_STEP = 10
PER_EXAMPLE_BATCH_AXIS = {'x': 0, 'loss_target': 0}
SHARED_INPUTS = []
_WEIGHT_DTYPES = {'ln0_g': _jnp.float32, 'ln0_b': _jnp.float32, 'w_in': _jnp.float32, 'b_in': _jnp.float32, 'lb_logits': _jnp.float32, 'g_norm_w': _jnp.float32, 'w_a': _jnp.float32, 'w_dw': _jnp.float32, 'b_dw': _jnp.float32, 'conv_ln_g': _jnp.float32, 'conv_ln_b': _jnp.float32, 'w_b': _jnp.float32, 'b_b': _jnp.float32, 'w_o': _jnp.float32, 'ln1_g': _jnp.float32, 'ln1_b': _jnp.float32, 'w_up': _jnp.float32, 'w_down': _jnp.float32, 'ln2_g': _jnp.float32, 'ln2_b': _jnp.float32}
MOMENT_SCALE = {'ln0_g': 5.268143e-01, 'ln0_b': 2.719165e-01, 'w_in': 2.962406e-03, 'b_in': 5.706404e-03, 'lb_logits': 4.113874e-04, 'g_norm_w': 1.221047e-02, 'w_a': 1.043906e-02, 'w_dw': 4.617099e-03, 'b_dw': 1.310227e-02, 'conv_ln_g': 6.422750e-03, 'conv_ln_b': 7.715013e-03, 'w_b': 1.190525e-02, 'b_b': 3.506780e-02, 'w_o': 1.592341e-02, 'ln1_g': 5.593467e-01, 'ln1_b': 2.761033e-01, 'w_up': 1.153696e-02, 'w_down': 4.477818e-02, 'ln2_g': 8.059879e+00, 'ln2_b': 5.146062e-01}


def _to_microbatches(a, axis):
    t = _jnp.moveaxis(a, axis, 0)
    t = t.reshape((N_MICROBATCH, t.shape[0] // N_MICROBATCH) + t.shape[1:])
    return _jnp.moveaxis(t, 1, axis + 1)


def setup_inputs(seed: int = 0) -> dict:
    inp = _fwd_setup_inputs(seed)
    key = _jax.random.fold_in(_jax.random.key(seed), 7919)
    shape, _ = _output_shape()
    out = dict(inp)
    out["loss_target"] = _jax.random.normal(_jax.random.fold_in(key, 0), shape, _jnp.float32)
    for i, name in enumerate(TWIN_WEIGHTS):
        w = inp[name].astype(_jnp.float32)
        if MOMENT_SCALE is None:
            s = _jnp.sqrt(_jnp.mean(_jnp.square(w)) + 1e-30)
        else:
            s = MOMENT_SCALE[name]
        km, kv = _jax.random.split(_jax.random.fold_in(key, i + 1))
        out[name] = w
        out["m_" + name] = s * _jax.random.normal(km, w.shape, _jnp.float32)
        out["v_" + name] = (s * s) * _jax.random.uniform(kv, w.shape, _jnp.float32, 0.5, 1.5)
    if N_MICROBATCH > 1:
        for name, axis in PER_EXAMPLE_BATCH_AXIS.items():
            out[name] = _to_microbatches(out[name], axis)
    return {'x': out['x'], 'ln0_g': out['ln0_g'], 'ln0_b': out['ln0_b'], 'w_in': out['w_in'], 'b_in': out['b_in'], 'lb_logits': out['lb_logits'], 'g_norm_w': out['g_norm_w'], 'w_a': out['w_a'], 'w_dw': out['w_dw'], 'b_dw': out['b_dw'], 'conv_ln_g': out['conv_ln_g'], 'conv_ln_b': out['conv_ln_b'], 'w_b': out['w_b'], 'b_b': out['b_b'], 'w_o': out['w_o'], 'ln1_g': out['ln1_g'], 'ln1_b': out['ln1_b'], 'w_up': out['w_up'], 'w_down': out['w_down'], 'ln2_g': out['ln2_g'], 'ln2_b': out['ln2_b'], 'loss_target': out['loss_target'], 'm_ln0_g': out['m_ln0_g'], 'm_ln0_b': out['m_ln0_b'], 'm_w_in': out['m_w_in'], 'm_b_in': out['m_b_in'], 'm_lb_logits': out['m_lb_logits'], 'm_g_norm_w': out['m_g_norm_w'], 'm_w_a': out['m_w_a'], 'm_w_dw': out['m_w_dw'], 'm_b_dw': out['m_b_dw'], 'm_conv_ln_g': out['m_conv_ln_g'], 'm_conv_ln_b': out['m_conv_ln_b'], 'm_w_b': out['m_w_b'], 'm_b_b': out['m_b_b'], 'm_w_o': out['m_w_o'], 'm_ln1_g': out['m_ln1_g'], 'm_ln1_b': out['m_ln1_b'], 'm_w_up': out['m_w_up'], 'm_w_down': out['m_w_down'], 'm_ln2_g': out['m_ln2_g'], 'm_ln2_b': out['m_ln2_b'], 'v_ln0_g': out['v_ln0_g'], 'v_ln0_b': out['v_ln0_b'], 'v_w_in': out['v_w_in'], 'v_b_in': out['v_b_in'], 'v_lb_logits': out['v_lb_logits'], 'v_g_norm_w': out['v_g_norm_w'], 'v_w_a': out['v_w_a'], 'v_w_dw': out['v_w_dw'], 'v_b_dw': out['v_b_dw'], 'v_conv_ln_g': out['v_conv_ln_g'], 'v_conv_ln_b': out['v_conv_ln_b'], 'v_w_b': out['v_w_b'], 'v_b_b': out['v_b_b'], 'v_w_o': out['v_w_o'], 'v_ln1_g': out['v_ln1_g'], 'v_ln1_b': out['v_ln1_b'], 'v_w_up': out['v_w_up'], 'v_w_down': out['v_w_down'], 'v_ln2_g': out['v_ln2_g'], 'v_ln2_b': out['v_ln2_b']}


def _loss(weights, diff, rest, loss_target):
    with _jax.named_scope("forward"):
        args = {**rest, TWIN_DIFF_INPUT: diff, **{k: w.astype(_WEIGHT_DTYPES[k]) for k, w in weights.items()}}
        y = _forward(args)
    with _jax.named_scope("loss_head"):
        err = _jnp.square(y.astype(_jnp.float32) - loss_target)
        return 0.5 * _jnp.sum(_jnp.mean(err, axis=-1)) if err.ndim else 0.5 * err


def _adamw(w, g, m, v):
    m = ADAM_B1 * m + (1.0 - ADAM_B1) * g
    v = ADAM_B2 * v + (1.0 - ADAM_B2) * _jnp.square(g)
    m_hat = m / (1.0 - ADAM_B1 ** ADAM_STEP)
    v_hat = v / (1.0 - ADAM_B2 ** ADAM_STEP)
    delta = -ADAM_LR * (m_hat / (_jnp.sqrt(v_hat) + ADAM_EPS) + ADAM_WD * w)
    return delta, m, v


def reference(x, ln0_g, ln0_b, w_in, b_in, lb_logits, g_norm_w, w_a, w_dw, b_dw, conv_ln_g, conv_ln_b, w_b, b_b, w_o, ln1_g, ln1_b, w_up, w_down, ln2_g, ln2_b, loss_target, m_ln0_g, m_ln0_b, m_w_in, m_b_in, m_lb_logits, m_g_norm_w, m_w_a, m_w_dw, m_b_dw, m_conv_ln_g, m_conv_ln_b, m_w_b, m_b_b, m_w_o, m_ln1_g, m_ln1_b, m_w_up, m_w_down, m_ln2_g, m_ln2_b, v_ln0_g, v_ln0_b, v_w_in, v_b_in, v_lb_logits, v_g_norm_w, v_w_a, v_w_dw, v_b_dw, v_conv_ln_g, v_conv_ln_b, v_w_b, v_b_b, v_w_o, v_ln1_g, v_ln1_b, v_w_up, v_w_down, v_ln2_g, v_ln2_b):
    given = dict(x=x, ln0_g=ln0_g, ln0_b=ln0_b, w_in=w_in, b_in=b_in, lb_logits=lb_logits, g_norm_w=g_norm_w, w_a=w_a, w_dw=w_dw, b_dw=b_dw, conv_ln_g=conv_ln_g, conv_ln_b=conv_ln_b, w_b=w_b, b_b=b_b, w_o=w_o, ln1_g=ln1_g, ln1_b=ln1_b, w_up=w_up, w_down=w_down, ln2_g=ln2_g, ln2_b=ln2_b, loss_target=loss_target, m_ln0_g=m_ln0_g, m_ln0_b=m_ln0_b, m_w_in=m_w_in, m_b_in=m_b_in, m_lb_logits=m_lb_logits, m_g_norm_w=m_g_norm_w, m_w_a=m_w_a, m_w_dw=m_w_dw, m_b_dw=m_b_dw, m_conv_ln_g=m_conv_ln_g, m_conv_ln_b=m_conv_ln_b, m_w_b=m_w_b, m_b_b=m_b_b, m_w_o=m_w_o, m_ln1_g=m_ln1_g, m_ln1_b=m_ln1_b, m_w_up=m_w_up, m_w_down=m_w_down, m_ln2_g=m_ln2_g, m_ln2_b=m_ln2_b, v_ln0_g=v_ln0_g, v_ln0_b=v_ln0_b, v_w_in=v_w_in, v_b_in=v_b_in, v_lb_logits=v_lb_logits, v_g_norm_w=v_g_norm_w, v_w_a=v_w_a, v_w_dw=v_w_dw, v_b_dw=v_b_dw, v_conv_ln_g=v_conv_ln_g, v_conv_ln_b=v_conv_ln_b, v_w_b=v_w_b, v_b_b=v_b_b, v_w_o=v_w_o, v_ln1_g=v_ln1_g, v_ln1_b=v_ln1_b, v_w_up=v_w_up, v_w_down=v_w_down, v_ln2_g=v_ln2_g, v_ln2_b=v_ln2_b)
    weights = {n: given[n] for n in TWIN_WEIGHTS}
    shared = {n: given[n] for n in SHARED_INPUTS}
    per_example = {n: given[n] for n in ['x']}
    grad_fn = _jax.value_and_grad(_loss, argnums=(0, 1))

    def one_microbatch(ex, loss_target):
        ex = dict(ex)
        diff = ex.pop(TWIN_DIFF_INPUT)
        return grad_fn(weights, diff, {**shared, **ex}, loss_target)

    if N_MICROBATCH == 1:
        loss, (grad_w, grad_x) = one_microbatch(per_example, given["loss_target"])
    else:
        def body(carry, xs):
            loss_sum, grad_sum = carry
            l_k, (gw_k, gx_k) = one_microbatch(xs[0], xs[1])
            with _jax.named_scope("update"):
                return (loss_sum + l_k, _jax.tree.map(_jnp.add, grad_sum, gw_k)), gx_k

        init = (_jnp.zeros((), _jnp.float32), _jax.tree.map(_jnp.zeros_like, weights))
        (loss, grad_w), grad_x = _jax.lax.scan(body, init, (per_example, given["loss_target"]))
    with _jax.named_scope("update"):
        delta_w, new_m, new_v = {}, {}, {}
        for n in TWIN_WEIGHTS:
            delta_w[n], new_m[n], new_v[n] = _adamw(weights[n], grad_w[n], given["m_" + n], given["v_" + n])
    return (loss, grad_x, *[grad_w[n] for n in TWIN_WEIGHTS], *[delta_w[n] for n in TWIN_WEIGHTS],
            *[new_m[n] for n in TWIN_WEIGHTS], *[new_v[n] for n in TWIN_WEIGHTS])
```

```python
import functools

import jax
import jax.numpy as jnp
from jax import lax
from jax.experimental import pallas as pl
from jax.experimental.pallas import tpu as pltpu

F32 = jnp.float32
MXU_DTYPE = jnp.bfloat16
ACT_DTYPE = jnp.bfloat16

LANES = 128
N_DEV = 8
N_SEC = 8
CONV_WIDTH = 31
CONV_HALO = 32
HG_C = 16
LN_EPS = 1e-5
RMS_EPS = 1e-6
F_MIN = 1e-30
ADAM_LR = 0.001
ADAM_B1 = 0.9
ADAM_B2 = 0.999
ADAM_EPS = 1e-08
ADAM_WD = 0.01
ADAM_STEP = 10
VMEM_LIMIT = 56 * 1024 * 1024
MESH = pl.DeviceIdType.MESH

_NN = (((1,), (0,)), ((), ()))
_NT = (((1,), (1,)), ((), ()))
_TN = (((0,), (0,)), ((), ()))


def _cp(*sem):
    return pltpu.CompilerParams(dimension_semantics=tuple(sem), vmem_limit_bytes=VMEM_LIMIT)


def _pick(n, cands):
    for c in cands:
        if c <= n and n % c == 0:
            return c
    return n


def _silu(x):
    return x * jax.nn.sigmoid(x)


def _dsilu(x):
    s = jax.nn.sigmoid(x)
    return s * (1.0 + x * (1.0 - s))


def _matmul(name, a, b, *, dims, grid, a_spec, b_spec, out_shape, out_spec, acc_shape, nk,
            bias=None, bias_spec=None, add=None, add_spec=None, add_scale=1.0):
    has_bias, has_add = bias is not None, add is not None
    kaxis = len(grid) - 1

    def body(*refs):
        a_ref, b_ref = refs[0], refs[1]
        pos = 2
        bias_ref = add_ref = None
        if has_bias:
            bias_ref = refs[pos]
            pos += 1
        if has_add:
            add_ref = refs[pos]
            pos += 1
        o_ref = refs[pos]
        acc_ref = refs[pos + 1] if nk > 1 else None

        part = lax.dot_general(a_ref[...].astype(MXU_DTYPE), b_ref[...].astype(MXU_DTYPE), dims,
                               preferred_element_type=F32)

        def finish(r):
            if has_bias:
                r = r + bias_ref[...]
            if has_add:
                r = r + add_scale * add_ref[...]
            o_ref[...] = r.astype(o_ref.dtype)

        if nk == 1:
            finish(part)
        else:
            k = pl.program_id(kaxis)

            @pl.when(k == 0)
            def _():
                acc_ref[...] = part

            @pl.when(k > 0)
            def _():
                acc_ref[...] += part

            @pl.when(k == nk - 1)
            def _():
                finish(acc_ref[...])

    ins, specs = [a, b], [a_spec, b_spec]
    if has_bias:
        ins.append(bias)
        specs.append(bias_spec)
    if has_add:
        ins.append(add)
        specs.append(add_spec)
    sem = ("parallel",) * (len(grid) - 1) + ("arbitrary",) if nk > 1 else ("parallel",) * len(grid)
    return pl.pallas_call(
        body, name=name, grid=grid, in_specs=specs, out_specs=out_spec, out_shape=out_shape,
        scratch_shapes=[pltpu.VMEM(acc_shape, F32)] if nk > 1 else [],
        compiler_params=_cp(*sem))(*ins)


def _mm_nn(name, a, b, out_dtype, bias=None):
    M, K = a.shape
    N = b.shape[1]
    tn = _pick(N, (512, 256, 128))
    tk = K if K <= 1024 else _pick(K, (1408, 1024, 512, 256, 128))
    nk = K // tk
    return _matmul(
        name, a, b, dims=_NN, grid=(N // tn, nk),
        a_spec=pl.BlockSpec((M, tk), lambda j, k: (0, k)),
        b_spec=pl.BlockSpec((tk, tn), lambda j, k: (k, j)),
        out_shape=jax.ShapeDtypeStruct((M, N), out_dtype),
        out_spec=pl.BlockSpec((M, tn), lambda j, k: (0, j)),
        acc_shape=(M, tn), nk=nk,
        bias=bias, bias_spec=None if bias is None else pl.BlockSpec((1, tn), lambda j, k: (0, j)))


def _mm_nt(name, a, b, out_dtype, add=None, add_scale=1.0):
    M, K = a.shape
    N = b.shape[0]
    tn = _pick(N, (512, 256, 128))
    tk = K if K <= 1024 else _pick(K, (1408, 1024, 512, 256, 128))
    nk = K // tk
    return _matmul(
        name, a, b, dims=_NT, grid=(N // tn, nk),
        a_spec=pl.BlockSpec((M, tk), lambda j, k: (0, k)),
        b_spec=pl.BlockSpec((tn, tk), lambda j, k: (j, k)),
        out_shape=jax.ShapeDtypeStruct((M, N), out_dtype),
        out_spec=pl.BlockSpec((M, tn), lambda j, k: (0, j)),
        acc_shape=(M, tn), nk=nk,
        add=add, add_spec=None if add is None else pl.BlockSpec((M, tn), lambda j, k: (0, j)),
        add_scale=add_scale)


def _mm_tn(name, a, b, out_dtype):
    K, M = a.shape
    N = b.shape[1]
    tm = _pick(M, (256, 128))
    return _matmul(
        name, a, b, dims=_TN, grid=(M // tm,),
        a_spec=pl.BlockSpec((K, tm), lambda i: (0, i)),
        b_spec=pl.BlockSpec((K, N), lambda i: (0, 0)),
        out_shape=jax.ShapeDtypeStruct((M, N), out_dtype),
        out_spec=pl.BlockSpec((tm, N), lambda i: (i, 0)),
        acc_shape=(tm, N), nk=1)


def _proj_in(x_bf, w_in, b_in):
    T, D = x_bf.shape
    tn = _pick(D, (512, 256, 128))
    return _matmul(
        "proj_in", x_bf, w_in, dims=_NN, grid=(N_SEC, D // tn),
        a_spec=pl.BlockSpec((T, D), lambda s, j: (0, 0)),
        b_spec=pl.BlockSpec((None, D, tn), lambda s, j: (s, 0, j)),
        out_shape=jax.ShapeDtypeStruct((N_SEC, T, D), F32),
        out_spec=pl.BlockSpec((None, T, tn), lambda s, j: (s, 0, j)),
        acc_shape=(T, tn), nk=1,
        bias=b_in, bias_spec=pl.BlockSpec((None, 1, tn), lambda s, j: (s, 0, j)))


def _proj_in_dx(dh, w_in, add, add_scale):
    _, T, D = dh.shape
    tn = _pick(D, (512, 256, 128))
    return _matmul(
        "proj_in_dx", dh, w_in, dims=_NT, grid=(D // tn, N_SEC),
        a_spec=pl.BlockSpec((None, T, D), lambda j, s: (s, 0, 0)),
        b_spec=pl.BlockSpec((None, tn, D), lambda j, s: (s, j, 0)),
        out_shape=jax.ShapeDtypeStruct((T, D), F32),
        out_spec=pl.BlockSpec((T, tn), lambda j, s: (0, j)),
        acc_shape=(T, tn), nk=N_SEC,
        add=add, add_spec=pl.BlockSpec((T, tn), lambda j, s: (0, j)), add_scale=add_scale)


def _proj_in_dw(x_bf, dh):
    _, T, D = dh.shape
    tn = _pick(D, (512, 256, 128))
    return _matmul(
        "proj_in_dw", x_bf, dh, dims=_TN, grid=(N_SEC, D // tn),
        a_spec=pl.BlockSpec((T, D), lambda s, j: (0, 0)),
        b_spec=pl.BlockSpec((None, T, tn), lambda s, j: (s, 0, j)),
        out_shape=jax.ShapeDtypeStruct((N_SEC, D, D), ACT_DTYPE),
        out_spec=pl.BlockSpec((None, D, tn), lambda s, j: (s, 0, j)),
        acc_shape=(D, tn), nk=1)


def _ffn_up(x_bf, w_up):
    T, D = x_bf.shape
    F = w_up.shape[1] // 2
    tn = _pick(F, (256, 128))
    nb = F // tn
    return _matmul(
        "ffn_up", x_bf, w_up, dims=_NN, grid=(2, nb),
        a_spec=pl.BlockSpec((T, D), lambda p, j: (0, 0)),
        b_spec=pl.BlockSpec((D, tn), lambda p, j: (0, p * nb + j)),
        out_shape=jax.ShapeDtypeStruct((2, T, F), F32),
        out_spec=pl.BlockSpec((None, T, tn), lambda p, j: (p, 0, j)),
        acc_shape=(T, tn), nk=1)


def _ffn_up_dx(dup, w_up, add, add_scale):
    _, T, F = dup.shape
    D = w_up.shape[0]
    tn = _pick(D, (512, 256, 128))
    tk = _pick(F, (256, 128))
    nb = F // tk
    return _matmul(
        "ffn_up_dx", dup, w_up, dims=_NT, grid=(D // tn, 2 * nb),
        a_spec=pl.BlockSpec((None, T, tk), lambda j, k: (k // nb, 0, k % nb)),
        b_spec=pl.BlockSpec((tn, tk), lambda j, k: (j, k)),
        out_shape=jax.ShapeDtypeStruct((T, D), F32),
        out_spec=pl.BlockSpec((T, tn), lambda j, k: (0, j)),
        acc_shape=(T, tn), nk=2 * nb,
        add=add, add_spec=pl.BlockSpec((T, tn), lambda j, k: (0, j)), add_scale=add_scale)


def _ffn_up_dw(x_bf, dup):
    _, T, F = dup.shape
    D = x_bf.shape[1]
    tn = _pick(F, (256, 128))
    nb = F // tn
    return _matmul(
        "ffn_up_dw", x_bf, dup, dims=_TN, grid=(2, nb),
        a_spec=pl.BlockSpec((T, D), lambda p, j: (0, 0)),
        b_spec=pl.BlockSpec((None, T, tn), lambda p, j: (p, 0, j)),
        out_shape=jax.ShapeDtypeStruct((D, 2 * F), ACT_DTYPE),
        out_spec=pl.BlockSpec((D, tn), lambda p, j: (0, p * nb + j)),
        acc_shape=(D, tn), nk=1)


def _ln_fwd(name, a, res, alpha, g, b):
    T, D = a.shape
    tr = _pick(T, (256, 128, 64, 32, 16))
    has_res = res is not None

    def body(*refs):
        if has_res:
            a_ref, r_ref, g_ref, b_ref, y_ref, yb_ref, z_ref = refs
            z = alpha * a_ref[...] + r_ref[...]
            z_ref[...] = z
        else:
            a_ref, g_ref, b_ref, y_ref, yb_ref = refs
            z = a_ref[...]
        mu = jnp.mean(z, axis=-1, keepdims=True)
        zc = z - mu
        var = jnp.mean(zc * zc, axis=-1, keepdims=True)
        y = zc * lax.rsqrt(var + LN_EPS) * g_ref[...] + b_ref[...]
        y_ref[...] = y
        yb_ref[...] = y.astype(ACT_DTYPE)

    row = pl.BlockSpec((tr, D), lambda i: (i, 0))
    vec = pl.BlockSpec((1, D), lambda i: (0, 0))
    ins = [a] + ([res] if has_res else []) + [g.reshape(1, D), b.reshape(1, D)]
    in_specs = [row] + ([row] if has_res else []) + [vec, vec]
    out_shape = [jax.ShapeDtypeStruct((T, D), F32), jax.ShapeDtypeStruct((T, D), ACT_DTYPE)]
    if has_res:
        out_shape.append(jax.ShapeDtypeStruct((T, D), F32))
    return pl.pallas_call(
        body, name=name, grid=(T // tr,), in_specs=in_specs,
        out_specs=[row] * len(out_shape), out_shape=out_shape, compiler_params=_cp("parallel"))(*ins)


def _ln_bwd(name, z, dy, g):
    T, D = z.shape
    tr = _pick(T, (256, 128, 64, 32, 16))

    def body(z_ref, dy_ref, g_ref, dz_ref, dzb_ref, dg_ref, db_ref):
        @pl.when(pl.program_id(0) == 0)
        def _():
            dg_ref[...] = jnp.zeros_like(dg_ref)
            db_ref[...] = jnp.zeros_like(db_ref)

        zv = z_ref[...]
        dy_ = dy_ref[...]
        mu = jnp.mean(zv, axis=-1, keepdims=True)
        zc = zv - mu
        rstd = lax.rsqrt(jnp.mean(zc * zc, axis=-1, keepdims=True) + LN_EPS)
        xhat = zc * rstd
        dxh = dy_ * g_ref[...]
        dz = rstd * (dxh - jnp.mean(dxh, axis=-1, keepdims=True)
                     - xhat * jnp.mean(dxh * xhat, axis=-1, keepdims=True))
        dz_ref[...] = dz
        dzb_ref[...] = dz.astype(ACT_DTYPE)
        dg_ref[...] += jnp.sum(dy_ * xhat, axis=0, keepdims=True)
        db_ref[...] += jnp.sum(dy_, axis=0, keepdims=True)

    row = pl.BlockSpec((tr, D), lambda i: (i, 0))
    vec = pl.BlockSpec((1, D), lambda i: (0, 0))
    return pl.pallas_call(
        body, name=name, grid=(T // tr,), in_specs=[row, row, vec], out_specs=[row, row, vec, vec],
        out_shape=[jax.ShapeDtypeStruct((T, D), F32), jax.ShapeDtypeStruct((T, D), ACT_DTYPE),
                   jax.ShapeDtypeStruct((1, D), F32), jax.ShapeDtypeStruct((1, D), F32)],
        compiler_params=_cp("arbitrary"))(z, dy, g.reshape(1, D))


def _loss_fwd_bwd(y, target):
    T, D = y.shape
    tr = _pick(T, (256, 128, 64, 32, 16))

    def body(y_ref, t_ref, dy_ref, l_ref):
        @pl.when(pl.program_id(0) == 0)
        def _():
            l_ref[...] = jnp.zeros_like(l_ref)

        e = y_ref[...] - t_ref[...]
        dy_ref[...] = e * (1.0 / D)
        row = jnp.sum(e * e, axis=-1, keepdims=True) * (1.0 / D)
        l_ref[...] += 0.5 * jnp.sum(row, axis=0, keepdims=True)

    rowspec = pl.BlockSpec((tr, D), lambda i: (i, 0))
    return pl.pallas_call(
        body, name="loss", grid=(T // tr,), in_specs=[rowspec, rowspec],
        out_specs=[rowspec, pl.BlockSpec((1, LANES), lambda i: (0, 0))],
        out_shape=[jax.ShapeDtypeStruct((T, D), F32), jax.ShapeDtypeStruct((1, LANES), F32)],
        compiler_params=_cp("arbitrary"))(y, target)


def _gate_fwd(y_h, y_c, h):
    T, D = y_h.shape
    tr = _pick(T, (256, 128, 64, 32, 16))

    def body(yh_ref, yc_ref, gh_ref, gc_ref, m_ref):
        m = jax.nn.sigmoid(gh_ref[...]) * yh_ref[...] + jax.nn.sigmoid(gc_ref[...]) * yc_ref[...]
        m_ref[...] = m.astype(ACT_DTYPE)

    row = pl.BlockSpec((tr, D), lambda i: (i, 0))
    return pl.pallas_call(
        body, name="gate_fwd", grid=(T // tr,),
        in_specs=[row, row, pl.BlockSpec((None, tr, D), lambda i: (6, i, 0)),
                  pl.BlockSpec((None, tr, D), lambda i: (7, i, 0))],
        out_specs=row, out_shape=jax.ShapeDtypeStruct((T, D), ACT_DTYPE),
        compiler_params=_cp("parallel"))(y_h, y_c, h, h)


def _gate_bwd(dm, y_h, y_c, h):
    T, D = y_h.shape
    tr = _pick(T, (256, 128, 64, 32, 16))

    def body(dm_ref, yh_ref, yc_ref, gh_ref, gc_ref, dyh_ref, dyc_ref, dbb_ref, dh_ref):
        @pl.when(pl.program_id(0) == 0)
        def _():
            dbb_ref[...] = jnp.zeros_like(dbb_ref)

        dm_ = dm_ref[...]
        sh = jax.nn.sigmoid(gh_ref[...])
        sc = jax.nn.sigmoid(gc_ref[...])
        dyc = dm_ * sc
        dyh_ref[...] = (dm_ * sh).astype(ACT_DTYPE)
        dyc_ref[...] = dyc.astype(ACT_DTYPE)
        dbb_ref[...] += jnp.sum(dyc, axis=0, keepdims=True)
        dh_ref[0] = (dm_ * yh_ref[...] * sh * (1.0 - sh)).astype(ACT_DTYPE)
        dh_ref[1] = (dm_ * yc_ref[...] * sc * (1.0 - sc)).astype(ACT_DTYPE)

    row = pl.BlockSpec((tr, D), lambda i: (i, 0))
    return pl.pallas_call(
        body, name="gate_bwd", grid=(T // tr,),
        in_specs=[row, row, row, pl.BlockSpec((None, tr, D), lambda i: (6, i, 0)),
                  pl.BlockSpec((None, tr, D), lambda i: (7, i, 0))],
        out_specs=[row, row, pl.BlockSpec((1, D), lambda i: (0, 0)),
                   pl.BlockSpec((2, tr, D), lambda i: (3, i, 0))],
        out_shape=[jax.ShapeDtypeStruct((T, D), ACT_DTYPE), jax.ShapeDtypeStruct((T, D), ACT_DTYPE),
                   jax.ShapeDtypeStruct((1, D), F32), jax.ShapeDtypeStruct((N_SEC, T, D), ACT_DTYPE)],
        compiler_params=_cp("arbitrary"))(dm, y_h, y_c, h, h)


def _swiglu_fwd(up):
    _, T, F = up.shape
    tr = _pick(T, (128, 64, 32, 16))

    def body(up_ref, act_ref):
        act_ref[...] = (_silu(up_ref[0]) * up_ref[1]).astype(ACT_DTYPE)

    return pl.pallas_call(
        body, name="swiglu_fwd", grid=(T // tr,),
        in_specs=[pl.BlockSpec((2, tr, F), lambda i: (0, i, 0))],
        out_specs=pl.BlockSpec((tr, F), lambda i: (i, 0)),
        out_shape=jax.ShapeDtypeStruct((T, F), ACT_DTYPE), compiler_params=_cp("parallel"))(up)


def _swiglu_bwd(dact, up):
    _, T, F = up.shape
    tr = _pick(T, (128, 64, 32, 16))

    def body(da_ref, up_ref, dup_ref):
        da = da_ref[...]
        ug = up_ref[0]
        dup_ref[0] = (da * up_ref[1] * _dsilu(ug)).astype(ACT_DTYPE)
        dup_ref[1] = (da * _silu(ug)).astype(ACT_DTYPE)

    blk = pl.BlockSpec((2, tr, F), lambda i: (0, i, 0))
    return pl.pallas_call(
        body, name="swiglu_bwd", grid=(T // tr,),
        in_specs=[pl.BlockSpec((tr, F), lambda i: (i, 0)), blk], out_specs=blk,
        out_shape=jax.ShapeDtypeStruct((2, T, F), ACT_DTYPE), compiler_params=_cp("parallel"))(dact, up)


def _colsum(dh):
    S, T, D = dh.shape
    tr = _pick(T, (512, 256, 128, 64, 32, 16))

    def body(x_ref, o_ref):
        @pl.when(pl.program_id(1) == 0)
        def _():
            o_ref[...] = jnp.zeros_like(o_ref)

        o_ref[...] += jnp.sum(x_ref[...].astype(F32), axis=0, keepdims=True)

    return pl.pallas_call(
        body, name="colsum", grid=(S, T // tr),
        in_specs=[pl.BlockSpec((None, tr, D), lambda s, i: (s, i, 0))],
        out_specs=pl.BlockSpec((None, 1, D), lambda s, i: (s, 0, 0)),
        out_shape=jax.ShapeDtypeStruct((S, 1, D), F32), compiler_params=_cp("parallel", "arbitrary"))(dh)


def _lb_softmax(x):
    L = x.shape[0]
    rows = [x[l:l + 1] for l in range(L)]
    m = rows[0]
    for r in rows[1:]:
        m = jnp.maximum(m, r)
    e = [jnp.exp(r - m) for r in rows]
    s = e[0]
    for r in e[1:]:
        s = s + r
    return [r / s for r in e]


def _lb_fwd(lb_logits):
    L, D = lb_logits.shape

    def body(x_ref, o_ref):
        p = _lb_softmax(x_ref[...])
        run = jnp.zeros_like(p[0])
        for l in range(L):
            if l > 0:
                run = run + p[l]
            o_ref[pl.ds(l, 1), :] = run

    return pl.pallas_call(body, name="lb_fwd", out_shape=jax.ShapeDtypeStruct((L, D), F32))(lb_logits)


def _lb_bwd(lb_logits, dlbs):
    L, D = lb_logits.shape

    def body(x_ref, d_ref, o_ref):
        p = _lb_softmax(x_ref[...])
        d = d_ref[...]
        dp = [jnp.zeros_like(p[0]) for _ in range(L)]
        run = jnp.zeros_like(p[0])
        for j in range(L - 1, 0, -1):
            run = run + d[j:j + 1]
            dp[j] = run
        dot = dp[0] * p[0]
        for j in range(1, L):
            dot = dot + dp[j] * p[j]
        for j in range(L):
            o_ref[pl.ds(j, 1), :] = p[j] * (dp[j] - dot)

    return pl.pallas_call(body, name="lb_bwd", out_shape=jax.ShapeDtypeStruct((L, D), F32))(lb_logits, dlbs)


def _blk_cumsum(x, c, reverse=False):
    n = x.shape[0]
    pos = lax.broadcasted_iota(jnp.int32, x.shape, 0) % c
    s = 1
    while s < c:
        if reverse:
            shifted = pltpu.roll(x, n - s, 0)
            x = x + jnp.where(pos + s < c, shifted, 0.0)
        else:
            shifted = pltpu.roll(x, s, 0)
            x = x + jnp.where(pos >= s, shifted, 0.0)
        s *= 2
    return x


def _hgrn_prologue(q_ref, f_ref, lb_ref):
    lbv = lb_ref[...]
    z = f_ref[...]
    sig = jax.nn.sigmoid(z)
    one_m = 1.0 - lbv
    f = lbv + one_m * sig
    logf = jnp.log(jnp.maximum(f, F_MIN))
    k = one_m * jax.nn.sigmoid(-z)
    q = _silu(q_ref[...])
    return q, k, logf, f, sig, one_m


def _hgrn_fwd(h, lbs_l, gw):
    _, T, D = h.shape
    nh = D // LANES
    c = HG_C
    Tt = _pick(T, (128, 64, 32, 16))
    nb = Tt // c

    def body(q_ref, f_ref, i_ref, g_ref, lb_ref, gw_ref, o_ref, y_ref, sall_ref,
             st_ref, G_s, q_s, k_s, W_s, o_s):
        @pl.when(pl.program_id(1) == 0)
        def _():
            st_ref[...] = jnp.zeros_like(st_ref)

        q, k, logf, _, _, _ = _hgrn_prologue(q_ref, f_ref, lb_ref)
        G_s[...] = _blk_cumsum(logf, c)
        q_s[...] = q
        k_s[...] = k
        ones = jnp.ones((LANES, LANES), MXU_DTYPE)
        rowid = lax.broadcasted_iota(jnp.int32, (c, LANES), 0)
        for bi in range(nb):
            r0 = bi * c
            Gb = G_s[pl.ds(r0, c), :]
            qb = q_s[pl.ds(r0, c), :]
            kb = k_s[pl.ds(r0, c), :]
            vb = i_ref[pl.ds(r0, c), :]
            glast = G_s[pl.ds(r0 + c - 1, 1), :]
            st = st_ref[...]
            sall_ref[bi] = st
            qd = qb * jnp.exp(Gb)
            o_s[pl.ds(r0, c), :] = lax.dot_general(qd.astype(MXU_DTYPE), st.astype(MXU_DTYPE), _NT,
                                                   preferred_element_type=F32)
            kd = kb * jnp.exp(glast - Gb)
            st_ref[...] = st * jnp.exp(glast) + lax.dot_general(
                vb.astype(MXU_DTYPE), kd.astype(MXU_DTYPE), _TN, preferred_element_type=F32)
            for t in range(c):
                gt = G_s[pl.ds(r0 + t, 1), :]
                qt = q_s[pl.ds(r0 + t, 1), :]
                e = jnp.where(rowid <= t, jnp.exp(jnp.minimum(gt - Gb, 0.0)), 0.0)
                W_s[pl.ds(t * c, c), :] = (e * kb * qt).astype(MXU_DTYPE)
            r = jnp.dot(W_s[...], ones, preferred_element_type=F32)
            for t in range(c):
                o_s[pl.ds(r0 + t, 1), :] += jnp.sum(r[t * c:(t + 1) * c] * vb, axis=0, keepdims=True)
        o = o_s[...]
        n = o * lax.rsqrt(jnp.mean(o * o, axis=-1, keepdims=True) + RMS_EPS)
        o_ref[...] = o
        y_ref[...] = (n * gw_ref[...] * _silu(g_ref[...])).astype(ACT_DTYPE)

    def sec(s):
        return pl.BlockSpec((None, Tt, LANES), lambda hd, i: (s, i, hd))

    col = pl.BlockSpec((Tt, LANES), lambda hd, i: (i, hd))
    return pl.pallas_call(
        body, name="hgrn_fwd", grid=(nh, T // Tt),
        in_specs=[sec(0), sec(1), sec(2), sec(3), pl.BlockSpec((1, LANES), lambda hd, i: (0, hd)),
                  pl.BlockSpec((1, LANES), lambda hd, i: (0, 0))],
        out_specs=[col, col, pl.BlockSpec((nb, None, LANES, LANES), lambda hd, i: (i, hd, 0, 0))],
        out_shape=[jax.ShapeDtypeStruct((T, D), F32), jax.ShapeDtypeStruct((T, D), ACT_DTYPE),
                   jax.ShapeDtypeStruct((T // c, nh, LANES, LANES), F32)],
        scratch_shapes=[pltpu.VMEM((LANES, LANES), F32), pltpu.VMEM((Tt, LANES), F32),
                        pltpu.VMEM((Tt, LANES), F32), pltpu.VMEM((Tt, LANES), F32),
                        pltpu.VMEM((c * c, LANES), MXU_DTYPE), pltpu.VMEM((Tt, LANES), F32)],
        compiler_params=_cp("parallel", "arbitrary"))(h, h, h, h, lbs_l, gw)


def _hgrn_bwd(h, lbs_l, gw, o_pre, st_all, dy, dh):
    _, T, D = h.shape
    nh = D // LANES
    c = HG_C
    Tt = _pick(T, (128, 64, 32, 16))
    nb = Tt // c
    nT = T // Tt

    def body(q_ref, f_ref, i_ref, g_ref, lb_ref, gw_ref, o_ref, sall_ref, dy_ref, dh_in_ref,
             dh_ref, dlb_ref, dgw_ref,
             dst_ref, G_s, q_s, k_s, do_s, E_s, WP_s, dq_s, dk_s, dv_s, dG_s):
        del dh_in_ref
        hd, ti = pl.program_id(0), pl.program_id(1)

        @pl.when(ti == 0)
        def _():
            dst_ref[...] = jnp.zeros_like(dst_ref)
            dlb_ref[...] = jnp.zeros_like(dlb_ref)

        @pl.when((ti == 0) & (hd == 0))
        def _():
            dgw_ref[...] = jnp.zeros_like(dgw_ref)

        q, k, logf, f, sig, one_m = _hgrn_prologue(q_ref, f_ref, lb_ref)
        G_s[...] = _blk_cumsum(logf, c)
        q_s[...] = q
        k_s[...] = k

        o = o_ref[...]
        gr = g_ref[...]
        dy_ = dy_ref[...]
        rr = lax.rsqrt(jnp.mean(o * o, axis=-1, keepdims=True) + RMS_EPS)
        n = o * rr
        sg = _silu(gr)
        gwv = gw_ref[...]
        dh_ref[3] = (dy_ * n * gwv * _dsilu(gr)).astype(ACT_DTYPE)
        dgw_ref[...] += jnp.sum(dy_ * n * sg, axis=0, keepdims=True)
        dn = dy_ * gwv * sg
        do_s[...] = rr * (dn - n * jnp.mean(dn * n, axis=-1, keepdims=True))

        ones = jnp.ones((LANES, LANES), MXU_DTYPE)
        rowid = lax.broadcasted_iota(jnp.int32, (c, LANES), 0)
        cc = c * c
        for bi in range(nb - 1, -1, -1):
            r0 = bi * c
            Gb = G_s[pl.ds(r0, c), :]
            qb = q_s[pl.ds(r0, c), :]
            kb = k_s[pl.ds(r0, c), :]
            vb = i_ref[pl.ds(r0, c), :]
            dob = do_s[pl.ds(r0, c), :]
            glast = G_s[pl.ds(r0 + c - 1, 1), :]
            st = sall_ref[bi]
            dst = dst_ref[...]
            a = jnp.exp(glast)
            eg = jnp.exp(Gb)
            egl = jnp.exp(glast - Gb)
            qd = qb * eg
            kd = kb * egl
            dob_m = dob.astype(MXU_DTYPE)
            dst_m = dst.astype(MXU_DTYPE)
            dqd = lax.dot_general(dob_m, st.astype(MXU_DTYPE), _NN, preferred_element_type=F32)
            dkd = lax.dot_general(vb.astype(MXU_DTYPE), dst_m, _NN, preferred_element_type=F32)
            dv_inter = lax.dot_general(kd.astype(MXU_DTYPE), dst_m, _NT, preferred_element_type=F32)
            da = jnp.sum(dst * st, axis=0, keepdims=True)
            dst_ref[...] = dst * a + lax.dot_general(dob_m, qd.astype(MXU_DTYPE), _TN,
                                                     preferred_element_type=F32)
            for t in range(c):
                gt = G_s[pl.ds(r0 + t, 1), :]
                qt = q_s[pl.ds(r0 + t, 1), :]
                dot_ = do_s[pl.ds(r0 + t, 1), :]
                e = jnp.where(rowid <= t, jnp.exp(jnp.minimum(gt - Gb, 0.0)), 0.0)
                E_s[pl.ds(t * c, c), :] = e
                WP_s[pl.ds(t * c, c), :] = (e * kb * qt).astype(MXU_DTYPE)
                WP_s[pl.ds(cc + t * c, c), :] = (vb * dot_).astype(MXU_DTYPE)
            r = jnp.dot(WP_s[...], ones, preferred_element_type=F32)
            dk_i = jnp.zeros((c, LANES), F32)
            dv_i = jnp.zeros((c, LANES), F32)
            for t in range(c):
                qt = q_s[pl.ds(r0 + t, 1), :]
                dot_ = do_s[pl.ds(r0 + t, 1), :]
                e = E_s[pl.ds(t * c, c), :]
                a_t = r[t * c:(t + 1) * c]
                da_t = r[cc + t * c:cc + (t + 1) * c]
                dae = da_t * e
                dq_s[pl.ds(r0 + t, 1), :] = jnp.sum(dae * kb, axis=0, keepdims=True)
                dk_i = dk_i + dae * qt
                dv_i = dv_i + a_t * dot_
            dq_i = dq_s[pl.ds(r0, c), :]
            dq_s[pl.ds(r0, c), :] = dqd * eg + dq_i
            dk_s[pl.ds(r0, c), :] = dkd * egl + dk_i
            dv_s[pl.ds(r0, c), :] = dv_inter + dv_i
            dkdkd = dkd * kd
            dG = dqd * qd + qb * dq_i - kb * dk_i - dkdkd
            dglast = jnp.sum(dkdkd, axis=0, keepdims=True) + da * a
            dG_s[pl.ds(r0, c), :] = dG + jnp.where(rowid == c - 1, dglast, 0.0)

        dlogf = _blk_cumsum(dG_s[...], c, reverse=True)
        df = jnp.where(f > F_MIN, dlogf / f, 0.0)
        dk = dk_s[...]
        dh_ref[0] = (dq_s[...] * _dsilu(q_ref[...])).astype(ACT_DTYPE)
        dh_ref[1] = ((df - dk) * one_m * sig * (1.0 - sig)).astype(ACT_DTYPE)
        dh_ref[2] = dv_s[...].astype(ACT_DTYPE)
        dlb_ref[...] += jnp.sum((df - dk) * (1.0 - sig), axis=0, keepdims=True)

    def sec(s):
        return pl.BlockSpec((None, Tt, LANES), lambda hd, i: (s, nT - 1 - i, hd))

    col = pl.BlockSpec((Tt, LANES), lambda hd, i: (nT - 1 - i, hd))
    tile = pltpu.VMEM((Tt, LANES), F32)
    return pl.pallas_call(
        body, name="hgrn_bwd", grid=(nh, nT),
        in_specs=[sec(0), sec(1), sec(2), sec(3), pl.BlockSpec((1, LANES), lambda hd, i: (0, hd)),
                  pl.BlockSpec((1, LANES), lambda hd, i: (0, 0)), col,
                  pl.BlockSpec((nb, None, LANES, LANES), lambda hd, i: (nT - 1 - i, hd, 0, 0)), col,
                  pl.BlockSpec(memory_space=pl.ANY)],
        out_specs=[pl.BlockSpec((4, Tt, LANES), lambda hd, i: (0, nT - 1 - i, hd)),
                   pl.BlockSpec((1, LANES), lambda hd, i: (0, hd)),
                   pl.BlockSpec((1, LANES), lambda hd, i: (0, 0))],
        out_shape=[jax.ShapeDtypeStruct(dh.shape, dh.dtype), jax.ShapeDtypeStruct((1, D), F32),
                   jax.ShapeDtypeStruct((1, LANES), F32)],
        scratch_shapes=[pltpu.VMEM((LANES, LANES), F32), tile, tile, tile, tile,
                        pltpu.VMEM((c * c, LANES), F32), pltpu.VMEM((2 * c * c, LANES), MXU_DTYPE),
                        tile, tile, tile, tile],
        input_output_aliases={9: 0},
        compiler_params=_cp("arbitrary", "arbitrary"))(h, h, h, h, lbs_l, gw, o_pre, st_all, dy, dh)


def _conv_fwd(h, w_dw, b_dw, ln_g, ln_b):
    _, T, D = h.shape
    Tt = _pick(T, (256, 128, 64, 32))
    hb = Tt // CONV_HALO
    off = CONV_HALO - (CONV_WIDTH - 1)

    def body(a_ref, b_ref, ap_ref, bp_ref, w_ref, bd_ref, g_ref, be_ref, yc_ref, y_ref, U_s):
        first = pl.program_id(0) == 0
        up = ap_ref[...] * jax.nn.sigmoid(bp_ref[...])
        U_s[pl.ds(0, CONV_HALO), :] = jnp.where(first, 0.0, up)
        U_s[pl.ds(CONV_HALO, Tt), :] = a_ref[...] * jax.nn.sigmoid(b_ref[...])
        for cb in range(D // LANES):
            cs = pl.ds(cb * LANES, LANES)
            acc = jnp.zeros((Tt, LANES), F32)
            for j in range(CONV_WIDTH):
                acc = acc + w_ref[pl.ds(j, 1), cs] * U_s[pl.ds(off + j, Tt), cs]
            yc_ref[:, cs] = acc + bd_ref[:, cs]
        yc = yc_ref[...]
        mu = jnp.mean(yc, axis=-1, keepdims=True)
        zc = yc - mu
        var = jnp.mean(zc * zc, axis=-1, keepdims=True)
        ln = zc * lax.rsqrt(var + LN_EPS) * g_ref[...] + be_ref[...]
        y_ref[...] = _silu(ln).astype(ACT_DTYPE)

    def main(s):
        return pl.BlockSpec((None, Tt, D), lambda i: (s, i, 0))

    def prev(s):
        return pl.BlockSpec((None, CONV_HALO, D), lambda i: (s, jnp.maximum(i * hb - 1, 0), 0))

    row = pl.BlockSpec((Tt, D), lambda i: (i, 0))
    vec = pl.BlockSpec((1, D), lambda i: (0, 0))
    return pl.pallas_call(
        body, name="conv_fwd", grid=(T // Tt,),
        in_specs=[main(4), main(5), prev(4), prev(5), pl.BlockSpec((CONV_HALO, D), lambda i: (0, 0)),
                  vec, vec, vec],
        out_specs=[row, row],
        out_shape=[jax.ShapeDtypeStruct((T, D), F32), jax.ShapeDtypeStruct((T, D), ACT_DTYPE)],
        scratch_shapes=[pltpu.VMEM((CONV_HALO + Tt, D), F32)],
        compiler_params=_cp("parallel"))(h, h, h, h, w_dw, b_dw, ln_g, ln_b)


def _conv_bwd(h, w_dw, ln_g, ln_b, yc, dy, dh):
    _, T, D = h.shape
    Tt = _pick(T, (256, 128, 64, 32))
    hb = Tt // CONV_HALO
    nT = T // Tt
    nhb = T // CONV_HALO
    off = CONV_HALO - (CONV_WIDTH - 1)

    def body(a_ref, b_ref, ap_ref, bp_ref, w_ref, g_ref, be_ref, yc_ref, ycn_ref, dy_ref, dyn_ref, dh_in_ref,
             dh_ref, dw_ref, dbd_ref, dg_ref, dbe_ref, U_s, DY_s, du_s):
        del dh_in_ref
        i = pl.program_id(0)

        @pl.when(i == 0)
        def _():
            dw_ref[...] = jnp.zeros_like(dw_ref)
            dbd_ref[...] = jnp.zeros_like(dbd_ref)
            dg_ref[...] = jnp.zeros_like(dg_ref)
            dbe_ref[...] = jnp.zeros_like(dbe_ref)

        gv = g_ref[...]
        bev = be_ref[...]

        def ln_silu_bwd(ycv, dyv):
            mu = jnp.mean(ycv, axis=-1, keepdims=True)
            zc = ycv - mu
            rstd = lax.rsqrt(jnp.mean(zc * zc, axis=-1, keepdims=True) + LN_EPS)
            xhat = zc * rstd
            dln = dyv * _dsilu(xhat * gv + bev)
            dxh = dln * gv
            dyc = rstd * (dxh - jnp.mean(dxh, axis=-1, keepdims=True)
                          - xhat * jnp.mean(dxh * xhat, axis=-1, keepdims=True))
            return dyc, dln, xhat

        dyc, dln, xhat = ln_silu_bwd(yc_ref[...], dy_ref[...])
        dg_ref[...] += jnp.sum(dln * xhat, axis=0, keepdims=True)
        dbe_ref[...] += jnp.sum(dln, axis=0, keepdims=True)
        dbd_ref[...] += jnp.sum(dyc, axis=0, keepdims=True)
        DY_s[pl.ds(0, Tt), :] = dyc
        dycn, _, _ = ln_silu_bwd(ycn_ref[...], dyn_ref[...])
        DY_s[pl.ds(Tt, CONV_HALO), :] = jnp.where(i == nT - 1, 0.0, dycn)

        sb = jax.nn.sigmoid(b_ref[...])
        av = a_ref[...]
        up = ap_ref[...] * jax.nn.sigmoid(bp_ref[...])
        U_s[pl.ds(0, CONV_HALO), :] = jnp.where(i == 0, 0.0, up)
        U_s[pl.ds(CONV_HALO, Tt), :] = av * sb

        for cb in range(D // LANES):
            cs = pl.ds(cb * LANES, LANES)
            dyb = DY_s[pl.ds(0, Tt), cs]
            acc = jnp.zeros((Tt, LANES), F32)
            for j in range(CONV_WIDTH):
                acc = acc + w_ref[pl.ds(j, 1), cs] * DY_s[pl.ds(CONV_WIDTH - 1 - j, Tt), cs]
                dw_ref[pl.ds(j, 1), cs] += jnp.sum(dyb * U_s[pl.ds(off + j, Tt), cs], axis=0, keepdims=True)
            du_s[:, cs] = acc
        du = du_s[...]
        dh_ref[0] = (du * sb).astype(ACT_DTYPE)
        dh_ref[1] = (du * av * sb * (1.0 - sb)).astype(ACT_DTYPE)

    def main(s):
        return pl.BlockSpec((None, Tt, D), lambda i: (s, i, 0))

    def prev(s):
        return pl.BlockSpec((None, CONV_HALO, D), lambda i: (s, jnp.maximum(i * hb - 1, 0), 0))

    row = pl.BlockSpec((Tt, D), lambda i: (i, 0))
    nxt = pl.BlockSpec((CONV_HALO, D), lambda i: (jnp.minimum((i + 1) * hb, nhb - 1), 0))
    vec = pl.BlockSpec((1, D), lambda i: (0, 0))
    wspec = pl.BlockSpec((CONV_HALO, D), lambda i: (0, 0))
    return pl.pallas_call(
        body, name="conv_bwd", grid=(nT,),
        in_specs=[main(4), main(5), prev(4), prev(5), wspec, vec, vec, row, nxt, row, nxt,
                  pl.BlockSpec(memory_space=pl.ANY)],
        out_specs=[pl.BlockSpec((2, Tt, D), lambda i: (2, i, 0)), wspec, vec, vec, vec],
        out_shape=[jax.ShapeDtypeStruct(dh.shape, dh.dtype), jax.ShapeDtypeStruct((CONV_HALO, D), F32),
                   jax.ShapeDtypeStruct((1, D), F32), jax.ShapeDtypeStruct((1, D), F32),
                   jax.ShapeDtypeStruct((1, D), F32)],
        scratch_shapes=[pltpu.VMEM((CONV_HALO + Tt, D), F32), pltpu.VMEM((Tt + CONV_HALO, D), F32),
                        pltpu.VMEM((Tt, D), F32)],
        input_output_aliases={11: 0},
        compiler_params=_cp("arbitrary"))(h, h, h, h, w_dw, ln_g, ln_b, yc, yc, dy, dy, dh)


def _adamw(name, w, m, v, parts, part_specs, tr, prefetch=None):
    R, C = w.shape
    bc1 = 1.0 - ADAM_B1 ** ADAM_STEP
    bc2 = 1.0 - ADAM_B2 ** ADAM_STEP
    npart = len(parts)
    npre = 0 if prefetch is None else 1

    def body(*refs):
        refs = refs[npre:]
        w_ref, m_ref, v_ref = refs[:3]
        p_refs = refs[3:3 + npart]
        g_ref, d_ref, mo_ref, vo_ref = refs[3 + npart:]
        g = p_refs[0][...].astype(F32)
        for p in p_refs[1:]:
            g = g + p[...].astype(F32)
        wv = w_ref[...]
        mn = ADAM_B1 * m_ref[...] + (1.0 - ADAM_B1) * g
        vn = ADAM_B2 * v_ref[...] + (1.0 - ADAM_B2) * (g * g)
        m_hat = mn / bc1
        v_hat = vn / bc2
        g_ref[...] = g
        d_ref[...] = -ADAM_LR * (m_hat / (jnp.sqrt(v_hat) + ADAM_EPS) + ADAM_WD * wv)
        mo_ref[...] = mn
        vo_ref[...] = vn

    if prefetch is None:
        row = pl.BlockSpec((tr, C), lambda i: (i, 0))
    else:
        row = pl.BlockSpec((tr, C), lambda i, s: (i, 0))
    out = jax.ShapeDtypeStruct((R, C), F32)
    gs = pltpu.PrefetchScalarGridSpec(
        num_scalar_prefetch=npre, grid=(R // tr,), in_specs=[row, row, row] + list(part_specs),
        out_specs=[row] * 4)
    args = ([prefetch] if npre else []) + [w, m, v] + list(parts)
    return pl.pallas_call(body, name=name, grid_spec=gs, out_shape=[out] * 4,
                          compiler_params=_cp("parallel"))(*args)


def _pair_add(p, r1, my_c):
    _, R, C = r1.shape
    tr = _pick(R, (512, 256, 128, 64, 32, 16))

    def body(c_ref, p_ref, r_ref, q_ref):
        del c_ref
        q_ref[...] = (p_ref[...].astype(F32) + r_ref[...].astype(F32)).astype(q_ref.dtype)

    gs = pltpu.PrefetchScalarGridSpec(
        num_scalar_prefetch=1, grid=(4, R // tr),
        in_specs=[pl.BlockSpec((None, tr, C), lambda j, i, c: (2 * j + c[0], i, 0)),
                  pl.BlockSpec((None, tr, C), lambda j, i, c: (j, i, 0))],
        out_specs=pl.BlockSpec((None, tr, C), lambda j, i, c: (j, i, 0)))
    return pl.pallas_call(body, name="pair_add", grid_spec=gs, out_shape=jax.ShapeDtypeStruct(r1.shape, r1.dtype),
                          compiler_params=_cp("parallel", "parallel"))(my_c, p, r1)


_ANY = pl.BlockSpec(memory_space=pl.ANY)


def _place():
    x, y, c = lax.axis_index("x"), lax.axis_index("y"), lax.axis_index("c")
    chips = [(1 - x, y), (x, 1 - y), (1 - x, 1 - y)]
    return x, y, c, chips


def _all_gather_weights(shards):
    n_arr = len(shards)

    def body(*refs):
        srcs, outs = refs[:n_arr], refs[n_arr:2 * n_arr]
        send_sems, recv_sems, local_sems = refs[2 * n_arr:]
        x, y, c, chips = _place()
        me, sibling = (x, y, c), (x, y, 1 - c)

        def slot(n, px, py, pc):
            return outs[n].at[4 * px + 2 * py + pc]

        def copy(n, k, block, to, src=None):
            return pltpu.make_async_remote_copy(
                src_ref=slot(n, *block) if src is None else src, dst_ref=slot(n, *block),
                send_sem=send_sems.at[7 * n + k], recv_sem=recv_sems.at[7 * n + k],
                device_id=to, device_id_type=MESH)

        mine = [pltpu.make_async_copy(srcs[n], slot(n, *me), local_sems.at[n]) for n in range(n_arr)]
        for cp in mine:
            cp.start()
        first = []
        for n in range(n_arr):
            first.append(copy(n, 0, me, sibling, src=srcs[n]))
            first += [copy(n, 1 + j, me, (*chip, c), src=srcs[n]) for j, chip in enumerate(chips)]
        for cp in first:
            cp.start()
        passed = []
        for j, chip in enumerate(chips):
            for n in range(n_arr):
                copy(n, 1 + j, (*chip, c), me).wait_recv()
                fwd = copy(n, 4 + j, (*chip, c), sibling)
                fwd.start()
                passed.append(fwd)
        for n in range(n_arr):
            copy(n, 0, sibling, me).wait_recv()
            for j, chip in enumerate(chips):
                copy(n, 4 + j, (*chip, 1 - c), me).wait_recv()
        for cp in first + passed:
            cp.wait_send()
        for cp in mine:
            cp.wait()

    return pl.pallas_call(
        body, name="all_gather_weights",
        out_shape=[jax.ShapeDtypeStruct((N_DEV,) + s.shape, s.dtype) for s in shards],
        in_specs=[_ANY] * n_arr, out_specs=[_ANY] * n_arr,
        scratch_shapes=[pltpu.SemaphoreType.DMA((7 * n_arr,)), pltpu.SemaphoreType.DMA((7 * n_arr,)),
                        pltpu.SemaphoreType.DMA((n_arr,))])(*shards)


def _exchange_sibling(bufs):
    n_arr = len(bufs)

    def body(*refs):
        srcs, outs = refs[:n_arr], refs[n_arr:2 * n_arr]
        send_sems, recv_sems = refs[2 * n_arr:]
        x, y, c, _ = _place()
        copies = []
        for n in range(n_arr):
            for j in range(4):
                copies.append(pltpu.make_async_remote_copy(
                    src_ref=srcs[n].at[2 * j + 1 - c], dst_ref=outs[n].at[j],
                    send_sem=send_sems.at[4 * n + j], recv_sem=recv_sems.at[4 * n + j],
                    device_id=(x, y, 1 - c), device_id_type=MESH))
        for cp in copies:
            cp.start()
        for cp in copies:
            cp.wait()

    return pl.pallas_call(
        body, name="exchange_sibling",
        out_shape=[jax.ShapeDtypeStruct((4,) + b.shape[1:], b.dtype) for b in bufs],
        in_specs=[_ANY] * n_arr, out_specs=[_ANY] * n_arr,
        scratch_shapes=[pltpu.SemaphoreType.DMA((4 * n_arr,)), pltpu.SemaphoreType.DMA((4 * n_arr,))])(*bufs)


def _exchange_chips(bufs):
    n_arr = len(bufs)

    def body(*refs):
        srcs, outs = refs[:n_arr], refs[n_arr:2 * n_arr]
        send_sems, recv_sems = refs[2 * n_arr:]
        _, _, c, chips = _place()
        copies = []
        for n in range(n_arr):
            for k, (cx, cy) in enumerate(chips):
                copies.append(pltpu.make_async_remote_copy(
                    src_ref=srcs[n].at[2 * cx + cy], dst_ref=outs[n].at[k],
                    send_sem=send_sems.at[3 * n + k], recv_sem=recv_sems.at[3 * n + k],
                    device_id=(cx, cy, c), device_id_type=MESH))
        for cp in copies:
            cp.start()
        for cp in copies:
            cp.wait()

    return pl.pallas_call(
        body, name="exchange_chips",
        out_shape=[jax.ShapeDtypeStruct((3,) + b.shape[1:], b.dtype) for b in bufs],
        in_specs=[_ANY] * n_arr, out_specs=[_ANY] * n_arr,
        scratch_shapes=[pltpu.SemaphoreType.DMA((3 * n_arr,)), pltpu.SemaphoreType.DMA((3 * n_arr,))])(*bufs)


def _all_gather_small(part):
    def body(src, out, send_sems, recv_sems, local_sem):
        x, y, c, _ = _place()
        mine = pltpu.make_async_copy(src, out.at[4 * x + 2 * y + c], local_sem)
        mine.start()
        copies = []
        for r in range(1, N_DEV):
            dx, dy, dc = (r >> 2) & 1, (r >> 1) & 1, r & 1
            peer = (1 - x if dx else x, 1 - y if dy else y, 1 - c if dc else c)
            copies.append(pltpu.make_async_remote_copy(
                src_ref=src, dst_ref=out.at[4 * x + 2 * y + c],
                send_sem=send_sems.at[r - 1], recv_sem=recv_sems.at[r - 1],
                device_id=peer, device_id_type=MESH))
        for cp in copies:
            cp.start()
        for cp in copies:
            cp.wait()
        mine.wait()

    return pl.pallas_call(
        body, name="all_gather_small",
        out_shape=jax.ShapeDtypeStruct((N_DEV,) + part.shape, part.dtype),
        in_specs=[_ANY], out_specs=_ANY,
        scratch_shapes=[pltpu.SemaphoreType.DMA((N_DEV - 1,)), pltpu.SemaphoreType.DMA((N_DEV - 1,)),
                        pltpu.SemaphoreType.DMA])(part)


def _layer_fwd(xin, xin_bf, W, P, alpha):
    h = _proj_in(xin_bf, W["w_in"], P["b_in"])
    o_pre, y_hg, st_all = _hgrn_fwd(h, P["lbs"], P["g_norm_w"])
    yc_pre, y_cv = _conv_fwd(h, P["w_dw"], P["b_dw"], P["conv_ln_g"], P["conv_ln_b"])
    y_h = _mm_nn("branch_a", y_hg, W["w_a"], F32)
    y_c = _mm_nn("branch_b", y_cv, W["w_b"], F32, bias=P["b_b"])
    merged = _gate_fwd(y_h, y_c, h)
    mix = _mm_nn("mix_out", merged, W["w_o"], F32)
    x1, x1_bf, z1 = _ln_fwd("ln1", xin, mix, alpha, P["ln1_g"], P["ln1_b"])
    up = _ffn_up(x1_bf, W["w_up"])
    act = _swiglu_fwd(up)
    ffn = _mm_nn("ffn_down", act, W["w_down"], F32)
    x2, x2_bf, z2 = _ln_fwd("ln2", x1, ffn, alpha, P["ln2_g"], P["ln2_b"])
    saved = dict(xin_bf=xin_bf, h=h, o_pre=o_pre, y_hg=y_hg, st_all=st_all, yc_pre=yc_pre, y_cv=y_cv,
                 y_h=y_h, y_c=y_c, merged=merged, z1=z1, x1_bf=x1_bf, up=up, act=act, z2=z2)
    return x2, x2_bf, saved


def _layer_bwd(dx2, S, W, P, alpha):
    dz2, dz2_bf, dln2_g, dln2_b = _ln_bwd("ln2_bwd", S["z2"], dx2, P["ln2_g"])
    dact = _mm_nt("ffn_down_dx", dz2_bf, W["w_down"], F32)
    dw_down = _mm_tn("ffn_down_dw", S["act"], dz2_bf, ACT_DTYPE)
    dup = _swiglu_bwd(dact, S["up"])
    dx1 = _ffn_up_dx(dup, W["w_up"], dz2, alpha)
    dw_up = _ffn_up_dw(S["x1_bf"], dup)
    dz1, dz1_bf, dln1_g, dln1_b = _ln_bwd("ln1_bwd", S["z1"], dx1, P["ln1_g"])
    dmerged = _mm_nt("mix_out_dx", dz1_bf, W["w_o"], F32)
    dw_o = _mm_tn("mix_out_dw", S["merged"], dz1_bf, ACT_DTYPE)
    dy_h, dy_c, db_b, dh = _gate_bwd(dmerged, S["y_h"], S["y_c"], S["h"])
    dy_cv = _mm_nt("branch_b_dx", dy_c, W["w_b"], F32)
    dw_b = _mm_tn("branch_b_dw", S["y_cv"], dy_c, ACT_DTYPE)
    dy_hg = _mm_nt("branch_a_dx", dy_h, W["w_a"], F32)
    dw_a = _mm_tn("branch_a_dw", S["y_hg"], dy_h, ACT_DTYPE)
    dh, dw_dw, db_dw, dcln_g, dcln_b = _conv_bwd(S["h"], P["w_dw"], P["conv_ln_g"], P["conv_ln_b"],
                                                 S["yc_pre"], dy_cv, dh)
    dh, dlbs, dgw = _hgrn_bwd(S["h"], P["lbs"], P["g_norm_w"], S["o_pre"], S["st_all"], dy_hg, dh)
    dxin = _proj_in_dx(dh, W["w_in"], dz1, alpha)
    dw_in = _proj_in_dw(S["xin_bf"], dh)
    db_in = _colsum(dh)
    big = dict(w_in=dw_in, w_a=dw_a, w_b=dw_b, w_o=dw_o, w_down=dw_down, w_up=dw_up)
    small = dict(b_in=db_in, lbs=dlbs, g_norm_w=dgw, w_dw=dw_dw, b_dw=db_dw, conv_ln_g=dcln_g,
                 conv_ln_b=dcln_b, b_b=db_b, ln1_g=dln1_g, ln1_b=dln1_b, ln2_g=dln2_g, ln2_b=dln2_b)
    return dxin, big, small


_SMALL = ("b_in", "lb_logits", "g_norm_w", "b_dw", "conv_ln_g", "conv_ln_b", "b_b", "ln1_g", "ln1_b", "ln2_g",
          "ln2_b")


def _pack_small(per_layer, ln0_g, ln0_b, extra_row, D, L):
    rows = []
    for l in range(L):
        for n in _SMALL:
            a = per_layer[n][l]
            if n == "b_in":
                rows.append(a.reshape(N_SEC, D))
            elif n == "g_norm_w":
                rows.append(jnp.pad(a.reshape(1, -1), ((0, 0), (0, D - a.size))))
            else:
                rows.append(a.reshape(1, D))
    rows += [ln0_g.reshape(1, D), ln0_b.reshape(1, D), extra_row]
    buf = jnp.concatenate(rows, axis=0)
    pad = (-buf.shape[0]) % 8
    return jnp.pad(buf, ((0, pad), (0, 0)))


def _unpack_small(buf, D, L, hv):
    out = {n: [] for n in _SMALL}
    r = 0
    for l in range(L):
        for n in _SMALL:
            if n == "b_in":
                out[n].append(buf[r:r + N_SEC].reshape(N_SEC * D))
                r += N_SEC
            elif n == "g_norm_w":
                out[n].append(buf[r, :hv])
                r += 1
            else:
                out[n].append(buf[r])
                r += 1
    res = {n: jnp.stack(v) for n, v in out.items()}
    res["ln0_g"] = buf[r]
    res["ln0_b"] = buf[r + 1]
    return res, r + 2


def kernel(x, ln0_g, ln0_b, w_in, b_in, lb_logits, g_norm_w, w_a, w_dw, b_dw, conv_ln_g, conv_ln_b, w_b, b_b, w_o, ln1_g, ln1_b, w_up, w_down, ln2_g, ln2_b, loss_target, m_ln0_g, m_ln0_b, m_w_in, m_b_in, m_lb_logits, m_g_norm_w, m_w_a, m_w_dw, m_b_dw, m_conv_ln_g, m_conv_ln_b, m_w_b, m_b_b, m_w_o, m_ln1_g, m_ln1_b, m_w_up, m_w_down, m_ln2_g, m_ln2_b, v_ln0_g, v_ln0_b, v_w_in, v_b_in, v_lb_logits, v_g_norm_w, v_w_a, v_w_dw, v_b_dw, v_conv_ln_g, v_conv_ln_b, v_w_b, v_b_b, v_w_o, v_ln1_g, v_ln1_b, v_w_up, v_w_down, v_ln2_g, v_ln2_b):
    L, D = w_in.shape[0], w_in.shape[1]
    T = x.shape[0] * x.shape[1]
    Dn = w_in.shape[2]
    rs = w_a.shape[1]
    rd = w_down.shape[1]
    cu = w_up.shape[2]
    F = rd * N_DEV
    hv = g_norm_w.shape[1]
    alpha = (2 * L) ** 0.25
    my_x, my_y, my_c = lax.axis_index("x"), lax.axis_index("y"), lax.axis_index("c")

    shard_a = jnp.concatenate([w_in.reshape(L * D, Dn), w_a.reshape(L * rs, D), w_b.reshape(L * rs, D),
                               w_o.reshape(L * rs, D), w_down.reshape(L * rd, D)], axis=0).astype(ACT_DTYPE)
    shard_b = w_up.reshape(L * D, cu).astype(ACT_DTYPE)
    ga, gb = _all_gather_weights([shard_a, shard_b])
    o_a, o_b, o_o, o_d = L * D, L * D + L * rs, L * D + 2 * L * rs, L * D + 3 * L * rs

    def weights(l):
        return dict(
            w_in=ga[:, l * D:(l + 1) * D, :],
            w_a=ga[:, o_a + l * rs:o_a + (l + 1) * rs, :].reshape(D, D),
            w_b=ga[:, o_b + l * rs:o_b + (l + 1) * rs, :].reshape(D, D),
            w_o=ga[:, o_o + l * rs:o_o + (l + 1) * rs, :].reshape(D, D),
            w_down=ga[:, o_d + l * rd:o_d + (l + 1) * rd, :].reshape(F, D),
            w_up=gb[:, l * D:(l + 1) * D, :].transpose(1, 0, 2).reshape(D, 2 * F))

    lbs = _lb_fwd(lb_logits)
    w_dw_full = None

    taps = jnp.pad(w_dw, ((0, 0), (0, CONV_HALO - CONV_WIDTH), (0, 0))).reshape(L * CONV_HALO, w_dw.shape[2])
    taps_all = _all_gather_small(taps)
    w_dw_full = taps_all.transpose(1, 0, 2).reshape(L, CONV_HALO, D)

    def params(l):
        return dict(b_in=b_in[l].reshape(N_SEC, 1, D), lbs=lbs[l].reshape(1, D), g_norm_w=g_norm_w[l].reshape(1, hv),
                    w_dw=w_dw_full[l], b_dw=b_dw[l].reshape(1, D), conv_ln_g=conv_ln_g[l].reshape(1, D),
                    conv_ln_b=conv_ln_b[l].reshape(1, D), b_b=b_b[l].reshape(1, D), ln1_g=ln1_g[l], ln1_b=ln1_b[l],
                    ln2_g=ln2_g[l], ln2_b=ln2_b[l])

    x2d = x.reshape(T, D)
    xc, xc_bf = _ln_fwd("ln0", x2d, None, 1.0, ln0_g, ln0_b)
    saved = []
    for l in range(L):
        xc, xc_bf, s = _layer_fwd(xc, xc_bf, weights(l), params(l), alpha)
        saved.append(s)

    dx, loss_row = _loss_fwd_bwd(xc, loss_target.reshape(T, D))
    big, small = [None] * L, [None] * L
    for l in range(L - 1, -1, -1):
        dx, big[l], small[l] = _layer_bwd(dx, saved[l], weights(l), params(l), alpha)
    dx0, _, dln0_g, dln0_b = _ln_bwd("ln0_bwd", x2d, dx, ln0_g)
    dlb_logits = _lb_bwd(lb_logits, jnp.concatenate([small[l]["lbs"] for l in range(L)], axis=0))

    send_a = jnp.concatenate(
        [big[l]["w_in"] for l in range(L)] + [big[l]["w_a"].reshape(N_DEV, rs, D) for l in range(L)]
        + [big[l]["w_b"].reshape(N_DEV, rs, D) for l in range(L)]
        + [big[l]["w_o"].reshape(N_DEV, rs, D) for l in range(L)]
        + [big[l]["w_down"].reshape(N_DEV, rd, D) for l in range(L)], axis=1)
    send_b = jnp.concatenate([big[l]["w_up"].reshape(D, N_DEV, cu).transpose(1, 0, 2) for l in range(L)], axis=1)
    r1a, r1b = _exchange_sibling([send_a, send_b])
    c_arr = jnp.reshape(my_c, (1,)).astype(jnp.int32)
    qa, qb = _pair_add(send_a, r1a, c_arr), _pair_add(send_b, r1b, c_arr)
    r2a, r2b = _exchange_chips([qa, qb])

    small_l = {n: [small[l][n] for l in range(L)] for n in _SMALL if n != "lb_logits"}
    small_l["lb_logits"] = [dlb_logits[l] for l in range(L)]
    loss_pad = jnp.pad(loss_row, ((0, 0), (0, D - LANES)))
    part = jnp.concatenate([_pack_small(small_l, dln0_g, dln0_b, loss_pad, D, L)]
                           + [small[l]["w_dw"] for l in range(L)], axis=0)
    parts_all = _all_gather_small(part)
    n_small = part.shape[0] - L * CONV_HALO

    chip_arr = jnp.reshape(2 * my_x + my_y, (1,)).astype(jnp.int32)

    def big_update(name, w, m, v, q, r2, row0):
        R = w.shape[0] * w.shape[1]
        C = w.shape[2]
        tr = _pick(R, (256, 128, 64, 32, 16))
        while row0 % tr:
            tr //= 2
        b0 = row0 // tr
        specs = [pl.BlockSpec((None, tr, C), lambda i, s: (s[0], b0 + i, 0))]
        specs += [pl.BlockSpec((None, tr, C), functools.partial(lambda i, s, k: (k, b0 + i, 0), k=k)) for k in range(3)]
        outs = _adamw("adamw_" + name, w.reshape(R, C), m.reshape(R, C), v.reshape(R, C), [q, r2, r2, r2], specs, tr,
                      prefetch=chip_arr)
        return [o.reshape(w.shape) for o in outs]

    upd = {}
    upd["w_in"] = big_update("w_in", w_in, m_w_in, v_w_in, qa, r2a, 0)
    upd["w_a"] = big_update("w_a", w_a, m_w_a, v_w_a, qa, r2a, o_a)
    upd["w_b"] = big_update("w_b", w_b, m_w_b, v_w_b, qa, r2a, o_b)
    upd["w_o"] = big_update("w_o", w_o, m_w_o, v_w_o, qa, r2a, o_o)
    upd["w_down"] = big_update("w_down", w_down, m_w_down, v_w_down, qa, r2a, o_d)
    upd["w_up"] = big_update("w_up", w_up, m_w_up, v_w_up, qb, r2b, 0)

    inputs = dict(b_in=(b_in, m_b_in, v_b_in), lb_logits=(lb_logits, m_lb_logits, v_lb_logits),
                  g_norm_w=(g_norm_w, m_g_norm_w, v_g_norm_w), b_dw=(b_dw, m_b_dw, v_b_dw),
                  conv_ln_g=(conv_ln_g, m_conv_ln_g, v_conv_ln_g), conv_ln_b=(conv_ln_b, m_conv_ln_b, v_conv_ln_b),
                  b_b=(b_b, m_b_b, v_b_b), ln1_g=(ln1_g, m_ln1_g, v_ln1_g), ln1_b=(ln1_b, m_ln1_b, v_ln1_b),
                  ln2_g=(ln2_g, m_ln2_g, v_ln2_g), ln2_b=(ln2_b, m_ln2_b, v_ln2_b))
    zero_row = jnp.zeros((1, D), F32)
    packed = [_pack_small({n: [inputs[n][i][l] for l in range(L)] for n in _SMALL},
                          (ln0_g, m_ln0_g, v_ln0_g)[i], (ln0_b, m_ln0_b, v_ln0_b)[i], zero_row, D, L)
              for i in range(3)]
    small_specs = [pl.BlockSpec((None, n_small, D), functools.partial(lambda i, d: (d, 0, 0), d=d))
                   for d in range(N_DEV)]
    s_out = _adamw("adamw_small", packed[0], packed[1], packed[2], [parts_all] * N_DEV, small_specs, n_small)
    s_g, n_rows = _unpack_small(s_out[0], D, L, hv)
    s_d, _ = _unpack_small(s_out[1], D, L, hv)
    s_m, _ = _unpack_small(s_out[2], D, L, hv)
    s_v, _ = _unpack_small(s_out[3], D, L, hv)
    loss = s_out[0][n_rows, 0]

    cw = w_dw.shape[2]
    dev = 4 * my_x + 2 * my_y + my_c
    tap_parts = lax.dynamic_slice_in_dim(parts_all[:, n_small:, :], dev * cw, cw, axis=2)
    tap_specs = [pl.BlockSpec((None, L * CONV_HALO, cw), functools.partial(lambda i, d: (d, 0, 0), d=d))
                 for d in range(N_DEV)]
    pad_t = lambda a: jnp.pad(a, ((0, 0), (0, CONV_HALO - CONV_WIDTH), (0, 0))).reshape(L * CONV_HALO, cw)
    t_out = _adamw("adamw_taps", pad_t(w_dw), pad_t(m_w_dw), pad_t(v_w_dw), [tap_parts] * N_DEV, tap_specs,
                   L * CONV_HALO)
    upd["w_dw"] = [o.reshape(L, CONV_HALO, cw)[:, :CONV_WIDTH, :] for o in t_out]

    order = ["ln0_g", "ln0_b", "w_in", "b_in", "lb_logits", "g_norm_w", "w_a", "w_dw", "b_dw", "conv_ln_g",
             "conv_ln_b", "w_b", "b_b", "w_o", "ln1_g", "ln1_b", "w_up", "w_down", "ln2_g", "ln2_b"]
    small_sets = (s_g, s_d, s_m, s_v)
    outs = [loss, dx0.reshape(x.shape)]
    for i in range(4):
        for n in order:
            outs.append(upd[n][i] if n in upd else small_sets[i][n])
    return tuple(outs)
```

```python
import functools

import jax
import jax.numpy as jnp
from jax import lax
from jax.experimental import pallas as pl
from jax.experimental.pallas import tpu as pltpu

F32 = jnp.float32
MXU_DTYPE = jnp.bfloat16
ACT_DTYPE = jnp.bfloat16

LANES = 128
N_DEV = 8
N_SEC = 8
CONV_WIDTH = 31
CONV_HALO = 32
HG_C = 16
LN_EPS = 1e-5
RMS_EPS = 1e-6
F_MIN = 1e-30
ADAM_LR = 0.001
ADAM_B1 = 0.9
ADAM_B2 = 0.999
ADAM_EPS = 1e-08
ADAM_WD = 0.01
ADAM_STEP = 10
VMEM_LIMIT = 56 * 1024 * 1024
MESH = pl.DeviceIdType.MESH

_NN = (((1,), (0,)), ((), ()))
_NT = (((1,), (1,)), ((), ()))
_TN = (((0,), (0,)), ((), ()))


_ANY = pl.BlockSpec(memory_space=pl.ANY)
_HBM = pl.BlockSpec(memory_space=pltpu.HBM)
_SEM = pl.BlockSpec(memory_space=pltpu.SEMAPHORE)
_EFFECT = pltpu.SideEffectType.DATAFLOW_SIDE_EFFECTING


def _cp(*sem):
    return pltpu.CompilerParams(dimension_semantics=tuple(sem), vmem_limit_bytes=VMEM_LIMIT)


def _pick(n, cands):
    for c in cands:
        if c <= n and n % c == 0:
            return c
    return n


def _silu(x):
    return x * jax.nn.sigmoid(x)


def _dsilu(x):
    s = jax.nn.sigmoid(x)
    return s * (1.0 + x * (1.0 - s))


def _matmul(name, a, b, *, dims, grid, a_spec, b_spec, out_shape, out_spec, acc_shape, nk,
            bias=None, bias_spec=None, add=None, add_spec=None, add_scale=1.0):
    has_bias, has_add = bias is not None, add is not None
    kaxis = len(grid) - 1

    def body(*refs):
        a_ref, b_ref = refs[0], refs[1]
        pos = 2
        bias_ref = add_ref = None
        if has_bias:
            bias_ref = refs[pos]
            pos += 1
        if has_add:
            add_ref = refs[pos]
            pos += 1
        o_ref = refs[pos]
        acc_ref = refs[pos + 1] if nk > 1 else None

        part = lax.dot_general(a_ref[...].astype(MXU_DTYPE), b_ref[...].astype(MXU_DTYPE), dims,
                               preferred_element_type=F32)

        def finish(r):
            if has_bias:
                r = r + bias_ref[...]
            if has_add:
                r = r + add_scale * add_ref[...]
            o_ref[...] = r.astype(o_ref.dtype)

        if nk == 1:
            finish(part)
        else:
            k = pl.program_id(kaxis)

            @pl.when(k == 0)
            def _():
                acc_ref[...] = part

            @pl.when(k > 0)
            def _():
                acc_ref[...] += part

            @pl.when(k == nk - 1)
            def _():
                finish(acc_ref[...])

    ins, specs = [a, b], [a_spec, b_spec]
    if has_bias:
        ins.append(bias)
        specs.append(bias_spec)
    if has_add:
        ins.append(add)
        specs.append(add_spec)
    sem = ("parallel",) * (len(grid) - 1) + ("arbitrary",) if nk > 1 else ("parallel",) * len(grid)
    return pl.pallas_call(
        body, name=name, grid=grid, in_specs=specs, out_specs=out_spec, out_shape=out_shape,
        scratch_shapes=[pltpu.VMEM(acc_shape, F32)] if nk > 1 else [],
        compiler_params=_cp(*sem))(*ins)


def _mm_nn(name, a, b, out_dtype, bias=None):
    M, K = a.shape
    N = b.shape[1]
    tn = _pick(N, (512, 256, 128))
    tk = K if K <= 1024 else _pick(K, (1408, 1024, 512, 256, 128))
    nk = K // tk
    return _matmul(
        name, a, b, dims=_NN, grid=(N // tn, nk),
        a_spec=pl.BlockSpec((M, tk), lambda j, k: (0, k)),
        b_spec=pl.BlockSpec((tk, tn), lambda j, k: (k, j)),
        out_shape=jax.ShapeDtypeStruct((M, N), out_dtype),
        out_spec=pl.BlockSpec((M, tn), lambda j, k: (0, j)),
        acc_shape=(M, tn), nk=nk,
        bias=bias, bias_spec=None if bias is None else pl.BlockSpec((1, tn), lambda j, k: (0, j)))


def _mm_nt(name, a, b, out_dtype, add=None, add_scale=1.0):
    M, K = a.shape
    N = b.shape[0]
    tn = _pick(N, (512, 256, 128))
    tk = K if K <= 1024 else _pick(K, (1408, 1024, 512, 256, 128))
    nk = K // tk
    return _matmul(
        name, a, b, dims=_NT, grid=(N // tn, nk),
        a_spec=pl.BlockSpec((M, tk), lambda j, k: (0, k)),
        b_spec=pl.BlockSpec((tn, tk), lambda j, k: (j, k)),
        out_shape=jax.ShapeDtypeStruct((M, N), out_dtype),
        out_spec=pl.BlockSpec((M, tn), lambda j, k: (0, j)),
        acc_shape=(M, tn), nk=nk,
        add=add, add_spec=None if add is None else pl.BlockSpec((M, tn), lambda j, k: (0, j)),
        add_scale=add_scale)


def _mm_tn(name, a, b, out_dtype):
    K, M = a.shape
    N = b.shape[1]
    tm = _pick(M, (256, 128))
    return _matmul(
        name, a, b, dims=_TN, grid=(M // tm,),
        a_spec=pl.BlockSpec((K, tm), lambda i: (0, i)),
        b_spec=pl.BlockSpec((K, N), lambda i: (0, 0)),
        out_shape=jax.ShapeDtypeStruct((M, N), out_dtype),
        out_spec=pl.BlockSpec((tm, N), lambda i: (i, 0)),
        acc_shape=(tm, N), nk=1)


def _proj_in(x_bf, w_in, b_in):
    T, D = x_bf.shape
    tn = _pick(D, (512, 256, 128))
    return _matmul(
        "proj_in", x_bf, w_in, dims=_NN, grid=(N_SEC, D // tn),
        a_spec=pl.BlockSpec((T, D), lambda s, j: (0, 0)),
        b_spec=pl.BlockSpec((None, D, tn), lambda s, j: (s, 0, j)),
        out_shape=jax.ShapeDtypeStruct((N_SEC, T, D), F32),
        out_spec=pl.BlockSpec((None, T, tn), lambda s, j: (s, 0, j)),
        acc_shape=(T, tn), nk=1,
        bias=b_in, bias_spec=pl.BlockSpec((None, 1, tn), lambda s, j: (s, 0, j)))


def _proj_in_dx(dh, w_in, add, add_scale):
    _, T, D = dh.shape
    tn = _pick(D, (512, 256, 128))
    return _matmul(
        "proj_in_dx", dh, w_in, dims=_NT, grid=(D // tn, N_SEC),
        a_spec=pl.BlockSpec((None, T, D), lambda j, s: (s, 0, 0)),
        b_spec=pl.BlockSpec((None, tn, D), lambda j, s: (s, j, 0)),
        out_shape=jax.ShapeDtypeStruct((T, D), F32),
        out_spec=pl.BlockSpec((T, tn), lambda j, s: (0, j)),
        acc_shape=(T, tn), nk=N_SEC,
        add=add, add_spec=pl.BlockSpec((T, tn), lambda j, s: (0, j)), add_scale=add_scale)


def _proj_in_dw(x_bf, dh):
    _, T, D = dh.shape
    tn = _pick(D, (512, 256, 128))
    return _matmul(
        "proj_in_dw", x_bf, dh, dims=_TN, grid=(N_SEC, D // tn),
        a_spec=pl.BlockSpec((T, D), lambda s, j: (0, 0)),
        b_spec=pl.BlockSpec((None, T, tn), lambda s, j: (s, 0, j)),
        out_shape=jax.ShapeDtypeStruct((N_SEC, D, D), ACT_DTYPE),
        out_spec=pl.BlockSpec((None, D, tn), lambda s, j: (s, 0, j)),
        acc_shape=(D, tn), nk=1)


def _ffn_up(x_bf, w_up):
    T, D = x_bf.shape
    F = w_up.shape[1] // 2
    tn = _pick(F, (256, 128))
    nb = F // tn
    return _matmul(
        "ffn_up", x_bf, w_up, dims=_NN, grid=(2, nb),
        a_spec=pl.BlockSpec((T, D), lambda p, j: (0, 0)),
        b_spec=pl.BlockSpec((D, tn), lambda p, j: (0, p * nb + j)),
        out_shape=jax.ShapeDtypeStruct((2, T, F), F32),
        out_spec=pl.BlockSpec((None, T, tn), lambda p, j: (p, 0, j)),
        acc_shape=(T, tn), nk=1)


def _ffn_up_dx(dup, w_up, add, add_scale):
    _, T, F = dup.shape
    D = w_up.shape[0]
    tn = _pick(D, (512, 256, 128))
    tk = _pick(F, (256, 128))
    nb = F // tk
    return _matmul(
        "ffn_up_dx", dup, w_up, dims=_NT, grid=(D // tn, 2 * nb),
        a_spec=pl.BlockSpec((None, T, tk), lambda j, k: (k // nb, 0, k % nb)),
        b_spec=pl.BlockSpec((tn, tk), lambda j, k: (j, k)),
        out_shape=jax.ShapeDtypeStruct((T, D), F32),
        out_spec=pl.BlockSpec((T, tn), lambda j, k: (0, j)),
        acc_shape=(T, tn), nk=2 * nb,
        add=add, add_spec=pl.BlockSpec((T, tn), lambda j, k: (0, j)), add_scale=add_scale)


def _ffn_up_dw(x_bf, dup):
    _, T, F = dup.shape
    D = x_bf.shape[1]
    tn = _pick(F, (256, 128))
    nb = F // tn
    return _matmul(
        "ffn_up_dw", x_bf, dup, dims=_TN, grid=(2, nb),
        a_spec=pl.BlockSpec((T, D), lambda p, j: (0, 0)),
        b_spec=pl.BlockSpec((None, T, tn), lambda p, j: (p, 0, j)),
        out_shape=jax.ShapeDtypeStruct((D, 2 * F), ACT_DTYPE),
        out_spec=pl.BlockSpec((D, tn), lambda p, j: (0, p * nb + j)),
        acc_shape=(D, tn), nk=1)


def _ln_fwd(name, a, res, alpha, g, b, dep=None):
    T, D = a.shape
    tr = _pick(T, (256, 128, 64, 32, 16))
    has_res = res is not None

    def body(*refs):
        if has_res:
            a_ref, r_ref, g_ref, b_ref = refs[:4]
            y_ref, yb_ref, z_ref = refs[-3:]
            z = alpha * a_ref[...] + r_ref[...]
            z_ref[...] = z
        else:
            a_ref, g_ref, b_ref = refs[:3]
            y_ref, yb_ref = refs[-2:]
            z = a_ref[...]
        mu = jnp.mean(z, axis=-1, keepdims=True)
        zc = z - mu
        var = jnp.mean(zc * zc, axis=-1, keepdims=True)
        y = zc * lax.rsqrt(var + LN_EPS) * g_ref[...] + b_ref[...]
        y_ref[...] = y
        yb_ref[...] = y.astype(ACT_DTYPE)

    row = pl.BlockSpec((tr, D), lambda i: (i, 0))
    vec = pl.BlockSpec((1, D), lambda i: (0, 0))
    ins = [a] + ([res] if has_res else []) + [g.reshape(1, D), b.reshape(1, D)]
    in_specs = [row] + ([row] if has_res else []) + [vec, vec]
    if dep is not None:
        ins.append(dep)
        in_specs.append(_ANY)
    out_shape = [jax.ShapeDtypeStruct((T, D), F32), jax.ShapeDtypeStruct((T, D), ACT_DTYPE)]
    if has_res:
        out_shape.append(jax.ShapeDtypeStruct((T, D), F32))
    return pl.pallas_call(
        body, name=name, grid=(T // tr,), in_specs=in_specs,
        out_specs=[row] * len(out_shape), out_shape=out_shape, compiler_params=_cp("parallel"))(*ins)


def _ln_bwd(name, z, dy, g, dep=None):
    T, D = z.shape
    tr = _pick(T, (256, 128, 64, 32, 16))

    def body(z_ref, dy_ref, g_ref, *rest):
        dz_ref, dzb_ref, dg_ref, db_ref = rest[-4:]

        @pl.when(pl.program_id(0) == 0)
        def _():
            dg_ref[...] = jnp.zeros_like(dg_ref)
            db_ref[...] = jnp.zeros_like(db_ref)

        zv = z_ref[...]
        dy_ = dy_ref[...]
        mu = jnp.mean(zv, axis=-1, keepdims=True)
        zc = zv - mu
        rstd = lax.rsqrt(jnp.mean(zc * zc, axis=-1, keepdims=True) + LN_EPS)
        xhat = zc * rstd
        dxh = dy_ * g_ref[...]
        dz = rstd * (dxh - jnp.mean(dxh, axis=-1, keepdims=True)
                     - xhat * jnp.mean(dxh * xhat, axis=-1, keepdims=True))
        dz_ref[...] = dz
        dzb_ref[...] = dz.astype(ACT_DTYPE)
        dg_ref[...] += jnp.sum(dy_ * xhat, axis=0, keepdims=True)
        db_ref[...] += jnp.sum(dy_, axis=0, keepdims=True)

    row = pl.BlockSpec((tr, D), lambda i: (i, 0))
    vec = pl.BlockSpec((1, D), lambda i: (0, 0))
    ins, in_specs = [z, dy, g.reshape(1, D)], [row, row, vec]
    if dep is not None:
        ins.append(dep)
        in_specs.append(_ANY)
    return pl.pallas_call(
        body, name=name, grid=(T // tr,), in_specs=in_specs, out_specs=[row, row, vec, vec],
        out_shape=[jax.ShapeDtypeStruct((T, D), F32), jax.ShapeDtypeStruct((T, D), ACT_DTYPE),
                   jax.ShapeDtypeStruct((1, D), F32), jax.ShapeDtypeStruct((1, D), F32)],
        compiler_params=_cp("arbitrary"))(*ins)


def _loss_fwd_bwd(y, target):
    T, D = y.shape
    tr = _pick(T, (256, 128, 64, 32, 16))

    def body(y_ref, t_ref, dy_ref, l_ref):
        @pl.when(pl.program_id(0) == 0)
        def _():
            l_ref[...] = jnp.zeros_like(l_ref)

        e = y_ref[...] - t_ref[...]
        dy_ref[...] = e * (1.0 / D)
        row = jnp.sum(e * e, axis=-1, keepdims=True) * (1.0 / D)
        l_ref[...] += 0.5 * jnp.sum(row, axis=0, keepdims=True)

    rowspec = pl.BlockSpec((tr, D), lambda i: (i, 0))
    return pl.pallas_call(
        body, name="loss", grid=(T // tr,), in_specs=[rowspec, rowspec],
        out_specs=[rowspec, pl.BlockSpec((1, LANES), lambda i: (0, 0))],
        out_shape=[jax.ShapeDtypeStruct((T, D), F32), jax.ShapeDtypeStruct((1, LANES), F32)],
        compiler_params=_cp("arbitrary"))(y, target)


def _gate_fwd(y_h, y_c, h):
    T, D = y_h.shape
    tr = _pick(T, (256, 128, 64, 32, 16))

    def body(yh_ref, yc_ref, gh_ref, gc_ref, m_ref):
        m = jax.nn.sigmoid(gh_ref[...]) * yh_ref[...] + jax.nn.sigmoid(gc_ref[...]) * yc_ref[...]
        m_ref[...] = m.astype(ACT_DTYPE)

    row = pl.BlockSpec((tr, D), lambda i: (i, 0))
    return pl.pallas_call(
        body, name="gate_fwd", grid=(T // tr,),
        in_specs=[row, row, pl.BlockSpec((None, tr, D), lambda i: (6, i, 0)),
                  pl.BlockSpec((None, tr, D), lambda i: (7, i, 0))],
        out_specs=row, out_shape=jax.ShapeDtypeStruct((T, D), ACT_DTYPE),
        compiler_params=_cp("parallel"))(y_h, y_c, h, h)


def _gate_bwd(dm, y_h, y_c, h):
    T, D = y_h.shape
    tr = _pick(T, (256, 128, 64, 32, 16))

    def body(dm_ref, yh_ref, yc_ref, gh_ref, gc_ref, dyh_ref, dyc_ref, dbb_ref, dh_ref):
        @pl.when(pl.program_id(0) == 0)
        def _():
            dbb_ref[...] = jnp.zeros_like(dbb_ref)

        dm_ = dm_ref[...]
        sh = jax.nn.sigmoid(gh_ref[...])
        sc = jax.nn.sigmoid(gc_ref[...])
        dyc = dm_ * sc
        dyh_ref[...] = (dm_ * sh).astype(ACT_DTYPE)
        dyc_ref[...] = dyc.astype(ACT_DTYPE)
        dbb_ref[...] += jnp.sum(dyc, axis=0, keepdims=True)
        dh_ref[0] = (dm_ * yh_ref[...] * sh * (1.0 - sh)).astype(ACT_DTYPE)
        dh_ref[1] = (dm_ * yc_ref[...] * sc * (1.0 - sc)).astype(ACT_DTYPE)

    row = pl.BlockSpec((tr, D), lambda i: (i, 0))
    return pl.pallas_call(
        body, name="gate_bwd", grid=(T // tr,),
        in_specs=[row, row, row, pl.BlockSpec((None, tr, D), lambda i: (6, i, 0)),
                  pl.BlockSpec((None, tr, D), lambda i: (7, i, 0))],
        out_specs=[row, row, pl.BlockSpec((1, D), lambda i: (0, 0)),
                   pl.BlockSpec((2, tr, D), lambda i: (3, i, 0))],
        out_shape=[jax.ShapeDtypeStruct((T, D), ACT_DTYPE), jax.ShapeDtypeStruct((T, D), ACT_DTYPE),
                   jax.ShapeDtypeStruct((1, D), F32), jax.ShapeDtypeStruct((N_SEC, T, D), ACT_DTYPE)],
        compiler_params=_cp("arbitrary"))(dm, y_h, y_c, h, h)


def _swiglu_fwd(up):
    _, T, F = up.shape
    tr = _pick(T, (128, 64, 32, 16))

    def body(up_ref, act_ref):
        act_ref[...] = (_silu(up_ref[0]) * up_ref[1]).astype(ACT_DTYPE)

    return pl.pallas_call(
        body, name="swiglu_fwd", grid=(T // tr,),
        in_specs=[pl.BlockSpec((2, tr, F), lambda i: (0, i, 0))],
        out_specs=pl.BlockSpec((tr, F), lambda i: (i, 0)),
        out_shape=jax.ShapeDtypeStruct((T, F), ACT_DTYPE), compiler_params=_cp("parallel"))(up)


def _swiglu_bwd(dact, up):
    _, T, F = up.shape
    tr = _pick(T, (128, 64, 32, 16))

    def body(da_ref, up_ref, dup_ref):
        da = da_ref[...]
        ug = up_ref[0]
        dup_ref[0] = (da * up_ref[1] * _dsilu(ug)).astype(ACT_DTYPE)
        dup_ref[1] = (da * _silu(ug)).astype(ACT_DTYPE)

    blk = pl.BlockSpec((2, tr, F), lambda i: (0, i, 0))
    return pl.pallas_call(
        body, name="swiglu_bwd", grid=(T // tr,),
        in_specs=[pl.BlockSpec((tr, F), lambda i: (i, 0)), blk], out_specs=blk,
        out_shape=jax.ShapeDtypeStruct((2, T, F), ACT_DTYPE), compiler_params=_cp("parallel"))(dact, up)


def _colsum(dh):
    S, T, D = dh.shape
    tr = _pick(T, (512, 256, 128, 64, 32, 16))

    def body(x_ref, o_ref):
        @pl.when(pl.program_id(1) == 0)
        def _():
            o_ref[...] = jnp.zeros_like(o_ref)

        o_ref[...] += jnp.sum(x_ref[...].astype(F32), axis=0, keepdims=True)

    return pl.pallas_call(
        body, name="colsum", grid=(S, T // tr),
        in_specs=[pl.BlockSpec((None, tr, D), lambda s, i: (s, i, 0))],
        out_specs=pl.BlockSpec((None, 1, D), lambda s, i: (s, 0, 0)),
        out_shape=jax.ShapeDtypeStruct((S, 1, D), F32), compiler_params=_cp("parallel", "arbitrary"))(dh)


def _lb_softmax(x):
    L = x.shape[0]
    rows = [x[l:l + 1] for l in range(L)]
    m = rows[0]
    for r in rows[1:]:
        m = jnp.maximum(m, r)
    e = [jnp.exp(r - m) for r in rows]
    s = e[0]
    for r in e[1:]:
        s = s + r
    return [r / s for r in e]


def _lb_fwd(lb_logits):
    L, D = lb_logits.shape

    def body(x_ref, o_ref):
        p = _lb_softmax(x_ref[...])
        run = jnp.zeros_like(p[0])
        for l in range(L):
            if l > 0:
                run = run + p[l]
            o_ref[pl.ds(l, 1), :] = run

    return pl.pallas_call(body, name="lb_fwd", out_shape=jax.ShapeDtypeStruct((L, D), F32))(lb_logits)


def _lb_bwd(lb_logits, dlbs):
    L, D = lb_logits.shape

    def body(x_ref, d_ref, o_ref):
        p = _lb_softmax(x_ref[...])
        d = d_ref[...]
        dp = [jnp.zeros_like(p[0]) for _ in range(L)]
        run = jnp.zeros_like(p[0])
        for j in range(L - 1, 0, -1):
            run = run + d[j:j + 1]
            dp[j] = run
        dot = dp[0] * p[0]
        for j in range(1, L):
            dot = dot + dp[j] * p[j]
        for j in range(L):
            o_ref[pl.ds(j, 1), :] = p[j] * (dp[j] - dot)

    return pl.pallas_call(body, name="lb_bwd", out_shape=jax.ShapeDtypeStruct((L, D), F32))(lb_logits, dlbs)


def _blk_cumsum(x, c, reverse=False):
    n = x.shape[0]
    pos = lax.broadcasted_iota(jnp.int32, x.shape, 0) % c
    s = 1
    while s < c:
        if reverse:
            shifted = pltpu.roll(x, n - s, 0)
            x = x + jnp.where(pos + s < c, shifted, 0.0)
        else:
            shifted = pltpu.roll(x, s, 0)
            x = x + jnp.where(pos >= s, shifted, 0.0)
        s *= 2
    return x


def _hgrn_prologue(q_ref, f_ref, lb_ref):
    lbv = lb_ref[...]
    z = f_ref[...]
    sig = jax.nn.sigmoid(z)
    one_m = 1.0 - lbv
    f = lbv + one_m * sig
    logf = jnp.log(jnp.maximum(f, F_MIN))
    k = one_m * jax.nn.sigmoid(-z)
    q = _silu(q_ref[...])
    return q, k, logf, f, sig, one_m


def _hgrn_fwd(h, lbs_l, gw):
    _, T, D = h.shape
    nh = D // LANES
    c = HG_C
    Tt = _pick(T, (128, 64, 32, 16))
    nb = Tt // c

    def body(q_ref, f_ref, i_ref, g_ref, lb_ref, gw_ref, o_ref, y_ref, sall_ref,
             st_ref, G_s, q_s, k_s, W_s, o_s):
        @pl.when(pl.program_id(1) == 0)
        def _():
            st_ref[...] = jnp.zeros_like(st_ref)

        q, k, logf, _, _, _ = _hgrn_prologue(q_ref, f_ref, lb_ref)
        G_s[...] = _blk_cumsum(logf, c)
        q_s[...] = q
        k_s[...] = k
        ones = jnp.ones((LANES, LANES), MXU_DTYPE)
        rowid = lax.broadcasted_iota(jnp.int32, (c, LANES), 0)
        for bi in range(nb):
            r0 = bi * c
            Gb = G_s[pl.ds(r0, c), :]
            qb = q_s[pl.ds(r0, c), :]
            kb = k_s[pl.ds(r0, c), :]
            vb = i_ref[pl.ds(r0, c), :]
            glast = G_s[pl.ds(r0 + c - 1, 1), :]
            st = st_ref[...]
            sall_ref[bi] = st
            qd = qb * jnp.exp(Gb)
            o_s[pl.ds(r0, c), :] = lax.dot_general(qd.astype(MXU_DTYPE), st.astype(MXU_DTYPE), _NT,
                                                   preferred_element_type=F32)
            kd = kb * jnp.exp(glast - Gb)
            st_ref[...] = st * jnp.exp(glast) + lax.dot_general(
                vb.astype(MXU_DTYPE), kd.astype(MXU_DTYPE), _TN, preferred_element_type=F32)
            for t in range(c):
                gt = G_s[pl.ds(r0 + t, 1), :]
                qt = q_s[pl.ds(r0 + t, 1), :]
                e = jnp.where(rowid <= t, jnp.exp(jnp.minimum(gt - Gb, 0.0)), 0.0)
                W_s[pl.ds(t * c, c), :] = (e * kb * qt).astype(MXU_DTYPE)
            r = jnp.dot(W_s[...], ones, preferred_element_type=F32)
            for t in range(c):
                o_s[pl.ds(r0 + t, 1), :] += jnp.sum(r[t * c:(t + 1) * c] * vb, axis=0, keepdims=True)
        o = o_s[...]
        n = o * lax.rsqrt(jnp.mean(o * o, axis=-1, keepdims=True) + RMS_EPS)
        o_ref[...] = o
        y_ref[...] = (n * gw_ref[...] * _silu(g_ref[...])).astype(ACT_DTYPE)

    def sec(s):
        return pl.BlockSpec((None, Tt, LANES), lambda hd, i: (s, i, hd))

    col = pl.BlockSpec((Tt, LANES), lambda hd, i: (i, hd))
    return pl.pallas_call(
        body, name="hgrn_fwd", grid=(nh, T // Tt),
        in_specs=[sec(0), sec(1), sec(2), sec(3), pl.BlockSpec((1, LANES), lambda hd, i: (0, hd)),
                  pl.BlockSpec((1, LANES), lambda hd, i: (0, 0))],
        out_specs=[col, col, pl.BlockSpec((nb, None, LANES, LANES), lambda hd, i: (i, hd, 0, 0))],
        out_shape=[jax.ShapeDtypeStruct((T, D), F32), jax.ShapeDtypeStruct((T, D), ACT_DTYPE),
                   jax.ShapeDtypeStruct((T // c, nh, LANES, LANES), F32)],
        scratch_shapes=[pltpu.VMEM((LANES, LANES), F32), pltpu.VMEM((Tt, LANES), F32),
                        pltpu.VMEM((Tt, LANES), F32), pltpu.VMEM((Tt, LANES), F32),
                        pltpu.VMEM((c * c, LANES), MXU_DTYPE), pltpu.VMEM((Tt, LANES), F32)],
        compiler_params=_cp("parallel", "arbitrary"))(h, h, h, h, lbs_l, gw)


def _hgrn_bwd(h, lbs_l, gw, o_pre, st_all, dy, dh):
    _, T, D = h.shape
    nh = D // LANES
    c = HG_C
    Tt = _pick(T, (128, 64, 32, 16))
    nb = Tt // c
    nT = T // Tt

    def body(q_ref, f_ref, i_ref, g_ref, lb_ref, gw_ref, o_ref, sall_ref, dy_ref, dh_in_ref,
             dh_ref, dlb_ref, dgw_ref,
             dst_ref, G_s, q_s, k_s, do_s, E_s, WP_s, dq_s, dk_s, dv_s, dG_s):
        del dh_in_ref
        hd, ti = pl.program_id(0), pl.program_id(1)

        @pl.when(ti == 0)
        def _():
            dst_ref[...] = jnp.zeros_like(dst_ref)
            dlb_ref[...] = jnp.zeros_like(dlb_ref)

        @pl.when((ti == 0) & (hd == 0))
        def _():
            dgw_ref[...] = jnp.zeros_like(dgw_ref)

        q, k, logf, f, sig, one_m = _hgrn_prologue(q_ref, f_ref, lb_ref)
        G_s[...] = _blk_cumsum(logf, c)
        q_s[...] = q
        k_s[...] = k

        o = o_ref[...]
        gr = g_ref[...]
        dy_ = dy_ref[...]
        rr = lax.rsqrt(jnp.mean(o * o, axis=-1, keepdims=True) + RMS_EPS)
        n = o * rr
        sg = _silu(gr)
        gwv = gw_ref[...]
        dh_ref[3] = (dy_ * n * gwv * _dsilu(gr)).astype(ACT_DTYPE)
        dgw_ref[...] += jnp.sum(dy_ * n * sg, axis=0, keepdims=True)
        dn = dy_ * gwv * sg
        do_s[...] = rr * (dn - n * jnp.mean(dn * n, axis=-1, keepdims=True))

        ones = jnp.ones((LANES, LANES), MXU_DTYPE)
        rowid = lax.broadcasted_iota(jnp.int32, (c, LANES), 0)
        cc = c * c
        for bi in range(nb - 1, -1, -1):
            r0 = bi * c
            Gb = G_s[pl.ds(r0, c), :]
            qb = q_s[pl.ds(r0, c), :]
            kb = k_s[pl.ds(r0, c), :]
            vb = i_ref[pl.ds(r0, c), :]
            dob = do_s[pl.ds(r0, c), :]
            glast = G_s[pl.ds(r0 + c - 1, 1), :]
            st = sall_ref[bi]
            dst = dst_ref[...]
            a = jnp.exp(glast)
            eg = jnp.exp(Gb)
            egl = jnp.exp(glast - Gb)
            qd = qb * eg
            kd = kb * egl
            dob_m = dob.astype(MXU_DTYPE)
            dst_m = dst.astype(MXU_DTYPE)
            dqd = lax.dot_general(dob_m, st.astype(MXU_DTYPE), _NN, preferred_element_type=F32)
            dkd = lax.dot_general(vb.astype(MXU_DTYPE), dst_m, _NN, preferred_element_type=F32)
            dv_inter = lax.dot_general(kd.astype(MXU_DTYPE), dst_m, _NT, preferred_element_type=F32)
            da = jnp.sum(dst * st, axis=0, keepdims=True)
            dst_ref[...] = dst * a + lax.dot_general(dob_m, qd.astype(MXU_DTYPE), _TN,
                                                     preferred_element_type=F32)
            for t in range(c):
                gt = G_s[pl.ds(r0 + t, 1), :]
                qt = q_s[pl.ds(r0 + t, 1), :]
                dot_ = do_s[pl.ds(r0 + t, 1), :]
                e = jnp.where(rowid <= t, jnp.exp(jnp.minimum(gt - Gb, 0.0)), 0.0)
                E_s[pl.ds(t * c, c), :] = e
                WP_s[pl.ds(t * c, c), :] = (e * kb * qt).astype(MXU_DTYPE)
                WP_s[pl.ds(cc + t * c, c), :] = (vb * dot_).astype(MXU_DTYPE)
            r = jnp.dot(WP_s[...], ones, preferred_element_type=F32)
            dk_i = jnp.zeros((c, LANES), F32)
            dv_i = jnp.zeros((c, LANES), F32)
            for t in range(c):
                qt = q_s[pl.ds(r0 + t, 1), :]
                dot_ = do_s[pl.ds(r0 + t, 1), :]
                e = E_s[pl.ds(t * c, c), :]
                a_t = r[t * c:(t + 1) * c]
                da_t = r[cc + t * c:cc + (t + 1) * c]
                dae = da_t * e
                dq_s[pl.ds(r0 + t, 1), :] = jnp.sum(dae * kb, axis=0, keepdims=True)
                dk_i = dk_i + dae * qt
                dv_i = dv_i + a_t * dot_
            dq_i = dq_s[pl.ds(r0, c), :]
            dq_s[pl.ds(r0, c), :] = dqd * eg + dq_i
            dk_s[pl.ds(r0, c), :] = dkd * egl + dk_i
            dv_s[pl.ds(r0, c), :] = dv_inter + dv_i
            dkdkd = dkd * kd
            dG = dqd * qd + qb * dq_i - kb * dk_i - dkdkd
            dglast = jnp.sum(dkdkd, axis=0, keepdims=True) + da * a
            dG_s[pl.ds(r0, c), :] = dG + jnp.where(rowid == c - 1, dglast, 0.0)

        dlogf = _blk_cumsum(dG_s[...], c, reverse=True)
        df = jnp.where(f > F_MIN, dlogf / f, 0.0)
        dk = dk_s[...]
        dh_ref[0] = (dq_s[...] * _dsilu(q_ref[...])).astype(ACT_DTYPE)
        dh_ref[1] = ((df - dk) * one_m * sig * (1.0 - sig)).astype(ACT_DTYPE)
        dh_ref[2] = dv_s[...].astype(ACT_DTYPE)
        dlb_ref[...] += jnp.sum((df - dk) * (1.0 - sig), axis=0, keepdims=True)

    def sec(s):
        return pl.BlockSpec((None, Tt, LANES), lambda hd, i: (s, nT - 1 - i, hd))

    col = pl.BlockSpec((Tt, LANES), lambda hd, i: (nT - 1 - i, hd))
    tile = pltpu.VMEM((Tt, LANES), F32)
    return pl.pallas_call(
        body, name="hgrn_bwd", grid=(nh, nT),
        in_specs=[sec(0), sec(1), sec(2), sec(3), pl.BlockSpec((1, LANES), lambda hd, i: (0, hd)),
                  pl.BlockSpec((1, LANES), lambda hd, i: (0, 0)), col,
                  pl.BlockSpec((nb, None, LANES, LANES), lambda hd, i: (nT - 1 - i, hd, 0, 0)), col,
                  pl.BlockSpec(memory_space=pl.ANY)],
        out_specs=[pl.BlockSpec((4, Tt, LANES), lambda hd, i: (0, nT - 1 - i, hd)),
                   pl.BlockSpec((1, LANES), lambda hd, i: (0, hd)),
                   pl.BlockSpec((1, LANES), lambda hd, i: (0, 0))],
        out_shape=[jax.ShapeDtypeStruct(dh.shape, dh.dtype), jax.ShapeDtypeStruct((1, D), F32),
                   jax.ShapeDtypeStruct((1, LANES), F32)],
        scratch_shapes=[pltpu.VMEM((LANES, LANES), F32), tile, tile, tile, tile,
                        pltpu.VMEM((c * c, LANES), F32), pltpu.VMEM((2 * c * c, LANES), MXU_DTYPE),
                        tile, tile, tile, tile],
        input_output_aliases={9: 0},
        compiler_params=_cp("arbitrary", "arbitrary"))(h, h, h, h, lbs_l, gw, o_pre, st_all, dy, dh)


def _conv_fwd(h, w_dw, b_dw, ln_g, ln_b):
    _, T, D = h.shape
    Tt = _pick(T, (256, 128, 64, 32))
    hb = Tt // CONV_HALO
    off = CONV_HALO - (CONV_WIDTH - 1)

    def body(a_ref, b_ref, ap_ref, bp_ref, w_ref, bd_ref, g_ref, be_ref, yc_ref, y_ref, U_s):
        first = pl.program_id(0) == 0
        up = ap_ref[...] * jax.nn.sigmoid(bp_ref[...])
        U_s[pl.ds(0, CONV_HALO), :] = jnp.where(first, 0.0, up)
        U_s[pl.ds(CONV_HALO, Tt), :] = a_ref[...] * jax.nn.sigmoid(b_ref[...])
        for cb in range(D // LANES):
            cs = pl.ds(cb * LANES, LANES)
            acc = jnp.zeros((Tt, LANES), F32)
            for j in range(CONV_WIDTH):
                acc = acc + w_ref[pl.ds(j, 1), cs] * U_s[pl.ds(off + j, Tt), cs]
            yc_ref[:, cs] = acc + bd_ref[:, cs]
        yc = yc_ref[...]
        mu = jnp.mean(yc, axis=-1, keepdims=True)
        zc = yc - mu
        var = jnp.mean(zc * zc, axis=-1, keepdims=True)
        ln = zc * lax.rsqrt(var + LN_EPS) * g_ref[...] + be_ref[...]
        y_ref[...] = _silu(ln).astype(ACT_DTYPE)

    def main(s):
        return pl.BlockSpec((None, Tt, D), lambda i: (s, i, 0))

    def prev(s):
        return pl.BlockSpec((None, CONV_HALO, D), lambda i: (s, jnp.maximum(i * hb - 1, 0), 0))

    row = pl.BlockSpec((Tt, D), lambda i: (i, 0))
    vec = pl.BlockSpec((1, D), lambda i: (0, 0))
    return pl.pallas_call(
        body, name="conv_fwd", grid=(T // Tt,),
        in_specs=[main(4), main(5), prev(4), prev(5), pl.BlockSpec((CONV_HALO, D), lambda i: (0, 0)),
                  vec, vec, vec],
        out_specs=[row, row],
        out_shape=[jax.ShapeDtypeStruct((T, D), F32), jax.ShapeDtypeStruct((T, D), ACT_DTYPE)],
        scratch_shapes=[pltpu.VMEM((CONV_HALO + Tt, D), F32)],
        compiler_params=_cp("parallel"))(h, h, h, h, w_dw, b_dw, ln_g, ln_b)


def _conv_bwd(h, w_dw, ln_g, ln_b, yc, dy, dh):
    _, T, D = h.shape
    Tt = _pick(T, (256, 128, 64, 32))
    hb = Tt // CONV_HALO
    nT = T // Tt
    nhb = T // CONV_HALO
    off = CONV_HALO - (CONV_WIDTH - 1)

    def body(a_ref, b_ref, ap_ref, bp_ref, w_ref, g_ref, be_ref, yc_ref, ycn_ref, dy_ref, dyn_ref, dh_in_ref,
             dh_ref, dw_ref, dbd_ref, dg_ref, dbe_ref, U_s, DY_s, du_s):
        del dh_in_ref
        i = pl.program_id(0)

        @pl.when(i == 0)
        def _():
            dw_ref[...] = jnp.zeros_like(dw_ref)
            dbd_ref[...] = jnp.zeros_like(dbd_ref)
            dg_ref[...] = jnp.zeros_like(dg_ref)
            dbe_ref[...] = jnp.zeros_like(dbe_ref)

        gv = g_ref[...]
        bev = be_ref[...]

        def ln_silu_bwd(ycv, dyv):
            mu = jnp.mean(ycv, axis=-1, keepdims=True)
            zc = ycv - mu
            rstd = lax.rsqrt(jnp.mean(zc * zc, axis=-1, keepdims=True) + LN_EPS)
            xhat = zc * rstd
            dln = dyv * _dsilu(xhat * gv + bev)
            dxh = dln * gv
            dyc = rstd * (dxh - jnp.mean(dxh, axis=-1, keepdims=True)
                          - xhat * jnp.mean(dxh * xhat, axis=-1, keepdims=True))
            return dyc, dln, xhat

        dyc, dln, xhat = ln_silu_bwd(yc_ref[...], dy_ref[...])
        dg_ref[...] += jnp.sum(dln * xhat, axis=0, keepdims=True)
        dbe_ref[...] += jnp.sum(dln, axis=0, keepdims=True)
        dbd_ref[...] += jnp.sum(dyc, axis=0, keepdims=True)
        DY_s[pl.ds(0, Tt), :] = dyc
        dycn, _, _ = ln_silu_bwd(ycn_ref[...], dyn_ref[...])
        DY_s[pl.ds(Tt, CONV_HALO), :] = jnp.where(i == nT - 1, 0.0, dycn)

        sb = jax.nn.sigmoid(b_ref[...])
        av = a_ref[...]
        up = ap_ref[...] * jax.nn.sigmoid(bp_ref[...])
        U_s[pl.ds(0, CONV_HALO), :] = jnp.where(i == 0, 0.0, up)
        U_s[pl.ds(CONV_HALO, Tt), :] = av * sb

        for cb in range(D // LANES):
            cs = pl.ds(cb * LANES, LANES)
            dyb = DY_s[pl.ds(0, Tt), cs]
            acc = jnp.zeros((Tt, LANES), F32)
            for j in range(CONV_WIDTH):
                acc = acc + w_ref[pl.ds(j, 1), cs] * DY_s[pl.ds(CONV_WIDTH - 1 - j, Tt), cs]
                dw_ref[pl.ds(j, 1), cs] += jnp.sum(dyb * U_s[pl.ds(off + j, Tt), cs], axis=0, keepdims=True)
            du_s[:, cs] = acc
        du = du_s[...]
        dh_ref[0] = (du * sb).astype(ACT_DTYPE)
        dh_ref[1] = (du * av * sb * (1.0 - sb)).astype(ACT_DTYPE)

    def main(s):
        return pl.BlockSpec((None, Tt, D), lambda i: (s, i, 0))

    def prev(s):
        return pl.BlockSpec((None, CONV_HALO, D), lambda i: (s, jnp.maximum(i * hb - 1, 0), 0))

    row = pl.BlockSpec((Tt, D), lambda i: (i, 0))
    nxt = pl.BlockSpec((CONV_HALO, D), lambda i: (jnp.minimum((i + 1) * hb, nhb - 1), 0))
    vec = pl.BlockSpec((1, D), lambda i: (0, 0))
    wspec = pl.BlockSpec((CONV_HALO, D), lambda i: (0, 0))
    return pl.pallas_call(
        body, name="conv_bwd", grid=(nT,),
        in_specs=[main(4), main(5), prev(4), prev(5), wspec, vec, vec, row, nxt, row, nxt,
                  pl.BlockSpec(memory_space=pl.ANY)],
        out_specs=[pl.BlockSpec((2, Tt, D), lambda i: (2, i, 0)), wspec, vec, vec, vec],
        out_shape=[jax.ShapeDtypeStruct(dh.shape, dh.dtype), jax.ShapeDtypeStruct((CONV_HALO, D), F32),
                   jax.ShapeDtypeStruct((1, D), F32), jax.ShapeDtypeStruct((1, D), F32),
                   jax.ShapeDtypeStruct((1, D), F32)],
        scratch_shapes=[pltpu.VMEM((CONV_HALO + Tt, D), F32), pltpu.VMEM((Tt + CONV_HALO, D), F32),
                        pltpu.VMEM((Tt, D), F32)],
        input_output_aliases={11: 0},
        compiler_params=_cp("arbitrary"))(h, h, h, h, w_dw, ln_g, ln_b, yc, yc, dy, dy, dh)


def _adamw(name, w, m, v, parts, part_specs, tr, prefetch=None, nsteps=None, row_map=None, prev=None):
    R, C = w.shape
    bc1 = 1.0 - ADAM_B1 ** ADAM_STEP
    bc2 = 1.0 - ADAM_B2 ** ADAM_STEP
    npart = len(parts)
    npre = 0 if prefetch is None else 1
    nprev = 0 if prev is None else 4

    def body(*refs):
        refs = refs[npre:]
        w_ref, m_ref, v_ref = refs[:3]
        p_refs = refs[3:3 + npart]
        g_ref, d_ref, mo_ref, vo_ref = refs[3 + npart + nprev:]
        g = p_refs[0][...].astype(F32)
        for p in p_refs[1:]:
            g = g + p[...].astype(F32)
        wv = w_ref[...]
        mn = ADAM_B1 * m_ref[...] + (1.0 - ADAM_B1) * g
        vn = ADAM_B2 * v_ref[...] + (1.0 - ADAM_B2) * (g * g)
        m_hat = mn / bc1
        v_hat = vn / bc2
        g_ref[...] = g
        d_ref[...] = -ADAM_LR * (m_hat / (jnp.sqrt(v_hat) + ADAM_EPS) + ADAM_WD * wv)
        mo_ref[...] = mn
        vo_ref[...] = vn

    if row_map is None:
        row_map = (lambda i: (i, 0)) if prefetch is None else (lambda i, s: (i, 0))
    row = pl.BlockSpec((tr, C), row_map)
    out = jax.ShapeDtypeStruct((R, C), F32)
    gs = pltpu.PrefetchScalarGridSpec(
        num_scalar_prefetch=npre, grid=(R // tr if nsteps is None else nsteps,),
        in_specs=[row, row, row] + list(part_specs) + [_ANY] * nprev, out_specs=[row] * 4)
    args = ([prefetch] if npre else []) + [w, m, v] + list(parts) + (list(prev) if nprev else [])
    first_prev = npre + 3 + npart
    return pl.pallas_call(body, name=name, grid_spec=gs, out_shape=[out] * 4,
                          input_output_aliases={first_prev + i: i for i in range(nprev)},
                          compiler_params=_cp("parallel"))(*args)


def _pair_add(p, r1, my_c):
    _, R, C = r1.shape
    tr = _pick(R, (512, 256, 128, 64, 32, 16))

    def body(c_ref, p_ref, r_ref, q_ref):
        del c_ref
        q_ref[...] = (p_ref[...].astype(F32) + r_ref[...].astype(F32)).astype(q_ref.dtype)

    gs = pltpu.PrefetchScalarGridSpec(
        num_scalar_prefetch=1, grid=(4, R // tr),
        in_specs=[pl.BlockSpec((None, tr, C), lambda j, i, c: (2 * j + c[0], i, 0)),
                  pl.BlockSpec((None, tr, C), lambda j, i, c: (j, i, 0))],
        out_specs=pl.BlockSpec((None, tr, C), lambda j, i, c: (j, i, 0)))
    return pl.pallas_call(body, name="pair_add", grid_spec=gs, out_shape=jax.ShapeDtypeStruct(r1.shape, r1.dtype),
                          compiler_params=_cp("parallel", "parallel"))(my_c, p, r1)


def _place():
    x, y, c = lax.axis_index("x"), lax.axis_index("y"), lax.axis_index("c")
    chips = [(1 - x, y), (x, 1 - y), (1 - x, 1 - y)]
    return x, y, c, chips


def _hbm(a):
    return pltpu.with_memory_space_constraint(a, pltpu.HBM)


def _gather_targets():
    x, y, c, chips = _place()
    return 4 * x + 2 * y + c, [(x, y, 1 - c)] + [(*chip, c) for chip in chips]


def _gather_start(shards, n_groups):
    n = len(shards)
    per = n // n_groups
    lands = [_hbm(lax.empty((N_DEV,) + s.shape, s.dtype)) for s in shards]

    def body(*refs):
        srcs, zones = refs[:n], refs[n:2 * n]
        send, recv = refs[2 * n:2 * n + n_groups], refs[2 * n + n_groups:2 * n + 2 * n_groups]
        token = refs[-1]
        mine, targets = _gather_targets()
        for i in range(n):
            g, a = divmod(i, per)
            for k, to in enumerate(targets):
                pltpu.make_async_remote_copy(
                    src_ref=srcs[i], dst_ref=zones[i].at[mine], send_sem=send[g].at[4 * a + k],
                    recv_sem=recv[g].at[4 * a + k], device_id=to, device_id_type=MESH).start()
        token[...] = jnp.zeros_like(token)

    sem = pltpu.SemaphoreType.DMA((4 * per,))
    out_shape = ([sem] * (2 * n_groups) + [pltpu.HBM(s.shape, s.dtype) for s in shards]
                 + [pltpu.HBM(z.shape, z.dtype) for z in lands] + [jax.ShapeDtypeStruct((8, LANES), F32)])
    outs = pl.pallas_call(
        body, name="gather_start", out_shape=out_shape, in_specs=[_HBM] * (2 * n),
        out_specs=[_SEM] * (2 * n_groups) + [_HBM] * (2 * n) + [pl.BlockSpec(memory_space=pltpu.VMEM)],
        input_output_aliases={i: 2 * n_groups + i for i in range(2 * n)},
        compiler_params=pltpu.CompilerParams(has_side_effects=_EFFECT))(*[_hbm(s) for s in shards], *lands)
    send, recv = outs[:n_groups], outs[n_groups:2 * n_groups]
    thru = outs[2 * n_groups:2 * n_groups + n]
    zones = outs[2 * n_groups + n:2 * n_groups + 2 * n]
    return send, recv, thru, zones, outs[-1]


def _gather_wait(name, shards, zones, send, recv, after):
    per = len(shards)

    def body(*refs):
        srcs, lz = refs[:per], refs[per:2 * per]
        send_s, recv_s = refs[2 * per], refs[2 * per + 1]
        mine, targets = _gather_targets()
        for a in range(per):
            for k, to in enumerate(targets):
                cp = pltpu.make_async_remote_copy(
                    src_ref=srcs[a], dst_ref=lz[a].at[mine], send_sem=send_s.at[4 * a + k],
                    recv_sem=recv_s.at[4 * a + k], device_id=to, device_id_type=MESH)
                cp.wait_send()
                cp.wait_recv()

    outs = pl.pallas_call(
        body, name=name, out_shape=[pltpu.HBM(s.shape, s.dtype) for s in shards + zones],
        in_specs=[_HBM] * (2 * per) + [_SEM, _SEM, _ANY], out_specs=[_HBM] * (2 * per),
        input_output_aliases={i: i for i in range(2 * per)},
        compiler_params=pltpu.CompilerParams(has_side_effects=_EFFECT))(*shards, *zones, send, recv, after)
    return outs[:per], outs[per:]


def _gather_finish(shards, zones):
    n = len(shards)

    def body(*refs):
        srcs, lz = refs[:n], refs[2 * n:3 * n]
        send_sems, recv_sems, local_sems = refs[3 * n:]
        x, y, c, chips = _place()

        def fwd(a, j, pc):
            cx, cy = chips[j]
            blk = lz[a].at[4 * cx + 2 * cy + pc]
            return pltpu.make_async_remote_copy(
                src_ref=blk, dst_ref=blk, send_sem=send_sems.at[3 * a + j], recv_sem=recv_sems.at[3 * a + j],
                device_id=(x, y, 1 - c), device_id_type=MESH)

        mine = [pltpu.make_async_copy(srcs[a], lz[a].at[4 * x + 2 * y + c], local_sems.at[a]) for a in range(n)]
        sends = [fwd(a, j, c) for a in range(n) for j in range(3)]
        for cp in mine + sends:
            cp.start()
        for a in range(n):
            for j in range(3):
                fwd(a, j, 1 - c).wait_recv()
        for cp in sends:
            cp.wait_send()
        for cp in mine:
            cp.wait()

    return pl.pallas_call(
        body, name="gather_finish", out_shape=[jax.ShapeDtypeStruct(z.shape, z.dtype) for z in zones],
        in_specs=[_ANY] * (2 * n), out_specs=[_ANY] * n, input_output_aliases={n + a: a for a in range(n)},
        scratch_shapes=[pltpu.SemaphoreType.DMA((3 * n,)), pltpu.SemaphoreType.DMA((3 * n,)),
                        pltpu.SemaphoreType.DMA((n,))])(*shards, *zones)


def _exchange_sibling(bufs):
    n_arr = len(bufs)

    def body(*refs):
        srcs, outs = refs[:n_arr], refs[n_arr:2 * n_arr]
        send_sems, recv_sems = refs[2 * n_arr:]
        x, y, c, _ = _place()
        copies = []
        for n in range(n_arr):
            for j in range(4):
                copies.append(pltpu.make_async_remote_copy(
                    src_ref=srcs[n].at[2 * j + 1 - c], dst_ref=outs[n].at[j],
                    send_sem=send_sems.at[4 * n + j], recv_sem=recv_sems.at[4 * n + j],
                    device_id=(x, y, 1 - c), device_id_type=MESH))
        for cp in copies:
            cp.start()
        for cp in copies:
            cp.wait()

    return pl.pallas_call(
        body, name="exchange_sibling",
        out_shape=[jax.ShapeDtypeStruct((4,) + b.shape[1:], b.dtype) for b in bufs],
        in_specs=[_ANY] * n_arr, out_specs=[_ANY] * n_arr,
        scratch_shapes=[pltpu.SemaphoreType.DMA((4 * n_arr,)), pltpu.SemaphoreType.DMA((4 * n_arr,))])(*bufs)


def _chip_copies(srcs, zones, send, recv):
    _, _, c, chips = _place()
    return [pltpu.make_async_remote_copy(
        src_ref=srcs[n].at[2 * cx + cy], dst_ref=zones[n].at[k], send_sem=send.at[3 * n + k],
        recv_sem=recv.at[3 * n + k], device_id=(cx, cy, c), device_id_type=MESH)
        for n in range(len(srcs)) for k, (cx, cy) in enumerate(chips)]


def _exchange_chips_start(name, bufs):
    n = len(bufs)
    lands = [_hbm(lax.empty((3,) + b.shape[1:], b.dtype)) for b in bufs]

    def body(*refs):
        srcs, zones = refs[:n], refs[n:2 * n]
        send, recv, token = refs[2 * n], refs[2 * n + 1], refs[-1]
        for cp in _chip_copies(srcs, zones, send, recv):
            cp.start()
        token[...] = jnp.zeros_like(token)

    sem = pltpu.SemaphoreType.DMA((3 * n,))
    outs = pl.pallas_call(
        body, name=name,
        out_shape=[sem, sem] + [pltpu.HBM(b.shape, b.dtype) for b in bufs]
        + [pltpu.HBM(z.shape, z.dtype) for z in lands] + [jax.ShapeDtypeStruct((8, LANES), F32)],
        in_specs=[_HBM] * (2 * n),
        out_specs=[_SEM, _SEM] + [_HBM] * (2 * n) + [pl.BlockSpec(memory_space=pltpu.VMEM)],
        input_output_aliases={i: 2 + i for i in range(2 * n)},
        compiler_params=pltpu.CompilerParams(has_side_effects=_EFFECT))(*[_hbm(b) for b in bufs], *lands)
    return outs[0], outs[1], outs[2:2 + n], outs[2 + n:2 + 2 * n], outs[-1]


def _exchange_chips_wait(name, bufs, zones, send, recv, after):
    n = len(bufs)

    def body(*refs):
        for cp in _chip_copies(refs[:n], refs[n:2 * n], refs[2 * n], refs[2 * n + 1]):
            cp.wait_send()
            cp.wait_recv()

    outs = pl.pallas_call(
        body, name=name, out_shape=[pltpu.HBM(a.shape, a.dtype) for a in list(bufs) + list(zones)],
        in_specs=[_HBM] * (2 * n) + [_SEM, _SEM, _ANY], out_specs=[_HBM] * (2 * n),
        input_output_aliases={i: i for i in range(2 * n)},
        compiler_params=pltpu.CompilerParams(has_side_effects=_EFFECT))(*bufs, *zones, send, recv, after)
    return outs[n:]


def _all_gather_small(part):
    def body(src, out, send_sems, recv_sems, local_sem):
        x, y, c, _ = _place()
        mine = pltpu.make_async_copy(src, out.at[4 * x + 2 * y + c], local_sem)
        mine.start()
        copies = []
        for r in range(1, N_DEV):
            dx, dy, dc = (r >> 2) & 1, (r >> 1) & 1, r & 1
            peer = (1 - x if dx else x, 1 - y if dy else y, 1 - c if dc else c)
            copies.append(pltpu.make_async_remote_copy(
                src_ref=src, dst_ref=out.at[4 * x + 2 * y + c],
                send_sem=send_sems.at[r - 1], recv_sem=recv_sems.at[r - 1],
                device_id=peer, device_id_type=MESH))
        for cp in copies:
            cp.start()
        for cp in copies:
            cp.wait()
        mine.wait()

    return pl.pallas_call(
        body, name="all_gather_small",
        out_shape=jax.ShapeDtypeStruct((N_DEV,) + part.shape, part.dtype),
        in_specs=[_ANY], out_specs=_ANY,
        scratch_shapes=[pltpu.SemaphoreType.DMA((N_DEV - 1,)), pltpu.SemaphoreType.DMA((N_DEV - 1,)),
                        pltpu.SemaphoreType.DMA])(part)


def _layer_fwd(xin, xin_bf, W, P, alpha):
    h = _proj_in(xin_bf, W["w_in"], P["b_in"])
    o_pre, y_hg, st_all = _hgrn_fwd(h, P["lbs"], P["g_norm_w"])
    yc_pre, y_cv = _conv_fwd(h, P["w_dw"], P["b_dw"], P["conv_ln_g"], P["conv_ln_b"])
    y_h = _mm_nn("branch_a", y_hg, W["w_a"], F32)
    y_c = _mm_nn("branch_b", y_cv, W["w_b"], F32, bias=P["b_b"])
    merged = _gate_fwd(y_h, y_c, h)
    mix = _mm_nn("mix_out", merged, W["w_o"], F32)
    x1, x1_bf, z1 = _ln_fwd("ln1", xin, mix, alpha, P["ln1_g"], P["ln1_b"])
    up = _ffn_up(x1_bf, W["w_up"])
    act = _swiglu_fwd(up)
    ffn = _mm_nn("ffn_down", act, W["w_down"], F32)
    x2, x2_bf, z2 = _ln_fwd("ln2", x1, ffn, alpha, P["ln2_g"], P["ln2_b"])
    saved = dict(xin_bf=xin_bf, h=h, o_pre=o_pre, y_hg=y_hg, st_all=st_all, yc_pre=yc_pre, y_cv=y_cv,
                 y_h=y_h, y_c=y_c, merged=merged, z1=z1, x1_bf=x1_bf, up=up, act=act, z2=z2)
    return x2, x2_bf, saved


def _layer_bwd(dx2, S, W, P, alpha, dep=None):
    dz2, dz2_bf, dln2_g, dln2_b = _ln_bwd("ln2_bwd", S["z2"], dx2, P["ln2_g"], dep=dep)
    dact = _mm_nt("ffn_down_dx", dz2_bf, W["w_down"], F32)
    dw_down = _mm_tn("ffn_down_dw", S["act"], dz2_bf, ACT_DTYPE)
    dup = _swiglu_bwd(dact, S["up"])
    dx1 = _ffn_up_dx(dup, W["w_up"], dz2, alpha)
    dw_up = _ffn_up_dw(S["x1_bf"], dup)
    dz1, dz1_bf, dln1_g, dln1_b = _ln_bwd("ln1_bwd", S["z1"], dx1, P["ln1_g"])
    dmerged = _mm_nt("mix_out_dx", dz1_bf, W["w_o"], F32)
    dw_o = _mm_tn("mix_out_dw", S["merged"], dz1_bf, ACT_DTYPE)
    dy_h, dy_c, db_b, dh = _gate_bwd(dmerged, S["y_h"], S["y_c"], S["h"])
    dy_cv = _mm_nt("branch_b_dx", dy_c, W["w_b"], F32)
    dw_b = _mm_tn("branch_b_dw", S["y_cv"], dy_c, ACT_DTYPE)
    dy_hg = _mm_nt("branch_a_dx", dy_h, W["w_a"], F32)
    dw_a = _mm_tn("branch_a_dw", S["y_hg"], dy_h, ACT_DTYPE)
    dh, dw_dw, db_dw, dcln_g, dcln_b = _conv_bwd(S["h"], P["w_dw"], P["conv_ln_g"], P["conv_ln_b"],
                                                 S["yc_pre"], dy_cv, dh)
    dh, dlbs, dgw = _hgrn_bwd(S["h"], P["lbs"], P["g_norm_w"], S["o_pre"], S["st_all"], dy_hg, dh)
    dxin = _proj_in_dx(dh, W["w_in"], dz1, alpha)
    dw_in = _proj_in_dw(S["xin_bf"], dh)
    db_in = _colsum(dh)
    big = dict(w_in=dw_in, w_a=dw_a, w_b=dw_b, w_o=dw_o, w_down=dw_down, w_up=dw_up)
    small = dict(b_in=db_in, lbs=dlbs, g_norm_w=dgw, w_dw=dw_dw, b_dw=db_dw, conv_ln_g=dcln_g,
                 conv_ln_b=dcln_b, b_b=db_b, ln1_g=dln1_g, ln1_b=dln1_b, ln2_g=dln2_g, ln2_b=dln2_b)
    return dxin, big, small


_SMALL = ("b_in", "lb_logits", "g_norm_w", "b_dw", "conv_ln_g", "conv_ln_b", "b_b", "ln1_g", "ln1_b", "ln2_g",
          "ln2_b")


def _pack_small(per_layer, ln0_g, ln0_b, extra_row, D, L):
    rows = []
    for l in range(L):
        for n in _SMALL:
            a = per_layer[n][l]
            if n == "b_in":
                rows.append(a.reshape(N_SEC, D))
            elif n == "g_norm_w":
                rows.append(jnp.pad(a.reshape(1, -1), ((0, 0), (0, D - a.size))))
            else:
                rows.append(a.reshape(1, D))
    rows += [ln0_g.reshape(1, D), ln0_b.reshape(1, D), extra_row]
    buf = jnp.concatenate(rows, axis=0)
    pad = (-buf.shape[0]) % 8
    return jnp.pad(buf, ((0, pad), (0, 0)))


def _unpack_small(buf, D, L, hv):
    out = {n: [] for n in _SMALL}
    r = 0
    for l in range(L):
        for n in _SMALL:
            if n == "b_in":
                out[n].append(buf[r:r + N_SEC].reshape(N_SEC * D))
                r += N_SEC
            elif n == "g_norm_w":
                out[n].append(buf[r, :hv])
                r += 1
            else:
                out[n].append(buf[r])
                r += 1
    res = {n: jnp.stack(v) for n, v in out.items()}
    res["ln0_g"] = buf[r]
    res["ln0_b"] = buf[r + 1]
    return res, r + 2


def kernel(x, ln0_g, ln0_b, w_in, b_in, lb_logits, g_norm_w, w_a, w_dw, b_dw, conv_ln_g, conv_ln_b, w_b, b_b, w_o, ln1_g, ln1_b, w_up, w_down, ln2_g, ln2_b, loss_target, m_ln0_g, m_ln0_b, m_w_in, m_b_in, m_lb_logits, m_g_norm_w, m_w_a, m_w_dw, m_b_dw, m_conv_ln_g, m_conv_ln_b, m_w_b, m_b_b, m_w_o, m_ln1_g, m_ln1_b, m_w_up, m_w_down, m_ln2_g, m_ln2_b, v_ln0_g, v_ln0_b, v_w_in, v_b_in, v_lb_logits, v_g_norm_w, v_w_a, v_w_dw, v_b_dw, v_conv_ln_g, v_conv_ln_b, v_w_b, v_b_b, v_w_o, v_ln1_g, v_ln1_b, v_w_up, v_w_down, v_ln2_g, v_ln2_b):
    L, D = w_in.shape[0], w_in.shape[1]
    T = x.shape[0] * x.shape[1]
    Dn = w_in.shape[2]
    rs = w_a.shape[1]
    rd = w_down.shape[1]
    cu = w_up.shape[2]
    F = rd * N_DEV
    hv = g_norm_w.shape[1]
    alpha = (2 * L) ** 0.25
    my_x, my_y, my_c = lax.axis_index("x"), lax.axis_index("y"), lax.axis_index("c")

    o_a, o_b, o_o, o_d = D, D + rs, D + 2 * rs, D + 3 * rs
    shards = []
    for l in range(L):
        shards.append(jnp.concatenate([w_in[l], w_a[l], w_b[l], w_o[l], w_down[l]], axis=0).astype(ACT_DTYPE))
        shards.append(w_up[l].astype(ACT_DTYPE))
    g_send, g_recv, g_thru, g_zone, g_token = _gather_start(shards, L)
    gathered = [None] * L

    def weights(l):
        ga, gb = gathered[l]
        return dict(
            w_in=ga[:, :D, :],
            w_a=ga[:, o_a:o_a + rs, :].reshape(D, D),
            w_b=ga[:, o_b:o_b + rs, :].reshape(D, D),
            w_o=ga[:, o_o:o_o + rs, :].reshape(D, D),
            w_down=ga[:, o_d:o_d + rd, :].reshape(F, D),
            w_up=gb.transpose(1, 0, 2).reshape(D, 2 * F))

    def finish_gather(l, after):
        sh, zn = _gather_wait("gather_wait_%d" % l, list(g_thru[2 * l:2 * l + 2]), list(g_zone[2 * l:2 * l + 2]),
                              g_send[l], g_recv[l], after)
        gathered[l] = _gather_finish(sh, zn)

    lbs = _lb_fwd(lb_logits)

    taps = jnp.pad(w_dw, ((0, 0), (0, CONV_HALO - CONV_WIDTH), (0, 0))).reshape(L * CONV_HALO, w_dw.shape[2])
    taps_all = _all_gather_small(taps)
    w_dw_full = taps_all.transpose(1, 0, 2).reshape(L, CONV_HALO, D)

    def params(l):
        return dict(b_in=b_in[l].reshape(N_SEC, 1, D), lbs=lbs[l].reshape(1, D), g_norm_w=g_norm_w[l].reshape(1, hv),
                    w_dw=w_dw_full[l], b_dw=b_dw[l].reshape(1, D), conv_ln_g=conv_ln_g[l].reshape(1, D),
                    conv_ln_b=conv_ln_b[l].reshape(1, D), b_b=b_b[l].reshape(1, D), ln1_g=ln1_g[l], ln1_b=ln1_b[l],
                    ln2_g=ln2_g[l], ln2_b=ln2_b[l])

    x2d = x.reshape(T, D)
    xc, xc_bf = _ln_fwd("ln0", x2d, None, 1.0, ln0_g, ln0_b, dep=g_token)
    finish_gather(0, xc_bf)
    saved = []
    for l in range(L):
        xc, xc_bf, s = _layer_fwd(xc, xc_bf, weights(l), params(l), alpha)
        saved.append(s)
        if l + 1 < L:
            finish_gather(l + 1, xc_bf)

    c_arr = jnp.reshape(my_c, (1,)).astype(jnp.int32)
    chip = 2 * my_x + my_y
    dx, loss_row = _loss_fwd_bwd(xc, loss_target.reshape(T, D))
    small = [None] * L
    pending = None
    upd_big = {n: None for n in ("w_in", "w_a", "w_b", "w_o", "w_down", "w_up")}
    wmv = dict(w_in=(w_in, m_w_in, v_w_in), w_a=(w_a, m_w_a, v_w_a), w_b=(w_b, m_w_b, v_w_b),
               w_o=(w_o, m_w_o, v_w_o), w_down=(w_down, m_w_down, v_w_down), w_up=(w_up, m_w_up, v_w_up))
    row0 = dict(w_in=0, w_a=o_a, w_b=o_b, w_o=o_o, w_down=o_d, w_up=0)

    def update_layer(l, q, r2):
        pre = jnp.stack([chip, jnp.int32(l)]).astype(jnp.int32)
        for name in upd_big:
            w, m, v = wmv[name]
            r, C = w.shape[1], w.shape[2]
            k = 1 if name == "w_up" else 0
            tr = _pick(r, (256, 128, 64, 32, 16))
            while row0[name] % tr:
                tr //= 2
            b0, nb = row0[name] // tr, r // tr
            specs = [pl.BlockSpec((None, tr, C), functools.partial(lambda i, s, b0: (s[0], b0 + i, 0), b0=b0))]
            specs += [pl.BlockSpec((None, tr, C), functools.partial(lambda i, s, j, b0: (j, b0 + i, 0), j=j, b0=b0))
                      for j in range(3)]
            upd_big[name] = _adamw(
                "adamw_" + name, w.reshape(L * r, C), m.reshape(L * r, C), v.reshape(L * r, C),
                [q[k], r2[k], r2[k], r2[k]], specs, tr, prefetch=pre, nsteps=nb,
                row_map=functools.partial(lambda i, s, nb: (s[1] * nb + i, 0), nb=nb), prev=upd_big[name])

    def finish_reduce(after):
        l, sems, q, zones = pending
        r2 = _exchange_chips_wait("reduce_wait_%d" % l, q, zones, sems[0], sems[1], after)
        update_layer(l, q, r2)

    token = None
    for l in range(L - 1, -1, -1):
        dx, big, small[l] = _layer_bwd(dx, saved[l], weights(l), params(l), alpha, dep=token)
        if pending is not None:
            finish_reduce(dx)
        send_a = jnp.concatenate([big["w_in"], big["w_a"].reshape(N_DEV, rs, D), big["w_b"].reshape(N_DEV, rs, D),
                                  big["w_o"].reshape(N_DEV, rs, D), big["w_down"].reshape(N_DEV, rd, D)], axis=1)
        send_b = big["w_up"].reshape(D, N_DEV, cu).transpose(1, 0, 2)
        r1a, r1b = _exchange_sibling([send_a, send_b])
        qs = [_pair_add(send_a, r1a, c_arr), _pair_add(send_b, r1b, c_arr)]
        s_send, s_recv, q_thru, zones, token = _exchange_chips_start("reduce_start_%d" % l, qs)
        pending = (l, (s_send, s_recv), list(q_thru), list(zones))
    dx0, _, dln0_g, dln0_b = _ln_bwd("ln0_bwd", x2d, dx, ln0_g, dep=token)
    dlb_logits = _lb_bwd(lb_logits, jnp.concatenate([small[l]["lbs"] for l in range(L)], axis=0))

    small_l = {n: [small[l][n] for l in range(L)] for n in _SMALL if n != "lb_logits"}
    small_l["lb_logits"] = [dlb_logits[l] for l in range(L)]
    loss_pad = jnp.pad(loss_row, ((0, 0), (0, D - LANES)))
    part = jnp.concatenate([_pack_small(small_l, dln0_g, dln0_b, loss_pad, D, L)]
                           + [small[l]["w_dw"] for l in range(L)], axis=0)
    parts_all = _all_gather_small(part)
    n_small = part.shape[0] - L * CONV_HALO

    finish_reduce(parts_all)
    upd = {n: [o.reshape(wmv[n][0].shape) for o in outs] for n, outs in upd_big.items()}

    inputs = dict(b_in=(b_in, m_b_in, v_b_in), lb_logits=(lb_logits, m_lb_logits, v_lb_logits),
                  g_norm_w=(g_norm_w, m_g_norm_w, v_g_norm_w), b_dw=(b_dw, m_b_dw, v_b_dw),
                  conv_ln_g=(conv_ln_g, m_conv_ln_g, v_conv_ln_g), conv_ln_b=(conv_ln_b, m_conv_ln_b, v_conv_ln_b),
                  b_b=(b_b, m_b_b, v_b_b), ln1_g=(ln1_g, m_ln1_g, v_ln1_g), ln1_b=(ln1_b, m_ln1_b, v_ln1_b),
                  ln2_g=(ln2_g, m_ln2_g, v_ln2_g), ln2_b=(ln2_b, m_ln2_b, v_ln2_b))
    zero_row = jnp.zeros((1, D), F32)
    packed = [_pack_small({n: [inputs[n][i][l] for l in range(L)] for n in _SMALL},
                          (ln0_g, m_ln0_g, v_ln0_g)[i], (ln0_b, m_ln0_b, v_ln0_b)[i], zero_row, D, L)
              for i in range(3)]
    small_specs = [pl.BlockSpec((None, n_small, D), functools.partial(lambda i, d: (d, 0, 0), d=d))
                   for d in range(N_DEV)]
    s_out = _adamw("adamw_small", packed[0], packed[1], packed[2], [parts_all] * N_DEV, small_specs, n_small)
    s_g, n_rows = _unpack_small(s_out[0], D, L, hv)
    s_d, _ = _unpack_small(s_out[1], D, L, hv)
    s_m, _ = _unpack_small(s_out[2], D, L, hv)
    s_v, _ = _unpack_small(s_out[3], D, L, hv)
    loss = s_out[0][n_rows, 0]

    cw = w_dw.shape[2]
    dev = 4 * my_x + 2 * my_y + my_c
    tap_parts = lax.dynamic_slice_in_dim(parts_all[:, n_small:, :], dev * cw, cw, axis=2)
    tap_specs = [pl.BlockSpec((None, L * CONV_HALO, cw), functools.partial(lambda i, d: (d, 0, 0), d=d))
                 for d in range(N_DEV)]
    pad_t = lambda a: jnp.pad(a, ((0, 0), (0, CONV_HALO - CONV_WIDTH), (0, 0))).reshape(L * CONV_HALO, cw)
    t_out = _adamw("adamw_taps", pad_t(w_dw), pad_t(m_w_dw), pad_t(v_w_dw), [tap_parts] * N_DEV, tap_specs,
                   L * CONV_HALO)
    upd["w_dw"] = [o.reshape(L, CONV_HALO, cw)[:, :CONV_WIDTH, :] for o in t_out]

    order = ["ln0_g", "ln0_b", "w_in", "b_in", "lb_logits", "g_norm_w", "w_a", "w_dw", "b_dw", "conv_ln_g",
             "conv_ln_b", "w_b", "b_b", "w_o", "ln1_g", "ln1_b", "w_up", "w_down", "ln2_g", "ln2_b"]
    small_sets = (s_g, s_d, s_m, s_v)
    outs = [loss, dx0.reshape(x.shape)]
    for i in range(4):
        for n in order:
            outs.append(upd[n][i] if n in upd else small_sets[i][n])
    return tuple(outs)
```

```python
import functools

import jax
import jax.numpy as jnp
from jax import lax
from jax.experimental import pallas as pl
from jax.experimental.pallas import tpu as pltpu

F32 = jnp.float32
MXU_DTYPE = jnp.bfloat16
ACT_DTYPE = jnp.bfloat16

LANES = 128
N_DEV = 8
N_SEC = 8
CONV_WIDTH = 31
CONV_HALO = 32
HG_C = 16
LN_EPS = 1e-5
RMS_EPS = 1e-6
F_MIN = 1e-30
ADAM_LR = 0.001
ADAM_B1 = 0.9
ADAM_B2 = 0.999
ADAM_EPS = 1e-08
ADAM_WD = 0.01
ADAM_STEP = 10
VMEM_LIMIT = 56 * 1024 * 1024
MESH = pl.DeviceIdType.MESH

_NN = (((1,), (0,)), ((), ()))
_NT = (((1,), (1,)), ((), ()))
_TN = (((0,), (0,)), ((), ()))


_ANY = pl.BlockSpec(memory_space=pl.ANY)
_HBM = pl.BlockSpec(memory_space=pltpu.HBM)
_SEM = pl.BlockSpec(memory_space=pltpu.SEMAPHORE)
_EFFECT = pltpu.SideEffectType.DATAFLOW_SIDE_EFFECTING


def _cp(*sem):
    return pltpu.CompilerParams(dimension_semantics=tuple(sem), vmem_limit_bytes=VMEM_LIMIT)


def _pick(n, cands):
    for c in cands:
        if c <= n and n % c == 0:
            return c
    return n


def _silu(x):
    return x * jax.nn.sigmoid(x)


def _dsilu(x):
    s = jax.nn.sigmoid(x)
    return s * (1.0 + x * (1.0 - s))


def _matmul(name, a, b, *, dims, grid, a_spec, b_spec, out_shape, out_spec, acc_shape, nk,
            bias=None, bias_spec=None, add=None, add_spec=None, add_scale=1.0, dep=None):
    has_bias, has_add = bias is not None, add is not None
    kaxis = len(grid) - 1

    def body(*refs):
        a_ref, b_ref = refs[0], refs[1]
        pos = 2
        bias_ref = add_ref = None
        if has_bias:
            bias_ref = refs[pos]
            pos += 1
        if has_add:
            add_ref = refs[pos]
            pos += 1
        if dep is not None:
            pos += 1
        o_ref = refs[pos]
        acc_ref = refs[pos + 1] if nk > 1 else None

        part = lax.dot_general(a_ref[...].astype(MXU_DTYPE), b_ref[...].astype(MXU_DTYPE), dims,
                               preferred_element_type=F32)

        def finish(r):
            if has_bias:
                r = r + bias_ref[...]
            if has_add:
                r = r + add_scale * add_ref[...]
            o_ref[...] = r.astype(o_ref.dtype)

        if nk == 1:
            finish(part)
        else:
            k = pl.program_id(kaxis)

            @pl.when(k == 0)
            def _():
                acc_ref[...] = part

            @pl.when(k > 0)
            def _():
                acc_ref[...] += part

            @pl.when(k == nk - 1)
            def _():
                finish(acc_ref[...])

    ins, specs = [a, b], [a_spec, b_spec]
    if has_bias:
        ins.append(bias)
        specs.append(bias_spec)
    if has_add:
        ins.append(add)
        specs.append(add_spec)
    if dep is not None:
        ins.append(dep)
        specs.append(_ANY)
    sem =("parallel",) * (len(grid) - 1) + ("arbitrary",) if nk > 1 else ("parallel",) * len(grid)
    return pl.pallas_call(
        body, name=name, grid=grid, in_specs=specs, out_specs=out_spec, out_shape=out_shape,
        scratch_shapes=[pltpu.VMEM(acc_shape, F32)] if nk > 1 else [],
        compiler_params=_cp(*sem))(*ins)


def _mm_nn(name, a, b, out_dtype, bias=None):
    M, K = a.shape
    N = b.shape[1]
    tn = _pick(N, (512, 256, 128))
    tk = K if K <= 1024 else _pick(K, (1408, 1024, 512, 256, 128))
    nk = K // tk
    return _matmul(
        name, a, b, dims=_NN, grid=(N // tn, nk),
        a_spec=pl.BlockSpec((M, tk), lambda j, k: (0, k)),
        b_spec=pl.BlockSpec((tk, tn), lambda j, k: (k, j)),
        out_shape=jax.ShapeDtypeStruct((M, N), out_dtype),
        out_spec=pl.BlockSpec((M, tn), lambda j, k: (0, j)),
        acc_shape=(M, tn), nk=nk,
        bias=bias, bias_spec=None if bias is None else pl.BlockSpec((1, tn), lambda j, k: (0, j)))


def _mm_nt(name, a, b, out_dtype, add=None, add_scale=1.0):
    M, K = a.shape
    N = b.shape[0]
    tn = _pick(N, (512, 256, 128))
    tk = K if K <= 1024 else _pick(K, (1408, 1024, 512, 256, 128))
    nk = K // tk
    return _matmul(
        name, a, b, dims=_NT, grid=(N // tn, nk),
        a_spec=pl.BlockSpec((M, tk), lambda j, k: (0, k)),
        b_spec=pl.BlockSpec((tn, tk), lambda j, k: (j, k)),
        out_shape=jax.ShapeDtypeStruct((M, N), out_dtype),
        out_spec=pl.BlockSpec((M, tn), lambda j, k: (0, j)),
        acc_shape=(M, tn), nk=nk,
        add=add, add_spec=None if add is None else pl.BlockSpec((M, tn), lambda j, k: (0, j)),
        add_scale=add_scale)


def _mm_tn(name, a, b, out_dtype):
    K, M = a.shape
    N = b.shape[1]
    tm = _pick(M, (256, 128))
    return _matmul(
        name, a, b, dims=_TN, grid=(M // tm,),
        a_spec=pl.BlockSpec((K, tm), lambda i: (0, i)),
        b_spec=pl.BlockSpec((K, N), lambda i: (0, 0)),
        out_shape=jax.ShapeDtypeStruct((M, N), out_dtype),
        out_spec=pl.BlockSpec((tm, N), lambda i: (i, 0)),
        acc_shape=(tm, N), nk=1)


def _proj_in(x_bf, w_in, b_in, dep=None):
    T, D = x_bf.shape
    tn = _pick(D, (512, 256, 128))
    return _matmul(
        "proj_in", x_bf, w_in, dims=_NN, grid=(N_SEC, D // tn),
        a_spec=pl.BlockSpec((T, D), lambda s, j: (0, 0)),
        b_spec=pl.BlockSpec((None, D, tn), lambda s, j: (s, 0, j)),
        out_shape=jax.ShapeDtypeStruct((N_SEC, T, D), F32),
        out_spec=pl.BlockSpec((None, T, tn), lambda s, j: (s, 0, j)),
        acc_shape=(T, tn), nk=1,
        bias=b_in, bias_spec=pl.BlockSpec((None, 1, tn), lambda s, j: (s, 0, j)), dep=dep)


def _proj_in_dx(dh, w_in, add, add_scale):
    _, T, D = dh.shape
    tn = _pick(D, (512, 256, 128))
    return _matmul(
        "proj_in_dx", dh, w_in, dims=_NT, grid=(D // tn, N_SEC),
        a_spec=pl.BlockSpec((None, T, D), lambda j, s: (s, 0, 0)),
        b_spec=pl.BlockSpec((None, tn, D), lambda j, s: (s, j, 0)),
        out_shape=jax.ShapeDtypeStruct((T, D), F32),
        out_spec=pl.BlockSpec((T, tn), lambda j, s: (0, j)),
        acc_shape=(T, tn), nk=N_SEC,
        add=add, add_spec=pl.BlockSpec((T, tn), lambda j, s: (0, j)), add_scale=add_scale)


def _proj_in_dw(x_bf, dh):
    _, T, D = dh.shape
    tn = _pick(D, (512, 256, 128))
    return _matmul(
        "proj_in_dw", x_bf, dh, dims=_TN, grid=(N_SEC, D // tn),
        a_spec=pl.BlockSpec((T, D), lambda s, j: (0, 0)),
        b_spec=pl.BlockSpec((None, T, tn), lambda s, j: (s, 0, j)),
        out_shape=jax.ShapeDtypeStruct((N_SEC, D, D), ACT_DTYPE),
        out_spec=pl.BlockSpec((None, D, tn), lambda s, j: (s, 0, j)),
        acc_shape=(D, tn), nk=1)


def _ffn_up(x_bf, w_up):
    T, D = x_bf.shape
    F = w_up.shape[1] // 2
    tn = _pick(F, (256, 128))
    nb = F // tn
    return _matmul(
        "ffn_up", x_bf, w_up, dims=_NN, grid=(2, nb),
        a_spec=pl.BlockSpec((T, D), lambda p, j: (0, 0)),
        b_spec=pl.BlockSpec((D, tn), lambda p, j: (0, p * nb + j)),
        out_shape=jax.ShapeDtypeStruct((2, T, F), F32),
        out_spec=pl.BlockSpec((None, T, tn), lambda p, j: (p, 0, j)),
        acc_shape=(T, tn), nk=1)


def _ffn_up_dx(dup, w_up, add, add_scale):
    _, T, F = dup.shape
    D = w_up.shape[0]
    tn = _pick(D, (512, 256, 128))
    tk = _pick(F, (256, 128))
    nb = F // tk
    return _matmul(
        "ffn_up_dx", dup, w_up, dims=_NT, grid=(D // tn, 2 * nb),
        a_spec=pl.BlockSpec((None, T, tk), lambda j, k: (k // nb, 0, k % nb)),
        b_spec=pl.BlockSpec((tn, tk), lambda j, k: (j, k)),
        out_shape=jax.ShapeDtypeStruct((T, D), F32),
        out_spec=pl.BlockSpec((T, tn), lambda j, k: (0, j)),
        acc_shape=(T, tn), nk=2 * nb,
        add=add, add_spec=pl.BlockSpec((T, tn), lambda j, k: (0, j)), add_scale=add_scale)


def _ffn_up_dw(x_bf, dup):
    _, T, F = dup.shape
    D = x_bf.shape[1]
    tn = _pick(F, (256, 128))
    nb = F // tn
    return _matmul(
        "ffn_up_dw", x_bf, dup, dims=_TN, grid=(2, nb),
        a_spec=pl.BlockSpec((T, D), lambda p, j: (0, 0)),
        b_spec=pl.BlockSpec((None, T, tn), lambda p, j: (p, 0, j)),
        out_shape=jax.ShapeDtypeStruct((D, 2 * F), ACT_DTYPE),
        out_spec=pl.BlockSpec((D, tn), lambda p, j: (0, p * nb + j)),
        acc_shape=(D, tn), nk=1)


def _ln_fwd(name, a, res, alpha, g, b, dep=None):
    T, D = a.shape
    tr = _pick(T, (256, 128, 64, 32, 16))
    has_res = res is not None

    def body(*refs):
        if has_res:
            a_ref, r_ref, g_ref, b_ref = refs[:4]
            y_ref, yb_ref, z_ref = refs[-3:]
            z = alpha * a_ref[...] + r_ref[...]
            z_ref[...] = z
        else:
            a_ref, g_ref, b_ref = refs[:3]
            y_ref, yb_ref = refs[-2:]
            z = a_ref[...]
        mu = jnp.mean(z, axis=-1, keepdims=True)
        zc = z - mu
        var = jnp.mean(zc * zc, axis=-1, keepdims=True)
        y = zc * lax.rsqrt(var + LN_EPS) * g_ref[...] + b_ref[...]
        y_ref[...] = y
        yb_ref[...] = y.astype(ACT_DTYPE)

    row = pl.BlockSpec((tr, D), lambda i: (i, 0))
    vec = pl.BlockSpec((1, D), lambda i: (0, 0))
    ins = [a] + ([res] if has_res else []) + [g.reshape(1, D), b.reshape(1, D)]
    in_specs = [row] + ([row] if has_res else []) + [vec, vec]
    if dep is not None:
        ins.append(dep)
        in_specs.append(_ANY)
    out_shape = [jax.ShapeDtypeStruct((T, D), F32), jax.ShapeDtypeStruct((T, D), ACT_DTYPE)]
    if has_res:
        out_shape.append(jax.ShapeDtypeStruct((T, D), F32))
    return pl.pallas_call(
        body, name=name, grid=(T // tr,), in_specs=in_specs,
        out_specs=[row] * len(out_shape), out_shape=out_shape, compiler_params=_cp("parallel"))(*ins)


def _ln_bwd(name, z, dy, g, dep=None):
    T, D = z.shape
    tr = _pick(T, (256, 128, 64, 32, 16))

    def body(z_ref, dy_ref, g_ref, *rest):
        dz_ref, dzb_ref, dg_ref, db_ref = rest[-4:]

        @pl.when(pl.program_id(0) == 0)
        def _():
            dg_ref[...] = jnp.zeros_like(dg_ref)
            db_ref[...] = jnp.zeros_like(db_ref)

        zv = z_ref[...]
        dy_ = dy_ref[...]
        mu = jnp.mean(zv, axis=-1, keepdims=True)
        zc = zv - mu
        rstd = lax.rsqrt(jnp.mean(zc * zc, axis=-1, keepdims=True) + LN_EPS)
        xhat = zc * rstd
        dxh = dy_ * g_ref[...]
        dz = rstd * (dxh - jnp.mean(dxh, axis=-1, keepdims=True)
                     - xhat * jnp.mean(dxh * xhat, axis=-1, keepdims=True))
        dz_ref[...] = dz
        dzb_ref[...] = dz.astype(ACT_DTYPE)
        dg_ref[...] += jnp.sum(dy_ * xhat, axis=0, keepdims=True)
        db_ref[...] += jnp.sum(dy_, axis=0, keepdims=True)

    row = pl.BlockSpec((tr, D), lambda i: (i, 0))
    vec = pl.BlockSpec((1, D), lambda i: (0, 0))
    ins, in_specs = [z, dy, g.reshape(1, D)], [row, row, vec]
    if dep is not None:
        ins.append(dep)
        in_specs.append(_ANY)
    return pl.pallas_call(
        body, name=name, grid=(T // tr,), in_specs=in_specs, out_specs=[row, row, vec, vec],
        out_shape=[jax.ShapeDtypeStruct((T, D), F32), jax.ShapeDtypeStruct((T, D), ACT_DTYPE),
                   jax.ShapeDtypeStruct((1, D), F32), jax.ShapeDtypeStruct((1, D), F32)],
        compiler_params=_cp("arbitrary"))(*ins)


def _loss_fwd_bwd(y, target):
    T, D = y.shape
    tr = _pick(T, (256, 128, 64, 32, 16))

    def body(y_ref, t_ref, dy_ref, l_ref):
        @pl.when(pl.program_id(0) == 0)
        def _():
            l_ref[...] = jnp.zeros_like(l_ref)

        e = y_ref[...] - t_ref[...]
        dy_ref[...] = e * (1.0 / D)
        row = jnp.sum(e * e, axis=-1, keepdims=True) * (1.0 / D)
        l_ref[...] += 0.5 * jnp.sum(row, axis=0, keepdims=True)

    rowspec = pl.BlockSpec((tr, D), lambda i: (i, 0))
    return pl.pallas_call(
        body, name="loss", grid=(T // tr,), in_specs=[rowspec, rowspec],
        out_specs=[rowspec, pl.BlockSpec((1, LANES), lambda i: (0, 0))],
        out_shape=[jax.ShapeDtypeStruct((T, D), F32), jax.ShapeDtypeStruct((1, LANES), F32)],
        compiler_params=_cp("arbitrary"))(y, target)


def _gate_fwd(y_h, y_c, h):
    T, D = y_h.shape
    tr = _pick(T, (256, 128, 64, 32, 16))

    def body(yh_ref, yc_ref, gh_ref, gc_ref, m_ref):
        m = jax.nn.sigmoid(gh_ref[...]) * yh_ref[...] + jax.nn.sigmoid(gc_ref[...]) * yc_ref[...]
        m_ref[...] = m.astype(ACT_DTYPE)

    row = pl.BlockSpec((tr, D), lambda i: (i, 0))
    return pl.pallas_call(
        body, name="gate_fwd", grid=(T // tr,),
        in_specs=[row, row, pl.BlockSpec((None, tr, D), lambda i: (6, i, 0)),
                  pl.BlockSpec((None, tr, D), lambda i: (7, i, 0))],
        out_specs=row, out_shape=jax.ShapeDtypeStruct((T, D), ACT_DTYPE),
        compiler_params=_cp("parallel"))(y_h, y_c, h, h)


def _gate_bwd(dm, y_h, y_c, h):
    T, D = y_h.shape
    tr = _pick(T, (256, 128, 64, 32, 16))

    def body(dm_ref, yh_ref, yc_ref, gh_ref, gc_ref, dyh_ref, dyc_ref, dbb_ref, dh_ref):
        @pl.when(pl.program_id(0) == 0)
        def _():
            dbb_ref[...] = jnp.zeros_like(dbb_ref)

        dm_ = dm_ref[...]
        sh = jax.nn.sigmoid(gh_ref[...])
        sc = jax.nn.sigmoid(gc_ref[...])
        dyc = dm_ * sc
        dyh_ref[...] = (dm_ * sh).astype(ACT_DTYPE)
        dyc_ref[...] = dyc.astype(ACT_DTYPE)
        dbb_ref[...] += jnp.sum(dyc, axis=0, keepdims=True)
        dh_ref[0] = (dm_ * yh_ref[...] * sh * (1.0 - sh)).astype(ACT_DTYPE)
        dh_ref[1] = (dm_ * yc_ref[...] * sc * (1.0 - sc)).astype(ACT_DTYPE)

    row = pl.BlockSpec((tr, D), lambda i: (i, 0))
    return pl.pallas_call(
        body, name="gate_bwd", grid=(T // tr,),
        in_specs=[row, row, row, pl.BlockSpec((None, tr, D), lambda i: (6, i, 0)),
                  pl.BlockSpec((None, tr, D), lambda i: (7, i, 0))],
        out_specs=[row, row, pl.BlockSpec((1, D), lambda i: (0, 0)),
                   pl.BlockSpec((2, tr, D), lambda i: (3, i, 0))],
        out_shape=[jax.ShapeDtypeStruct((T, D), ACT_DTYPE), jax.ShapeDtypeStruct((T, D), ACT_DTYPE),
                   jax.ShapeDtypeStruct((1, D), F32), jax.ShapeDtypeStruct((N_SEC, T, D), ACT_DTYPE)],
        compiler_params=_cp("arbitrary"))(dm, y_h, y_c, h, h)


def _swiglu_fwd(up):
    _, T, F = up.shape
    tr = _pick(T, (128, 64, 32, 16))

    def body(up_ref, act_ref):
        act_ref[...] = (_silu(up_ref[0]) * up_ref[1]).astype(ACT_DTYPE)

    return pl.pallas_call(
        body, name="swiglu_fwd", grid=(T // tr,),
        in_specs=[pl.BlockSpec((2, tr, F), lambda i: (0, i, 0))],
        out_specs=pl.BlockSpec((tr, F), lambda i: (i, 0)),
        out_shape=jax.ShapeDtypeStruct((T, F), ACT_DTYPE), compiler_params=_cp("parallel"))(up)


def _swiglu_bwd(dact, up):
    _, T, F = up.shape
    tr = _pick(T, (128, 64, 32, 16))

    def body(da_ref, up_ref, dup_ref):
        da = da_ref[...]
        ug = up_ref[0]
        dup_ref[0] = (da * up_ref[1] * _dsilu(ug)).astype(ACT_DTYPE)
        dup_ref[1] = (da * _silu(ug)).astype(ACT_DTYPE)

    blk = pl.BlockSpec((2, tr, F), lambda i: (0, i, 0))
    return pl.pallas_call(
        body, name="swiglu_bwd", grid=(T // tr,),
        in_specs=[pl.BlockSpec((tr, F), lambda i: (i, 0)), blk], out_specs=blk,
        out_shape=jax.ShapeDtypeStruct((2, T, F), ACT_DTYPE), compiler_params=_cp("parallel"))(dact, up)


def _colsum(dh):
    S, T, D = dh.shape
    tr = _pick(T, (512, 256, 128, 64, 32, 16))

    def body(x_ref, o_ref):
        @pl.when(pl.program_id(1) == 0)
        def _():
            o_ref[...] = jnp.zeros_like(o_ref)

        o_ref[...] += jnp.sum(x_ref[...].astype(F32), axis=0, keepdims=True)

    return pl.pallas_call(
        body, name="colsum", grid=(S, T // tr),
        in_specs=[pl.BlockSpec((None, tr, D), lambda s, i: (s, i, 0))],
        out_specs=pl.BlockSpec((None, 1, D), lambda s, i: (s, 0, 0)),
        out_shape=jax.ShapeDtypeStruct((S, 1, D), F32), compiler_params=_cp("parallel", "arbitrary"))(dh)


def _lb_softmax(x):
    L = x.shape[0]
    rows = [x[l:l + 1] for l in range(L)]
    m = rows[0]
    for r in rows[1:]:
        m = jnp.maximum(m, r)
    e = [jnp.exp(r - m) for r in rows]
    s = e[0]
    for r in e[1:]:
        s = s + r
    return [r / s for r in e]


def _lb_fwd(lb_logits):
    L, D = lb_logits.shape

    def body(x_ref, o_ref):
        p = _lb_softmax(x_ref[...])
        run = jnp.zeros_like(p[0])
        for l in range(L):
            if l > 0:
                run = run + p[l]
            o_ref[pl.ds(l, 1), :] = run

    return pl.pallas_call(body, name="lb_fwd", out_shape=jax.ShapeDtypeStruct((L, D), F32))(lb_logits)


def _lb_bwd(lb_logits, dlbs):
    L, D = lb_logits.shape

    def body(x_ref, d_ref, o_ref):
        p = _lb_softmax(x_ref[...])
        d = d_ref[...]
        dp = [jnp.zeros_like(p[0]) for _ in range(L)]
        run = jnp.zeros_like(p[0])
        for j in range(L - 1, 0, -1):
            run = run + d[j:j + 1]
            dp[j] = run
        dot = dp[0] * p[0]
        for j in range(1, L):
            dot = dot + dp[j] * p[j]
        for j in range(L):
            o_ref[pl.ds(j, 1), :] = p[j] * (dp[j] - dot)

    return pl.pallas_call(body, name="lb_bwd", out_shape=jax.ShapeDtypeStruct((L, D), F32))(lb_logits, dlbs)


def _blk_cumsum(x, c, reverse=False):
    n = x.shape[0]
    pos = lax.broadcasted_iota(jnp.int32, x.shape, 0) % c
    s = 1
    while s < c:
        if reverse:
            shifted = pltpu.roll(x, n - s, 0)
            x = x + jnp.where(pos + s < c, shifted, 0.0)
        else:
            shifted = pltpu.roll(x, s, 0)
            x = x + jnp.where(pos >= s, shifted, 0.0)
        s *= 2
    return x


def _hgrn_prologue(q_ref, f_ref, lb_ref):
    lbv = lb_ref[...]
    z = f_ref[...]
    sig = jax.nn.sigmoid(z)
    one_m = 1.0 - lbv
    f = lbv + one_m * sig
    logf = jnp.log(jnp.maximum(f, F_MIN))
    k = one_m * jax.nn.sigmoid(-z)
    q = _silu(q_ref[...])
    return q, k, logf, f, sig, one_m


def _hgrn_fwd(h, lbs_l, gw):
    _, T, D = h.shape
    nh = D // LANES
    c = HG_C
    Tt = _pick(T, (128, 64, 32, 16))
    nb = Tt // c

    def body(q_ref, f_ref, i_ref, g_ref, lb_ref, gw_ref, o_ref, y_ref, sall_ref,
             st_ref, G_s, q_s, k_s, W_s, o_s):
        @pl.when(pl.program_id(1) == 0)
        def _():
            st_ref[...] = jnp.zeros_like(st_ref)

        q, k, logf, _, _, _ = _hgrn_prologue(q_ref, f_ref, lb_ref)
        G_s[...] = _blk_cumsum(logf, c)
        q_s[...] = q
        k_s[...] = k
        ones = jnp.ones((LANES, LANES), MXU_DTYPE)
        rowid = lax.broadcasted_iota(jnp.int32, (c, LANES), 0)
        for bi in range(nb):
            r0 = bi * c
            Gb = G_s[pl.ds(r0, c), :]
            qb = q_s[pl.ds(r0, c), :]
            kb = k_s[pl.ds(r0, c), :]
            vb = i_ref[pl.ds(r0, c), :]
            glast = G_s[pl.ds(r0 + c - 1, 1), :]
            st = st_ref[...]
            sall_ref[bi] = st
            qd = qb * jnp.exp(Gb)
            o_s[pl.ds(r0, c), :] = lax.dot_general(qd.astype(MXU_DTYPE), st.astype(MXU_DTYPE), _NT,
                                                   preferred_element_type=F32)
            kd = kb * jnp.exp(glast - Gb)
            st_ref[...] = st * jnp.exp(glast) + lax.dot_general(
                vb.astype(MXU_DTYPE), kd.astype(MXU_DTYPE), _TN, preferred_element_type=F32)
            for t in range(c):
                gt = G_s[pl.ds(r0 + t, 1), :]
                qt = q_s[pl.ds(r0 + t, 1), :]
                e = jnp.where(rowid <= t, jnp.exp(jnp.minimum(gt - Gb, 0.0)), 0.0)
                W_s[pl.ds(t * c, c), :] = (e * kb * qt).astype(MXU_DTYPE)
            r = jnp.dot(W_s[...], ones, preferred_element_type=F32)
            for t in range(c):
                o_s[pl.ds(r0 + t, 1), :] += jnp.sum(r[t * c:(t + 1) * c] * vb, axis=0, keepdims=True)
        o = o_s[...]
        n = o * lax.rsqrt(jnp.mean(o * o, axis=-1, keepdims=True) + RMS_EPS)
        o_ref[...] = o
        y_ref[...] = (n * gw_ref[...] * _silu(g_ref[...])).astype(ACT_DTYPE)

    def sec(s):
        return pl.BlockSpec((None, Tt, LANES), lambda hd, i: (s, i, hd))

    col = pl.BlockSpec((Tt, LANES), lambda hd, i: (i, hd))
    return pl.pallas_call(
        body, name="hgrn_fwd", grid=(nh, T // Tt),
        in_specs=[sec(0), sec(1), sec(2), sec(3), pl.BlockSpec((1, LANES), lambda hd, i: (0, hd)),
                  pl.BlockSpec((1, LANES), lambda hd, i: (0, 0))],
        out_specs=[col, col, pl.BlockSpec((nb, None, LANES, LANES), lambda hd, i: (i, hd, 0, 0))],
        out_shape=[jax.ShapeDtypeStruct((T, D), F32), jax.ShapeDtypeStruct((T, D), ACT_DTYPE),
                   jax.ShapeDtypeStruct((T // c, nh, LANES, LANES), F32)],
        scratch_shapes=[pltpu.VMEM((LANES, LANES), F32), pltpu.VMEM((Tt, LANES), F32),
                        pltpu.VMEM((Tt, LANES), F32), pltpu.VMEM((Tt, LANES), F32),
                        pltpu.VMEM((c * c, LANES), MXU_DTYPE), pltpu.VMEM((Tt, LANES), F32)],
        compiler_params=_cp("parallel", "arbitrary"))(h, h, h, h, lbs_l, gw)


def _hgrn_bwd(h, lbs_l, gw, o_pre, st_all, dy, dh):
    _, T, D = h.shape
    nh = D // LANES
    c = HG_C
    Tt = _pick(T, (128, 64, 32, 16))
    nb = Tt // c
    nT = T // Tt

    def body(q_ref, f_ref, i_ref, g_ref, lb_ref, gw_ref, o_ref, sall_ref, dy_ref, dh_in_ref,
             dh_ref, dlb_ref, dgw_ref,
             dst_ref, G_s, q_s, k_s, do_s, E_s, WP_s, dq_s, dk_s, dv_s, dG_s):
        del dh_in_ref
        hd, ti = pl.program_id(0), pl.program_id(1)

        @pl.when(ti == 0)
        def _():
            dst_ref[...] = jnp.zeros_like(dst_ref)
            dlb_ref[...] = jnp.zeros_like(dlb_ref)

        @pl.when((ti == 0) & (hd == 0))
        def _():
            dgw_ref[...] = jnp.zeros_like(dgw_ref)

        q, k, logf, f, sig, one_m = _hgrn_prologue(q_ref, f_ref, lb_ref)
        G_s[...] = _blk_cumsum(logf, c)
        q_s[...] = q
        k_s[...] = k

        o = o_ref[...]
        gr = g_ref[...]
        dy_ = dy_ref[...]
        rr = lax.rsqrt(jnp.mean(o * o, axis=-1, keepdims=True) + RMS_EPS)
        n = o * rr
        sg = _silu(gr)
        gwv = gw_ref[...]
        dh_ref[3] = (dy_ * n * gwv * _dsilu(gr)).astype(ACT_DTYPE)
        dgw_ref[...] += jnp.sum(dy_ * n * sg, axis=0, keepdims=True)
        dn = dy_ * gwv * sg
        do_s[...] = rr * (dn - n * jnp.mean(dn * n, axis=-1, keepdims=True))

        ones = jnp.ones((LANES, LANES), MXU_DTYPE)
        rowid = lax.broadcasted_iota(jnp.int32, (c, LANES), 0)
        cc = c * c
        for bi in range(nb - 1, -1, -1):
            r0 = bi * c
            Gb = G_s[pl.ds(r0, c), :]
            qb = q_s[pl.ds(r0, c), :]
            kb = k_s[pl.ds(r0, c), :]
            vb = i_ref[pl.ds(r0, c), :]
            dob = do_s[pl.ds(r0, c), :]
            glast = G_s[pl.ds(r0 + c - 1, 1), :]
            st = sall_ref[bi]
            dst = dst_ref[...]
            a = jnp.exp(glast)
            eg = jnp.exp(Gb)
            egl = jnp.exp(glast - Gb)
            qd = qb * eg
            kd = kb * egl
            dob_m = dob.astype(MXU_DTYPE)
            dst_m = dst.astype(MXU_DTYPE)
            dqd = lax.dot_general(dob_m, st.astype(MXU_DTYPE), _NN, preferred_element_type=F32)
            dkd = lax.dot_general(vb.astype(MXU_DTYPE), dst_m, _NN, preferred_element_type=F32)
            dv_inter = lax.dot_general(kd.astype(MXU_DTYPE), dst_m, _NT, preferred_element_type=F32)
            da = jnp.sum(dst * st, axis=0, keepdims=True)
            dst_ref[...] = dst * a + lax.dot_general(dob_m, qd.astype(MXU_DTYPE), _TN,
                                                     preferred_element_type=F32)
            for t in range(c):
                gt = G_s[pl.ds(r0 + t, 1), :]
                qt = q_s[pl.ds(r0 + t, 1), :]
                dot_ = do_s[pl.ds(r0 + t, 1), :]
                e = jnp.where(rowid <= t, jnp.exp(jnp.minimum(gt - Gb, 0.0)), 0.0)
                E_s[pl.ds(t * c, c), :] = e
                WP_s[pl.ds(t * c, c), :] = (e * kb * qt).astype(MXU_DTYPE)
                WP_s[pl.ds(cc + t * c, c), :] = (vb * dot_).astype(MXU_DTYPE)
            r = jnp.dot(WP_s[...], ones, preferred_element_type=F32)
            dk_i = jnp.zeros((c, LANES), F32)
            dv_i = jnp.zeros((c, LANES), F32)
            for t in range(c):
                qt = q_s[pl.ds(r0 + t, 1), :]
                dot_ = do_s[pl.ds(r0 + t, 1), :]
                e = E_s[pl.ds(t * c, c), :]
                a_t = r[t * c:(t + 1) * c]
                da_t = r[cc + t * c:cc + (t + 1) * c]
                dae = da_t * e
                dq_s[pl.ds(r0 + t, 1), :] = jnp.sum(dae * kb, axis=0, keepdims=True)
                dk_i = dk_i + dae * qt
                dv_i = dv_i + a_t * dot_
            dq_i = dq_s[pl.ds(r0, c), :]
            dq_s[pl.ds(r0, c), :] = dqd * eg + dq_i
            dk_s[pl.ds(r0, c), :] = dkd * egl + dk_i
            dv_s[pl.ds(r0, c), :] = dv_inter + dv_i
            dkdkd = dkd * kd
            dG = dqd * qd + qb * dq_i - kb * dk_i - dkdkd
            dglast = jnp.sum(dkdkd, axis=0, keepdims=True) + da * a
            dG_s[pl.ds(r0, c), :] = dG + jnp.where(rowid == c - 1, dglast, 0.0)

        dlogf = _blk_cumsum(dG_s[...], c, reverse=True)
        df = jnp.where(f > F_MIN, dlogf / f, 0.0)
        dk = dk_s[...]
        dh_ref[0] = (dq_s[...] * _dsilu(q_ref[...])).astype(ACT_DTYPE)
        dh_ref[1] = ((df - dk) * one_m * sig * (1.0 - sig)).astype(ACT_DTYPE)
        dh_ref[2] = dv_s[...].astype(ACT_DTYPE)
        dlb_ref[...] += jnp.sum((df - dk) * (1.0 - sig), axis=0, keepdims=True)

    def sec(s):
        return pl.BlockSpec((None, Tt, LANES), lambda hd, i: (s, nT - 1 - i, hd))

    col = pl.BlockSpec((Tt, LANES), lambda hd, i: (nT - 1 - i, hd))
    tile = pltpu.VMEM((Tt, LANES), F32)
    return pl.pallas_call(
        body, name="hgrn_bwd", grid=(nh, nT),
        in_specs=[sec(0), sec(1), sec(2), sec(3), pl.BlockSpec((1, LANES), lambda hd, i: (0, hd)),
                  pl.BlockSpec((1, LANES), lambda hd, i: (0, 0)), col,
                  pl.BlockSpec((nb, None, LANES, LANES), lambda hd, i: (nT - 1 - i, hd, 0, 0)), col,
                  pl.BlockSpec(memory_space=pl.ANY)],
        out_specs=[pl.BlockSpec((4, Tt, LANES), lambda hd, i: (0, nT - 1 - i, hd)),
                   pl.BlockSpec((1, LANES), lambda hd, i: (0, hd)),
                   pl.BlockSpec((1, LANES), lambda hd, i: (0, 0))],
        out_shape=[jax.ShapeDtypeStruct(dh.shape, dh.dtype), jax.ShapeDtypeStruct((1, D), F32),
                   jax.ShapeDtypeStruct((1, LANES), F32)],
        scratch_shapes=[pltpu.VMEM((LANES, LANES), F32), tile, tile, tile, tile,
                        pltpu.VMEM((c * c, LANES), F32), pltpu.VMEM((2 * c * c, LANES), MXU_DTYPE),
                        tile, tile, tile, tile],
        input_output_aliases={9: 0},
        compiler_params=_cp("arbitrary", "arbitrary"))(h, h, h, h, lbs_l, gw, o_pre, st_all, dy, dh)


def _conv_fwd(h, w_dw, b_dw, ln_g, ln_b):
    _, T, D = h.shape
    Tt = _pick(T, (256, 128, 64, 32))
    hb = Tt // CONV_HALO
    off = CONV_HALO - (CONV_WIDTH - 1)

    def body(a_ref, b_ref, ap_ref, bp_ref, w_ref, bd_ref, g_ref, be_ref, yc_ref, y_ref, U_s):
        first = pl.program_id(0) == 0
        up = ap_ref[...] * jax.nn.sigmoid(bp_ref[...])
        U_s[pl.ds(0, CONV_HALO), :] = jnp.where(first, 0.0, up)
        U_s[pl.ds(CONV_HALO, Tt), :] = a_ref[...] * jax.nn.sigmoid(b_ref[...])
        for cb in range(D // LANES):
            cs = pl.ds(cb * LANES, LANES)
            acc = jnp.zeros((Tt, LANES), F32)
            for j in range(CONV_WIDTH):
                acc = acc + w_ref[pl.ds(j, 1), cs] * U_s[pl.ds(off + j, Tt), cs]
            yc_ref[:, cs] = acc + bd_ref[:, cs]
        yc = yc_ref[...]
        mu = jnp.mean(yc, axis=-1, keepdims=True)
        zc = yc - mu
        var = jnp.mean(zc * zc, axis=-1, keepdims=True)
        ln = zc * lax.rsqrt(var + LN_EPS) * g_ref[...] + be_ref[...]
        y_ref[...] = _silu(ln).astype(ACT_DTYPE)

    def main(s):
        return pl.BlockSpec((None, Tt, D), lambda i: (s, i, 0))

    def prev(s):
        return pl.BlockSpec((None, CONV_HALO, D), lambda i: (s, jnp.maximum(i * hb - 1, 0), 0))

    row = pl.BlockSpec((Tt, D), lambda i: (i, 0))
    vec = pl.BlockSpec((1, D), lambda i: (0, 0))
    return pl.pallas_call(
        body, name="conv_fwd", grid=(T // Tt,),
        in_specs=[main(4), main(5), prev(4), prev(5), pl.BlockSpec((CONV_HALO, D), lambda i: (0, 0)),
                  vec, vec, vec],
        out_specs=[row, row],
        out_shape=[jax.ShapeDtypeStruct((T, D), F32), jax.ShapeDtypeStruct((T, D), ACT_DTYPE)],
        scratch_shapes=[pltpu.VMEM((CONV_HALO + Tt, D), F32)],
        compiler_params=_cp("parallel"))(h, h, h, h, w_dw, b_dw, ln_g, ln_b)


def _conv_bwd(h, w_dw, ln_g, ln_b, yc, dy, dh):
    _, T, D = h.shape
    Tt = _pick(T, (256, 128, 64, 32))
    hb = Tt // CONV_HALO
    nT = T // Tt
    nhb = T // CONV_HALO
    off = CONV_HALO - (CONV_WIDTH - 1)

    def body(a_ref, b_ref, ap_ref, bp_ref, w_ref, g_ref, be_ref, yc_ref, ycn_ref, dy_ref, dyn_ref, dh_in_ref,
             dh_ref, dw_ref, dbd_ref, dg_ref, dbe_ref, U_s, DY_s, du_s):
        del dh_in_ref
        i = pl.program_id(0)

        @pl.when(i == 0)
        def _():
            dw_ref[...] = jnp.zeros_like(dw_ref)
            dbd_ref[...] = jnp.zeros_like(dbd_ref)
            dg_ref[...] = jnp.zeros_like(dg_ref)
            dbe_ref[...] = jnp.zeros_like(dbe_ref)

        gv = g_ref[...]
        bev = be_ref[...]

        def ln_silu_bwd(ycv, dyv):
            mu = jnp.mean(ycv, axis=-1, keepdims=True)
            zc = ycv - mu
            rstd = lax.rsqrt(jnp.mean(zc * zc, axis=-1, keepdims=True) + LN_EPS)
            xhat = zc * rstd
            dln = dyv * _dsilu(xhat * gv + bev)
            dxh = dln * gv
            dyc = rstd * (dxh - jnp.mean(dxh, axis=-1, keepdims=True)
                          - xhat * jnp.mean(dxh * xhat, axis=-1, keepdims=True))
            return dyc, dln, xhat

        dyc, dln, xhat = ln_silu_bwd(yc_ref[...], dy_ref[...])
        dg_ref[...] += jnp.sum(dln * xhat, axis=0, keepdims=True)
        dbe_ref[...] += jnp.sum(dln, axis=0, keepdims=True)
        dbd_ref[...] += jnp.sum(dyc, axis=0, keepdims=True)
        DY_s[pl.ds(0, Tt), :] = dyc
        dycn, _, _ = ln_silu_bwd(ycn_ref[...], dyn_ref[...])
        DY_s[pl.ds(Tt, CONV_HALO), :] = jnp.where(i == nT - 1, 0.0, dycn)

        sb = jax.nn.sigmoid(b_ref[...])
        av = a_ref[...]
        up = ap_ref[...] * jax.nn.sigmoid(bp_ref[...])
        U_s[pl.ds(0, CONV_HALO), :] = jnp.where(i == 0, 0.0, up)
        U_s[pl.ds(CONV_HALO, Tt), :] = av * sb

        for cb in range(D // LANES):
            cs = pl.ds(cb * LANES, LANES)
            dyb = DY_s[pl.ds(0, Tt), cs]
            acc = jnp.zeros((Tt, LANES), F32)
            for j in range(CONV_WIDTH):
                acc = acc + w_ref[pl.ds(j, 1), cs] * DY_s[pl.ds(CONV_WIDTH - 1 - j, Tt), cs]
                dw_ref[pl.ds(j, 1), cs] += jnp.sum(dyb * U_s[pl.ds(off + j, Tt), cs], axis=0, keepdims=True)
            du_s[:, cs] = acc
        du = du_s[...]
        dh_ref[0] = (du * sb).astype(ACT_DTYPE)
        dh_ref[1] = (du * av * sb * (1.0 - sb)).astype(ACT_DTYPE)

    def main(s):
        return pl.BlockSpec((None, Tt, D), lambda i: (s, i, 0))

    def prev(s):
        return pl.BlockSpec((None, CONV_HALO, D), lambda i: (s, jnp.maximum(i * hb - 1, 0), 0))

    row = pl.BlockSpec((Tt, D), lambda i: (i, 0))
    nxt = pl.BlockSpec((CONV_HALO, D), lambda i: (jnp.minimum((i + 1) * hb, nhb - 1), 0))
    vec = pl.BlockSpec((1, D), lambda i: (0, 0))
    wspec = pl.BlockSpec((CONV_HALO, D), lambda i: (0, 0))
    return pl.pallas_call(
        body, name="conv_bwd", grid=(nT,),
        in_specs=[main(4), main(5), prev(4), prev(5), wspec, vec, vec, row, nxt, row, nxt,
                  pl.BlockSpec(memory_space=pl.ANY)],
        out_specs=[pl.BlockSpec((2, Tt, D), lambda i: (2, i, 0)), wspec, vec, vec, vec],
        out_shape=[jax.ShapeDtypeStruct(dh.shape, dh.dtype), jax.ShapeDtypeStruct((CONV_HALO, D), F32),
                   jax.ShapeDtypeStruct((1, D), F32), jax.ShapeDtypeStruct((1, D), F32),
                   jax.ShapeDtypeStruct((1, D), F32)],
        scratch_shapes=[pltpu.VMEM((CONV_HALO + Tt, D), F32), pltpu.VMEM((Tt + CONV_HALO, D), F32),
                        pltpu.VMEM((Tt, D), F32)],
        input_output_aliases={11: 0},
        compiler_params=_cp("arbitrary"))(h, h, h, h, w_dw, ln_g, ln_b, yc, yc, dy, dy, dh)


def _adamw(name, w, m, v, parts, part_specs, tr, prefetch=None, nsteps=None, row_map=None, prev=None):
    R, C = w.shape
    bc1 = 1.0 - ADAM_B1 ** ADAM_STEP
    bc2 = 1.0 - ADAM_B2 ** ADAM_STEP
    npart = len(parts)
    npre = 0 if prefetch is None else 1
    nprev = 0 if prev is None else 4

    def body(*refs):
        refs = refs[npre:]
        w_ref, m_ref, v_ref = refs[:3]
        p_refs = refs[3:3 + npart]
        g_ref, d_ref, mo_ref, vo_ref = refs[3 + npart + nprev:]
        g = p_refs[0][...].astype(F32)
        for p in p_refs[1:]:
            g = g + p[...].astype(F32)
        wv = w_ref[...]
        mn = ADAM_B1 * m_ref[...] + (1.0 - ADAM_B1) * g
        vn = ADAM_B2 * v_ref[...] + (1.0 - ADAM_B2) * (g * g)
        m_hat = mn / bc1
        v_hat = vn / bc2
        g_ref[...] = g
        d_ref[...] = -ADAM_LR * (m_hat / (jnp.sqrt(v_hat) + ADAM_EPS) + ADAM_WD * wv)
        mo_ref[...] = mn
        vo_ref[...] = vn

    if row_map is None:
        row_map = (lambda i: (i, 0)) if prefetch is None else (lambda i, s: (i, 0))
    row = pl.BlockSpec((tr, C), row_map)
    out = jax.ShapeDtypeStruct((R, C), F32)
    gs = pltpu.PrefetchScalarGridSpec(
        num_scalar_prefetch=npre, grid=(R // tr if nsteps is None else nsteps,),
        in_specs=[row, row, row] + list(part_specs) + [_ANY] * nprev, out_specs=[row] * 4)
    args = ([prefetch] if npre else []) + [w, m, v] + list(parts) + (list(prev) if nprev else [])
    first_prev = npre + 3 + npart
    return pl.pallas_call(body, name=name, grid_spec=gs, out_shape=[out] * 4,
                          input_output_aliases={first_prev + i: i for i in range(nprev)},
                          compiler_params=_cp("parallel"))(*args)


def _pair_add(p, r1, my_c):
    _, R, C = r1.shape
    tr = max(t for t in range(16, 1025, 16) if R % t == 0)

    def body(c_ref, p_ref, r_ref, q_ref):
        del c_ref
        q_ref[...] = (p_ref[...].astype(F32) + r_ref[...].astype(F32)).astype(q_ref.dtype)

    gs = pltpu.PrefetchScalarGridSpec(
        num_scalar_prefetch=1, grid=(4, R // tr),
        in_specs=[pl.BlockSpec((None, tr, C), lambda j, i, c: (2 * j + c[0], i, 0)),
                  pl.BlockSpec((None, tr, C), lambda j, i, c: (j, i, 0))],
        out_specs=pl.BlockSpec((None, tr, C), lambda j, i, c: (j, i, 0)))
    return pl.pallas_call(body, name="pair_add", grid_spec=gs, out_shape=jax.ShapeDtypeStruct(r1.shape, r1.dtype),
                          compiler_params=_cp("parallel", "parallel"))(my_c, p, r1)


def _place():
    x, y, c = lax.axis_index("x"), lax.axis_index("y"), lax.axis_index("c")
    chips = [(1 - x, y), (x, 1 - y), (1 - x, 1 - y)]
    return x, y, c, chips


def _hbm(a):
    return pltpu.with_memory_space_constraint(a, pltpu.HBM)


def _gather_targets():
    x, y, c, chips = _place()
    return 4 * x + 2 * y + c, [(x, y, 1 - c)] + [(*chip, c) for chip in chips]


def _gather_start(name, shards, after=None):
    n = len(shards)
    lands = [_hbm(lax.empty((N_DEV,) + s.shape, s.dtype)) for s in shards]
    n_in = 2 * n + (0 if after is None else 1)

    def body(*refs):
        srcs, zones = refs[:n], refs[n:2 * n]
        send, recv, token = refs[n_in], refs[n_in + 1], refs[-1]
        mine, targets = _gather_targets()
        for a in range(n):
            for k, to in enumerate(targets):
                pltpu.make_async_remote_copy(
                    src_ref=srcs[a], dst_ref=zones[a].at[mine], send_sem=send.at[4 * a + k],
                    recv_sem=recv.at[4 * a + k], device_id=to, device_id_type=MESH).start()
        token[...] = jnp.zeros_like(token)

    sem = pltpu.SemaphoreType.DMA((4 * n,))
    out_shape = ([sem, sem] + [pltpu.HBM(s.shape, s.dtype) for s in shards]
                 + [pltpu.HBM(z.shape, z.dtype) for z in lands] + [jax.ShapeDtypeStruct((8, LANES), F32)])
    outs = pl.pallas_call(
        body, name=name, out_shape=out_shape, in_specs=[_HBM] * (2 * n) + ([] if after is None else [_ANY]),
        out_specs=[_SEM, _SEM] + [_HBM] * (2 * n) + [pl.BlockSpec(memory_space=pltpu.VMEM)],
        input_output_aliases={i: 2 + i for i in range(2 * n)},
        compiler_params=pltpu.CompilerParams(has_side_effects=_EFFECT))(
            *[_hbm(s) for s in shards], *lands, *([] if after is None else [after]))
    return outs[0], outs[1], list(outs[2:2 + n]), list(outs[2 + n:2 + 2 * n]), outs[-1]


def _gather_wait(name, shards, zones, send, recv, after):
    per = len(shards)

    def body(*refs):
        srcs, lz = refs[:per], refs[per:2 * per]
        send_s, recv_s = refs[2 * per], refs[2 * per + 1]
        mine, targets = _gather_targets()
        for a in range(per):
            for k, to in enumerate(targets):
                cp = pltpu.make_async_remote_copy(
                    src_ref=srcs[a], dst_ref=lz[a].at[mine], send_sem=send_s.at[4 * a + k],
                    recv_sem=recv_s.at[4 * a + k], device_id=to, device_id_type=MESH)
                cp.wait_send()
                cp.wait_recv()

    outs = pl.pallas_call(
        body, name=name, out_shape=[pltpu.HBM(s.shape, s.dtype) for s in shards + zones],
        in_specs=[_HBM] * (2 * per) + [_SEM, _SEM, _ANY], out_specs=[_HBM] * (2 * per),
        input_output_aliases={i: i for i in range(2 * per)},
        compiler_params=pltpu.CompilerParams(has_side_effects=_EFFECT))(*shards, *zones, send, recv, after)
    return outs[:per], outs[per:]


def _gather_finish(shards, zones):
    n = len(shards)

    def body(*refs):
        srcs, lz = refs[:n], refs[2 * n:3 * n]
        send_sems, recv_sems, local_sems = refs[3 * n:]
        x, y, c, chips = _place()

        def fwd(a, j, pc):
            cx, cy = chips[j]
            blk = lz[a].at[4 * cx + 2 * cy + pc]
            return pltpu.make_async_remote_copy(
                src_ref=blk, dst_ref=blk, send_sem=send_sems.at[3 * a + j], recv_sem=recv_sems.at[3 * a + j],
                device_id=(x, y, 1 - c), device_id_type=MESH)

        mine = [pltpu.make_async_copy(srcs[a], lz[a].at[4 * x + 2 * y + c], local_sems.at[a]) for a in range(n)]
        sends = [fwd(a, j, c) for a in range(n) for j in range(3)]
        for cp in mine + sends:
            cp.start()
        for a in range(n):
            for j in range(3):
                fwd(a, j, 1 - c).wait_recv()
        for cp in sends:
            cp.wait_send()
        for cp in mine:
            cp.wait()

    return pl.pallas_call(
        body, name="gather_finish", out_shape=[jax.ShapeDtypeStruct(z.shape, z.dtype) for z in zones],
        in_specs=[_ANY] * (2 * n), out_specs=[_ANY] * n, input_output_aliases={n + a: a for a in range(n)},
        scratch_shapes=[pltpu.SemaphoreType.DMA((3 * n,)), pltpu.SemaphoreType.DMA((3 * n,)),
                        pltpu.SemaphoreType.DMA((n,))])(*shards, *zones)


def _exchange_sibling(bufs):
    n_arr = len(bufs)

    def body(*refs):
        srcs, outs = refs[:n_arr], refs[n_arr:2 * n_arr]
        send_sems, recv_sems = refs[2 * n_arr:]
        x, y, c, _ = _place()
        copies = []
        for n in range(n_arr):
            for j in range(4):
                copies.append(pltpu.make_async_remote_copy(
                    src_ref=srcs[n].at[2 * j + 1 - c], dst_ref=outs[n].at[j],
                    send_sem=send_sems.at[4 * n + j], recv_sem=recv_sems.at[4 * n + j],
                    device_id=(x, y, 1 - c), device_id_type=MESH))
        for cp in copies:
            cp.start()
        for cp in copies:
            cp.wait()

    return pl.pallas_call(
        body, name="exchange_sibling",
        out_shape=[jax.ShapeDtypeStruct((4,) + b.shape[1:], b.dtype) for b in bufs],
        in_specs=[_ANY] * n_arr, out_specs=[_ANY] * n_arr,
        scratch_shapes=[pltpu.SemaphoreType.DMA((4 * n_arr,)), pltpu.SemaphoreType.DMA((4 * n_arr,))])(*bufs)


def _chip_copies(srcs, zones, send, recv):
    _, _, c, chips = _place()
    return [pltpu.make_async_remote_copy(
        src_ref=srcs[n].at[2 * cx + cy], dst_ref=zones[n].at[k], send_sem=send.at[3 * n + k],
        recv_sem=recv.at[3 * n + k], device_id=(cx, cy, c), device_id_type=MESH)
        for n in range(len(srcs)) for k, (cx, cy) in enumerate(chips)]


def _exchange_chips_start(name, bufs, after=None):
    n = len(bufs)
    n_in = 2 * n + (0 if after is None else 1)
    lands = [_hbm(lax.empty((3,) + b.shape[1:], b.dtype)) for b in bufs]

    def body(*refs):
        srcs, zones = refs[:n], refs[n:2 * n]
        send, recv, token = refs[n_in], refs[n_in + 1], refs[-1]
        for cp in _chip_copies(srcs, zones, send, recv):
            cp.start()
        token[...] = jnp.zeros_like(token)

    sem = pltpu.SemaphoreType.DMA((3 * n,))
    outs = pl.pallas_call(
        body, name=name,
        out_shape=[sem, sem] + [pltpu.HBM(b.shape, b.dtype) for b in bufs]
        + [pltpu.HBM(z.shape, z.dtype) for z in lands] + [jax.ShapeDtypeStruct((8, LANES), F32)],
        in_specs=[_HBM] * (2 * n) + ([] if after is None else [_ANY]),
        out_specs=[_SEM, _SEM] + [_HBM] * (2 * n) + [pl.BlockSpec(memory_space=pltpu.VMEM)],
        input_output_aliases={i: 2 + i for i in range(2 * n)},
        compiler_params=pltpu.CompilerParams(has_side_effects=_EFFECT))(
            *[_hbm(b) for b in bufs], *lands, *([] if after is None else [after]))
    return outs[0], outs[1], outs[2:2 + n], outs[2 + n:2 + 2 * n], outs[-1]


def _exchange_chips_wait(name, bufs, zones, send, recv, after):
    n = len(bufs)

    def body(*refs):
        for cp in _chip_copies(refs[:n], refs[n:2 * n], refs[2 * n], refs[2 * n + 1]):
            cp.wait_send()
            cp.wait_recv()

    outs = pl.pallas_call(
        body, name=name, out_shape=[pltpu.HBM(a.shape, a.dtype) for a in list(bufs) + list(zones)],
        in_specs=[_HBM] * (2 * n) + [_SEM, _SEM, _ANY], out_specs=[_HBM] * (2 * n),
        input_output_aliases={i: i for i in range(2 * n)},
        compiler_params=pltpu.CompilerParams(has_side_effects=_EFFECT))(*bufs, *zones, send, recv, after)
    return outs[n:]


def _all_gather_small(part):
    def body(src, out, send_sems, recv_sems, local_sem):
        x, y, c, _ = _place()
        mine = pltpu.make_async_copy(src, out.at[4 * x + 2 * y + c], local_sem)
        mine.start()
        copies = []
        for r in range(1, N_DEV):
            dx, dy, dc = (r >> 2) & 1, (r >> 1) & 1, r & 1
            peer = (1 - x if dx else x, 1 - y if dy else y, 1 - c if dc else c)
            copies.append(pltpu.make_async_remote_copy(
                src_ref=src, dst_ref=out.at[4 * x + 2 * y + c],
                send_sem=send_sems.at[r - 1], recv_sem=recv_sems.at[r - 1],
                device_id=peer, device_id_type=MESH))
        for cp in copies:
            cp.start()
        for cp in copies:
            cp.wait()
        mine.wait()

    return pl.pallas_call(
        body, name="all_gather_small",
        out_shape=jax.ShapeDtypeStruct((N_DEV,) + part.shape, part.dtype),
        in_specs=[_ANY], out_specs=_ANY,
        scratch_shapes=[pltpu.SemaphoreType.DMA((N_DEV - 1,)), pltpu.SemaphoreType.DMA((N_DEV - 1,)),
                        pltpu.SemaphoreType.DMA])(part)


def _layer_fwd(xin, xin_bf, W, P, alpha, dep=None):
    h = _proj_in(xin_bf, W["w_in"], P["b_in"], dep=dep)
    o_pre, y_hg, st_all = _hgrn_fwd(h, P["lbs"], P["g_norm_w"])
    yc_pre, y_cv = _conv_fwd(h, P["w_dw"], P["b_dw"], P["conv_ln_g"], P["conv_ln_b"])
    y_h = _mm_nn("branch_a", y_hg, W["w_a"], F32)
    y_c = _mm_nn("branch_b", y_cv, W["w_b"], F32, bias=P["b_b"])
    merged = _gate_fwd(y_h, y_c, h)
    mix = _mm_nn("mix_out", merged, W["w_o"], F32)
    x1, x1_bf, z1 = _ln_fwd("ln1", xin, mix, alpha, P["ln1_g"], P["ln1_b"])
    up = _ffn_up(x1_bf, W["w_up"])
    act = _swiglu_fwd(up)
    ffn = _mm_nn("ffn_down", act, W["w_down"], F32)
    x2, x2_bf, z2 = _ln_fwd("ln2", x1, ffn, alpha, P["ln2_g"], P["ln2_b"])
    saved = dict(xin_bf=xin_bf, h=h, o_pre=o_pre, y_hg=y_hg, st_all=st_all, yc_pre=yc_pre, y_cv=y_cv,
                 y_h=y_h, y_c=y_c, merged=merged, z1=z1, x1_bf=x1_bf, up=up, act=act, z2=z2)
    return x2, x2_bf, saved


def _layer_bwd(dx2, S, W, P, alpha, dep=None):
    dz2, dz2_bf, dln2_g, dln2_b = _ln_bwd("ln2_bwd", S["z2"], dx2, P["ln2_g"], dep=dep)
    dact = _mm_nt("ffn_down_dx", dz2_bf, W["w_down"], F32)
    dw_down = _mm_tn("ffn_down_dw", S["act"], dz2_bf, ACT_DTYPE)
    dup = _swiglu_bwd(dact, S["up"])
    dx1 = _ffn_up_dx(dup, W["w_up"], dz2, alpha)
    dw_up = _ffn_up_dw(S["x1_bf"], dup)
    dz1, dz1_bf, dln1_g, dln1_b = _ln_bwd("ln1_bwd", S["z1"], dx1, P["ln1_g"])
    dmerged = _mm_nt("mix_out_dx", dz1_bf, W["w_o"], F32)
    dw_o = _mm_tn("mix_out_dw", S["merged"], dz1_bf, ACT_DTYPE)
    dy_h, dy_c, db_b, dh = _gate_bwd(dmerged, S["y_h"], S["y_c"], S["h"])
    dy_cv = _mm_nt("branch_b_dx", dy_c, W["w_b"], F32)
    dw_b = _mm_tn("branch_b_dw", S["y_cv"], dy_c, ACT_DTYPE)
    dy_hg = _mm_nt("branch_a_dx", dy_h, W["w_a"], F32)
    dw_a = _mm_tn("branch_a_dw", S["y_hg"], dy_h, ACT_DTYPE)
    dh, dw_dw, db_dw, dcln_g, dcln_b = _conv_bwd(S["h"], P["w_dw"], P["conv_ln_g"], P["conv_ln_b"],
                                                 S["yc_pre"], dy_cv, dh)
    dh, dlbs, dgw = _hgrn_bwd(S["h"], P["lbs"], P["g_norm_w"], S["o_pre"], S["st_all"], dy_hg, dh)
    dxin = _proj_in_dx(dh, W["w_in"], dz1, alpha)
    dw_in = _proj_in_dw(S["xin_bf"], dh)
    db_in = _colsum(dh)
    big = dict(w_in=dw_in, w_a=dw_a, w_b=dw_b, w_o=dw_o, w_down=dw_down, w_up=dw_up)
    small = dict(b_in=db_in, lbs=dlbs, g_norm_w=dgw, w_dw=dw_dw, b_dw=db_dw, conv_ln_g=dcln_g,
                 conv_ln_b=dcln_b, b_b=db_b, ln1_g=dln1_g, ln1_b=dln1_b, ln2_g=dln2_g, ln2_b=dln2_b)
    return dxin, big, small


_SMALL = ("b_in", "lb_logits", "g_norm_w", "b_dw", "conv_ln_g", "conv_ln_b", "b_b", "ln1_g", "ln1_b", "ln2_g",
          "ln2_b")


def _pack_small(per_layer, ln0_g, ln0_b, extra_row, D, L):
    rows = []
    for l in range(L):
        for n in _SMALL:
            a = per_layer[n][l]
            if n == "b_in":
                rows.append(a.reshape(N_SEC, D))
            elif n == "g_norm_w":
                rows.append(jnp.pad(a.reshape(1, -1), ((0, 0), (0, D - a.size))))
            else:
                rows.append(a.reshape(1, D))
    rows += [ln0_g.reshape(1, D), ln0_b.reshape(1, D), extra_row]
    buf = jnp.concatenate(rows, axis=0)
    pad = (-buf.shape[0]) % 8
    return jnp.pad(buf, ((0, pad), (0, 0)))


def _unpack_small(buf, D, L, hv):
    out = {n: [] for n in _SMALL}
    r = 0
    for l in range(L):
        for n in _SMALL:
            if n == "b_in":
                out[n].append(buf[r:r + N_SEC].reshape(N_SEC * D))
                r += N_SEC
            elif n == "g_norm_w":
                out[n].append(buf[r, :hv])
                r += 1
            else:
                out[n].append(buf[r])
                r += 1
    res = {n: jnp.stack(v) for n, v in out.items()}
    res["ln0_g"] = buf[r]
    res["ln0_b"] = buf[r + 1]
    return res, r + 2


def kernel(x, ln0_g, ln0_b, w_in, b_in, lb_logits, g_norm_w, w_a, w_dw, b_dw, conv_ln_g, conv_ln_b, w_b, b_b, w_o, ln1_g, ln1_b, w_up, w_down, ln2_g, ln2_b, loss_target, m_ln0_g, m_ln0_b, m_w_in, m_b_in, m_lb_logits, m_g_norm_w, m_w_a, m_w_dw, m_b_dw, m_conv_ln_g, m_conv_ln_b, m_w_b, m_b_b, m_w_o, m_ln1_g, m_ln1_b, m_w_up, m_w_down, m_ln2_g, m_ln2_b, v_ln0_g, v_ln0_b, v_w_in, v_b_in, v_lb_logits, v_g_norm_w, v_w_a, v_w_dw, v_b_dw, v_conv_ln_g, v_conv_ln_b, v_w_b, v_b_b, v_w_o, v_ln1_g, v_ln1_b, v_w_up, v_w_down, v_ln2_g, v_ln2_b):
    L, D = w_in.shape[0], w_in.shape[1]
    T = x.shape[0] * x.shape[1]
    Dn = w_in.shape[2]
    rs = w_a.shape[1]
    rd = w_down.shape[1]
    cu = w_up.shape[2]
    F = rd * N_DEV
    hv = g_norm_w.shape[1]
    alpha = (2 * L) ** 0.25
    my_x, my_y, my_c = lax.axis_index("x"), lax.axis_index("y"), lax.axis_index("c")

    o_a, o_b, o_o, o_d = D, D + rs, D + 2 * rs, D + 3 * rs
    taps = jnp.pad(w_dw, ((0, 0), (0, CONV_HALO - CONV_WIDTH), (0, 0))).reshape(L * CONV_HALO, w_dw.shape[2])
    taps_all = _all_gather_small(taps)
    w_dw_full = taps_all.transpose(1, 0, 2).reshape(L, CONV_HALO, D)

    gathered = [None] * L
    started = [None] * L

    def start_gather(l, after):
        shards = [jnp.concatenate([w_in[l], w_a[l], w_b[l], w_o[l], w_down[l]], axis=0).astype(ACT_DTYPE),
                  w_up[l].astype(ACT_DTYPE)]
        started[l] = _gather_start("gather_start_%d" % l, shards, after)
        return started[l][4]

    def weights(l):
        ga, gb = gathered[l]
        return dict(
            w_in=ga[:, :D, :],
            w_a=ga[:, o_a:o_a + rs, :].reshape(D, D),
            w_b=ga[:, o_b:o_b + rs, :].reshape(D, D),
            w_o=ga[:, o_o:o_o + rs, :].reshape(D, D),
            w_down=ga[:, o_d:o_d + rd, :].reshape(F, D),
            w_up=gb.transpose(1, 0, 2).reshape(D, 2 * F))

    def finish_gather(l, after):
        send, recv, thru, zone, _ = started[l]
        sh, zn = _gather_wait("gather_wait_%d" % l, thru, zone, send, recv, after)
        gathered[l] = _gather_finish(sh, zn)

    lbs = _lb_fwd(lb_logits)

    def params(l):
        return dict(b_in=b_in[l].reshape(N_SEC, 1, D), lbs=lbs[l].reshape(1, D), g_norm_w=g_norm_w[l].reshape(1, hv),
                    w_dw=w_dw_full[l], b_dw=b_dw[l].reshape(1, D), conv_ln_g=conv_ln_g[l].reshape(1, D),
                    conv_ln_b=conv_ln_b[l].reshape(1, D), b_b=b_b[l].reshape(1, D), ln1_g=ln1_g[l], ln1_b=ln1_b[l],
                    ln2_g=ln2_g[l], ln2_b=ln2_b[l])

    x2d = x.reshape(T, D)
    token = start_gather(0, taps_all)
    xc, xc_bf = _ln_fwd("ln0", x2d, None, 1.0, ln0_g, ln0_b, dep=token)
    finish_gather(0, xc_bf)
    saved = []
    for l in range(L):
        token = start_gather(l + 1, gathered[l][0]) if l + 1 < L else None
        xc, xc_bf, s = _layer_fwd(xc, xc_bf, weights(l), params(l), alpha, dep=token)
        saved.append(s)
        if l + 1 < L:
            finish_gather(l + 1, xc_bf)

    c_arr = jnp.reshape(my_c, (1,)).astype(jnp.int32)
    chip = 2 * my_x + my_y
    dx, loss_row = _loss_fwd_bwd(xc, loss_target.reshape(T, D))
    small = [None] * L
    pending = None
    upd_big = {n: None for n in ("w_in", "w_a", "w_b", "w_o", "w_down", "w_up")}
    wmv = dict(w_in=(w_in, m_w_in, v_w_in), w_a=(w_a, m_w_a, v_w_a), w_b=(w_b, m_w_b, v_w_b),
               w_o=(w_o, m_w_o, v_w_o), w_down=(w_down, m_w_down, v_w_down), w_up=(w_up, m_w_up, v_w_up))
    row0 = dict(w_in=0, w_a=o_a, w_b=o_b, w_o=o_o, w_down=o_d, w_up=0)

    def update_layer(l, q, r2):
        pre = jnp.stack([chip, jnp.int32(l)]).astype(jnp.int32)
        for name in upd_big:
            w, m, v = wmv[name]
            r, C = w.shape[1], w.shape[2]
            k = 1 if name == "w_up" else 0
            tr = _pick(r, (256, 128, 64, 32, 16))
            while row0[name] % tr:
                tr //= 2
            b0, nb = row0[name] // tr, r // tr
            specs = [pl.BlockSpec((None, tr, C), functools.partial(lambda i, s, b0: (s[0], b0 + i, 0), b0=b0))]
            specs += [pl.BlockSpec((None, tr, C), functools.partial(lambda i, s, j, b0: (j, b0 + i, 0), j=j, b0=b0))
                      for j in range(3)]
            upd_big[name] = _adamw(
                "adamw_" + name, w.reshape(L * r, C), m.reshape(L * r, C), v.reshape(L * r, C),
                [q[k], r2[k], r2[k], r2[k]], specs, tr, prefetch=pre, nsteps=nb,
                row_map=functools.partial(lambda i, s, nb: (s[1] * nb + i, 0), nb=nb), prev=upd_big[name])

    def finish_reduce(after):
        l, sems, q, zones = pending
        r2 = _exchange_chips_wait("reduce_wait_%d" % l, q, zones, sems[0], sems[1], after)
        update_layer(l, q, r2)

    token = None
    for l in range(L - 1, -1, -1):
        dx, big, small[l] = _layer_bwd(dx, saved[l], weights(l), params(l), alpha, dep=token)
        if pending is not None:
            finish_reduce(dx)
        send_a = jnp.concatenate([big["w_in"], big["w_a"].reshape(N_DEV, rs, D), big["w_b"].reshape(N_DEV, rs, D),
                                  big["w_o"].reshape(N_DEV, rs, D), big["w_down"].reshape(N_DEV, rd, D)], axis=1)
        send_b = big["w_up"].reshape(D, N_DEV, cu).transpose(1, 0, 2)
        r1a, r1b = _exchange_sibling([send_a, send_b])
        qs = [_pair_add(send_a, r1a, c_arr), _pair_add(send_b, r1b, c_arr)]
        if l > 0:
            s_send, s_recv, q_thru, zones, token = _exchange_chips_start("reduce_start_%d" % l, qs)
            pending = (l, (s_send, s_recv), list(q_thru), list(zones))
    dx0, _, dln0_g, dln0_b = _ln_bwd("ln0_bwd", x2d, dx, ln0_g)
    dlb_logits = _lb_bwd(lb_logits, jnp.concatenate([small[l]["lbs"] for l in range(L)], axis=0))

    small_l = {n: [small[l][n] for l in range(L)] for n in _SMALL if n != "lb_logits"}
    small_l["lb_logits"] = [dlb_logits[l] for l in range(L)]
    loss_pad = jnp.pad(loss_row, ((0, 0), (0, D - LANES)))
    part = jnp.concatenate([_pack_small(small_l, dln0_g, dln0_b, loss_pad, D, L)]
                           + [small[l]["w_dw"] for l in range(L)], axis=0)
    parts_all = _all_gather_small(part)
    n_small = part.shape[0] - L * CONV_HALO

    s_send, s_recv, q_thru, zones, _ = _exchange_chips_start("reduce_start_0", qs, after=parts_all)
    pending = (0, (s_send, s_recv), list(q_thru), list(zones))


    inputs = dict(b_in=(b_in, m_b_in, v_b_in), lb_logits=(lb_logits, m_lb_logits, v_lb_logits),
                  g_norm_w=(g_norm_w, m_g_norm_w, v_g_norm_w), b_dw=(b_dw, m_b_dw, v_b_dw),
                  conv_ln_g=(conv_ln_g, m_conv_ln_g, v_conv_ln_g), conv_ln_b=(conv_ln_b, m_conv_ln_b, v_conv_ln_b),
                  b_b=(b_b, m_b_b, v_b_b), ln1_g=(ln1_g, m_ln1_g, v_ln1_g), ln1_b=(ln1_b, m_ln1_b, v_ln1_b),
                  ln2_g=(ln2_g, m_ln2_g, v_ln2_g), ln2_b=(ln2_b, m_ln2_b, v_ln2_b))
    zero_row = jnp.zeros((1, D), F32)
    packed = [_pack_small({n: [inputs[n][i][l] for l in range(L)] for n in _SMALL},
                          (ln0_g, m_ln0_g, v_ln0_g)[i], (ln0_b, m_ln0_b, v_ln0_b)[i], zero_row, D, L)
              for i in range(3)]
    small_specs = [pl.BlockSpec((None, n_small, D), functools.partial(lambda i, d: (d, 0, 0), d=d))
                   for d in range(N_DEV)]
    s_out = _adamw("adamw_small", packed[0], packed[1], packed[2], [parts_all] * N_DEV, small_specs, n_small)
    s_g, n_rows = _unpack_small(s_out[0], D, L, hv)
    s_d, _ = _unpack_small(s_out[1], D, L, hv)
    s_m, _ = _unpack_small(s_out[2], D, L, hv)
    s_v, _ = _unpack_small(s_out[3], D, L, hv)
    loss = s_out[0][n_rows, 0]

    cw = w_dw.shape[2]
    dev = 4 * my_x + 2 * my_y + my_c
    tap_parts = lax.dynamic_slice_in_dim(parts_all[:, n_small:, :], dev * cw, cw, axis=2)
    tap_specs = [pl.BlockSpec((None, L * CONV_HALO, cw), functools.partial(lambda i, d: (d, 0, 0), d=d))
                 for d in range(N_DEV)]
    pad_t = lambda a: jnp.pad(a, ((0, 0), (0, CONV_HALO - CONV_WIDTH), (0, 0))).reshape(L * CONV_HALO, cw)
    t_out = _adamw("adamw_taps", pad_t(w_dw), pad_t(m_w_dw), pad_t(v_w_dw), [tap_parts] * N_DEV, tap_specs,
                   L * CONV_HALO)
    finish_reduce(t_out[0])
    upd = {n: [o.reshape(wmv[n][0].shape) for o in outs] for n, outs in upd_big.items()}
    upd["w_dw"] = [o.reshape(L, CONV_HALO, cw)[:, :CONV_WIDTH, :] for o in t_out]

    order = ["ln0_g", "ln0_b", "w_in", "b_in", "lb_logits", "g_norm_w", "w_a", "w_dw", "b_dw", "conv_ln_g",
             "conv_ln_b", "w_b", "b_b", "w_o", "ln1_g", "ln1_b", "w_up", "w_down", "ln2_g", "ln2_b"]
    small_sets = (s_g, s_d, s_m, s_v)
    outs = [loss, dx0.reshape(x.shape)]
    for i in range(4):
        for n in order:
            outs.append(upd[n][i] if n in upd else small_sets[i][n])
    return tuple(outs)
```

```python
import functools

import jax
import jax.numpy as jnp
from jax import lax
from jax.experimental import pallas as pl
from jax.experimental.pallas import tpu as pltpu

F32 = jnp.float32
MXU_DTYPE = jnp.bfloat16
ACT_DTYPE = jnp.bfloat16

LANES = 128
SUB = 8
N_DEV = 8
N_SEC = 8
CONV_WIDTH = 31
CONV_HALO = 32
HG_C = 16
LN_EPS = 1e-5
RMS_EPS = 1e-6
F_MIN = 1e-30
ADAM_LR = 0.001
ADAM_B1 = 0.9
ADAM_B2 = 0.999
ADAM_EPS = 1e-08
ADAM_WD = 0.01
ADAM_STEP = 10
VMEM_LIMIT = 56 * 1024 * 1024
MESH = pl.DeviceIdType.MESH

_NN = (((1,), (0,)), ((), ()))
_NT = (((1,), (1,)), ((), ()))
_TN = (((0,), (0,)), ((), ()))


_ANY = pl.BlockSpec(memory_space=pl.ANY)
_HBM = pl.BlockSpec(memory_space=pltpu.HBM)
_SEM = pl.BlockSpec(memory_space=pltpu.SEMAPHORE)
_EFFECT = pltpu.SideEffectType.DATAFLOW_SIDE_EFFECTING


def _cp(*sem):
    return pltpu.CompilerParams(dimension_semantics=tuple(sem), vmem_limit_bytes=VMEM_LIMIT)


def _pick(n, cands):
    for c in cands:
        if c <= n and n % c == 0:
            return c
    return n


def _silu(x):
    return x * jax.nn.sigmoid(x)


def _dsilu(x):
    s = jax.nn.sigmoid(x)
    return s * (1.0 + x * (1.0 - s))


def _matmul(name, a, b, *, dims, grid, a_spec, b_spec, out_shape, out_spec, acc_shape, nk,
            bias=None, bias_spec=None, add=None, add_spec=None, add_scale=1.0, dep=None):
    has_bias, has_add = bias is not None, add is not None
    kaxis = len(grid) - 1

    def body(*refs):
        a_ref, b_ref = refs[0], refs[1]
        pos = 2
        bias_ref = add_ref = None
        if has_bias:
            bias_ref = refs[pos]
            pos += 1
        if has_add:
            add_ref = refs[pos]
            pos += 1
        if dep is not None:
            pos += 1
        o_ref = refs[pos]
        acc_ref = refs[pos + 1] if nk > 1 else None

        part = lax.dot_general(a_ref[...].astype(MXU_DTYPE), b_ref[...].astype(MXU_DTYPE), dims,
                               preferred_element_type=F32)

        def finish(r):
            if has_bias:
                r = r + bias_ref[...]
            if has_add:
                r = r + add_scale * add_ref[...]
            o_ref[...] = r.astype(o_ref.dtype)

        if nk == 1:
            finish(part)
        else:
            k = pl.program_id(kaxis)

            @pl.when(k == 0)
            def _():
                acc_ref[...] = part

            @pl.when(k > 0)
            def _():
                acc_ref[...] += part

            @pl.when(k == nk - 1)
            def _():
                finish(acc_ref[...])

    ins, specs = [a, b], [a_spec, b_spec]
    if has_bias:
        ins.append(bias)
        specs.append(bias_spec)
    if has_add:
        ins.append(add)
        specs.append(add_spec)
    if dep is not None:
        ins.append(dep)
        specs.append(_ANY)
    sem =("parallel",) * (len(grid) - 1) + ("arbitrary",) if nk > 1 else ("parallel",) * len(grid)
    return pl.pallas_call(
        body, name=name, grid=grid, in_specs=specs, out_specs=out_spec, out_shape=out_shape,
        scratch_shapes=[pltpu.VMEM(acc_shape, F32)] if nk > 1 else [],
        compiler_params=_cp(*sem))(*ins)


def _mm_nn(name, a, b, out_dtype, bias=None):
    M, K = a.shape
    N = b.shape[1]
    tn = _pick(N, (512, 256, 128))
    tk = K if K <= 1024 else _pick(K, (1408, 1024, 512, 256, 128))
    nk = K // tk
    return _matmul(
        name, a, b, dims=_NN, grid=(N // tn, nk),
        a_spec=pl.BlockSpec((M, tk), lambda j, k: (0, k)),
        b_spec=pl.BlockSpec((tk, tn), lambda j, k: (k, j)),
        out_shape=jax.ShapeDtypeStruct((M, N), out_dtype),
        out_spec=pl.BlockSpec((M, tn), lambda j, k: (0, j)),
        acc_shape=(M, tn), nk=nk,
        bias=bias, bias_spec=None if bias is None else pl.BlockSpec((1, tn), lambda j, k: (0, j)))


def _mm_nt(name, a, b, out_dtype, add=None, add_scale=1.0):
    M, K = a.shape
    N = b.shape[0]
    tn = _pick(N, (512, 256, 128))
    tk = K if K <= 1024 else _pick(K, (1408, 1024, 512, 256, 128))
    nk = K // tk
    return _matmul(
        name, a, b, dims=_NT, grid=(N // tn, nk),
        a_spec=pl.BlockSpec((M, tk), lambda j, k: (0, k)),
        b_spec=pl.BlockSpec((tn, tk), lambda j, k: (j, k)),
        out_shape=jax.ShapeDtypeStruct((M, N), out_dtype),
        out_spec=pl.BlockSpec((M, tn), lambda j, k: (0, j)),
        acc_shape=(M, tn), nk=nk,
        add=add, add_spec=None if add is None else pl.BlockSpec((M, tn), lambda j, k: (0, j)),
        add_scale=add_scale)


def _mm_tn(name, a, b, out_dtype):
    K, M = a.shape
    N = b.shape[1]
    tm = _pick(M, (256, 128))
    return _matmul(
        name, a, b, dims=_TN, grid=(M // tm,),
        a_spec=pl.BlockSpec((K, tm), lambda i: (0, i)),
        b_spec=pl.BlockSpec((K, N), lambda i: (0, 0)),
        out_shape=jax.ShapeDtypeStruct((M, N), out_dtype),
        out_spec=pl.BlockSpec((tm, N), lambda i: (i, 0)),
        acc_shape=(tm, N), nk=1)


def _proj_in(x_bf, w_in, b_in, dep=None):
    T, D = x_bf.shape
    tn = _pick(D, (512, 256, 128))
    return _matmul(
        "proj_in", x_bf, w_in, dims=_NN, grid=(N_SEC, D // tn),
        a_spec=pl.BlockSpec((T, D), lambda s, j: (0, 0)),
        b_spec=pl.BlockSpec((None, D, tn), lambda s, j: (s, 0, j)),
        out_shape=jax.ShapeDtypeStruct((N_SEC, T, D), F32),
        out_spec=pl.BlockSpec((None, T, tn), lambda s, j: (s, 0, j)),
        acc_shape=(T, tn), nk=1,
        bias=b_in, bias_spec=pl.BlockSpec((None, 1, tn), lambda s, j: (s, 0, j)), dep=dep)


def _proj_in_dx(dh, w_in, add, add_scale):
    _, T, D = dh.shape
    tn = _pick(D, (512, 256, 128))
    return _matmul(
        "proj_in_dx", dh, w_in, dims=_NT, grid=(D // tn, N_SEC),
        a_spec=pl.BlockSpec((None, T, D), lambda j, s: (s, 0, 0)),
        b_spec=pl.BlockSpec((None, tn, D), lambda j, s: (s, j, 0)),
        out_shape=jax.ShapeDtypeStruct((T, D), F32),
        out_spec=pl.BlockSpec((T, tn), lambda j, s: (0, j)),
        acc_shape=(T, tn), nk=N_SEC,
        add=add, add_spec=pl.BlockSpec((T, tn), lambda j, s: (0, j)), add_scale=add_scale)


def _proj_in_dw(x_bf, dh):
    _, T, D = dh.shape
    tn = _pick(D, (512, 256, 128))
    return _matmul(
        "proj_in_dw", x_bf, dh, dims=_TN, grid=(N_SEC, D // tn),
        a_spec=pl.BlockSpec((T, D), lambda s, j: (0, 0)),
        b_spec=pl.BlockSpec((None, T, tn), lambda s, j: (s, 0, j)),
        out_shape=jax.ShapeDtypeStruct((N_SEC, D, D), ACT_DTYPE),
        out_spec=pl.BlockSpec((None, D, tn), lambda s, j: (s, 0, j)),
        acc_shape=(D, tn), nk=1)


def _ffn_up(x_bf, w_up):
    T, D = x_bf.shape
    F = w_up.shape[1] // 2
    tn = _pick(F, (256, 128))
    nb = F // tn
    return _matmul(
        "ffn_up", x_bf, w_up, dims=_NN, grid=(2, nb),
        a_spec=pl.BlockSpec((T, D), lambda p, j: (0, 0)),
        b_spec=pl.BlockSpec((D, tn), lambda p, j: (0, p * nb + j)),
        out_shape=jax.ShapeDtypeStruct((2, T, F), F32),
        out_spec=pl.BlockSpec((None, T, tn), lambda p, j: (p, 0, j)),
        acc_shape=(T, tn), nk=1)


def _ffn_up_dx(dup, w_up, add, add_scale):
    _, T, F = dup.shape
    D = w_up.shape[0]
    tn = _pick(D, (512, 256, 128))
    tk = _pick(F, (256, 128))
    nb = F // tk
    return _matmul(
        "ffn_up_dx", dup, w_up, dims=_NT, grid=(D // tn, 2 * nb),
        a_spec=pl.BlockSpec((None, T, tk), lambda j, k: (k // nb, 0, k % nb)),
        b_spec=pl.BlockSpec((tn, tk), lambda j, k: (j, k)),
        out_shape=jax.ShapeDtypeStruct((T, D), F32),
        out_spec=pl.BlockSpec((T, tn), lambda j, k: (0, j)),
        acc_shape=(T, tn), nk=2 * nb,
        add=add, add_spec=pl.BlockSpec((T, tn), lambda j, k: (0, j)), add_scale=add_scale)


def _ffn_up_dw(x_bf, dup):
    _, T, F = dup.shape
    D = x_bf.shape[1]
    tn = _pick(F, (256, 128))
    nb = F // tn
    return _matmul(
        "ffn_up_dw", x_bf, dup, dims=_TN, grid=(2, nb),
        a_spec=pl.BlockSpec((T, D), lambda p, j: (0, 0)),
        b_spec=pl.BlockSpec((None, T, tn), lambda p, j: (p, 0, j)),
        out_shape=jax.ShapeDtypeStruct((D, 2 * F), ACT_DTYPE),
        out_spec=pl.BlockSpec((D, tn), lambda p, j: (0, p * nb + j)),
        acc_shape=(D, tn), nk=1)


def _ln_fwd(name, a, res, alpha, g, b, dep=None):
    T, D = a.shape
    tr = _pick(T, (256, 128, 64, 32, 16))
    has_res = res is not None

    def body(*refs):
        if has_res:
            a_ref, r_ref, g_ref, b_ref = refs[:4]
            y_ref, yb_ref, z_ref = refs[-3:]
            z = alpha * a_ref[...] + r_ref[...]
            z_ref[...] = z
        else:
            a_ref, g_ref, b_ref = refs[:3]
            y_ref, yb_ref = refs[-2:]
            z = a_ref[...]
        mu = jnp.mean(z, axis=-1, keepdims=True)
        zc = z - mu
        var = jnp.mean(zc * zc, axis=-1, keepdims=True)
        y = zc * lax.rsqrt(var + LN_EPS) * g_ref[...] + b_ref[...]
        y_ref[...] = y
        yb_ref[...] = y.astype(ACT_DTYPE)

    row = pl.BlockSpec((tr, D), lambda i: (i, 0))
    vec = pl.BlockSpec((1, D), lambda i: (0, 0))
    ins = [a] + ([res] if has_res else []) + [g.reshape(1, D), b.reshape(1, D)]
    in_specs = [row] + ([row] if has_res else []) + [vec, vec]
    if dep is not None:
        ins.append(dep)
        in_specs.append(_ANY)
    out_shape = [jax.ShapeDtypeStruct((T, D), F32), jax.ShapeDtypeStruct((T, D), ACT_DTYPE)]
    if has_res:
        out_shape.append(jax.ShapeDtypeStruct((T, D), F32))
    return pl.pallas_call(
        body, name=name, grid=(T // tr,), in_specs=in_specs,
        out_specs=[row] * len(out_shape), out_shape=out_shape, compiler_params=_cp("parallel"))(*ins)


def _ln_bwd(name, z, dy, g, dep=None):
    T, D = z.shape
    tr = _pick(T, (256, 128, 64, 32, 16))

    def body(z_ref, dy_ref, g_ref, *rest):
        dz_ref, dzb_ref, dg_ref, db_ref = rest[-4:]

        @pl.when(pl.program_id(0) == 0)
        def _():
            dg_ref[...] = jnp.zeros_like(dg_ref)
            db_ref[...] = jnp.zeros_like(db_ref)

        zv = z_ref[...]
        dy_ = dy_ref[...]
        mu = jnp.mean(zv, axis=-1, keepdims=True)
        zc = zv - mu
        rstd = lax.rsqrt(jnp.mean(zc * zc, axis=-1, keepdims=True) + LN_EPS)
        xhat = zc * rstd
        dxh = dy_ * g_ref[...]
        dz = rstd * (dxh - jnp.mean(dxh, axis=-1, keepdims=True)
                     - xhat * jnp.mean(dxh * xhat, axis=-1, keepdims=True))
        dz_ref[...] = dz
        dzb_ref[...] = dz.astype(ACT_DTYPE)
        dg_ref[...] += jnp.sum(dy_ * xhat, axis=0, keepdims=True)
        db_ref[...] += jnp.sum(dy_, axis=0, keepdims=True)

    row = pl.BlockSpec((tr, D), lambda i: (i, 0))
    vec = pl.BlockSpec((1, D), lambda i: (0, 0))
    ins, in_specs = [z, dy, g.reshape(1, D)], [row, row, vec]
    if dep is not None:
        ins.append(dep)
        in_specs.append(_ANY)
    return pl.pallas_call(
        body, name=name, grid=(T // tr,), in_specs=in_specs, out_specs=[row, row, vec, vec],
        out_shape=[jax.ShapeDtypeStruct((T, D), F32), jax.ShapeDtypeStruct((T, D), ACT_DTYPE),
                   jax.ShapeDtypeStruct((1, D), F32), jax.ShapeDtypeStruct((1, D), F32)],
        compiler_params=_cp("arbitrary"))(*ins)


def _loss_fwd_bwd(y, target):
    T, D = y.shape
    tr = _pick(T, (256, 128, 64, 32, 16))

    def body(y_ref, t_ref, dy_ref, l_ref):
        @pl.when(pl.program_id(0) == 0)
        def _():
            l_ref[...] = jnp.zeros_like(l_ref)

        e = y_ref[...] - t_ref[...]
        dy_ref[...] = e * (1.0 / D)
        row = jnp.sum(e * e, axis=-1, keepdims=True) * (1.0 / D)
        l_ref[...] += 0.5 * jnp.sum(row, axis=0, keepdims=True)

    rowspec = pl.BlockSpec((tr, D), lambda i: (i, 0))
    return pl.pallas_call(
        body, name="loss", grid=(T // tr,), in_specs=[rowspec, rowspec],
        out_specs=[rowspec, pl.BlockSpec((1, LANES), lambda i: (0, 0))],
        out_shape=[jax.ShapeDtypeStruct((T, D), F32), jax.ShapeDtypeStruct((1, LANES), F32)],
        compiler_params=_cp("arbitrary"))(y, target)


def _gate_fwd(y_h, y_c, h):
    T, D = y_h.shape
    tr = _pick(T, (256, 128, 64, 32, 16))

    def body(yh_ref, yc_ref, gh_ref, gc_ref, m_ref):
        m = jax.nn.sigmoid(gh_ref[...]) * yh_ref[...] + jax.nn.sigmoid(gc_ref[...]) * yc_ref[...]
        m_ref[...] = m.astype(ACT_DTYPE)

    row = pl.BlockSpec((tr, D), lambda i: (i, 0))
    return pl.pallas_call(
        body, name="gate_fwd", grid=(T // tr,),
        in_specs=[row, row, pl.BlockSpec((None, tr, D), lambda i: (6, i, 0)),
                  pl.BlockSpec((None, tr, D), lambda i: (7, i, 0))],
        out_specs=row, out_shape=jax.ShapeDtypeStruct((T, D), ACT_DTYPE),
        compiler_params=_cp("parallel"))(y_h, y_c, h, h)


def _gate_bwd(dm, y_h, y_c, h):
    T, D = y_h.shape
    tr = _pick(T, (256, 128, 64, 32, 16))

    def body(dm_ref, yh_ref, yc_ref, gh_ref, gc_ref, dyh_ref, dyc_ref, dbb_ref, dh_ref):
        @pl.when(pl.program_id(0) == 0)
        def _():
            dbb_ref[...] = jnp.zeros_like(dbb_ref)

        dm_ = dm_ref[...]
        sh = jax.nn.sigmoid(gh_ref[...])
        sc = jax.nn.sigmoid(gc_ref[...])
        dyc = dm_ * sc
        dyh_ref[...] = (dm_ * sh).astype(ACT_DTYPE)
        dyc_ref[...] = dyc.astype(ACT_DTYPE)
        dbb_ref[...] += jnp.sum(dyc, axis=0, keepdims=True)
        dh_ref[0] = (dm_ * yh_ref[...] * sh * (1.0 - sh)).astype(ACT_DTYPE)
        dh_ref[1] = (dm_ * yc_ref[...] * sc * (1.0 - sc)).astype(ACT_DTYPE)

    row = pl.BlockSpec((tr, D), lambda i: (i, 0))
    return pl.pallas_call(
        body, name="gate_bwd", grid=(T // tr,),
        in_specs=[row, row, row, pl.BlockSpec((None, tr, D), lambda i: (6, i, 0)),
                  pl.BlockSpec((None, tr, D), lambda i: (7, i, 0))],
        out_specs=[row, row, pl.BlockSpec((1, D), lambda i: (0, 0)),
                   pl.BlockSpec((2, tr, D), lambda i: (3, i, 0))],
        out_shape=[jax.ShapeDtypeStruct((T, D), ACT_DTYPE), jax.ShapeDtypeStruct((T, D), ACT_DTYPE),
                   jax.ShapeDtypeStruct((1, D), F32), jax.ShapeDtypeStruct((N_SEC, T, D), ACT_DTYPE)],
        compiler_params=_cp("arbitrary"))(dm, y_h, y_c, h, h)


def _swiglu_fwd(up):
    _, T, F = up.shape
    tr = _pick(T, (128, 64, 32, 16))

    def body(up_ref, act_ref):
        act_ref[...] = (_silu(up_ref[0]) * up_ref[1]).astype(ACT_DTYPE)

    return pl.pallas_call(
        body, name="swiglu_fwd", grid=(T // tr,),
        in_specs=[pl.BlockSpec((2, tr, F), lambda i: (0, i, 0))],
        out_specs=pl.BlockSpec((tr, F), lambda i: (i, 0)),
        out_shape=jax.ShapeDtypeStruct((T, F), ACT_DTYPE), compiler_params=_cp("parallel"))(up)


def _swiglu_bwd(dact, up):
    _, T, F = up.shape
    tr = _pick(T, (128, 64, 32, 16))

    def body(da_ref, up_ref, dup_ref):
        da = da_ref[...]
        ug = up_ref[0]
        dup_ref[0] = (da * up_ref[1] * _dsilu(ug)).astype(ACT_DTYPE)
        dup_ref[1] = (da * _silu(ug)).astype(ACT_DTYPE)

    blk = pl.BlockSpec((2, tr, F), lambda i: (0, i, 0))
    return pl.pallas_call(
        body, name="swiglu_bwd", grid=(T // tr,),
        in_specs=[pl.BlockSpec((tr, F), lambda i: (i, 0)), blk], out_specs=blk,
        out_shape=jax.ShapeDtypeStruct((2, T, F), ACT_DTYPE), compiler_params=_cp("parallel"))(dact, up)


def _colsum(dh):
    S, T, D = dh.shape
    tr = _pick(T, (512, 256, 128, 64, 32, 16))

    def body(x_ref, o_ref):
        @pl.when(pl.program_id(1) == 0)
        def _():
            o_ref[...] = jnp.zeros_like(o_ref)

        o_ref[...] += jnp.sum(x_ref[...].astype(F32), axis=0, keepdims=True)

    return pl.pallas_call(
        body, name="colsum", grid=(S, T // tr),
        in_specs=[pl.BlockSpec((None, tr, D), lambda s, i: (s, i, 0))],
        out_specs=pl.BlockSpec((None, 1, D), lambda s, i: (s, 0, 0)),
        out_shape=jax.ShapeDtypeStruct((S, 1, D), F32), compiler_params=_cp("parallel", "arbitrary"))(dh)


def _lb_softmax(x):
    L = x.shape[0]
    rows = [x[l:l + 1] for l in range(L)]
    m = rows[0]
    for r in rows[1:]:
        m = jnp.maximum(m, r)
    e = [jnp.exp(r - m) for r in rows]
    s = e[0]
    for r in e[1:]:
        s = s + r
    return [r / s for r in e]


def _lb_fwd(lb_logits):
    L, D = lb_logits.shape

    def body(x_ref, o_ref):
        p = _lb_softmax(x_ref[...])
        run = jnp.zeros_like(p[0])
        for l in range(L):
            if l > 0:
                run = run + p[l]
            o_ref[pl.ds(l, 1), :] = run

    return pl.pallas_call(body, name="lb_fwd", out_shape=jax.ShapeDtypeStruct((L, D), F32))(lb_logits)


def _lb_bwd(lb_logits, dlbs):
    L, D = lb_logits.shape

    def body(x_ref, d_ref, o_ref):
        p = _lb_softmax(x_ref[...])
        d = d_ref[...]
        dp = [jnp.zeros_like(p[0]) for _ in range(L)]
        run = jnp.zeros_like(p[0])
        for j in range(L - 1, 0, -1):
            run = run + d[j:j + 1]
            dp[j] = run
        dot = dp[0] * p[0]
        for j in range(1, L):
            dot = dot + dp[j] * p[j]
        for j in range(L):
            o_ref[pl.ds(j, 1), :] = p[j] * (dp[j] - dot)

    return pl.pallas_call(body, name="lb_bwd", out_shape=jax.ShapeDtypeStruct((L, D), F32))(lb_logits, dlbs)


def _blk_cumsum(x, c, reverse=False):
    n = x.shape[0]
    pos = lax.broadcasted_iota(jnp.int32, x.shape, 0) % c
    s = 1
    while s < c:
        if reverse:
            shifted = pltpu.roll(x, n - s, 0)
            x = x + jnp.where(pos + s < c, shifted, 0.0)
        else:
            shifted = pltpu.roll(x, s, 0)
            x = x + jnp.where(pos >= s, shifted, 0.0)
        s *= 2
    return x


def _hgrn_prologue(q_ref, f_ref, lb_ref):
    lbv = lb_ref[...]
    z = f_ref[...]
    sig = jax.nn.sigmoid(z)
    one_m = 1.0 - lbv
    f = lbv + one_m * sig
    logf = jnp.log(jnp.maximum(f, F_MIN))
    k = one_m * jax.nn.sigmoid(-z)
    q = _silu(q_ref[...])
    return q, k, logf, f, sig, one_m


def _hgrn_fwd(h, lbs_l, gw):
    _, T, D = h.shape
    nh = D // LANES
    c = HG_C
    Tt = _pick(T, (128, 64, 32, 16))
    nb = Tt // c
    ng = c // SUB

    def body(q_ref, f_ref, i_ref, g_ref, lb_ref, gw_ref, o_ref, y_ref, sall_ref,
             st_ref, G_s, q_s, k_s, W_s, R_s, dS_s, o_s):
        @pl.when(pl.program_id(1) == 0)
        def _():
            st_ref[...] = jnp.zeros_like(st_ref)

        q, k, logf, _, _, _ = _hgrn_prologue(q_ref, f_ref, lb_ref)
        G_s[...] = _blk_cumsum(logf, c)
        q_s[...] = q
        k_s[...] = k
        ones = jnp.ones((LANES, LANES), MXU_DTYPE)
        rowid = lax.broadcasted_iota(jnp.int32, (SUB, LANES), 0)
        zero = jnp.zeros((SUB, LANES), F32)
        for bi in range(nb):
            r0 = bi * c
            glast = G_s[pl.ds(r0 + c - 1, 1), :]
            kd = k_s[pl.ds(r0, c), :] * jnp.exp(glast - G_s[pl.ds(r0, c), :])
            dS_s[bi] = lax.dot_general(i_ref[pl.ds(r0, c), :].astype(MXU_DTYPE), kd.astype(MXU_DTYPE), _TN,
                                       preferred_element_type=F32)
        st = st_ref[...]
        for bi in range(nb):
            sall_ref[bi] = st
            st = st * jnp.exp(G_s[pl.ds(bi * c + c - 1, 1), :]) + dS_s[bi]
        st_ref[...] = st
        for bi in range(nb):
            r0 = bi * c
            qd = q_s[pl.ds(r0, c), :] * jnp.exp(G_s[pl.ds(r0, c), :])
            o_s[pl.ds(r0, c), :] = lax.dot_general(qd.astype(MXU_DTYPE), sall_ref[bi].astype(MXU_DTYPE), _NT,
                                                   preferred_element_type=F32)
        for bi in range(nb):
            r0 = bi * c
            w0 = bi * c * c
            Gg = [G_s[pl.ds(r0 + gi * SUB, SUB), :] for gi in range(ng)]
            qg = [q_s[pl.ds(r0 + gi * SUB, SUB), :] for gi in range(ng)]
            for s in range(c):
                gs = G_s[pl.ds(r0 + s, 1), :]
                ks = k_s[pl.ds(r0 + s, 1), :]
                parts = []
                for gi in range(ng):
                    if gi < s // SUB:
                        parts.append(zero)
                        continue
                    e = jnp.exp(jnp.minimum(Gg[gi] - gs, 0.0))
                    if gi == s // SUB:
                        e = jnp.where(rowid >= s - gi * SUB, e, 0.0)
                    parts.append(e * qg[gi] * ks)
                W_s[pl.ds(w0 + s * c, c), :] = jnp.concatenate(parts, axis=0).astype(MXU_DTYPE)
        R_s[...] = jnp.dot(W_s[...], ones, preferred_element_type=F32)
        for bi in range(nb):
            r0 = bi * c
            w0 = bi * c * c
            acc = [o_s[pl.ds(r0 + gi * SUB, SUB), :] for gi in range(ng)]
            for s in range(c):
                vs = i_ref[pl.ds(r0 + s, 1), :]
                for gi in range(s // SUB, ng):
                    acc[gi] = acc[gi] + R_s[pl.ds(w0 + s * c + gi * SUB, SUB), :] * vs
            o_s[pl.ds(r0, c), :] = jnp.concatenate(acc, axis=0)
        o = o_s[...]
        n = o * lax.rsqrt(jnp.mean(o * o, axis=-1, keepdims=True) + RMS_EPS)
        o_ref[...] = o
        y_ref[...] = (n * gw_ref[...] * _silu(g_ref[...])).astype(ACT_DTYPE)

    def sec(s):
        return pl.BlockSpec((None, Tt, LANES), lambda hd, i: (s, i, hd))

    col = pl.BlockSpec((Tt, LANES), lambda hd, i: (i, hd))
    return pl.pallas_call(
        body, name="hgrn_fwd", grid=(nh, T // Tt),
        in_specs=[sec(0), sec(1), sec(2), sec(3), pl.BlockSpec((1, LANES), lambda hd, i: (0, hd)),
                  pl.BlockSpec((1, LANES), lambda hd, i: (0, 0))],
        out_specs=[col, col, pl.BlockSpec((nb, None, LANES, LANES), lambda hd, i: (i, hd, 0, 0))],
        out_shape=[jax.ShapeDtypeStruct((T, D), F32), jax.ShapeDtypeStruct((T, D), ACT_DTYPE),
                   jax.ShapeDtypeStruct((T // c, nh, LANES, LANES), F32)],
        scratch_shapes=[pltpu.VMEM((LANES, LANES), F32), pltpu.VMEM((Tt, LANES), F32),
                        pltpu.VMEM((Tt, LANES), F32), pltpu.VMEM((Tt, LANES), F32),
                        pltpu.VMEM((nb * c * c, LANES), MXU_DTYPE), pltpu.VMEM((nb * c * c, LANES), F32),
                        pltpu.VMEM((nb, LANES, LANES), F32), pltpu.VMEM((Tt, LANES), F32)],
        compiler_params=_cp("parallel", "arbitrary"))(h, h, h, h, lbs_l, gw)


def _hgrn_bwd(h, lbs_l, gw, o_pre, st_all, dy, dh):
    _, T, D = h.shape
    nh = D // LANES
    c = HG_C
    Tt = _pick(T, (128, 64, 32, 16))
    nb = Tt // c
    ng = c // SUB
    nT = T // Tt

    def body(q_ref, f_ref, i_ref, g_ref, lb_ref, gw_ref, o_ref, sall_ref, dy_ref, dh_in_ref,
             dh_ref, dlb_ref, dgw_ref,
             dst_ref, G_s, q_s, k_s, do_s, E_s, WP_s, dq_s, dk_s, dv_s, dG_s,
             R_s, dS_s, dstA_s, dqd_s, dkd_s, dvi_s, da_s):
        del dh_in_ref
        hd, ti = pl.program_id(0), pl.program_id(1)

        @pl.when(ti == 0)
        def _():
            dst_ref[...] = jnp.zeros_like(dst_ref)
            dlb_ref[...] = jnp.zeros_like(dlb_ref)

        @pl.when((ti == 0) & (hd == 0))
        def _():
            dgw_ref[...] = jnp.zeros_like(dgw_ref)

        q, k, logf, f, sig, one_m = _hgrn_prologue(q_ref, f_ref, lb_ref)
        G_s[...] = _blk_cumsum(logf, c)
        q_s[...] = q
        k_s[...] = k

        o = o_ref[...]
        gr = g_ref[...]
        dy_ = dy_ref[...]
        rr = lax.rsqrt(jnp.mean(o * o, axis=-1, keepdims=True) + RMS_EPS)
        n = o * rr
        sg = _silu(gr)
        gwv = gw_ref[...]
        dh_ref[3] = (dy_ * n * gwv * _dsilu(gr)).astype(ACT_DTYPE)
        dgw_ref[...] += jnp.sum(dy_ * n * sg, axis=0, keepdims=True)
        dn = dy_ * gwv * sg
        do_s[...] = rr * (dn - n * jnp.mean(dn * n, axis=-1, keepdims=True))

        ones = jnp.ones((LANES, LANES), MXU_DTYPE)
        rowid = lax.broadcasted_iota(jnp.int32, (SUB, LANES), 0)
        rowid_c = lax.broadcasted_iota(jnp.int32, (c, LANES), 0)
        zero = jnp.zeros((SUB, LANES), F32)
        cc = c * c
        for bi in range(nb):
            r0 = bi * c
            qd = q_s[pl.ds(r0, c), :] * jnp.exp(G_s[pl.ds(r0, c), :])
            dS_s[bi] = lax.dot_general(do_s[pl.ds(r0, c), :].astype(MXU_DTYPE), qd.astype(MXU_DTYPE), _TN,
                                       preferred_element_type=F32)
        dst = dst_ref[...]
        for bi in range(nb - 1, -1, -1):
            dstA_s[bi] = dst
            dst = dst * jnp.exp(G_s[pl.ds(bi * c + c - 1, 1), :]) + dS_s[bi]
        dst_ref[...] = dst
        for bi in range(nb):
            r0 = bi * c
            glast = G_s[pl.ds(r0 + c - 1, 1), :]
            kd = k_s[pl.ds(r0, c), :] * jnp.exp(glast - G_s[pl.ds(r0, c), :])
            st = sall_ref[bi]
            dstb = dstA_s[bi]
            dst_m = dstb.astype(MXU_DTYPE)
            dqd_s[pl.ds(r0, c), :] = lax.dot_general(do_s[pl.ds(r0, c), :].astype(MXU_DTYPE), st.astype(MXU_DTYPE),
                                                     _NN, preferred_element_type=F32)
            dkd_s[pl.ds(r0, c), :] = lax.dot_general(i_ref[pl.ds(r0, c), :].astype(MXU_DTYPE), dst_m, _NN,
                                                     preferred_element_type=F32)
            dvi_s[pl.ds(r0, c), :] = lax.dot_general(kd.astype(MXU_DTYPE), dst_m, _NT,
                                                     preferred_element_type=F32)
            da_s[pl.ds(bi * SUB, 1), :] = jnp.sum(dstb * st, axis=0, keepdims=True)
        for bi in range(nb):
            r0 = bi * c
            e0, w0 = bi * cc, bi * 2 * cc
            Gg = [G_s[pl.ds(r0 + gi * SUB, SUB), :] for gi in range(ng)]
            kg = [k_s[pl.ds(r0 + gi * SUB, SUB), :] for gi in range(ng)]
            vg = [i_ref[pl.ds(r0 + gi * SUB, SUB), :] for gi in range(ng)]
            for t in range(c):
                gt = G_s[pl.ds(r0 + t, 1), :]
                qt = q_s[pl.ds(r0 + t, 1), :]
                dot_ = do_s[pl.ds(r0 + t, 1), :]
                ep, wp, pp = [], [], []
                for gi in range(ng):
                    if gi > t // SUB:
                        ep.append(zero)
                        wp.append(zero)
                        pp.append(zero)
                        continue
                    e = jnp.exp(jnp.minimum(gt - Gg[gi], 0.0))
                    if gi == t // SUB:
                        e = jnp.where(rowid <= t - gi * SUB, e, 0.0)
                    ep.append(e)
                    wp.append(e * kg[gi] * qt)
                    pp.append(vg[gi] * dot_)
                E_s[pl.ds(e0 + t * c, c), :] = jnp.concatenate(ep, axis=0)
                WP_s[pl.ds(w0 + t * c, c), :] = jnp.concatenate(wp, axis=0).astype(MXU_DTYPE)
                WP_s[pl.ds(w0 + cc + t * c, c), :] = jnp.concatenate(pp, axis=0).astype(MXU_DTYPE)
        R_s[...] = jnp.dot(WP_s[...], ones, preferred_element_type=F32)
        for bi in range(nb):
            r0 = bi * c
            e0, w0 = bi * cc, bi * 2 * cc
            kg = [k_s[pl.ds(r0 + gi * SUB, SUB), :] for gi in range(ng)]
            dk_g = [zero] * ng
            dv_g = [zero] * ng
            dq_g = [zero] * ng
            for t in range(c):
                qt = q_s[pl.ds(r0 + t, 1), :]
                dot_ = do_s[pl.ds(r0 + t, 1), :]
                tot = None
                for gi in range(t // SUB + 1):
                    lo = t * c + gi * SUB
                    dae = R_s[pl.ds(w0 + cc + lo, SUB), :] * E_s[pl.ds(e0 + lo, SUB), :]
                    z = dae * kg[gi]
                    tot = z if tot is None else tot + z
                    dk_g[gi] = dk_g[gi] + dae * qt
                    dv_g[gi] = dv_g[gi] + R_s[pl.ds(w0 + lo, SUB), :] * dot_
                gt_ = t // SUB
                dq_g[gt_] = jnp.where(rowid == t - gt_ * SUB, jnp.sum(tot, axis=0, keepdims=True), dq_g[gt_])
            dq_i = jnp.concatenate(dq_g, axis=0)
            dk_i = jnp.concatenate(dk_g, axis=0)
            dv_i = jnp.concatenate(dv_g, axis=0)
            Gb = G_s[pl.ds(r0, c), :]
            qb = q_s[pl.ds(r0, c), :]
            kb = k_s[pl.ds(r0, c), :]
            glast = G_s[pl.ds(r0 + c - 1, 1), :]
            eg = jnp.exp(Gb)
            egl = jnp.exp(glast - Gb)
            dqd = dqd_s[pl.ds(r0, c), :]
            dkd = dkd_s[pl.ds(r0, c), :]
            dq_s[pl.ds(r0, c), :] = dqd * eg + dq_i
            dk_s[pl.ds(r0, c), :] = dkd * egl + dk_i
            dv_s[pl.ds(r0, c), :] = dvi_s[pl.ds(r0, c), :] + dv_i
            dkdkd = dkd * kb * egl
            dG = dqd * qb * eg + qb * dq_i - kb * dk_i - dkdkd
            dglast = jnp.sum(dkdkd, axis=0, keepdims=True) + da_s[pl.ds(bi * SUB, 1), :] * jnp.exp(glast)
            dG_s[pl.ds(r0, c), :] = dG + jnp.where(rowid_c == c - 1, dglast, 0.0)

        dlogf = _blk_cumsum(dG_s[...], c, reverse=True)
        df = jnp.where(f > F_MIN, dlogf / f, 0.0)
        dk = dk_s[...]
        dh_ref[0] = (dq_s[...] * _dsilu(q_ref[...])).astype(ACT_DTYPE)
        dh_ref[1] = ((df - dk) * one_m * sig * (1.0 - sig)).astype(ACT_DTYPE)
        dh_ref[2] = dv_s[...].astype(ACT_DTYPE)
        dlb_ref[...] += jnp.sum((df - dk) * (1.0 - sig), axis=0, keepdims=True)

    def sec(s):
        return pl.BlockSpec((None, Tt, LANES), lambda hd, i: (s, nT - 1 - i, hd))

    col = pl.BlockSpec((Tt, LANES), lambda hd, i: (nT - 1 - i, hd))
    tile = pltpu.VMEM((Tt, LANES), F32)
    return pl.pallas_call(
        body, name="hgrn_bwd", grid=(nh, nT),
        in_specs=[sec(0), sec(1), sec(2), sec(3), pl.BlockSpec((1, LANES), lambda hd, i: (0, hd)),
                  pl.BlockSpec((1, LANES), lambda hd, i: (0, 0)), col,
                  pl.BlockSpec((nb, None, LANES, LANES), lambda hd, i: (nT - 1 - i, hd, 0, 0)), col,
                  pl.BlockSpec(memory_space=pl.ANY)],
        out_specs=[pl.BlockSpec((4, Tt, LANES), lambda hd, i: (0, nT - 1 - i, hd)),
                   pl.BlockSpec((1, LANES), lambda hd, i: (0, hd)),
                   pl.BlockSpec((1, LANES), lambda hd, i: (0, 0))],
        out_shape=[jax.ShapeDtypeStruct(dh.shape, dh.dtype), jax.ShapeDtypeStruct((1, D), F32),
                   jax.ShapeDtypeStruct((1, LANES), F32)],
        scratch_shapes=[pltpu.VMEM((LANES, LANES), F32), tile, tile, tile, tile,
                        pltpu.VMEM((nb * c * c, LANES), F32), pltpu.VMEM((2 * nb * c * c, LANES), MXU_DTYPE),
                        tile, tile, tile, tile,
                        pltpu.VMEM((2 * nb * c * c, LANES), F32), pltpu.VMEM((nb, LANES, LANES), F32),
                        pltpu.VMEM((nb, LANES, LANES), F32), tile, tile, tile, pltpu.VMEM((nb * SUB, LANES), F32)],
        input_output_aliases={9: 0},
        compiler_params=_cp("arbitrary", "arbitrary"))(h, h, h, h, lbs_l, gw, o_pre, st_all, dy, dh)


def _conv_fwd(h, w_dw, b_dw, ln_g, ln_b):
    _, T, D = h.shape
    Tt = _pick(T, (256, 128, 64, 32))
    hb = Tt // CONV_HALO
    off = CONV_HALO - (CONV_WIDTH - 1)

    def body(a_ref, b_ref, ap_ref, bp_ref, w_ref, bd_ref, g_ref, be_ref, yc_ref, y_ref, U_s):
        first = pl.program_id(0) == 0
        up = ap_ref[...] * jax.nn.sigmoid(bp_ref[...])
        U_s[pl.ds(0, CONV_HALO), :] = jnp.where(first, 0.0, up)
        U_s[pl.ds(CONV_HALO, Tt), :] = a_ref[...] * jax.nn.sigmoid(b_ref[...])
        for cb in range(D // LANES):
            cs = pl.ds(cb * LANES, LANES)
            acc = jnp.zeros((Tt, LANES), F32)
            for j in range(CONV_WIDTH):
                acc = acc + w_ref[pl.ds(j, 1), cs] * U_s[pl.ds(off + j, Tt), cs]
            yc_ref[:, cs] = acc + bd_ref[:, cs]
        yc = yc_ref[...]
        mu = jnp.mean(yc, axis=-1, keepdims=True)
        zc = yc - mu
        var = jnp.mean(zc * zc, axis=-1, keepdims=True)
        ln = zc * lax.rsqrt(var + LN_EPS) * g_ref[...] + be_ref[...]
        y_ref[...] = _silu(ln).astype(ACT_DTYPE)

    def main(s):
        return pl.BlockSpec((None, Tt, D), lambda i: (s, i, 0))

    def prev(s):
        return pl.BlockSpec((None, CONV_HALO, D), lambda i: (s, jnp.maximum(i * hb - 1, 0), 0))

    row = pl.BlockSpec((Tt, D), lambda i: (i, 0))
    vec = pl.BlockSpec((1, D), lambda i: (0, 0))
    return pl.pallas_call(
        body, name="conv_fwd", grid=(T // Tt,),
        in_specs=[main(4), main(5), prev(4), prev(5), pl.BlockSpec((CONV_HALO, D), lambda i: (0, 0)),
                  vec, vec, vec],
        out_specs=[row, row],
        out_shape=[jax.ShapeDtypeStruct((T, D), F32), jax.ShapeDtypeStruct((T, D), ACT_DTYPE)],
        scratch_shapes=[pltpu.VMEM((CONV_HALO + Tt, D), F32)],
        compiler_params=_cp("parallel"))(h, h, h, h, w_dw, b_dw, ln_g, ln_b)


def _conv_bwd(h, w_dw, ln_g, ln_b, yc, dy, dh):
    _, T, D = h.shape
    Tt = _pick(T, (256, 128, 64, 32))
    hb = Tt // CONV_HALO
    nT = T // Tt
    nhb = T // CONV_HALO
    off = CONV_HALO - (CONV_WIDTH - 1)

    def body(a_ref, b_ref, ap_ref, bp_ref, w_ref, g_ref, be_ref, yc_ref, ycn_ref, dy_ref, dyn_ref, dh_in_ref,
             dh_ref, dw_ref, dbd_ref, dg_ref, dbe_ref, U_s, DY_s, du_s):
        del dh_in_ref
        i = pl.program_id(0)

        @pl.when(i == 0)
        def _():
            dw_ref[...] = jnp.zeros_like(dw_ref)
            dbd_ref[...] = jnp.zeros_like(dbd_ref)
            dg_ref[...] = jnp.zeros_like(dg_ref)
            dbe_ref[...] = jnp.zeros_like(dbe_ref)

        gv = g_ref[...]
        bev = be_ref[...]

        def ln_silu_bwd(ycv, dyv):
            mu = jnp.mean(ycv, axis=-1, keepdims=True)
            zc = ycv - mu
            rstd = lax.rsqrt(jnp.mean(zc * zc, axis=-1, keepdims=True) + LN_EPS)
            xhat = zc * rstd
            dln = dyv * _dsilu(xhat * gv + bev)
            dxh = dln * gv
            dyc = rstd * (dxh - jnp.mean(dxh, axis=-1, keepdims=True)
                          - xhat * jnp.mean(dxh * xhat, axis=-1, keepdims=True))
            return dyc, dln, xhat

        dyc, dln, xhat = ln_silu_bwd(yc_ref[...], dy_ref[...])
        dg_ref[...] += jnp.sum(dln * xhat, axis=0, keepdims=True)
        dbe_ref[...] += jnp.sum(dln, axis=0, keepdims=True)
        dbd_ref[...] += jnp.sum(dyc, axis=0, keepdims=True)
        DY_s[pl.ds(0, Tt), :] = dyc
        dycn, _, _ = ln_silu_bwd(ycn_ref[...], dyn_ref[...])
        DY_s[pl.ds(Tt, CONV_HALO), :] = jnp.where(i == nT - 1, 0.0, dycn)

        sb = jax.nn.sigmoid(b_ref[...])
        av = a_ref[...]
        up = ap_ref[...] * jax.nn.sigmoid(bp_ref[...])
        U_s[pl.ds(0, CONV_HALO), :] = jnp.where(i == 0, 0.0, up)
        U_s[pl.ds(CONV_HALO, Tt), :] = av * sb

        for cb in range(D // LANES):
            cs = pl.ds(cb * LANES, LANES)
            dyb = DY_s[pl.ds(0, Tt), cs]
            acc = jnp.zeros((Tt, LANES), F32)
            for j in range(CONV_WIDTH):
                acc = acc + w_ref[pl.ds(j, 1), cs] * DY_s[pl.ds(CONV_WIDTH - 1 - j, Tt), cs]
                dw_ref[pl.ds(j, 1), cs] += jnp.sum(dyb * U_s[pl.ds(off + j, Tt), cs], axis=0, keepdims=True)
            du_s[:, cs] = acc
        du = du_s[...]
        dh_ref[0] = (du * sb).astype(ACT_DTYPE)
        dh_ref[1] = (du * av * sb * (1.0 - sb)).astype(ACT_DTYPE)

    def main(s):
        return pl.BlockSpec((None, Tt, D), lambda i: (s, i, 0))

    def prev(s):
        return pl.BlockSpec((None, CONV_HALO, D), lambda i: (s, jnp.maximum(i * hb - 1, 0), 0))

    row = pl.BlockSpec((Tt, D), lambda i: (i, 0))
    nxt = pl.BlockSpec((CONV_HALO, D), lambda i: (jnp.minimum((i + 1) * hb, nhb - 1), 0))
    vec = pl.BlockSpec((1, D), lambda i: (0, 0))
    wspec = pl.BlockSpec((CONV_HALO, D), lambda i: (0, 0))
    return pl.pallas_call(
        body, name="conv_bwd", grid=(nT,),
        in_specs=[main(4), main(5), prev(4), prev(5), wspec, vec, vec, row, nxt, row, nxt,
                  pl.BlockSpec(memory_space=pl.ANY)],
        out_specs=[pl.BlockSpec((2, Tt, D), lambda i: (2, i, 0)), wspec, vec, vec, vec],
        out_shape=[jax.ShapeDtypeStruct(dh.shape, dh.dtype), jax.ShapeDtypeStruct((CONV_HALO, D), F32),
                   jax.ShapeDtypeStruct((1, D), F32), jax.ShapeDtypeStruct((1, D), F32),
                   jax.ShapeDtypeStruct((1, D), F32)],
        scratch_shapes=[pltpu.VMEM((CONV_HALO + Tt, D), F32), pltpu.VMEM((Tt + CONV_HALO, D), F32),
                        pltpu.VMEM((Tt, D), F32)],
        input_output_aliases={11: 0},
        compiler_params=_cp("arbitrary"))(h, h, h, h, w_dw, ln_g, ln_b, yc, yc, dy, dy, dh)


def _adamw(name, w, m, v, parts, part_specs, tr, prefetch=None, nsteps=None, row_map=None, prev=None):
    R, C = w.shape
    bc1 = 1.0 - ADAM_B1 ** ADAM_STEP
    bc2 = 1.0 - ADAM_B2 ** ADAM_STEP
    npart = len(parts)
    npre = 0 if prefetch is None else 1
    nprev = 0 if prev is None else 4

    def body(*refs):
        refs = refs[npre:]
        w_ref, m_ref, v_ref = refs[:3]
        p_refs = refs[3:3 + npart]
        g_ref, d_ref, mo_ref, vo_ref = refs[3 + npart + nprev:]
        g = p_refs[0][...].astype(F32)
        for p in p_refs[1:]:
            g = g + p[...].astype(F32)
        wv = w_ref[...]
        mn = ADAM_B1 * m_ref[...] + (1.0 - ADAM_B1) * g
        vn = ADAM_B2 * v_ref[...] + (1.0 - ADAM_B2) * (g * g)
        m_hat = mn / bc1
        v_hat = vn / bc2
        g_ref[...] = g
        d_ref[...] = -ADAM_LR * (m_hat / (jnp.sqrt(v_hat) + ADAM_EPS) + ADAM_WD * wv)
        mo_ref[...] = mn
        vo_ref[...] = vn

    if row_map is None:
        row_map = (lambda i: (i, 0)) if prefetch is None else (lambda i, s: (i, 0))
    row = pl.BlockSpec((tr, C), row_map)
    out = jax.ShapeDtypeStruct((R, C), F32)
    gs = pltpu.PrefetchScalarGridSpec(
        num_scalar_prefetch=npre, grid=(R // tr if nsteps is None else nsteps,),
        in_specs=[row, row, row] + list(part_specs) + [_ANY] * nprev, out_specs=[row] * 4)
    args = ([prefetch] if npre else []) + [w, m, v] + list(parts) + (list(prev) if nprev else [])
    first_prev = npre + 3 + npart
    return pl.pallas_call(body, name=name, grid_spec=gs, out_shape=[out] * 4,
                          input_output_aliases={first_prev + i: i for i in range(nprev)},
                          compiler_params=_cp("parallel"))(*args)


def _pair_add(p, r1, my_c):
    _, R, C = r1.shape
    tr = max(t for t in range(16, 1025, 16) if R % t == 0)

    def body(c_ref, p_ref, r_ref, q_ref):
        del c_ref
        q_ref[...] = (p_ref[...].astype(F32) + r_ref[...].astype(F32)).astype(q_ref.dtype)

    gs = pltpu.PrefetchScalarGridSpec(
        num_scalar_prefetch=1, grid=(4, R // tr),
        in_specs=[pl.BlockSpec((None, tr, C), lambda j, i, c: (2 * j + c[0], i, 0)),
                  pl.BlockSpec((None, tr, C), lambda j, i, c: (j, i, 0))],
        out_specs=pl.BlockSpec((None, tr, C), lambda j, i, c: (j, i, 0)))
    return pl.pallas_call(body, name="pair_add", grid_spec=gs, out_shape=jax.ShapeDtypeStruct(r1.shape, r1.dtype),
                          compiler_params=_cp("parallel", "parallel"))(my_c, p, r1)


def _place():
    x, y, c = lax.axis_index("x"), lax.axis_index("y"), lax.axis_index("c")
    chips = [(1 - x, y), (x, 1 - y), (1 - x, 1 - y)]
    return x, y, c, chips


def _hbm(a):
    return pltpu.with_memory_space_constraint(a, pltpu.HBM)


def _gather_targets():
    x, y, c, chips = _place()
    return 4 * x + 2 * y + c, [(x, y, 1 - c)] + [(*chip, c) for chip in chips]


def _gather_start(name, shards, zones, after=None):
    n = len(shards)
    lands = [_hbm(z) for z in zones]
    n_in = 2 * n + (0 if after is None else 1)

    def body(*refs):
        srcs, zones = refs[:n], refs[n:2 * n]
        send, recv, token = refs[n_in], refs[n_in + 1], refs[-1]
        mine, targets = _gather_targets()
        for a in range(n):
            for k, to in enumerate(targets):
                pltpu.make_async_remote_copy(
                    src_ref=srcs[a], dst_ref=zones[a].at[mine], send_sem=send.at[4 * a + k],
                    recv_sem=recv.at[4 * a + k], device_id=to, device_id_type=MESH).start()
        token[...] = jnp.zeros_like(token)

    sem = pltpu.SemaphoreType.DMA((4 * n,))
    out_shape = ([sem, sem] + [pltpu.HBM(s.shape, s.dtype) for s in shards]
                 + [pltpu.HBM(z.shape, z.dtype) for z in lands] + [jax.ShapeDtypeStruct((8, LANES), F32)])
    outs = pl.pallas_call(
        body, name=name, out_shape=out_shape, in_specs=[_HBM] * (2 * n) + ([] if after is None else [_ANY]),
        out_specs=[_SEM, _SEM] + [_HBM] * (2 * n) + [pl.BlockSpec(memory_space=pltpu.VMEM)],
        input_output_aliases={i: 2 + i for i in range(2 * n)},
        compiler_params=pltpu.CompilerParams(has_side_effects=_EFFECT))(
            *[_hbm(s) for s in shards], *lands, *([] if after is None else [after]))
    return outs[0], outs[1], list(outs[2:2 + n]), list(outs[2 + n:2 + 2 * n]), outs[-1]


def _gather_wait(name, shards, zones, send, recv, after):
    per = len(shards)

    def body(*refs):
        srcs, lz = refs[:per], refs[per:2 * per]
        send_s, recv_s = refs[2 * per], refs[2 * per + 1]
        mine, targets = _gather_targets()
        for a in range(per):
            for k, to in enumerate(targets):
                cp = pltpu.make_async_remote_copy(
                    src_ref=srcs[a], dst_ref=lz[a].at[mine], send_sem=send_s.at[4 * a + k],
                    recv_sem=recv_s.at[4 * a + k], device_id=to, device_id_type=MESH)
                cp.wait_send()
                cp.wait_recv()

    outs = pl.pallas_call(
        body, name=name, out_shape=[pltpu.HBM(s.shape, s.dtype) for s in shards + zones],
        in_specs=[_HBM] * (2 * per) + [_SEM, _SEM, _ANY], out_specs=[_HBM] * (2 * per),
        input_output_aliases={i: i for i in range(2 * per)},
        compiler_params=pltpu.CompilerParams(has_side_effects=_EFFECT))(*shards, *zones, send, recv, after)
    return outs[:per], outs[per:]


def _gather_finish(zones):
    n = len(zones)

    def body(*refs):
        lz = refs[n:2 * n]
        send_sems, recv_sems = refs[2 * n:]
        x, y, c, chips = _place()

        def fwd(a, j, pc):
            cx, cy = chips[j]
            blk = lz[a].at[4 * cx + 2 * cy + pc]
            return pltpu.make_async_remote_copy(
                src_ref=blk, dst_ref=blk, send_sem=send_sems.at[3 * a + j], recv_sem=recv_sems.at[3 * a + j],
                device_id=(x, y, 1 - c), device_id_type=MESH)

        sends = [fwd(a, j, c) for a in range(n) for j in range(3)]
        for cp in sends:
            cp.start()
        for a in range(n):
            for j in range(3):
                fwd(a, j, 1 - c).wait_recv()
        for cp in sends:
            cp.wait_send()

    return pl.pallas_call(
        body, name="gather_finish", out_shape=[jax.ShapeDtypeStruct(z.shape, z.dtype) for z in zones],
        in_specs=[_ANY] * n, out_specs=[_ANY] * n, input_output_aliases={a: a for a in range(n)},
        scratch_shapes=[pltpu.SemaphoreType.DMA((3 * n,)), pltpu.SemaphoreType.DMA((3 * n,))])(*zones)


def _place_own(shard, dev):
    R, C = shard.shape
    tr = max(t for t in range(16, 1025, 16) if R % t == 0)

    def body(d_ref, s_ref, z_ref):
        del d_ref
        z_ref[...] = s_ref[...]

    gs = pltpu.PrefetchScalarGridSpec(
        num_scalar_prefetch=1, grid=(R // tr,), in_specs=[pl.BlockSpec((tr, C), lambda i, d: (i, 0))],
        out_specs=pl.BlockSpec((None, tr, C), lambda i, d: (d[0], i, 0)))
    return pl.pallas_call(body, name="place_own", grid_spec=gs,
                          out_shape=jax.ShapeDtypeStruct((N_DEV, R, C), shard.dtype),
                          compiler_params=_cp("parallel"))(dev, shard)


def _exchange_sibling(bufs):
    n_arr = len(bufs)

    def body(*refs):
        srcs, outs = refs[:n_arr], refs[n_arr:2 * n_arr]
        send_sems, recv_sems = refs[2 * n_arr:]
        x, y, c, _ = _place()
        copies = []
        for n in range(n_arr):
            for j in range(4):
                copies.append(pltpu.make_async_remote_copy(
                    src_ref=srcs[n].at[2 * j + 1 - c], dst_ref=outs[n].at[j],
                    send_sem=send_sems.at[4 * n + j], recv_sem=recv_sems.at[4 * n + j],
                    device_id=(x, y, 1 - c), device_id_type=MESH))
        for cp in copies:
            cp.start()
        for cp in copies:
            cp.wait()

    return pl.pallas_call(
        body, name="exchange_sibling",
        out_shape=[jax.ShapeDtypeStruct((4,) + b.shape[1:], b.dtype) for b in bufs],
        in_specs=[_ANY] * n_arr, out_specs=[_ANY] * n_arr,
        scratch_shapes=[pltpu.SemaphoreType.DMA((4 * n_arr,)), pltpu.SemaphoreType.DMA((4 * n_arr,))])(*bufs)


def _chip_copies(srcs, zones, send, recv):
    _, _, c, chips = _place()
    return [pltpu.make_async_remote_copy(
        src_ref=srcs[n].at[2 * cx + cy], dst_ref=zones[n].at[k], send_sem=send.at[3 * n + k],
        recv_sem=recv.at[3 * n + k], device_id=(cx, cy, c), device_id_type=MESH)
        for n in range(len(srcs)) for k, (cx, cy) in enumerate(chips)]


def _exchange_chips_start(name, bufs, after=None):
    n = len(bufs)
    n_in = 2 * n + (0 if after is None else 1)
    lands = [_hbm(lax.empty((3,) + b.shape[1:], b.dtype)) for b in bufs]

    def body(*refs):
        srcs, zones = refs[:n], refs[n:2 * n]
        send, recv, token = refs[n_in], refs[n_in + 1], refs[-1]
        for cp in _chip_copies(srcs, zones, send, recv):
            cp.start()
        token[...] = jnp.zeros_like(token)

    sem = pltpu.SemaphoreType.DMA((3 * n,))
    outs = pl.pallas_call(
        body, name=name,
        out_shape=[sem, sem] + [pltpu.HBM(b.shape, b.dtype) for b in bufs]
        + [pltpu.HBM(z.shape, z.dtype) for z in lands] + [jax.ShapeDtypeStruct((8, LANES), F32)],
        in_specs=[_HBM] * (2 * n) + ([] if after is None else [_ANY]),
        out_specs=[_SEM, _SEM] + [_HBM] * (2 * n) + [pl.BlockSpec(memory_space=pltpu.VMEM)],
        input_output_aliases={i: 2 + i for i in range(2 * n)},
        compiler_params=pltpu.CompilerParams(has_side_effects=_EFFECT))(
            *[_hbm(b) for b in bufs], *lands, *([] if after is None else [after]))
    return outs[0], outs[1], outs[2:2 + n], outs[2 + n:2 + 2 * n], outs[-1]


def _exchange_chips_wait(name, bufs, zones, send, recv, after):
    n = len(bufs)

    def body(*refs):
        for cp in _chip_copies(refs[:n], refs[n:2 * n], refs[2 * n], refs[2 * n + 1]):
            cp.wait_send()
            cp.wait_recv()

    outs = pl.pallas_call(
        body, name=name, out_shape=[pltpu.HBM(a.shape, a.dtype) for a in list(bufs) + list(zones)],
        in_specs=[_HBM] * (2 * n) + [_SEM, _SEM, _ANY], out_specs=[_HBM] * (2 * n),
        input_output_aliases={i: i for i in range(2 * n)},
        compiler_params=pltpu.CompilerParams(has_side_effects=_EFFECT))(*bufs, *zones, send, recv, after)
    return outs[n:]


def _all_gather_small(part):
    def body(src, out, send_sems, recv_sems, local_sem):
        x, y, c, _ = _place()
        mine = pltpu.make_async_copy(src, out.at[4 * x + 2 * y + c], local_sem)
        mine.start()
        copies = []
        for r in range(1, N_DEV):
            dx, dy, dc = (r >> 2) & 1, (r >> 1) & 1, r & 1
            peer = (1 - x if dx else x, 1 - y if dy else y, 1 - c if dc else c)
            copies.append(pltpu.make_async_remote_copy(
                src_ref=src, dst_ref=out.at[4 * x + 2 * y + c],
                send_sem=send_sems.at[r - 1], recv_sem=recv_sems.at[r - 1],
                device_id=peer, device_id_type=MESH))
        for cp in copies:
            cp.start()
        for cp in copies:
            cp.wait()
        mine.wait()

    return pl.pallas_call(
        body, name="all_gather_small",
        out_shape=jax.ShapeDtypeStruct((N_DEV,) + part.shape, part.dtype),
        in_specs=[_ANY], out_specs=_ANY,
        scratch_shapes=[pltpu.SemaphoreType.DMA((N_DEV - 1,)), pltpu.SemaphoreType.DMA((N_DEV - 1,)),
                        pltpu.SemaphoreType.DMA])(part)


def _layer_fwd(xin, xin_bf, W, P, alpha, dep=None):
    h = _proj_in(xin_bf, W["w_in"], P["b_in"], dep=dep)
    o_pre, y_hg, st_all = _hgrn_fwd(h, P["lbs"], P["g_norm_w"])
    yc_pre, y_cv = _conv_fwd(h, P["w_dw"], P["b_dw"], P["conv_ln_g"], P["conv_ln_b"])
    y_h = _mm_nn("branch_a", y_hg, W["w_a"], F32)
    y_c = _mm_nn("branch_b", y_cv, W["w_b"], F32, bias=P["b_b"])
    merged = _gate_fwd(y_h, y_c, h)
    mix = _mm_nn("mix_out", merged, W["w_o"], F32)
    x1, x1_bf, z1 = _ln_fwd("ln1", xin, mix, alpha, P["ln1_g"], P["ln1_b"])
    up = _ffn_up(x1_bf, W["w_up"])
    act = _swiglu_fwd(up)
    ffn = _mm_nn("ffn_down", act, W["w_down"], F32)
    x2, x2_bf, z2 = _ln_fwd("ln2", x1, ffn, alpha, P["ln2_g"], P["ln2_b"])
    saved = dict(xin_bf=xin_bf, h=h, o_pre=o_pre, y_hg=y_hg, st_all=st_all, yc_pre=yc_pre, y_cv=y_cv,
                 y_h=y_h, y_c=y_c, merged=merged, z1=z1, x1_bf=x1_bf, up=up, act=act, z2=z2)
    return x2, x2_bf, saved


def _layer_bwd(dx2, S, W, P, alpha, dep=None):
    dz2, dz2_bf, dln2_g, dln2_b = _ln_bwd("ln2_bwd", S["z2"], dx2, P["ln2_g"], dep=dep)
    dact = _mm_nt("ffn_down_dx", dz2_bf, W["w_down"], F32)
    dw_down = _mm_tn("ffn_down_dw", S["act"], dz2_bf, ACT_DTYPE)
    dup = _swiglu_bwd(dact, S["up"])
    dx1 = _ffn_up_dx(dup, W["w_up"], dz2, alpha)
    dw_up = _ffn_up_dw(S["x1_bf"], dup)
    dz1, dz1_bf, dln1_g, dln1_b = _ln_bwd("ln1_bwd", S["z1"], dx1, P["ln1_g"])
    dmerged = _mm_nt("mix_out_dx", dz1_bf, W["w_o"], F32)
    dw_o = _mm_tn("mix_out_dw", S["merged"], dz1_bf, ACT_DTYPE)
    dy_h, dy_c, db_b, dh = _gate_bwd(dmerged, S["y_h"], S["y_c"], S["h"])
    dy_cv = _mm_nt("branch_b_dx", dy_c, W["w_b"], F32)
    dw_b = _mm_tn("branch_b_dw", S["y_cv"], dy_c, ACT_DTYPE)
    dy_hg = _mm_nt("branch_a_dx", dy_h, W["w_a"], F32)
    dw_a = _mm_tn("branch_a_dw", S["y_hg"], dy_h, ACT_DTYPE)
    dh, dw_dw, db_dw, dcln_g, dcln_b = _conv_bwd(S["h"], P["w_dw"], P["conv_ln_g"], P["conv_ln_b"],
                                                 S["yc_pre"], dy_cv, dh)
    dh, dlbs, dgw = _hgrn_bwd(S["h"], P["lbs"], P["g_norm_w"], S["o_pre"], S["st_all"], dy_hg, dh)
    dxin = _proj_in_dx(dh, W["w_in"], dz1, alpha)
    dw_in = _proj_in_dw(S["xin_bf"], dh)
    db_in = _colsum(dh)
    big = dict(w_in=dw_in, w_a=dw_a, w_b=dw_b, w_o=dw_o, w_down=dw_down, w_up=dw_up)
    small = dict(b_in=db_in, lbs=dlbs, g_norm_w=dgw, w_dw=dw_dw, b_dw=db_dw, conv_ln_g=dcln_g,
                 conv_ln_b=dcln_b, b_b=db_b, ln1_g=dln1_g, ln1_b=dln1_b, ln2_g=dln2_g, ln2_b=dln2_b)
    return dxin, big, small


_SMALL = ("b_in", "lb_logits", "g_norm_w", "b_dw", "conv_ln_g", "conv_ln_b", "b_b", "ln1_g", "ln1_b", "ln2_g",
          "ln2_b")


def _pack_small(per_layer, ln0_g, ln0_b, extra_row, D, L):
    rows = []
    for l in range(L):
        for n in _SMALL:
            a = per_layer[n][l]
            if n == "b_in":
                rows.append(a.reshape(N_SEC, D))
            elif n == "g_norm_w":
                rows.append(jnp.pad(a.reshape(1, -1), ((0, 0), (0, D - a.size))))
            else:
                rows.append(a.reshape(1, D))
    rows += [ln0_g.reshape(1, D), ln0_b.reshape(1, D), extra_row]
    buf = jnp.concatenate(rows, axis=0)
    pad = (-buf.shape[0]) % 8
    return jnp.pad(buf, ((0, pad), (0, 0)))


def _unpack_small(buf, D, L, hv):
    out = {n: [] for n in _SMALL}
    r = 0
    for l in range(L):
        for n in _SMALL:
            if n == "b_in":
                out[n].append(buf[r:r + N_SEC].reshape(N_SEC * D))
                r += N_SEC
            elif n == "g_norm_w":
                out[n].append(buf[r, :hv])
                r += 1
            else:
                out[n].append(buf[r])
                r += 1
    res = {n: jnp.stack(v) for n, v in out.items()}
    res["ln0_g"] = buf[r]
    res["ln0_b"] = buf[r + 1]
    return res, r + 2


def kernel(x, ln0_g, ln0_b, w_in, b_in, lb_logits, g_norm_w, w_a, w_dw, b_dw, conv_ln_g, conv_ln_b, w_b, b_b, w_o, ln1_g, ln1_b, w_up, w_down, ln2_g, ln2_b, loss_target, m_ln0_g, m_ln0_b, m_w_in, m_b_in, m_lb_logits, m_g_norm_w, m_w_a, m_w_dw, m_b_dw, m_conv_ln_g, m_conv_ln_b, m_w_b, m_b_b, m_w_o, m_ln1_g, m_ln1_b, m_w_up, m_w_down, m_ln2_g, m_ln2_b, v_ln0_g, v_ln0_b, v_w_in, v_b_in, v_lb_logits, v_g_norm_w, v_w_a, v_w_dw, v_b_dw, v_conv_ln_g, v_conv_ln_b, v_w_b, v_b_b, v_w_o, v_ln1_g, v_ln1_b, v_w_up, v_w_down, v_ln2_g, v_ln2_b):
    L, D = w_in.shape[0], w_in.shape[1]
    T = x.shape[0] * x.shape[1]
    Dn = w_in.shape[2]
    rs = w_a.shape[1]
    rd = w_down.shape[1]
    cu = w_up.shape[2]
    F = rd * N_DEV
    hv = g_norm_w.shape[1]
    alpha = (2 * L) ** 0.25
    my_x, my_y, my_c = lax.axis_index("x"), lax.axis_index("y"), lax.axis_index("c")
    dev_arr = jnp.reshape(4 * my_x + 2 * my_y + my_c, (1,)).astype(jnp.int32)

    o_a, o_b, o_o, o_d = D, D + rs, D + 2 * rs, D + 3 * rs
    taps = jnp.pad(w_dw, ((0, 0), (0, CONV_HALO - CONV_WIDTH), (0, 0))).reshape(L * CONV_HALO, w_dw.shape[2])
    taps_all = _all_gather_small(taps)
    w_dw_full = taps_all.transpose(1, 0, 2).reshape(L, CONV_HALO, D)

    gathered = [None] * L
    started = [None] * L

    def start_gather(l, after):
        shards = [jnp.concatenate([w_in[l], w_a[l], w_b[l], w_o[l], w_down[l]], axis=0).astype(ACT_DTYPE),
                  w_up[l].astype(ACT_DTYPE)]
        started[l] = _gather_start("gather_start_%d" % l, shards, [_place_own(s, dev_arr) for s in shards], after)
        return started[l][4]

    def weights(l):
        ga, gb = gathered[l]
        return dict(
            w_in=ga[:, :D, :],
            w_a=ga[:, o_a:o_a + rs, :].reshape(D, D),
            w_b=ga[:, o_b:o_b + rs, :].reshape(D, D),
            w_o=ga[:, o_o:o_o + rs, :].reshape(D, D),
            w_down=ga[:, o_d:o_d + rd, :].reshape(F, D),
            w_up=gb.transpose(1, 0, 2).reshape(D, 2 * F))

    def finish_gather(l, after):
        send, recv, thru, zone, _ = started[l]
        sh, zn = _gather_wait("gather_wait_%d" % l, thru, zone, send, recv, after)
        gathered[l] = _gather_finish(zn)

    lbs = _lb_fwd(lb_logits)

    def params(l):
        return dict(b_in=b_in[l].reshape(N_SEC, 1, D), lbs=lbs[l].reshape(1, D), g_norm_w=g_norm_w[l].reshape(1, hv),
                    w_dw=w_dw_full[l], b_dw=b_dw[l].reshape(1, D), conv_ln_g=conv_ln_g[l].reshape(1, D),
                    conv_ln_b=conv_ln_b[l].reshape(1, D), b_b=b_b[l].reshape(1, D), ln1_g=ln1_g[l], ln1_b=ln1_b[l],
                    ln2_g=ln2_g[l], ln2_b=ln2_b[l])

    x2d = x.reshape(T, D)
    token = start_gather(0, taps_all)
    xc, xc_bf = _ln_fwd("ln0", x2d, None, 1.0, ln0_g, ln0_b, dep=token)
    finish_gather(0, xc_bf)
    saved = []
    for l in range(L):
        token = start_gather(l + 1, gathered[l][0]) if l + 1 < L else None
        xc, xc_bf, s = _layer_fwd(xc, xc_bf, weights(l), params(l), alpha, dep=token)
        saved.append(s)
        if l + 1 < L:
            finish_gather(l + 1, xc_bf)

    c_arr = jnp.reshape(my_c, (1,)).astype(jnp.int32)
    chip = 2 * my_x + my_y
    dx, loss_row = _loss_fwd_bwd(xc, loss_target.reshape(T, D))
    small = [None] * L
    pending = None
    upd_big = {n: None for n in ("w_in", "w_a", "w_b", "w_o", "w_down", "w_up")}
    wmv = dict(w_in=(w_in, m_w_in, v_w_in), w_a=(w_a, m_w_a, v_w_a), w_b=(w_b, m_w_b, v_w_b),
               w_o=(w_o, m_w_o, v_w_o), w_down=(w_down, m_w_down, v_w_down), w_up=(w_up, m_w_up, v_w_up))
    row0 = dict(w_in=0, w_a=o_a, w_b=o_b, w_o=o_o, w_down=o_d, w_up=0)

    def update_layer(l, q, r2):
        pre = jnp.stack([chip, jnp.int32(l)]).astype(jnp.int32)
        for name in upd_big:
            w, m, v = wmv[name]
            r, C = w.shape[1], w.shape[2]
            k = 1 if name == "w_up" else 0
            tr = _pick(r, (256, 128, 64, 32, 16))
            while row0[name] % tr:
                tr //= 2
            b0, nb = row0[name] // tr, r // tr
            specs = [pl.BlockSpec((None, tr, C), functools.partial(lambda i, s, b0: (s[0], b0 + i, 0), b0=b0))]
            specs += [pl.BlockSpec((None, tr, C), functools.partial(lambda i, s, j, b0: (j, b0 + i, 0), j=j, b0=b0))
                      for j in range(3)]
            upd_big[name] = _adamw(
                "adamw_" + name, w.reshape(L * r, C), m.reshape(L * r, C), v.reshape(L * r, C),
                [q[k], r2[k], r2[k], r2[k]], specs, tr, prefetch=pre, nsteps=nb,
                row_map=functools.partial(lambda i, s, nb: (s[1] * nb + i, 0), nb=nb), prev=upd_big[name])

    def finish_reduce(after):
        l, sems, q, zones = pending
        r2 = _exchange_chips_wait("reduce_wait_%d" % l, q, zones, sems[0], sems[1], after)
        update_layer(l, q, r2)

    token = None
    for l in range(L - 1, -1, -1):
        dx, big, small[l] = _layer_bwd(dx, saved[l], weights(l), params(l), alpha, dep=token)
        if pending is not None:
            finish_reduce(dx)
        send_a = jnp.concatenate([big["w_in"], big["w_a"].reshape(N_DEV, rs, D), big["w_b"].reshape(N_DEV, rs, D),
                                  big["w_o"].reshape(N_DEV, rs, D), big["w_down"].reshape(N_DEV, rd, D)], axis=1)
        send_b = big["w_up"].reshape(D, N_DEV, cu).transpose(1, 0, 2)
        r1a, r1b = _exchange_sibling([send_a, send_b])
        qs = [_pair_add(send_a, r1a, c_arr), _pair_add(send_b, r1b, c_arr)]
        if l > 0:
            s_send, s_recv, q_thru, zones, token = _exchange_chips_start("reduce_start_%d" % l, qs)
            pending = (l, (s_send, s_recv), list(q_thru), list(zones))
    dx0, _, dln0_g, dln0_b = _ln_bwd("ln0_bwd", x2d, dx, ln0_g)
    dlb_logits = _lb_bwd(lb_logits, jnp.concatenate([small[l]["lbs"] for l in range(L)], axis=0))

    small_l = {n: [small[l][n] for l in range(L)] for n in _SMALL if n != "lb_logits"}
    small_l["lb_logits"] = [dlb_logits[l] for l in range(L)]
    loss_pad = jnp.pad(loss_row, ((0, 0), (0, D - LANES)))
    part = jnp.concatenate([_pack_small(small_l, dln0_g, dln0_b, loss_pad, D, L)]
                           + [small[l]["w_dw"] for l in range(L)], axis=0)
    parts_all = _all_gather_small(part)
    n_small = part.shape[0] - L * CONV_HALO

    s_send, s_recv, q_thru, zones, _ = _exchange_chips_start("reduce_start_0", qs, after=parts_all)
    pending = (0, (s_send, s_recv), list(q_thru), list(zones))


    inputs = dict(b_in=(b_in, m_b_in, v_b_in), lb_logits=(lb_logits, m_lb_logits, v_lb_logits),
                  g_norm_w=(g_norm_w, m_g_norm_w, v_g_norm_w), b_dw=(b_dw, m_b_dw, v_b_dw),
                  conv_ln_g=(conv_ln_g, m_conv_ln_g, v_conv_ln_g), conv_ln_b=(conv_ln_b, m_conv_ln_b, v_conv_ln_b),
                  b_b=(b_b, m_b_b, v_b_b), ln1_g=(ln1_g, m_ln1_g, v_ln1_g), ln1_b=(ln1_b, m_ln1_b, v_ln1_b),
                  ln2_g=(ln2_g, m_ln2_g, v_ln2_g), ln2_b=(ln2_b, m_ln2_b, v_ln2_b))
    zero_row = jnp.zeros((1, D), F32)
    packed = [_pack_small({n: [inputs[n][i][l] for l in range(L)] for n in _SMALL},
                          (ln0_g, m_ln0_g, v_ln0_g)[i], (ln0_b, m_ln0_b, v_ln0_b)[i], zero_row, D, L)
              for i in range(3)]
    small_specs = [pl.BlockSpec((None, n_small, D), functools.partial(lambda i, d: (d, 0, 0), d=d))
                   for d in range(N_DEV)]
    s_out = _adamw("adamw_small", packed[0], packed[1], packed[2], [parts_all] * N_DEV, small_specs, n_small)
    s_g, n_rows = _unpack_small(s_out[0], D, L, hv)
    s_d, _ = _unpack_small(s_out[1], D, L, hv)
    s_m, _ = _unpack_small(s_out[2], D, L, hv)
    s_v, _ = _unpack_small(s_out[3], D, L, hv)
    loss = s_out[0][n_rows, 0]

    cw = w_dw.shape[2]
    dev = 4 * my_x + 2 * my_y + my_c
    tap_parts = lax.dynamic_slice_in_dim(parts_all[:, n_small:, :], dev * cw, cw, axis=2)
    tap_specs = [pl.BlockSpec((None, L * CONV_HALO, cw), functools.partial(lambda i, d: (d, 0, 0), d=d))
                 for d in range(N_DEV)]
    pad_t = lambda a: jnp.pad(a, ((0, 0), (0, CONV_HALO - CONV_WIDTH), (0, 0))).reshape(L * CONV_HALO, cw)
    t_out = _adamw("adamw_taps", pad_t(w_dw), pad_t(m_w_dw), pad_t(v_w_dw), [tap_parts] * N_DEV, tap_specs,
                   L * CONV_HALO)
    finish_reduce(t_out[0])
    upd = {n: [o.reshape(wmv[n][0].shape) for o in outs] for n, outs in upd_big.items()}
    upd["w_dw"] = [o.reshape(L, CONV_HALO, cw)[:, :CONV_WIDTH, :] for o in t_out]

    order = ["ln0_g", "ln0_b", "w_in", "b_in", "lb_logits", "g_norm_w", "w_a", "w_dw", "b_dw", "conv_ln_g",
             "conv_ln_b", "w_b", "b_b", "w_o", "ln1_g", "ln1_b", "w_up", "w_down", "ln2_g", "ln2_b"]
    small_sets = (s_g, s_d, s_m, s_v)
    outs = [loss, dx0.reshape(x.shape)]
    for i in range(4):
        for n in order:
            outs.append(upd[n][i] if n in upd else small_sets[i][n])
    return tuple(outs)
```

```python
import functools

import jax
import jax.numpy as jnp
from jax import lax
from jax.experimental import pallas as pl
from jax.experimental.pallas import tpu as pltpu

F32 = jnp.float32
MXU_DTYPE = jnp.bfloat16
ACT_DTYPE = jnp.bfloat16

LANES = 128
SUB = 8
N_DEV = 8
N_SEC = 8
CONV_WIDTH = 31
CONV_HALO = 32
HG_C = 16
LN_EPS = 1e-5
RMS_EPS = 1e-6
F_MIN = 1e-30
ADAM_LR = 0.001
ADAM_B1 = 0.9
ADAM_B2 = 0.999
ADAM_EPS = 1e-08
ADAM_WD = 0.01
ADAM_STEP = 10
VMEM_LIMIT = 56 * 1024 * 1024
MESH = pl.DeviceIdType.MESH

_NN = (((1,), (0,)), ((), ()))
_NT = (((1,), (1,)), ((), ()))
_TN = (((0,), (0,)), ((), ()))


_ANY = pl.BlockSpec(memory_space=pl.ANY)
_HBM = pl.BlockSpec(memory_space=pltpu.HBM)
_SEM = pl.BlockSpec(memory_space=pltpu.SEMAPHORE)
_EFFECT = pltpu.SideEffectType.DATAFLOW_SIDE_EFFECTING


def _cp(*sem):
    return pltpu.CompilerParams(dimension_semantics=tuple(sem), vmem_limit_bytes=VMEM_LIMIT)


def _pick(n, cands):
    for c in cands:
        if c <= n and n % c == 0:
            return c
    return n


def _silu(x):
    return x * jax.nn.sigmoid(x)


def _dsilu(x):
    s = jax.nn.sigmoid(x)
    return s * (1.0 + x * (1.0 - s))


def _matmul(name, a, b, *, dims, grid, a_spec, b_spec, out_shape, out_spec, acc_shape, nk,
            bias=None, bias_spec=None, add=None, add_spec=None, add_scale=1.0, dep=None):
    has_bias, has_add = bias is not None, add is not None
    kaxis = len(grid) - 1

    def body(*refs):
        a_ref, b_ref = refs[0], refs[1]
        pos = 2
        bias_ref = add_ref = None
        if has_bias:
            bias_ref = refs[pos]
            pos += 1
        if has_add:
            add_ref = refs[pos]
            pos += 1
        if dep is not None:
            pos += 1
        o_ref = refs[pos]
        acc_ref = refs[pos + 1] if nk > 1 else None

        part = lax.dot_general(a_ref[...].astype(MXU_DTYPE), b_ref[...].astype(MXU_DTYPE), dims,
                               preferred_element_type=F32)

        def finish(r):
            if has_bias:
                r = r + bias_ref[...]
            if has_add:
                r = r + add_scale * add_ref[...]
            o_ref[...] = r.astype(o_ref.dtype)

        if nk == 1:
            finish(part)
        else:
            k = pl.program_id(kaxis)

            @pl.when(k == 0)
            def _():
                acc_ref[...] = part

            @pl.when(k > 0)
            def _():
                acc_ref[...] += part

            @pl.when(k == nk - 1)
            def _():
                finish(acc_ref[...])

    ins, specs = [a, b], [a_spec, b_spec]
    if has_bias:
        ins.append(bias)
        specs.append(bias_spec)
    if has_add:
        ins.append(add)
        specs.append(add_spec)
    if dep is not None:
        ins.append(dep)
        specs.append(_ANY)
    sem =("parallel",) * (len(grid) - 1) + ("arbitrary",) if nk > 1 else ("parallel",) * len(grid)
    return pl.pallas_call(
        body, name=name, grid=grid, in_specs=specs, out_specs=out_spec, out_shape=out_shape,
        scratch_shapes=[pltpu.VMEM(acc_shape, F32)] if nk > 1 else [],
        compiler_params=_cp(*sem))(*ins)


def _mm_nn(name, a, b, out_dtype, bias=None):
    M, K = a.shape
    N = b.shape[1]
    tn = _pick(N, (512, 256, 128))
    tk = K if K <= 1024 else _pick(K, (1408, 1024, 512, 256, 128))
    nk = K // tk
    return _matmul(
        name, a, b, dims=_NN, grid=(N // tn, nk),
        a_spec=pl.BlockSpec((M, tk), lambda j, k: (0, k)),
        b_spec=pl.BlockSpec((tk, tn), lambda j, k: (k, j)),
        out_shape=jax.ShapeDtypeStruct((M, N), out_dtype),
        out_spec=pl.BlockSpec((M, tn), lambda j, k: (0, j)),
        acc_shape=(M, tn), nk=nk,
        bias=bias, bias_spec=None if bias is None else pl.BlockSpec((1, tn), lambda j, k: (0, j)))


def _mm_nt(name, a, b, out_dtype, add=None, add_scale=1.0):
    M, K = a.shape
    N = b.shape[0]
    tn = _pick(N, (512, 256, 128))
    tk = K if K <= 1024 else _pick(K, (1408, 1024, 512, 256, 128))
    nk = K // tk
    return _matmul(
        name, a, b, dims=_NT, grid=(N // tn, nk),
        a_spec=pl.BlockSpec((M, tk), lambda j, k: (0, k)),
        b_spec=pl.BlockSpec((tn, tk), lambda j, k: (j, k)),
        out_shape=jax.ShapeDtypeStruct((M, N), out_dtype),
        out_spec=pl.BlockSpec((M, tn), lambda j, k: (0, j)),
        acc_shape=(M, tn), nk=nk,
        add=add, add_spec=None if add is None else pl.BlockSpec((M, tn), lambda j, k: (0, j)),
        add_scale=add_scale)


def _mm_tn(name, a, b, out_dtype):
    K, M = a.shape
    N = b.shape[1]
    tm = _pick(M, (256, 128))
    return _matmul(
        name, a, b, dims=_TN, grid=(M // tm,),
        a_spec=pl.BlockSpec((K, tm), lambda i: (0, i)),
        b_spec=pl.BlockSpec((K, N), lambda i: (0, 0)),
        out_shape=jax.ShapeDtypeStruct((M, N), out_dtype),
        out_spec=pl.BlockSpec((tm, N), lambda i: (i, 0)),
        acc_shape=(tm, N), nk=1)


def _proj_in(x_bf, w_in, b_in, dep=None):
    T, D = x_bf.shape
    tn = _pick(D, (512, 256, 128))
    return _matmul(
        "proj_in", x_bf, w_in, dims=_NN, grid=(N_SEC, D // tn),
        a_spec=pl.BlockSpec((T, D), lambda s, j: (0, 0)),
        b_spec=pl.BlockSpec((None, D, tn), lambda s, j: (s, 0, j)),
        out_shape=jax.ShapeDtypeStruct((N_SEC, T, D), F32),
        out_spec=pl.BlockSpec((None, T, tn), lambda s, j: (s, 0, j)),
        acc_shape=(T, tn), nk=1,
        bias=b_in, bias_spec=pl.BlockSpec((None, 1, tn), lambda s, j: (s, 0, j)), dep=dep)


def _proj_in_dx(dh, w_in, add, add_scale):
    _, T, D = dh.shape
    tn = _pick(D, (512, 256, 128))
    return _matmul(
        "proj_in_dx", dh, w_in, dims=_NT, grid=(D // tn, N_SEC),
        a_spec=pl.BlockSpec((None, T, D), lambda j, s: (s, 0, 0)),
        b_spec=pl.BlockSpec((None, tn, D), lambda j, s: (s, j, 0)),
        out_shape=jax.ShapeDtypeStruct((T, D), F32),
        out_spec=pl.BlockSpec((T, tn), lambda j, s: (0, j)),
        acc_shape=(T, tn), nk=N_SEC,
        add=add, add_spec=pl.BlockSpec((T, tn), lambda j, s: (0, j)), add_scale=add_scale)


def _proj_in_dw(x_bf, dh):
    _, T, D = dh.shape
    tn = _pick(D, (512, 256, 128))

    def body(x_ref, dh_ref, dw_ref, db_ref):
        dhv = dh_ref[...]
        dw_ref[...] = lax.dot_general(x_ref[...].astype(MXU_DTYPE), dhv.astype(MXU_DTYPE), _TN,
                                      preferred_element_type=F32).astype(dw_ref.dtype)
        db_ref[...] = jnp.sum(dhv.astype(F32), axis=0, keepdims=True)

    return pl.pallas_call(
        body, name="proj_in_dw", grid=(N_SEC, D // tn),
        in_specs=[pl.BlockSpec((T, D), lambda s, j: (0, 0)), pl.BlockSpec((None, T, tn), lambda s, j: (s, 0, j))],
        out_specs=[pl.BlockSpec((None, D, tn), lambda s, j: (s, 0, j)),
                   pl.BlockSpec((None, 1, tn), lambda s, j: (s, 0, j))],
        out_shape=[jax.ShapeDtypeStruct((N_SEC, D, D), ACT_DTYPE), jax.ShapeDtypeStruct((N_SEC, 1, D), F32)],
        compiler_params=_cp("parallel", "parallel"))(x_bf, dh)


def _ffn_up(x_bf, w_up):
    T, D = x_bf.shape
    F = w_up.shape[1] // 2
    tn = _pick(F, (256, 128))
    nb = F // tn
    return _matmul(
        "ffn_up", x_bf, w_up, dims=_NN, grid=(2, nb),
        a_spec=pl.BlockSpec((T, D), lambda p, j: (0, 0)),
        b_spec=pl.BlockSpec((D, tn), lambda p, j: (0, p * nb + j)),
        out_shape=jax.ShapeDtypeStruct((2, T, F), F32),
        out_spec=pl.BlockSpec((None, T, tn), lambda p, j: (p, 0, j)),
        acc_shape=(T, tn), nk=1)


def _ffn_up_dx(dup, w_up, add, add_scale):
    _, T, F = dup.shape
    D = w_up.shape[0]
    tn = _pick(D, (512, 256, 128))
    tk = _pick(F, (1408, 256, 128))
    nb = F // tk
    return _matmul(
        "ffn_up_dx", dup, w_up, dims=_NT, grid=(D // tn, 2 * nb),
        a_spec=pl.BlockSpec((None, T, tk), lambda j, k: (k // nb, 0, k % nb)),
        b_spec=pl.BlockSpec((tn, tk), lambda j, k: (j, k)),
        out_shape=jax.ShapeDtypeStruct((T, D), F32),
        out_spec=pl.BlockSpec((T, tn), lambda j, k: (0, j)),
        acc_shape=(T, tn), nk=2 * nb,
        add=add, add_spec=pl.BlockSpec((T, tn), lambda j, k: (0, j)), add_scale=add_scale)


def _ffn_up_dw(x_bf, dup):
    _, T, F = dup.shape
    D = x_bf.shape[1]
    tn = _pick(F, (1408, 256, 128))
    nb = F // tn
    return _matmul(
        "ffn_up_dw", x_bf, dup, dims=_TN, grid=(2, nb),
        a_spec=pl.BlockSpec((T, D), lambda p, j: (0, 0)),
        b_spec=pl.BlockSpec((None, T, tn), lambda p, j: (p, 0, j)),
        out_shape=jax.ShapeDtypeStruct((D, 2 * F), ACT_DTYPE),
        out_spec=pl.BlockSpec((D, tn), lambda p, j: (0, p * nb + j)),
        acc_shape=(D, tn), nk=1)


def _ln_fwd(name, a, res, alpha, g, b, dep=None):
    T, D = a.shape
    tr = _pick(T, (256, 128, 64, 32, 16))
    has_res = res is not None

    def body(*refs):
        if has_res:
            a_ref, r_ref, g_ref, b_ref = refs[:4]
            y_ref, yb_ref, z_ref = refs[-3:]
            z = alpha * a_ref[...] + r_ref[...]
            z_ref[...] = z
        else:
            a_ref, g_ref, b_ref = refs[:3]
            y_ref, yb_ref = refs[-2:]
            z = a_ref[...]
        mu = jnp.mean(z, axis=-1, keepdims=True)
        zc = z - mu
        var = jnp.mean(zc * zc, axis=-1, keepdims=True)
        y = zc * lax.rsqrt(var + LN_EPS) * g_ref[...] + b_ref[...]
        y_ref[...] = y
        yb_ref[...] = y.astype(ACT_DTYPE)

    row = pl.BlockSpec((tr, D), lambda i: (i, 0))
    vec = pl.BlockSpec((1, D), lambda i: (0, 0))
    ins = [a] + ([res] if has_res else []) + [g.reshape(1, D), b.reshape(1, D)]
    in_specs = [row] + ([row] if has_res else []) + [vec, vec]
    if dep is not None:
        ins.append(dep)
        in_specs.append(_ANY)
    out_shape = [jax.ShapeDtypeStruct((T, D), F32), jax.ShapeDtypeStruct((T, D), ACT_DTYPE)]
    if has_res:
        out_shape.append(jax.ShapeDtypeStruct((T, D), F32))
    return pl.pallas_call(
        body, name=name, grid=(T // tr,), in_specs=in_specs,
        out_specs=[row] * len(out_shape), out_shape=out_shape, compiler_params=_cp("parallel"))(*ins)


def _ln_bwd(name, z, dy, g, dep=None):
    T, D = z.shape
    tr = _pick(T, (256, 128, 64, 32, 16))

    def body(z_ref, dy_ref, g_ref, *rest):
        dz_ref, dzb_ref, dg_ref, db_ref = rest[-4:]

        @pl.when(pl.program_id(0) == 0)
        def _():
            dg_ref[...] = jnp.zeros_like(dg_ref)
            db_ref[...] = jnp.zeros_like(db_ref)

        zv = z_ref[...]
        dy_ = dy_ref[...]
        mu = jnp.mean(zv, axis=-1, keepdims=True)
        zc = zv - mu
        rstd = lax.rsqrt(jnp.mean(zc * zc, axis=-1, keepdims=True) + LN_EPS)
        xhat = zc * rstd
        dxh = dy_ * g_ref[...]
        dz = rstd * (dxh - jnp.mean(dxh, axis=-1, keepdims=True)
                     - xhat * jnp.mean(dxh * xhat, axis=-1, keepdims=True))
        dz_ref[...] = dz
        dzb_ref[...] = dz.astype(ACT_DTYPE)
        dg_ref[...] += jnp.sum(dy_ * xhat, axis=0, keepdims=True)
        db_ref[...] += jnp.sum(dy_, axis=0, keepdims=True)

    row = pl.BlockSpec((tr, D), lambda i: (i, 0))
    vec = pl.BlockSpec((1, D), lambda i: (0, 0))
    ins, in_specs = [z, dy, g.reshape(1, D)], [row, row, vec]
    if dep is not None:
        ins.append(dep)
        in_specs.append(_ANY)
    return pl.pallas_call(
        body, name=name, grid=(T // tr,), in_specs=in_specs, out_specs=[row, row, vec, vec],
        out_shape=[jax.ShapeDtypeStruct((T, D), F32), jax.ShapeDtypeStruct((T, D), ACT_DTYPE),
                   jax.ShapeDtypeStruct((1, D), F32), jax.ShapeDtypeStruct((1, D), F32)],
        compiler_params=_cp("arbitrary"))(*ins)


def _loss_fwd_bwd(y, target):
    T, D = y.shape
    tr = _pick(T, (256, 128, 64, 32, 16))

    def body(y_ref, t_ref, dy_ref, l_ref):
        @pl.when(pl.program_id(0) == 0)
        def _():
            l_ref[...] = jnp.zeros_like(l_ref)

        e = y_ref[...] - t_ref[...]
        dy_ref[...] = e * (1.0 / D)
        row = jnp.sum(e * e, axis=-1, keepdims=True) * (1.0 / D)
        l_ref[...] += 0.5 * jnp.sum(row, axis=0, keepdims=True)

    rowspec = pl.BlockSpec((tr, D), lambda i: (i, 0))
    return pl.pallas_call(
        body, name="loss", grid=(T // tr,), in_specs=[rowspec, rowspec],
        out_specs=[rowspec, pl.BlockSpec((1, LANES), lambda i: (0, 0))],
        out_shape=[jax.ShapeDtypeStruct((T, D), F32), jax.ShapeDtypeStruct((1, LANES), F32)],
        compiler_params=_cp("arbitrary"))(y, target)


def _gate_fwd(y_h, y_c, h):
    T, D = y_h.shape
    tr = _pick(T, (256, 128, 64, 32, 16))

    def body(yh_ref, yc_ref, gh_ref, gc_ref, m_ref):
        m = jax.nn.sigmoid(gh_ref[...]) * yh_ref[...] + jax.nn.sigmoid(gc_ref[...]) * yc_ref[...]
        m_ref[...] = m.astype(ACT_DTYPE)

    row = pl.BlockSpec((tr, D), lambda i: (i, 0))
    return pl.pallas_call(
        body, name="gate_fwd", grid=(T // tr,),
        in_specs=[row, row, pl.BlockSpec((None, tr, D), lambda i: (6, i, 0)),
                  pl.BlockSpec((None, tr, D), lambda i: (7, i, 0))],
        out_specs=row, out_shape=jax.ShapeDtypeStruct((T, D), ACT_DTYPE),
        compiler_params=_cp("parallel"))(y_h, y_c, h, h)


def _gate_bwd(dm, y_h, y_c, h):
    T, D = y_h.shape
    tr = _pick(T, (256, 128, 64, 32, 16))

    def body(dm_ref, yh_ref, yc_ref, gh_ref, gc_ref, dyh_ref, dyc_ref, dbb_ref, dh_ref):
        @pl.when(pl.program_id(0) == 0)
        def _():
            dbb_ref[...] = jnp.zeros_like(dbb_ref)

        dm_ = dm_ref[...]
        sh = jax.nn.sigmoid(gh_ref[...])
        sc = jax.nn.sigmoid(gc_ref[...])
        dyc = dm_ * sc
        dyh_ref[...] = (dm_ * sh).astype(ACT_DTYPE)
        dyc_ref[...] = dyc.astype(ACT_DTYPE)
        dbb_ref[...] += jnp.sum(dyc, axis=0, keepdims=True)
        dh_ref[0] = (dm_ * yh_ref[...] * sh * (1.0 - sh)).astype(ACT_DTYPE)
        dh_ref[1] = (dm_ * yc_ref[...] * sc * (1.0 - sc)).astype(ACT_DTYPE)

    row = pl.BlockSpec((tr, D), lambda i: (i, 0))
    return pl.pallas_call(
        body, name="gate_bwd", grid=(T // tr,),
        in_specs=[row, row, row, pl.BlockSpec((None, tr, D), lambda i: (6, i, 0)),
                  pl.BlockSpec((None, tr, D), lambda i: (7, i, 0))],
        out_specs=[row, row, pl.BlockSpec((1, D), lambda i: (0, 0)),
                   pl.BlockSpec((2, tr, D), lambda i: (3, i, 0))],
        out_shape=[jax.ShapeDtypeStruct((T, D), ACT_DTYPE), jax.ShapeDtypeStruct((T, D), ACT_DTYPE),
                   jax.ShapeDtypeStruct((1, D), F32), jax.ShapeDtypeStruct((N_SEC, T, D), ACT_DTYPE)],
        compiler_params=_cp("arbitrary"))(dm, y_h, y_c, h, h)


def _swiglu_fwd(up):
    _, T, F = up.shape
    tr = _pick(T, (128, 64, 32, 16))

    def body(up_ref, act_ref):
        act_ref[...] = (_silu(up_ref[0]) * up_ref[1]).astype(ACT_DTYPE)

    return pl.pallas_call(
        body, name="swiglu_fwd", grid=(T // tr,),
        in_specs=[pl.BlockSpec((2, tr, F), lambda i: (0, i, 0))],
        out_specs=pl.BlockSpec((tr, F), lambda i: (i, 0)),
        out_shape=jax.ShapeDtypeStruct((T, F), ACT_DTYPE), compiler_params=_cp("parallel"))(up)


def _swiglu_bwd(dact, up):
    _, T, F = up.shape
    tr = _pick(T, (128, 64, 32, 16))

    def body(da_ref, up_ref, dup_ref):
        da = da_ref[...]
        ug = up_ref[0]
        dup_ref[0] = (da * up_ref[1] * _dsilu(ug)).astype(ACT_DTYPE)
        dup_ref[1] = (da * _silu(ug)).astype(ACT_DTYPE)

    blk = pl.BlockSpec((2, tr, F), lambda i: (0, i, 0))
    return pl.pallas_call(
        body, name="swiglu_bwd", grid=(T // tr,),
        in_specs=[pl.BlockSpec((tr, F), lambda i: (i, 0)), blk], out_specs=blk,
        out_shape=jax.ShapeDtypeStruct((2, T, F), ACT_DTYPE), compiler_params=_cp("parallel"))(dact, up)


def _lb_softmax(x):
    L = x.shape[0]
    rows = [x[l:l + 1] for l in range(L)]
    m = rows[0]
    for r in rows[1:]:
        m = jnp.maximum(m, r)
    e = [jnp.exp(r - m) for r in rows]
    s = e[0]
    for r in e[1:]:
        s = s + r
    return [r / s for r in e]


def _lb_fwd(lb_logits):
    L, D = lb_logits.shape

    def body(x_ref, o_ref):
        p = _lb_softmax(x_ref[...])
        run = jnp.zeros_like(p[0])
        for l in range(L):
            if l > 0:
                run = run + p[l]
            o_ref[pl.ds(l, 1), :] = run

    return pl.pallas_call(body, name="lb_fwd", out_shape=jax.ShapeDtypeStruct((L, D), F32))(lb_logits)


def _lb_bwd(lb_logits, dlbs):
    L, D = lb_logits.shape

    def body(x_ref, d_ref, o_ref):
        p = _lb_softmax(x_ref[...])
        d = d_ref[...]
        dp = [jnp.zeros_like(p[0]) for _ in range(L)]
        run = jnp.zeros_like(p[0])
        for j in range(L - 1, 0, -1):
            run = run + d[j:j + 1]
            dp[j] = run
        dot = dp[0] * p[0]
        for j in range(1, L):
            dot = dot + dp[j] * p[j]
        for j in range(L):
            o_ref[pl.ds(j, 1), :] = p[j] * (dp[j] - dot)

    return pl.pallas_call(body, name="lb_bwd", out_shape=jax.ShapeDtypeStruct((L, D), F32))(lb_logits, dlbs)


def _blk_cumsum(x, c, reverse=False):
    n = x.shape[0]
    pos = lax.broadcasted_iota(jnp.int32, x.shape, 0) % c
    s = 1
    while s < c:
        if reverse:
            shifted = pltpu.roll(x, n - s, 0)
            x = x + jnp.where(pos + s < c, shifted, 0.0)
        else:
            shifted = pltpu.roll(x, s, 0)
            x = x + jnp.where(pos >= s, shifted, 0.0)
        s *= 2
    return x


def _hgrn_prologue(q_ref, f_ref, lb_ref):
    lbv = lb_ref[...]
    z = f_ref[...]
    sig = jax.nn.sigmoid(z)
    one_m = 1.0 - lbv
    f = lbv + one_m * sig
    logf = jnp.log(jnp.maximum(f, F_MIN))
    k = one_m * jax.nn.sigmoid(-z)
    q = _silu(q_ref[...])
    return q, k, logf, f, sig, one_m


def _hgrn_fwd(h, lbs_l, gw):
    _, T, D = h.shape
    nh = D // LANES
    c = HG_C
    Tt = _pick(T, (128, 64, 32, 16))
    nb = Tt // c
    ng = c // SUB

    def body(q_ref, f_ref, i_ref, g_ref, lb_ref, gw_ref, o_ref, y_ref, sall_ref,
             st_ref, G_s, q_s, k_s, W_s, R_s, dS_s, o_s):
        @pl.when(pl.program_id(1) == 0)
        def _():
            st_ref[...] = jnp.zeros_like(st_ref)

        q, k, logf, _, _, _ = _hgrn_prologue(q_ref, f_ref, lb_ref)
        G_s[...] = _blk_cumsum(logf, c)
        q_s[...] = q
        k_s[...] = k
        ones = jnp.ones((LANES, LANES), MXU_DTYPE)
        rowid = lax.broadcasted_iota(jnp.int32, (SUB, LANES), 0)
        zero = jnp.zeros((SUB, LANES), F32)
        for bi in range(nb):
            r0 = bi * c
            glast = G_s[pl.ds(r0 + c - 1, 1), :]
            kd = k_s[pl.ds(r0, c), :] * jnp.exp(glast - G_s[pl.ds(r0, c), :])
            dS_s[bi] = lax.dot_general(i_ref[pl.ds(r0, c), :].astype(MXU_DTYPE), kd.astype(MXU_DTYPE), _TN,
                                       preferred_element_type=F32)
        st = st_ref[...]
        for bi in range(nb):
            sall_ref[bi] = st
            st = st * jnp.exp(G_s[pl.ds(bi * c + c - 1, 1), :]) + dS_s[bi]
        st_ref[...] = st
        for bi in range(nb):
            r0 = bi * c
            qd = q_s[pl.ds(r0, c), :] * jnp.exp(G_s[pl.ds(r0, c), :])
            o_s[pl.ds(r0, c), :] = lax.dot_general(qd.astype(MXU_DTYPE), sall_ref[bi].astype(MXU_DTYPE), _NT,
                                                   preferred_element_type=F32)
        for bi in range(nb):
            r0 = bi * c
            w0 = bi * c * c
            Gg = [G_s[pl.ds(r0 + gi * SUB, SUB), :] for gi in range(ng)]
            qg = [q_s[pl.ds(r0 + gi * SUB, SUB), :] for gi in range(ng)]
            for s in range(c):
                gs = G_s[pl.ds(r0 + s, 1), :]
                ks = k_s[pl.ds(r0 + s, 1), :]
                parts = []
                for gi in range(ng):
                    if gi < s // SUB:
                        parts.append(zero)
                        continue
                    e = jnp.exp(jnp.minimum(Gg[gi] - gs, 0.0))
                    if gi == s // SUB:
                        e = jnp.where(rowid >= s - gi * SUB, e, 0.0)
                    parts.append(e * qg[gi] * ks)
                W_s[pl.ds(w0 + s * c, c), :] = jnp.concatenate(parts, axis=0).astype(MXU_DTYPE)
        R_s[...] = jnp.dot(W_s[...], ones, preferred_element_type=F32)
        for bi in range(nb):
            r0 = bi * c
            w0 = bi * c * c
            acc = [o_s[pl.ds(r0 + gi * SUB, SUB), :] for gi in range(ng)]
            for s in range(c):
                vs = i_ref[pl.ds(r0 + s, 1), :]
                for gi in range(s // SUB, ng):
                    acc[gi] = acc[gi] + R_s[pl.ds(w0 + s * c + gi * SUB, SUB), :] * vs
            o_s[pl.ds(r0, c), :] = jnp.concatenate(acc, axis=0)
        o = o_s[...]
        n = o * lax.rsqrt(jnp.mean(o * o, axis=-1, keepdims=True) + RMS_EPS)
        o_ref[...] = o
        y_ref[...] = (n * gw_ref[...] * _silu(g_ref[...])).astype(ACT_DTYPE)

    def sec(s):
        return pl.BlockSpec((None, Tt, LANES), lambda hd, i: (s, i, hd))

    col = pl.BlockSpec((Tt, LANES), lambda hd, i: (i, hd))
    return pl.pallas_call(
        body, name="hgrn_fwd", grid=(nh, T // Tt),
        in_specs=[sec(0), sec(1), sec(2), sec(3), pl.BlockSpec((1, LANES), lambda hd, i: (0, hd)),
                  pl.BlockSpec((1, LANES), lambda hd, i: (0, 0))],
        out_specs=[col, col, pl.BlockSpec((nb, None, LANES, LANES), lambda hd, i: (i, hd, 0, 0))],
        out_shape=[jax.ShapeDtypeStruct((T, D), F32), jax.ShapeDtypeStruct((T, D), ACT_DTYPE),
                   jax.ShapeDtypeStruct((T // c, nh, LANES, LANES), F32)],
        scratch_shapes=[pltpu.VMEM((LANES, LANES), F32), pltpu.VMEM((Tt, LANES), F32),
                        pltpu.VMEM((Tt, LANES), F32), pltpu.VMEM((Tt, LANES), F32),
                        pltpu.VMEM((nb * c * c, LANES), MXU_DTYPE), pltpu.VMEM((nb * c * c, LANES), F32),
                        pltpu.VMEM((nb, LANES, LANES), F32), pltpu.VMEM((Tt, LANES), F32)],
        compiler_params=_cp("parallel", "arbitrary"))(h, h, h, h, lbs_l, gw)


def _hgrn_bwd(h, lbs_l, gw, o_pre, st_all, dy, dh):
    _, T, D = h.shape
    nh = D // LANES
    c = HG_C
    Tt = _pick(T, (128, 64, 32, 16))
    nb = Tt // c
    ng = c // SUB
    nT = T // Tt

    def body(q_ref, f_ref, i_ref, g_ref, lb_ref, gw_ref, o_ref, sall_ref, dy_ref, dh_in_ref,
             dh_ref, dlb_ref, dgw_ref,
             dst_ref, G_s, q_s, k_s, do_s, E_s, WP_s, dq_s, dk_s, dv_s, dG_s,
             R_s, dS_s, dstA_s, dqd_s, dkd_s, dvi_s, da_s):
        del dh_in_ref
        hd, ti = pl.program_id(0), pl.program_id(1)

        @pl.when(ti == 0)
        def _():
            dst_ref[...] = jnp.zeros_like(dst_ref)
            dlb_ref[...] = jnp.zeros_like(dlb_ref)

        @pl.when((ti == 0) & (hd == 0))
        def _():
            dgw_ref[...] = jnp.zeros_like(dgw_ref)

        q, k, logf, f, sig, one_m = _hgrn_prologue(q_ref, f_ref, lb_ref)
        G_s[...] = _blk_cumsum(logf, c)
        q_s[...] = q
        k_s[...] = k

        o = o_ref[...]
        gr = g_ref[...]
        dy_ = dy_ref[...]
        rr = lax.rsqrt(jnp.mean(o * o, axis=-1, keepdims=True) + RMS_EPS)
        n = o * rr
        sg = _silu(gr)
        gwv = gw_ref[...]
        dh_ref[3] = (dy_ * n * gwv * _dsilu(gr)).astype(ACT_DTYPE)
        dgw_ref[...] += jnp.sum(dy_ * n * sg, axis=0, keepdims=True)
        dn = dy_ * gwv * sg
        do_s[...] = rr * (dn - n * jnp.mean(dn * n, axis=-1, keepdims=True))

        ones = jnp.ones((LANES, LANES), MXU_DTYPE)
        rowid = lax.broadcasted_iota(jnp.int32, (SUB, LANES), 0)
        rowid_c = lax.broadcasted_iota(jnp.int32, (c, LANES), 0)
        zero = jnp.zeros((SUB, LANES), F32)
        cc = c * c
        for bi in range(nb):
            r0 = bi * c
            qd = q_s[pl.ds(r0, c), :] * jnp.exp(G_s[pl.ds(r0, c), :])
            dS_s[bi] = lax.dot_general(do_s[pl.ds(r0, c), :].astype(MXU_DTYPE), qd.astype(MXU_DTYPE), _TN,
                                       preferred_element_type=F32)
        dst = dst_ref[...]
        for bi in range(nb - 1, -1, -1):
            dstA_s[bi] = dst
            dst = dst * jnp.exp(G_s[pl.ds(bi * c + c - 1, 1), :]) + dS_s[bi]
        dst_ref[...] = dst
        for bi in range(nb):
            r0 = bi * c
            glast = G_s[pl.ds(r0 + c - 1, 1), :]
            kd = k_s[pl.ds(r0, c), :] * jnp.exp(glast - G_s[pl.ds(r0, c), :])
            st = sall_ref[bi]
            dstb = dstA_s[bi]
            dst_m = dstb.astype(MXU_DTYPE)
            dqd_s[pl.ds(r0, c), :] = lax.dot_general(do_s[pl.ds(r0, c), :].astype(MXU_DTYPE), st.astype(MXU_DTYPE),
                                                     _NN, preferred_element_type=F32)
            dkd_s[pl.ds(r0, c), :] = lax.dot_general(i_ref[pl.ds(r0, c), :].astype(MXU_DTYPE), dst_m, _NN,
                                                     preferred_element_type=F32)
            dvi_s[pl.ds(r0, c), :] = lax.dot_general(kd.astype(MXU_DTYPE), dst_m, _NT,
                                                     preferred_element_type=F32)
            da_s[pl.ds(bi * SUB, 1), :] = jnp.sum(dstb * st, axis=0, keepdims=True)
        for bi in range(nb):
            r0 = bi * c
            e0, w0 = bi * cc, bi * 2 * cc
            Gg = [G_s[pl.ds(r0 + gi * SUB, SUB), :] for gi in range(ng)]
            kg = [k_s[pl.ds(r0 + gi * SUB, SUB), :] for gi in range(ng)]
            vg = [i_ref[pl.ds(r0 + gi * SUB, SUB), :] for gi in range(ng)]
            for t in range(c):
                gt = G_s[pl.ds(r0 + t, 1), :]
                qt = q_s[pl.ds(r0 + t, 1), :]
                dot_ = do_s[pl.ds(r0 + t, 1), :]
                ep, wp, pp = [], [], []
                for gi in range(ng):
                    if gi > t // SUB:
                        ep.append(zero)
                        wp.append(zero)
                        pp.append(zero)
                        continue
                    e = jnp.exp(jnp.minimum(gt - Gg[gi], 0.0))
                    if gi == t // SUB:
                        e = jnp.where(rowid <= t - gi * SUB, e, 0.0)
                    ep.append(e)
                    wp.append(e * kg[gi] * qt)
                    pp.append(vg[gi] * dot_)
                E_s[pl.ds(e0 + t * c, c), :] = jnp.concatenate(ep, axis=0)
                WP_s[pl.ds(w0 + t * c, c), :] = jnp.concatenate(wp, axis=0).astype(MXU_DTYPE)
                WP_s[pl.ds(w0 + cc + t * c, c), :] = jnp.concatenate(pp, axis=0).astype(MXU_DTYPE)
        R_s[...] = jnp.dot(WP_s[...], ones, preferred_element_type=F32)
        for bi in range(nb):
            r0 = bi * c
            e0, w0 = bi * cc, bi * 2 * cc
            kg = [k_s[pl.ds(r0 + gi * SUB, SUB), :] for gi in range(ng)]
            dk_g = [zero] * ng
            dv_g = [zero] * ng
            dq_g = [zero] * ng
            for t in range(c):
                qt = q_s[pl.ds(r0 + t, 1), :]
                dot_ = do_s[pl.ds(r0 + t, 1), :]
                tot = None
                for gi in range(t // SUB + 1):
                    lo = t * c + gi * SUB
                    dae = R_s[pl.ds(w0 + cc + lo, SUB), :] * E_s[pl.ds(e0 + lo, SUB), :]
                    z = dae * kg[gi]
                    tot = z if tot is None else tot + z
                    dk_g[gi] = dk_g[gi] + dae * qt
                    dv_g[gi] = dv_g[gi] + R_s[pl.ds(w0 + lo, SUB), :] * dot_
                gt_ = t // SUB
                dq_g[gt_] = jnp.where(rowid == t - gt_ * SUB, jnp.sum(tot, axis=0, keepdims=True), dq_g[gt_])
            dq_i = jnp.concatenate(dq_g, axis=0)
            dk_i = jnp.concatenate(dk_g, axis=0)
            dv_i = jnp.concatenate(dv_g, axis=0)
            Gb = G_s[pl.ds(r0, c), :]
            qb = q_s[pl.ds(r0, c), :]
            kb = k_s[pl.ds(r0, c), :]
            glast = G_s[pl.ds(r0 + c - 1, 1), :]
            eg = jnp.exp(Gb)
            egl = jnp.exp(glast - Gb)
            dqd = dqd_s[pl.ds(r0, c), :]
            dkd = dkd_s[pl.ds(r0, c), :]
            dq_s[pl.ds(r0, c), :] = dqd * eg + dq_i
            dk_s[pl.ds(r0, c), :] = dkd * egl + dk_i
            dv_s[pl.ds(r0, c), :] = dvi_s[pl.ds(r0, c), :] + dv_i
            dkdkd = dkd * kb * egl
            dG = dqd * qb * eg + qb * dq_i - kb * dk_i - dkdkd
            dglast = jnp.sum(dkdkd, axis=0, keepdims=True) + da_s[pl.ds(bi * SUB, 1), :] * jnp.exp(glast)
            dG_s[pl.ds(r0, c), :] = dG + jnp.where(rowid_c == c - 1, dglast, 0.0)

        dlogf = _blk_cumsum(dG_s[...], c, reverse=True)
        df = jnp.where(f > F_MIN, dlogf / f, 0.0)
        dk = dk_s[...]
        dh_ref[0] = (dq_s[...] * _dsilu(q_ref[...])).astype(ACT_DTYPE)
        dh_ref[1] = ((df - dk) * one_m * sig * (1.0 - sig)).astype(ACT_DTYPE)
        dh_ref[2] = dv_s[...].astype(ACT_DTYPE)
        dlb_ref[...] += jnp.sum((df - dk) * (1.0 - sig), axis=0, keepdims=True)

    def sec(s):
        return pl.BlockSpec((None, Tt, LANES), lambda hd, i: (s, nT - 1 - i, hd))

    col = pl.BlockSpec((Tt, LANES), lambda hd, i: (nT - 1 - i, hd))
    tile = pltpu.VMEM((Tt, LANES), F32)
    return pl.pallas_call(
        body, name="hgrn_bwd", grid=(nh, nT),
        in_specs=[sec(0), sec(1), sec(2), sec(3), pl.BlockSpec((1, LANES), lambda hd, i: (0, hd)),
                  pl.BlockSpec((1, LANES), lambda hd, i: (0, 0)), col,
                  pl.BlockSpec((nb, None, LANES, LANES), lambda hd, i: (nT - 1 - i, hd, 0, 0)), col,
                  pl.BlockSpec(memory_space=pl.ANY)],
        out_specs=[pl.BlockSpec((4, Tt, LANES), lambda hd, i: (0, nT - 1 - i, hd)),
                   pl.BlockSpec((1, LANES), lambda hd, i: (0, hd)),
                   pl.BlockSpec((1, LANES), lambda hd, i: (0, 0))],
        out_shape=[jax.ShapeDtypeStruct(dh.shape, dh.dtype), jax.ShapeDtypeStruct((1, D), F32),
                   jax.ShapeDtypeStruct((1, LANES), F32)],
        scratch_shapes=[pltpu.VMEM((LANES, LANES), F32), tile, tile, tile, tile,
                        pltpu.VMEM((nb * c * c, LANES), F32), pltpu.VMEM((2 * nb * c * c, LANES), MXU_DTYPE),
                        tile, tile, tile, tile,
                        pltpu.VMEM((2 * nb * c * c, LANES), F32), pltpu.VMEM((nb, LANES, LANES), F32),
                        pltpu.VMEM((nb, LANES, LANES), F32), tile, tile, tile, pltpu.VMEM((nb * SUB, LANES), F32)],
        input_output_aliases={9: 0},
        compiler_params=_cp("arbitrary", "arbitrary"))(h, h, h, h, lbs_l, gw, o_pre, st_all, dy, dh)


def _shifted_copies(src, cs, dst, rows):
    for b in range(1, SUB):
        dst[b - 1] = src[pl.ds(b, rows + CONV_HALO - SUB), cs]


def _shifted(src, cs, copies, shift, rows):
    a8, b = divmod(shift, SUB)
    if b == 0:
        return src[pl.ds(shift, rows), cs]
    return copies[b - 1, pl.ds(a8 * SUB, rows), :]


def _conv_fwd(h, w_dw, b_dw, ln_g, ln_b):
    _, T, D = h.shape
    Tt = _pick(T, (256, 128, 64, 32))
    hb = Tt // CONV_HALO
    off = CONV_HALO - (CONV_WIDTH - 1)

    def body(a_ref, b_ref, ap_ref, bp_ref, w_ref, bd_ref, g_ref, be_ref, yc_ref, y_ref, U_s, Ub_s):
        first = pl.program_id(0) == 0
        up = ap_ref[...] * jax.nn.sigmoid(bp_ref[...])
        U_s[pl.ds(0, CONV_HALO), :] = jnp.where(first, 0.0, up)
        U_s[pl.ds(CONV_HALO, Tt), :] = a_ref[...] * jax.nn.sigmoid(b_ref[...])
        for cb in range(D // LANES):
            cs = pl.ds(cb * LANES, LANES)
            _shifted_copies(U_s, cs, Ub_s, Tt)
            acc = jnp.zeros((Tt, LANES), F32)
            for j in range(CONV_WIDTH):
                acc = acc + w_ref[pl.ds(j, 1), cs] * _shifted(U_s, cs, Ub_s, off + j, Tt)
            yc_ref[:, cs] = acc + bd_ref[:, cs]
        yc = yc_ref[...]
        mu = jnp.mean(yc, axis=-1, keepdims=True)
        zc = yc - mu
        var = jnp.mean(zc * zc, axis=-1, keepdims=True)
        ln = zc * lax.rsqrt(var + LN_EPS) * g_ref[...] + be_ref[...]
        y_ref[...] = _silu(ln).astype(ACT_DTYPE)

    def main(s):
        return pl.BlockSpec((None, Tt, D), lambda i: (s, i, 0))

    def prev(s):
        return pl.BlockSpec((None, CONV_HALO, D), lambda i: (s, jnp.maximum(i * hb - 1, 0), 0))

    row = pl.BlockSpec((Tt, D), lambda i: (i, 0))
    vec = pl.BlockSpec((1, D), lambda i: (0, 0))
    return pl.pallas_call(
        body, name="conv_fwd", grid=(T // Tt,),
        in_specs=[main(4), main(5), prev(4), prev(5), pl.BlockSpec((CONV_HALO, D), lambda i: (0, 0)),
                  vec, vec, vec],
        out_specs=[row, row],
        out_shape=[jax.ShapeDtypeStruct((T, D), F32), jax.ShapeDtypeStruct((T, D), ACT_DTYPE)],
        scratch_shapes=[pltpu.VMEM((CONV_HALO + Tt, D), F32),
                        pltpu.VMEM((SUB - 1, Tt + CONV_HALO - SUB, LANES), F32)],
        compiler_params=_cp("parallel"))(h, h, h, h, w_dw, b_dw, ln_g, ln_b)


def _conv_bwd(h, w_dw, ln_g, ln_b, yc, dy, dh):
    _, T, D = h.shape
    Tt = _pick(T, (256, 128, 64, 32))
    hb = Tt // CONV_HALO
    nT = T // Tt
    nhb = T // CONV_HALO
    off = CONV_HALO - (CONV_WIDTH - 1)

    def body(a_ref, b_ref, ap_ref, bp_ref, w_ref, g_ref, be_ref, yc_ref, ycn_ref, dy_ref, dyn_ref, dh_in_ref,
             dh_ref, dw_ref, dbd_ref, dg_ref, dbe_ref, U_s, DY_s, du_s, Ub_s, DYb_s):
        del dh_in_ref
        i = pl.program_id(0)

        @pl.when(i == 0)
        def _():
            dw_ref[...] = jnp.zeros_like(dw_ref)
            dbd_ref[...] = jnp.zeros_like(dbd_ref)
            dg_ref[...] = jnp.zeros_like(dg_ref)
            dbe_ref[...] = jnp.zeros_like(dbe_ref)

        gv = g_ref[...]
        bev = be_ref[...]

        def ln_silu_bwd(ycv, dyv):
            mu = jnp.mean(ycv, axis=-1, keepdims=True)
            zc = ycv - mu
            rstd = lax.rsqrt(jnp.mean(zc * zc, axis=-1, keepdims=True) + LN_EPS)
            xhat = zc * rstd
            dln = dyv * _dsilu(xhat * gv + bev)
            dxh = dln * gv
            dyc = rstd * (dxh - jnp.mean(dxh, axis=-1, keepdims=True)
                          - xhat * jnp.mean(dxh * xhat, axis=-1, keepdims=True))
            return dyc, dln, xhat

        dyc, dln, xhat = ln_silu_bwd(yc_ref[...], dy_ref[...])
        dg_ref[...] += jnp.sum(dln * xhat, axis=0, keepdims=True)
        dbe_ref[...] += jnp.sum(dln, axis=0, keepdims=True)
        dbd_ref[...] += jnp.sum(dyc, axis=0, keepdims=True)
        DY_s[pl.ds(0, Tt), :] = dyc
        dycn, _, _ = ln_silu_bwd(ycn_ref[...], dyn_ref[...])
        DY_s[pl.ds(Tt, CONV_HALO), :] = jnp.where(i == nT - 1, 0.0, dycn)

        sb = jax.nn.sigmoid(b_ref[...])
        av = a_ref[...]
        up = ap_ref[...] * jax.nn.sigmoid(bp_ref[...])
        U_s[pl.ds(0, CONV_HALO), :] = jnp.where(i == 0, 0.0, up)
        U_s[pl.ds(CONV_HALO, Tt), :] = av * sb

        for cb in range(D // LANES):
            cs = pl.ds(cb * LANES, LANES)
            _shifted_copies(U_s, cs, Ub_s, Tt)
            _shifted_copies(DY_s, cs, DYb_s, Tt)
            dyb = DY_s[pl.ds(0, Tt), cs]
            acc = jnp.zeros((Tt, LANES), F32)
            for j in range(CONV_WIDTH):
                acc = acc + w_ref[pl.ds(j, 1), cs] * _shifted(DY_s, cs, DYb_s, CONV_WIDTH - 1 - j, Tt)
                dw_ref[pl.ds(j, 1), cs] += jnp.sum(dyb * _shifted(U_s, cs, Ub_s, off + j, Tt), axis=0, keepdims=True)
            du_s[:, cs] = acc
        du = du_s[...]
        dh_ref[0] = (du * sb).astype(ACT_DTYPE)
        dh_ref[1] = (du * av * sb * (1.0 - sb)).astype(ACT_DTYPE)

    def main(s):
        return pl.BlockSpec((None, Tt, D), lambda i: (s, i, 0))

    def prev(s):
        return pl.BlockSpec((None, CONV_HALO, D), lambda i: (s, jnp.maximum(i * hb - 1, 0), 0))

    row = pl.BlockSpec((Tt, D), lambda i: (i, 0))
    nxt = pl.BlockSpec((CONV_HALO, D), lambda i: (jnp.minimum((i + 1) * hb, nhb - 1), 0))
    vec = pl.BlockSpec((1, D), lambda i: (0, 0))
    wspec = pl.BlockSpec((CONV_HALO, D), lambda i: (0, 0))
    return pl.pallas_call(
        body, name="conv_bwd", grid=(nT,),
        in_specs=[main(4), main(5), prev(4), prev(5), wspec, vec, vec, row, nxt, row, nxt,
                  pl.BlockSpec(memory_space=pl.ANY)],
        out_specs=[pl.BlockSpec((2, Tt, D), lambda i: (2, i, 0)), wspec, vec, vec, vec],
        out_shape=[jax.ShapeDtypeStruct(dh.shape, dh.dtype), jax.ShapeDtypeStruct((CONV_HALO, D), F32),
                   jax.ShapeDtypeStruct((1, D), F32), jax.ShapeDtypeStruct((1, D), F32),
                   jax.ShapeDtypeStruct((1, D), F32)],
        scratch_shapes=[pltpu.VMEM((CONV_HALO + Tt, D), F32), pltpu.VMEM((Tt + CONV_HALO, D), F32),
                        pltpu.VMEM((Tt, D), F32),
                        pltpu.VMEM((SUB - 1, Tt + CONV_HALO - SUB, LANES), F32),
                        pltpu.VMEM((SUB - 1, Tt + CONV_HALO - SUB, LANES), F32)],
        input_output_aliases={11: 0},
        compiler_params=_cp("arbitrary"))(h, h, h, h, w_dw, ln_g, ln_b, yc, yc, dy, dy, dh)


def _adamw(name, w, m, v, parts, part_specs, tr, prefetch=None, nsteps=None, row_map=None, prev=None):
    R, C = w.shape
    bc1 = 1.0 - ADAM_B1 ** ADAM_STEP
    bc2 = 1.0 - ADAM_B2 ** ADAM_STEP
    npart = len(parts)
    npre = 0 if prefetch is None else 1
    nprev = 0 if prev is None else 4

    def body(*refs):
        refs = refs[npre:]
        w_ref, m_ref, v_ref = refs[:3]
        p_refs = refs[3:3 + npart]
        g_ref, d_ref, mo_ref, vo_ref = refs[3 + npart + nprev:]
        g = p_refs[0][...].astype(F32)
        for p in p_refs[1:]:
            g = g + p[...].astype(F32)
        wv = w_ref[...]
        mn = ADAM_B1 * m_ref[...] + (1.0 - ADAM_B1) * g
        vn = ADAM_B2 * v_ref[...] + (1.0 - ADAM_B2) * (g * g)
        m_hat = mn / bc1
        v_hat = vn / bc2
        g_ref[...] = g
        d_ref[...] = -ADAM_LR * (m_hat / (jnp.sqrt(v_hat) + ADAM_EPS) + ADAM_WD * wv)
        mo_ref[...] = mn
        vo_ref[...] = vn

    if row_map is None:
        row_map = (lambda i: (i, 0)) if prefetch is None else (lambda i, s: (i, 0))
    row = pl.BlockSpec((tr, C), row_map)
    out = jax.ShapeDtypeStruct((R, C), F32)
    gs = pltpu.PrefetchScalarGridSpec(
        num_scalar_prefetch=npre, grid=(R // tr if nsteps is None else nsteps,),
        in_specs=[row, row, row] + list(part_specs) + [_ANY] * nprev, out_specs=[row] * 4)
    args = ([prefetch] if npre else []) + [w, m, v] + list(parts) + (list(prev) if nprev else [])
    first_prev = npre + 3 + npart
    return pl.pallas_call(body, name=name, grid_spec=gs, out_shape=[out] * 4,
                          input_output_aliases={first_prev + i: i for i in range(nprev)},
                          compiler_params=_cp("parallel"))(*args)


def _pair_add(p, r1, my_c):
    _, R, C = r1.shape
    tr = max(t for t in range(16, 1025, 16) if R % t == 0)

    def body(c_ref, p_ref, r_ref, q_ref):
        del c_ref
        q_ref[...] = (p_ref[...].astype(F32) + r_ref[...].astype(F32)).astype(q_ref.dtype)

    gs = pltpu.PrefetchScalarGridSpec(
        num_scalar_prefetch=1, grid=(4, R // tr),
        in_specs=[pl.BlockSpec((None, tr, C), lambda j, i, c: (2 * j + c[0], i, 0)),
                  pl.BlockSpec((None, tr, C), lambda j, i, c: (j, i, 0))],
        out_specs=pl.BlockSpec((None, tr, C), lambda j, i, c: (j, i, 0)))
    return pl.pallas_call(body, name="pair_add", grid_spec=gs, out_shape=jax.ShapeDtypeStruct(r1.shape, r1.dtype),
                          compiler_params=_cp("parallel", "parallel"))(my_c, p, r1)


def _place():
    x, y, c = lax.axis_index("x"), lax.axis_index("y"), lax.axis_index("c")
    chips = [(1 - x, y), (x, 1 - y), (1 - x, 1 - y)]
    return x, y, c, chips


def _hbm(a):
    return pltpu.with_memory_space_constraint(a, pltpu.HBM)


def _gather_targets():
    x, y, c, chips = _place()
    return 4 * x + 2 * y + c, [(x, y, 1 - c)] + [(*chip, c) for chip in chips]


def _gather_start(name, shards, zones, after=None):
    n = len(shards)
    lands = [_hbm(z) for z in zones]
    n_in = 2 * n + (0 if after is None else 1)

    def body(*refs):
        srcs, zones = refs[:n], refs[n:2 * n]
        send, recv, token = refs[n_in], refs[n_in + 1], refs[-1]
        mine, targets = _gather_targets()
        for a in range(n):
            for k, to in enumerate(targets):
                pltpu.make_async_remote_copy(
                    src_ref=srcs[a], dst_ref=zones[a].at[mine], send_sem=send.at[4 * a + k],
                    recv_sem=recv.at[4 * a + k], device_id=to, device_id_type=MESH).start()
        token[...] = jnp.zeros_like(token)

    sem = pltpu.SemaphoreType.DMA((4 * n,))
    out_shape = ([sem, sem] + [pltpu.HBM(s.shape, s.dtype) for s in shards]
                 + [pltpu.HBM(z.shape, z.dtype) for z in lands] + [jax.ShapeDtypeStruct((8, LANES), F32)])
    outs = pl.pallas_call(
        body, name=name, out_shape=out_shape, in_specs=[_HBM] * (2 * n) + ([] if after is None else [_ANY]),
        out_specs=[_SEM, _SEM] + [_HBM] * (2 * n) + [pl.BlockSpec(memory_space=pltpu.VMEM)],
        input_output_aliases={i: 2 + i for i in range(2 * n)},
        compiler_params=pltpu.CompilerParams(has_side_effects=_EFFECT))(
            *[_hbm(s) for s in shards], *lands, *([] if after is None else [after]))
    return outs[0], outs[1], list(outs[2:2 + n]), list(outs[2 + n:2 + 2 * n]), outs[-1]


def _gather_wait(name, shards, zones, send, recv, after):
    per = len(shards)

    def body(*refs):
        srcs, lz = refs[:per], refs[per:2 * per]
        send_s, recv_s = refs[2 * per], refs[2 * per + 1]
        mine, targets = _gather_targets()
        for a in range(per):
            for k, to in enumerate(targets):
                cp = pltpu.make_async_remote_copy(
                    src_ref=srcs[a], dst_ref=lz[a].at[mine], send_sem=send_s.at[4 * a + k],
                    recv_sem=recv_s.at[4 * a + k], device_id=to, device_id_type=MESH)
                cp.wait_send()
                cp.wait_recv()

    outs = pl.pallas_call(
        body, name=name, out_shape=[pltpu.HBM(s.shape, s.dtype) for s in shards + zones],
        in_specs=[_HBM] * (2 * per) + [_SEM, _SEM, _ANY], out_specs=[_HBM] * (2 * per),
        input_output_aliases={i: i for i in range(2 * per)},
        compiler_params=pltpu.CompilerParams(has_side_effects=_EFFECT))(*shards, *zones, send, recv, after)
    return outs[:per], outs[per:]


def _gather_finish(zones):
    n = len(zones)

    def body(*refs):
        lz = refs[n:2 * n]
        send_sems, recv_sems = refs[2 * n:]
        x, y, c, chips = _place()

        def fwd(a, j, pc):
            cx, cy = chips[j]
            blk = lz[a].at[4 * cx + 2 * cy + pc]
            return pltpu.make_async_remote_copy(
                src_ref=blk, dst_ref=blk, send_sem=send_sems.at[3 * a + j], recv_sem=recv_sems.at[3 * a + j],
                device_id=(x, y, 1 - c), device_id_type=MESH)

        sends = [fwd(a, j, c) for a in range(n) for j in range(3)]
        for cp in sends:
            cp.start()
        for a in range(n):
            for j in range(3):
                fwd(a, j, 1 - c).wait_recv()
        for cp in sends:
            cp.wait_send()

    return pl.pallas_call(
        body, name="gather_finish", out_shape=[jax.ShapeDtypeStruct(z.shape, z.dtype) for z in zones],
        in_specs=[_ANY] * n, out_specs=[_ANY] * n, input_output_aliases={a: a for a in range(n)},
        scratch_shapes=[pltpu.SemaphoreType.DMA((3 * n,)), pltpu.SemaphoreType.DMA((3 * n,))])(*zones)


def _place_own(shard, dev):
    R, C = shard.shape
    tr = max(t for t in range(16, 1025, 16) if R % t == 0)

    def body(d_ref, s_ref, z_ref):
        del d_ref
        z_ref[...] = s_ref[...]

    gs = pltpu.PrefetchScalarGridSpec(
        num_scalar_prefetch=1, grid=(R // tr,), in_specs=[pl.BlockSpec((tr, C), lambda i, d: (i, 0))],
        out_specs=pl.BlockSpec((None, tr, C), lambda i, d: (d[0], i, 0)))
    return pl.pallas_call(body, name="place_own", grid_spec=gs,
                          out_shape=jax.ShapeDtypeStruct((N_DEV, R, C), shard.dtype),
                          compiler_params=_cp("parallel"))(dev, shard)


def _exchange_sibling(bufs):
    n_arr = len(bufs)

    def body(*refs):
        srcs, outs = refs[:n_arr], refs[n_arr:2 * n_arr]
        send_sems, recv_sems = refs[2 * n_arr:]
        x, y, c, _ = _place()
        copies = []
        for n in range(n_arr):
            for j in range(4):
                copies.append(pltpu.make_async_remote_copy(
                    src_ref=srcs[n].at[2 * j + 1 - c], dst_ref=outs[n].at[j],
                    send_sem=send_sems.at[4 * n + j], recv_sem=recv_sems.at[4 * n + j],
                    device_id=(x, y, 1 - c), device_id_type=MESH))
        for cp in copies:
            cp.start()
        for cp in copies:
            cp.wait()

    return pl.pallas_call(
        body, name="exchange_sibling",
        out_shape=[jax.ShapeDtypeStruct((4,) + b.shape[1:], b.dtype) for b in bufs],
        in_specs=[_ANY] * n_arr, out_specs=[_ANY] * n_arr,
        scratch_shapes=[pltpu.SemaphoreType.DMA((4 * n_arr,)), pltpu.SemaphoreType.DMA((4 * n_arr,))])(*bufs)


def _chip_copies(srcs, zones, send, recv):
    _, _, c, chips = _place()
    return [pltpu.make_async_remote_copy(
        src_ref=srcs[n].at[2 * cx + cy], dst_ref=zones[n].at[k], send_sem=send.at[3 * n + k],
        recv_sem=recv.at[3 * n + k], device_id=(cx, cy, c), device_id_type=MESH)
        for n in range(len(srcs)) for k, (cx, cy) in enumerate(chips)]


def _exchange_chips_start(name, bufs, after=None):
    n = len(bufs)
    n_in = 2 * n + (0 if after is None else 1)
    lands = [_hbm(lax.empty((3,) + b.shape[1:], b.dtype)) for b in bufs]

    def body(*refs):
        srcs, zones = refs[:n], refs[n:2 * n]
        send, recv, token = refs[n_in], refs[n_in + 1], refs[-1]
        for cp in _chip_copies(srcs, zones, send, recv):
            cp.start()
        token[...] = jnp.zeros_like(token)

    sem = pltpu.SemaphoreType.DMA((3 * n,))
    outs = pl.pallas_call(
        body, name=name,
        out_shape=[sem, sem] + [pltpu.HBM(b.shape, b.dtype) for b in bufs]
        + [pltpu.HBM(z.shape, z.dtype) for z in lands] + [jax.ShapeDtypeStruct((8, LANES), F32)],
        in_specs=[_HBM] * (2 * n) + ([] if after is None else [_ANY]),
        out_specs=[_SEM, _SEM] + [_HBM] * (2 * n) + [pl.BlockSpec(memory_space=pltpu.VMEM)],
        input_output_aliases={i: 2 + i for i in range(2 * n)},
        compiler_params=pltpu.CompilerParams(has_side_effects=_EFFECT))(
            *[_hbm(b) for b in bufs], *lands, *([] if after is None else [after]))
    return outs[0], outs[1], outs[2:2 + n], outs[2 + n:2 + 2 * n], outs[-1]


def _exchange_chips_wait(name, bufs, zones, send, recv, after):
    n = len(bufs)

    def body(*refs):
        for cp in _chip_copies(refs[:n], refs[n:2 * n], refs[2 * n], refs[2 * n + 1]):
            cp.wait_send()
            cp.wait_recv()

    outs = pl.pallas_call(
        body, name=name, out_shape=[pltpu.HBM(a.shape, a.dtype) for a in list(bufs) + list(zones)],
        in_specs=[_HBM] * (2 * n) + [_SEM, _SEM, _ANY], out_specs=[_HBM] * (2 * n),
        input_output_aliases={i: i for i in range(2 * n)},
        compiler_params=pltpu.CompilerParams(has_side_effects=_EFFECT))(*bufs, *zones, send, recv, after)
    return outs[n:]


def _all_gather_small(part):
    def body(src, out, send_sems, recv_sems, local_sem):
        x, y, c, _ = _place()
        mine = pltpu.make_async_copy(src, out.at[4 * x + 2 * y + c], local_sem)
        mine.start()
        copies = []
        for r in range(1, N_DEV):
            dx, dy, dc = (r >> 2) & 1, (r >> 1) & 1, r & 1
            peer = (1 - x if dx else x, 1 - y if dy else y, 1 - c if dc else c)
            copies.append(pltpu.make_async_remote_copy(
                src_ref=src, dst_ref=out.at[4 * x + 2 * y + c],
                send_sem=send_sems.at[r - 1], recv_sem=recv_sems.at[r - 1],
                device_id=peer, device_id_type=MESH))
        for cp in copies:
            cp.start()
        for cp in copies:
            cp.wait()
        mine.wait()

    return pl.pallas_call(
        body, name="all_gather_small",
        out_shape=jax.ShapeDtypeStruct((N_DEV,) + part.shape, part.dtype),
        in_specs=[_ANY], out_specs=_ANY,
        scratch_shapes=[pltpu.SemaphoreType.DMA((N_DEV - 1,)), pltpu.SemaphoreType.DMA((N_DEV - 1,)),
                        pltpu.SemaphoreType.DMA])(part)


def _layer_fwd(xin, xin_bf, W, P, alpha, dep=None):
    h = _proj_in(xin_bf, W["w_in"], P["b_in"], dep=dep)
    o_pre, y_hg, st_all = _hgrn_fwd(h, P["lbs"], P["g_norm_w"])
    yc_pre, y_cv = _conv_fwd(h, P["w_dw"], P["b_dw"], P["conv_ln_g"], P["conv_ln_b"])
    y_h = _mm_nn("branch_a", y_hg, W["w_a"], F32)
    y_c = _mm_nn("branch_b", y_cv, W["w_b"], F32, bias=P["b_b"])
    merged = _gate_fwd(y_h, y_c, h)
    mix = _mm_nn("mix_out", merged, W["w_o"], F32)
    x1, x1_bf, z1 = _ln_fwd("ln1", xin, mix, alpha, P["ln1_g"], P["ln1_b"])
    up = _ffn_up(x1_bf, W["w_up"])
    act = _swiglu_fwd(up)
    ffn = _mm_nn("ffn_down", act, W["w_down"], F32)
    x2, x2_bf, z2 = _ln_fwd("ln2", x1, ffn, alpha, P["ln2_g"], P["ln2_b"])
    saved = dict(xin_bf=xin_bf, h=h, o_pre=o_pre, y_hg=y_hg, st_all=st_all, yc_pre=yc_pre, y_cv=y_cv,
                 y_h=y_h, y_c=y_c, merged=merged, z1=z1, x1_bf=x1_bf, up=up, act=act, z2=z2)
    return x2, x2_bf, saved


def _layer_bwd(dx2, S, W, P, alpha, dep=None):
    dz2, dz2_bf, dln2_g, dln2_b = _ln_bwd("ln2_bwd", S["z2"], dx2, P["ln2_g"], dep=dep)
    dact = _mm_nt("ffn_down_dx", dz2_bf, W["w_down"], F32)
    dw_down = _mm_tn("ffn_down_dw", S["act"], dz2_bf, ACT_DTYPE)
    dup = _swiglu_bwd(dact, S["up"])
    dx1 = _ffn_up_dx(dup, W["w_up"], dz2, alpha)
    dw_up = _ffn_up_dw(S["x1_bf"], dup)
    dz1, dz1_bf, dln1_g, dln1_b = _ln_bwd("ln1_bwd", S["z1"], dx1, P["ln1_g"])
    dmerged = _mm_nt("mix_out_dx", dz1_bf, W["w_o"], F32)
    dw_o = _mm_tn("mix_out_dw", S["merged"], dz1_bf, ACT_DTYPE)
    dy_h, dy_c, db_b, dh = _gate_bwd(dmerged, S["y_h"], S["y_c"], S["h"])
    dy_cv = _mm_nt("branch_b_dx", dy_c, W["w_b"], F32)
    dw_b = _mm_tn("branch_b_dw", S["y_cv"], dy_c, ACT_DTYPE)
    dy_hg = _mm_nt("branch_a_dx", dy_h, W["w_a"], F32)
    dw_a = _mm_tn("branch_a_dw", S["y_hg"], dy_h, ACT_DTYPE)
    dh, dw_dw, db_dw, dcln_g, dcln_b = _conv_bwd(S["h"], P["w_dw"], P["conv_ln_g"], P["conv_ln_b"],
                                                 S["yc_pre"], dy_cv, dh)
    dh, dlbs, dgw = _hgrn_bwd(S["h"], P["lbs"], P["g_norm_w"], S["o_pre"], S["st_all"], dy_hg, dh)
    dxin = _proj_in_dx(dh, W["w_in"], dz1, alpha)
    dw_in, db_in = _proj_in_dw(S["xin_bf"], dh)
    big = dict(w_in=dw_in, w_a=dw_a, w_b=dw_b, w_o=dw_o, w_down=dw_down, w_up=dw_up)
    small = dict(b_in=db_in, lbs=dlbs, g_norm_w=dgw, w_dw=dw_dw, b_dw=db_dw, conv_ln_g=dcln_g,
                 conv_ln_b=dcln_b, b_b=db_b, ln1_g=dln1_g, ln1_b=dln1_b, ln2_g=dln2_g, ln2_b=dln2_b)
    return dxin, big, small


_SMALL = ("b_in", "lb_logits", "g_norm_w", "b_dw", "conv_ln_g", "conv_ln_b", "b_b", "ln1_g", "ln1_b", "ln2_g",
          "ln2_b")


def _pack_small(per_layer, ln0_g, ln0_b, extra_row, D, L):
    rows = []
    for l in range(L):
        for n in _SMALL:
            a = per_layer[n][l]
            if n == "b_in":
                rows.append(a.reshape(N_SEC, D))
            elif n == "g_norm_w":
                rows.append(jnp.pad(a.reshape(1, -1), ((0, 0), (0, D - a.size))))
            else:
                rows.append(a.reshape(1, D))
    rows += [ln0_g.reshape(1, D), ln0_b.reshape(1, D), extra_row]
    buf = jnp.concatenate(rows, axis=0)
    pad = (-buf.shape[0]) % 8
    return jnp.pad(buf, ((0, pad), (0, 0)))


def _unpack_small(buf, D, L, hv):
    out = {n: [] for n in _SMALL}
    r = 0
    for l in range(L):
        for n in _SMALL:
            if n == "b_in":
                out[n].append(buf[r:r + N_SEC].reshape(N_SEC * D))
                r += N_SEC
            elif n == "g_norm_w":
                out[n].append(buf[r, :hv])
                r += 1
            else:
                out[n].append(buf[r])
                r += 1
    res = {n: jnp.stack(v) for n, v in out.items()}
    res["ln0_g"] = buf[r]
    res["ln0_b"] = buf[r + 1]
    return res, r + 2


def kernel(x, ln0_g, ln0_b, w_in, b_in, lb_logits, g_norm_w, w_a, w_dw, b_dw, conv_ln_g, conv_ln_b, w_b, b_b, w_o, ln1_g, ln1_b, w_up, w_down, ln2_g, ln2_b, loss_target, m_ln0_g, m_ln0_b, m_w_in, m_b_in, m_lb_logits, m_g_norm_w, m_w_a, m_w_dw, m_b_dw, m_conv_ln_g, m_conv_ln_b, m_w_b, m_b_b, m_w_o, m_ln1_g, m_ln1_b, m_w_up, m_w_down, m_ln2_g, m_ln2_b, v_ln0_g, v_ln0_b, v_w_in, v_b_in, v_lb_logits, v_g_norm_w, v_w_a, v_w_dw, v_b_dw, v_conv_ln_g, v_conv_ln_b, v_w_b, v_b_b, v_w_o, v_ln1_g, v_ln1_b, v_w_up, v_w_down, v_ln2_g, v_ln2_b):
    L, D = w_in.shape[0], w_in.shape[1]
    T = x.shape[0] * x.shape[1]
    Dn = w_in.shape[2]
    rs = w_a.shape[1]
    rd = w_down.shape[1]
    cu = w_up.shape[2]
    F = rd * N_DEV
    hv = g_norm_w.shape[1]
    alpha = (2 * L) ** 0.25
    my_x, my_y, my_c = lax.axis_index("x"), lax.axis_index("y"), lax.axis_index("c")
    dev_arr = jnp.reshape(4 * my_x + 2 * my_y + my_c, (1,)).astype(jnp.int32)

    o_a, o_b, o_o, o_d = D, D + rs, D + 2 * rs, D + 3 * rs
    taps = jnp.pad(w_dw, ((0, 0), (0, CONV_HALO - CONV_WIDTH), (0, 0))).reshape(L * CONV_HALO, w_dw.shape[2])
    taps_all = _all_gather_small(taps)
    w_dw_full = taps_all.transpose(1, 0, 2).reshape(L, CONV_HALO, D)

    gathered = [None] * L
    started = [None] * L

    def start_gather(l, after):
        shards = [jnp.concatenate([w_in[l], w_a[l], w_b[l], w_o[l], w_down[l]], axis=0).astype(ACT_DTYPE),
                  w_up[l].astype(ACT_DTYPE)]
        started[l] = _gather_start("gather_start_%d" % l, shards, [_place_own(s, dev_arr) for s in shards], after)
        return started[l][4]

    def weights(l):
        ga, gb = gathered[l]
        return dict(
            w_in=ga,
            w_a=ga[:, o_a:o_a + rs, :].reshape(D, D),
            w_b=ga[:, o_b:o_b + rs, :].reshape(D, D),
            w_o=ga[:, o_o:o_o + rs, :].reshape(D, D),
            w_down=ga[:, o_d:o_d + rd, :].reshape(F, D),
            w_up=gb.transpose(1, 0, 2).reshape(D, 2 * F))

    def finish_gather(l, after):
        send, recv, thru, zone, _ = started[l]
        sh, zn = _gather_wait("gather_wait_%d" % l, thru, zone, send, recv, after)
        gathered[l] = _gather_finish(zn)

    lbs = _lb_fwd(lb_logits)

    def params(l):
        return dict(b_in=b_in[l].reshape(N_SEC, 1, D), lbs=lbs[l].reshape(1, D), g_norm_w=g_norm_w[l].reshape(1, hv),
                    w_dw=w_dw_full[l], b_dw=b_dw[l].reshape(1, D), conv_ln_g=conv_ln_g[l].reshape(1, D),
                    conv_ln_b=conv_ln_b[l].reshape(1, D), b_b=b_b[l].reshape(1, D), ln1_g=ln1_g[l], ln1_b=ln1_b[l],
                    ln2_g=ln2_g[l], ln2_b=ln2_b[l])

    x2d = x.reshape(T, D)
    token = start_gather(0, taps_all)
    xc, xc_bf = _ln_fwd("ln0", x2d, None, 1.0, ln0_g, ln0_b, dep=token)
    finish_gather(0, xc_bf)
    saved = []
    for l in range(L):
        token = start_gather(l + 1, gathered[l][0]) if l + 1 < L else None
        xc, xc_bf, s = _layer_fwd(xc, xc_bf, weights(l), params(l), alpha, dep=token)
        saved.append(s)
        if l + 1 < L:
            finish_gather(l + 1, xc_bf)

    c_arr = jnp.reshape(my_c, (1,)).astype(jnp.int32)
    chip = 2 * my_x + my_y
    dx, loss_row = _loss_fwd_bwd(xc, loss_target.reshape(T, D))
    small = [None] * L
    pending = None
    upd_big = {n: None for n in ("w_in", "w_a", "w_b", "w_o", "w_down", "w_up")}
    wmv = dict(w_in=(w_in, m_w_in, v_w_in), w_a=(w_a, m_w_a, v_w_a), w_b=(w_b, m_w_b, v_w_b),
               w_o=(w_o, m_w_o, v_w_o), w_down=(w_down, m_w_down, v_w_down), w_up=(w_up, m_w_up, v_w_up))

    def update_layer(l, q, r2):
        pre = jnp.stack([chip, jnp.int32(l)]).astype(jnp.int32)
        for k, name in enumerate(upd_big):
            w, m, v = wmv[name]
            r, C = w.shape[1], w.shape[2]
            tr = max(t for t in range(16, 513, 16) if r % t == 0)
            nb = r // tr
            specs = [pl.BlockSpec((None, tr, C), lambda i, s: (s[0], i, 0))]
            specs += [pl.BlockSpec((None, tr, C), functools.partial(lambda i, s, j: (j, i, 0), j=j)) for j in range(3)]
            upd_big[name] = _adamw(
                "adamw_" + name, w.reshape(L * r, C), m.reshape(L * r, C), v.reshape(L * r, C),
                [q[k], r2[k], r2[k], r2[k]], specs, tr, prefetch=pre, nsteps=nb,
                row_map=functools.partial(lambda i, s, nb: (s[1] * nb + i, 0), nb=nb), prev=upd_big[name])

    def finish_reduce(after):
        l, sems, q, zones = pending
        r2 = _exchange_chips_wait("reduce_wait_%d" % l, q, zones, sems[0], sems[1], after)
        update_layer(l, q, r2)

    token = None
    for l in range(L - 1, -1, -1):
        dx, big, small[l] = _layer_bwd(dx, saved[l], weights(l), params(l), alpha, dep=token)
        if pending is not None:
            finish_reduce(dx)
        sends = [big["w_in"], big["w_a"].reshape(N_DEV, rs, D), big["w_b"].reshape(N_DEV, rs, D),
                 big["w_o"].reshape(N_DEV, rs, D), big["w_down"].reshape(N_DEV, rd, D),
                 big["w_up"].reshape(D, N_DEV, cu).transpose(1, 0, 2)]
        r1 = _exchange_sibling(sends)
        qs = [_pair_add(p, r, c_arr) for p, r in zip(sends, r1)]
        if l > 0:
            s_send, s_recv, q_thru, zones, token = _exchange_chips_start("reduce_start_%d" % l, qs)
            pending = (l, (s_send, s_recv), list(q_thru), list(zones))
    dx0, _, dln0_g, dln0_b = _ln_bwd("ln0_bwd", x2d, dx, ln0_g)
    dlb_logits = _lb_bwd(lb_logits, jnp.concatenate([small[l]["lbs"] for l in range(L)], axis=0))

    small_l = {n: [small[l][n] for l in range(L)] for n in _SMALL if n != "lb_logits"}
    small_l["lb_logits"] = [dlb_logits[l] for l in range(L)]
    loss_pad = jnp.pad(loss_row, ((0, 0), (0, D - LANES)))
    part = jnp.concatenate([_pack_small(small_l, dln0_g, dln0_b, loss_pad, D, L)]
                           + [small[l]["w_dw"] for l in range(L)], axis=0)
    parts_all = _all_gather_small(part)
    n_small = part.shape[0] - L * CONV_HALO

    s_send, s_recv, q_thru, zones, _ = _exchange_chips_start("reduce_start_0", qs, after=parts_all)
    pending = (0, (s_send, s_recv), list(q_thru), list(zones))


    inputs = dict(b_in=(b_in, m_b_in, v_b_in), lb_logits=(lb_logits, m_lb_logits, v_lb_logits),
                  g_norm_w=(g_norm_w, m_g_norm_w, v_g_norm_w), b_dw=(b_dw, m_b_dw, v_b_dw),
                  conv_ln_g=(conv_ln_g, m_conv_ln_g, v_conv_ln_g), conv_ln_b=(conv_ln_b, m_conv_ln_b, v_conv_ln_b),
                  b_b=(b_b, m_b_b, v_b_b), ln1_g=(ln1_g, m_ln1_g, v_ln1_g), ln1_b=(ln1_b, m_ln1_b, v_ln1_b),
                  ln2_g=(ln2_g, m_ln2_g, v_ln2_g), ln2_b=(ln2_b, m_ln2_b, v_ln2_b))
    zero_row = jnp.zeros((1, D), F32)
    packed = [_pack_small({n: [inputs[n][i][l] for l in range(L)] for n in _SMALL},
                          (ln0_g, m_ln0_g, v_ln0_g)[i], (ln0_b, m_ln0_b, v_ln0_b)[i], zero_row, D, L)
              for i in range(3)]
    small_specs = [pl.BlockSpec((None, n_small, D), functools.partial(lambda i, d: (d, 0, 0), d=d))
                   for d in range(N_DEV)]
    s_out = _adamw("adamw_small", packed[0], packed[1], packed[2], [parts_all] * N_DEV, small_specs, n_small)
    s_g, n_rows = _unpack_small(s_out[0], D, L, hv)
    s_d, _ = _unpack_small(s_out[1], D, L, hv)
    s_m, _ = _unpack_small(s_out[2], D, L, hv)
    s_v, _ = _unpack_small(s_out[3], D, L, hv)
    loss = s_out[0][n_rows, 0]

    cw = w_dw.shape[2]
    dev = 4 * my_x + 2 * my_y + my_c
    tap_parts = lax.dynamic_slice_in_dim(parts_all[:, n_small:, :], dev * cw, cw, axis=2)
    tap_specs = [pl.BlockSpec((None, L * CONV_HALO, cw), functools.partial(lambda i, d: (d, 0, 0), d=d))
                 for d in range(N_DEV)]
    pad_t = lambda a: jnp.pad(a, ((0, 0), (0, CONV_HALO - CONV_WIDTH), (0, 0))).reshape(L * CONV_HALO, cw)
    t_out = _adamw("adamw_taps", pad_t(w_dw), pad_t(m_w_dw), pad_t(v_w_dw), [tap_parts] * N_DEV, tap_specs,
                   L * CONV_HALO)
    finish_reduce(t_out[0])
    upd = {n: [o.reshape(wmv[n][0].shape) for o in outs] for n, outs in upd_big.items()}
    upd["w_dw"] = [o.reshape(L, CONV_HALO, cw)[:, :CONV_WIDTH, :] for o in t_out]

    order = ["ln0_g", "ln0_b", "w_in", "b_in", "lb_logits", "g_norm_w", "w_a", "w_dw", "b_dw", "conv_ln_g",
             "conv_ln_b", "w_b", "b_b", "w_o", "ln1_g", "ln1_b", "w_up", "w_down", "ln2_g", "ln2_b"]
    small_sets = (s_g, s_d, s_m, s_v)
    outs = [loss, dx0.reshape(x.shape)]
    for i in range(4):
        for n in order:
            outs.append(upd[n][i] if n in upd else small_sets[i][n])
    return tuple(outs)
```

```python
import functools

import jax
import jax.numpy as jnp
from jax import lax
from jax.experimental import pallas as pl
from jax.experimental.pallas import tpu as pltpu

F32 = jnp.float32
MXU_DTYPE = jnp.bfloat16
ACT_DTYPE = jnp.bfloat16

LANES = 128
SUB = 8
N_DEV = 8
N_SEC = 8
CONV_WIDTH = 31
CONV_HALO = 32
HG_C = 16
LN_EPS = 1e-5
RMS_EPS = 1e-6
F_MIN = 1e-30
ADAM_LR = 0.001
ADAM_B1 = 0.9
ADAM_B2 = 0.999
ADAM_EPS = 1e-08
ADAM_WD = 0.01
ADAM_STEP = 10
VMEM_LIMIT = 56 * 1024 * 1024
MESH = pl.DeviceIdType.MESH

_NN = (((1,), (0,)), ((), ()))
_NT = (((1,), (1,)), ((), ()))
_TN = (((0,), (0,)), ((), ()))


_ANY = pl.BlockSpec(memory_space=pl.ANY)
_HBM = pl.BlockSpec(memory_space=pltpu.HBM)
_SEM = pl.BlockSpec(memory_space=pltpu.SEMAPHORE)
_EFFECT = pltpu.SideEffectType.DATAFLOW_SIDE_EFFECTING


def _cp(*sem):
    return pltpu.CompilerParams(dimension_semantics=tuple(sem), vmem_limit_bytes=VMEM_LIMIT)


def _pick(n, cands):
    for c in cands:
        if c <= n and n % c == 0:
            return c
    return n


def _silu(x):
    return x * jax.nn.sigmoid(x)


def _dsilu(x):
    s = jax.nn.sigmoid(x)
    return s * (1.0 + x * (1.0 - s))


def _matmul(name, a, b, *, dims, grid, a_spec, b_spec, out_shape, out_spec, acc_shape, nk,
            bias=None, bias_spec=None, add=None, add_spec=None, add_scale=1.0, dep=None):
    has_bias, has_add = bias is not None, add is not None
    kaxis = len(grid) - 1

    def body(*refs):
        a_ref, b_ref = refs[0], refs[1]
        pos = 2
        bias_ref = add_ref = None
        if has_bias:
            bias_ref = refs[pos]
            pos += 1
        if has_add:
            add_ref = refs[pos]
            pos += 1
        if dep is not None:
            pos += 1
        o_ref = refs[pos]
        acc_ref = refs[pos + 1] if nk > 1 else None

        part = lax.dot_general(a_ref[...].astype(MXU_DTYPE), b_ref[...].astype(MXU_DTYPE), dims,
                               preferred_element_type=F32)

        def finish(r):
            if has_bias:
                r = r + bias_ref[...]
            if has_add:
                r = r + add_scale * add_ref[...]
            o_ref[...] = r.astype(o_ref.dtype)

        if nk == 1:
            finish(part)
        else:
            k = pl.program_id(kaxis)

            @pl.when(k == 0)
            def _():
                acc_ref[...] = part

            @pl.when(k > 0)
            def _():
                acc_ref[...] += part

            @pl.when(k == nk - 1)
            def _():
                finish(acc_ref[...])

    ins, specs = [a, b], [a_spec, b_spec]
    if has_bias:
        ins.append(bias)
        specs.append(bias_spec)
    if has_add:
        ins.append(add)
        specs.append(add_spec)
    if dep is not None:
        ins.append(dep)
        specs.append(_ANY)
    sem =("parallel",) * (len(grid) - 1) + ("arbitrary",) if nk > 1 else ("parallel",) * len(grid)
    return pl.pallas_call(
        body, name=name, grid=grid, in_specs=specs, out_specs=out_spec, out_shape=out_shape,
        scratch_shapes=[pltpu.VMEM(acc_shape, F32)] if nk > 1 else [],
        compiler_params=_cp(*sem))(*ins)


def _mm_nn(name, a, b, out_dtype, bias=None):
    M, K = a.shape
    N = b.shape[1]
    tn = _pick(N, (512, 256, 128))
    tk = K if K <= 1024 else _pick(K, (1408, 1024, 512, 256, 128))
    nk = K // tk
    return _matmul(
        name, a, b, dims=_NN, grid=(N // tn, nk),
        a_spec=pl.BlockSpec((M, tk), lambda j, k: (0, k)),
        b_spec=pl.BlockSpec((tk, tn), lambda j, k: (k, j)),
        out_shape=jax.ShapeDtypeStruct((M, N), out_dtype),
        out_spec=pl.BlockSpec((M, tn), lambda j, k: (0, j)),
        acc_shape=(M, tn), nk=nk,
        bias=bias, bias_spec=None if bias is None else pl.BlockSpec((1, tn), lambda j, k: (0, j)))


def _mm_nt(name, a, b, out_dtype, add=None, add_scale=1.0):
    M, K = a.shape
    N = b.shape[0]
    tn = _pick(N, (512, 256, 128))
    tk = K if K <= 1024 else _pick(K, (1408, 1024, 512, 256, 128))
    nk = K // tk
    return _matmul(
        name, a, b, dims=_NT, grid=(N // tn, nk),
        a_spec=pl.BlockSpec((M, tk), lambda j, k: (0, k)),
        b_spec=pl.BlockSpec((tn, tk), lambda j, k: (j, k)),
        out_shape=jax.ShapeDtypeStruct((M, N), out_dtype),
        out_spec=pl.BlockSpec((M, tn), lambda j, k: (0, j)),
        acc_shape=(M, tn), nk=nk,
        add=add, add_spec=None if add is None else pl.BlockSpec((M, tn), lambda j, k: (0, j)),
        add_scale=add_scale)


def _mm_tn(name, a, b, out_dtype):
    K, M = a.shape
    N = b.shape[1]
    tm = _pick(M, (256, 128))
    return _matmul(
        name, a, b, dims=_TN, grid=(M // tm,),
        a_spec=pl.BlockSpec((K, tm), lambda i: (0, i)),
        b_spec=pl.BlockSpec((K, N), lambda i: (0, 0)),
        out_shape=jax.ShapeDtypeStruct((M, N), out_dtype),
        out_spec=pl.BlockSpec((tm, N), lambda i: (i, 0)),
        acc_shape=(tm, N), nk=1)


def _proj_in(x_bf, w_in, b_in, dep=None):
    T, D = x_bf.shape
    tn = _pick(D, (512, 256, 128))
    return _matmul(
        "proj_in", x_bf, w_in, dims=_NN, grid=(N_SEC, D // tn),
        a_spec=pl.BlockSpec((T, D), lambda s, j: (0, 0)),
        b_spec=pl.BlockSpec((None, D, tn), lambda s, j: (s, 0, j)),
        out_shape=jax.ShapeDtypeStruct((N_SEC, T, D), F32),
        out_spec=pl.BlockSpec((None, T, tn), lambda s, j: (s, 0, j)),
        acc_shape=(T, tn), nk=1,
        bias=b_in, bias_spec=pl.BlockSpec((None, 1, tn), lambda s, j: (s, 0, j)), dep=dep)


def _proj_in_dx(dh, w_in, add, add_scale):
    _, T, D = dh.shape
    tn = _pick(D, (512, 256, 128))
    return _matmul(
        "proj_in_dx", dh, w_in, dims=_NT, grid=(D // tn, N_SEC),
        a_spec=pl.BlockSpec((None, T, D), lambda j, s: (s, 0, 0)),
        b_spec=pl.BlockSpec((None, tn, D), lambda j, s: (s, j, 0)),
        out_shape=jax.ShapeDtypeStruct((T, D), F32),
        out_spec=pl.BlockSpec((T, tn), lambda j, s: (0, j)),
        acc_shape=(T, tn), nk=N_SEC,
        add=add, add_spec=pl.BlockSpec((T, tn), lambda j, s: (0, j)), add_scale=add_scale)


def _proj_in_dw(x_bf, dh):
    _, T, D = dh.shape
    tn = _pick(D, (512, 256, 128))

    def body(x_ref, dh_ref, dw_ref, db_ref):
        dhv = dh_ref[...]
        dw_ref[...] = lax.dot_general(x_ref[...].astype(MXU_DTYPE), dhv.astype(MXU_DTYPE), _TN,
                                      preferred_element_type=F32).astype(dw_ref.dtype)
        db_ref[...] = jnp.sum(dhv.astype(F32), axis=0, keepdims=True)

    return pl.pallas_call(
        body, name="proj_in_dw", grid=(N_SEC, D // tn),
        in_specs=[pl.BlockSpec((T, D), lambda s, j: (0, 0)), pl.BlockSpec((None, T, tn), lambda s, j: (s, 0, j))],
        out_specs=[pl.BlockSpec((None, D, tn), lambda s, j: (s, 0, j)),
                   pl.BlockSpec((None, 1, tn), lambda s, j: (s, 0, j))],
        out_shape=[jax.ShapeDtypeStruct((N_SEC, D, D), ACT_DTYPE), jax.ShapeDtypeStruct((N_SEC, 1, D), F32)],
        compiler_params=_cp("parallel", "parallel"))(x_bf, dh)


def _ffn_up(x_bf, w_up_t):
    T, D = x_bf.shape
    F = w_up_t.shape[0] // 2
    tn = _pick(F, (256, 128))
    nb = F // tn
    return _matmul(
        "ffn_up", x_bf, w_up_t, dims=_NT, grid=(2, nb),
        a_spec=pl.BlockSpec((T, D), lambda p, j: (0, 0)),
        b_spec=pl.BlockSpec((tn, D), lambda p, j: (p * nb + j, 0)),
        out_shape=jax.ShapeDtypeStruct((2, T, F), F32),
        out_spec=pl.BlockSpec((None, T, tn), lambda p, j: (p, 0, j)),
        acc_shape=(T, tn), nk=1)


def _ffn_up_dx(dup, w_up_t, add, add_scale):
    _, T, F = dup.shape
    D = w_up_t.shape[1]
    tn = _pick(D, (512, 256, 128))
    tk = _pick(F, (1408, 256, 128))
    nb = F // tk
    return _matmul(
        "ffn_up_dx", dup, w_up_t, dims=_NN, grid=(D // tn, 2 * nb),
        a_spec=pl.BlockSpec((None, T, tk), lambda j, k: (k // nb, 0, k % nb)),
        b_spec=pl.BlockSpec((tk, tn), lambda j, k: (k, j)),
        out_shape=jax.ShapeDtypeStruct((T, D), F32),
        out_spec=pl.BlockSpec((T, tn), lambda j, k: (0, j)),
        acc_shape=(T, tn), nk=2 * nb,
        add=add, add_spec=pl.BlockSpec((T, tn), lambda j, k: (0, j)), add_scale=add_scale)


def _ffn_up_dw(x_bf, dup):
    _, T, F = dup.shape
    D = x_bf.shape[1]
    tm = _pick(F, (1408, 256, 128))
    nb = F // tm
    return _matmul(
        "ffn_up_dw", dup, x_bf, dims=_TN, grid=(2, nb),
        a_spec=pl.BlockSpec((None, T, tm), lambda p, j: (p, 0, j)),
        b_spec=pl.BlockSpec((T, D), lambda p, j: (0, 0)),
        out_shape=jax.ShapeDtypeStruct((2 * F, D), ACT_DTYPE),
        out_spec=pl.BlockSpec((tm, D), lambda p, j: (p * nb + j, 0)),
        acc_shape=(tm, D), nk=1)


def _ln_fwd(name, a, res, alpha, g, b, dep=None):
    T, D = a.shape
    tr = _pick(T, (256, 128, 64, 32, 16))
    has_res = res is not None

    def body(*refs):
        if has_res:
            a_ref, r_ref, g_ref, b_ref = refs[:4]
            y_ref, yb_ref, z_ref = refs[-3:]
            z = alpha * a_ref[...] + r_ref[...]
            z_ref[...] = z
        else:
            a_ref, g_ref, b_ref = refs[:3]
            y_ref, yb_ref = refs[-2:]
            z = a_ref[...]
        mu = jnp.mean(z, axis=-1, keepdims=True)
        zc = z - mu
        var = jnp.mean(zc * zc, axis=-1, keepdims=True)
        y = zc * lax.rsqrt(var + LN_EPS) * g_ref[...] + b_ref[...]
        y_ref[...] = y
        yb_ref[...] = y.astype(ACT_DTYPE)

    row = pl.BlockSpec((tr, D), lambda i: (i, 0))
    vec = pl.BlockSpec((1, D), lambda i: (0, 0))
    ins = [a] + ([res] if has_res else []) + [g.reshape(1, D), b.reshape(1, D)]
    in_specs = [row] + ([row] if has_res else []) + [vec, vec]
    if dep is not None:
        ins.append(dep)
        in_specs.append(_ANY)
    out_shape = [jax.ShapeDtypeStruct((T, D), F32), jax.ShapeDtypeStruct((T, D), ACT_DTYPE)]
    if has_res:
        out_shape.append(jax.ShapeDtypeStruct((T, D), F32))
    return pl.pallas_call(
        body, name=name, grid=(T // tr,), in_specs=in_specs,
        out_specs=[row] * len(out_shape), out_shape=out_shape, compiler_params=_cp("parallel"))(*ins)


def _ln_bwd(name, z, dy, g, dep=None):
    T, D = z.shape
    tr = _pick(T, (256, 128, 64, 32, 16))

    def body(z_ref, dy_ref, g_ref, *rest):
        dz_ref, dzb_ref, dg_ref, db_ref = rest[-4:]

        @pl.when(pl.program_id(0) == 0)
        def _():
            dg_ref[...] = jnp.zeros_like(dg_ref)
            db_ref[...] = jnp.zeros_like(db_ref)

        zv = z_ref[...]
        dy_ = dy_ref[...]
        mu = jnp.mean(zv, axis=-1, keepdims=True)
        zc = zv - mu
        rstd = lax.rsqrt(jnp.mean(zc * zc, axis=-1, keepdims=True) + LN_EPS)
        xhat = zc * rstd
        dxh = dy_ * g_ref[...]
        dz = rstd * (dxh - jnp.mean(dxh, axis=-1, keepdims=True)
                     - xhat * jnp.mean(dxh * xhat, axis=-1, keepdims=True))
        dz_ref[...] = dz
        dzb_ref[...] = dz.astype(ACT_DTYPE)
        dg_ref[...] += jnp.sum(dy_ * xhat, axis=0, keepdims=True)
        db_ref[...] += jnp.sum(dy_, axis=0, keepdims=True)

    row = pl.BlockSpec((tr, D), lambda i: (i, 0))
    vec = pl.BlockSpec((1, D), lambda i: (0, 0))
    ins, in_specs = [z, dy, g.reshape(1, D)], [row, row, vec]
    if dep is not None:
        ins.append(dep)
        in_specs.append(_ANY)
    return pl.pallas_call(
        body, name=name, grid=(T // tr,), in_specs=in_specs, out_specs=[row, row, vec, vec],
        out_shape=[jax.ShapeDtypeStruct((T, D), F32), jax.ShapeDtypeStruct((T, D), ACT_DTYPE),
                   jax.ShapeDtypeStruct((1, D), F32), jax.ShapeDtypeStruct((1, D), F32)],
        compiler_params=_cp("arbitrary"))(*ins)


def _loss_fwd_bwd(y, target):
    T, D = y.shape
    tr = _pick(T, (256, 128, 64, 32, 16))

    def body(y_ref, t_ref, dy_ref, l_ref):
        @pl.when(pl.program_id(0) == 0)
        def _():
            l_ref[...] = jnp.zeros_like(l_ref)

        e = y_ref[...] - t_ref[...]
        dy_ref[...] = e * (1.0 / D)
        row = jnp.sum(e * e, axis=-1, keepdims=True) * (1.0 / D)
        l_ref[...] += 0.5 * jnp.sum(row, axis=0, keepdims=True)

    rowspec = pl.BlockSpec((tr, D), lambda i: (i, 0))
    return pl.pallas_call(
        body, name="loss", grid=(T // tr,), in_specs=[rowspec, rowspec],
        out_specs=[rowspec, pl.BlockSpec((1, LANES), lambda i: (0, 0))],
        out_shape=[jax.ShapeDtypeStruct((T, D), F32), jax.ShapeDtypeStruct((1, LANES), F32)],
        compiler_params=_cp("arbitrary"))(y, target)


def _gate_fwd(y_h, y_c, h):
    T, D = y_h.shape
    tr = _pick(T, (256, 128, 64, 32, 16))

    def body(yh_ref, yc_ref, gh_ref, gc_ref, m_ref):
        m = jax.nn.sigmoid(gh_ref[...]) * yh_ref[...] + jax.nn.sigmoid(gc_ref[...]) * yc_ref[...]
        m_ref[...] = m.astype(ACT_DTYPE)

    row = pl.BlockSpec((tr, D), lambda i: (i, 0))
    return pl.pallas_call(
        body, name="gate_fwd", grid=(T // tr,),
        in_specs=[row, row, pl.BlockSpec((None, tr, D), lambda i: (6, i, 0)),
                  pl.BlockSpec((None, tr, D), lambda i: (7, i, 0))],
        out_specs=row, out_shape=jax.ShapeDtypeStruct((T, D), ACT_DTYPE),
        compiler_params=_cp("parallel"))(y_h, y_c, h, h)


def _gate_bwd(dm, y_h, y_c, h):
    T, D = y_h.shape
    tr = _pick(T, (256, 128, 64, 32, 16))

    def body(dm_ref, yh_ref, yc_ref, gh_ref, gc_ref, dyh_ref, dyc_ref, dbb_ref, dh_ref):
        @pl.when(pl.program_id(0) == 0)
        def _():
            dbb_ref[...] = jnp.zeros_like(dbb_ref)

        dm_ = dm_ref[...]
        sh = jax.nn.sigmoid(gh_ref[...])
        sc = jax.nn.sigmoid(gc_ref[...])
        dyc = dm_ * sc
        dyh_ref[...] = (dm_ * sh).astype(ACT_DTYPE)
        dyc_ref[...] = dyc.astype(ACT_DTYPE)
        dbb_ref[...] += jnp.sum(dyc, axis=0, keepdims=True)
        dh_ref[0] = (dm_ * yh_ref[...] * sh * (1.0 - sh)).astype(ACT_DTYPE)
        dh_ref[1] = (dm_ * yc_ref[...] * sc * (1.0 - sc)).astype(ACT_DTYPE)

    row = pl.BlockSpec((tr, D), lambda i: (i, 0))
    return pl.pallas_call(
        body, name="gate_bwd", grid=(T // tr,),
        in_specs=[row, row, row, pl.BlockSpec((None, tr, D), lambda i: (6, i, 0)),
                  pl.BlockSpec((None, tr, D), lambda i: (7, i, 0))],
        out_specs=[row, row, pl.BlockSpec((1, D), lambda i: (0, 0)),
                   pl.BlockSpec((2, tr, D), lambda i: (3, i, 0))],
        out_shape=[jax.ShapeDtypeStruct((T, D), ACT_DTYPE), jax.ShapeDtypeStruct((T, D), ACT_DTYPE),
                   jax.ShapeDtypeStruct((1, D), F32), jax.ShapeDtypeStruct((N_SEC, T, D), ACT_DTYPE)],
        compiler_params=_cp("arbitrary"))(dm, y_h, y_c, h, h)


def _swiglu_fwd(up):
    _, T, F = up.shape
    tr = _pick(T, (128, 64, 32, 16))

    def body(up_ref, act_ref):
        act_ref[...] = (_silu(up_ref[0]) * up_ref[1]).astype(ACT_DTYPE)

    return pl.pallas_call(
        body, name="swiglu_fwd", grid=(T // tr,),
        in_specs=[pl.BlockSpec((2, tr, F), lambda i: (0, i, 0))],
        out_specs=pl.BlockSpec((tr, F), lambda i: (i, 0)),
        out_shape=jax.ShapeDtypeStruct((T, F), ACT_DTYPE), compiler_params=_cp("parallel"))(up)


def _swiglu_bwd(dact, up):
    _, T, F = up.shape
    tr = _pick(T, (128, 64, 32, 16))

    def body(da_ref, up_ref, dup_ref):
        da = da_ref[...]
        ug = up_ref[0]
        dup_ref[0] = (da * up_ref[1] * _dsilu(ug)).astype(ACT_DTYPE)
        dup_ref[1] = (da * _silu(ug)).astype(ACT_DTYPE)

    blk = pl.BlockSpec((2, tr, F), lambda i: (0, i, 0))
    return pl.pallas_call(
        body, name="swiglu_bwd", grid=(T // tr,),
        in_specs=[pl.BlockSpec((tr, F), lambda i: (i, 0)), blk], out_specs=blk,
        out_shape=jax.ShapeDtypeStruct((2, T, F), ACT_DTYPE), compiler_params=_cp("parallel"))(dact, up)


def _lb_softmax(x):
    L = x.shape[0]
    rows = [x[l:l + 1] for l in range(L)]
    m = rows[0]
    for r in rows[1:]:
        m = jnp.maximum(m, r)
    e = [jnp.exp(r - m) for r in rows]
    s = e[0]
    for r in e[1:]:
        s = s + r
    return [r / s for r in e]


def _lb_fwd(lb_logits):
    L, D = lb_logits.shape

    def body(x_ref, o_ref):
        p = _lb_softmax(x_ref[...])
        run = jnp.zeros_like(p[0])
        for l in range(L):
            if l > 0:
                run = run + p[l]
            o_ref[pl.ds(l, 1), :] = run

    return pl.pallas_call(body, name="lb_fwd", out_shape=jax.ShapeDtypeStruct((L, D), F32))(lb_logits)


def _lb_bwd(lb_logits, dlbs):
    L, D = lb_logits.shape

    def body(x_ref, d_ref, o_ref):
        p = _lb_softmax(x_ref[...])
        d = d_ref[...]
        dp = [jnp.zeros_like(p[0]) for _ in range(L)]
        run = jnp.zeros_like(p[0])
        for j in range(L - 1, 0, -1):
            run = run + d[j:j + 1]
            dp[j] = run
        dot = dp[0] * p[0]
        for j in range(1, L):
            dot = dot + dp[j] * p[j]
        for j in range(L):
            o_ref[pl.ds(j, 1), :] = p[j] * (dp[j] - dot)

    return pl.pallas_call(body, name="lb_bwd", out_shape=jax.ShapeDtypeStruct((L, D), F32))(lb_logits, dlbs)


def _blk_cumsum(x, c, reverse=False):
    n = x.shape[0]
    pos = lax.broadcasted_iota(jnp.int32, x.shape, 0) % c
    s = 1
    while s < c:
        if reverse:
            shifted = pltpu.roll(x, n - s, 0)
            x = x + jnp.where(pos + s < c, shifted, 0.0)
        else:
            shifted = pltpu.roll(x, s, 0)
            x = x + jnp.where(pos >= s, shifted, 0.0)
        s *= 2
    return x


def _hgrn_prologue(q_ref, f_ref, lb_ref):
    lbv = lb_ref[...]
    z = f_ref[...]
    sig = jax.nn.sigmoid(z)
    one_m = 1.0 - lbv
    f = lbv + one_m * sig
    logf = jnp.log(jnp.maximum(f, F_MIN))
    k = one_m * jax.nn.sigmoid(-z)
    q = _silu(q_ref[...])
    return q, k, logf, f, sig, one_m


def _hgrn_fwd(h, lbs_l, gw):
    _, T, D = h.shape
    nh = D // LANES
    c = HG_C
    Tt = _pick(T, (128, 64, 32, 16))
    nb = Tt // c
    ng = c // SUB

    def body(q_ref, f_ref, i_ref, g_ref, lb_ref, gw_ref, o_ref, y_ref, sall_ref,
             st_ref, G_s, q_s, k_s, W_s, R_s, dS_s, o_s):
        @pl.when(pl.program_id(1) == 0)
        def _():
            st_ref[...] = jnp.zeros_like(st_ref)

        q, k, logf, _, _, _ = _hgrn_prologue(q_ref, f_ref, lb_ref)
        G_s[...] = _blk_cumsum(logf, c)
        q_s[...] = q
        k_s[...] = k
        ones = jnp.ones((LANES, LANES), MXU_DTYPE)
        rowid = lax.broadcasted_iota(jnp.int32, (SUB, LANES), 0)
        zero = jnp.zeros((SUB, LANES), F32)
        for bi in range(nb):
            r0 = bi * c
            glast = G_s[pl.ds(r0 + c - 1, 1), :]
            kd = k_s[pl.ds(r0, c), :] * jnp.exp(glast - G_s[pl.ds(r0, c), :])
            dS_s[bi] = lax.dot_general(i_ref[pl.ds(r0, c), :].astype(MXU_DTYPE), kd.astype(MXU_DTYPE), _TN,
                                       preferred_element_type=F32)
        st = st_ref[...]
        for bi in range(nb):
            sall_ref[bi] = st
            st = st * jnp.exp(G_s[pl.ds(bi * c + c - 1, 1), :]) + dS_s[bi]
        st_ref[...] = st
        for bi in range(nb):
            r0 = bi * c
            qd = q_s[pl.ds(r0, c), :] * jnp.exp(G_s[pl.ds(r0, c), :])
            o_s[pl.ds(r0, c), :] = lax.dot_general(qd.astype(MXU_DTYPE), sall_ref[bi].astype(MXU_DTYPE), _NT,
                                                   preferred_element_type=F32)
        for bi in range(nb):
            r0 = bi * c
            w0 = bi * c * c
            Gg = [G_s[pl.ds(r0 + gi * SUB, SUB), :] for gi in range(ng)]
            qg = [q_s[pl.ds(r0 + gi * SUB, SUB), :] for gi in range(ng)]
            for s in range(c):
                gs = G_s[pl.ds(r0 + s, 1), :]
                ks = k_s[pl.ds(r0 + s, 1), :]
                parts = []
                for gi in range(ng):
                    if gi < s // SUB:
                        parts.append(zero)
                        continue
                    e = jnp.exp(jnp.minimum(Gg[gi] - gs, 0.0))
                    if gi == s // SUB:
                        e = jnp.where(rowid >= s - gi * SUB, e, 0.0)
                    parts.append(e * qg[gi] * ks)
                W_s[pl.ds(w0 + s * c, c), :] = jnp.concatenate(parts, axis=0).astype(MXU_DTYPE)
        R_s[...] = jnp.dot(W_s[...], ones, preferred_element_type=F32)
        for bi in range(nb):
            r0 = bi * c
            w0 = bi * c * c
            acc = [o_s[pl.ds(r0 + gi * SUB, SUB), :] for gi in range(ng)]
            for s in range(c):
                vs = i_ref[pl.ds(r0 + s, 1), :]
                for gi in range(s // SUB, ng):
                    acc[gi] = acc[gi] + R_s[pl.ds(w0 + s * c + gi * SUB, SUB), :] * vs
            o_s[pl.ds(r0, c), :] = jnp.concatenate(acc, axis=0)
        o = o_s[...]
        n = o * lax.rsqrt(jnp.mean(o * o, axis=-1, keepdims=True) + RMS_EPS)
        o_ref[...] = o
        y_ref[...] = (n * gw_ref[...] * _silu(g_ref[...])).astype(ACT_DTYPE)

    def sec(s):
        return pl.BlockSpec((None, Tt, LANES), lambda hd, i: (s, i, hd))

    col = pl.BlockSpec((Tt, LANES), lambda hd, i: (i, hd))
    return pl.pallas_call(
        body, name="hgrn_fwd", grid=(nh, T // Tt),
        in_specs=[sec(0), sec(1), sec(2), sec(3), pl.BlockSpec((1, LANES), lambda hd, i: (0, hd)),
                  pl.BlockSpec((1, LANES), lambda hd, i: (0, 0))],
        out_specs=[col, col, pl.BlockSpec((nb, None, LANES, LANES), lambda hd, i: (i, hd, 0, 0))],
        out_shape=[jax.ShapeDtypeStruct((T, D), F32), jax.ShapeDtypeStruct((T, D), ACT_DTYPE),
                   jax.ShapeDtypeStruct((T // c, nh, LANES, LANES), F32)],
        scratch_shapes=[pltpu.VMEM((LANES, LANES), F32), pltpu.VMEM((Tt, LANES), F32),
                        pltpu.VMEM((Tt, LANES), F32), pltpu.VMEM((Tt, LANES), F32),
                        pltpu.VMEM((nb * c * c, LANES), MXU_DTYPE), pltpu.VMEM((nb * c * c, LANES), F32),
                        pltpu.VMEM((nb, LANES, LANES), F32), pltpu.VMEM((Tt, LANES), F32)],
        compiler_params=_cp("parallel", "arbitrary"))(h, h, h, h, lbs_l, gw)


def _hgrn_bwd(h, lbs_l, gw, o_pre, st_all, dy, dh):
    _, T, D = h.shape
    nh = D // LANES
    c = HG_C
    Tt = _pick(T, (128, 64, 32, 16))
    nb = Tt // c
    ng = c // SUB
    nT = T // Tt

    def body(q_ref, f_ref, i_ref, g_ref, lb_ref, gw_ref, o_ref, sall_ref, dy_ref, dh_in_ref,
             dh_ref, dlb_ref, dgw_ref,
             dst_ref, G_s, q_s, k_s, do_s, E_s, WP_s, dq_s, dk_s, dv_s, dG_s,
             R_s, dS_s, dstA_s, dqd_s, dkd_s, dvi_s, da_s):
        del dh_in_ref
        hd, ti = pl.program_id(0), pl.program_id(1)

        @pl.when(ti == 0)
        def _():
            dst_ref[...] = jnp.zeros_like(dst_ref)
            dlb_ref[...] = jnp.zeros_like(dlb_ref)

        @pl.when((ti == 0) & (hd == 0))
        def _():
            dgw_ref[...] = jnp.zeros_like(dgw_ref)

        q, k, logf, f, sig, one_m = _hgrn_prologue(q_ref, f_ref, lb_ref)
        G_s[...] = _blk_cumsum(logf, c)
        q_s[...] = q
        k_s[...] = k

        o = o_ref[...]
        gr = g_ref[...]
        dy_ = dy_ref[...]
        rr = lax.rsqrt(jnp.mean(o * o, axis=-1, keepdims=True) + RMS_EPS)
        n = o * rr
        sg = _silu(gr)
        gwv = gw_ref[...]
        dh_ref[3] = (dy_ * n * gwv * _dsilu(gr)).astype(ACT_DTYPE)
        dgw_ref[...] += jnp.sum(dy_ * n * sg, axis=0, keepdims=True)
        dn = dy_ * gwv * sg
        do_s[...] = rr * (dn - n * jnp.mean(dn * n, axis=-1, keepdims=True))

        ones = jnp.ones((LANES, LANES), MXU_DTYPE)
        rowid = lax.broadcasted_iota(jnp.int32, (SUB, LANES), 0)
        rowid_c = lax.broadcasted_iota(jnp.int32, (c, LANES), 0)
        zero = jnp.zeros((SUB, LANES), F32)
        cc = c * c
        for bi in range(nb):
            r0 = bi * c
            qd = q_s[pl.ds(r0, c), :] * jnp.exp(G_s[pl.ds(r0, c), :])
            dS_s[bi] = lax.dot_general(do_s[pl.ds(r0, c), :].astype(MXU_DTYPE), qd.astype(MXU_DTYPE), _TN,
                                       preferred_element_type=F32)
        dst = dst_ref[...]
        for bi in range(nb - 1, -1, -1):
            dstA_s[bi] = dst
            dst = dst * jnp.exp(G_s[pl.ds(bi * c + c - 1, 1), :]) + dS_s[bi]
        dst_ref[...] = dst
        for bi in range(nb):
            r0 = bi * c
            glast = G_s[pl.ds(r0 + c - 1, 1), :]
            kd = k_s[pl.ds(r0, c), :] * jnp.exp(glast - G_s[pl.ds(r0, c), :])
            st = sall_ref[bi]
            dstb = dstA_s[bi]
            dst_m = dstb.astype(MXU_DTYPE)
            dqd_s[pl.ds(r0, c), :] = lax.dot_general(do_s[pl.ds(r0, c), :].astype(MXU_DTYPE), st.astype(MXU_DTYPE),
                                                     _NN, preferred_element_type=F32)
            dkd_s[pl.ds(r0, c), :] = lax.dot_general(i_ref[pl.ds(r0, c), :].astype(MXU_DTYPE), dst_m, _NN,
                                                     preferred_element_type=F32)
            dvi_s[pl.ds(r0, c), :] = lax.dot_general(kd.astype(MXU_DTYPE), dst_m, _NT,
                                                     preferred_element_type=F32)
            da_s[pl.ds(bi * SUB, 1), :] = jnp.sum(dstb * st, axis=0, keepdims=True)
        for bi in range(nb):
            r0 = bi * c
            e0, w0 = bi * cc, bi * 2 * cc
            Gg = [G_s[pl.ds(r0 + gi * SUB, SUB), :] for gi in range(ng)]
            kg = [k_s[pl.ds(r0 + gi * SUB, SUB), :] for gi in range(ng)]
            vg = [i_ref[pl.ds(r0 + gi * SUB, SUB), :] for gi in range(ng)]
            for t in range(c):
                gt = G_s[pl.ds(r0 + t, 1), :]
                qt = q_s[pl.ds(r0 + t, 1), :]
                dot_ = do_s[pl.ds(r0 + t, 1), :]
                ep, wp, pp = [], [], []
                for gi in range(ng):
                    if gi > t // SUB:
                        ep.append(zero)
                        wp.append(zero)
                        pp.append(zero)
                        continue
                    e = jnp.exp(jnp.minimum(gt - Gg[gi], 0.0))
                    if gi == t // SUB:
                        e = jnp.where(rowid <= t - gi * SUB, e, 0.0)
                    ep.append(e)
                    wp.append(e * kg[gi] * qt)
                    pp.append(vg[gi] * dot_)
                E_s[pl.ds(e0 + t * c, c), :] = jnp.concatenate(ep, axis=0)
                WP_s[pl.ds(w0 + t * c, c), :] = jnp.concatenate(wp, axis=0).astype(MXU_DTYPE)
                WP_s[pl.ds(w0 + cc + t * c, c), :] = jnp.concatenate(pp, axis=0).astype(MXU_DTYPE)
        R_s[...] = jnp.dot(WP_s[...], ones, preferred_element_type=F32)
        for bi in range(nb):
            r0 = bi * c
            e0, w0 = bi * cc, bi * 2 * cc
            kg = [k_s[pl.ds(r0 + gi * SUB, SUB), :] for gi in range(ng)]
            dk_g = [zero] * ng
            dv_g = [zero] * ng
            dq_g = [zero] * ng
            for t in range(c):
                qt = q_s[pl.ds(r0 + t, 1), :]
                dot_ = do_s[pl.ds(r0 + t, 1), :]
                tot = None
                for gi in range(t // SUB + 1):
                    lo = t * c + gi * SUB
                    dae = R_s[pl.ds(w0 + cc + lo, SUB), :] * E_s[pl.ds(e0 + lo, SUB), :]
                    z = dae * kg[gi]
                    tot = z if tot is None else tot + z
                    dk_g[gi] = dk_g[gi] + dae * qt
                    dv_g[gi] = dv_g[gi] + R_s[pl.ds(w0 + lo, SUB), :] * dot_
                gt_ = t // SUB
                dq_g[gt_] = jnp.where(rowid == t - gt_ * SUB, jnp.sum(tot, axis=0, keepdims=True), dq_g[gt_])
            dq_i = jnp.concatenate(dq_g, axis=0)
            dk_i = jnp.concatenate(dk_g, axis=0)
            dv_i = jnp.concatenate(dv_g, axis=0)
            Gb = G_s[pl.ds(r0, c), :]
            qb = q_s[pl.ds(r0, c), :]
            kb = k_s[pl.ds(r0, c), :]
            glast = G_s[pl.ds(r0 + c - 1, 1), :]
            eg = jnp.exp(Gb)
            egl = jnp.exp(glast - Gb)
            dqd = dqd_s[pl.ds(r0, c), :]
            dkd = dkd_s[pl.ds(r0, c), :]
            dq_s[pl.ds(r0, c), :] = dqd * eg + dq_i
            dk_s[pl.ds(r0, c), :] = dkd * egl + dk_i
            dv_s[pl.ds(r0, c), :] = dvi_s[pl.ds(r0, c), :] + dv_i
            dkdkd = dkd * kb * egl
            dG = dqd * qb * eg + qb * dq_i - kb * dk_i - dkdkd
            dglast = jnp.sum(dkdkd, axis=0, keepdims=True) + da_s[pl.ds(bi * SUB, 1), :] * jnp.exp(glast)
            dG_s[pl.ds(r0, c), :] = dG + jnp.where(rowid_c == c - 1, dglast, 0.0)

        dlogf = _blk_cumsum(dG_s[...], c, reverse=True)
        df = jnp.where(f > F_MIN, dlogf / f, 0.0)
        dk = dk_s[...]
        dh_ref[0] = (dq_s[...] * _dsilu(q_ref[...])).astype(ACT_DTYPE)
        dh_ref[1] = ((df - dk) * one_m * sig * (1.0 - sig)).astype(ACT_DTYPE)
        dh_ref[2] = dv_s[...].astype(ACT_DTYPE)
        dlb_ref[...] += jnp.sum((df - dk) * (1.0 - sig), axis=0, keepdims=True)

    def sec(s):
        return pl.BlockSpec((None, Tt, LANES), lambda hd, i: (s, nT - 1 - i, hd))

    col = pl.BlockSpec((Tt, LANES), lambda hd, i: (nT - 1 - i, hd))
    tile = pltpu.VMEM((Tt, LANES), F32)
    return pl.pallas_call(
        body, name="hgrn_bwd", grid=(nh, nT),
        in_specs=[sec(0), sec(1), sec(2), sec(3), pl.BlockSpec((1, LANES), lambda hd, i: (0, hd)),
                  pl.BlockSpec((1, LANES), lambda hd, i: (0, 0)), col,
                  pl.BlockSpec((nb, None, LANES, LANES), lambda hd, i: (nT - 1 - i, hd, 0, 0)), col,
                  pl.BlockSpec(memory_space=pl.ANY)],
        out_specs=[pl.BlockSpec((4, Tt, LANES), lambda hd, i: (0, nT - 1 - i, hd)),
                   pl.BlockSpec((1, LANES), lambda hd, i: (0, hd)),
                   pl.BlockSpec((1, LANES), lambda hd, i: (0, 0))],
        out_shape=[jax.ShapeDtypeStruct(dh.shape, dh.dtype), jax.ShapeDtypeStruct((1, D), F32),
                   jax.ShapeDtypeStruct((1, LANES), F32)],
        scratch_shapes=[pltpu.VMEM((LANES, LANES), F32), tile, tile, tile, tile,
                        pltpu.VMEM((nb * c * c, LANES), F32), pltpu.VMEM((2 * nb * c * c, LANES), MXU_DTYPE),
                        tile, tile, tile, tile,
                        pltpu.VMEM((2 * nb * c * c, LANES), F32), pltpu.VMEM((nb, LANES, LANES), F32),
                        pltpu.VMEM((nb, LANES, LANES), F32), tile, tile, tile, pltpu.VMEM((nb * SUB, LANES), F32)],
        input_output_aliases={9: 0},
        compiler_params=_cp("arbitrary", "arbitrary"))(h, h, h, h, lbs_l, gw, o_pre, st_all, dy, dh)


def _shifted_copies(src, cs, dst, rows):
    for b in range(1, SUB):
        dst[b - 1] = src[pl.ds(b, rows + CONV_HALO - SUB), cs]


def _shifted(src, cs, copies, shift, rows):
    a8, b = divmod(shift, SUB)
    if b == 0:
        return src[pl.ds(shift, rows), cs]
    return copies[b - 1, pl.ds(a8 * SUB, rows), :]


def _conv_fwd(h, w_dw, b_dw, ln_g, ln_b):
    _, T, D = h.shape
    Tt = _pick(T, (256, 128, 64, 32))
    hb = Tt // CONV_HALO
    off = CONV_HALO - (CONV_WIDTH - 1)

    def body(a_ref, b_ref, ap_ref, bp_ref, w_ref, bd_ref, g_ref, be_ref, yc_ref, y_ref, U_s, Ub_s):
        first = pl.program_id(0) == 0
        up = ap_ref[...] * jax.nn.sigmoid(bp_ref[...])
        U_s[pl.ds(0, CONV_HALO), :] = jnp.where(first, 0.0, up)
        U_s[pl.ds(CONV_HALO, Tt), :] = a_ref[...] * jax.nn.sigmoid(b_ref[...])
        for cb in range(D // LANES):
            cs = pl.ds(cb * LANES, LANES)
            _shifted_copies(U_s, cs, Ub_s, Tt)
            acc = jnp.zeros((Tt, LANES), F32)
            for j in range(CONV_WIDTH):
                acc = acc + w_ref[pl.ds(j, 1), cs] * _shifted(U_s, cs, Ub_s, off + j, Tt)
            yc_ref[:, cs] = acc + bd_ref[:, cs]
        yc = yc_ref[...]
        mu = jnp.mean(yc, axis=-1, keepdims=True)
        zc = yc - mu
        var = jnp.mean(zc * zc, axis=-1, keepdims=True)
        ln = zc * lax.rsqrt(var + LN_EPS) * g_ref[...] + be_ref[...]
        y_ref[...] = _silu(ln).astype(ACT_DTYPE)

    def main(s):
        return pl.BlockSpec((None, Tt, D), lambda i: (s, i, 0))

    def prev(s):
        return pl.BlockSpec((None, CONV_HALO, D), lambda i: (s, jnp.maximum(i * hb - 1, 0), 0))

    row = pl.BlockSpec((Tt, D), lambda i: (i, 0))
    vec = pl.BlockSpec((1, D), lambda i: (0, 0))
    return pl.pallas_call(
        body, name="conv_fwd", grid=(T // Tt,),
        in_specs=[main(4), main(5), prev(4), prev(5), pl.BlockSpec((CONV_HALO, D), lambda i: (0, 0)),
                  vec, vec, vec],
        out_specs=[row, row],
        out_shape=[jax.ShapeDtypeStruct((T, D), F32), jax.ShapeDtypeStruct((T, D), ACT_DTYPE)],
        scratch_shapes=[pltpu.VMEM((CONV_HALO + Tt, D), F32),
                        pltpu.VMEM((SUB - 1, Tt + CONV_HALO - SUB, LANES), F32)],
        compiler_params=_cp("parallel"))(h, h, h, h, w_dw, b_dw, ln_g, ln_b)


def _conv_bwd(h, w_dw, ln_g, ln_b, yc, dy, dh):
    _, T, D = h.shape
    Tt = _pick(T, (256, 128, 64, 32))
    hb = Tt // CONV_HALO
    nT = T // Tt
    nhb = T // CONV_HALO
    off = CONV_HALO - (CONV_WIDTH - 1)

    def body(a_ref, b_ref, ap_ref, bp_ref, w_ref, g_ref, be_ref, yc_ref, ycn_ref, dy_ref, dyn_ref, dh_in_ref,
             dh_ref, dw_ref, dbd_ref, dg_ref, dbe_ref, U_s, DY_s, du_s, Ub_s, DYb_s):
        del dh_in_ref
        i = pl.program_id(0)

        @pl.when(i == 0)
        def _():
            dw_ref[...] = jnp.zeros_like(dw_ref)
            dbd_ref[...] = jnp.zeros_like(dbd_ref)
            dg_ref[...] = jnp.zeros_like(dg_ref)
            dbe_ref[...] = jnp.zeros_like(dbe_ref)

        gv = g_ref[...]
        bev = be_ref[...]

        def ln_silu_bwd(ycv, dyv):
            mu = jnp.mean(ycv, axis=-1, keepdims=True)
            zc = ycv - mu
            rstd = lax.rsqrt(jnp.mean(zc * zc, axis=-1, keepdims=True) + LN_EPS)
            xhat = zc * rstd
            dln = dyv * _dsilu(xhat * gv + bev)
            dxh = dln * gv
            dyc = rstd * (dxh - jnp.mean(dxh, axis=-1, keepdims=True)
                          - xhat * jnp.mean(dxh * xhat, axis=-1, keepdims=True))
            return dyc, dln, xhat

        dyc, dln, xhat = ln_silu_bwd(yc_ref[...], dy_ref[...])
        dg_ref[...] += jnp.sum(dln * xhat, axis=0, keepdims=True)
        dbe_ref[...] += jnp.sum(dln, axis=0, keepdims=True)
        dbd_ref[...] += jnp.sum(dyc, axis=0, keepdims=True)
        DY_s[pl.ds(0, Tt), :] = dyc
        dycn, _, _ = ln_silu_bwd(ycn_ref[...], dyn_ref[...])
        DY_s[pl.ds(Tt, CONV_HALO), :] = jnp.where(i == nT - 1, 0.0, dycn)

        sb = jax.nn.sigmoid(b_ref[...])
        av = a_ref[...]
        up = ap_ref[...] * jax.nn.sigmoid(bp_ref[...])
        U_s[pl.ds(0, CONV_HALO), :] = jnp.where(i == 0, 0.0, up)
        U_s[pl.ds(CONV_HALO, Tt), :] = av * sb

        for cb in range(D // LANES):
            cs = pl.ds(cb * LANES, LANES)
            _shifted_copies(U_s, cs, Ub_s, Tt)
            _shifted_copies(DY_s, cs, DYb_s, Tt)
            dyb = DY_s[pl.ds(0, Tt), cs]
            acc = jnp.zeros((Tt, LANES), F32)
            for j in range(CONV_WIDTH):
                acc = acc + w_ref[pl.ds(j, 1), cs] * _shifted(DY_s, cs, DYb_s, CONV_WIDTH - 1 - j, Tt)
                dw_ref[pl.ds(j, 1), cs] += jnp.sum(dyb * _shifted(U_s, cs, Ub_s, off + j, Tt), axis=0, keepdims=True)
            du_s[:, cs] = acc
        du = du_s[...]
        dh_ref[0] = (du * sb).astype(ACT_DTYPE)
        dh_ref[1] = (du * av * sb * (1.0 - sb)).astype(ACT_DTYPE)

    def main(s):
        return pl.BlockSpec((None, Tt, D), lambda i: (s, i, 0))

    def prev(s):
        return pl.BlockSpec((None, CONV_HALO, D), lambda i: (s, jnp.maximum(i * hb - 1, 0), 0))

    row = pl.BlockSpec((Tt, D), lambda i: (i, 0))
    nxt = pl.BlockSpec((CONV_HALO, D), lambda i: (jnp.minimum((i + 1) * hb, nhb - 1), 0))
    vec = pl.BlockSpec((1, D), lambda i: (0, 0))
    wspec = pl.BlockSpec((CONV_HALO, D), lambda i: (0, 0))
    return pl.pallas_call(
        body, name="conv_bwd", grid=(nT,),
        in_specs=[main(4), main(5), prev(4), prev(5), wspec, vec, vec, row, nxt, row, nxt,
                  pl.BlockSpec(memory_space=pl.ANY)],
        out_specs=[pl.BlockSpec((2, Tt, D), lambda i: (2, i, 0)), wspec, vec, vec, vec],
        out_shape=[jax.ShapeDtypeStruct(dh.shape, dh.dtype), jax.ShapeDtypeStruct((CONV_HALO, D), F32),
                   jax.ShapeDtypeStruct((1, D), F32), jax.ShapeDtypeStruct((1, D), F32),
                   jax.ShapeDtypeStruct((1, D), F32)],
        scratch_shapes=[pltpu.VMEM((CONV_HALO + Tt, D), F32), pltpu.VMEM((Tt + CONV_HALO, D), F32),
                        pltpu.VMEM((Tt, D), F32),
                        pltpu.VMEM((SUB - 1, Tt + CONV_HALO - SUB, LANES), F32),
                        pltpu.VMEM((SUB - 1, Tt + CONV_HALO - SUB, LANES), F32)],
        input_output_aliases={11: 0},
        compiler_params=_cp("arbitrary"))(h, h, h, h, w_dw, ln_g, ln_b, yc, yc, dy, dy, dh)


def _adamw(name, w, m, v, parts, part_specs, tr, prefetch=None, nsteps=None, row_map=None, prev=None):
    R, C = w.shape
    bc1 = 1.0 - ADAM_B1 ** ADAM_STEP
    bc2 = 1.0 - ADAM_B2 ** ADAM_STEP
    npart = len(parts)
    npre = 0 if prefetch is None else 1
    nprev = 0 if prev is None else 4

    def body(*refs):
        refs = refs[npre:]
        w_ref, m_ref, v_ref = refs[:3]
        p_refs = refs[3:3 + npart]
        g_ref, d_ref, mo_ref, vo_ref = refs[3 + npart + nprev:]
        g = p_refs[0][...].astype(F32)
        for p in p_refs[1:]:
            g = g + p[...].astype(F32)
        wv = w_ref[...]
        mn = ADAM_B1 * m_ref[...] + (1.0 - ADAM_B1) * g
        vn = ADAM_B2 * v_ref[...] + (1.0 - ADAM_B2) * (g * g)
        m_hat = mn / bc1
        v_hat = vn / bc2
        g_ref[...] = g
        d_ref[...] = -ADAM_LR * (m_hat / (jnp.sqrt(v_hat) + ADAM_EPS) + ADAM_WD * wv)
        mo_ref[...] = mn
        vo_ref[...] = vn

    if row_map is None:
        row_map = (lambda i: (i, 0)) if prefetch is None else (lambda i, s: (i, 0))
    row = pl.BlockSpec((tr, C), row_map)
    out = jax.ShapeDtypeStruct((R, C), F32)
    gs = pltpu.PrefetchScalarGridSpec(
        num_scalar_prefetch=npre, grid=(R // tr if nsteps is None else nsteps,),
        in_specs=[row, row, row] + list(part_specs) + [_ANY] * nprev, out_specs=[row] * 4)
    args = ([prefetch] if npre else []) + [w, m, v] + list(parts) + (list(prev) if nprev else [])
    first_prev = npre + 3 + npart
    return pl.pallas_call(body, name=name, grid_spec=gs, out_shape=[out] * 4,
                          input_output_aliases={first_prev + i: i for i in range(nprev)},
                          compiler_params=_cp("parallel"))(*args)


def _pair_add(p, r1, my_c):
    _, R, C = r1.shape
    tr = max(t for t in range(16, 1025, 16) if R % t == 0)

    def body(c_ref, p_ref, r_ref, q_ref):
        del c_ref
        q_ref[...] = (p_ref[...].astype(F32) + r_ref[...].astype(F32)).astype(q_ref.dtype)

    gs = pltpu.PrefetchScalarGridSpec(
        num_scalar_prefetch=1, grid=(4, R // tr),
        in_specs=[pl.BlockSpec((None, tr, C), lambda j, i, c: (2 * j + c[0], i, 0)),
                  pl.BlockSpec((None, tr, C), lambda j, i, c: (j, i, 0))],
        out_specs=pl.BlockSpec((None, tr, C), lambda j, i, c: (j, i, 0)))
    return pl.pallas_call(body, name="pair_add", grid_spec=gs, out_shape=jax.ShapeDtypeStruct(r1.shape, r1.dtype),
                          compiler_params=_cp("parallel", "parallel"))(my_c, p, r1)


def _place():
    x, y, c = lax.axis_index("x"), lax.axis_index("y"), lax.axis_index("c")
    chips = [(1 - x, y), (x, 1 - y), (1 - x, 1 - y)]
    return x, y, c, chips


def _hbm(a):
    return pltpu.with_memory_space_constraint(a, pltpu.HBM)


def _gather_targets():
    x, y, c, chips = _place()
    return 4 * x + 2 * y + c, [(x, y, 1 - c)] + [(*chip, c) for chip in chips]


def _gather_start(name, shards, zones, after=None):
    n = len(shards)
    lands = [_hbm(z) for z in zones]
    n_in = 2 * n + (0 if after is None else 1)

    def body(*refs):
        srcs, zones = refs[:n], refs[n:2 * n]
        send, recv, token = refs[n_in], refs[n_in + 1], refs[-1]
        mine, targets = _gather_targets()
        for a in range(n):
            for k, to in enumerate(targets):
                pltpu.make_async_remote_copy(
                    src_ref=srcs[a], dst_ref=zones[a].at[mine], send_sem=send.at[4 * a + k],
                    recv_sem=recv.at[4 * a + k], device_id=to, device_id_type=MESH).start()
        token[...] = jnp.zeros_like(token)

    sem = pltpu.SemaphoreType.DMA((4 * n,))
    out_shape = ([sem, sem] + [pltpu.HBM(s.shape, s.dtype) for s in shards]
                 + [pltpu.HBM(z.shape, z.dtype) for z in lands] + [jax.ShapeDtypeStruct((8, LANES), F32)])
    outs = pl.pallas_call(
        body, name=name, out_shape=out_shape, in_specs=[_HBM] * (2 * n) + ([] if after is None else [_ANY]),
        out_specs=[_SEM, _SEM] + [_HBM] * (2 * n) + [pl.BlockSpec(memory_space=pltpu.VMEM)],
        input_output_aliases={i: 2 + i for i in range(2 * n)},
        compiler_params=pltpu.CompilerParams(has_side_effects=_EFFECT))(
            *[_hbm(s) for s in shards], *lands, *([] if after is None else [after]))
    return outs[0], outs[1], list(outs[2:2 + n]), list(outs[2 + n:2 + 2 * n]), outs[-1]


def _gather_wait(name, shards, zones, send, recv, after):
    per = len(shards)

    def body(*refs):
        srcs, lz = refs[:per], refs[per:2 * per]
        send_s, recv_s = refs[2 * per], refs[2 * per + 1]
        mine, targets = _gather_targets()
        for a in range(per):
            for k, to in enumerate(targets):
                cp = pltpu.make_async_remote_copy(
                    src_ref=srcs[a], dst_ref=lz[a].at[mine], send_sem=send_s.at[4 * a + k],
                    recv_sem=recv_s.at[4 * a + k], device_id=to, device_id_type=MESH)
                cp.wait_send()
                cp.wait_recv()

    outs = pl.pallas_call(
        body, name=name, out_shape=[pltpu.HBM(s.shape, s.dtype) for s in shards + zones],
        in_specs=[_HBM] * (2 * per) + [_SEM, _SEM, _ANY], out_specs=[_HBM] * (2 * per),
        input_output_aliases={i: i for i in range(2 * per)},
        compiler_params=pltpu.CompilerParams(has_side_effects=_EFFECT))(*shards, *zones, send, recv, after)
    return outs[:per], outs[per:]


def _gather_finish(zones):
    n = len(zones)

    def body(*refs):
        lz = refs[n:2 * n]
        send_sems, recv_sems = refs[2 * n:]
        x, y, c, chips = _place()

        def fwd(a, j, pc):
            cx, cy = chips[j]
            blk = lz[a].at[4 * cx + 2 * cy + pc]
            return pltpu.make_async_remote_copy(
                src_ref=blk, dst_ref=blk, send_sem=send_sems.at[3 * a + j], recv_sem=recv_sems.at[3 * a + j],
                device_id=(x, y, 1 - c), device_id_type=MESH)

        sends = [fwd(a, j, c) for a in range(n) for j in range(3)]
        for cp in sends:
            cp.start()
        for a in range(n):
            for j in range(3):
                fwd(a, j, 1 - c).wait_recv()
        for cp in sends:
            cp.wait_send()

    return pl.pallas_call(
        body, name="gather_finish", out_shape=[jax.ShapeDtypeStruct(z.shape, z.dtype) for z in zones],
        in_specs=[_ANY] * n, out_specs=[_ANY] * n, input_output_aliases={a: a for a in range(n)},
        scratch_shapes=[pltpu.SemaphoreType.DMA((3 * n,)), pltpu.SemaphoreType.DMA((3 * n,))])(*zones)


def _place_own(shard, dev):
    R, C = shard.shape
    tr = max(t for t in range(16, 1025, 16) if R % t == 0)

    def body(d_ref, s_ref, z_ref):
        del d_ref
        z_ref[...] = s_ref[...]

    gs = pltpu.PrefetchScalarGridSpec(
        num_scalar_prefetch=1, grid=(R // tr,), in_specs=[pl.BlockSpec((tr, C), lambda i, d: (i, 0))],
        out_specs=pl.BlockSpec((None, tr, C), lambda i, d: (d[0], i, 0)))
    return pl.pallas_call(body, name="place_own", grid_spec=gs,
                          out_shape=jax.ShapeDtypeStruct((N_DEV, R, C), shard.dtype),
                          compiler_params=_cp("parallel"))(dev, shard)


def _exchange_sibling(bufs):
    n_arr = len(bufs)

    def body(*refs):
        srcs, outs = refs[:n_arr], refs[n_arr:2 * n_arr]
        send_sems, recv_sems = refs[2 * n_arr:]
        x, y, c, _ = _place()
        copies = []
        for n in range(n_arr):
            for j in range(4):
                copies.append(pltpu.make_async_remote_copy(
                    src_ref=srcs[n].at[2 * j + 1 - c], dst_ref=outs[n].at[j],
                    send_sem=send_sems.at[4 * n + j], recv_sem=recv_sems.at[4 * n + j],
                    device_id=(x, y, 1 - c), device_id_type=MESH))
        for cp in copies:
            cp.start()
        for cp in copies:
            cp.wait()

    return pl.pallas_call(
        body, name="exchange_sibling",
        out_shape=[jax.ShapeDtypeStruct((4,) + b.shape[1:], b.dtype) for b in bufs],
        in_specs=[_ANY] * n_arr, out_specs=[_ANY] * n_arr,
        scratch_shapes=[pltpu.SemaphoreType.DMA((4 * n_arr,)), pltpu.SemaphoreType.DMA((4 * n_arr,))])(*bufs)


def _chip_copies(srcs, zones, send, recv):
    _, _, c, chips = _place()
    return [pltpu.make_async_remote_copy(
        src_ref=srcs[n].at[2 * cx + cy], dst_ref=zones[n].at[k], send_sem=send.at[3 * n + k],
        recv_sem=recv.at[3 * n + k], device_id=(cx, cy, c), device_id_type=MESH)
        for n in range(len(srcs)) for k, (cx, cy) in enumerate(chips)]


def _exchange_chips_start(name, bufs, after=None):
    n = len(bufs)
    n_in = 2 * n + (0 if after is None else 1)
    lands = [_hbm(lax.empty((3,) + b.shape[1:], b.dtype)) for b in bufs]

    def body(*refs):
        srcs, zones = refs[:n], refs[n:2 * n]
        send, recv, token = refs[n_in], refs[n_in + 1], refs[-1]
        for cp in _chip_copies(srcs, zones, send, recv):
            cp.start()
        token[...] = jnp.zeros_like(token)

    sem = pltpu.SemaphoreType.DMA((3 * n,))
    outs = pl.pallas_call(
        body, name=name,
        out_shape=[sem, sem] + [pltpu.HBM(b.shape, b.dtype) for b in bufs]
        + [pltpu.HBM(z.shape, z.dtype) for z in lands] + [jax.ShapeDtypeStruct((8, LANES), F32)],
        in_specs=[_HBM] * (2 * n) + ([] if after is None else [_ANY]),
        out_specs=[_SEM, _SEM] + [_HBM] * (2 * n) + [pl.BlockSpec(memory_space=pltpu.VMEM)],
        input_output_aliases={i: 2 + i for i in range(2 * n)},
        compiler_params=pltpu.CompilerParams(has_side_effects=_EFFECT))(
            *[_hbm(b) for b in bufs], *lands, *([] if after is None else [after]))
    return outs[0], outs[1], outs[2:2 + n], outs[2 + n:2 + 2 * n], outs[-1]


def _exchange_chips_wait(name, bufs, zones, send, recv, after):
    n = len(bufs)

    def body(*refs):
        for cp in _chip_copies(refs[:n], refs[n:2 * n], refs[2 * n], refs[2 * n + 1]):
            cp.wait_send()
            cp.wait_recv()

    outs = pl.pallas_call(
        body, name=name, out_shape=[pltpu.HBM(a.shape, a.dtype) for a in list(bufs) + list(zones)],
        in_specs=[_HBM] * (2 * n) + [_SEM, _SEM, _ANY], out_specs=[_HBM] * (2 * n),
        input_output_aliases={i: i for i in range(2 * n)},
        compiler_params=pltpu.CompilerParams(has_side_effects=_EFFECT))(*bufs, *zones, send, recv, after)
    return outs[n:]


def _all_gather_small(part):
    def body(src, out, send_sems, recv_sems, local_sem):
        x, y, c, _ = _place()
        mine = pltpu.make_async_copy(src, out.at[4 * x + 2 * y + c], local_sem)
        mine.start()
        copies = []
        for r in range(1, N_DEV):
            dx, dy, dc = (r >> 2) & 1, (r >> 1) & 1, r & 1
            peer = (1 - x if dx else x, 1 - y if dy else y, 1 - c if dc else c)
            copies.append(pltpu.make_async_remote_copy(
                src_ref=src, dst_ref=out.at[4 * x + 2 * y + c],
                send_sem=send_sems.at[r - 1], recv_sem=recv_sems.at[r - 1],
                device_id=peer, device_id_type=MESH))
        for cp in copies:
            cp.start()
        for cp in copies:
            cp.wait()
        mine.wait()

    return pl.pallas_call(
        body, name="all_gather_small",
        out_shape=jax.ShapeDtypeStruct((N_DEV,) + part.shape, part.dtype),
        in_specs=[_ANY], out_specs=_ANY,
        scratch_shapes=[pltpu.SemaphoreType.DMA((N_DEV - 1,)), pltpu.SemaphoreType.DMA((N_DEV - 1,)),
                        pltpu.SemaphoreType.DMA])(part)


def _layer_fwd(xin, xin_bf, w_in, rest, P, alpha, dep=None):
    h = _proj_in(xin_bf, w_in, P["b_in"], dep=dep)
    o_pre, y_hg, st_all = _hgrn_fwd(h, P["lbs"], P["g_norm_w"])
    yc_pre, y_cv = _conv_fwd(h, P["w_dw"], P["b_dw"], P["conv_ln_g"], P["conv_ln_b"])
    W = rest(y_cv)
    y_h = _mm_nn("branch_a", y_hg, W["w_a"], F32)
    y_c = _mm_nn("branch_b", y_cv, W["w_b"], F32, bias=P["b_b"])
    merged = _gate_fwd(y_h, y_c, h)
    mix = _mm_nn("mix_out", merged, W["w_o"], F32)
    x1, x1_bf, z1 = _ln_fwd("ln1", xin, mix, alpha, P["ln1_g"], P["ln1_b"])
    up = _ffn_up(x1_bf, W["w_up"])
    act = _swiglu_fwd(up)
    ffn = _mm_nn("ffn_down", act, W["w_down"], F32)
    x2, x2_bf, z2 = _ln_fwd("ln2", x1, ffn, alpha, P["ln2_g"], P["ln2_b"])
    saved = dict(xin_bf=xin_bf, h=h, o_pre=o_pre, y_hg=y_hg, st_all=st_all, yc_pre=yc_pre, y_cv=y_cv,
                 y_h=y_h, y_c=y_c, merged=merged, z1=z1, x1_bf=x1_bf, up=up, act=act, z2=z2)
    return x2, x2_bf, saved


def _layer_bwd(dx2, S, W, P, alpha, dep=None, early=None):
    dz2, dz2_bf, dln2_g, dln2_b = _ln_bwd("ln2_bwd", S["z2"], dx2, P["ln2_g"], dep=dep)
    dact = _mm_nt("ffn_down_dx", dz2_bf, W["w_down"], F32)
    dw_down = _mm_tn("ffn_down_dw", S["act"], dz2_bf, ACT_DTYPE)
    dup = _swiglu_bwd(dact, S["up"])
    dx1 = _ffn_up_dx(dup, W["w_up"], dz2, alpha)
    dw_up = _ffn_up_dw(S["x1_bf"], dup)
    dz1, dz1_bf, dln1_g, dln1_b = _ln_bwd("ln1_bwd", S["z1"], dx1, P["ln1_g"])
    dmerged = _mm_nt("mix_out_dx", dz1_bf, W["w_o"], F32)
    dw_o = _mm_tn("mix_out_dw", S["merged"], dz1_bf, ACT_DTYPE)
    dy_h, dy_c, db_b, dh = _gate_bwd(dmerged, S["y_h"], S["y_c"], S["h"])
    dy_cv = _mm_nt("branch_b_dx", dy_c, W["w_b"], F32)
    dw_b = _mm_tn("branch_b_dw", S["y_cv"], dy_c, ACT_DTYPE)
    dy_hg = _mm_nt("branch_a_dx", dy_h, W["w_a"], F32)
    dw_a = _mm_tn("branch_a_dw", S["y_hg"], dy_h, ACT_DTYPE)
    if early is not None:
        token = early(dict(w_a=dw_a, w_b=dw_b, w_o=dw_o, w_down=dw_down, w_up=dw_up))
        dy_cv = dy_cv + token[0, 0]
    dh, dw_dw, db_dw, dcln_g, dcln_b = _conv_bwd(S["h"], P["w_dw"], P["conv_ln_g"], P["conv_ln_b"],
                                                 S["yc_pre"], dy_cv, dh)
    dh, dlbs, dgw = _hgrn_bwd(S["h"], P["lbs"], P["g_norm_w"], S["o_pre"], S["st_all"], dy_hg, dh)
    dxin = _proj_in_dx(dh, W["w_in"], dz1, alpha)
    dw_in, db_in = _proj_in_dw(S["xin_bf"], dh)
    big = dict(w_in=dw_in, w_a=dw_a, w_b=dw_b, w_o=dw_o, w_down=dw_down, w_up=dw_up)
    small = dict(b_in=db_in, lbs=dlbs, g_norm_w=dgw, w_dw=dw_dw, b_dw=db_dw, conv_ln_g=dcln_g,
                 conv_ln_b=dcln_b, b_b=db_b, ln1_g=dln1_g, ln1_b=dln1_b, ln2_g=dln2_g, ln2_b=dln2_b)
    return dxin, big, small


_SMALL = ("b_in", "lb_logits", "g_norm_w", "b_dw", "conv_ln_g", "conv_ln_b", "b_b", "ln1_g", "ln1_b", "ln2_g",
          "ln2_b")


def _pack_small(per_layer, ln0_g, ln0_b, extra_row, D, L):
    rows = []
    for l in range(L):
        for n in _SMALL:
            a = per_layer[n][l]
            if n == "b_in":
                rows.append(a.reshape(N_SEC, D))
            elif n == "g_norm_w":
                rows.append(jnp.pad(a.reshape(1, -1), ((0, 0), (0, D - a.size))))
            else:
                rows.append(a.reshape(1, D))
    rows += [ln0_g.reshape(1, D), ln0_b.reshape(1, D), extra_row]
    buf = jnp.concatenate(rows, axis=0)
    pad = (-buf.shape[0]) % 8
    return jnp.pad(buf, ((0, pad), (0, 0)))


def _unpack_small(buf, D, L, hv):
    out = {n: [] for n in _SMALL}
    r = 0
    for l in range(L):
        for n in _SMALL:
            if n == "b_in":
                out[n].append(buf[r:r + N_SEC].reshape(N_SEC * D))
                r += N_SEC
            elif n == "g_norm_w":
                out[n].append(buf[r, :hv])
                r += 1
            else:
                out[n].append(buf[r])
                r += 1
    res = {n: jnp.stack(v) for n, v in out.items()}
    res["ln0_g"] = buf[r]
    res["ln0_b"] = buf[r + 1]
    return res, r + 2


def kernel(x, ln0_g, ln0_b, w_in, b_in, lb_logits, g_norm_w, w_a, w_dw, b_dw, conv_ln_g, conv_ln_b, w_b, b_b, w_o, ln1_g, ln1_b, w_up, w_down, ln2_g, ln2_b, loss_target, m_ln0_g, m_ln0_b, m_w_in, m_b_in, m_lb_logits, m_g_norm_w, m_w_a, m_w_dw, m_b_dw, m_conv_ln_g, m_conv_ln_b, m_w_b, m_b_b, m_w_o, m_ln1_g, m_ln1_b, m_w_up, m_w_down, m_ln2_g, m_ln2_b, v_ln0_g, v_ln0_b, v_w_in, v_b_in, v_lb_logits, v_g_norm_w, v_w_a, v_w_dw, v_b_dw, v_conv_ln_g, v_conv_ln_b, v_w_b, v_b_b, v_w_o, v_ln1_g, v_ln1_b, v_w_up, v_w_down, v_ln2_g, v_ln2_b):
    L, D = w_in.shape[0], w_in.shape[1]
    T = x.shape[0] * x.shape[1]
    Dn = w_in.shape[2]
    rs = w_a.shape[1]
    rd = w_down.shape[1]
    cu = w_up.shape[2]
    F = rd * N_DEV
    hv = g_norm_w.shape[1]
    alpha = (2 * L) ** 0.25
    my_x, my_y, my_c = lax.axis_index("x"), lax.axis_index("y"), lax.axis_index("c")
    dev_arr = jnp.reshape(4 * my_x + 2 * my_y + my_c, (1,)).astype(jnp.int32)

    o_a, o_b, o_o, o_d = D, D + rs, D + 2 * rs, D + 3 * rs
    taps = jnp.pad(w_dw, ((0, 0), (0, CONV_HALO - CONV_WIDTH), (0, 0))).reshape(L * CONV_HALO, w_dw.shape[2])
    taps_all = _all_gather_small(taps)
    w_dw_full = taps_all.transpose(1, 0, 2).reshape(L, CONV_HALO, D)

    started, gathered = {}, {}

    def start_gather(key, after):
        l, part = key
        rest = [w_a[l], w_b[l], w_o[l], w_down[l]]
        rows = dict(all=[w_in[l]] + rest, rest=rest)
        if part == "in":
            shards = [w_in[l].astype(ACT_DTYPE)]
        else:
            shards = [jnp.concatenate(rows[part], axis=0).astype(ACT_DTYPE),
                      jnp.swapaxes(w_up[l], 0, 1).astype(ACT_DTYPE)]
        started[key] = _gather_start("gather_start_%d_%s" % key, shards, [_place_own(s, dev_arr) for s in shards],
                                     after)
        return started[key][4]

    def finish_gather(key, after):
        send, recv, thru, zone, _ = started[key]
        _, zn = _gather_wait("gather_wait_%d_%s" % key, thru, zone, send, recv, after)
        gathered[key] = _gather_finish(zn)

    def w_in_of(l):
        return gathered[(l, "in") if l == 0 else (l, "all")][0]

    def rest_of(l):
        ga, gb = gathered[(l, "rest") if l == 0 else (l, "all")]
        base = 0 if l == 0 else D
        return dict(
            w_a=ga[:, base:base + rs, :].reshape(D, D),
            w_b=ga[:, base + rs:base + 2 * rs, :].reshape(D, D),
            w_o=ga[:, base + 2 * rs:base + 3 * rs, :].reshape(D, D),
            w_down=ga[:, base + 3 * rs:base + 3 * rs + rd, :].reshape(F, D),
            w_up=gb.reshape(2 * F, D))

    def weights(l):
        return dict(w_in=w_in_of(l), **rest_of(l))

    lbs = _lb_fwd(lb_logits)

    def params(l):
        return dict(b_in=b_in[l].reshape(N_SEC, 1, D), lbs=lbs[l].reshape(1, D), g_norm_w=g_norm_w[l].reshape(1, hv),
                    w_dw=w_dw_full[l], b_dw=b_dw[l].reshape(1, D), conv_ln_g=conv_ln_g[l].reshape(1, D),
                    conv_ln_b=conv_ln_b[l].reshape(1, D), b_b=b_b[l].reshape(1, D), ln1_g=ln1_g[l], ln1_b=ln1_b[l],
                    ln2_g=ln2_g[l], ln2_b=ln2_b[l])

    x2d = x.reshape(T, D)
    token = start_gather((0, "in"), taps_all)
    token = start_gather((0, "rest"), token)
    if L > 1:
        token = start_gather((1, "all"), token)
    xc, xc_bf = _ln_fwd("ln0", x2d, None, 1.0, ln0_g, ln0_b, dep=token)
    finish_gather((0, "in"), xc_bf)
    saved = []
    for l in range(L):
        if l == 0:
            def rest(after):
                finish_gather((0, "rest"), after)
                return rest_of(0)
            token = None
        else:
            rest = functools.partial(lambda after, l: rest_of(l), l=l)
            token = start_gather((l + 1, "all"), gathered[(l, "all")][0]) if l + 1 < L else None
        xc, xc_bf, s = _layer_fwd(xc, xc_bf, w_in_of(l), rest, params(l), alpha, dep=token)
        saved.append(s)
        if l + 1 < L:
            finish_gather((l + 1, "all"), xc_bf)

    c_arr = jnp.reshape(my_c, (1,)).astype(jnp.int32)
    chip = 2 * my_x + my_y
    dx, loss_row = _loss_fwd_bwd(xc, loss_target.reshape(T, D))
    small = [None] * L
    pending = None
    upd_big = {n: None for n in ("w_in", "w_a", "w_b", "w_o", "w_down", "w_up")}
    wmv = dict(w_in=(w_in, m_w_in, v_w_in), w_a=(w_a, m_w_a, v_w_a), w_b=(w_b, m_w_b, v_w_b),
               w_o=(w_o, m_w_o, v_w_o), w_down=(w_down, m_w_down, v_w_down),
               w_up=tuple(jnp.swapaxes(a, 1, 2) for a in (w_up, m_w_up, v_w_up)))

    def send_buffer(name, g):
        return g if name == "w_in" else g.reshape(N_DEV, wmv[name][0].shape[1], D)

    def update_layer(l, names, q, r2):
        pre = jnp.stack([chip, jnp.int32(l)]).astype(jnp.int32)
        for k, name in enumerate(names):
            w, m, v = wmv[name]
            r, C = w.shape[1], w.shape[2]
            tr = max(t for t in range(16, 513, 16) if r % t == 0)
            nb = r // tr
            specs = [pl.BlockSpec((None, tr, C), lambda i, s: (s[0], i, 0))]
            specs += [pl.BlockSpec((None, tr, C), functools.partial(lambda i, s, j: (j, i, 0), j=j)) for j in range(3)]
            upd_big[name] = _adamw(
                "adamw_" + name, w.reshape(L * r, C), m.reshape(L * r, C), v.reshape(L * r, C),
                [q[k], r2[k], r2[k], r2[k]], specs, tr, prefetch=pre, nsteps=nb,
                row_map=functools.partial(lambda i, s, nb: (s[1] * nb + i, 0), nb=nb), prev=upd_big[name])

    def pair_sums(names, grads):
        sends = [send_buffer(n, grads[n]) for n in names]
        return [_pair_add(p, r, c_arr) for p, r in zip(sends, _exchange_sibling(sends))]

    def start_reduce(tag, names, qs, after=None):
        s_send, s_recv, q_thru, zones, token = _exchange_chips_start("reduce_start_" + tag, qs, after)
        return (tag, names, (s_send, s_recv), list(q_thru), list(zones)), token

    def finish_reduce(l, handle, after):
        tag, names, sems, q, zones = handle
        r2 = _exchange_chips_wait("reduce_wait_" + tag, q, zones, sems[0], sems[1], after)
        update_layer(l, names, q, r2)

    names_all = tuple(upd_big)
    names_early = names_all[1:]
    token = None
    first_early = []
    for l in range(L - 1, -1, -1):
        if l > 0:
            dx, big, small[l] = _layer_bwd(dx, saved[l], weights(l), params(l), alpha, dep=token)
            if pending is not None:
                finish_reduce(l + 1, pending, dx)
            pending, token = start_reduce(str(l), names_all, pair_sums(names_all, big))
        else:
            def early(grads):
                if pending is not None:
                    finish_reduce(1, pending, grads["w_a"])
                handle, tok = start_reduce("0_rest", names_early, pair_sums(names_early, grads))
                first_early.append(handle)
                return tok
            dx, big, small[l] = _layer_bwd(dx, saved[l], weights(l), params(l), alpha, dep=token, early=early)
            q_in = pair_sums(("w_in",), big)
    dx0, _, dln0_g, dln0_b = _ln_bwd("ln0_bwd", x2d, dx, ln0_g)
    dlb_logits = _lb_bwd(lb_logits, jnp.concatenate([small[l]["lbs"] for l in range(L)], axis=0))

    small_l = {n: [small[l][n] for l in range(L)] for n in _SMALL if n != "lb_logits"}
    small_l["lb_logits"] = [dlb_logits[l] for l in range(L)]
    loss_pad = jnp.pad(loss_row, ((0, 0), (0, D - LANES)))
    part = jnp.concatenate([_pack_small(small_l, dln0_g, dln0_b, loss_pad, D, L)]
                           + [small[l]["w_dw"] for l in range(L)], axis=0)
    parts_all = _all_gather_small(part)
    n_small = part.shape[0] - L * CONV_HALO

    last, _ = start_reduce("0_in", ("w_in",), q_in, after=parts_all)


    inputs = dict(b_in=(b_in, m_b_in, v_b_in), lb_logits=(lb_logits, m_lb_logits, v_lb_logits),
                  g_norm_w=(g_norm_w, m_g_norm_w, v_g_norm_w), b_dw=(b_dw, m_b_dw, v_b_dw),
                  conv_ln_g=(conv_ln_g, m_conv_ln_g, v_conv_ln_g), conv_ln_b=(conv_ln_b, m_conv_ln_b, v_conv_ln_b),
                  b_b=(b_b, m_b_b, v_b_b), ln1_g=(ln1_g, m_ln1_g, v_ln1_g), ln1_b=(ln1_b, m_ln1_b, v_ln1_b),
                  ln2_g=(ln2_g, m_ln2_g, v_ln2_g), ln2_b=(ln2_b, m_ln2_b, v_ln2_b))
    zero_row = jnp.zeros((1, D), F32)
    packed = [_pack_small({n: [inputs[n][i][l] for l in range(L)] for n in _SMALL},
                          (ln0_g, m_ln0_g, v_ln0_g)[i], (ln0_b, m_ln0_b, v_ln0_b)[i], zero_row, D, L)
              for i in range(3)]
    small_specs = [pl.BlockSpec((None, n_small, D), functools.partial(lambda i, d: (d, 0, 0), d=d))
                   for d in range(N_DEV)]
    s_out = _adamw("adamw_small", packed[0], packed[1], packed[2], [parts_all] * N_DEV, small_specs, n_small)
    s_g, n_rows = _unpack_small(s_out[0], D, L, hv)
    s_d, _ = _unpack_small(s_out[1], D, L, hv)
    s_m, _ = _unpack_small(s_out[2], D, L, hv)
    s_v, _ = _unpack_small(s_out[3], D, L, hv)
    loss = s_out[0][n_rows, 0]

    cw = w_dw.shape[2]
    dev = 4 * my_x + 2 * my_y + my_c
    tap_parts = lax.dynamic_slice_in_dim(parts_all[:, n_small:, :], dev * cw, cw, axis=2)
    tap_specs = [pl.BlockSpec((None, L * CONV_HALO, cw), functools.partial(lambda i, d: (d, 0, 0), d=d))
                 for d in range(N_DEV)]
    pad_t = lambda a: jnp.pad(a, ((0, 0), (0, CONV_HALO - CONV_WIDTH), (0, 0))).reshape(L * CONV_HALO, cw)
    t_out = _adamw("adamw_taps", pad_t(w_dw), pad_t(m_w_dw), pad_t(v_w_dw), [tap_parts] * N_DEV, tap_specs,
                   L * CONV_HALO)
    finish_reduce(0, first_early[0], t_out[0])
    finish_reduce(0, last, upd_big["w_up"][0])
    upd ={n: [o.reshape(wmv[n][0].shape) for o in outs] for n, outs in upd_big.items()}
    upd["w_up"] = [jnp.swapaxes(o, 1, 2) for o in upd["w_up"]]
    upd["w_dw"] = [o.reshape(L, CONV_HALO, cw)[:, :CONV_WIDTH, :] for o in t_out]

    order = ["ln0_g", "ln0_b", "w_in", "b_in", "lb_logits", "g_norm_w", "w_a", "w_dw", "b_dw", "conv_ln_g",
             "conv_ln_b", "w_b", "b_b", "w_o", "ln1_g", "ln1_b", "w_up", "w_down", "ln2_g", "ln2_b"]
    small_sets = (s_g, s_d, s_m, s_v)
    outs = [loss, dx0.reshape(x.shape)]
    for i in range(4):
        for n in order:
            outs.append(upd[n][i] if n in upd else small_sets[i][n])
    return tuple(outs)
```

```python
import functools

import jax
import jax.numpy as jnp
from jax import lax
from jax.experimental import pallas as pl
from jax.experimental.pallas import tpu as pltpu

F32 = jnp.float32
MXU_DTYPE = jnp.bfloat16
ACT_DTYPE = jnp.bfloat16

LANES = 128
SUB = 8
N_DEV = 8
N_SEC = 8
CONV_WIDTH = 31
CONV_HALO = 32
HG_C = 16
LN_EPS = 1e-5
RMS_EPS = 1e-6
F_MIN = 1e-30
LOG2E = 1.4426950408889634
ADAM_LR = 0.001
ADAM_B1 = 0.9
ADAM_B2 = 0.999
ADAM_EPS = 1e-08
ADAM_WD = 0.01
ADAM_STEP = 10
VMEM_LIMIT = 56 * 1024 * 1024
MESH = pl.DeviceIdType.MESH

_NN = (((1,), (0,)), ((), ()))
_NT = (((1,), (1,)), ((), ()))
_TN = (((0,), (0,)), ((), ()))


_ANY = pl.BlockSpec(memory_space=pl.ANY)
_HBM = pl.BlockSpec(memory_space=pltpu.HBM)
_SEM = pl.BlockSpec(memory_space=pltpu.SEMAPHORE)
_EFFECT = pltpu.SideEffectType.DATAFLOW_SIDE_EFFECTING


def _cp(*sem):
    return pltpu.CompilerParams(dimension_semantics=tuple(sem), vmem_limit_bytes=VMEM_LIMIT)


def _pick(n, cands):
    for c in cands:
        if c <= n and n % c == 0:
            return c
    return n


def _silu(x):
    return x * jax.nn.sigmoid(x)


def _dsilu(x):
    s = jax.nn.sigmoid(x)
    return s * (1.0 + x * (1.0 - s))


def _matmul(name, a, b, *, dims, grid, a_spec, b_spec, out_shape, out_spec, acc_shape, nk,
            bias=None, bias_spec=None, add=None, add_spec=None, add_scale=1.0, dep=None):
    has_bias, has_add = bias is not None, add is not None
    kaxis = len(grid) - 1

    def body(*refs):
        a_ref, b_ref = refs[0], refs[1]
        pos = 2
        bias_ref = add_ref = None
        if has_bias:
            bias_ref = refs[pos]
            pos += 1
        if has_add:
            add_ref = refs[pos]
            pos += 1
        if dep is not None:
            pos += 1
        o_ref = refs[pos]
        acc_ref = refs[pos + 1] if nk > 1 else None

        part = lax.dot_general(a_ref[...].astype(MXU_DTYPE), b_ref[...].astype(MXU_DTYPE), dims,
                               preferred_element_type=F32)

        def finish(r):
            if has_bias:
                r = r + bias_ref[...]
            if has_add:
                r = r + add_scale * add_ref[...]
            o_ref[...] = r.astype(o_ref.dtype)

        if nk == 1:
            finish(part)
        else:
            k = pl.program_id(kaxis)

            @pl.when(k == 0)
            def _():
                acc_ref[...] = part

            @pl.when(k > 0)
            def _():
                acc_ref[...] += part

            @pl.when(k == nk - 1)
            def _():
                finish(acc_ref[...])

    ins, specs = [a, b], [a_spec, b_spec]
    if has_bias:
        ins.append(bias)
        specs.append(bias_spec)
    if has_add:
        ins.append(add)
        specs.append(add_spec)
    if dep is not None:
        ins.append(dep)
        specs.append(_ANY)
    sem =("parallel",) * (len(grid) - 1) + ("arbitrary",) if nk > 1 else ("parallel",) * len(grid)
    return pl.pallas_call(
        body, name=name, grid=grid, in_specs=specs, out_specs=out_spec, out_shape=out_shape,
        scratch_shapes=[pltpu.VMEM(acc_shape, F32)] if nk > 1 else [],
        compiler_params=_cp(*sem))(*ins)


def _mm_rows(name, a, b, dims, n_out, out_dtype, bias=None):
    M, K = a.shape
    tm = _pick(M, (512, 256, 128, 64, 32, 16))
    return _matmul(
        name, a, b, dims=dims, grid=(M // tm,),
        a_spec=pl.BlockSpec((tm, K), lambda i: (i, 0)),
        b_spec=pl.BlockSpec(b.shape, lambda i: (0, 0)),
        out_shape=jax.ShapeDtypeStruct((M, n_out), out_dtype),
        out_spec=pl.BlockSpec((tm, n_out), lambda i: (i, 0)),
        acc_shape=(tm, n_out), nk=1,
        bias=bias, bias_spec=None if bias is None else pl.BlockSpec((1, n_out), lambda i: (0, 0)))


def _mm_nn(name, a, b, out_dtype, bias=None):
    return _mm_rows(name, a, b, _NN, b.shape[1], out_dtype, bias)


def _mm_nt(name, a, b, out_dtype):
    return _mm_rows(name, a, b, _NT, b.shape[0], out_dtype)


def _mm_tn(name, a, b, out_dtype):
    K, M = a.shape
    N = b.shape[1]
    tm = _pick(M, (256, 128))
    return _matmul(
        name, a, b, dims=_TN, grid=(M // tm,),
        a_spec=pl.BlockSpec((K, tm), lambda i: (0, i)),
        b_spec=pl.BlockSpec((K, N), lambda i: (0, 0)),
        out_shape=jax.ShapeDtypeStruct((M, N), out_dtype),
        out_spec=pl.BlockSpec((tm, N), lambda i: (i, 0)),
        acc_shape=(tm, N), nk=1)


def _proj_in(x_bf, w_in, b_in, dep=None):
    T, D = x_bf.shape
    tn = _pick(D, (512, 256, 128))
    return _matmul(
        "proj_in", x_bf, w_in, dims=_NN, grid=(N_SEC, D // tn),
        a_spec=pl.BlockSpec((T, D), lambda s, j: (0, 0)),
        b_spec=pl.BlockSpec((None, D, tn), lambda s, j: (s, 0, j)),
        out_shape=jax.ShapeDtypeStruct((N_SEC, T, D), F32),
        out_spec=pl.BlockSpec((None, T, tn), lambda s, j: (s, 0, j)),
        acc_shape=(T, tn), nk=1,
        bias=b_in, bias_spec=pl.BlockSpec((None, 1, tn), lambda s, j: (s, 0, j)), dep=dep)


def _proj_in_dx(dh, w_in, add, add_scale):
    _, T, D = dh.shape
    tn = _pick(D, (512, 256, 128))
    return _matmul(
        "proj_in_dx", dh, w_in, dims=_NT, grid=(D // tn, N_SEC),
        a_spec=pl.BlockSpec((None, T, D), lambda j, s: (s, 0, 0)),
        b_spec=pl.BlockSpec((None, tn, D), lambda j, s: (s, j, 0)),
        out_shape=jax.ShapeDtypeStruct((T, D), F32),
        out_spec=pl.BlockSpec((T, tn), lambda j, s: (0, j)),
        acc_shape=(T, tn), nk=N_SEC,
        add=add, add_spec=pl.BlockSpec((T, tn), lambda j, s: (0, j)), add_scale=add_scale)


def _proj_in_dw(x_bf, dh):
    _, T, D = dh.shape
    tn = _pick(D, (512, 256, 128))

    def body(x_ref, dh_ref, dw_ref, db_ref):
        dhv = dh_ref[...]
        dw_ref[...] = lax.dot_general(x_ref[...].astype(MXU_DTYPE), dhv.astype(MXU_DTYPE), _TN,
                                      preferred_element_type=F32).astype(dw_ref.dtype)
        db_ref[...] = jnp.sum(dhv.astype(F32), axis=0, keepdims=True)

    return pl.pallas_call(
        body, name="proj_in_dw", grid=(N_SEC, D // tn),
        in_specs=[pl.BlockSpec((T, D), lambda s, j: (0, 0)), pl.BlockSpec((None, T, tn), lambda s, j: (s, 0, j))],
        out_specs=[pl.BlockSpec((None, D, tn), lambda s, j: (s, 0, j)),
                   pl.BlockSpec((None, 1, tn), lambda s, j: (s, 0, j))],
        out_shape=[jax.ShapeDtypeStruct((N_SEC, D, D), ACT_DTYPE), jax.ShapeDtypeStruct((N_SEC, 1, D), F32)],
        compiler_params=_cp("parallel", "parallel"))(x_bf, dh)


def _ffn_up(x_bf, w_up_t):
    T, D = x_bf.shape
    F = w_up_t.shape[0] // 2
    tn = _pick(F, (256, 128))
    nb = F // tn
    return _matmul(
        "ffn_up", x_bf, w_up_t, dims=_NT, grid=(2, nb),
        a_spec=pl.BlockSpec((T, D), lambda p, j: (0, 0)),
        b_spec=pl.BlockSpec((tn, D), lambda p, j: (p * nb + j, 0)),
        out_shape=jax.ShapeDtypeStruct((2, T, F), F32),
        out_spec=pl.BlockSpec((None, T, tn), lambda p, j: (p, 0, j)),
        acc_shape=(T, tn), nk=1)


def _ffn_up_dx(dup, w_up_t, add, add_scale):
    _, T, F = dup.shape
    D = w_up_t.shape[1]
    tn = _pick(D, (512, 256, 128))
    tk = _pick(F, (1408, 256, 128))
    nb = F // tk
    return _matmul(
        "ffn_up_dx", dup, w_up_t, dims=_NN, grid=(D // tn, 2 * nb),
        a_spec=pl.BlockSpec((None, T, tk), lambda j, k: (k // nb, 0, k % nb)),
        b_spec=pl.BlockSpec((tk, tn), lambda j, k: (k, j)),
        out_shape=jax.ShapeDtypeStruct((T, D), F32),
        out_spec=pl.BlockSpec((T, tn), lambda j, k: (0, j)),
        acc_shape=(T, tn), nk=2 * nb,
        add=add, add_spec=pl.BlockSpec((T, tn), lambda j, k: (0, j)), add_scale=add_scale)


def _ffn_up_dw(x_bf, dup):
    _, T, F = dup.shape
    D = x_bf.shape[1]
    tm = _pick(F, (1408, 256, 128))
    nb = F // tm
    return _matmul(
        "ffn_up_dw", dup, x_bf, dims=_TN, grid=(2, nb),
        a_spec=pl.BlockSpec((None, T, tm), lambda p, j: (p, 0, j)),
        b_spec=pl.BlockSpec((T, D), lambda p, j: (0, 0)),
        out_shape=jax.ShapeDtypeStruct((2 * F, D), ACT_DTYPE),
        out_spec=pl.BlockSpec((tm, D), lambda p, j: (p * nb + j, 0)),
        acc_shape=(tm, D), nk=1)


def _ln_fwd(name, a, res, alpha, g, b, dep=None):
    T, D = a.shape
    tr = _pick(T, (256, 128, 64, 32, 16))
    has_res = res is not None

    def body(*refs):
        if has_res:
            a_ref, r_ref, g_ref, b_ref = refs[:4]
            y_ref, yb_ref, z_ref = refs[-3:]
            z = alpha * a_ref[...] + r_ref[...]
            z_ref[...] = z
        else:
            a_ref, g_ref, b_ref = refs[:3]
            y_ref, yb_ref = refs[-2:]
            z = a_ref[...]
        mu = jnp.mean(z, axis=-1, keepdims=True)
        zc = z - mu
        var = jnp.mean(zc * zc, axis=-1, keepdims=True)
        y = zc * lax.rsqrt(var + LN_EPS) * g_ref[...] + b_ref[...]
        y_ref[...] = y
        yb_ref[...] = y.astype(ACT_DTYPE)

    row = pl.BlockSpec((tr, D), lambda i: (i, 0))
    vec = pl.BlockSpec((1, D), lambda i: (0, 0))
    ins = [a] + ([res] if has_res else []) + [g.reshape(1, D), b.reshape(1, D)]
    in_specs = [row] + ([row] if has_res else []) + [vec, vec]
    if dep is not None:
        ins.append(dep)
        in_specs.append(_ANY)
    out_shape = [jax.ShapeDtypeStruct((T, D), F32), jax.ShapeDtypeStruct((T, D), ACT_DTYPE)]
    if has_res:
        out_shape.append(jax.ShapeDtypeStruct((T, D), F32))
    return pl.pallas_call(
        body, name=name, grid=(T // tr,), in_specs=in_specs,
        out_specs=[row] * len(out_shape), out_shape=out_shape, compiler_params=_cp("parallel"))(*ins)


def _ln_bwd(name, z, dy, g, dep=None):
    T, D = z.shape
    tr = _pick(T, (256, 128, 64, 32, 16))

    def body(z_ref, dy_ref, g_ref, *rest):
        dz_ref, dzb_ref, dg_ref, db_ref = rest[-4:]

        @pl.when(pl.program_id(0) == 0)
        def _():
            dg_ref[...] = jnp.zeros_like(dg_ref)
            db_ref[...] = jnp.zeros_like(db_ref)

        zv = z_ref[...]
        dy_ = dy_ref[...]
        mu = jnp.mean(zv, axis=-1, keepdims=True)
        zc = zv - mu
        rstd = lax.rsqrt(jnp.mean(zc * zc, axis=-1, keepdims=True) + LN_EPS)
        xhat = zc * rstd
        dxh = dy_ * g_ref[...]
        dz = rstd * (dxh - jnp.mean(dxh, axis=-1, keepdims=True)
                     - xhat * jnp.mean(dxh * xhat, axis=-1, keepdims=True))
        dz_ref[...] = dz
        dzb_ref[...] = dz.astype(ACT_DTYPE)
        dg_ref[...] += jnp.sum(dy_ * xhat, axis=0, keepdims=True)
        db_ref[...] += jnp.sum(dy_, axis=0, keepdims=True)

    row = pl.BlockSpec((tr, D), lambda i: (i, 0))
    vec = pl.BlockSpec((1, D), lambda i: (0, 0))
    ins, in_specs = [z, dy, g.reshape(1, D)], [row, row, vec]
    if dep is not None:
        ins.append(dep)
        in_specs.append(_ANY)
    return pl.pallas_call(
        body, name=name, grid=(T // tr,), in_specs=in_specs, out_specs=[row, row, vec, vec],
        out_shape=[jax.ShapeDtypeStruct((T, D), F32), jax.ShapeDtypeStruct((T, D), ACT_DTYPE),
                   jax.ShapeDtypeStruct((1, D), F32), jax.ShapeDtypeStruct((1, D), F32)],
        compiler_params=_cp("arbitrary"))(*ins)


def _loss_fwd_bwd(y, target):
    T, D = y.shape
    tr = _pick(T, (256, 128, 64, 32, 16))

    def body(y_ref, t_ref, dy_ref, l_ref):
        @pl.when(pl.program_id(0) == 0)
        def _():
            l_ref[...] = jnp.zeros_like(l_ref)

        e = y_ref[...] - t_ref[...]
        dy_ref[...] = e * (1.0 / D)
        row = jnp.sum(e * e, axis=-1, keepdims=True) * (1.0 / D)
        l_ref[...] += 0.5 * jnp.sum(row, axis=0, keepdims=True)

    rowspec = pl.BlockSpec((tr, D), lambda i: (i, 0))
    return pl.pallas_call(
        body, name="loss", grid=(T // tr,), in_specs=[rowspec, rowspec],
        out_specs=[rowspec, pl.BlockSpec((1, LANES), lambda i: (0, 0))],
        out_shape=[jax.ShapeDtypeStruct((T, D), F32), jax.ShapeDtypeStruct((1, LANES), F32)],
        compiler_params=_cp("arbitrary"))(y, target)


def _gate_fwd(y_h, y_c, h):
    T, D = y_h.shape
    tr = _pick(T, (256, 128, 64, 32, 16))

    def body(yh_ref, yc_ref, gh_ref, gc_ref, m_ref):
        m = jax.nn.sigmoid(gh_ref[...]) * yh_ref[...] + jax.nn.sigmoid(gc_ref[...]) * yc_ref[...]
        m_ref[...] = m.astype(ACT_DTYPE)

    row = pl.BlockSpec((tr, D), lambda i: (i, 0))
    return pl.pallas_call(
        body, name="gate_fwd", grid=(T // tr,),
        in_specs=[row, row, pl.BlockSpec((None, tr, D), lambda i: (6, i, 0)),
                  pl.BlockSpec((None, tr, D), lambda i: (7, i, 0))],
        out_specs=row, out_shape=jax.ShapeDtypeStruct((T, D), ACT_DTYPE),
        compiler_params=_cp("parallel"))(y_h, y_c, h, h)


def _gate_bwd(dm, y_h, y_c, h):
    T, D = y_h.shape
    tr = _pick(T, (256, 128, 64, 32, 16))

    def body(dm_ref, yh_ref, yc_ref, gh_ref, gc_ref, dyh_ref, dyc_ref, dbb_ref, dh_ref):
        @pl.when(pl.program_id(0) == 0)
        def _():
            dbb_ref[...] = jnp.zeros_like(dbb_ref)

        dm_ = dm_ref[...]
        sh = jax.nn.sigmoid(gh_ref[...])
        sc = jax.nn.sigmoid(gc_ref[...])
        dyc = dm_ * sc
        dyh_ref[...] = (dm_ * sh).astype(ACT_DTYPE)
        dyc_ref[...] = dyc.astype(ACT_DTYPE)
        dbb_ref[...] += jnp.sum(dyc, axis=0, keepdims=True)
        dh_ref[0] = (dm_ * yh_ref[...] * sh * (1.0 - sh)).astype(ACT_DTYPE)
        dh_ref[1] = (dm_ * yc_ref[...] * sc * (1.0 - sc)).astype(ACT_DTYPE)

    row = pl.BlockSpec((tr, D), lambda i: (i, 0))
    return pl.pallas_call(
        body, name="gate_bwd", grid=(T // tr,),
        in_specs=[row, row, row, pl.BlockSpec((None, tr, D), lambda i: (6, i, 0)),
                  pl.BlockSpec((None, tr, D), lambda i: (7, i, 0))],
        out_specs=[row, row, pl.BlockSpec((1, D), lambda i: (0, 0)),
                   pl.BlockSpec((2, tr, D), lambda i: (3, i, 0))],
        out_shape=[jax.ShapeDtypeStruct((T, D), ACT_DTYPE), jax.ShapeDtypeStruct((T, D), ACT_DTYPE),
                   jax.ShapeDtypeStruct((1, D), F32), jax.ShapeDtypeStruct((N_SEC, T, D), ACT_DTYPE)],
        compiler_params=_cp("arbitrary"))(dm, y_h, y_c, h, h)


def _swiglu_fwd(up):
    _, T, F = up.shape
    tr = _pick(T, (128, 64, 32, 16))

    def body(up_ref, act_ref):
        act_ref[...] = (_silu(up_ref[0]) * up_ref[1]).astype(ACT_DTYPE)

    return pl.pallas_call(
        body, name="swiglu_fwd", grid=(T // tr,),
        in_specs=[pl.BlockSpec((2, tr, F), lambda i: (0, i, 0))],
        out_specs=pl.BlockSpec((tr, F), lambda i: (i, 0)),
        out_shape=jax.ShapeDtypeStruct((T, F), ACT_DTYPE), compiler_params=_cp("parallel"))(up)


def _swiglu_bwd(dact, up):
    _, T, F = up.shape
    tr = _pick(T, (128, 64, 32, 16))

    def body(da_ref, up_ref, dup_ref):
        da = da_ref[...]
        ug = up_ref[0]
        dup_ref[0] = (da * up_ref[1] * _dsilu(ug)).astype(ACT_DTYPE)
        dup_ref[1] = (da * _silu(ug)).astype(ACT_DTYPE)

    blk = pl.BlockSpec((2, tr, F), lambda i: (0, i, 0))
    return pl.pallas_call(
        body, name="swiglu_bwd", grid=(T // tr,),
        in_specs=[pl.BlockSpec((tr, F), lambda i: (i, 0)), blk], out_specs=blk,
        out_shape=jax.ShapeDtypeStruct((2, T, F), ACT_DTYPE), compiler_params=_cp("parallel"))(dact, up)


def _lb_softmax(x):
    L = x.shape[0]
    rows = [x[l:l + 1] for l in range(L)]
    m = rows[0]
    for r in rows[1:]:
        m = jnp.maximum(m, r)
    e = [jnp.exp(r - m) for r in rows]
    s = e[0]
    for r in e[1:]:
        s = s + r
    return [r / s for r in e]


def _lb_fwd(lb_logits):
    L, D = lb_logits.shape

    def body(x_ref, o_ref):
        p = _lb_softmax(x_ref[...])
        run = jnp.zeros_like(p[0])
        for l in range(L):
            if l > 0:
                run = run + p[l]
            o_ref[pl.ds(l, 1), :] = run

    return pl.pallas_call(body, name="lb_fwd", out_shape=jax.ShapeDtypeStruct((L, D), F32))(lb_logits)


def _lb_bwd(lb_logits, dlbs):
    L, D = lb_logits.shape

    def body(x_ref, d_ref, o_ref):
        p = _lb_softmax(x_ref[...])
        d = d_ref[...]
        dp = [jnp.zeros_like(p[0]) for _ in range(L)]
        run = jnp.zeros_like(p[0])
        for j in range(L - 1, 0, -1):
            run = run + d[j:j + 1]
            dp[j] = run
        dot = dp[0] * p[0]
        for j in range(1, L):
            dot = dot + dp[j] * p[j]
        for j in range(L):
            o_ref[pl.ds(j, 1), :] = p[j] * (dp[j] - dot)

    return pl.pallas_call(body, name="lb_bwd", out_shape=jax.ShapeDtypeStruct((L, D), F32))(lb_logits, dlbs)


def _blk_cumsum(x, c, reverse=False):
    n = x.shape[0]
    pos = lax.broadcasted_iota(jnp.int32, x.shape, 0) % c
    s = 1
    while s < c:
        if reverse:
            shifted = pltpu.roll(x, n - s, 0)
            x = x + jnp.where(pos + s < c, shifted, 0.0)
        else:
            shifted = pltpu.roll(x, s, 0)
            x = x + jnp.where(pos >= s, shifted, 0.0)
        s *= 2
    return x


def _hgrn_prologue(q_ref, f_ref, lb_ref):
    lbv = lb_ref[...]
    z = f_ref[...]
    sig = jax.nn.sigmoid(z)
    one_m = 1.0 - lbv
    f = lbv + one_m * sig
    logf = jnp.log(jnp.maximum(f, F_MIN))
    k = one_m * jax.nn.sigmoid(-z)
    q = _silu(q_ref[...])
    return q, k, logf, f, sig, one_m


def _hgrn_fwd(h, lbs_l, gw):
    _, T, D = h.shape
    nh = D // LANES
    c = HG_C
    Tt = _pick(T, (256, 128, 64, 32, 16))
    nb = Tt // c
    ng = c // SUB

    def body(q_ref, f_ref, i_ref, g_ref, lb_ref, gw_ref, o_ref, y_ref, sall_ref,
             st_ref, G_s, q_s, k_s, W_s, R_s, dS_s, o_s):
        @pl.when(pl.program_id(1) == 0)
        def _():
            st_ref[...] = jnp.zeros_like(st_ref)

        q, k, logf, _, _, _ = _hgrn_prologue(q_ref, f_ref, lb_ref)
        G_s[...] = _blk_cumsum(logf, c) * LOG2E
        q_s[...] = q
        k_s[...] = k
        ones = jnp.ones((LANES, LANES), MXU_DTYPE)
        rowid = lax.broadcasted_iota(jnp.int32, (SUB, LANES), 0)
        zero = jnp.zeros((SUB, LANES), F32)
        for bi in range(nb):
            r0 = bi * c
            glast = G_s[pl.ds(r0 + c - 1, 1), :]
            kd = k_s[pl.ds(r0, c), :] * jnp.exp2(glast - G_s[pl.ds(r0, c), :])
            dS_s[bi] = lax.dot_general(i_ref[pl.ds(r0, c), :].astype(MXU_DTYPE), kd.astype(MXU_DTYPE), _TN,
                                       preferred_element_type=F32)
        st = st_ref[...]
        for bi in range(nb):
            sall_ref[bi] = st
            st = st * jnp.exp2(G_s[pl.ds(bi * c + c - 1, 1), :]) + dS_s[bi]
        st_ref[...] = st
        for bi in range(nb):
            r0 = bi * c
            qd = q_s[pl.ds(r0, c), :] * jnp.exp2(G_s[pl.ds(r0, c), :])
            o_s[pl.ds(r0, c), :] = lax.dot_general(qd.astype(MXU_DTYPE), sall_ref[bi].astype(MXU_DTYPE), _NT,
                                                   preferred_element_type=F32)
        for bi in range(nb):
            r0 = bi * c
            w0 = bi * c * c
            Gg = [G_s[pl.ds(r0 + gi * SUB, SUB), :] for gi in range(ng)]
            qg = [q_s[pl.ds(r0 + gi * SUB, SUB), :] for gi in range(ng)]
            for s in range(c):
                gs = G_s[pl.ds(r0 + s, 1), :]
                ks = k_s[pl.ds(r0 + s, 1), :]
                parts = []
                for gi in range(ng):
                    if gi < s // SUB:
                        parts.append(zero)
                        continue
                    e = jnp.exp2(jnp.minimum(Gg[gi] - gs, 0.0))
                    if gi == s // SUB:
                        e = jnp.where(rowid >= s - gi * SUB, e, 0.0)
                    parts.append(e * qg[gi] * ks)
                W_s[pl.ds(w0 + s * c, c), :] = jnp.concatenate(parts, axis=0).astype(MXU_DTYPE)
        R_s[...] = jnp.dot(W_s[...], ones, preferred_element_type=F32)
        for bi in range(nb):
            r0 = bi * c
            w0 = bi * c * c
            acc = [o_s[pl.ds(r0 + gi * SUB, SUB), :] for gi in range(ng)]
            for s in range(c):
                vs = i_ref[pl.ds(r0 + s, 1), :]
                for gi in range(s // SUB, ng):
                    acc[gi] = acc[gi] + R_s[pl.ds(w0 + s * c + gi * SUB, SUB), :] * vs
            o_s[pl.ds(r0, c), :] = jnp.concatenate(acc, axis=0)
        o = o_s[...]
        n = o * lax.rsqrt(jnp.mean(o * o, axis=-1, keepdims=True) + RMS_EPS)
        o_ref[...] = o
        y_ref[...] = (n * gw_ref[...] * _silu(g_ref[...])).astype(ACT_DTYPE)

    def sec(s):
        return pl.BlockSpec((None, Tt, LANES), lambda hd, i: (s, i, hd))

    col = pl.BlockSpec((Tt, LANES), lambda hd, i: (i, hd))
    return pl.pallas_call(
        body, name="hgrn_fwd", grid=(nh, T // Tt),
        in_specs=[sec(0), sec(1), sec(2), sec(3), pl.BlockSpec((1, LANES), lambda hd, i: (0, hd)),
                  pl.BlockSpec((1, LANES), lambda hd, i: (0, 0))],
        out_specs=[col, col, pl.BlockSpec((nb, None, LANES, LANES), lambda hd, i: (i, hd, 0, 0))],
        out_shape=[jax.ShapeDtypeStruct((T, D), F32), jax.ShapeDtypeStruct((T, D), ACT_DTYPE),
                   jax.ShapeDtypeStruct((T // c, nh, LANES, LANES), F32)],
        scratch_shapes=[pltpu.VMEM((LANES, LANES), F32), pltpu.VMEM((Tt, LANES), F32),
                        pltpu.VMEM((Tt, LANES), F32), pltpu.VMEM((Tt, LANES), F32),
                        pltpu.VMEM((nb * c * c, LANES), MXU_DTYPE), pltpu.VMEM((nb * c * c, LANES), F32),
                        pltpu.VMEM((nb, LANES, LANES), F32), pltpu.VMEM((Tt, LANES), F32)],
        compiler_params=_cp("parallel", "arbitrary"))(h, h, h, h, lbs_l, gw)


def _hgrn_bwd(h, lbs_l, gw, o_pre, st_all, dy, dh):
    _, T, D = h.shape
    nh = D // LANES
    c = HG_C
    Tt = _pick(T, (256, 128, 64, 32, 16))
    nb = Tt // c
    ng = c // SUB
    nT = T // Tt

    def body(q_ref, f_ref, i_ref, g_ref, lb_ref, gw_ref, o_ref, sall_ref, dy_ref, dh_in_ref,
             dh_ref, dlb_ref, dgw_ref,
             dst_ref, G_s, q_s, k_s, do_s, E_s, WP_s, dq_s, dk_s, dv_s, dG_s,
             R_s, dS_s, dstA_s, dqd_s, dkd_s, dvi_s, da_s):
        del dh_in_ref
        hd, ti = pl.program_id(0), pl.program_id(1)

        @pl.when(ti == 0)
        def _():
            dst_ref[...] = jnp.zeros_like(dst_ref)
            dlb_ref[...] = jnp.zeros_like(dlb_ref)

        @pl.when((ti == 0) & (hd == 0))
        def _():
            dgw_ref[...] = jnp.zeros_like(dgw_ref)

        q, k, logf, f, sig, one_m = _hgrn_prologue(q_ref, f_ref, lb_ref)
        G_s[...] = _blk_cumsum(logf, c) * LOG2E
        q_s[...] = q
        k_s[...] = k

        o = o_ref[...]
        gr = g_ref[...]
        dy_ = dy_ref[...]
        rr = lax.rsqrt(jnp.mean(o * o, axis=-1, keepdims=True) + RMS_EPS)
        n = o * rr
        sg = _silu(gr)
        gwv = gw_ref[...]
        dh_ref[3] = (dy_ * n * gwv * _dsilu(gr)).astype(ACT_DTYPE)
        dgw_ref[...] += jnp.sum(dy_ * n * sg, axis=0, keepdims=True)
        dn = dy_ * gwv * sg
        do_s[...] = rr * (dn - n * jnp.mean(dn * n, axis=-1, keepdims=True))

        ones = jnp.ones((LANES, LANES), MXU_DTYPE)
        rowid = lax.broadcasted_iota(jnp.int32, (SUB, LANES), 0)
        rowid_c = lax.broadcasted_iota(jnp.int32, (c, LANES), 0)
        zero = jnp.zeros((SUB, LANES), F32)
        cc = c * c
        for bi in range(nb):
            r0 = bi * c
            qd = q_s[pl.ds(r0, c), :] * jnp.exp2(G_s[pl.ds(r0, c), :])
            dS_s[bi] = lax.dot_general(do_s[pl.ds(r0, c), :].astype(MXU_DTYPE), qd.astype(MXU_DTYPE), _TN,
                                       preferred_element_type=F32)
        dst = dst_ref[...]
        for bi in range(nb - 1, -1, -1):
            dstA_s[bi] = dst
            dst = dst * jnp.exp2(G_s[pl.ds(bi * c + c - 1, 1), :]) + dS_s[bi]
        dst_ref[...] = dst
        for bi in range(nb):
            r0 = bi * c
            glast = G_s[pl.ds(r0 + c - 1, 1), :]
            kd = k_s[pl.ds(r0, c), :] * jnp.exp2(glast - G_s[pl.ds(r0, c), :])
            st = sall_ref[bi]
            dstb = dstA_s[bi]
            dst_m = dstb.astype(MXU_DTYPE)
            dqd_s[pl.ds(r0, c), :] = lax.dot_general(do_s[pl.ds(r0, c), :].astype(MXU_DTYPE), st.astype(MXU_DTYPE),
                                                     _NN, preferred_element_type=F32)
            dkd_s[pl.ds(r0, c), :] = lax.dot_general(i_ref[pl.ds(r0, c), :].astype(MXU_DTYPE), dst_m, _NN,
                                                     preferred_element_type=F32)
            dvi_s[pl.ds(r0, c), :] = lax.dot_general(kd.astype(MXU_DTYPE), dst_m, _NT,
                                                     preferred_element_type=F32)
            da_s[pl.ds(bi * SUB, 1), :] = jnp.sum(dstb * st, axis=0, keepdims=True)
        for bi in range(nb):
            r0 = bi * c
            e0, w0 = bi * cc, bi * 2 * cc
            Gg = [G_s[pl.ds(r0 + gi * SUB, SUB), :] for gi in range(ng)]
            kg = [k_s[pl.ds(r0 + gi * SUB, SUB), :] for gi in range(ng)]
            vg = [i_ref[pl.ds(r0 + gi * SUB, SUB), :] for gi in range(ng)]
            for t in range(c):
                gt = G_s[pl.ds(r0 + t, 1), :]
                qt = q_s[pl.ds(r0 + t, 1), :]
                dot_ = do_s[pl.ds(r0 + t, 1), :]
                ep, wp, pp = [], [], []
                for gi in range(ng):
                    if gi > t // SUB:
                        ep.append(zero)
                        wp.append(zero)
                        pp.append(zero)
                        continue
                    e = jnp.exp2(jnp.minimum(gt - Gg[gi], 0.0))
                    if gi == t // SUB:
                        e = jnp.where(rowid <= t - gi * SUB, e, 0.0)
                    ep.append(e)
                    wp.append(e * kg[gi] * qt)
                    pp.append(vg[gi] * dot_)
                E_s[pl.ds(e0 + t * c, c), :] = jnp.concatenate(ep, axis=0)
                WP_s[pl.ds(w0 + t * c, c), :] = jnp.concatenate(wp, axis=0).astype(MXU_DTYPE)
                WP_s[pl.ds(w0 + cc + t * c, c), :] = jnp.concatenate(pp, axis=0).astype(MXU_DTYPE)
        R_s[...] = jnp.dot(WP_s[...], ones, preferred_element_type=F32)
        for bi in range(nb):
            r0 = bi * c
            e0, w0 = bi * cc, bi * 2 * cc
            kg = [k_s[pl.ds(r0 + gi * SUB, SUB), :] for gi in range(ng)]
            dk_g = [zero] * ng
            dv_g = [zero] * ng
            dq_g = [zero] * ng
            for t in range(c):
                qt = q_s[pl.ds(r0 + t, 1), :]
                dot_ = do_s[pl.ds(r0 + t, 1), :]
                tot = None
                for gi in range(t // SUB + 1):
                    lo = t * c + gi * SUB
                    dae = R_s[pl.ds(w0 + cc + lo, SUB), :] * E_s[pl.ds(e0 + lo, SUB), :]
                    z = dae * kg[gi]
                    tot = z if tot is None else tot + z
                    dk_g[gi] = dk_g[gi] + dae * qt
                    dv_g[gi] = dv_g[gi] + R_s[pl.ds(w0 + lo, SUB), :] * dot_
                gt_ = t // SUB
                dq_g[gt_] = jnp.where(rowid == t - gt_ * SUB, jnp.sum(tot, axis=0, keepdims=True), dq_g[gt_])
            dq_i = jnp.concatenate(dq_g, axis=0)
            dk_i = jnp.concatenate(dk_g, axis=0)
            dv_i = jnp.concatenate(dv_g, axis=0)
            Gb = G_s[pl.ds(r0, c), :]
            qb = q_s[pl.ds(r0, c), :]
            kb = k_s[pl.ds(r0, c), :]
            glast = G_s[pl.ds(r0 + c - 1, 1), :]
            eg = jnp.exp2(Gb)
            egl = jnp.exp2(glast - Gb)
            dqd = dqd_s[pl.ds(r0, c), :]
            dkd = dkd_s[pl.ds(r0, c), :]
            dq_s[pl.ds(r0, c), :] = dqd * eg + dq_i
            dk_s[pl.ds(r0, c), :] = dkd * egl + dk_i
            dv_s[pl.ds(r0, c), :] = dvi_s[pl.ds(r0, c), :] + dv_i
            dkdkd = dkd * kb * egl
            dG = dqd * qb * eg + qb * dq_i - kb * dk_i - dkdkd
            dglast = jnp.sum(dkdkd, axis=0, keepdims=True) + da_s[pl.ds(bi * SUB, 1), :] * jnp.exp2(glast)
            dG_s[pl.ds(r0, c), :] = dG + jnp.where(rowid_c == c - 1, dglast, 0.0)

        dlogf = _blk_cumsum(dG_s[...], c, reverse=True)
        df = jnp.where(f > F_MIN, dlogf / f, 0.0)
        dk = dk_s[...]
        dh_ref[0] = (dq_s[...] * _dsilu(q_ref[...])).astype(ACT_DTYPE)
        dh_ref[1] = ((df - dk) * one_m * sig * (1.0 - sig)).astype(ACT_DTYPE)
        dh_ref[2] = dv_s[...].astype(ACT_DTYPE)
        dlb_ref[...] += jnp.sum((df - dk) * (1.0 - sig), axis=0, keepdims=True)

    def sec(s):
        return pl.BlockSpec((None, Tt, LANES), lambda hd, i: (s, nT - 1 - i, hd))

    col = pl.BlockSpec((Tt, LANES), lambda hd, i: (nT - 1 - i, hd))
    tile = pltpu.VMEM((Tt, LANES), F32)
    return pl.pallas_call(
        body, name="hgrn_bwd", grid=(nh, nT),
        in_specs=[sec(0), sec(1), sec(2), sec(3), pl.BlockSpec((1, LANES), lambda hd, i: (0, hd)),
                  pl.BlockSpec((1, LANES), lambda hd, i: (0, 0)), col,
                  pl.BlockSpec((nb, None, LANES, LANES), lambda hd, i: (nT - 1 - i, hd, 0, 0)), col,
                  pl.BlockSpec(memory_space=pl.ANY)],
        out_specs=[pl.BlockSpec((4, Tt, LANES), lambda hd, i: (0, nT - 1 - i, hd)),
                   pl.BlockSpec((1, LANES), lambda hd, i: (0, hd)),
                   pl.BlockSpec((1, LANES), lambda hd, i: (0, 0))],
        out_shape=[jax.ShapeDtypeStruct(dh.shape, dh.dtype), jax.ShapeDtypeStruct((1, D), F32),
                   jax.ShapeDtypeStruct((1, LANES), F32)],
        scratch_shapes=[pltpu.VMEM((LANES, LANES), F32), tile, tile, tile, tile,
                        pltpu.VMEM((nb * c * c, LANES), F32), pltpu.VMEM((2 * nb * c * c, LANES), MXU_DTYPE),
                        tile, tile, tile, tile,
                        pltpu.VMEM((2 * nb * c * c, LANES), F32), pltpu.VMEM((nb, LANES, LANES), F32),
                        pltpu.VMEM((nb, LANES, LANES), F32), tile, tile, tile, pltpu.VMEM((nb * SUB, LANES), F32)],
        input_output_aliases={9: 0},
        compiler_params=_cp("arbitrary", "arbitrary"))(h, h, h, h, lbs_l, gw, o_pre, st_all, dy, dh)


def _shifted_copies(src, cs, dst, rows):
    for b in range(1, SUB):
        dst[b - 1] = src[pl.ds(b, rows + CONV_HALO - SUB), cs]


def _shifted(src, cs, copies, shift, rows):
    a8, b = divmod(shift, SUB)
    if b == 0:
        return src[pl.ds(shift, rows), cs]
    return copies[b - 1, pl.ds(a8 * SUB, rows), :]


def _conv_fwd(h, w_dw, b_dw, ln_g, ln_b):
    _, T, D = h.shape
    Tt = _pick(T, (256, 128, 64, 32))
    hb = Tt // CONV_HALO
    off = CONV_HALO - (CONV_WIDTH - 1)

    def body(a_ref, b_ref, ap_ref, bp_ref, w_ref, bd_ref, g_ref, be_ref, yc_ref, y_ref, U_s, Ub_s):
        first = pl.program_id(0) == 0
        up = ap_ref[...] * jax.nn.sigmoid(bp_ref[...])
        U_s[pl.ds(0, CONV_HALO), :] = jnp.where(first, 0.0, up)
        U_s[pl.ds(CONV_HALO, Tt), :] = a_ref[...] * jax.nn.sigmoid(b_ref[...])
        for cb in range(D // LANES):
            cs = pl.ds(cb * LANES, LANES)
            _shifted_copies(U_s, cs, Ub_s, Tt)
            acc = jnp.zeros((Tt, LANES), F32)
            for j in range(CONV_WIDTH):
                acc = acc + w_ref[pl.ds(j, 1), cs] * _shifted(U_s, cs, Ub_s, off + j, Tt)
            yc_ref[:, cs] = acc + bd_ref[:, cs]
        yc = yc_ref[...]
        mu = jnp.mean(yc, axis=-1, keepdims=True)
        zc = yc - mu
        var = jnp.mean(zc * zc, axis=-1, keepdims=True)
        ln = zc * lax.rsqrt(var + LN_EPS) * g_ref[...] + be_ref[...]
        y_ref[...] = _silu(ln).astype(ACT_DTYPE)

    def main(s):
        return pl.BlockSpec((None, Tt, D), lambda i: (s, i, 0))

    def prev(s):
        return pl.BlockSpec((None, CONV_HALO, D), lambda i: (s, jnp.maximum(i * hb - 1, 0), 0))

    row = pl.BlockSpec((Tt, D), lambda i: (i, 0))
    vec = pl.BlockSpec((1, D), lambda i: (0, 0))
    return pl.pallas_call(
        body, name="conv_fwd", grid=(T // Tt,),
        in_specs=[main(4), main(5), prev(4), prev(5), pl.BlockSpec((CONV_HALO, D), lambda i: (0, 0)),
                  vec, vec, vec],
        out_specs=[row, row],
        out_shape=[jax.ShapeDtypeStruct((T, D), F32), jax.ShapeDtypeStruct((T, D), ACT_DTYPE)],
        scratch_shapes=[pltpu.VMEM((CONV_HALO + Tt, D), F32),
                        pltpu.VMEM((SUB - 1, Tt + CONV_HALO - SUB, LANES), F32)],
        compiler_params=_cp("parallel"))(h, h, h, h, w_dw, b_dw, ln_g, ln_b)


def _conv_bwd(h, w_dw, ln_g, ln_b, yc, dy, dh):
    _, T, D = h.shape
    Tt = _pick(T, (256, 128, 64, 32))
    hb = Tt // CONV_HALO
    nT = T // Tt
    nhb = T // CONV_HALO
    off = CONV_HALO - (CONV_WIDTH - 1)

    def body(a_ref, b_ref, ap_ref, bp_ref, w_ref, g_ref, be_ref, yc_ref, ycn_ref, dy_ref, dyn_ref, dh_in_ref,
             dh_ref, dw_ref, dbd_ref, dg_ref, dbe_ref, U_s, DY_s, du_s, Ub_s, DYb_s):
        del dh_in_ref
        i = pl.program_id(0)

        @pl.when(i == 0)
        def _():
            dw_ref[...] = jnp.zeros_like(dw_ref)
            dbd_ref[...] = jnp.zeros_like(dbd_ref)
            dg_ref[...] = jnp.zeros_like(dg_ref)
            dbe_ref[...] = jnp.zeros_like(dbe_ref)

        gv = g_ref[...]
        bev = be_ref[...]

        def ln_silu_bwd(ycv, dyv):
            mu = jnp.mean(ycv, axis=-1, keepdims=True)
            zc = ycv - mu
            rstd = lax.rsqrt(jnp.mean(zc * zc, axis=-1, keepdims=True) + LN_EPS)
            xhat = zc * rstd
            dln = dyv * _dsilu(xhat * gv + bev)
            dxh = dln * gv
            dyc = rstd * (dxh - jnp.mean(dxh, axis=-1, keepdims=True)
                          - xhat * jnp.mean(dxh * xhat, axis=-1, keepdims=True))
            return dyc, dln, xhat

        dyc, dln, xhat = ln_silu_bwd(yc_ref[...], dy_ref[...])
        dg_ref[...] += jnp.sum(dln * xhat, axis=0, keepdims=True)
        dbe_ref[...] += jnp.sum(dln, axis=0, keepdims=True)
        dbd_ref[...] += jnp.sum(dyc, axis=0, keepdims=True)
        DY_s[pl.ds(0, Tt), :] = dyc
        dycn, _, _ = ln_silu_bwd(ycn_ref[...], dyn_ref[...])
        DY_s[pl.ds(Tt, CONV_HALO), :] = jnp.where(i == nT - 1, 0.0, dycn)

        sb = jax.nn.sigmoid(b_ref[...])
        av = a_ref[...]
        up = ap_ref[...] * jax.nn.sigmoid(bp_ref[...])
        U_s[pl.ds(0, CONV_HALO), :] = jnp.where(i == 0, 0.0, up)
        U_s[pl.ds(CONV_HALO, Tt), :] = av * sb

        for cb in range(D // LANES):
            cs = pl.ds(cb * LANES, LANES)
            _shifted_copies(U_s, cs, Ub_s, Tt)
            _shifted_copies(DY_s, cs, DYb_s, Tt)
            dyb = DY_s[pl.ds(0, Tt), cs]
            acc = jnp.zeros((Tt, LANES), F32)
            for j in range(CONV_WIDTH):
                acc = acc + w_ref[pl.ds(j, 1), cs] * _shifted(DY_s, cs, DYb_s, CONV_WIDTH - 1 - j, Tt)
                dw_ref[pl.ds(j, 1), cs] += jnp.sum(dyb * _shifted(U_s, cs, Ub_s, off + j, Tt), axis=0, keepdims=True)
            du_s[:, cs] = acc
        du = du_s[...]
        dh_ref[0] = (du * sb).astype(ACT_DTYPE)
        dh_ref[1] = (du * av * sb * (1.0 - sb)).astype(ACT_DTYPE)

    def main(s):
        return pl.BlockSpec((None, Tt, D), lambda i: (s, i, 0))

    def prev(s):
        return pl.BlockSpec((None, CONV_HALO, D), lambda i: (s, jnp.maximum(i * hb - 1, 0), 0))

    row = pl.BlockSpec((Tt, D), lambda i: (i, 0))
    nxt = pl.BlockSpec((CONV_HALO, D), lambda i: (jnp.minimum((i + 1) * hb, nhb - 1), 0))
    vec = pl.BlockSpec((1, D), lambda i: (0, 0))
    wspec = pl.BlockSpec((CONV_HALO, D), lambda i: (0, 0))
    return pl.pallas_call(
        body, name="conv_bwd", grid=(nT,),
        in_specs=[main(4), main(5), prev(4), prev(5), wspec, vec, vec, row, nxt, row, nxt,
                  pl.BlockSpec(memory_space=pl.ANY)],
        out_specs=[pl.BlockSpec((2, Tt, D), lambda i: (2, i, 0)), wspec, vec, vec, vec],
        out_shape=[jax.ShapeDtypeStruct(dh.shape, dh.dtype), jax.ShapeDtypeStruct((CONV_HALO, D), F32),
                   jax.ShapeDtypeStruct((1, D), F32), jax.ShapeDtypeStruct((1, D), F32),
                   jax.ShapeDtypeStruct((1, D), F32)],
        scratch_shapes=[pltpu.VMEM((CONV_HALO + Tt, D), F32), pltpu.VMEM((Tt + CONV_HALO, D), F32),
                        pltpu.VMEM((Tt, D), F32),
                        pltpu.VMEM((SUB - 1, Tt + CONV_HALO - SUB, LANES), F32),
                        pltpu.VMEM((SUB - 1, Tt + CONV_HALO - SUB, LANES), F32)],
        input_output_aliases={11: 0},
        compiler_params=_cp("arbitrary"))(h, h, h, h, w_dw, ln_g, ln_b, yc, yc, dy, dy, dh)


def _adamw(name, w, m, v, parts, part_specs, tr, prefetch=None, nsteps=None, row_map=None, prev=None):
    R, C = w.shape
    bc1 = 1.0 - ADAM_B1 ** ADAM_STEP
    bc2 = 1.0 - ADAM_B2 ** ADAM_STEP
    npart = len(parts)
    npre = 0 if prefetch is None else 1
    nprev = 0 if prev is None else 4

    def body(*refs):
        refs = refs[npre:]
        w_ref, m_ref, v_ref = refs[:3]
        p_refs = refs[3:3 + npart]
        g_ref, d_ref, mo_ref, vo_ref = refs[3 + npart + nprev:]
        g = p_refs[0][...].astype(F32)
        for p in p_refs[1:]:
            g = g + p[...].astype(F32)
        wv = w_ref[...]
        mn = ADAM_B1 * m_ref[...] + (1.0 - ADAM_B1) * g
        vn = ADAM_B2 * v_ref[...] + (1.0 - ADAM_B2) * (g * g)
        m_hat = mn / bc1
        v_hat = vn / bc2
        g_ref[...] = g
        d_ref[...] = -ADAM_LR * (m_hat / (jnp.sqrt(v_hat) + ADAM_EPS) + ADAM_WD * wv)
        mo_ref[...] = mn
        vo_ref[...] = vn

    if row_map is None:
        row_map = (lambda i: (i, 0)) if prefetch is None else (lambda i, s: (i, 0))
    row = pl.BlockSpec((tr, C), row_map)
    out = jax.ShapeDtypeStruct((R, C), F32)
    gs = pltpu.PrefetchScalarGridSpec(
        num_scalar_prefetch=npre, grid=(R // tr if nsteps is None else nsteps,),
        in_specs=[row, row, row] + list(part_specs) + [_ANY] * nprev, out_specs=[row] * 4)
    args = ([prefetch] if npre else []) + [w, m, v] + list(parts) + (list(prev) if nprev else [])
    first_prev = npre + 3 + npart
    return pl.pallas_call(body, name=name, grid_spec=gs, out_shape=[out] * 4,
                          input_output_aliases={first_prev + i: i for i in range(nprev)},
                          compiler_params=_cp("parallel"))(*args)


def _pair_add(p, r1, my_c):
    _, R, C = r1.shape
    tr = max(t for t in range(16, 1025, 16) if R % t == 0)

    def body(c_ref, p_ref, r_ref, q_ref):
        del c_ref
        q_ref[...] = (p_ref[...].astype(F32) + r_ref[...].astype(F32)).astype(q_ref.dtype)

    gs = pltpu.PrefetchScalarGridSpec(
        num_scalar_prefetch=1, grid=(4, R // tr),
        in_specs=[pl.BlockSpec((None, tr, C), lambda j, i, c: (2 * j + c[0], i, 0)),
                  pl.BlockSpec((None, tr, C), lambda j, i, c: (j, i, 0))],
        out_specs=pl.BlockSpec((None, tr, C), lambda j, i, c: (j, i, 0)))
    return pl.pallas_call(body, name="pair_add", grid_spec=gs, out_shape=jax.ShapeDtypeStruct(r1.shape, r1.dtype),
                          compiler_params=_cp("parallel", "parallel"))(my_c, p, r1)


def _place():
    x, y, c = lax.axis_index("x"), lax.axis_index("y"), lax.axis_index("c")
    chips = [(1 - x, y), (x, 1 - y), (1 - x, 1 - y)]
    return x, y, c, chips


def _hbm(a):
    return pltpu.with_memory_space_constraint(a, pltpu.HBM)


def _gather_targets():
    x, y, c, chips = _place()
    return 4 * x + 2 * y + c, [(x, y, 1 - c)] + [(*chip, c) for chip in chips]


def _gather_start(name, shards, zones, after=None):
    n = len(shards)
    lands = [_hbm(z) for z in zones]
    n_in = 2 * n + (0 if after is None else 1)

    def body(*refs):
        srcs, zones = refs[:n], refs[n:2 * n]
        send, recv, token = refs[n_in], refs[n_in + 1], refs[-1]
        mine, targets = _gather_targets()
        for a in range(n):
            for k, to in enumerate(targets):
                pltpu.make_async_remote_copy(
                    src_ref=srcs[a], dst_ref=zones[a].at[mine], send_sem=send.at[4 * a + k],
                    recv_sem=recv.at[4 * a + k], device_id=to, device_id_type=MESH).start()
        token[...] = jnp.zeros_like(token)

    sem = pltpu.SemaphoreType.DMA((4 * n,))
    out_shape = ([sem, sem] + [pltpu.HBM(s.shape, s.dtype) for s in shards]
                 + [pltpu.HBM(z.shape, z.dtype) for z in lands] + [jax.ShapeDtypeStruct((8, LANES), F32)])
    outs = pl.pallas_call(
        body, name=name, out_shape=out_shape, in_specs=[_HBM] * (2 * n) + ([] if after is None else [_ANY]),
        out_specs=[_SEM, _SEM] + [_HBM] * (2 * n) + [pl.BlockSpec(memory_space=pltpu.VMEM)],
        input_output_aliases={i: 2 + i for i in range(2 * n)},
        compiler_params=pltpu.CompilerParams(has_side_effects=_EFFECT))(
            *[_hbm(s) for s in shards], *lands, *([] if after is None else [after]))
    return outs[0], outs[1], list(outs[2:2 + n]), list(outs[2 + n:2 + 2 * n]), outs[-1]


def _gather_wait(name, shards, zones, send, recv, after):
    per = len(shards)

    def body(*refs):
        srcs, lz = refs[:per], refs[per:2 * per]
        send_s, recv_s = refs[2 * per], refs[2 * per + 1]
        mine, targets = _gather_targets()
        for a in range(per):
            for k, to in enumerate(targets):
                cp = pltpu.make_async_remote_copy(
                    src_ref=srcs[a], dst_ref=lz[a].at[mine], send_sem=send_s.at[4 * a + k],
                    recv_sem=recv_s.at[4 * a + k], device_id=to, device_id_type=MESH)
                cp.wait_send()
                cp.wait_recv()

    outs = pl.pallas_call(
        body, name=name, out_shape=[pltpu.HBM(s.shape, s.dtype) for s in shards + zones],
        in_specs=[_HBM] * (2 * per) + [_SEM, _SEM, _ANY], out_specs=[_HBM] * (2 * per),
        input_output_aliases={i: i for i in range(2 * per)},
        compiler_params=pltpu.CompilerParams(has_side_effects=_EFFECT))(*shards, *zones, send, recv, after)
    return outs[:per], outs[per:]


def _gather_finish(zones):
    n = len(zones)

    def body(*refs):
        lz = refs[n:2 * n]
        send_sems, recv_sems = refs[2 * n:]
        x, y, c, chips = _place()

        def fwd(a, j, pc):
            cx, cy = chips[j]
            blk = lz[a].at[4 * cx + 2 * cy + pc]
            return pltpu.make_async_remote_copy(
                src_ref=blk, dst_ref=blk, send_sem=send_sems.at[3 * a + j], recv_sem=recv_sems.at[3 * a + j],
                device_id=(x, y, 1 - c), device_id_type=MESH)

        sends = [fwd(a, j, c) for a in range(n) for j in range(3)]
        for cp in sends:
            cp.start()
        for a in range(n):
            for j in range(3):
                fwd(a, j, 1 - c).wait_recv()
        for cp in sends:
            cp.wait_send()

    return pl.pallas_call(
        body, name="gather_finish", out_shape=[jax.ShapeDtypeStruct(z.shape, z.dtype) for z in zones],
        in_specs=[_ANY] * n, out_specs=[_ANY] * n, input_output_aliases={a: a for a in range(n)},
        scratch_shapes=[pltpu.SemaphoreType.DMA((3 * n,)), pltpu.SemaphoreType.DMA((3 * n,))])(*zones)


def _place_own(shard, dev):
    R, C = shard.shape
    tr = max(t for t in range(16, 1025, 16) if R % t == 0)

    def body(d_ref, s_ref, z_ref):
        del d_ref
        z_ref[...] = s_ref[...]

    gs = pltpu.PrefetchScalarGridSpec(
        num_scalar_prefetch=1, grid=(R // tr,), in_specs=[pl.BlockSpec((tr, C), lambda i, d: (i, 0))],
        out_specs=pl.BlockSpec((None, tr, C), lambda i, d: (d[0], i, 0)))
    return pl.pallas_call(body, name="place_own", grid_spec=gs,
                          out_shape=jax.ShapeDtypeStruct((N_DEV, R, C), shard.dtype),
                          compiler_params=_cp("parallel"))(dev, shard)


def _exchange_sibling(bufs):
    n_arr = len(bufs)

    def body(*refs):
        srcs, outs = refs[:n_arr], refs[n_arr:2 * n_arr]
        send_sems, recv_sems = refs[2 * n_arr:]
        x, y, c, _ = _place()
        copies = []
        for n in range(n_arr):
            for j in range(4):
                copies.append(pltpu.make_async_remote_copy(
                    src_ref=srcs[n].at[2 * j + 1 - c], dst_ref=outs[n].at[j],
                    send_sem=send_sems.at[4 * n + j], recv_sem=recv_sems.at[4 * n + j],
                    device_id=(x, y, 1 - c), device_id_type=MESH))
        for cp in copies:
            cp.start()
        for cp in copies:
            cp.wait()

    return pl.pallas_call(
        body, name="exchange_sibling",
        out_shape=[jax.ShapeDtypeStruct((4,) + b.shape[1:], b.dtype) for b in bufs],
        in_specs=[_ANY] * n_arr, out_specs=[_ANY] * n_arr,
        scratch_shapes=[pltpu.SemaphoreType.DMA((4 * n_arr,)), pltpu.SemaphoreType.DMA((4 * n_arr,))])(*bufs)


def _chip_copies(srcs, zones, send, recv):
    _, _, c, chips = _place()
    return [pltpu.make_async_remote_copy(
        src_ref=srcs[n].at[2 * cx + cy], dst_ref=zones[n].at[k], send_sem=send.at[3 * n + k],
        recv_sem=recv.at[3 * n + k], device_id=(cx, cy, c), device_id_type=MESH)
        for n in range(len(srcs)) for k, (cx, cy) in enumerate(chips)]


def _exchange_chips_start(name, bufs, after=None):
    n = len(bufs)
    n_in = 2 * n + (0 if after is None else 1)
    lands = [_hbm(lax.empty((3,) + b.shape[1:], b.dtype)) for b in bufs]

    def body(*refs):
        srcs, zones = refs[:n], refs[n:2 * n]
        send, recv, token = refs[n_in], refs[n_in + 1], refs[-1]
        for cp in _chip_copies(srcs, zones, send, recv):
            cp.start()
        token[...] = jnp.zeros_like(token)

    sem = pltpu.SemaphoreType.DMA((3 * n,))
    outs = pl.pallas_call(
        body, name=name,
        out_shape=[sem, sem] + [pltpu.HBM(b.shape, b.dtype) for b in bufs]
        + [pltpu.HBM(z.shape, z.dtype) for z in lands] + [jax.ShapeDtypeStruct((8, LANES), F32)],
        in_specs=[_HBM] * (2 * n) + ([] if after is None else [_ANY]),
        out_specs=[_SEM, _SEM] + [_HBM] * (2 * n) + [pl.BlockSpec(memory_space=pltpu.VMEM)],
        input_output_aliases={i: 2 + i for i in range(2 * n)},
        compiler_params=pltpu.CompilerParams(has_side_effects=_EFFECT))(
            *[_hbm(b) for b in bufs], *lands, *([] if after is None else [after]))
    return outs[0], outs[1], outs[2:2 + n], outs[2 + n:2 + 2 * n], outs[-1]


def _exchange_chips_wait(name, bufs, zones, send, recv, after):
    n = len(bufs)

    def body(*refs):
        for cp in _chip_copies(refs[:n], refs[n:2 * n], refs[2 * n], refs[2 * n + 1]):
            cp.wait_send()
            cp.wait_recv()

    outs = pl.pallas_call(
        body, name=name, out_shape=[pltpu.HBM(a.shape, a.dtype) for a in list(bufs) + list(zones)],
        in_specs=[_HBM] * (2 * n) + [_SEM, _SEM, _ANY], out_specs=[_HBM] * (2 * n),
        input_output_aliases={i: i for i in range(2 * n)},
        compiler_params=pltpu.CompilerParams(has_side_effects=_EFFECT))(*bufs, *zones, send, recv, after)
    return outs[n:]


def _all_gather_small(part):
    def body(src, out, send_sems, recv_sems, local_sem):
        x, y, c, _ = _place()
        mine = pltpu.make_async_copy(src, out.at[4 * x + 2 * y + c], local_sem)
        mine.start()
        copies = []
        for r in range(1, N_DEV):
            dx, dy, dc = (r >> 2) & 1, (r >> 1) & 1, r & 1
            peer = (1 - x if dx else x, 1 - y if dy else y, 1 - c if dc else c)
            copies.append(pltpu.make_async_remote_copy(
                src_ref=src, dst_ref=out.at[4 * x + 2 * y + c],
                send_sem=send_sems.at[r - 1], recv_sem=recv_sems.at[r - 1],
                device_id=peer, device_id_type=MESH))
        for cp in copies:
            cp.start()
        for cp in copies:
            cp.wait()
        mine.wait()

    return pl.pallas_call(
        body, name="all_gather_small",
        out_shape=jax.ShapeDtypeStruct((N_DEV,) + part.shape, part.dtype),
        in_specs=[_ANY], out_specs=_ANY,
        scratch_shapes=[pltpu.SemaphoreType.DMA((N_DEV - 1,)), pltpu.SemaphoreType.DMA((N_DEV - 1,)),
                        pltpu.SemaphoreType.DMA])(part)


def _layer_fwd(xin, xin_bf, w_in, rest, P, alpha, dep=None):
    h = _proj_in(xin_bf, w_in, P["b_in"], dep=dep)
    o_pre, y_hg, st_all = _hgrn_fwd(h, P["lbs"], P["g_norm_w"])
    yc_pre, y_cv = _conv_fwd(h, P["w_dw"], P["b_dw"], P["conv_ln_g"], P["conv_ln_b"])
    W = rest(y_cv)
    y_h = _mm_nn("branch_a", y_hg, W["w_a"], F32)
    y_c = _mm_nn("branch_b", y_cv, W["w_b"], F32, bias=P["b_b"])
    merged = _gate_fwd(y_h, y_c, h)
    mix = _mm_nn("mix_out", merged, W["w_o"], F32)
    x1, x1_bf, z1 = _ln_fwd("ln1", xin, mix, alpha, P["ln1_g"], P["ln1_b"])
    up = _ffn_up(x1_bf, W["w_up"])
    act = _swiglu_fwd(up)
    ffn = _mm_nn("ffn_down", act, W["w_down"], F32)
    x2, x2_bf, z2 = _ln_fwd("ln2", x1, ffn, alpha, P["ln2_g"], P["ln2_b"])
    saved = dict(xin_bf=xin_bf, h=h, o_pre=o_pre, y_hg=y_hg, st_all=st_all, yc_pre=yc_pre, y_cv=y_cv,
                 y_h=y_h, y_c=y_c, merged=merged, z1=z1, x1_bf=x1_bf, up=up, act=act, z2=z2)
    return x2, x2_bf, saved


def _layer_bwd(dx2, S, W, P, alpha, dep=None, early=None):
    dz2, dz2_bf, dln2_g, dln2_b = _ln_bwd("ln2_bwd", S["z2"], dx2, P["ln2_g"], dep=dep)
    dact = _mm_nt("ffn_down_dx", dz2_bf, W["w_down"], F32)
    dw_down = _mm_tn("ffn_down_dw", S["act"], dz2_bf, ACT_DTYPE)
    dup = _swiglu_bwd(dact, S["up"])
    dx1 = _ffn_up_dx(dup, W["w_up"], dz2, alpha)
    dw_up = _ffn_up_dw(S["x1_bf"], dup)
    dz1, dz1_bf, dln1_g, dln1_b = _ln_bwd("ln1_bwd", S["z1"], dx1, P["ln1_g"])
    dmerged = _mm_nt("mix_out_dx", dz1_bf, W["w_o"], F32)
    dw_o = _mm_tn("mix_out_dw", S["merged"], dz1_bf, ACT_DTYPE)
    dy_h, dy_c, db_b, dh = _gate_bwd(dmerged, S["y_h"], S["y_c"], S["h"])
    dy_cv = _mm_nt("branch_b_dx", dy_c, W["w_b"], F32)
    dw_b = _mm_tn("branch_b_dw", S["y_cv"], dy_c, ACT_DTYPE)
    dy_hg = _mm_nt("branch_a_dx", dy_h, W["w_a"], F32)
    dw_a = _mm_tn("branch_a_dw", S["y_hg"], dy_h, ACT_DTYPE)
    if early is not None:
        token = early(dict(w_a=dw_a, w_b=dw_b, w_o=dw_o, w_down=dw_down, w_up=dw_up))
        dy_cv = dy_cv + token[0, 0]
    dh, dw_dw, db_dw, dcln_g, dcln_b = _conv_bwd(S["h"], P["w_dw"], P["conv_ln_g"], P["conv_ln_b"],
                                                 S["yc_pre"], dy_cv, dh)
    dh, dlbs, dgw = _hgrn_bwd(S["h"], P["lbs"], P["g_norm_w"], S["o_pre"], S["st_all"], dy_hg, dh)
    dxin = _proj_in_dx(dh, W["w_in"], dz1, alpha)
    dw_in, db_in = _proj_in_dw(S["xin_bf"], dh)
    big = dict(w_in=dw_in, w_a=dw_a, w_b=dw_b, w_o=dw_o, w_down=dw_down, w_up=dw_up)
    small = dict(b_in=db_in, lbs=dlbs, g_norm_w=dgw, w_dw=dw_dw, b_dw=db_dw, conv_ln_g=dcln_g,
                 conv_ln_b=dcln_b, b_b=db_b, ln1_g=dln1_g, ln1_b=dln1_b, ln2_g=dln2_g, ln2_b=dln2_b)
    return dxin, big, small


_SMALL = ("b_in", "lb_logits", "g_norm_w", "b_dw", "conv_ln_g", "conv_ln_b", "b_b", "ln1_g", "ln1_b", "ln2_g",
          "ln2_b")


def _pack_small(per_layer, ln0_g, ln0_b, extra_row, D, L):
    rows = []
    for l in range(L):
        for n in _SMALL:
            a = per_layer[n][l]
            if n == "b_in":
                rows.append(a.reshape(N_SEC, D))
            elif n == "g_norm_w":
                rows.append(jnp.pad(a.reshape(1, -1), ((0, 0), (0, D - a.size))))
            else:
                rows.append(a.reshape(1, D))
    rows += [ln0_g.reshape(1, D), ln0_b.reshape(1, D), extra_row]
    buf = jnp.concatenate(rows, axis=0)
    pad = (-buf.shape[0]) % 8
    return jnp.pad(buf, ((0, pad), (0, 0)))


def _unpack_small(buf, D, L, hv):
    out = {n: [] for n in _SMALL}
    r = 0
    for l in range(L):
        for n in _SMALL:
            if n == "b_in":
                out[n].append(buf[r:r + N_SEC].reshape(N_SEC * D))
                r += N_SEC
            elif n == "g_norm_w":
                out[n].append(buf[r, :hv])
                r += 1
            else:
                out[n].append(buf[r])
                r += 1
    res = {n: jnp.stack(v) for n, v in out.items()}
    res["ln0_g"] = buf[r]
    res["ln0_b"] = buf[r + 1]
    return res, r + 2


def kernel(x, ln0_g, ln0_b, w_in, b_in, lb_logits, g_norm_w, w_a, w_dw, b_dw, conv_ln_g, conv_ln_b, w_b, b_b, w_o, ln1_g, ln1_b, w_up, w_down, ln2_g, ln2_b, loss_target, m_ln0_g, m_ln0_b, m_w_in, m_b_in, m_lb_logits, m_g_norm_w, m_w_a, m_w_dw, m_b_dw, m_conv_ln_g, m_conv_ln_b, m_w_b, m_b_b, m_w_o, m_ln1_g, m_ln1_b, m_w_up, m_w_down, m_ln2_g, m_ln2_b, v_ln0_g, v_ln0_b, v_w_in, v_b_in, v_lb_logits, v_g_norm_w, v_w_a, v_w_dw, v_b_dw, v_conv_ln_g, v_conv_ln_b, v_w_b, v_b_b, v_w_o, v_ln1_g, v_ln1_b, v_w_up, v_w_down, v_ln2_g, v_ln2_b):
    L, D = w_in.shape[0], w_in.shape[1]
    T = x.shape[0] * x.shape[1]
    Dn = w_in.shape[2]
    rs = w_a.shape[1]
    rd = w_down.shape[1]
    cu = w_up.shape[2]
    F = rd * N_DEV
    hv = g_norm_w.shape[1]
    alpha = (2 * L) ** 0.25
    my_x, my_y, my_c = lax.axis_index("x"), lax.axis_index("y"), lax.axis_index("c")
    dev_arr = jnp.reshape(4 * my_x + 2 * my_y + my_c, (1,)).astype(jnp.int32)

    o_a, o_b, o_o, o_d = D, D + rs, D + 2 * rs, D + 3 * rs
    taps = jnp.pad(w_dw, ((0, 0), (0, CONV_HALO - CONV_WIDTH), (0, 0))).reshape(L * CONV_HALO, w_dw.shape[2])
    taps_all = _all_gather_small(taps)
    w_dw_full = taps_all.transpose(1, 0, 2).reshape(L, CONV_HALO, D)

    started, gathered = {}, {}

    def start_gather(key, after):
        l, part = key
        rest = [w_a[l], w_b[l], w_o[l], w_down[l]]
        rows = dict(all=[w_in[l]] + rest, rest=rest)
        if part == "in":
            shards = [w_in[l].astype(ACT_DTYPE)]
        else:
            shards = [jnp.concatenate(rows[part], axis=0).astype(ACT_DTYPE),
                      jnp.swapaxes(w_up[l], 0, 1).astype(ACT_DTYPE)]
        started[key] = _gather_start("gather_start_%d_%s" % key, shards, [_place_own(s, dev_arr) for s in shards],
                                     after)
        return started[key][4]

    def finish_gather(key, after):
        send, recv, thru, zone, _ = started[key]
        _, zn = _gather_wait("gather_wait_%d_%s" % key, thru, zone, send, recv, after)
        gathered[key] = _gather_finish(zn)

    def w_in_of(l):
        return gathered[(l, "in") if l == 0 else (l, "all")][0]

    def rest_of(l):
        ga, gb = gathered[(l, "rest") if l == 0 else (l, "all")]
        base = 0 if l == 0 else D
        return dict(
            w_a=ga[:, base:base + rs, :].reshape(D, D),
            w_b=ga[:, base + rs:base + 2 * rs, :].reshape(D, D),
            w_o=ga[:, base + 2 * rs:base + 3 * rs, :].reshape(D, D),
            w_down=ga[:, base + 3 * rs:base + 3 * rs + rd, :].reshape(F, D),
            w_up=gb.reshape(2 * F, D))

    def weights(l):
        return dict(w_in=w_in_of(l), **rest_of(l))

    lbs = _lb_fwd(lb_logits)

    def params(l):
        return dict(b_in=b_in[l].reshape(N_SEC, 1, D), lbs=lbs[l].reshape(1, D), g_norm_w=g_norm_w[l].reshape(1, hv),
                    w_dw=w_dw_full[l], b_dw=b_dw[l].reshape(1, D), conv_ln_g=conv_ln_g[l].reshape(1, D),
                    conv_ln_b=conv_ln_b[l].reshape(1, D), b_b=b_b[l].reshape(1, D), ln1_g=ln1_g[l], ln1_b=ln1_b[l],
                    ln2_g=ln2_g[l], ln2_b=ln2_b[l])

    x2d = x.reshape(T, D)
    token = start_gather((0, "in"), taps_all)
    token = start_gather((0, "rest"), token)
    if L > 1:
        token = start_gather((1, "all"), token)
    xc, xc_bf = _ln_fwd("ln0", x2d, None, 1.0, ln0_g, ln0_b, dep=token)
    finish_gather((0, "in"), xc_bf)
    saved = []
    for l in range(L):
        if l == 0:
            def rest(after):
                finish_gather((0, "rest"), after)
                return rest_of(0)
            token = None
        else:
            rest = functools.partial(lambda after, l: rest_of(l), l=l)
            token = start_gather((l + 1, "all"), gathered[(l, "all")][0]) if l + 1 < L else None
        xc, xc_bf, s = _layer_fwd(xc, xc_bf, w_in_of(l), rest, params(l), alpha, dep=token)
        saved.append(s)
        if l + 1 < L:
            finish_gather((l + 1, "all"), xc_bf)

    c_arr = jnp.reshape(my_c, (1,)).astype(jnp.int32)
    chip = 2 * my_x + my_y
    dx, loss_row = _loss_fwd_bwd(xc, loss_target.reshape(T, D))
    small = [None] * L
    pending = None
    upd_big = {n: None for n in ("w_in", "w_a", "w_b", "w_o", "w_down", "w_up")}
    wmv = dict(w_in=(w_in, m_w_in, v_w_in), w_a=(w_a, m_w_a, v_w_a), w_b=(w_b, m_w_b, v_w_b),
               w_o=(w_o, m_w_o, v_w_o), w_down=(w_down, m_w_down, v_w_down),
               w_up=tuple(jnp.swapaxes(a, 1, 2) for a in (w_up, m_w_up, v_w_up)))

    def send_buffer(name, g):
        return g if name == "w_in" else g.reshape(N_DEV, wmv[name][0].shape[1], D)

    def update_layer(l, names, q, r2):
        pre = jnp.stack([chip, jnp.int32(l)]).astype(jnp.int32)
        for k, name in enumerate(names):
            w, m, v = wmv[name]
            r, C = w.shape[1], w.shape[2]
            tr = max(t for t in range(16, 513, 16) if r % t == 0)
            nb = r // tr
            specs = [pl.BlockSpec((None, tr, C), lambda i, s: (s[0], i, 0))]
            specs += [pl.BlockSpec((None, tr, C), functools.partial(lambda i, s, j: (j, i, 0), j=j)) for j in range(3)]
            upd_big[name] = _adamw(
                "adamw_" + name, w.reshape(L * r, C), m.reshape(L * r, C), v.reshape(L * r, C),
                [q[k], r2[k], r2[k], r2[k]], specs, tr, prefetch=pre, nsteps=nb,
                row_map=functools.partial(lambda i, s, nb: (s[1] * nb + i, 0), nb=nb), prev=upd_big[name])

    def pair_sums(names, grads):
        sends = [send_buffer(n, grads[n]) for n in names]
        return [_pair_add(p, r, c_arr) for p, r in zip(sends, _exchange_sibling(sends))]

    def start_reduce(tag, names, qs, after=None):
        s_send, s_recv, q_thru, zones, token = _exchange_chips_start("reduce_start_" + tag, qs, after)
        return (tag, names, (s_send, s_recv), list(q_thru), list(zones)), token

    def finish_reduce(l, handle, after):
        tag, names, sems, q, zones = handle
        r2 = _exchange_chips_wait("reduce_wait_" + tag, q, zones, sems[0], sems[1], after)
        update_layer(l, names, q, r2)

    names_all = tuple(upd_big)
    names_early = names_all[1:]
    token = None
    first_early = []
    for l in range(L - 1, -1, -1):
        if l > 0:
            dx, big, small[l] = _layer_bwd(dx, saved[l], weights(l), params(l), alpha, dep=token)
            if pending is not None:
                finish_reduce(l + 1, pending, dx)
            pending, token = start_reduce(str(l), names_all, pair_sums(names_all, big))
        else:
            def early(grads):
                if pending is not None:
                    finish_reduce(1, pending, grads["w_a"])
                handle, tok = start_reduce("0_rest", names_early, pair_sums(names_early, grads))
                first_early.append(handle)
                return tok
            dx, big, small[l] = _layer_bwd(dx, saved[l], weights(l), params(l), alpha, dep=token, early=early)
            q_in = pair_sums(("w_in",), big)
    dx0, _, dln0_g, dln0_b = _ln_bwd("ln0_bwd", x2d, dx, ln0_g)
    dlb_logits = _lb_bwd(lb_logits, jnp.concatenate([small[l]["lbs"] for l in range(L)], axis=0))

    small_l = {n: [small[l][n] for l in range(L)] for n in _SMALL if n != "lb_logits"}
    small_l["lb_logits"] = [dlb_logits[l] for l in range(L)]
    loss_pad = jnp.pad(loss_row, ((0, 0), (0, D - LANES)))
    part = jnp.concatenate([_pack_small(small_l, dln0_g, dln0_b, loss_pad, D, L)]
                           + [small[l]["w_dw"] for l in range(L)], axis=0)
    parts_all = _all_gather_small(part)
    n_small = part.shape[0] - L * CONV_HALO

    last, _ = start_reduce("0_in", ("w_in",), q_in, after=parts_all)


    inputs = dict(b_in=(b_in, m_b_in, v_b_in), lb_logits=(lb_logits, m_lb_logits, v_lb_logits),
                  g_norm_w=(g_norm_w, m_g_norm_w, v_g_norm_w), b_dw=(b_dw, m_b_dw, v_b_dw),
                  conv_ln_g=(conv_ln_g, m_conv_ln_g, v_conv_ln_g), conv_ln_b=(conv_ln_b, m_conv_ln_b, v_conv_ln_b),
                  b_b=(b_b, m_b_b, v_b_b), ln1_g=(ln1_g, m_ln1_g, v_ln1_g), ln1_b=(ln1_b, m_ln1_b, v_ln1_b),
                  ln2_g=(ln2_g, m_ln2_g, v_ln2_g), ln2_b=(ln2_b, m_ln2_b, v_ln2_b))
    zero_row = jnp.zeros((1, D), F32)
    packed = [_pack_small({n: [inputs[n][i][l] for l in range(L)] for n in _SMALL},
                          (ln0_g, m_ln0_g, v_ln0_g)[i], (ln0_b, m_ln0_b, v_ln0_b)[i], zero_row, D, L)
              for i in range(3)]
    small_specs = [pl.BlockSpec((None, n_small, D), functools.partial(lambda i, d: (d, 0, 0), d=d))
                   for d in range(N_DEV)]
    s_out = _adamw("adamw_small", packed[0], packed[1], packed[2], [parts_all] * N_DEV, small_specs, n_small)
    s_g, n_rows = _unpack_small(s_out[0], D, L, hv)
    s_d, _ = _unpack_small(s_out[1], D, L, hv)
    s_m, _ = _unpack_small(s_out[2], D, L, hv)
    s_v, _ = _unpack_small(s_out[3], D, L, hv)
    loss = s_out[0][n_rows, 0]

    cw = w_dw.shape[2]
    dev = 4 * my_x + 2 * my_y + my_c
    tap_parts = lax.dynamic_slice_in_dim(parts_all[:, n_small:, :], dev * cw, cw, axis=2)
    tap_specs = [pl.BlockSpec((None, L * CONV_HALO, cw), functools.partial(lambda i, d: (d, 0, 0), d=d))
                 for d in range(N_DEV)]
    pad_t = lambda a: jnp.pad(a, ((0, 0), (0, CONV_HALO - CONV_WIDTH), (0, 0))).reshape(L * CONV_HALO, cw)
    t_out = _adamw("adamw_taps", pad_t(w_dw), pad_t(m_w_dw), pad_t(v_w_dw), [tap_parts] * N_DEV, tap_specs,
                   L * CONV_HALO)
    finish_reduce(0, first_early[0], t_out[0])
    finish_reduce(0, last, upd_big["w_up"][0])
    upd ={n: [o.reshape(wmv[n][0].shape) for o in outs] for n, outs in upd_big.items()}
    upd["w_up"] = [jnp.swapaxes(o, 1, 2) for o in upd["w_up"]]
    upd["w_dw"] = [o.reshape(L, CONV_HALO, cw)[:, :CONV_WIDTH, :] for o in t_out]

    order = ["ln0_g", "ln0_b", "w_in", "b_in", "lb_logits", "g_norm_w", "w_a", "w_dw", "b_dw", "conv_ln_g",
             "conv_ln_b", "w_b", "b_b", "w_o", "ln1_g", "ln1_b", "w_up", "w_down", "ln2_g", "ln2_b"]
    small_sets = (s_g, s_d, s_m, s_v)
    outs = [loss, dx0.reshape(x.shape)]
    for i in range(4):
        for n in order:
            outs.append(upd[n][i] if n in upd else small_sets[i][n])
    return tuple(outs)
```

```python
import functools

import jax
import jax.numpy as jnp
from jax import lax
from jax.experimental import pallas as pl
from jax.experimental.pallas import tpu as pltpu

F32 = jnp.float32
MXU_DTYPE = jnp.bfloat16
ACT_DTYPE = jnp.bfloat16

LANES = 128
SUB = 8
N_DEV = 8
N_SEC = 8
CONV_WIDTH = 31
CONV_HALO = 32
HG_C = 16
LN_EPS = 1e-5
RMS_EPS = 1e-6
F_MIN = 1e-30
LOG2E = 1.4426950408889634
ADAM_LR = 0.001
ADAM_B1 = 0.9
ADAM_B2 = 0.999
ADAM_EPS = 1e-08
ADAM_WD = 0.01
ADAM_STEP = 10
VMEM_LIMIT = 56 * 1024 * 1024
MESH = pl.DeviceIdType.MESH

_NN = (((1,), (0,)), ((), ()))
_NT = (((1,), (1,)), ((), ()))
_TN = (((0,), (0,)), ((), ()))


_ANY = pl.BlockSpec(memory_space=pl.ANY)
_HBM = pl.BlockSpec(memory_space=pltpu.HBM)
_SEM = pl.BlockSpec(memory_space=pltpu.SEMAPHORE)
_EFFECT = pltpu.SideEffectType.DATAFLOW_SIDE_EFFECTING


def _cp(*sem):
    return pltpu.CompilerParams(dimension_semantics=tuple(sem), vmem_limit_bytes=VMEM_LIMIT)


def _pick(n, cands):
    for c in cands:
        if c <= n and n % c == 0:
            return c
    return n


def _silu(x):
    return x * jax.nn.sigmoid(x)


def _dsilu(x):
    s = jax.nn.sigmoid(x)
    return s * (1.0 + x * (1.0 - s))


def _matmul(name, a, b, *, dims, grid, a_spec, b_spec, out_shape, out_spec, acc_shape, nk,
            bias=None, bias_spec=None, add=None, add_spec=None, add_scale=1.0, dep=None):
    has_bias, has_add = bias is not None, add is not None
    kaxis = len(grid) - 1

    def body(*refs):
        a_ref, b_ref = refs[0], refs[1]
        pos = 2
        bias_ref = add_ref = None
        if has_bias:
            bias_ref = refs[pos]
            pos += 1
        if has_add:
            add_ref = refs[pos]
            pos += 1
        if dep is not None:
            pos += 1
        o_ref = refs[pos]
        acc_ref = refs[pos + 1] if nk > 1 else None

        part = lax.dot_general(a_ref[...].astype(MXU_DTYPE), b_ref[...].astype(MXU_DTYPE), dims,
                               preferred_element_type=F32)

        def finish(r):
            if has_bias:
                r = r + bias_ref[...]
            if has_add:
                r = r + add_scale * add_ref[...]
            o_ref[...] = r.astype(o_ref.dtype)

        if nk == 1:
            finish(part)
        else:
            k = pl.program_id(kaxis)

            @pl.when(k == 0)
            def _():
                acc_ref[...] = part

            @pl.when(k > 0)
            def _():
                acc_ref[...] += part

            @pl.when(k == nk - 1)
            def _():
                finish(acc_ref[...])

    ins, specs = [a, b], [a_spec, b_spec]
    if has_bias:
        ins.append(bias)
        specs.append(bias_spec)
    if has_add:
        ins.append(add)
        specs.append(add_spec)
    if dep is not None:
        ins.append(dep)
        specs.append(_ANY)
    sem =("parallel",) * (len(grid) - 1) + ("arbitrary",) if nk > 1 else ("parallel",) * len(grid)
    return pl.pallas_call(
        body, name=name, grid=grid, in_specs=specs, out_specs=out_spec, out_shape=out_shape,
        scratch_shapes=[pltpu.VMEM(acc_shape, F32)] if nk > 1 else [],
        compiler_params=_cp(*sem))(*ins)


def _mm_rows(name, a, b, dims, n_out, out_dtype, bias=None):
    M, K = a.shape
    tm = _pick(M, (512, 256, 128, 64, 32, 16))
    return _matmul(
        name, a, b, dims=dims, grid=(M // tm,),
        a_spec=pl.BlockSpec((tm, K), lambda i: (i, 0)),
        b_spec=pl.BlockSpec(b.shape, lambda i: (0, 0)),
        out_shape=jax.ShapeDtypeStruct((M, n_out), out_dtype),
        out_spec=pl.BlockSpec((tm, n_out), lambda i: (i, 0)),
        acc_shape=(tm, n_out), nk=1,
        bias=bias, bias_spec=None if bias is None else pl.BlockSpec((1, n_out), lambda i: (0, 0)))


def _mm_nn(name, a, b, out_dtype, bias=None):
    return _mm_rows(name, a, b, _NN, b.shape[1], out_dtype, bias)


def _mm_nt(name, a, b, out_dtype):
    return _mm_rows(name, a, b, _NT, b.shape[0], out_dtype)


def _mm_tn(name, a, b, out_dtype):
    K, M = a.shape
    N = b.shape[1]
    tm = _pick(M, (256, 128))
    return _matmul(
        name, a, b, dims=_TN, grid=(M // tm,),
        a_spec=pl.BlockSpec((K, tm), lambda i: (0, i)),
        b_spec=pl.BlockSpec((K, N), lambda i: (0, 0)),
        out_shape=jax.ShapeDtypeStruct((M, N), out_dtype),
        out_spec=pl.BlockSpec((tm, N), lambda i: (i, 0)),
        acc_shape=(tm, N), nk=1)


def _proj_in(x_bf, w_in, b_in, dep=None):
    T, D = x_bf.shape
    tn = _pick(D, (512, 256, 128))
    return _matmul(
        "proj_in", x_bf, w_in, dims=_NN, grid=(N_SEC, D // tn),
        a_spec=pl.BlockSpec((T, D), lambda s, j: (0, 0)),
        b_spec=pl.BlockSpec((None, D, tn), lambda s, j: (s, 0, j)),
        out_shape=jax.ShapeDtypeStruct((N_SEC, T, D), F32),
        out_spec=pl.BlockSpec((None, T, tn), lambda s, j: (s, 0, j)),
        acc_shape=(T, tn), nk=1,
        bias=b_in, bias_spec=pl.BlockSpec((None, 1, tn), lambda s, j: (s, 0, j)), dep=dep)


def _proj_in_dx(dh, w_in, add, add_scale):
    _, T, D = dh.shape
    tn = _pick(D, (512, 256, 128))
    return _matmul(
        "proj_in_dx", dh, w_in, dims=_NT, grid=(D // tn, N_SEC),
        a_spec=pl.BlockSpec((None, T, D), lambda j, s: (s, 0, 0)),
        b_spec=pl.BlockSpec((None, tn, D), lambda j, s: (s, j, 0)),
        out_shape=jax.ShapeDtypeStruct((T, D), F32),
        out_spec=pl.BlockSpec((T, tn), lambda j, s: (0, j)),
        acc_shape=(T, tn), nk=N_SEC,
        add=add, add_spec=pl.BlockSpec((T, tn), lambda j, s: (0, j)), add_scale=add_scale)


def _proj_in_dw(x_bf, dh):
    _, T, D = dh.shape
    tn = _pick(D, (512, 256, 128))

    def body(x_ref, dh_ref, dw_ref, db_ref):
        dhv = dh_ref[...]
        dw_ref[...] = lax.dot_general(x_ref[...].astype(MXU_DTYPE), dhv.astype(MXU_DTYPE), _TN,
                                      preferred_element_type=F32).astype(dw_ref.dtype)
        db_ref[...] = jnp.sum(dhv.astype(F32), axis=0, keepdims=True)

    return pl.pallas_call(
        body, name="proj_in_dw", grid=(N_SEC, D // tn),
        in_specs=[pl.BlockSpec((T, D), lambda s, j: (0, 0)), pl.BlockSpec((None, T, tn), lambda s, j: (s, 0, j))],
        out_specs=[pl.BlockSpec((None, D, tn), lambda s, j: (s, 0, j)),
                   pl.BlockSpec((None, 1, tn), lambda s, j: (s, 0, j))],
        out_shape=[jax.ShapeDtypeStruct((N_SEC, D, D), ACT_DTYPE), jax.ShapeDtypeStruct((N_SEC, 1, D), F32)],
        compiler_params=_cp("parallel", "parallel"))(x_bf, dh)


def _ffn_up(x_bf, w_up_t):
    T, D = x_bf.shape
    F = w_up_t.shape[0] // 2
    tn = _pick(F, (256, 128))
    nb = F // tn
    return _matmul(
        "ffn_up", x_bf, w_up_t, dims=_NT, grid=(2, nb),
        a_spec=pl.BlockSpec((T, D), lambda p, j: (0, 0)),
        b_spec=pl.BlockSpec((tn, D), lambda p, j: (p * nb + j, 0)),
        out_shape=jax.ShapeDtypeStruct((2, T, F), F32),
        out_spec=pl.BlockSpec((None, T, tn), lambda p, j: (p, 0, j)),
        acc_shape=(T, tn), nk=1)


def _ffn_up_dx(dup, w_up_t, add, add_scale):
    _, T, F = dup.shape
    D = w_up_t.shape[1]
    tn = _pick(D, (512, 256, 128))
    tk = _pick(F, (1408, 256, 128))
    nb = F // tk
    return _matmul(
        "ffn_up_dx", dup, w_up_t, dims=_NN, grid=(D // tn, 2 * nb),
        a_spec=pl.BlockSpec((None, T, tk), lambda j, k: (k // nb, 0, k % nb)),
        b_spec=pl.BlockSpec((tk, tn), lambda j, k: (k, j)),
        out_shape=jax.ShapeDtypeStruct((T, D), F32),
        out_spec=pl.BlockSpec((T, tn), lambda j, k: (0, j)),
        acc_shape=(T, tn), nk=2 * nb,
        add=add, add_spec=pl.BlockSpec((T, tn), lambda j, k: (0, j)), add_scale=add_scale)


def _ffn_up_dw(x_bf, dup):
    _, T, F = dup.shape
    D = x_bf.shape[1]
    tm = _pick(F, (1408, 256, 128))
    nb = F // tm
    return _matmul(
        "ffn_up_dw", dup, x_bf, dims=_TN, grid=(2, nb),
        a_spec=pl.BlockSpec((None, T, tm), lambda p, j: (p, 0, j)),
        b_spec=pl.BlockSpec((T, D), lambda p, j: (0, 0)),
        out_shape=jax.ShapeDtypeStruct((2 * F, D), ACT_DTYPE),
        out_spec=pl.BlockSpec((tm, D), lambda p, j: (p * nb + j, 0)),
        acc_shape=(tm, D), nk=1)


def _ln_fwd(name, a, res, alpha, g, b, dep=None):
    T, D = a.shape
    tr = _pick(T, (256, 128, 64, 32, 16))
    has_res = res is not None

    def body(*refs):
        if has_res:
            a_ref, r_ref, g_ref, b_ref = refs[:4]
            y_ref, yb_ref, z_ref = refs[-3:]
            z = alpha * a_ref[...] + r_ref[...]
            z_ref[...] = z
        else:
            a_ref, g_ref, b_ref = refs[:3]
            y_ref, yb_ref = refs[-2:]
            z = a_ref[...]
        mu = jnp.mean(z, axis=-1, keepdims=True)
        zc = z - mu
        var = jnp.mean(zc * zc, axis=-1, keepdims=True)
        y = zc * lax.rsqrt(var + LN_EPS) * g_ref[...] + b_ref[...]
        y_ref[...] = y
        yb_ref[...] = y.astype(ACT_DTYPE)

    row = pl.BlockSpec((tr, D), lambda i: (i, 0))
    vec = pl.BlockSpec((1, D), lambda i: (0, 0))
    ins = [a] + ([res] if has_res else []) + [g.reshape(1, D), b.reshape(1, D)]
    in_specs = [row] + ([row] if has_res else []) + [vec, vec]
    if dep is not None:
        ins.append(dep)
        in_specs.append(_ANY)
    out_shape = [jax.ShapeDtypeStruct((T, D), F32), jax.ShapeDtypeStruct((T, D), ACT_DTYPE)]
    if has_res:
        out_shape.append(jax.ShapeDtypeStruct((T, D), F32))
    return pl.pallas_call(
        body, name=name, grid=(T // tr,), in_specs=in_specs,
        out_specs=[row] * len(out_shape), out_shape=out_shape, compiler_params=_cp("parallel"))(*ins)


def _ln_bwd(name, z, dy, g, dep=None):
    T, D = z.shape
    tr = _pick(T, (256, 128, 64, 32, 16))

    def body(z_ref, dy_ref, g_ref, *rest):
        dz_ref, dzb_ref, dg_ref, db_ref = rest[-4:]

        @pl.when(pl.program_id(0) == 0)
        def _():
            dg_ref[...] = jnp.zeros_like(dg_ref)
            db_ref[...] = jnp.zeros_like(db_ref)

        zv = z_ref[...]
        dy_ = dy_ref[...]
        mu = jnp.mean(zv, axis=-1, keepdims=True)
        zc = zv - mu
        rstd = lax.rsqrt(jnp.mean(zc * zc, axis=-1, keepdims=True) + LN_EPS)
        xhat = zc * rstd
        dxh = dy_ * g_ref[...]
        dz = rstd * (dxh - jnp.mean(dxh, axis=-1, keepdims=True)
                     - xhat * jnp.mean(dxh * xhat, axis=-1, keepdims=True))
        dz_ref[...] = dz
        dzb_ref[...] = dz.astype(ACT_DTYPE)
        dg_ref[...] += jnp.sum(dy_ * xhat, axis=0, keepdims=True)
        db_ref[...] += jnp.sum(dy_, axis=0, keepdims=True)

    row = pl.BlockSpec((tr, D), lambda i: (i, 0))
    vec = pl.BlockSpec((1, D), lambda i: (0, 0))
    ins, in_specs = [z, dy, g.reshape(1, D)], [row, row, vec]
    if dep is not None:
        ins.append(dep)
        in_specs.append(_ANY)
    return pl.pallas_call(
        body, name=name, grid=(T // tr,), in_specs=in_specs, out_specs=[row, row, vec, vec],
        out_shape=[jax.ShapeDtypeStruct((T, D), F32), jax.ShapeDtypeStruct((T, D), ACT_DTYPE),
                   jax.ShapeDtypeStruct((1, D), F32), jax.ShapeDtypeStruct((1, D), F32)],
        compiler_params=_cp("arbitrary"))(*ins)


def _loss_fwd_bwd(y, target):
    T, D = y.shape
    tr = _pick(T, (256, 128, 64, 32, 16))

    def body(y_ref, t_ref, dy_ref, l_ref):
        @pl.when(pl.program_id(0) == 0)
        def _():
            l_ref[...] = jnp.zeros_like(l_ref)

        e = y_ref[...] - t_ref[...]
        dy_ref[...] = e * (1.0 / D)
        row = jnp.sum(e * e, axis=-1, keepdims=True) * (1.0 / D)
        l_ref[...] += 0.5 * jnp.sum(row, axis=0, keepdims=True)

    rowspec = pl.BlockSpec((tr, D), lambda i: (i, 0))
    return pl.pallas_call(
        body, name="loss", grid=(T // tr,), in_specs=[rowspec, rowspec],
        out_specs=[rowspec, pl.BlockSpec((1, LANES), lambda i: (0, 0))],
        out_shape=[jax.ShapeDtypeStruct((T, D), F32), jax.ShapeDtypeStruct((1, LANES), F32)],
        compiler_params=_cp("arbitrary"))(y, target)


def _gate_fwd(y_h, y_c, h):
    T, D = y_h.shape
    tr = _pick(T, (256, 128, 64, 32, 16))

    def body(yh_ref, yc_ref, gh_ref, gc_ref, m_ref):
        m = jax.nn.sigmoid(gh_ref[...]) * yh_ref[...] + jax.nn.sigmoid(gc_ref[...]) * yc_ref[...]
        m_ref[...] = m.astype(ACT_DTYPE)

    row = pl.BlockSpec((tr, D), lambda i: (i, 0))
    return pl.pallas_call(
        body, name="gate_fwd", grid=(T // tr,),
        in_specs=[row, row, pl.BlockSpec((None, tr, D), lambda i: (6, i, 0)),
                  pl.BlockSpec((None, tr, D), lambda i: (7, i, 0))],
        out_specs=row, out_shape=jax.ShapeDtypeStruct((T, D), ACT_DTYPE),
        compiler_params=_cp("parallel"))(y_h, y_c, h, h)


def _layer_norm_rows(z, g, b):
    mu = jnp.mean(z, axis=-1, keepdims=True)
    zc = z - mu
    var = jnp.mean(zc * zc, axis=-1, keepdims=True)
    return zc * lax.rsqrt(var + LN_EPS) * g + b


def _mixer_out(y_hg, y_cv, h, xin, w_a, w_b, b_b, w_o, alpha, ln_g, ln_b):
    T, D = xin.shape
    tm = _pick(T, (256, 128, 64, 32, 16))

    def body(yhg_ref, ycv_ref, gh_ref, gc_ref, x_ref, wa_ref, wb_ref, bb_ref, wo_ref, g_ref, b_ref,
             yh_ref, yc_ref, m_ref, x1_ref, x1b_ref, z_ref):
        y_h = jnp.dot(yhg_ref[...].astype(MXU_DTYPE), wa_ref[...].astype(MXU_DTYPE), preferred_element_type=F32)
        y_c = jnp.dot(ycv_ref[...].astype(MXU_DTYPE), wb_ref[...].astype(MXU_DTYPE),
                      preferred_element_type=F32) + bb_ref[...]
        yh_ref[...] = y_h
        yc_ref[...] = y_c
        merged = (jax.nn.sigmoid(gh_ref[...]) * y_h + jax.nn.sigmoid(gc_ref[...]) * y_c).astype(ACT_DTYPE)
        m_ref[...] = merged
        z = alpha * x_ref[...] + jnp.dot(merged.astype(MXU_DTYPE), wo_ref[...].astype(MXU_DTYPE),
                                         preferred_element_type=F32)
        z_ref[...] = z
        x1 = _layer_norm_rows(z, g_ref[...], b_ref[...])
        x1_ref[...] = x1
        x1b_ref[...] = x1.astype(ACT_DTYPE)

    row = pl.BlockSpec((tm, D), lambda i: (i, 0))
    mat = pl.BlockSpec((D, D), lambda i: (0, 0))
    vec = pl.BlockSpec((1, D), lambda i: (0, 0))
    f32, act = jax.ShapeDtypeStruct((T, D), F32), jax.ShapeDtypeStruct((T, D), ACT_DTYPE)
    return pl.pallas_call(
        body, name="mixer_out", grid=(T // tm,),
        in_specs=[row, row, pl.BlockSpec((None, tm, D), lambda i: (6, i, 0)),
                  pl.BlockSpec((None, tm, D), lambda i: (7, i, 0)), row, mat, mat, vec, mat, vec, vec],
        out_specs=[row] * 6, out_shape=[f32, f32, act, f32, act, f32],
        compiler_params=_cp("parallel"))(y_hg, y_cv, h, h, xin, w_a, w_b, b_b, w_o, ln_g.reshape(1, D),
                                         ln_b.reshape(1, D))


def _ffn_up_swiglu(x_bf, w_up_t):
    T, D = x_bf.shape
    F = w_up_t.shape[0] // 2
    tn = _pick(F, (256, 128))
    nb = F // tn

    def body(x_ref, wg_ref, wv_ref, up_ref, act_ref):
        xv = x_ref[...].astype(MXU_DTYPE)
        g = lax.dot_general(xv, wg_ref[...].astype(MXU_DTYPE), _NT, preferred_element_type=F32)
        v = lax.dot_general(xv, wv_ref[...].astype(MXU_DTYPE), _NT, preferred_element_type=F32)
        up_ref[0] = g
        up_ref[1] = v
        act_ref[...] = (_silu(g) * v).astype(ACT_DTYPE)

    return pl.pallas_call(
        body, name="ffn_up", grid=(nb,),
        in_specs=[pl.BlockSpec((T, D), lambda j: (0, 0)), pl.BlockSpec((tn, D), lambda j: (j, 0)),
                  pl.BlockSpec((tn, D), lambda j: (nb + j, 0))],
        out_specs=[pl.BlockSpec((2, T, tn), lambda j: (0, 0, j)), pl.BlockSpec((T, tn), lambda j: (0, j))],
        out_shape=[jax.ShapeDtypeStruct((2, T, F), F32), jax.ShapeDtypeStruct((T, F), ACT_DTYPE)],
        compiler_params=_cp("parallel"))(x_bf, w_up_t, w_up_t)


def _ffn_down_ln2(act, w_down, x1, alpha, ln_g, ln_b):
    T, D = x1.shape
    F = act.shape[1]
    tm = _pick(T, (256, 128, 64, 32, 16))

    def body(a_ref, w_ref, x_ref, g_ref, b_ref, x2_ref, x2b_ref, z_ref):
        z = alpha * x_ref[...] + jnp.dot(a_ref[...].astype(MXU_DTYPE), w_ref[...].astype(MXU_DTYPE),
                                         preferred_element_type=F32)
        z_ref[...] = z
        x2 = _layer_norm_rows(z, g_ref[...], b_ref[...])
        x2_ref[...] = x2
        x2b_ref[...] = x2.astype(ACT_DTYPE)

    row = pl.BlockSpec((tm, D), lambda i: (i, 0))
    vec = pl.BlockSpec((1, D), lambda i: (0, 0))
    f32, actt = jax.ShapeDtypeStruct((T, D), F32), jax.ShapeDtypeStruct((T, D), ACT_DTYPE)
    return pl.pallas_call(
        body, name="ffn_down", grid=(T // tm,),
        in_specs=[pl.BlockSpec((tm, F), lambda i: (i, 0)), pl.BlockSpec((F, D), lambda i: (0, 0)), row, vec, vec],
        out_specs=[row] * 3, out_shape=[f32, actt, f32],
        compiler_params=_cp("parallel"))(act, w_down, x1, ln_g.reshape(1, D), ln_b.reshape(1, D))


def _ffn_down_dx_swiglu(dz_bf, w_down, up):
    T, D = dz_bf.shape
    F = w_down.shape[0]
    tm = _pick(T, (256, 128, 64, 32, 16))

    def body(dz_ref, w_ref, up_ref, dup_ref):
        da = lax.dot_general(dz_ref[...].astype(MXU_DTYPE), w_ref[...].astype(MXU_DTYPE), _NT,
                             preferred_element_type=F32)
        ug = up_ref[0]
        dup_ref[0] = (da * up_ref[1] * _dsilu(ug)).astype(ACT_DTYPE)
        dup_ref[1] = (da * _silu(ug)).astype(ACT_DTYPE)

    blk = pl.BlockSpec((2, tm, F), lambda i: (0, i, 0))
    return pl.pallas_call(
        body, name="ffn_down_dx", grid=(T // tm,),
        in_specs=[pl.BlockSpec((tm, D), lambda i: (i, 0)), pl.BlockSpec((F, D), lambda i: (0, 0)), blk],
        out_specs=blk, out_shape=jax.ShapeDtypeStruct((2, T, F), ACT_DTYPE),
        compiler_params=_cp("parallel"))(dz_bf, w_down, up)


def _mixer_out_bwd(z, dx1, y_h, y_c, h, w_a, w_b, w_o, ln_g):
    T, D = z.shape
    tm = _pick(T, (256, 128, 64, 32, 16))

    def body(z_ref, dx_ref, yh_ref, yc_ref, gh_ref, gc_ref, wa_ref, wb_ref, wo_ref, g_ref,
             dz_ref, dzb_ref, dyh_ref, dyc_ref, dyhg_ref, dycv_ref, dh_ref, dg_ref, db_ref, dbb_ref):
        @pl.when(pl.program_id(0) == 0)
        def _():
            dg_ref[...] = jnp.zeros_like(dg_ref)
            db_ref[...] = jnp.zeros_like(db_ref)
            dbb_ref[...] = jnp.zeros_like(dbb_ref)

        zv = z_ref[...]
        dy_ = dx_ref[...]
        mu = jnp.mean(zv, axis=-1, keepdims=True)
        zc = zv - mu
        rstd = lax.rsqrt(jnp.mean(zc * zc, axis=-1, keepdims=True) + LN_EPS)
        xhat = zc * rstd
        dxh = dy_ * g_ref[...]
        dz = rstd * (dxh - jnp.mean(dxh, axis=-1, keepdims=True)
                     - xhat * jnp.mean(dxh * xhat, axis=-1, keepdims=True))
        dz_ref[...] = dz
        dzb = dz.astype(ACT_DTYPE)
        dzb_ref[...] = dzb
        dg_ref[...] += jnp.sum(dy_ * xhat, axis=0, keepdims=True)
        db_ref[...] += jnp.sum(dy_, axis=0, keepdims=True)
        dm_ = lax.dot_general(dzb.astype(MXU_DTYPE), wo_ref[...].astype(MXU_DTYPE), _NT, preferred_element_type=F32)
        sh = jax.nn.sigmoid(gh_ref[...])
        sc = jax.nn.sigmoid(gc_ref[...])
        dyc = dm_ * sc
        dyh_b = (dm_ * sh).astype(ACT_DTYPE)
        dyc_b = dyc.astype(ACT_DTYPE)
        dyh_ref[...] = dyh_b
        dyc_ref[...] = dyc_b
        dbb_ref[...] += jnp.sum(dyc, axis=0, keepdims=True)
        dh_ref[0] = (dm_ * yh_ref[...] * sh * (1.0 - sh)).astype(ACT_DTYPE)
        dh_ref[1] = (dm_ * yc_ref[...] * sc * (1.0 - sc)).astype(ACT_DTYPE)
        dyhg_ref[...] = lax.dot_general(dyh_b.astype(MXU_DTYPE), wa_ref[...].astype(MXU_DTYPE), _NT,
                                        preferred_element_type=F32)
        dycv_ref[...] = lax.dot_general(dyc_b.astype(MXU_DTYPE), wb_ref[...].astype(MXU_DTYPE), _NT,
                                        preferred_element_type=F32)

    row = pl.BlockSpec((tm, D), lambda i: (i, 0))
    mat = pl.BlockSpec((D, D), lambda i: (0, 0))
    vec = pl.BlockSpec((1, D), lambda i: (0, 0))
    f32, act = jax.ShapeDtypeStruct((T, D), F32), jax.ShapeDtypeStruct((T, D), ACT_DTYPE)
    v32 = jax.ShapeDtypeStruct((1, D), F32)
    return pl.pallas_call(
        body, name="mixer_out_bwd", grid=(T // tm,),
        in_specs=[row, row, row, row, pl.BlockSpec((None, tm, D), lambda i: (6, i, 0)),
                  pl.BlockSpec((None, tm, D), lambda i: (7, i, 0)), mat, mat, mat, vec],
        out_specs=[row, row, row, row, row, row, pl.BlockSpec((2, tm, D), lambda i: (3, i, 0)), vec, vec, vec],
        out_shape=[f32, act, act, act, f32, f32, jax.ShapeDtypeStruct((N_SEC, T, D), ACT_DTYPE), v32, v32, v32],
        compiler_params=_cp("arbitrary"))(z, dx1, y_h, y_c, h, h, w_a, w_b, w_o, ln_g.reshape(1, D))


def _gate_bwd(dm, y_h, y_c, h):
    T, D = y_h.shape
    tr = _pick(T, (256, 128, 64, 32, 16))

    def body(dm_ref, yh_ref, yc_ref, gh_ref, gc_ref, dyh_ref, dyc_ref, dbb_ref, dh_ref):
        @pl.when(pl.program_id(0) == 0)
        def _():
            dbb_ref[...] = jnp.zeros_like(dbb_ref)

        dm_ = dm_ref[...]
        sh = jax.nn.sigmoid(gh_ref[...])
        sc = jax.nn.sigmoid(gc_ref[...])
        dyc = dm_ * sc
        dyh_ref[...] = (dm_ * sh).astype(ACT_DTYPE)
        dyc_ref[...] = dyc.astype(ACT_DTYPE)
        dbb_ref[...] += jnp.sum(dyc, axis=0, keepdims=True)
        dh_ref[0] = (dm_ * yh_ref[...] * sh * (1.0 - sh)).astype(ACT_DTYPE)
        dh_ref[1] = (dm_ * yc_ref[...] * sc * (1.0 - sc)).astype(ACT_DTYPE)

    row = pl.BlockSpec((tr, D), lambda i: (i, 0))
    return pl.pallas_call(
        body, name="gate_bwd", grid=(T // tr,),
        in_specs=[row, row, row, pl.BlockSpec((None, tr, D), lambda i: (6, i, 0)),
                  pl.BlockSpec((None, tr, D), lambda i: (7, i, 0))],
        out_specs=[row, row, pl.BlockSpec((1, D), lambda i: (0, 0)),
                   pl.BlockSpec((2, tr, D), lambda i: (3, i, 0))],
        out_shape=[jax.ShapeDtypeStruct((T, D), ACT_DTYPE), jax.ShapeDtypeStruct((T, D), ACT_DTYPE),
                   jax.ShapeDtypeStruct((1, D), F32), jax.ShapeDtypeStruct((N_SEC, T, D), ACT_DTYPE)],
        compiler_params=_cp("arbitrary"))(dm, y_h, y_c, h, h)


def _swiglu_fwd(up):
    _, T, F = up.shape
    tr = _pick(T, (128, 64, 32, 16))

    def body(up_ref, act_ref):
        act_ref[...] = (_silu(up_ref[0]) * up_ref[1]).astype(ACT_DTYPE)

    return pl.pallas_call(
        body, name="swiglu_fwd", grid=(T // tr,),
        in_specs=[pl.BlockSpec((2, tr, F), lambda i: (0, i, 0))],
        out_specs=pl.BlockSpec((tr, F), lambda i: (i, 0)),
        out_shape=jax.ShapeDtypeStruct((T, F), ACT_DTYPE), compiler_params=_cp("parallel"))(up)


def _swiglu_bwd(dact, up):
    _, T, F = up.shape
    tr = _pick(T, (128, 64, 32, 16))

    def body(da_ref, up_ref, dup_ref):
        da = da_ref[...]
        ug = up_ref[0]
        dup_ref[0] = (da * up_ref[1] * _dsilu(ug)).astype(ACT_DTYPE)
        dup_ref[1] = (da * _silu(ug)).astype(ACT_DTYPE)

    blk = pl.BlockSpec((2, tr, F), lambda i: (0, i, 0))
    return pl.pallas_call(
        body, name="swiglu_bwd", grid=(T // tr,),
        in_specs=[pl.BlockSpec((tr, F), lambda i: (i, 0)), blk], out_specs=blk,
        out_shape=jax.ShapeDtypeStruct((2, T, F), ACT_DTYPE), compiler_params=_cp("parallel"))(dact, up)


def _lb_softmax(x):
    L = x.shape[0]
    rows = [x[l:l + 1] for l in range(L)]
    m = rows[0]
    for r in rows[1:]:
        m = jnp.maximum(m, r)
    e = [jnp.exp(r - m) for r in rows]
    s = e[0]
    for r in e[1:]:
        s = s + r
    return [r / s for r in e]


def _lb_fwd(lb_logits):
    L, D = lb_logits.shape

    def body(x_ref, o_ref):
        p = _lb_softmax(x_ref[...])
        run = jnp.zeros_like(p[0])
        for l in range(L):
            if l > 0:
                run = run + p[l]
            o_ref[pl.ds(l, 1), :] = run

    return pl.pallas_call(body, name="lb_fwd", out_shape=jax.ShapeDtypeStruct((L, D), F32))(lb_logits)


def _lb_bwd(lb_logits, dlbs):
    L, D = lb_logits.shape

    def body(x_ref, d_ref, o_ref):
        p = _lb_softmax(x_ref[...])
        d = d_ref[...]
        dp = [jnp.zeros_like(p[0]) for _ in range(L)]
        run = jnp.zeros_like(p[0])
        for j in range(L - 1, 0, -1):
            run = run + d[j:j + 1]
            dp[j] = run
        dot = dp[0] * p[0]
        for j in range(1, L):
            dot = dot + dp[j] * p[j]
        for j in range(L):
            o_ref[pl.ds(j, 1), :] = p[j] * (dp[j] - dot)

    return pl.pallas_call(body, name="lb_bwd", out_shape=jax.ShapeDtypeStruct((L, D), F32))(lb_logits, dlbs)


def _blk_cumsum(x, c, reverse=False):
    n = x.shape[0]
    pos = lax.broadcasted_iota(jnp.int32, x.shape, 0) % c
    s = 1
    while s < c:
        if reverse:
            shifted = pltpu.roll(x, n - s, 0)
            x = x + jnp.where(pos + s < c, shifted, 0.0)
        else:
            shifted = pltpu.roll(x, s, 0)
            x = x + jnp.where(pos >= s, shifted, 0.0)
        s *= 2
    return x


def _hgrn_prologue(q_ref, f_ref, lb_ref):
    lbv = lb_ref[...]
    z = f_ref[...]
    sig = jax.nn.sigmoid(z)
    one_m = 1.0 - lbv
    f = lbv + one_m * sig
    logf = jnp.log(jnp.maximum(f, F_MIN))
    k = one_m * jax.nn.sigmoid(-z)
    q = _silu(q_ref[...])
    return q, k, logf, f, sig, one_m


def _hgrn_fwd(h, lbs_l, gw):
    _, T, D = h.shape
    nh = D // LANES
    c = HG_C
    Tt = _pick(T, (256, 128, 64, 32, 16))
    nb = Tt // c
    ng = c // SUB

    def body(q_ref, f_ref, i_ref, g_ref, lb_ref, gw_ref, o_ref, y_ref, sall_ref,
             st_ref, G_s, q_s, k_s, W_s, R_s, dS_s, o_s):
        @pl.when(pl.program_id(1) == 0)
        def _():
            st_ref[...] = jnp.zeros_like(st_ref)

        q, k, logf, _, _, _ = _hgrn_prologue(q_ref, f_ref, lb_ref)
        G_s[...] = _blk_cumsum(logf, c) * LOG2E
        q_s[...] = q
        k_s[...] = k
        ones = jnp.ones((LANES, LANES), MXU_DTYPE)
        rowid = lax.broadcasted_iota(jnp.int32, (SUB, LANES), 0)
        zero = jnp.zeros((SUB, LANES), F32)
        for bi in range(nb):
            r0 = bi * c
            glast = G_s[pl.ds(r0 + c - 1, 1), :]
            kd = k_s[pl.ds(r0, c), :] * jnp.exp2(glast - G_s[pl.ds(r0, c), :])
            dS_s[bi] = lax.dot_general(i_ref[pl.ds(r0, c), :].astype(MXU_DTYPE), kd.astype(MXU_DTYPE), _TN,
                                       preferred_element_type=F32)
        st = st_ref[...]
        for bi in range(nb):
            sall_ref[bi] = st
            st = st * jnp.exp2(G_s[pl.ds(bi * c + c - 1, 1), :]) + dS_s[bi]
        st_ref[...] = st
        for bi in range(nb):
            r0 = bi * c
            qd = q_s[pl.ds(r0, c), :] * jnp.exp2(G_s[pl.ds(r0, c), :])
            o_s[pl.ds(r0, c), :] = lax.dot_general(qd.astype(MXU_DTYPE), sall_ref[bi].astype(MXU_DTYPE), _NT,
                                                   preferred_element_type=F32)
        for bi in range(nb):
            r0 = bi * c
            w0 = bi * c * c
            Gg = [G_s[pl.ds(r0 + gi * SUB, SUB), :] for gi in range(ng)]
            qg = [q_s[pl.ds(r0 + gi * SUB, SUB), :] for gi in range(ng)]
            for s in range(c):
                gs = G_s[pl.ds(r0 + s, 1), :]
                ks = k_s[pl.ds(r0 + s, 1), :]
                parts = []
                for gi in range(ng):
                    if gi < s // SUB:
                        parts.append(zero)
                        continue
                    e = jnp.exp2(jnp.minimum(Gg[gi] - gs, 0.0))
                    if gi == s // SUB:
                        e = jnp.where(rowid >= s - gi * SUB, e, 0.0)
                    parts.append(e * qg[gi] * ks)
                W_s[pl.ds(w0 + s * c, c), :] = jnp.concatenate(parts, axis=0).astype(MXU_DTYPE)
        R_s[...] = jnp.dot(W_s[...], ones, preferred_element_type=F32)
        for bi in range(nb):
            r0 = bi * c
            w0 = bi * c * c
            acc = [o_s[pl.ds(r0 + gi * SUB, SUB), :] for gi in range(ng)]
            for s in range(c):
                vs = i_ref[pl.ds(r0 + s, 1), :]
                for gi in range(s // SUB, ng):
                    acc[gi] = acc[gi] + R_s[pl.ds(w0 + s * c + gi * SUB, SUB), :] * vs
            o_s[pl.ds(r0, c), :] = jnp.concatenate(acc, axis=0)
        o = o_s[...]
        n = o * lax.rsqrt(jnp.mean(o * o, axis=-1, keepdims=True) + RMS_EPS)
        o_ref[...] = o
        y_ref[...] = (n * gw_ref[...] * _silu(g_ref[...])).astype(ACT_DTYPE)

    def sec(s):
        return pl.BlockSpec((None, Tt, LANES), lambda hd, i: (s, i, hd))

    col = pl.BlockSpec((Tt, LANES), lambda hd, i: (i, hd))
    return pl.pallas_call(
        body, name="hgrn_fwd", grid=(nh, T // Tt),
        in_specs=[sec(0), sec(1), sec(2), sec(3), pl.BlockSpec((1, LANES), lambda hd, i: (0, hd)),
                  pl.BlockSpec((1, LANES), lambda hd, i: (0, 0))],
        out_specs=[col, col, pl.BlockSpec((nb, None, LANES, LANES), lambda hd, i: (i, hd, 0, 0))],
        out_shape=[jax.ShapeDtypeStruct((T, D), F32), jax.ShapeDtypeStruct((T, D), ACT_DTYPE),
                   jax.ShapeDtypeStruct((T // c, nh, LANES, LANES), F32)],
        scratch_shapes=[pltpu.VMEM((LANES, LANES), F32), pltpu.VMEM((Tt, LANES), F32),
                        pltpu.VMEM((Tt, LANES), F32), pltpu.VMEM((Tt, LANES), F32),
                        pltpu.VMEM((nb * c * c, LANES), MXU_DTYPE), pltpu.VMEM((nb * c * c, LANES), F32),
                        pltpu.VMEM((nb, LANES, LANES), F32), pltpu.VMEM((Tt, LANES), F32)],
        compiler_params=_cp("parallel", "arbitrary"))(h, h, h, h, lbs_l, gw)


def _hgrn_bwd(h, lbs_l, gw, o_pre, st_all, dy, dh):
    _, T, D = h.shape
    nh = D // LANES
    c = HG_C
    Tt = _pick(T, (256, 128, 64, 32, 16))
    nb = Tt // c
    ng = c // SUB
    nT = T // Tt

    def body(q_ref, f_ref, i_ref, g_ref, lb_ref, gw_ref, o_ref, sall_ref, dy_ref, dh_in_ref,
             dh_ref, dlb_ref, dgw_ref,
             dst_ref, G_s, q_s, k_s, do_s, E_s, WP_s, dq_s, dk_s, dv_s, dG_s,
             R_s, dS_s, dstA_s, dqd_s, dkd_s, dvi_s, da_s):
        del dh_in_ref
        hd, ti = pl.program_id(0), pl.program_id(1)

        @pl.when(ti == 0)
        def _():
            dst_ref[...] = jnp.zeros_like(dst_ref)
            dlb_ref[...] = jnp.zeros_like(dlb_ref)

        @pl.when((ti == 0) & (hd == 0))
        def _():
            dgw_ref[...] = jnp.zeros_like(dgw_ref)

        q, k, logf, f, sig, one_m = _hgrn_prologue(q_ref, f_ref, lb_ref)
        G_s[...] = _blk_cumsum(logf, c) * LOG2E
        q_s[...] = q
        k_s[...] = k

        o = o_ref[...]
        gr = g_ref[...]
        dy_ = dy_ref[...]
        rr = lax.rsqrt(jnp.mean(o * o, axis=-1, keepdims=True) + RMS_EPS)
        n = o * rr
        sg = _silu(gr)
        gwv = gw_ref[...]
        dh_ref[3] = (dy_ * n * gwv * _dsilu(gr)).astype(ACT_DTYPE)
        dgw_ref[...] += jnp.sum(dy_ * n * sg, axis=0, keepdims=True)
        dn = dy_ * gwv * sg
        do_s[...] = rr * (dn - n * jnp.mean(dn * n, axis=-1, keepdims=True))

        ones = jnp.ones((LANES, LANES), MXU_DTYPE)
        rowid = lax.broadcasted_iota(jnp.int32, (SUB, LANES), 0)
        rowid_c = lax.broadcasted_iota(jnp.int32, (c, LANES), 0)
        zero = jnp.zeros((SUB, LANES), F32)
        cc = c * c
        for bi in range(nb):
            r0 = bi * c
            qd = q_s[pl.ds(r0, c), :] * jnp.exp2(G_s[pl.ds(r0, c), :])
            dS_s[bi] = lax.dot_general(do_s[pl.ds(r0, c), :].astype(MXU_DTYPE), qd.astype(MXU_DTYPE), _TN,
                                       preferred_element_type=F32)
        dst = dst_ref[...]
        for bi in range(nb - 1, -1, -1):
            dstA_s[bi] = dst
            dst = dst * jnp.exp2(G_s[pl.ds(bi * c + c - 1, 1), :]) + dS_s[bi]
        dst_ref[...] = dst
        for bi in range(nb):
            r0 = bi * c
            glast = G_s[pl.ds(r0 + c - 1, 1), :]
            kd = k_s[pl.ds(r0, c), :] * jnp.exp2(glast - G_s[pl.ds(r0, c), :])
            st = sall_ref[bi]
            dstb = dstA_s[bi]
            dst_m = dstb.astype(MXU_DTYPE)
            dqd_s[pl.ds(r0, c), :] = lax.dot_general(do_s[pl.ds(r0, c), :].astype(MXU_DTYPE), st.astype(MXU_DTYPE),
                                                     _NN, preferred_element_type=F32)
            dkd_s[pl.ds(r0, c), :] = lax.dot_general(i_ref[pl.ds(r0, c), :].astype(MXU_DTYPE), dst_m, _NN,
                                                     preferred_element_type=F32)
            dvi_s[pl.ds(r0, c), :] = lax.dot_general(kd.astype(MXU_DTYPE), dst_m, _NT,
                                                     preferred_element_type=F32)
            da_s[pl.ds(bi * SUB, 1), :] = jnp.sum(dstb * st, axis=0, keepdims=True)
        for bi in range(nb):
            r0 = bi * c
            e0, w0 = bi * cc, bi * 2 * cc
            Gg = [G_s[pl.ds(r0 + gi * SUB, SUB), :] for gi in range(ng)]
            kg = [k_s[pl.ds(r0 + gi * SUB, SUB), :] for gi in range(ng)]
            vg = [i_ref[pl.ds(r0 + gi * SUB, SUB), :] for gi in range(ng)]
            for t in range(c):
                gt = G_s[pl.ds(r0 + t, 1), :]
                qt = q_s[pl.ds(r0 + t, 1), :]
                dot_ = do_s[pl.ds(r0 + t, 1), :]
                ep, wp, pp = [], [], []
                for gi in range(ng):
                    if gi > t // SUB:
                        ep.append(zero)
                        wp.append(zero)
                        pp.append(zero)
                        continue
                    e = jnp.exp2(jnp.minimum(gt - Gg[gi], 0.0))
                    if gi == t // SUB:
                        e = jnp.where(rowid <= t - gi * SUB, e, 0.0)
                    ep.append(e)
                    wp.append(e * kg[gi] * qt)
                    pp.append(vg[gi] * dot_)
                E_s[pl.ds(e0 + t * c, c), :] = jnp.concatenate(ep, axis=0)
                WP_s[pl.ds(w0 + t * c, c), :] = jnp.concatenate(wp, axis=0).astype(MXU_DTYPE)
                WP_s[pl.ds(w0 + cc + t * c, c), :] = jnp.concatenate(pp, axis=0).astype(MXU_DTYPE)
        R_s[...] = jnp.dot(WP_s[...], ones, preferred_element_type=F32)
        for bi in range(nb):
            r0 = bi * c
            e0, w0 = bi * cc, bi * 2 * cc
            kg = [k_s[pl.ds(r0 + gi * SUB, SUB), :] for gi in range(ng)]
            dk_g = [zero] * ng
            dv_g = [zero] * ng
            dq_g = [zero] * ng
            for t in range(c):
                qt = q_s[pl.ds(r0 + t, 1), :]
                dot_ = do_s[pl.ds(r0 + t, 1), :]
                tot = None
                for gi in range(t // SUB + 1):
                    lo = t * c + gi * SUB
                    dae = R_s[pl.ds(w0 + cc + lo, SUB), :] * E_s[pl.ds(e0 + lo, SUB), :]
                    z = dae * kg[gi]
                    tot = z if tot is None else tot + z
                    dk_g[gi] = dk_g[gi] + dae * qt
                    dv_g[gi] = dv_g[gi] + R_s[pl.ds(w0 + lo, SUB), :] * dot_
                gt_ = t // SUB
                dq_g[gt_] = jnp.where(rowid == t - gt_ * SUB, jnp.sum(tot, axis=0, keepdims=True), dq_g[gt_])
            dq_i = jnp.concatenate(dq_g, axis=0)
            dk_i = jnp.concatenate(dk_g, axis=0)
            dv_i = jnp.concatenate(dv_g, axis=0)
            Gb = G_s[pl.ds(r0, c), :]
            qb = q_s[pl.ds(r0, c), :]
            kb = k_s[pl.ds(r0, c), :]
            glast = G_s[pl.ds(r0 + c - 1, 1), :]
            eg = jnp.exp2(Gb)
            egl = jnp.exp2(glast - Gb)
            dqd = dqd_s[pl.ds(r0, c), :]
            dkd = dkd_s[pl.ds(r0, c), :]
            dq_s[pl.ds(r0, c), :] = dqd * eg + dq_i
            dk_s[pl.ds(r0, c), :] = dkd * egl + dk_i
            dv_s[pl.ds(r0, c), :] = dvi_s[pl.ds(r0, c), :] + dv_i
            dkdkd = dkd * kb * egl
            dG = dqd * qb * eg + qb * dq_i - kb * dk_i - dkdkd
            dglast = jnp.sum(dkdkd, axis=0, keepdims=True) + da_s[pl.ds(bi * SUB, 1), :] * jnp.exp2(glast)
            dG_s[pl.ds(r0, c), :] = dG + jnp.where(rowid_c == c - 1, dglast, 0.0)

        dlogf = _blk_cumsum(dG_s[...], c, reverse=True)
        df = jnp.where(f > F_MIN, dlogf / f, 0.0)
        dk = dk_s[...]
        dh_ref[0] = (dq_s[...] * _dsilu(q_ref[...])).astype(ACT_DTYPE)
        dh_ref[1] = ((df - dk) * one_m * sig * (1.0 - sig)).astype(ACT_DTYPE)
        dh_ref[2] = dv_s[...].astype(ACT_DTYPE)
        dlb_ref[...] += jnp.sum((df - dk) * (1.0 - sig), axis=0, keepdims=True)

    def sec(s):
        return pl.BlockSpec((None, Tt, LANES), lambda hd, i: (s, nT - 1 - i, hd))

    col = pl.BlockSpec((Tt, LANES), lambda hd, i: (nT - 1 - i, hd))
    tile = pltpu.VMEM((Tt, LANES), F32)
    return pl.pallas_call(
        body, name="hgrn_bwd", grid=(nh, nT),
        in_specs=[sec(0), sec(1), sec(2), sec(3), pl.BlockSpec((1, LANES), lambda hd, i: (0, hd)),
                  pl.BlockSpec((1, LANES), lambda hd, i: (0, 0)), col,
                  pl.BlockSpec((nb, None, LANES, LANES), lambda hd, i: (nT - 1 - i, hd, 0, 0)), col,
                  pl.BlockSpec(memory_space=pl.ANY)],
        out_specs=[pl.BlockSpec((4, Tt, LANES), lambda hd, i: (0, nT - 1 - i, hd)),
                   pl.BlockSpec((1, LANES), lambda hd, i: (0, hd)),
                   pl.BlockSpec((1, LANES), lambda hd, i: (0, 0))],
        out_shape=[jax.ShapeDtypeStruct(dh.shape, dh.dtype), jax.ShapeDtypeStruct((1, D), F32),
                   jax.ShapeDtypeStruct((1, LANES), F32)],
        scratch_shapes=[pltpu.VMEM((LANES, LANES), F32), tile, tile, tile, tile,
                        pltpu.VMEM((nb * c * c, LANES), F32), pltpu.VMEM((2 * nb * c * c, LANES), MXU_DTYPE),
                        tile, tile, tile, tile,
                        pltpu.VMEM((2 * nb * c * c, LANES), F32), pltpu.VMEM((nb, LANES, LANES), F32),
                        pltpu.VMEM((nb, LANES, LANES), F32), tile, tile, tile, pltpu.VMEM((nb * SUB, LANES), F32)],
        input_output_aliases={9: 0},
        compiler_params=_cp("arbitrary", "arbitrary"))(h, h, h, h, lbs_l, gw, o_pre, st_all, dy, dh)


def _shifted_copies(src, cs, dst, rows):
    for b in range(1, SUB):
        dst[b - 1] = src[pl.ds(b, rows + CONV_HALO - SUB), cs]


def _shifted(src, cs, copies, shift, rows):
    a8, b = divmod(shift, SUB)
    if b == 0:
        return src[pl.ds(shift, rows), cs]
    return copies[b - 1, pl.ds(a8 * SUB, rows), :]


def _conv_fwd(h, w_dw, b_dw, ln_g, ln_b):
    _, T, D = h.shape
    Tt = _pick(T, (256, 128, 64, 32))
    hb = Tt // CONV_HALO
    off = CONV_HALO - (CONV_WIDTH - 1)

    def body(a_ref, b_ref, ap_ref, bp_ref, w_ref, bd_ref, g_ref, be_ref, yc_ref, y_ref, U_s, Ub_s):
        first = pl.program_id(0) == 0
        up = ap_ref[...] * jax.nn.sigmoid(bp_ref[...])
        U_s[pl.ds(0, CONV_HALO), :] = jnp.where(first, 0.0, up)
        U_s[pl.ds(CONV_HALO, Tt), :] = a_ref[...] * jax.nn.sigmoid(b_ref[...])
        for cb in range(D // LANES):
            cs = pl.ds(cb * LANES, LANES)
            _shifted_copies(U_s, cs, Ub_s, Tt)
            acc = jnp.zeros((Tt, LANES), F32)
            for j in range(CONV_WIDTH):
                acc = acc + w_ref[pl.ds(j, 1), cs] * _shifted(U_s, cs, Ub_s, off + j, Tt)
            yc_ref[:, cs] = acc + bd_ref[:, cs]
        yc = yc_ref[...]
        mu = jnp.mean(yc, axis=-1, keepdims=True)
        zc = yc - mu
        var = jnp.mean(zc * zc, axis=-1, keepdims=True)
        ln = zc * lax.rsqrt(var + LN_EPS) * g_ref[...] + be_ref[...]
        y_ref[...] = _silu(ln).astype(ACT_DTYPE)

    def main(s):
        return pl.BlockSpec((None, Tt, D), lambda i: (s, i, 0))

    def prev(s):
        return pl.BlockSpec((None, CONV_HALO, D), lambda i: (s, jnp.maximum(i * hb - 1, 0), 0))

    row = pl.BlockSpec((Tt, D), lambda i: (i, 0))
    vec = pl.BlockSpec((1, D), lambda i: (0, 0))
    return pl.pallas_call(
        body, name="conv_fwd", grid=(T // Tt,),
        in_specs=[main(4), main(5), prev(4), prev(5), pl.BlockSpec((CONV_HALO, D), lambda i: (0, 0)),
                  vec, vec, vec],
        out_specs=[row, row],
        out_shape=[jax.ShapeDtypeStruct((T, D), F32), jax.ShapeDtypeStruct((T, D), ACT_DTYPE)],
        scratch_shapes=[pltpu.VMEM((CONV_HALO + Tt, D), F32),
                        pltpu.VMEM((SUB - 1, Tt + CONV_HALO - SUB, LANES), F32)],
        compiler_params=_cp("parallel"))(h, h, h, h, w_dw, b_dw, ln_g, ln_b)


def _conv_bwd(h, w_dw, ln_g, ln_b, yc, dy, dh):
    _, T, D = h.shape
    Tt = _pick(T, (256, 128, 64, 32))
    hb = Tt // CONV_HALO
    nT = T // Tt
    nhb = T // CONV_HALO
    off = CONV_HALO - (CONV_WIDTH - 1)

    def body(a_ref, b_ref, ap_ref, bp_ref, w_ref, g_ref, be_ref, yc_ref, ycn_ref, dy_ref, dyn_ref, dh_in_ref,
             dh_ref, dw_ref, dbd_ref, dg_ref, dbe_ref, U_s, DY_s, du_s, Ub_s, DYb_s):
        del dh_in_ref
        i = pl.program_id(0)

        @pl.when(i == 0)
        def _():
            dw_ref[...] = jnp.zeros_like(dw_ref)
            dbd_ref[...] = jnp.zeros_like(dbd_ref)
            dg_ref[...] = jnp.zeros_like(dg_ref)
            dbe_ref[...] = jnp.zeros_like(dbe_ref)

        gv = g_ref[...]
        bev = be_ref[...]

        def ln_silu_bwd(ycv, dyv):
            mu = jnp.mean(ycv, axis=-1, keepdims=True)
            zc = ycv - mu
            rstd = lax.rsqrt(jnp.mean(zc * zc, axis=-1, keepdims=True) + LN_EPS)
            xhat = zc * rstd
            dln = dyv * _dsilu(xhat * gv + bev)
            dxh = dln * gv
            dyc = rstd * (dxh - jnp.mean(dxh, axis=-1, keepdims=True)
                          - xhat * jnp.mean(dxh * xhat, axis=-1, keepdims=True))
            return dyc, dln, xhat

        dyc, dln, xhat = ln_silu_bwd(yc_ref[...], dy_ref[...])
        dg_ref[...] += jnp.sum(dln * xhat, axis=0, keepdims=True)
        dbe_ref[...] += jnp.sum(dln, axis=0, keepdims=True)
        dbd_ref[...] += jnp.sum(dyc, axis=0, keepdims=True)
        DY_s[pl.ds(0, Tt), :] = dyc
        dycn, _, _ = ln_silu_bwd(ycn_ref[...], dyn_ref[...])
        DY_s[pl.ds(Tt, CONV_HALO), :] = jnp.where(i == nT - 1, 0.0, dycn)

        sb = jax.nn.sigmoid(b_ref[...])
        av = a_ref[...]
        up = ap_ref[...] * jax.nn.sigmoid(bp_ref[...])
        U_s[pl.ds(0, CONV_HALO), :] = jnp.where(i == 0, 0.0, up)
        U_s[pl.ds(CONV_HALO, Tt), :] = av * sb

        for cb in range(D // LANES):
            cs = pl.ds(cb * LANES, LANES)
            _shifted_copies(U_s, cs, Ub_s, Tt)
            _shifted_copies(DY_s, cs, DYb_s, Tt)
            dyb = DY_s[pl.ds(0, Tt), cs]
            acc = jnp.zeros((Tt, LANES), F32)
            for j in range(CONV_WIDTH):
                acc = acc + w_ref[pl.ds(j, 1), cs] * _shifted(DY_s, cs, DYb_s, CONV_WIDTH - 1 - j, Tt)
                dw_ref[pl.ds(j, 1), cs] += jnp.sum(dyb * _shifted(U_s, cs, Ub_s, off + j, Tt), axis=0, keepdims=True)
            du_s[:, cs] = acc
        du = du_s[...]
        dh_ref[0] = (du * sb).astype(ACT_DTYPE)
        dh_ref[1] = (du * av * sb * (1.0 - sb)).astype(ACT_DTYPE)

    def main(s):
        return pl.BlockSpec((None, Tt, D), lambda i: (s, i, 0))

    def prev(s):
        return pl.BlockSpec((None, CONV_HALO, D), lambda i: (s, jnp.maximum(i * hb - 1, 0), 0))

    row = pl.BlockSpec((Tt, D), lambda i: (i, 0))
    nxt = pl.BlockSpec((CONV_HALO, D), lambda i: (jnp.minimum((i + 1) * hb, nhb - 1), 0))
    vec = pl.BlockSpec((1, D), lambda i: (0, 0))
    wspec = pl.BlockSpec((CONV_HALO, D), lambda i: (0, 0))
    return pl.pallas_call(
        body, name="conv_bwd", grid=(nT,),
        in_specs=[main(4), main(5), prev(4), prev(5), wspec, vec, vec, row, nxt, row, nxt,
                  pl.BlockSpec(memory_space=pl.ANY)],
        out_specs=[pl.BlockSpec((2, Tt, D), lambda i: (2, i, 0)), wspec, vec, vec, vec],
        out_shape=[jax.ShapeDtypeStruct(dh.shape, dh.dtype), jax.ShapeDtypeStruct((CONV_HALO, D), F32),
                   jax.ShapeDtypeStruct((1, D), F32), jax.ShapeDtypeStruct((1, D), F32),
                   jax.ShapeDtypeStruct((1, D), F32)],
        scratch_shapes=[pltpu.VMEM((CONV_HALO + Tt, D), F32), pltpu.VMEM((Tt + CONV_HALO, D), F32),
                        pltpu.VMEM((Tt, D), F32),
                        pltpu.VMEM((SUB - 1, Tt + CONV_HALO - SUB, LANES), F32),
                        pltpu.VMEM((SUB - 1, Tt + CONV_HALO - SUB, LANES), F32)],
        input_output_aliases={11: 0},
        compiler_params=_cp("arbitrary"))(h, h, h, h, w_dw, ln_g, ln_b, yc, yc, dy, dy, dh)


def _adamw(name, w, m, v, parts, part_specs, tr, prefetch=None, nsteps=None, row_map=None, prev=None):
    R, C = w.shape
    bc1 = 1.0 - ADAM_B1 ** ADAM_STEP
    bc2 = 1.0 - ADAM_B2 ** ADAM_STEP
    npart = len(parts)
    npre = 0 if prefetch is None else 1
    nprev = 0 if prev is None else 4

    def body(*refs):
        refs = refs[npre:]
        w_ref, m_ref, v_ref = refs[:3]
        p_refs = refs[3:3 + npart]
        g_ref, d_ref, mo_ref, vo_ref = refs[3 + npart + nprev:]
        g = p_refs[0][...].astype(F32)
        for p in p_refs[1:]:
            g = g + p[...].astype(F32)
        wv = w_ref[...]
        mn = ADAM_B1 * m_ref[...] + (1.0 - ADAM_B1) * g
        vn = ADAM_B2 * v_ref[...] + (1.0 - ADAM_B2) * (g * g)
        m_hat = mn / bc1
        v_hat = vn / bc2
        g_ref[...] = g
        d_ref[...] = -ADAM_LR * (m_hat / (jnp.sqrt(v_hat) + ADAM_EPS) + ADAM_WD * wv)
        mo_ref[...] = mn
        vo_ref[...] = vn

    if row_map is None:
        row_map = (lambda i: (i, 0)) if prefetch is None else (lambda i, s: (i, 0))
    row = pl.BlockSpec((tr, C), row_map)
    out = jax.ShapeDtypeStruct((R, C), F32)
    gs = pltpu.PrefetchScalarGridSpec(
        num_scalar_prefetch=npre, grid=(R // tr if nsteps is None else nsteps,),
        in_specs=[row, row, row] + list(part_specs) + [_ANY] * nprev, out_specs=[row] * 4)
    args = ([prefetch] if npre else []) + [w, m, v] + list(parts) + (list(prev) if nprev else [])
    first_prev = npre + 3 + npart
    return pl.pallas_call(body, name=name, grid_spec=gs, out_shape=[out] * 4,
                          input_output_aliases={first_prev + i: i for i in range(nprev)},
                          compiler_params=_cp("parallel"))(*args)


def _pair_add(p, r1, my_c):
    _, R, C = r1.shape
    tr = max(t for t in range(16, 1025, 16) if R % t == 0)

    def body(c_ref, p_ref, r_ref, q_ref):
        del c_ref
        q_ref[...] = (p_ref[...].astype(F32) + r_ref[...].astype(F32)).astype(q_ref.dtype)

    gs = pltpu.PrefetchScalarGridSpec(
        num_scalar_prefetch=1, grid=(4, R // tr),
        in_specs=[pl.BlockSpec((None, tr, C), lambda j, i, c: (2 * j + c[0], i, 0)),
                  pl.BlockSpec((None, tr, C), lambda j, i, c: (j, i, 0))],
        out_specs=pl.BlockSpec((None, tr, C), lambda j, i, c: (j, i, 0)))
    return pl.pallas_call(body, name="pair_add", grid_spec=gs, out_shape=jax.ShapeDtypeStruct(r1.shape, r1.dtype),
                          compiler_params=_cp("parallel", "parallel"))(my_c, p, r1)


def _place():
    x, y, c = lax.axis_index("x"), lax.axis_index("y"), lax.axis_index("c")
    chips = [(1 - x, y), (x, 1 - y), (1 - x, 1 - y)]
    return x, y, c, chips


def _hbm(a):
    return pltpu.with_memory_space_constraint(a, pltpu.HBM)


def _gather_targets():
    x, y, c, chips = _place()
    return 4 * x + 2 * y + c, [(x, y, 1 - c)] + [(*chip, c) for chip in chips]


def _gather_start(name, shards, zones, after=None):
    n = len(shards)
    lands = [_hbm(z) for z in zones]
    n_in = 2 * n + (0 if after is None else 1)

    def body(*refs):
        srcs, zones = refs[:n], refs[n:2 * n]
        send, recv, token = refs[n_in], refs[n_in + 1], refs[-1]
        mine, targets = _gather_targets()
        for a in range(n):
            for k, to in enumerate(targets):
                pltpu.make_async_remote_copy(
                    src_ref=srcs[a], dst_ref=zones[a].at[mine], send_sem=send.at[4 * a + k],
                    recv_sem=recv.at[4 * a + k], device_id=to, device_id_type=MESH).start()
        token[...] = jnp.zeros_like(token)

    sem = pltpu.SemaphoreType.DMA((4 * n,))
    out_shape = ([sem, sem] + [pltpu.HBM(s.shape, s.dtype) for s in shards]
                 + [pltpu.HBM(z.shape, z.dtype) for z in lands] + [jax.ShapeDtypeStruct((8, LANES), F32)])
    outs = pl.pallas_call(
        body, name=name, out_shape=out_shape, in_specs=[_HBM] * (2 * n) + ([] if after is None else [_ANY]),
        out_specs=[_SEM, _SEM] + [_HBM] * (2 * n) + [pl.BlockSpec(memory_space=pltpu.VMEM)],
        input_output_aliases={i: 2 + i for i in range(2 * n)},
        compiler_params=pltpu.CompilerParams(has_side_effects=_EFFECT))(
            *[_hbm(s) for s in shards], *lands, *([] if after is None else [after]))
    return outs[0], outs[1], list(outs[2:2 + n]), list(outs[2 + n:2 + 2 * n]), outs[-1]


def _gather_wait(name, shards, zones, send, recv, after):
    per = len(shards)

    def body(*refs):
        srcs, lz = refs[:per], refs[per:2 * per]
        send_s, recv_s = refs[2 * per], refs[2 * per + 1]
        mine, targets = _gather_targets()
        for a in range(per):
            for k, to in enumerate(targets):
                cp = pltpu.make_async_remote_copy(
                    src_ref=srcs[a], dst_ref=lz[a].at[mine], send_sem=send_s.at[4 * a + k],
                    recv_sem=recv_s.at[4 * a + k], device_id=to, device_id_type=MESH)
                cp.wait_send()
                cp.wait_recv()

    outs = pl.pallas_call(
        body, name=name, out_shape=[pltpu.HBM(s.shape, s.dtype) for s in shards + zones],
        in_specs=[_HBM] * (2 * per) + [_SEM, _SEM, _ANY], out_specs=[_HBM] * (2 * per),
        input_output_aliases={i: i for i in range(2 * per)},
        compiler_params=pltpu.CompilerParams(has_side_effects=_EFFECT))(*shards, *zones, send, recv, after)
    return outs[:per], outs[per:]


def _gather_finish(zones):
    n = len(zones)

    def body(*refs):
        lz = refs[n:2 * n]
        send_sems, recv_sems = refs[2 * n:]
        x, y, c, chips = _place()

        def fwd(a, j, pc):
            cx, cy = chips[j]
            blk = lz[a].at[4 * cx + 2 * cy + pc]
            return pltpu.make_async_remote_copy(
                src_ref=blk, dst_ref=blk, send_sem=send_sems.at[3 * a + j], recv_sem=recv_sems.at[3 * a + j],
                device_id=(x, y, 1 - c), device_id_type=MESH)

        sends = [fwd(a, j, c) for a in range(n) for j in range(3)]
        for cp in sends:
            cp.start()
        for a in range(n):
            for j in range(3):
                fwd(a, j, 1 - c).wait_recv()
        for cp in sends:
            cp.wait_send()

    return pl.pallas_call(
        body, name="gather_finish", out_shape=[jax.ShapeDtypeStruct(z.shape, z.dtype) for z in zones],
        in_specs=[_ANY] * n, out_specs=[_ANY] * n, input_output_aliases={a: a for a in range(n)},
        scratch_shapes=[pltpu.SemaphoreType.DMA((3 * n,)), pltpu.SemaphoreType.DMA((3 * n,))])(*zones)


def _place_own(shard, dev):
    R, C = shard.shape
    tr = max(t for t in range(16, 1025, 16) if R % t == 0)

    def body(d_ref, s_ref, z_ref):
        del d_ref
        z_ref[...] = s_ref[...]

    gs = pltpu.PrefetchScalarGridSpec(
        num_scalar_prefetch=1, grid=(R // tr,), in_specs=[pl.BlockSpec((tr, C), lambda i, d: (i, 0))],
        out_specs=pl.BlockSpec((None, tr, C), lambda i, d: (d[0], i, 0)))
    return pl.pallas_call(body, name="place_own", grid_spec=gs,
                          out_shape=jax.ShapeDtypeStruct((N_DEV, R, C), shard.dtype),
                          compiler_params=_cp("parallel"))(dev, shard)


def _exchange_sibling(bufs):
    n_arr = len(bufs)

    def body(*refs):
        srcs, outs = refs[:n_arr], refs[n_arr:2 * n_arr]
        send_sems, recv_sems = refs[2 * n_arr:]
        x, y, c, _ = _place()
        copies = []
        for n in range(n_arr):
            for j in range(4):
                copies.append(pltpu.make_async_remote_copy(
                    src_ref=srcs[n].at[2 * j + 1 - c], dst_ref=outs[n].at[j],
                    send_sem=send_sems.at[4 * n + j], recv_sem=recv_sems.at[4 * n + j],
                    device_id=(x, y, 1 - c), device_id_type=MESH))
        for cp in copies:
            cp.start()
        for cp in copies:
            cp.wait()

    return pl.pallas_call(
        body, name="exchange_sibling",
        out_shape=[jax.ShapeDtypeStruct((4,) + b.shape[1:], b.dtype) for b in bufs],
        in_specs=[_ANY] * n_arr, out_specs=[_ANY] * n_arr,
        scratch_shapes=[pltpu.SemaphoreType.DMA((4 * n_arr,)), pltpu.SemaphoreType.DMA((4 * n_arr,))])(*bufs)


def _chip_copies(srcs, zones, send, recv):
    _, _, c, chips = _place()
    return [pltpu.make_async_remote_copy(
        src_ref=srcs[n].at[2 * cx + cy], dst_ref=zones[n].at[k], send_sem=send.at[3 * n + k],
        recv_sem=recv.at[3 * n + k], device_id=(cx, cy, c), device_id_type=MESH)
        for n in range(len(srcs)) for k, (cx, cy) in enumerate(chips)]


def _exchange_chips_start(name, bufs, after=None):
    n = len(bufs)
    n_in = 2 * n + (0 if after is None else 1)
    lands = [_hbm(lax.empty((3,) + b.shape[1:], b.dtype)) for b in bufs]

    def body(*refs):
        srcs, zones = refs[:n], refs[n:2 * n]
        send, recv, token = refs[n_in], refs[n_in + 1], refs[-1]
        for cp in _chip_copies(srcs, zones, send, recv):
            cp.start()
        token[...] = jnp.zeros_like(token)

    sem = pltpu.SemaphoreType.DMA((3 * n,))
    outs = pl.pallas_call(
        body, name=name,
        out_shape=[sem, sem] + [pltpu.HBM(b.shape, b.dtype) for b in bufs]
        + [pltpu.HBM(z.shape, z.dtype) for z in lands] + [jax.ShapeDtypeStruct((8, LANES), F32)],
        in_specs=[_HBM] * (2 * n) + ([] if after is None else [_ANY]),
        out_specs=[_SEM, _SEM] + [_HBM] * (2 * n) + [pl.BlockSpec(memory_space=pltpu.VMEM)],
        input_output_aliases={i: 2 + i for i in range(2 * n)},
        compiler_params=pltpu.CompilerParams(has_side_effects=_EFFECT))(
            *[_hbm(b) for b in bufs], *lands, *([] if after is None else [after]))
    return outs[0], outs[1], outs[2:2 + n], outs[2 + n:2 + 2 * n], outs[-1]


def _exchange_chips_wait(name, bufs, zones, send, recv, after):
    n = len(bufs)

    def body(*refs):
        for cp in _chip_copies(refs[:n], refs[n:2 * n], refs[2 * n], refs[2 * n + 1]):
            cp.wait_send()
            cp.wait_recv()

    outs = pl.pallas_call(
        body, name=name, out_shape=[pltpu.HBM(a.shape, a.dtype) for a in list(bufs) + list(zones)],
        in_specs=[_HBM] * (2 * n) + [_SEM, _SEM, _ANY], out_specs=[_HBM] * (2 * n),
        input_output_aliases={i: i for i in range(2 * n)},
        compiler_params=pltpu.CompilerParams(has_side_effects=_EFFECT))(*bufs, *zones, send, recv, after)
    return outs[n:]


def _all_gather_small(part):
    def body(src, out, send_sems, recv_sems, local_sem):
        x, y, c, _ = _place()
        mine = pltpu.make_async_copy(src, out.at[4 * x + 2 * y + c], local_sem)
        mine.start()
        copies = []
        for r in range(1, N_DEV):
            dx, dy, dc = (r >> 2) & 1, (r >> 1) & 1, r & 1
            peer = (1 - x if dx else x, 1 - y if dy else y, 1 - c if dc else c)
            copies.append(pltpu.make_async_remote_copy(
                src_ref=src, dst_ref=out.at[4 * x + 2 * y + c],
                send_sem=send_sems.at[r - 1], recv_sem=recv_sems.at[r - 1],
                device_id=peer, device_id_type=MESH))
        for cp in copies:
            cp.start()
        for cp in copies:
            cp.wait()
        mine.wait()

    return pl.pallas_call(
        body, name="all_gather_small",
        out_shape=jax.ShapeDtypeStruct((N_DEV,) + part.shape, part.dtype),
        in_specs=[_ANY], out_specs=_ANY,
        scratch_shapes=[pltpu.SemaphoreType.DMA((N_DEV - 1,)), pltpu.SemaphoreType.DMA((N_DEV - 1,)),
                        pltpu.SemaphoreType.DMA])(part)


def _layer_fwd(xin, xin_bf, w_in, rest, P, alpha, dep=None):
    h = _proj_in(xin_bf, w_in, P["b_in"], dep=dep)
    o_pre, y_hg, st_all = _hgrn_fwd(h, P["lbs"], P["g_norm_w"])
    yc_pre, y_cv = _conv_fwd(h, P["w_dw"], P["b_dw"], P["conv_ln_g"], P["conv_ln_b"])
    W = rest(y_cv)
    y_h, y_c, merged, x1, x1_bf, z1 = _mixer_out(y_hg, y_cv, h, xin, W["w_a"], W["w_b"], P["b_b"], W["w_o"], alpha,
                                                 P["ln1_g"], P["ln1_b"])
    up, act = _ffn_up_swiglu(x1_bf, W["w_up"])
    x2, x2_bf, z2 = _ffn_down_ln2(act, W["w_down"], x1, alpha, P["ln2_g"], P["ln2_b"])
    saved = dict(xin_bf=xin_bf, h=h, o_pre=o_pre, y_hg=y_hg, st_all=st_all, yc_pre=yc_pre, y_cv=y_cv,
                 y_h=y_h, y_c=y_c, merged=merged, z1=z1, x1_bf=x1_bf, up=up, act=act, z2=z2)
    return x2, x2_bf, saved


def _layer_bwd(dx2, S, W, P, alpha, dep=None, early=None):
    dz2, dz2_bf, dln2_g, dln2_b = _ln_bwd("ln2_bwd", S["z2"], dx2, P["ln2_g"], dep=dep)
    dup = _ffn_down_dx_swiglu(dz2_bf, W["w_down"], S["up"])
    dw_down = _mm_tn("ffn_down_dw", S["act"], dz2_bf, ACT_DTYPE)
    dx1 = _ffn_up_dx(dup, W["w_up"], dz2, alpha)
    dw_up = _ffn_up_dw(S["x1_bf"], dup)
    dz1, dz1_bf, dy_h, dy_c, dy_hg, dy_cv, dh, dln1_g, dln1_b, db_b = _mixer_out_bwd(
        S["z1"], dx1, S["y_h"], S["y_c"], S["h"], W["w_a"], W["w_b"], W["w_o"], P["ln1_g"])
    dw_o = _mm_tn("mix_out_dw", S["merged"], dz1_bf, ACT_DTYPE)
    dw_b = _mm_tn("branch_b_dw", S["y_cv"], dy_c, ACT_DTYPE)
    dw_a = _mm_tn("branch_a_dw", S["y_hg"], dy_h, ACT_DTYPE)
    if early is not None:
        token = early(dict(w_a=dw_a, w_b=dw_b, w_o=dw_o, w_down=dw_down, w_up=dw_up))
        dy_cv = dy_cv + token[0, 0]
    dh, dw_dw, db_dw, dcln_g, dcln_b = _conv_bwd(S["h"], P["w_dw"], P["conv_ln_g"], P["conv_ln_b"],
                                                 S["yc_pre"], dy_cv, dh)
    dh, dlbs, dgw = _hgrn_bwd(S["h"], P["lbs"], P["g_norm_w"], S["o_pre"], S["st_all"], dy_hg, dh)
    dxin = _proj_in_dx(dh, W["w_in"], dz1, alpha)
    dw_in, db_in = _proj_in_dw(S["xin_bf"], dh)
    big = dict(w_in=dw_in, w_a=dw_a, w_b=dw_b, w_o=dw_o, w_down=dw_down, w_up=dw_up)
    small = dict(b_in=db_in, lbs=dlbs, g_norm_w=dgw, w_dw=dw_dw, b_dw=db_dw, conv_ln_g=dcln_g,
                 conv_ln_b=dcln_b, b_b=db_b, ln1_g=dln1_g, ln1_b=dln1_b, ln2_g=dln2_g, ln2_b=dln2_b)
    return dxin, big, small


_SMALL = ("b_in", "lb_logits", "g_norm_w", "b_dw", "conv_ln_g", "conv_ln_b", "b_b", "ln1_g", "ln1_b", "ln2_g",
          "ln2_b")


def _pack_small(per_layer, ln0_g, ln0_b, extra_row, D, L):
    rows = []
    for l in range(L):
        for n in _SMALL:
            a = per_layer[n][l]
            if n == "b_in":
                rows.append(a.reshape(N_SEC, D))
            elif n == "g_norm_w":
                rows.append(jnp.pad(a.reshape(1, -1), ((0, 0), (0, D - a.size))))
            else:
                rows.append(a.reshape(1, D))
    rows += [ln0_g.reshape(1, D), ln0_b.reshape(1, D), extra_row]
    buf = jnp.concatenate(rows, axis=0)
    pad = (-buf.shape[0]) % 8
    return jnp.pad(buf, ((0, pad), (0, 0)))


def _unpack_small(buf, D, L, hv):
    out = {n: [] for n in _SMALL}
    r = 0
    for l in range(L):
        for n in _SMALL:
            if n == "b_in":
                out[n].append(buf[r:r + N_SEC].reshape(N_SEC * D))
                r += N_SEC
            elif n == "g_norm_w":
                out[n].append(buf[r, :hv])
                r += 1
            else:
                out[n].append(buf[r])
                r += 1
    res = {n: jnp.stack(v) for n, v in out.items()}
    res["ln0_g"] = buf[r]
    res["ln0_b"] = buf[r + 1]
    return res, r + 2


def kernel(x, ln0_g, ln0_b, w_in, b_in, lb_logits, g_norm_w, w_a, w_dw, b_dw, conv_ln_g, conv_ln_b, w_b, b_b, w_o, ln1_g, ln1_b, w_up, w_down, ln2_g, ln2_b, loss_target, m_ln0_g, m_ln0_b, m_w_in, m_b_in, m_lb_logits, m_g_norm_w, m_w_a, m_w_dw, m_b_dw, m_conv_ln_g, m_conv_ln_b, m_w_b, m_b_b, m_w_o, m_ln1_g, m_ln1_b, m_w_up, m_w_down, m_ln2_g, m_ln2_b, v_ln0_g, v_ln0_b, v_w_in, v_b_in, v_lb_logits, v_g_norm_w, v_w_a, v_w_dw, v_b_dw, v_conv_ln_g, v_conv_ln_b, v_w_b, v_b_b, v_w_o, v_ln1_g, v_ln1_b, v_w_up, v_w_down, v_ln2_g, v_ln2_b):
    L, D = w_in.shape[0], w_in.shape[1]
    T = x.shape[0] * x.shape[1]
    Dn = w_in.shape[2]
    rs = w_a.shape[1]
    rd = w_down.shape[1]
    cu = w_up.shape[2]
    F = rd * N_DEV
    hv = g_norm_w.shape[1]
    alpha = (2 * L) ** 0.25
    my_x, my_y, my_c = lax.axis_index("x"), lax.axis_index("y"), lax.axis_index("c")
    dev_arr = jnp.reshape(4 * my_x + 2 * my_y + my_c, (1,)).astype(jnp.int32)

    o_a, o_b, o_o, o_d = D, D + rs, D + 2 * rs, D + 3 * rs
    taps = jnp.pad(w_dw, ((0, 0), (0, CONV_HALO - CONV_WIDTH), (0, 0))).reshape(L * CONV_HALO, w_dw.shape[2])
    taps_all = _all_gather_small(taps)
    w_dw_full = taps_all.transpose(1, 0, 2).reshape(L, CONV_HALO, D)

    started, gathered = {}, {}

    def start_gather(key, after):
        l, part = key
        rest = [w_a[l], w_b[l], w_o[l], w_down[l]]
        rows = dict(all=[w_in[l]] + rest, rest=rest)
        if part == "in":
            shards = [w_in[l].astype(ACT_DTYPE)]
        else:
            shards = [jnp.concatenate(rows[part], axis=0).astype(ACT_DTYPE),
                      jnp.swapaxes(w_up[l], 0, 1).astype(ACT_DTYPE)]
        started[key] = _gather_start("gather_start_%d_%s" % key, shards, [_place_own(s, dev_arr) for s in shards],
                                     after)
        return started[key][4]

    def finish_gather(key, after):
        send, recv, thru, zone, _ = started[key]
        _, zn = _gather_wait("gather_wait_%d_%s" % key, thru, zone, send, recv, after)
        gathered[key] = _gather_finish(zn)

    def w_in_of(l):
        return gathered[(l, "in") if l == 0 else (l, "all")][0]

    def rest_of(l):
        ga, gb = gathered[(l, "rest") if l == 0 else (l, "all")]
        base = 0 if l == 0 else D
        return dict(
            w_a=ga[:, base:base + rs, :].reshape(D, D),
            w_b=ga[:, base + rs:base + 2 * rs, :].reshape(D, D),
            w_o=ga[:, base + 2 * rs:base + 3 * rs, :].reshape(D, D),
            w_down=ga[:, base + 3 * rs:base + 3 * rs + rd, :].reshape(F, D),
            w_up=gb.reshape(2 * F, D))

    def weights(l):
        return dict(w_in=w_in_of(l), **rest_of(l))

    lbs = _lb_fwd(lb_logits)

    def params(l):
        return dict(b_in=b_in[l].reshape(N_SEC, 1, D), lbs=lbs[l].reshape(1, D), g_norm_w=g_norm_w[l].reshape(1, hv),
                    w_dw=w_dw_full[l], b_dw=b_dw[l].reshape(1, D), conv_ln_g=conv_ln_g[l].reshape(1, D),
                    conv_ln_b=conv_ln_b[l].reshape(1, D), b_b=b_b[l].reshape(1, D), ln1_g=ln1_g[l], ln1_b=ln1_b[l],
                    ln2_g=ln2_g[l], ln2_b=ln2_b[l])

    x2d = x.reshape(T, D)
    token = start_gather((0, "in"), taps_all)
    token = start_gather((0, "rest"), token)
    if L > 1:
        token = start_gather((1, "all"), token)
    xc, xc_bf = _ln_fwd("ln0", x2d, None, 1.0, ln0_g, ln0_b, dep=token)
    finish_gather((0, "in"), xc_bf)
    saved = []
    for l in range(L):
        if l == 0:
            def rest(after):
                finish_gather((0, "rest"), after)
                return rest_of(0)
            token = None
        else:
            rest = functools.partial(lambda after, l: rest_of(l), l=l)
            token = start_gather((l + 1, "all"), gathered[(l, "all")][0]) if l + 1 < L else None
        xc, xc_bf, s = _layer_fwd(xc, xc_bf, w_in_of(l), rest, params(l), alpha, dep=token)
        saved.append(s)
        if l + 1 < L:
            finish_gather((l + 1, "all"), xc_bf)

    c_arr = jnp.reshape(my_c, (1,)).astype(jnp.int32)
    chip = 2 * my_x + my_y
    dx, loss_row = _loss_fwd_bwd(xc, loss_target.reshape(T, D))
    small = [None] * L
    pending = None
    upd_big = {n: None for n in ("w_in", "w_a", "w_b", "w_o", "w_down", "w_up")}
    wmv = dict(w_in=(w_in, m_w_in, v_w_in), w_a=(w_a, m_w_a, v_w_a), w_b=(w_b, m_w_b, v_w_b),
               w_o=(w_o, m_w_o, v_w_o), w_down=(w_down, m_w_down, v_w_down),
               w_up=tuple(jnp.swapaxes(a, 1, 2) for a in (w_up, m_w_up, v_w_up)))

    def send_buffer(name, g):
        return g if name == "w_in" else g.reshape(N_DEV, wmv[name][0].shape[1], D)

    def update_layer(l, names, q, r2):
        pre = jnp.stack([chip, jnp.int32(l)]).astype(jnp.int32)
        for k, name in enumerate(names):
            w, m, v = wmv[name]
            r, C = w.shape[1], w.shape[2]
            tr = max(t for t in range(16, 513, 16) if r % t == 0)
            nb = r // tr
            specs = [pl.BlockSpec((None, tr, C), lambda i, s: (s[0], i, 0))]
            specs += [pl.BlockSpec((None, tr, C), functools.partial(lambda i, s, j: (j, i, 0), j=j)) for j in range(3)]
            upd_big[name] = _adamw(
                "adamw_" + name, w.reshape(L * r, C), m.reshape(L * r, C), v.reshape(L * r, C),
                [q[k], r2[k], r2[k], r2[k]], specs, tr, prefetch=pre, nsteps=nb,
                row_map=functools.partial(lambda i, s, nb: (s[1] * nb + i, 0), nb=nb), prev=upd_big[name])

    def pair_sums(names, grads):
        sends = [send_buffer(n, grads[n]) for n in names]
        return [_pair_add(p, r, c_arr) for p, r in zip(sends, _exchange_sibling(sends))]

    def start_reduce(tag, names, qs, after=None):
        s_send, s_recv, q_thru, zones, token = _exchange_chips_start("reduce_start_" + tag, qs, after)
        return (tag, names, (s_send, s_recv), list(q_thru), list(zones)), token

    def finish_reduce(l, handle, after):
        tag, names, sems, q, zones = handle
        r2 = _exchange_chips_wait("reduce_wait_" + tag, q, zones, sems[0], sems[1], after)
        update_layer(l, names, q, r2)

    names_all = tuple(upd_big)
    names_early = names_all[1:]
    token = None
    first_early = []
    for l in range(L - 1, -1, -1):
        if l > 0:
            dx, big, small[l] = _layer_bwd(dx, saved[l], weights(l), params(l), alpha, dep=token)
            if pending is not None:
                finish_reduce(l + 1, pending, dx)
            pending, token = start_reduce(str(l), names_all, pair_sums(names_all, big))
        else:
            def early(grads):
                if pending is not None:
                    finish_reduce(1, pending, grads["w_a"])
                handle, tok = start_reduce("0_rest", names_early, pair_sums(names_early, grads))
                first_early.append(handle)
                return tok
            dx, big, small[l] = _layer_bwd(dx, saved[l], weights(l), params(l), alpha, dep=token, early=early)
            q_in = pair_sums(("w_in",), big)
    dx0, _, dln0_g, dln0_b = _ln_bwd("ln0_bwd", x2d, dx, ln0_g)
    dlb_logits = _lb_bwd(lb_logits, jnp.concatenate([small[l]["lbs"] for l in range(L)], axis=0))

    small_l = {n: [small[l][n] for l in range(L)] for n in _SMALL if n != "lb_logits"}
    small_l["lb_logits"] = [dlb_logits[l] for l in range(L)]
    loss_pad = jnp.pad(loss_row, ((0, 0), (0, D - LANES)))
    part = jnp.concatenate([_pack_small(small_l, dln0_g, dln0_b, loss_pad, D, L)]
                           + [small[l]["w_dw"] for l in range(L)], axis=0)
    parts_all = _all_gather_small(part)
    n_small = part.shape[0] - L * CONV_HALO

    last, _ = start_reduce("0_in", ("w_in",), q_in, after=parts_all)


    inputs = dict(b_in=(b_in, m_b_in, v_b_in), lb_logits=(lb_logits, m_lb_logits, v_lb_logits),
                  g_norm_w=(g_norm_w, m_g_norm_w, v_g_norm_w), b_dw=(b_dw, m_b_dw, v_b_dw),
                  conv_ln_g=(conv_ln_g, m_conv_ln_g, v_conv_ln_g), conv_ln_b=(conv_ln_b, m_conv_ln_b, v_conv_ln_b),
                  b_b=(b_b, m_b_b, v_b_b), ln1_g=(ln1_g, m_ln1_g, v_ln1_g), ln1_b=(ln1_b, m_ln1_b, v_ln1_b),
                  ln2_g=(ln2_g, m_ln2_g, v_ln2_g), ln2_b=(ln2_b, m_ln2_b, v_ln2_b))
    zero_row = jnp.zeros((1, D), F32)
    packed = [_pack_small({n: [inputs[n][i][l] for l in range(L)] for n in _SMALL},
                          (ln0_g, m_ln0_g, v_ln0_g)[i], (ln0_b, m_ln0_b, v_ln0_b)[i], zero_row, D, L)
              for i in range(3)]
    small_specs = [pl.BlockSpec((None, n_small, D), functools.partial(lambda i, d: (d, 0, 0), d=d))
                   for d in range(N_DEV)]
    s_out = _adamw("adamw_small", packed[0], packed[1], packed[2], [parts_all] * N_DEV, small_specs, n_small)
    s_g, n_rows = _unpack_small(s_out[0], D, L, hv)
    s_d, _ = _unpack_small(s_out[1], D, L, hv)
    s_m, _ = _unpack_small(s_out[2], D, L, hv)
    s_v, _ = _unpack_small(s_out[3], D, L, hv)
    loss = s_out[0][n_rows, 0]

    cw = w_dw.shape[2]
    dev = 4 * my_x + 2 * my_y + my_c
    tap_parts = lax.dynamic_slice_in_dim(parts_all[:, n_small:, :], dev * cw, cw, axis=2)
    tap_specs = [pl.BlockSpec((None, L * CONV_HALO, cw), functools.partial(lambda i, d: (d, 0, 0), d=d))
                 for d in range(N_DEV)]
    pad_t = lambda a: jnp.pad(a, ((0, 0), (0, CONV_HALO - CONV_WIDTH), (0, 0))).reshape(L * CONV_HALO, cw)
    t_out = _adamw("adamw_taps", pad_t(w_dw), pad_t(m_w_dw), pad_t(v_w_dw), [tap_parts] * N_DEV, tap_specs,
                   L * CONV_HALO)
    finish_reduce(0, first_early[0], t_out[0])
    finish_reduce(0, last, upd_big["w_up"][0])
    upd ={n: [o.reshape(wmv[n][0].shape) for o in outs] for n, outs in upd_big.items()}
    upd["w_up"] = [jnp.swapaxes(o, 1, 2) for o in upd["w_up"]]
    upd["w_dw"] = [o.reshape(L, CONV_HALO, cw)[:, :CONV_WIDTH, :] for o in t_out]

    order = ["ln0_g", "ln0_b", "w_in", "b_in", "lb_logits", "g_norm_w", "w_a", "w_dw", "b_dw", "conv_ln_g",
             "conv_ln_b", "w_b", "b_b", "w_o", "ln1_g", "ln1_b", "w_up", "w_down", "ln2_g", "ln2_b"]
    small_sets = (s_g, s_d, s_m, s_v)
    outs = [loss, dx0.reshape(x.shape)]
    for i in range(4):
        for n in order:
            outs.append(upd[n][i] if n in upd else small_sets[i][n])
    return tuple(outs)
```

```python
import functools

import jax
import jax.numpy as jnp
from jax import lax
from jax.experimental import pallas as pl
from jax.experimental.pallas import tpu as pltpu

F32 = jnp.float32
MXU_DTYPE = jnp.bfloat16
ACT_DTYPE = jnp.bfloat16

LANES = 128
SUB = 8
N_DEV = 8
N_SEC = 8
CONV_WIDTH = 31
CONV_HALO = 32
HG_C = 16
LN_EPS = 1e-5
RMS_EPS = 1e-6
F_MIN = 1e-30
LOG2E = 1.4426950408889634
ADAM_LR = 0.001
ADAM_B1 = 0.9
ADAM_B2 = 0.999
ADAM_EPS = 1e-08
ADAM_WD = 0.01
ADAM_STEP = 10
VMEM_LIMIT = 56 * 1024 * 1024
MESH = pl.DeviceIdType.MESH

_NN = (((1,), (0,)), ((), ()))
_NT = (((1,), (1,)), ((), ()))
_TN = (((0,), (0,)), ((), ()))


_ANY = pl.BlockSpec(memory_space=pl.ANY)
_HBM = pl.BlockSpec(memory_space=pltpu.HBM)
_SEM = pl.BlockSpec(memory_space=pltpu.SEMAPHORE)
_EFFECT = pltpu.SideEffectType.DATAFLOW_SIDE_EFFECTING


def _cp(*sem):
    return pltpu.CompilerParams(dimension_semantics=tuple(sem), vmem_limit_bytes=VMEM_LIMIT)


def _pick(n, cands):
    for c in cands:
        if c <= n and n % c == 0:
            return c
    return n


def _silu(x):
    return x * jax.nn.sigmoid(x)


def _dsilu(x):
    s = jax.nn.sigmoid(x)
    return s * (1.0 + x * (1.0 - s))


def _matmul(name, a, b, *, dims, grid, a_spec, b_spec, out_shape, out_spec, acc_shape, nk,
            bias=None, bias_spec=None, add=None, add_spec=None, add_scale=1.0, dep=None):
    has_bias, has_add = bias is not None, add is not None
    kaxis = len(grid) - 1

    def body(*refs):
        a_ref, b_ref = refs[0], refs[1]
        pos = 2
        bias_ref = add_ref = None
        if has_bias:
            bias_ref = refs[pos]
            pos += 1
        if has_add:
            add_ref = refs[pos]
            pos += 1
        if dep is not None:
            pos += 1
        o_ref = refs[pos]
        acc_ref = refs[pos + 1] if nk > 1 else None

        part = lax.dot_general(a_ref[...].astype(MXU_DTYPE), b_ref[...].astype(MXU_DTYPE), dims,
                               preferred_element_type=F32)

        def finish(r):
            if has_bias:
                r = r + bias_ref[...]
            if has_add:
                r = r + add_scale * add_ref[...]
            o_ref[...] = r.astype(o_ref.dtype)

        if nk == 1:
            finish(part)
        else:
            k = pl.program_id(kaxis)

            @pl.when(k == 0)
            def _():
                acc_ref[...] = part

            @pl.when(k > 0)
            def _():
                acc_ref[...] += part

            @pl.when(k == nk - 1)
            def _():
                finish(acc_ref[...])

    ins, specs = [a, b], [a_spec, b_spec]
    if has_bias:
        ins.append(bias)
        specs.append(bias_spec)
    if has_add:
        ins.append(add)
        specs.append(add_spec)
    if dep is not None:
        ins.append(dep)
        specs.append(_ANY)
    sem =("parallel",) * (len(grid) - 1) + ("arbitrary",) if nk > 1 else ("parallel",) * len(grid)
    return pl.pallas_call(
        body, name=name, grid=grid, in_specs=specs, out_specs=out_spec, out_shape=out_shape,
        scratch_shapes=[pltpu.VMEM(acc_shape, F32)] if nk > 1 else [],
        compiler_params=_cp(*sem))(*ins)


def _mm_rows(name, a, b, dims, n_out, out_dtype, bias=None):
    M, K = a.shape
    tm = _pick(M, (512, 256, 128, 64, 32, 16))
    return _matmul(
        name, a, b, dims=dims, grid=(M // tm,),
        a_spec=pl.BlockSpec((tm, K), lambda i: (i, 0)),
        b_spec=pl.BlockSpec(b.shape, lambda i: (0, 0)),
        out_shape=jax.ShapeDtypeStruct((M, n_out), out_dtype),
        out_spec=pl.BlockSpec((tm, n_out), lambda i: (i, 0)),
        acc_shape=(tm, n_out), nk=1,
        bias=bias, bias_spec=None if bias is None else pl.BlockSpec((1, n_out), lambda i: (0, 0)))


def _mm_nn(name, a, b, out_dtype, bias=None):
    return _mm_rows(name, a, b, _NN, b.shape[1], out_dtype, bias)


def _mm_nt(name, a, b, out_dtype):
    return _mm_rows(name, a, b, _NT, b.shape[0], out_dtype)


def _mm_tn(name, a, b, out_dtype):
    K, M = a.shape
    N = b.shape[1]
    tm = _pick(M, (256, 128))
    return _matmul(
        name, a, b, dims=_TN, grid=(M // tm,),
        a_spec=pl.BlockSpec((K, tm), lambda i: (0, i)),
        b_spec=pl.BlockSpec((K, N), lambda i: (0, 0)),
        out_shape=jax.ShapeDtypeStruct((M, N), out_dtype),
        out_spec=pl.BlockSpec((tm, N), lambda i: (i, 0)),
        acc_shape=(tm, N), nk=1)


def _branch_dw(pairs, rs):
    T, D = pairs[0][0].shape
    nslot = max(1, LANES // rs)
    tm = nslot * rs
    nk = len(pairs)

    def body(*refs):
        o_ref = refs[-1]
        k = pl.program_id(0)
        for kk in range(nk):
            @pl.when(k == kk)
            def _():
                r = lax.dot_general(refs[2 * kk][...].astype(MXU_DTYPE), refs[2 * kk + 1][...].astype(MXU_DTYPE),
                                    _TN, preferred_element_type=F32)
                o_ref[...] = r.astype(o_ref.dtype).reshape(nslot, rs, D)

    in_specs, ins = [], []
    for kk, (a, b) in enumerate(pairs):
        in_specs.append(pl.BlockSpec((T, tm), functools.partial(lambda k, i, kk: (0, jnp.where(k == kk, i, 0)), kk=kk)))
        in_specs.append(pl.BlockSpec((T, D), lambda k, i: (0, 0)))
        ins += [a, b]
    return pl.pallas_call(
        body, name="branch_dw", grid=(nk, N_DEV // nslot), in_specs=in_specs,
        out_specs=pl.BlockSpec((nslot, rs, D), lambda k, i: (i, k, 0)),
        out_shape=jax.ShapeDtypeStruct((N_DEV, nk * rs, D), ACT_DTYPE),
        compiler_params=_cp("arbitrary", "arbitrary"))(*ins)


def _proj_in(x_bf, w_in, b_in, dep=None):
    T, D = x_bf.shape
    tn = _pick(D, (512, 256, 128))
    return _matmul(
        "proj_in", x_bf, w_in, dims=_NN, grid=(N_SEC, D // tn),
        a_spec=pl.BlockSpec((T, D), lambda s, j: (0, 0)),
        b_spec=pl.BlockSpec((None, D, tn), lambda s, j: (s, 0, j)),
        out_shape=jax.ShapeDtypeStruct((N_SEC, T, D), F32),
        out_spec=pl.BlockSpec((None, T, tn), lambda s, j: (s, 0, j)),
        acc_shape=(T, tn), nk=1,
        bias=b_in, bias_spec=pl.BlockSpec((None, 1, tn), lambda s, j: (s, 0, j)), dep=dep)


def _proj_in_dx(dh, w_in, add, add_scale):
    _, T, D = dh.shape
    tn = _pick(D, (512, 256, 128))
    return _matmul(
        "proj_in_dx", dh, w_in, dims=_NT, grid=(D // tn, N_SEC),
        a_spec=pl.BlockSpec((None, T, D), lambda j, s: (s, 0, 0)),
        b_spec=pl.BlockSpec((None, tn, D), lambda j, s: (s, j, 0)),
        out_shape=jax.ShapeDtypeStruct((T, D), F32),
        out_spec=pl.BlockSpec((T, tn), lambda j, s: (0, j)),
        acc_shape=(T, tn), nk=N_SEC,
        add=add, add_spec=pl.BlockSpec((T, tn), lambda j, s: (0, j)), add_scale=add_scale)


def _proj_in_dw(x_bf, dh):
    _, T, D = dh.shape
    tn = _pick(D, (512, 256, 128))

    def body(x_ref, dh_ref, dw_ref, db_ref):
        dhv = dh_ref[...]
        dw_ref[...] = lax.dot_general(x_ref[...].astype(MXU_DTYPE), dhv.astype(MXU_DTYPE), _TN,
                                      preferred_element_type=F32).astype(dw_ref.dtype)
        db_ref[...] = jnp.sum(dhv.astype(F32), axis=0, keepdims=True)

    return pl.pallas_call(
        body, name="proj_in_dw", grid=(N_SEC, D // tn),
        in_specs=[pl.BlockSpec((T, D), lambda s, j: (0, 0)), pl.BlockSpec((None, T, tn), lambda s, j: (s, 0, j))],
        out_specs=[pl.BlockSpec((None, D, tn), lambda s, j: (s, 0, j)),
                   pl.BlockSpec((None, 1, tn), lambda s, j: (s, 0, j))],
        out_shape=[jax.ShapeDtypeStruct((N_SEC, D, D), ACT_DTYPE), jax.ShapeDtypeStruct((N_SEC, 1, D), F32)],
        compiler_params=_cp("parallel", "parallel"))(x_bf, dh)


def _ffn_up(x_bf, w_up_t):
    T, D = x_bf.shape
    F = w_up_t.shape[0] // 2
    tn = _pick(F, (256, 128))
    nb = F // tn
    return _matmul(
        "ffn_up", x_bf, w_up_t, dims=_NT, grid=(2, nb),
        a_spec=pl.BlockSpec((T, D), lambda p, j: (0, 0)),
        b_spec=pl.BlockSpec((tn, D), lambda p, j: (p * nb + j, 0)),
        out_shape=jax.ShapeDtypeStruct((2, T, F), F32),
        out_spec=pl.BlockSpec((None, T, tn), lambda p, j: (p, 0, j)),
        acc_shape=(T, tn), nk=1)


def _ffn_up_dx(dup, w_up_t, add, add_scale):
    _, T, F = dup.shape
    D = w_up_t.shape[1]
    tn = _pick(D, (512, 256, 128))
    tk = _pick(F, (1408, 256, 128))
    nb = F // tk
    return _matmul(
        "ffn_up_dx", dup, w_up_t, dims=_NN, grid=(D // tn, 2 * nb),
        a_spec=pl.BlockSpec((None, T, tk), lambda j, k: (k // nb, 0, k % nb)),
        b_spec=pl.BlockSpec((tk, tn), lambda j, k: (k, j)),
        out_shape=jax.ShapeDtypeStruct((T, D), F32),
        out_spec=pl.BlockSpec((T, tn), lambda j, k: (0, j)),
        acc_shape=(T, tn), nk=2 * nb,
        add=add, add_spec=pl.BlockSpec((T, tn), lambda j, k: (0, j)), add_scale=add_scale)


def _ffn_up_dw(x_bf, dup):
    _, T, F = dup.shape
    D = x_bf.shape[1]
    tm = _pick(F, (1408, 256, 128))
    nb = F // tm
    return _matmul(
        "ffn_up_dw", dup, x_bf, dims=_TN, grid=(2, nb),
        a_spec=pl.BlockSpec((None, T, tm), lambda p, j: (p, 0, j)),
        b_spec=pl.BlockSpec((T, D), lambda p, j: (0, 0)),
        out_shape=jax.ShapeDtypeStruct((2 * F, D), ACT_DTYPE),
        out_spec=pl.BlockSpec((tm, D), lambda p, j: (p * nb + j, 0)),
        acc_shape=(tm, D), nk=1)


def _ln_fwd(name, a, res, alpha, g, b, dep=None):
    T, D = a.shape
    tr = _pick(T, (256, 128, 64, 32, 16))
    has_res = res is not None

    def body(*refs):
        if has_res:
            a_ref, r_ref, g_ref, b_ref = refs[:4]
            y_ref, yb_ref, z_ref = refs[-3:]
            z = alpha * a_ref[...] + r_ref[...]
            z_ref[...] = z
        else:
            a_ref, g_ref, b_ref = refs[:3]
            y_ref, yb_ref = refs[-2:]
            z = a_ref[...]
        mu = jnp.mean(z, axis=-1, keepdims=True)
        zc = z - mu
        var = jnp.mean(zc * zc, axis=-1, keepdims=True)
        y = zc * lax.rsqrt(var + LN_EPS) * g_ref[...] + b_ref[...]
        y_ref[...] = y
        yb_ref[...] = y.astype(ACT_DTYPE)

    row = pl.BlockSpec((tr, D), lambda i: (i, 0))
    vec = pl.BlockSpec((1, D), lambda i: (0, 0))
    ins = [a] + ([res] if has_res else []) + [g.reshape(1, D), b.reshape(1, D)]
    in_specs = [row] + ([row] if has_res else []) + [vec, vec]
    if dep is not None:
        ins.append(dep)
        in_specs.append(_ANY)
    out_shape = [jax.ShapeDtypeStruct((T, D), F32), jax.ShapeDtypeStruct((T, D), ACT_DTYPE)]
    if has_res:
        out_shape.append(jax.ShapeDtypeStruct((T, D), F32))
    return pl.pallas_call(
        body, name=name, grid=(T // tr,), in_specs=in_specs,
        out_specs=[row] * len(out_shape), out_shape=out_shape, compiler_params=_cp("parallel"))(*ins)


def _ln_bwd(name, z, dy, g, dep=None):
    T, D = z.shape
    tr = _pick(T, (256, 128, 64, 32, 16))

    def body(z_ref, dy_ref, g_ref, *rest):
        dz_ref, dzb_ref, dg_ref, db_ref = rest[-4:]

        @pl.when(pl.program_id(0) == 0)
        def _():
            dg_ref[...] = jnp.zeros_like(dg_ref)
            db_ref[...] = jnp.zeros_like(db_ref)

        zv = z_ref[...]
        dy_ = dy_ref[...]
        mu = jnp.mean(zv, axis=-1, keepdims=True)
        zc = zv - mu
        rstd = lax.rsqrt(jnp.mean(zc * zc, axis=-1, keepdims=True) + LN_EPS)
        xhat = zc * rstd
        dxh = dy_ * g_ref[...]
        dz = rstd * (dxh - jnp.mean(dxh, axis=-1, keepdims=True)
                     - xhat * jnp.mean(dxh * xhat, axis=-1, keepdims=True))
        dz_ref[...] = dz
        dzb_ref[...] = dz.astype(ACT_DTYPE)
        dg_ref[...] += jnp.sum(dy_ * xhat, axis=0, keepdims=True)
        db_ref[...] += jnp.sum(dy_, axis=0, keepdims=True)

    row = pl.BlockSpec((tr, D), lambda i: (i, 0))
    vec = pl.BlockSpec((1, D), lambda i: (0, 0))
    ins, in_specs = [z, dy, g.reshape(1, D)], [row, row, vec]
    if dep is not None:
        ins.append(dep)
        in_specs.append(_ANY)
    return pl.pallas_call(
        body, name=name, grid=(T // tr,), in_specs=in_specs, out_specs=[row, row, vec, vec],
        out_shape=[jax.ShapeDtypeStruct((T, D), F32), jax.ShapeDtypeStruct((T, D), ACT_DTYPE),
                   jax.ShapeDtypeStruct((1, D), F32), jax.ShapeDtypeStruct((1, D), F32)],
        compiler_params=_cp("arbitrary"))(*ins)


def _loss_fwd_bwd(y, target):
    T, D = y.shape
    tr = _pick(T, (256, 128, 64, 32, 16))

    def body(y_ref, t_ref, dy_ref, l_ref):
        @pl.when(pl.program_id(0) == 0)
        def _():
            l_ref[...] = jnp.zeros_like(l_ref)

        e = y_ref[...] - t_ref[...]
        dy_ref[...] = e * (1.0 / D)
        row = jnp.sum(e * e, axis=-1, keepdims=True) * (1.0 / D)
        l_ref[...] += 0.5 * jnp.sum(row, axis=0, keepdims=True)

    rowspec = pl.BlockSpec((tr, D), lambda i: (i, 0))
    return pl.pallas_call(
        body, name="loss", grid=(T // tr,), in_specs=[rowspec, rowspec],
        out_specs=[rowspec, pl.BlockSpec((1, LANES), lambda i: (0, 0))],
        out_shape=[jax.ShapeDtypeStruct((T, D), F32), jax.ShapeDtypeStruct((1, LANES), F32)],
        compiler_params=_cp("arbitrary"))(y, target)


def _gate_fwd(y_h, y_c, h):
    T, D = y_h.shape
    tr = _pick(T, (256, 128, 64, 32, 16))

    def body(yh_ref, yc_ref, gh_ref, gc_ref, m_ref):
        m = jax.nn.sigmoid(gh_ref[...]) * yh_ref[...] + jax.nn.sigmoid(gc_ref[...]) * yc_ref[...]
        m_ref[...] = m.astype(ACT_DTYPE)

    row = pl.BlockSpec((tr, D), lambda i: (i, 0))
    return pl.pallas_call(
        body, name="gate_fwd", grid=(T // tr,),
        in_specs=[row, row, pl.BlockSpec((None, tr, D), lambda i: (6, i, 0)),
                  pl.BlockSpec((None, tr, D), lambda i: (7, i, 0))],
        out_specs=row, out_shape=jax.ShapeDtypeStruct((T, D), ACT_DTYPE),
        compiler_params=_cp("parallel"))(y_h, y_c, h, h)


def _layer_norm_rows(z, g, b):
    mu = jnp.mean(z, axis=-1, keepdims=True)
    zc = z - mu
    var = jnp.mean(zc * zc, axis=-1, keepdims=True)
    return zc * lax.rsqrt(var + LN_EPS) * g + b


def _mixer_out(y_hg, y_cv, h, xin, w_a, w_b, b_b, w_o, alpha, ln_g, ln_b):
    T, D = xin.shape
    tm = _pick(T, (256, 128, 64, 32, 16))

    def body(yhg_ref, ycv_ref, gh_ref, gc_ref, x_ref, wa_ref, wb_ref, bb_ref, wo_ref, g_ref, b_ref,
             yh_ref, yc_ref, m_ref, x1_ref, x1b_ref, z_ref):
        y_h = jnp.dot(yhg_ref[...].astype(MXU_DTYPE), wa_ref[...].astype(MXU_DTYPE), preferred_element_type=F32)
        y_c = jnp.dot(ycv_ref[...].astype(MXU_DTYPE), wb_ref[...].astype(MXU_DTYPE),
                      preferred_element_type=F32) + bb_ref[...]
        yh_ref[...] = y_h
        yc_ref[...] = y_c
        merged = (jax.nn.sigmoid(gh_ref[...]) * y_h + jax.nn.sigmoid(gc_ref[...]) * y_c).astype(ACT_DTYPE)
        m_ref[...] = merged
        z = alpha * x_ref[...] + jnp.dot(merged.astype(MXU_DTYPE), wo_ref[...].astype(MXU_DTYPE),
                                         preferred_element_type=F32)
        z_ref[...] = z
        x1 = _layer_norm_rows(z, g_ref[...], b_ref[...])
        x1_ref[...] = x1
        x1b_ref[...] = x1.astype(ACT_DTYPE)

    row = pl.BlockSpec((tm, D), lambda i: (i, 0))
    mat = pl.BlockSpec((D, D), lambda i: (0, 0))
    vec = pl.BlockSpec((1, D), lambda i: (0, 0))
    f32, act = jax.ShapeDtypeStruct((T, D), F32), jax.ShapeDtypeStruct((T, D), ACT_DTYPE)
    return pl.pallas_call(
        body, name="mixer_out", grid=(T // tm,),
        in_specs=[row, row, pl.BlockSpec((None, tm, D), lambda i: (6, i, 0)),
                  pl.BlockSpec((None, tm, D), lambda i: (7, i, 0)), row, mat, mat, vec, mat, vec, vec],
        out_specs=[row] * 6, out_shape=[f32, f32, act, f32, act, f32],
        compiler_params=_cp("parallel"))(y_hg, y_cv, h, h, xin, w_a, w_b, b_b, w_o, ln_g.reshape(1, D),
                                         ln_b.reshape(1, D))


def _ffn_up_swiglu(x_bf, w_up_t):
    T, D = x_bf.shape
    F = w_up_t.shape[0] // 2
    tn = _pick(F, (256, 128))
    nb = F // tn

    def body(x_ref, wg_ref, wv_ref, up_ref, act_ref):
        xv = x_ref[...].astype(MXU_DTYPE)
        g = lax.dot_general(xv, wg_ref[...].astype(MXU_DTYPE), _NT, preferred_element_type=F32)
        v = lax.dot_general(xv, wv_ref[...].astype(MXU_DTYPE), _NT, preferred_element_type=F32)
        up_ref[0] = g
        up_ref[1] = v
        act_ref[...] = (_silu(g) * v).astype(ACT_DTYPE)

    return pl.pallas_call(
        body, name="ffn_up", grid=(nb,),
        in_specs=[pl.BlockSpec((T, D), lambda j: (0, 0)), pl.BlockSpec((tn, D), lambda j: (j, 0)),
                  pl.BlockSpec((tn, D), lambda j: (nb + j, 0))],
        out_specs=[pl.BlockSpec((2, T, tn), lambda j: (0, 0, j)), pl.BlockSpec((T, tn), lambda j: (0, j))],
        out_shape=[jax.ShapeDtypeStruct((2, T, F), F32), jax.ShapeDtypeStruct((T, F), ACT_DTYPE)],
        compiler_params=_cp("parallel"))(x_bf, w_up_t, w_up_t)


def _ffn_down_ln2(act, w_down, x1, alpha, ln_g, ln_b):
    T, D = x1.shape
    F = act.shape[1]
    tm = _pick(T, (256, 128, 64, 32, 16))

    def body(a_ref, w_ref, x_ref, g_ref, b_ref, x2_ref, x2b_ref, z_ref):
        z = alpha * x_ref[...] + jnp.dot(a_ref[...].astype(MXU_DTYPE), w_ref[...].astype(MXU_DTYPE),
                                         preferred_element_type=F32)
        z_ref[...] = z
        x2 = _layer_norm_rows(z, g_ref[...], b_ref[...])
        x2_ref[...] = x2
        x2b_ref[...] = x2.astype(ACT_DTYPE)

    row = pl.BlockSpec((tm, D), lambda i: (i, 0))
    vec = pl.BlockSpec((1, D), lambda i: (0, 0))
    f32, actt = jax.ShapeDtypeStruct((T, D), F32), jax.ShapeDtypeStruct((T, D), ACT_DTYPE)
    return pl.pallas_call(
        body, name="ffn_down", grid=(T // tm,),
        in_specs=[pl.BlockSpec((tm, F), lambda i: (i, 0)), pl.BlockSpec((F, D), lambda i: (0, 0)), row, vec, vec],
        out_specs=[row] * 3, out_shape=[f32, actt, f32],
        compiler_params=_cp("parallel"))(act, w_down, x1, ln_g.reshape(1, D), ln_b.reshape(1, D))


def _ln2_ffn_down_bwd(z, dy, ln_g, w_down, up, dep=None):
    T, D = z.shape
    F = w_down.shape[0]
    tm = _pick(T, (256, 128, 64, 32, 16))

    def body(z_ref, dy_ref, g_ref, w_ref, up_ref, *rest):
        dz_ref, dzb_ref, dup_ref, dg_ref, db_ref = rest[-5:]

        @pl.when(pl.program_id(0) == 0)
        def _():
            dg_ref[...] = jnp.zeros_like(dg_ref)
            db_ref[...] = jnp.zeros_like(db_ref)

        zv = z_ref[...]
        dy_ = dy_ref[...]
        mu = jnp.mean(zv, axis=-1, keepdims=True)
        zc = zv - mu
        rstd = lax.rsqrt(jnp.mean(zc * zc, axis=-1, keepdims=True) + LN_EPS)
        xhat = zc * rstd
        dxh = dy_ * g_ref[...]
        dz = rstd * (dxh - jnp.mean(dxh, axis=-1, keepdims=True)
                     - xhat * jnp.mean(dxh * xhat, axis=-1, keepdims=True))
        dz_ref[...] = dz
        dzb = dz.astype(ACT_DTYPE)
        dzb_ref[...] = dzb
        dg_ref[...] += jnp.sum(dy_ * xhat, axis=0, keepdims=True)
        db_ref[...] += jnp.sum(dy_, axis=0, keepdims=True)
        da = lax.dot_general(dzb.astype(MXU_DTYPE), w_ref[...].astype(MXU_DTYPE), _NT, preferred_element_type=F32)
        ug = up_ref[0]
        dup_ref[0] = (da * up_ref[1] * _dsilu(ug)).astype(ACT_DTYPE)
        dup_ref[1] = (da * _silu(ug)).astype(ACT_DTYPE)

    row = pl.BlockSpec((tm, D), lambda i: (i, 0))
    vec = pl.BlockSpec((1, D), lambda i: (0, 0))
    blk = pl.BlockSpec((2, tm, F), lambda i: (0, i, 0))
    ins = [z, dy, ln_g.reshape(1, D), w_down, up]
    in_specs = [row, row, vec, pl.BlockSpec((F, D), lambda i: (0, 0)), blk]
    if dep is not None:
        ins.append(dep)
        in_specs.append(_ANY)
    v32 = jax.ShapeDtypeStruct((1, D), F32)
    return pl.pallas_call(
        body, name="ln2_ffn_down_bwd", grid=(T // tm,), in_specs=in_specs,
        out_specs=[row, row, blk, vec, vec],
        out_shape=[jax.ShapeDtypeStruct((T, D), F32), jax.ShapeDtypeStruct((T, D), ACT_DTYPE),
                   jax.ShapeDtypeStruct((2, T, F), ACT_DTYPE), v32, v32],
        compiler_params=_cp("arbitrary"))(*ins)


def _mixer_out_bwd(z, dx1, y_h, y_c, h, w_a, w_b, w_o, ln_g):
    T, D = z.shape
    tm = _pick(T, (256, 128, 64, 32, 16))

    def body(z_ref, dx_ref, yh_ref, yc_ref, gh_ref, gc_ref, wa_ref, wb_ref, wo_ref, g_ref,
             dz_ref, dzb_ref, dyh_ref, dyc_ref, dyhg_ref, dycv_ref, dh_ref, dg_ref, db_ref, dbb_ref):
        @pl.when(pl.program_id(0) == 0)
        def _():
            dg_ref[...] = jnp.zeros_like(dg_ref)
            db_ref[...] = jnp.zeros_like(db_ref)
            dbb_ref[...] = jnp.zeros_like(dbb_ref)

        zv = z_ref[...]
        dy_ = dx_ref[...]
        mu = jnp.mean(zv, axis=-1, keepdims=True)
        zc = zv - mu
        rstd = lax.rsqrt(jnp.mean(zc * zc, axis=-1, keepdims=True) + LN_EPS)
        xhat = zc * rstd
        dxh = dy_ * g_ref[...]
        dz = rstd * (dxh - jnp.mean(dxh, axis=-1, keepdims=True)
                     - xhat * jnp.mean(dxh * xhat, axis=-1, keepdims=True))
        dz_ref[...] = dz
        dzb = dz.astype(ACT_DTYPE)
        dzb_ref[...] = dzb
        dg_ref[...] += jnp.sum(dy_ * xhat, axis=0, keepdims=True)
        db_ref[...] += jnp.sum(dy_, axis=0, keepdims=True)
        dm_ = lax.dot_general(dzb.astype(MXU_DTYPE), wo_ref[...].astype(MXU_DTYPE), _NT, preferred_element_type=F32)
        sh = jax.nn.sigmoid(gh_ref[...])
        sc = jax.nn.sigmoid(gc_ref[...])
        dyc = dm_ * sc
        dyh_b = (dm_ * sh).astype(ACT_DTYPE)
        dyc_b = dyc.astype(ACT_DTYPE)
        dyh_ref[...] = dyh_b
        dyc_ref[...] = dyc_b
        dbb_ref[...] += jnp.sum(dyc, axis=0, keepdims=True)
        dh_ref[0] = (dm_ * yh_ref[...] * sh * (1.0 - sh)).astype(ACT_DTYPE)
        dh_ref[1] = (dm_ * yc_ref[...] * sc * (1.0 - sc)).astype(ACT_DTYPE)
        dyhg_ref[...] = lax.dot_general(dyh_b.astype(MXU_DTYPE), wa_ref[...].astype(MXU_DTYPE), _NT,
                                        preferred_element_type=F32)
        dycv_ref[...] = lax.dot_general(dyc_b.astype(MXU_DTYPE), wb_ref[...].astype(MXU_DTYPE), _NT,
                                        preferred_element_type=F32)

    row = pl.BlockSpec((tm, D), lambda i: (i, 0))
    mat = pl.BlockSpec((D, D), lambda i: (0, 0))
    vec = pl.BlockSpec((1, D), lambda i: (0, 0))
    f32, act = jax.ShapeDtypeStruct((T, D), F32), jax.ShapeDtypeStruct((T, D), ACT_DTYPE)
    v32 = jax.ShapeDtypeStruct((1, D), F32)
    return pl.pallas_call(
        body, name="mixer_out_bwd", grid=(T // tm,),
        in_specs=[row, row, row, row, pl.BlockSpec((None, tm, D), lambda i: (6, i, 0)),
                  pl.BlockSpec((None, tm, D), lambda i: (7, i, 0)), mat, mat, mat, vec],
        out_specs=[row, row, row, row, row, row, pl.BlockSpec((2, tm, D), lambda i: (3, i, 0)), vec, vec, vec],
        out_shape=[f32, act, act, act, f32, f32, jax.ShapeDtypeStruct((N_SEC, T, D), ACT_DTYPE), v32, v32, v32],
        compiler_params=_cp("arbitrary"))(z, dx1, y_h, y_c, h, h, w_a, w_b, w_o, ln_g.reshape(1, D))


def _gate_bwd(dm, y_h, y_c, h):
    T, D = y_h.shape
    tr = _pick(T, (256, 128, 64, 32, 16))

    def body(dm_ref, yh_ref, yc_ref, gh_ref, gc_ref, dyh_ref, dyc_ref, dbb_ref, dh_ref):
        @pl.when(pl.program_id(0) == 0)
        def _():
            dbb_ref[...] = jnp.zeros_like(dbb_ref)

        dm_ = dm_ref[...]
        sh = jax.nn.sigmoid(gh_ref[...])
        sc = jax.nn.sigmoid(gc_ref[...])
        dyc = dm_ * sc
        dyh_ref[...] = (dm_ * sh).astype(ACT_DTYPE)
        dyc_ref[...] = dyc.astype(ACT_DTYPE)
        dbb_ref[...] += jnp.sum(dyc, axis=0, keepdims=True)
        dh_ref[0] = (dm_ * yh_ref[...] * sh * (1.0 - sh)).astype(ACT_DTYPE)
        dh_ref[1] = (dm_ * yc_ref[...] * sc * (1.0 - sc)).astype(ACT_DTYPE)

    row = pl.BlockSpec((tr, D), lambda i: (i, 0))
    return pl.pallas_call(
        body, name="gate_bwd", grid=(T // tr,),
        in_specs=[row, row, row, pl.BlockSpec((None, tr, D), lambda i: (6, i, 0)),
                  pl.BlockSpec((None, tr, D), lambda i: (7, i, 0))],
        out_specs=[row, row, pl.BlockSpec((1, D), lambda i: (0, 0)),
                   pl.BlockSpec((2, tr, D), lambda i: (3, i, 0))],
        out_shape=[jax.ShapeDtypeStruct((T, D), ACT_DTYPE), jax.ShapeDtypeStruct((T, D), ACT_DTYPE),
                   jax.ShapeDtypeStruct((1, D), F32), jax.ShapeDtypeStruct((N_SEC, T, D), ACT_DTYPE)],
        compiler_params=_cp("arbitrary"))(dm, y_h, y_c, h, h)


def _swiglu_fwd(up):
    _, T, F = up.shape
    tr = _pick(T, (128, 64, 32, 16))

    def body(up_ref, act_ref):
        act_ref[...] = (_silu(up_ref[0]) * up_ref[1]).astype(ACT_DTYPE)

    return pl.pallas_call(
        body, name="swiglu_fwd", grid=(T // tr,),
        in_specs=[pl.BlockSpec((2, tr, F), lambda i: (0, i, 0))],
        out_specs=pl.BlockSpec((tr, F), lambda i: (i, 0)),
        out_shape=jax.ShapeDtypeStruct((T, F), ACT_DTYPE), compiler_params=_cp("parallel"))(up)


def _swiglu_bwd(dact, up):
    _, T, F = up.shape
    tr = _pick(T, (128, 64, 32, 16))

    def body(da_ref, up_ref, dup_ref):
        da = da_ref[...]
        ug = up_ref[0]
        dup_ref[0] = (da * up_ref[1] * _dsilu(ug)).astype(ACT_DTYPE)
        dup_ref[1] = (da * _silu(ug)).astype(ACT_DTYPE)

    blk = pl.BlockSpec((2, tr, F), lambda i: (0, i, 0))
    return pl.pallas_call(
        body, name="swiglu_bwd", grid=(T // tr,),
        in_specs=[pl.BlockSpec((tr, F), lambda i: (i, 0)), blk], out_specs=blk,
        out_shape=jax.ShapeDtypeStruct((2, T, F), ACT_DTYPE), compiler_params=_cp("parallel"))(dact, up)


def _lb_softmax(x):
    L = x.shape[0]
    rows = [x[l:l + 1] for l in range(L)]
    m = rows[0]
    for r in rows[1:]:
        m = jnp.maximum(m, r)
    e = [jnp.exp(r - m) for r in rows]
    s = e[0]
    for r in e[1:]:
        s = s + r
    return [r / s for r in e]


def _lb_fwd(lb_logits):
    L, D = lb_logits.shape

    def body(x_ref, o_ref):
        p = _lb_softmax(x_ref[...])
        run = jnp.zeros_like(p[0])
        for l in range(L):
            if l > 0:
                run = run + p[l]
            o_ref[pl.ds(l, 1), :] = run

    return pl.pallas_call(body, name="lb_fwd", out_shape=jax.ShapeDtypeStruct((L, D), F32))(lb_logits)


def _lb_bwd(lb_logits, dlbs):
    L, D = lb_logits.shape

    def body(x_ref, d_ref, o_ref):
        p = _lb_softmax(x_ref[...])
        d = d_ref[...]
        dp = [jnp.zeros_like(p[0]) for _ in range(L)]
        run = jnp.zeros_like(p[0])
        for j in range(L - 1, 0, -1):
            run = run + d[j:j + 1]
            dp[j] = run
        dot = dp[0] * p[0]
        for j in range(1, L):
            dot = dot + dp[j] * p[j]
        for j in range(L):
            o_ref[pl.ds(j, 1), :] = p[j] * (dp[j] - dot)

    return pl.pallas_call(body, name="lb_bwd", out_shape=jax.ShapeDtypeStruct((L, D), F32))(lb_logits, dlbs)


def _blk_cumsum(x, c, reverse=False):
    n = x.shape[0]
    pos = lax.broadcasted_iota(jnp.int32, x.shape, 0) % c
    s = 1
    while s < c:
        if reverse:
            shifted = pltpu.roll(x, n - s, 0)
            x = x + jnp.where(pos + s < c, shifted, 0.0)
        else:
            shifted = pltpu.roll(x, s, 0)
            x = x + jnp.where(pos >= s, shifted, 0.0)
        s *= 2
    return x


def _hgrn_prologue(q_ref, f_ref, lb_ref):
    lbv = lb_ref[...]
    z = f_ref[...]
    sig = jax.nn.sigmoid(z)
    one_m = 1.0 - lbv
    f = lbv + one_m * sig
    logf = jnp.log(jnp.maximum(f, F_MIN))
    k = one_m * jax.nn.sigmoid(-z)
    q = _silu(q_ref[...])
    return q, k, logf, f, sig, one_m


def _hgrn_fwd(h, lbs_l, gw):
    _, T, D = h.shape
    nh = D // LANES
    c = HG_C
    Tt = _pick(T, (512, 256, 128, 64, 32, 16))
    nb = Tt // c
    ng = c // SUB

    def body(q_ref, f_ref, i_ref, g_ref, lb_ref, gw_ref, o_ref, y_ref, sall_ref,
             st_ref, G_s, q_s, k_s, W_s, R_s, dS_s, o_s):
        @pl.when(pl.program_id(1) == 0)
        def _():
            st_ref[...] = jnp.zeros_like(st_ref)

        q, k, logf, _, _, _ = _hgrn_prologue(q_ref, f_ref, lb_ref)
        G_s[...] = _blk_cumsum(logf, c) * LOG2E
        q_s[...] = q
        k_s[...] = k
        ones = jnp.ones((LANES, LANES), MXU_DTYPE)
        rowid = lax.broadcasted_iota(jnp.int32, (SUB, LANES), 0)
        zero = jnp.zeros((SUB, LANES), F32)
        for bi in range(nb):
            r0 = bi * c
            glast = G_s[pl.ds(r0 + c - 1, 1), :]
            kd = k_s[pl.ds(r0, c), :] * jnp.exp2(glast - G_s[pl.ds(r0, c), :])
            dS_s[bi] = lax.dot_general(i_ref[pl.ds(r0, c), :].astype(MXU_DTYPE), kd.astype(MXU_DTYPE), _TN,
                                       preferred_element_type=F32)
        st = st_ref[...]
        for bi in range(nb):
            sall_ref[bi] = st
            st = st * jnp.exp2(G_s[pl.ds(bi * c + c - 1, 1), :]) + dS_s[bi]
        st_ref[...] = st
        for bi in range(nb):
            r0 = bi * c
            qd = q_s[pl.ds(r0, c), :] * jnp.exp2(G_s[pl.ds(r0, c), :])
            o_s[pl.ds(r0, c), :] = lax.dot_general(qd.astype(MXU_DTYPE), sall_ref[bi].astype(MXU_DTYPE), _NT,
                                                   preferred_element_type=F32)
        for bi in range(nb):
            r0 = bi * c
            w0 = bi * c * c
            Gg = [G_s[pl.ds(r0 + gi * SUB, SUB), :] for gi in range(ng)]
            qg = [q_s[pl.ds(r0 + gi * SUB, SUB), :] for gi in range(ng)]
            for s in range(c):
                gs = G_s[pl.ds(r0 + s, 1), :]
                ks = k_s[pl.ds(r0 + s, 1), :]
                parts = []
                for gi in range(ng):
                    if gi < s // SUB:
                        parts.append(zero)
                        continue
                    e = jnp.exp2(jnp.minimum(Gg[gi] - gs, 0.0))
                    if gi == s // SUB:
                        e = jnp.where(rowid >= s - gi * SUB, e, 0.0)
                    parts.append(e * qg[gi] * ks)
                W_s[pl.ds(w0 + s * c, c), :] = jnp.concatenate(parts, axis=0).astype(MXU_DTYPE)
        R_s[...] = jnp.dot(W_s[...], ones, preferred_element_type=F32)
        for bi in range(nb):
            r0 = bi * c
            w0 = bi * c * c
            acc = [o_s[pl.ds(r0 + gi * SUB, SUB), :] for gi in range(ng)]
            for s in range(c):
                vs = i_ref[pl.ds(r0 + s, 1), :]
                for gi in range(s // SUB, ng):
                    acc[gi] = acc[gi] + R_s[pl.ds(w0 + s * c + gi * SUB, SUB), :] * vs
            o_s[pl.ds(r0, c), :] = jnp.concatenate(acc, axis=0)
        o = o_s[...]
        n = o * lax.rsqrt(jnp.mean(o * o, axis=-1, keepdims=True) + RMS_EPS)
        o_ref[...] = o
        y_ref[...] = (n * gw_ref[...] * _silu(g_ref[...])).astype(ACT_DTYPE)

    def sec(s):
        return pl.BlockSpec((None, Tt, LANES), lambda hd, i: (s, i, hd))

    col = pl.BlockSpec((Tt, LANES), lambda hd, i: (i, hd))
    return pl.pallas_call(
        body, name="hgrn_fwd", grid=(nh, T // Tt),
        in_specs=[sec(0), sec(1), sec(2), sec(3), pl.BlockSpec((1, LANES), lambda hd, i: (0, hd)),
                  pl.BlockSpec((1, LANES), lambda hd, i: (0, 0))],
        out_specs=[col, col, pl.BlockSpec((nb, None, LANES, LANES), lambda hd, i: (i, hd, 0, 0))],
        out_shape=[jax.ShapeDtypeStruct((T, D), F32), jax.ShapeDtypeStruct((T, D), ACT_DTYPE),
                   jax.ShapeDtypeStruct((T // c, nh, LANES, LANES), F32)],
        scratch_shapes=[pltpu.VMEM((LANES, LANES), F32), pltpu.VMEM((Tt, LANES), F32),
                        pltpu.VMEM((Tt, LANES), F32), pltpu.VMEM((Tt, LANES), F32),
                        pltpu.VMEM((nb * c * c, LANES), MXU_DTYPE), pltpu.VMEM((nb * c * c, LANES), F32),
                        pltpu.VMEM((nb, LANES, LANES), F32), pltpu.VMEM((Tt, LANES), F32)],
        compiler_params=_cp("parallel", "arbitrary"))(h, h, h, h, lbs_l, gw)


def _hgrn_bwd(h, lbs_l, gw, o_pre, st_all, dy, dh):
    _, T, D = h.shape
    nh = D // LANES
    c = HG_C
    Tt = _pick(T, (512, 256, 128, 64, 32, 16))
    nb = Tt // c
    ng = c // SUB
    nT = T // Tt

    def body(q_ref, f_ref, i_ref, g_ref, lb_ref, gw_ref, o_ref, sall_ref, dy_ref, dh_in_ref,
             dh_ref, dlb_ref, dgw_ref,
             dst_ref, G_s, q_s, k_s, do_s, E_s, WP_s, dq_s, dk_s, dv_s, dG_s,
             R_s, dS_s, dstA_s, dqd_s, dkd_s, dvi_s, da_s):
        del dh_in_ref
        hd, ti = pl.program_id(0), pl.program_id(1)

        @pl.when(ti == 0)
        def _():
            dst_ref[...] = jnp.zeros_like(dst_ref)
            dlb_ref[...] = jnp.zeros_like(dlb_ref)

        @pl.when((ti == 0) & (hd == 0))
        def _():
            dgw_ref[...] = jnp.zeros_like(dgw_ref)

        q, k, logf, f, sig, one_m = _hgrn_prologue(q_ref, f_ref, lb_ref)
        G_s[...] = _blk_cumsum(logf, c) * LOG2E
        q_s[...] = q
        k_s[...] = k

        o = o_ref[...]
        gr = g_ref[...]
        dy_ = dy_ref[...]
        rr = lax.rsqrt(jnp.mean(o * o, axis=-1, keepdims=True) + RMS_EPS)
        n = o * rr
        sg = _silu(gr)
        gwv = gw_ref[...]
        dh_ref[3] = (dy_ * n * gwv * _dsilu(gr)).astype(ACT_DTYPE)
        dgw_ref[...] += jnp.sum(dy_ * n * sg, axis=0, keepdims=True)
        dn = dy_ * gwv * sg
        do_s[...] = rr * (dn - n * jnp.mean(dn * n, axis=-1, keepdims=True))

        ones = jnp.ones((LANES, LANES), MXU_DTYPE)
        rowid = lax.broadcasted_iota(jnp.int32, (SUB, LANES), 0)
        rowid_c = lax.broadcasted_iota(jnp.int32, (c, LANES), 0)
        zero = jnp.zeros((SUB, LANES), F32)
        cc = c * c
        for bi in range(nb):
            r0 = bi * c
            qd = q_s[pl.ds(r0, c), :] * jnp.exp2(G_s[pl.ds(r0, c), :])
            dS_s[bi] = lax.dot_general(do_s[pl.ds(r0, c), :].astype(MXU_DTYPE), qd.astype(MXU_DTYPE), _TN,
                                       preferred_element_type=F32)
        dst = dst_ref[...]
        for bi in range(nb - 1, -1, -1):
            dstA_s[bi] = dst
            dst = dst * jnp.exp2(G_s[pl.ds(bi * c + c - 1, 1), :]) + dS_s[bi]
        dst_ref[...] = dst
        for bi in range(nb):
            r0 = bi * c
            glast = G_s[pl.ds(r0 + c - 1, 1), :]
            kd = k_s[pl.ds(r0, c), :] * jnp.exp2(glast - G_s[pl.ds(r0, c), :])
            st = sall_ref[bi]
            dstb = dstA_s[bi]
            dst_m = dstb.astype(MXU_DTYPE)
            dqd_s[pl.ds(r0, c), :] = lax.dot_general(do_s[pl.ds(r0, c), :].astype(MXU_DTYPE), st.astype(MXU_DTYPE),
                                                     _NN, preferred_element_type=F32)
            dkd_s[pl.ds(r0, c), :] = lax.dot_general(i_ref[pl.ds(r0, c), :].astype(MXU_DTYPE), dst_m, _NN,
                                                     preferred_element_type=F32)
            dvi_s[pl.ds(r0, c), :] = lax.dot_general(kd.astype(MXU_DTYPE), dst_m, _NT,
                                                     preferred_element_type=F32)
            da_s[pl.ds(bi * SUB, 1), :] = jnp.sum(dstb * st, axis=0, keepdims=True)
        for bi in range(nb):
            r0 = bi * c
            e0, w0 = bi * cc, bi * 2 * cc
            Gg = [G_s[pl.ds(r0 + gi * SUB, SUB), :] for gi in range(ng)]
            kg = [k_s[pl.ds(r0 + gi * SUB, SUB), :] for gi in range(ng)]
            vg = [i_ref[pl.ds(r0 + gi * SUB, SUB), :] for gi in range(ng)]
            for t in range(c):
                gt = G_s[pl.ds(r0 + t, 1), :]
                qt = q_s[pl.ds(r0 + t, 1), :]
                dot_ = do_s[pl.ds(r0 + t, 1), :]
                ep, wp, pp = [], [], []
                for gi in range(ng):
                    if gi > t // SUB:
                        ep.append(zero)
                        wp.append(zero)
                        pp.append(zero)
                        continue
                    e = jnp.exp2(jnp.minimum(gt - Gg[gi], 0.0))
                    if gi == t // SUB:
                        e = jnp.where(rowid <= t - gi * SUB, e, 0.0)
                    ep.append(e)
                    wp.append(e * kg[gi] * qt)
                    pp.append(vg[gi] * dot_)
                E_s[pl.ds(e0 + t * c, c), :] = jnp.concatenate(ep, axis=0)
                WP_s[pl.ds(w0 + t * c, c), :] = jnp.concatenate(wp, axis=0).astype(MXU_DTYPE)
                WP_s[pl.ds(w0 + cc + t * c, c), :] = jnp.concatenate(pp, axis=0).astype(MXU_DTYPE)
        R_s[...] = jnp.dot(WP_s[...], ones, preferred_element_type=F32)
        for bi in range(nb):
            r0 = bi * c
            e0, w0 = bi * cc, bi * 2 * cc
            kg = [k_s[pl.ds(r0 + gi * SUB, SUB), :] for gi in range(ng)]
            dk_g = [zero] * ng
            dv_g = [zero] * ng
            dq_g = [zero] * ng
            for t in range(c):
                qt = q_s[pl.ds(r0 + t, 1), :]
                dot_ = do_s[pl.ds(r0 + t, 1), :]
                tot = None
                for gi in range(t // SUB + 1):
                    lo = t * c + gi * SUB
                    dae = R_s[pl.ds(w0 + cc + lo, SUB), :] * E_s[pl.ds(e0 + lo, SUB), :]
                    z = dae * kg[gi]
                    tot = z if tot is None else tot + z
                    dk_g[gi] = dk_g[gi] + dae * qt
                    dv_g[gi] = dv_g[gi] + R_s[pl.ds(w0 + lo, SUB), :] * dot_
                gt_ = t // SUB
                dq_g[gt_] = jnp.where(rowid == t - gt_ * SUB, jnp.sum(tot, axis=0, keepdims=True), dq_g[gt_])
            dq_i = jnp.concatenate(dq_g, axis=0)
            dk_i = jnp.concatenate(dk_g, axis=0)
            dv_i = jnp.concatenate(dv_g, axis=0)
            Gb = G_s[pl.ds(r0, c), :]
            qb = q_s[pl.ds(r0, c), :]
            kb = k_s[pl.ds(r0, c), :]
            glast = G_s[pl.ds(r0 + c - 1, 1), :]
            eg = jnp.exp2(Gb)
            egl = jnp.exp2(glast - Gb)
            dqd = dqd_s[pl.ds(r0, c), :]
            dkd = dkd_s[pl.ds(r0, c), :]
            dq_s[pl.ds(r0, c), :] = dqd * eg + dq_i
            dk_s[pl.ds(r0, c), :] = dkd * egl + dk_i
            dv_s[pl.ds(r0, c), :] = dvi_s[pl.ds(r0, c), :] + dv_i
            dkdkd = dkd * kb * egl
            dG = dqd * qb * eg + qb * dq_i - kb * dk_i - dkdkd
            dglast = jnp.sum(dkdkd, axis=0, keepdims=True) + da_s[pl.ds(bi * SUB, 1), :] * jnp.exp2(glast)
            dG_s[pl.ds(r0, c), :] = dG + jnp.where(rowid_c == c - 1, dglast, 0.0)

        dlogf = _blk_cumsum(dG_s[...], c, reverse=True)
        df = jnp.where(f > F_MIN, dlogf / f, 0.0)
        dk = dk_s[...]
        dh_ref[0] = (dq_s[...] * _dsilu(q_ref[...])).astype(ACT_DTYPE)
        dh_ref[1] = ((df - dk) * one_m * sig * (1.0 - sig)).astype(ACT_DTYPE)
        dh_ref[2] = dv_s[...].astype(ACT_DTYPE)
        dlb_ref[...] += jnp.sum((df - dk) * (1.0 - sig), axis=0, keepdims=True)

    def sec(s):
        return pl.BlockSpec((None, Tt, LANES), lambda hd, i: (s, nT - 1 - i, hd))

    col = pl.BlockSpec((Tt, LANES), lambda hd, i: (nT - 1 - i, hd))
    tile = pltpu.VMEM((Tt, LANES), F32)
    return pl.pallas_call(
        body, name="hgrn_bwd", grid=(nh, nT),
        in_specs=[sec(0), sec(1), sec(2), sec(3), pl.BlockSpec((1, LANES), lambda hd, i: (0, hd)),
                  pl.BlockSpec((1, LANES), lambda hd, i: (0, 0)), col,
                  pl.BlockSpec((nb, None, LANES, LANES), lambda hd, i: (nT - 1 - i, hd, 0, 0)), col,
                  pl.BlockSpec(memory_space=pl.ANY)],
        out_specs=[pl.BlockSpec((4, Tt, LANES), lambda hd, i: (0, nT - 1 - i, hd)),
                   pl.BlockSpec((1, LANES), lambda hd, i: (0, hd)),
                   pl.BlockSpec((1, LANES), lambda hd, i: (0, 0))],
        out_shape=[jax.ShapeDtypeStruct(dh.shape, dh.dtype), jax.ShapeDtypeStruct((1, D), F32),
                   jax.ShapeDtypeStruct((1, LANES), F32)],
        scratch_shapes=[pltpu.VMEM((LANES, LANES), F32), tile, tile, tile, tile,
                        pltpu.VMEM((nb * c * c, LANES), F32), pltpu.VMEM((2 * nb * c * c, LANES), MXU_DTYPE),
                        tile, tile, tile, tile,
                        pltpu.VMEM((2 * nb * c * c, LANES), F32), pltpu.VMEM((nb, LANES, LANES), F32),
                        pltpu.VMEM((nb, LANES, LANES), F32), tile, tile, tile, pltpu.VMEM((nb * SUB, LANES), F32)],
        input_output_aliases={9: 0},
        compiler_params=_cp("arbitrary", "arbitrary"))(h, h, h, h, lbs_l, gw, o_pre, st_all, dy, dh)


def _shifted_copies(src, cs, dst, rows):
    for b in range(1, SUB):
        dst[b - 1] = src[pl.ds(b, rows + CONV_HALO - SUB), cs]


def _shifted(src, cs, copies, shift, rows):
    a8, b = divmod(shift, SUB)
    if b == 0:
        return src[pl.ds(shift, rows), cs]
    return copies[b - 1, pl.ds(a8 * SUB, rows), :]


def _conv_fwd(h, w_dw, b_dw, ln_g, ln_b):
    _, T, D = h.shape
    Tt = _pick(T, (256, 128, 64, 32))
    hb = Tt // CONV_HALO
    off = CONV_HALO - (CONV_WIDTH - 1)

    def body(a_ref, b_ref, ap_ref, bp_ref, w_ref, bd_ref, g_ref, be_ref, yc_ref, y_ref, U_s, Ub_s):
        first = pl.program_id(0) == 0
        up = ap_ref[...] * jax.nn.sigmoid(bp_ref[...])
        U_s[pl.ds(0, CONV_HALO), :] = jnp.where(first, 0.0, up)
        U_s[pl.ds(CONV_HALO, Tt), :] = a_ref[...] * jax.nn.sigmoid(b_ref[...])
        for cb in range(D // LANES):
            cs = pl.ds(cb * LANES, LANES)
            _shifted_copies(U_s, cs, Ub_s, Tt)
            acc = jnp.zeros((Tt, LANES), F32)
            for j in range(CONV_WIDTH):
                acc = acc + w_ref[pl.ds(j, 1), cs] * _shifted(U_s, cs, Ub_s, off + j, Tt)
            yc_ref[:, cs] = acc + bd_ref[:, cs]
        yc = yc_ref[...]
        mu = jnp.mean(yc, axis=-1, keepdims=True)
        zc = yc - mu
        var = jnp.mean(zc * zc, axis=-1, keepdims=True)
        ln = zc * lax.rsqrt(var + LN_EPS) * g_ref[...] + be_ref[...]
        y_ref[...] = _silu(ln).astype(ACT_DTYPE)

    def main(s):
        return pl.BlockSpec((None, Tt, D), lambda i: (s, i, 0))

    def prev(s):
        return pl.BlockSpec((None, CONV_HALO, D), lambda i: (s, jnp.maximum(i * hb - 1, 0), 0))

    row = pl.BlockSpec((Tt, D), lambda i: (i, 0))
    vec = pl.BlockSpec((1, D), lambda i: (0, 0))
    return pl.pallas_call(
        body, name="conv_fwd", grid=(T // Tt,),
        in_specs=[main(4), main(5), prev(4), prev(5), pl.BlockSpec((CONV_HALO, D), lambda i: (0, 0)),
                  vec, vec, vec],
        out_specs=[row, row],
        out_shape=[jax.ShapeDtypeStruct((T, D), F32), jax.ShapeDtypeStruct((T, D), ACT_DTYPE)],
        scratch_shapes=[pltpu.VMEM((CONV_HALO + Tt, D), F32),
                        pltpu.VMEM((SUB - 1, Tt + CONV_HALO - SUB, LANES), F32)],
        compiler_params=_cp("parallel"))(h, h, h, h, w_dw, b_dw, ln_g, ln_b)


def _conv_bwd(h, w_dw, ln_g, ln_b, yc, dy, dh):
    _, T, D = h.shape
    Tt = _pick(T, (256, 128, 64, 32))
    hb = Tt // CONV_HALO
    nT = T // Tt
    nhb = T // CONV_HALO
    off = CONV_HALO - (CONV_WIDTH - 1)

    def body(a_ref, b_ref, ap_ref, bp_ref, w_ref, g_ref, be_ref, yc_ref, ycn_ref, dy_ref, dyn_ref, dh_in_ref,
             dh_ref, dw_ref, dbd_ref, dg_ref, dbe_ref, U_s, DY_s, du_s, Ub_s, DYb_s):
        del dh_in_ref
        i = pl.program_id(0)

        @pl.when(i == 0)
        def _():
            dw_ref[...] = jnp.zeros_like(dw_ref)
            dbd_ref[...] = jnp.zeros_like(dbd_ref)
            dg_ref[...] = jnp.zeros_like(dg_ref)
            dbe_ref[...] = jnp.zeros_like(dbe_ref)

        gv = g_ref[...]
        bev = be_ref[...]

        def ln_silu_bwd(ycv, dyv):
            mu = jnp.mean(ycv, axis=-1, keepdims=True)
            zc = ycv - mu
            rstd = lax.rsqrt(jnp.mean(zc * zc, axis=-1, keepdims=True) + LN_EPS)
            xhat = zc * rstd
            dln = dyv * _dsilu(xhat * gv + bev)
            dxh = dln * gv
            dyc = rstd * (dxh - jnp.mean(dxh, axis=-1, keepdims=True)
                          - xhat * jnp.mean(dxh * xhat, axis=-1, keepdims=True))
            return dyc, dln, xhat

        dyc, dln, xhat = ln_silu_bwd(yc_ref[...], dy_ref[...])
        dg_ref[...] += jnp.sum(dln * xhat, axis=0, keepdims=True)
        dbe_ref[...] += jnp.sum(dln, axis=0, keepdims=True)
        dbd_ref[...] += jnp.sum(dyc, axis=0, keepdims=True)
        DY_s[pl.ds(0, Tt), :] = dyc
        dycn, _, _ = ln_silu_bwd(ycn_ref[...], dyn_ref[...])
        DY_s[pl.ds(Tt, CONV_HALO), :] = jnp.where(i == nT - 1, 0.0, dycn)

        sb = jax.nn.sigmoid(b_ref[...])
        av = a_ref[...]
        up = ap_ref[...] * jax.nn.sigmoid(bp_ref[...])
        U_s[pl.ds(0, CONV_HALO), :] = jnp.where(i == 0, 0.0, up)
        U_s[pl.ds(CONV_HALO, Tt), :] = av * sb

        for cb in range(D // LANES):
            cs = pl.ds(cb * LANES, LANES)
            _shifted_copies(U_s, cs, Ub_s, Tt)
            _shifted_copies(DY_s, cs, DYb_s, Tt)
            dyb = DY_s[pl.ds(0, Tt), cs]
            acc = jnp.zeros((Tt, LANES), F32)
            for j in range(CONV_WIDTH):
                acc = acc + w_ref[pl.ds(j, 1), cs] * _shifted(DY_s, cs, DYb_s, CONV_WIDTH - 1 - j, Tt)
                dw_ref[pl.ds(j, 1), cs] += jnp.sum(dyb * _shifted(U_s, cs, Ub_s, off + j, Tt), axis=0, keepdims=True)
            du_s[:, cs] = acc
        du = du_s[...]
        dh_ref[0] = (du * sb).astype(ACT_DTYPE)
        dh_ref[1] = (du * av * sb * (1.0 - sb)).astype(ACT_DTYPE)

    def main(s):
        return pl.BlockSpec((None, Tt, D), lambda i: (s, i, 0))

    def prev(s):
        return pl.BlockSpec((None, CONV_HALO, D), lambda i: (s, jnp.maximum(i * hb - 1, 0), 0))

    row = pl.BlockSpec((Tt, D), lambda i: (i, 0))
    nxt = pl.BlockSpec((CONV_HALO, D), lambda i: (jnp.minimum((i + 1) * hb, nhb - 1), 0))
    vec = pl.BlockSpec((1, D), lambda i: (0, 0))
    wspec = pl.BlockSpec((CONV_HALO, D), lambda i: (0, 0))
    return pl.pallas_call(
        body, name="conv_bwd", grid=(nT,),
        in_specs=[main(4), main(5), prev(4), prev(5), wspec, vec, vec, row, nxt, row, nxt,
                  pl.BlockSpec(memory_space=pl.ANY)],
        out_specs=[pl.BlockSpec((2, Tt, D), lambda i: (2, i, 0)), wspec, vec, vec, vec],
        out_shape=[jax.ShapeDtypeStruct(dh.shape, dh.dtype), jax.ShapeDtypeStruct((CONV_HALO, D), F32),
                   jax.ShapeDtypeStruct((1, D), F32), jax.ShapeDtypeStruct((1, D), F32),
                   jax.ShapeDtypeStruct((1, D), F32)],
        scratch_shapes=[pltpu.VMEM((CONV_HALO + Tt, D), F32), pltpu.VMEM((Tt + CONV_HALO, D), F32),
                        pltpu.VMEM((Tt, D), F32),
                        pltpu.VMEM((SUB - 1, Tt + CONV_HALO - SUB, LANES), F32),
                        pltpu.VMEM((SUB - 1, Tt + CONV_HALO - SUB, LANES), F32)],
        input_output_aliases={11: 0},
        compiler_params=_cp("arbitrary"))(h, h, h, h, w_dw, ln_g, ln_b, yc, yc, dy, dy, dh)


def _adamw(name, w, m, v, parts, part_specs, tr, prefetch=None, nsteps=None, row_map=None, prev=None):
    R, C = w.shape
    bc1 = 1.0 - ADAM_B1 ** ADAM_STEP
    bc2 = 1.0 - ADAM_B2 ** ADAM_STEP
    npart = len(parts)
    npre = 0 if prefetch is None else 1
    nprev = 0 if prev is None else 4

    def body(*refs):
        refs = refs[npre:]
        w_ref, m_ref, v_ref = refs[:3]
        p_refs = refs[3:3 + npart]
        g_ref, d_ref, mo_ref, vo_ref = refs[3 + npart + nprev:]
        g = p_refs[0][...].astype(F32)
        for p in p_refs[1:]:
            g = g + p[...].astype(F32)
        wv = w_ref[...]
        mn = ADAM_B1 * m_ref[...] + (1.0 - ADAM_B1) * g
        vn = ADAM_B2 * v_ref[...] + (1.0 - ADAM_B2) * (g * g)
        m_hat = mn / bc1
        v_hat = vn / bc2
        g_ref[...] = g
        d_ref[...] = -ADAM_LR * (m_hat / (jnp.sqrt(v_hat) + ADAM_EPS) + ADAM_WD * wv)
        mo_ref[...] = mn
        vo_ref[...] = vn

    if row_map is None:
        row_map = (lambda i: (i, 0)) if prefetch is None else (lambda i, s: (i, 0))
    row = pl.BlockSpec((tr, C), row_map)
    out = jax.ShapeDtypeStruct((R, C), F32)
    gs = pltpu.PrefetchScalarGridSpec(
        num_scalar_prefetch=npre, grid=(R // tr if nsteps is None else nsteps,),
        in_specs=[row, row, row] + list(part_specs) + [_ANY] * nprev, out_specs=[row] * 4)
    args = ([prefetch] if npre else []) + [w, m, v] + list(parts) + (list(prev) if nprev else [])
    first_prev = npre + 3 + npart
    return pl.pallas_call(body, name=name, grid_spec=gs, out_shape=[out] * 4,
                          input_output_aliases={first_prev + i: i for i in range(nprev)},
                          compiler_params=_cp("parallel"))(*args)


def _pair_add(p, r1, my_c):
    _, R, C = r1.shape
    tr = max(t for t in range(16, 1025, 16) if R % t == 0)

    def body(c_ref, p_ref, r_ref, q_ref):
        del c_ref
        q_ref[...] = (p_ref[...].astype(F32) + r_ref[...].astype(F32)).astype(q_ref.dtype)

    gs = pltpu.PrefetchScalarGridSpec(
        num_scalar_prefetch=1, grid=(4, R // tr),
        in_specs=[pl.BlockSpec((None, tr, C), lambda j, i, c: (2 * j + c[0], i, 0)),
                  pl.BlockSpec((None, tr, C), lambda j, i, c: (j, i, 0))],
        out_specs=pl.BlockSpec((None, tr, C), lambda j, i, c: (j, i, 0)))
    return pl.pallas_call(body, name="pair_add", grid_spec=gs, out_shape=jax.ShapeDtypeStruct(r1.shape, r1.dtype),
                          compiler_params=_cp("parallel", "parallel"))(my_c, p, r1)


def _place():
    x, y, c = lax.axis_index("x"), lax.axis_index("y"), lax.axis_index("c")
    chips = [(1 - x, y), (x, 1 - y), (1 - x, 1 - y)]
    return x, y, c, chips


def _hbm(a):
    return pltpu.with_memory_space_constraint(a, pltpu.HBM)


def _gather_targets():
    x, y, c, chips = _place()
    return 4 * x + 2 * y + c, [(x, y, 1 - c)] + [(*chip, c) for chip in chips]


def _gather_start(name, shards, zones, after=None):
    n = len(shards)
    lands = [_hbm(z) for z in zones]
    n_in = 2 * n + (0 if after is None else 1)

    def body(*refs):
        srcs, zones = refs[:n], refs[n:2 * n]
        send, recv, token = refs[n_in], refs[n_in + 1], refs[-1]
        mine, targets = _gather_targets()
        for a in range(n):
            for k, to in enumerate(targets):
                pltpu.make_async_remote_copy(
                    src_ref=srcs[a], dst_ref=zones[a].at[mine], send_sem=send.at[4 * a + k],
                    recv_sem=recv.at[4 * a + k], device_id=to, device_id_type=MESH).start()
        token[...] = jnp.zeros_like(token)

    sem = pltpu.SemaphoreType.DMA((4 * n,))
    out_shape = ([sem, sem] + [pltpu.HBM(s.shape, s.dtype) for s in shards]
                 + [pltpu.HBM(z.shape, z.dtype) for z in lands] + [jax.ShapeDtypeStruct((8, LANES), F32)])
    outs = pl.pallas_call(
        body, name=name, out_shape=out_shape, in_specs=[_HBM] * (2 * n) + ([] if after is None else [_ANY]),
        out_specs=[_SEM, _SEM] + [_HBM] * (2 * n) + [pl.BlockSpec(memory_space=pltpu.VMEM)],
        input_output_aliases={i: 2 + i for i in range(2 * n)},
        compiler_params=pltpu.CompilerParams(has_side_effects=_EFFECT))(
            *[_hbm(s) for s in shards], *lands, *([] if after is None else [after]))
    return outs[0], outs[1], list(outs[2:2 + n]), list(outs[2 + n:2 + 2 * n]), outs[-1]


def _gather_wait(name, shards, zones, send, recv, after):
    per = len(shards)

    def body(*refs):
        srcs, lz = refs[:per], refs[per:2 * per]
        send_s, recv_s = refs[2 * per], refs[2 * per + 1]
        mine, targets = _gather_targets()
        for a in range(per):
            for k, to in enumerate(targets):
                cp = pltpu.make_async_remote_copy(
                    src_ref=srcs[a], dst_ref=lz[a].at[mine], send_sem=send_s.at[4 * a + k],
                    recv_sem=recv_s.at[4 * a + k], device_id=to, device_id_type=MESH)
                cp.wait_send()
                cp.wait_recv()

    outs = pl.pallas_call(
        body, name=name, out_shape=[pltpu.HBM(s.shape, s.dtype) for s in shards + zones],
        in_specs=[_HBM] * (2 * per) + [_SEM, _SEM, _ANY], out_specs=[_HBM] * (2 * per),
        input_output_aliases={i: i for i in range(2 * per)},
        compiler_params=pltpu.CompilerParams(has_side_effects=_EFFECT))(*shards, *zones, send, recv, after)
    return outs[:per], outs[per:]


def _gather_finish(zones):
    n = len(zones)

    def body(*refs):
        lz = refs[n:2 * n]
        send_sems, recv_sems = refs[2 * n:]
        x, y, c, chips = _place()

        def fwd(a, j, pc):
            cx, cy = chips[j]
            blk = lz[a].at[4 * cx + 2 * cy + pc]
            return pltpu.make_async_remote_copy(
                src_ref=blk, dst_ref=blk, send_sem=send_sems.at[3 * a + j], recv_sem=recv_sems.at[3 * a + j],
                device_id=(x, y, 1 - c), device_id_type=MESH)

        sends = [fwd(a, j, c) for a in range(n) for j in range(3)]
        for cp in sends:
            cp.start()
        for a in range(n):
            for j in range(3):
                fwd(a, j, 1 - c).wait_recv()
        for cp in sends:
            cp.wait_send()

    return pl.pallas_call(
        body, name="gather_finish", out_shape=[jax.ShapeDtypeStruct(z.shape, z.dtype) for z in zones],
        in_specs=[_ANY] * n, out_specs=[_ANY] * n, input_output_aliases={a: a for a in range(n)},
        scratch_shapes=[pltpu.SemaphoreType.DMA((3 * n,)), pltpu.SemaphoreType.DMA((3 * n,))])(*zones)


def _place_own(shard, dev):
    R, C = shard.shape
    tr = max(t for t in range(16, 1025, 16) if R % t == 0)

    def body(d_ref, s_ref, z_ref):
        del d_ref
        z_ref[...] = s_ref[...]

    gs = pltpu.PrefetchScalarGridSpec(
        num_scalar_prefetch=1, grid=(R // tr,), in_specs=[pl.BlockSpec((tr, C), lambda i, d: (i, 0))],
        out_specs=pl.BlockSpec((None, tr, C), lambda i, d: (d[0], i, 0)))
    return pl.pallas_call(body, name="place_own", grid_spec=gs,
                          out_shape=jax.ShapeDtypeStruct((N_DEV, R, C), shard.dtype),
                          compiler_params=_cp("parallel"))(dev, shard)


def _exchange_sibling(bufs):
    n_arr = len(bufs)

    def body(*refs):
        srcs, outs = refs[:n_arr], refs[n_arr:2 * n_arr]
        send_sems, recv_sems = refs[2 * n_arr:]
        x, y, c, _ = _place()
        copies = []
        for n in range(n_arr):
            for j in range(4):
                copies.append(pltpu.make_async_remote_copy(
                    src_ref=srcs[n].at[2 * j + 1 - c], dst_ref=outs[n].at[j],
                    send_sem=send_sems.at[4 * n + j], recv_sem=recv_sems.at[4 * n + j],
                    device_id=(x, y, 1 - c), device_id_type=MESH))
        for cp in copies:
            cp.start()
        for cp in copies:
            cp.wait()

    return pl.pallas_call(
        body, name="exchange_sibling",
        out_shape=[jax.ShapeDtypeStruct((4,) + b.shape[1:], b.dtype) for b in bufs],
        in_specs=[_ANY] * n_arr, out_specs=[_ANY] * n_arr,
        scratch_shapes=[pltpu.SemaphoreType.DMA((4 * n_arr,)), pltpu.SemaphoreType.DMA((4 * n_arr,))])(*bufs)


def _chip_copies(srcs, zones, send, recv):
    _, _, c, chips = _place()
    return [pltpu.make_async_remote_copy(
        src_ref=srcs[n].at[2 * cx + cy], dst_ref=zones[n].at[k], send_sem=send.at[3 * n + k],
        recv_sem=recv.at[3 * n + k], device_id=(cx, cy, c), device_id_type=MESH)
        for n in range(len(srcs)) for k, (cx, cy) in enumerate(chips)]


def _exchange_chips_start(name, bufs, after=None):
    n = len(bufs)
    n_in = 2 * n + (0 if after is None else 1)
    lands = [_hbm(lax.empty((3,) + b.shape[1:], b.dtype)) for b in bufs]

    def body(*refs):
        srcs, zones = refs[:n], refs[n:2 * n]
        send, recv, token = refs[n_in], refs[n_in + 1], refs[-1]
        for cp in _chip_copies(srcs, zones, send, recv):
            cp.start()
        token[...] = jnp.zeros_like(token)

    sem = pltpu.SemaphoreType.DMA((3 * n,))
    outs = pl.pallas_call(
        body, name=name,
        out_shape=[sem, sem] + [pltpu.HBM(b.shape, b.dtype) for b in bufs]
        + [pltpu.HBM(z.shape, z.dtype) for z in lands] + [jax.ShapeDtypeStruct((8, LANES), F32)],
        in_specs=[_HBM] * (2 * n) + ([] if after is None else [_ANY]),
        out_specs=[_SEM, _SEM] + [_HBM] * (2 * n) + [pl.BlockSpec(memory_space=pltpu.VMEM)],
        input_output_aliases={i: 2 + i for i in range(2 * n)},
        compiler_params=pltpu.CompilerParams(has_side_effects=_EFFECT))(
            *[_hbm(b) for b in bufs], *lands, *([] if after is None else [after]))
    return outs[0], outs[1], outs[2:2 + n], outs[2 + n:2 + 2 * n], outs[-1]


def _exchange_chips_wait(name, bufs, zones, send, recv, after):
    n = len(bufs)

    def body(*refs):
        for cp in _chip_copies(refs[:n], refs[n:2 * n], refs[2 * n], refs[2 * n + 1]):
            cp.wait_send()
            cp.wait_recv()

    outs = pl.pallas_call(
        body, name=name, out_shape=[pltpu.HBM(a.shape, a.dtype) for a in list(bufs) + list(zones)],
        in_specs=[_HBM] * (2 * n) + [_SEM, _SEM, _ANY], out_specs=[_HBM] * (2 * n),
        input_output_aliases={i: i for i in range(2 * n)},
        compiler_params=pltpu.CompilerParams(has_side_effects=_EFFECT))(*bufs, *zones, send, recv, after)
    return outs[n:]


def _all_gather_small(part):
    def body(src, out, send_sems, recv_sems, local_sem):
        x, y, c, _ = _place()
        mine = pltpu.make_async_copy(src, out.at[4 * x + 2 * y + c], local_sem)
        mine.start()
        copies = []
        for r in range(1, N_DEV):
            dx, dy, dc = (r >> 2) & 1, (r >> 1) & 1, r & 1
            peer = (1 - x if dx else x, 1 - y if dy else y, 1 - c if dc else c)
            copies.append(pltpu.make_async_remote_copy(
                src_ref=src, dst_ref=out.at[4 * x + 2 * y + c],
                send_sem=send_sems.at[r - 1], recv_sem=recv_sems.at[r - 1],
                device_id=peer, device_id_type=MESH))
        for cp in copies:
            cp.start()
        for cp in copies:
            cp.wait()
        mine.wait()

    return pl.pallas_call(
        body, name="all_gather_small",
        out_shape=jax.ShapeDtypeStruct((N_DEV,) + part.shape, part.dtype),
        in_specs=[_ANY], out_specs=_ANY,
        scratch_shapes=[pltpu.SemaphoreType.DMA((N_DEV - 1,)), pltpu.SemaphoreType.DMA((N_DEV - 1,)),
                        pltpu.SemaphoreType.DMA])(part)


def _layer_fwd(xin, xin_bf, w_in, rest, P, alpha, dep=None):
    h = _proj_in(xin_bf, w_in, P["b_in"], dep=dep)
    o_pre, y_hg, st_all = _hgrn_fwd(h, P["lbs"], P["g_norm_w"])
    yc_pre, y_cv = _conv_fwd(h, P["w_dw"], P["b_dw"], P["conv_ln_g"], P["conv_ln_b"])
    W = rest(y_cv)
    y_h, y_c, merged, x1, x1_bf, z1 = _mixer_out(y_hg, y_cv, h, xin, W["w_a"], W["w_b"], P["b_b"], W["w_o"], alpha,
                                                 P["ln1_g"], P["ln1_b"])
    up, act = _ffn_up_swiglu(x1_bf, W["w_up"])
    x2, x2_bf, z2 = _ffn_down_ln2(act, W["w_down"], x1, alpha, P["ln2_g"], P["ln2_b"])
    saved = dict(xin_bf=xin_bf, h=h, o_pre=o_pre, y_hg=y_hg, st_all=st_all, yc_pre=yc_pre, y_cv=y_cv,
                 y_h=y_h, y_c=y_c, merged=merged, z1=z1, x1_bf=x1_bf, up=up, act=act, z2=z2)
    return x2, x2_bf, saved


def _layer_bwd(dx2, S, W, P, alpha, dep=None, early=None):
    dz2, dz2_bf, dup, dln2_g, dln2_b = _ln2_ffn_down_bwd(S["z2"], dx2, P["ln2_g"], W["w_down"], S["up"], dep=dep)
    dw_down = _mm_tn("ffn_down_dw", S["act"], dz2_bf, ACT_DTYPE)
    dx1 = _ffn_up_dx(dup, W["w_up"], dz2, alpha)
    dw_up = _ffn_up_dw(S["x1_bf"], dup)
    dz1, dz1_bf, dy_h, dy_c, dy_hg, dy_cv, dh, dln1_g, dln1_b, db_b = _mixer_out_bwd(
        S["z1"], dx1, S["y_h"], S["y_c"], S["h"], W["w_a"], W["w_b"], W["w_o"], P["ln1_g"])
    dw_abo = _branch_dw([(S["y_hg"], dy_h), (S["y_cv"], dy_c), (S["merged"], dz1_bf)], dz1.shape[1] // N_DEV)
    if early is not None:
        token = early(dict(w_abo=dw_abo, w_down=dw_down, w_up=dw_up))
        dy_cv = dy_cv + token[0, 0]
    dh, dw_dw, db_dw, dcln_g, dcln_b = _conv_bwd(S["h"], P["w_dw"], P["conv_ln_g"], P["conv_ln_b"],
                                                 S["yc_pre"], dy_cv, dh)
    dh, dlbs, dgw = _hgrn_bwd(S["h"], P["lbs"], P["g_norm_w"], S["o_pre"], S["st_all"], dy_hg, dh)
    dxin = _proj_in_dx(dh, W["w_in"], dz1, alpha)
    dw_in, db_in = _proj_in_dw(S["xin_bf"], dh)
    big = dict(w_in=dw_in, w_abo=dw_abo, w_down=dw_down, w_up=dw_up)
    small = dict(b_in=db_in, lbs=dlbs, g_norm_w=dgw, w_dw=dw_dw, b_dw=db_dw, conv_ln_g=dcln_g,
                 conv_ln_b=dcln_b, b_b=db_b, ln1_g=dln1_g, ln1_b=dln1_b, ln2_g=dln2_g, ln2_b=dln2_b)
    return dxin, big, small


_SMALL = ("b_in", "lb_logits", "g_norm_w", "b_dw", "conv_ln_g", "conv_ln_b", "b_b", "ln1_g", "ln1_b", "ln2_g",
          "ln2_b")


def _pack_small(per_layer, ln0_g, ln0_b, extra_row, D, L):
    rows = []
    for l in range(L):
        for n in _SMALL:
            a = per_layer[n][l]
            if n == "b_in":
                rows.append(a.reshape(N_SEC, D))
            elif n == "g_norm_w":
                rows.append(jnp.pad(a.reshape(1, -1), ((0, 0), (0, D - a.size))))
            else:
                rows.append(a.reshape(1, D))
    rows += [ln0_g.reshape(1, D), ln0_b.reshape(1, D), extra_row]
    buf = jnp.concatenate(rows, axis=0)
    pad = (-buf.shape[0]) % 8
    return jnp.pad(buf, ((0, pad), (0, 0)))


def _unpack_small(buf, D, L, hv):
    out = {n: [] for n in _SMALL}
    r = 0
    for l in range(L):
        for n in _SMALL:
            if n == "b_in":
                out[n].append(buf[r:r + N_SEC].reshape(N_SEC * D))
                r += N_SEC
            elif n == "g_norm_w":
                out[n].append(buf[r, :hv])
                r += 1
            else:
                out[n].append(buf[r])
                r += 1
    res = {n: jnp.stack(v) for n, v in out.items()}
    res["ln0_g"] = buf[r]
    res["ln0_b"] = buf[r + 1]
    return res, r + 2


def kernel(x, ln0_g, ln0_b, w_in, b_in, lb_logits, g_norm_w, w_a, w_dw, b_dw, conv_ln_g, conv_ln_b, w_b, b_b, w_o, ln1_g, ln1_b, w_up, w_down, ln2_g, ln2_b, loss_target, m_ln0_g, m_ln0_b, m_w_in, m_b_in, m_lb_logits, m_g_norm_w, m_w_a, m_w_dw, m_b_dw, m_conv_ln_g, m_conv_ln_b, m_w_b, m_b_b, m_w_o, m_ln1_g, m_ln1_b, m_w_up, m_w_down, m_ln2_g, m_ln2_b, v_ln0_g, v_ln0_b, v_w_in, v_b_in, v_lb_logits, v_g_norm_w, v_w_a, v_w_dw, v_b_dw, v_conv_ln_g, v_conv_ln_b, v_w_b, v_b_b, v_w_o, v_ln1_g, v_ln1_b, v_w_up, v_w_down, v_ln2_g, v_ln2_b):
    L, D = w_in.shape[0], w_in.shape[1]
    T = x.shape[0] * x.shape[1]
    Dn = w_in.shape[2]
    rs = w_a.shape[1]
    rd = w_down.shape[1]
    cu = w_up.shape[2]
    F = rd * N_DEV
    hv = g_norm_w.shape[1]
    alpha = (2 * L) ** 0.25
    my_x, my_y, my_c = lax.axis_index("x"), lax.axis_index("y"), lax.axis_index("c")
    dev_arr = jnp.reshape(4 * my_x + 2 * my_y + my_c, (1,)).astype(jnp.int32)

    o_a, o_b, o_o, o_d = D, D + rs, D + 2 * rs, D + 3 * rs
    taps = jnp.pad(w_dw, ((0, 0), (0, CONV_HALO - CONV_WIDTH), (0, 0))).reshape(L * CONV_HALO, w_dw.shape[2])
    taps_all = _all_gather_small(taps)
    w_dw_full = taps_all.transpose(1, 0, 2).reshape(L, CONV_HALO, D)

    started, gathered = {}, {}

    def start_gather(key, after):
        l, part = key
        rest = [w_a[l], w_b[l], w_o[l], w_down[l]]
        rows = dict(all=[w_in[l]] + rest, rest=rest)
        if part == "in":
            shards = [w_in[l].astype(ACT_DTYPE)]
        else:
            shards = [jnp.concatenate(rows[part], axis=0).astype(ACT_DTYPE),
                      jnp.swapaxes(w_up[l], 0, 1).astype(ACT_DTYPE)]
        started[key] = _gather_start("gather_start_%d_%s" % key, shards, [_place_own(s, dev_arr) for s in shards],
                                     after)
        return started[key][4]

    def finish_gather(key, after):
        send, recv, thru, zone, _ = started[key]
        _, zn = _gather_wait("gather_wait_%d_%s" % key, thru, zone, send, recv, after)
        gathered[key] = _gather_finish(zn)

    def w_in_of(l):
        return gathered[(l, "in") if l == 0 else (l, "all")][0]

    def rest_of(l):
        ga, gb = gathered[(l, "rest") if l == 0 else (l, "all")]
        base = 0 if l == 0 else D
        return dict(
            w_a=ga[:, base:base + rs, :].reshape(D, D),
            w_b=ga[:, base + rs:base + 2 * rs, :].reshape(D, D),
            w_o=ga[:, base + 2 * rs:base + 3 * rs, :].reshape(D, D),
            w_down=ga[:, base + 3 * rs:base + 3 * rs + rd, :].reshape(F, D),
            w_up=gb.reshape(2 * F, D))

    def weights(l):
        return dict(w_in=w_in_of(l), **rest_of(l))

    lbs = _lb_fwd(lb_logits)

    def params(l):
        return dict(b_in=b_in[l].reshape(N_SEC, 1, D), lbs=lbs[l].reshape(1, D), g_norm_w=g_norm_w[l].reshape(1, hv),
                    w_dw=w_dw_full[l], b_dw=b_dw[l].reshape(1, D), conv_ln_g=conv_ln_g[l].reshape(1, D),
                    conv_ln_b=conv_ln_b[l].reshape(1, D), b_b=b_b[l].reshape(1, D), ln1_g=ln1_g[l], ln1_b=ln1_b[l],
                    ln2_g=ln2_g[l], ln2_b=ln2_b[l])

    x2d = x.reshape(T, D)
    token = start_gather((0, "in"), taps_all)
    token = start_gather((0, "rest"), token)
    if L > 1:
        token = start_gather((1, "all"), token)
    xc, xc_bf = _ln_fwd("ln0", x2d, None, 1.0, ln0_g, ln0_b, dep=token)
    finish_gather((0, "in"), xc_bf)
    saved = []
    for l in range(L):
        if l == 0:
            def rest(after):
                finish_gather((0, "rest"), after)
                return rest_of(0)
            token = None
        else:
            rest = functools.partial(lambda after, l: rest_of(l), l=l)
            token = start_gather((l + 1, "all"), gathered[(l, "all")][0]) if l + 1 < L else None
        xc, xc_bf, s = _layer_fwd(xc, xc_bf, w_in_of(l), rest, params(l), alpha, dep=token)
        saved.append(s)
        if l + 1 < L:
            finish_gather((l + 1, "all"), xc_bf)

    c_arr = jnp.reshape(my_c, (1,)).astype(jnp.int32)
    chip = 2 * my_x + my_y
    dx, loss_row = _loss_fwd_bwd(xc, loss_target.reshape(T, D))
    small = [None] * L
    pending = None
    upd_big = {n: None for n in ("w_in", "w_a", "w_b", "w_o", "w_down", "w_up")}
    wmv = dict(w_in=(w_in, m_w_in, v_w_in), w_a=(w_a, m_w_a, v_w_a), w_b=(w_b, m_w_b, v_w_b),
               w_o=(w_o, m_w_o, v_w_o), w_down=(w_down, m_w_down, v_w_down),
               w_up=tuple(jnp.swapaxes(a, 1, 2) for a in (w_up, m_w_up, v_w_up)))

    held = dict(w_in=("w_in",), w_abo=("w_a", "w_b", "w_o"), w_down=("w_down",), w_up=("w_up",))

    def send_buffer(buf, g):
        return g if g.ndim == 3 else g.reshape(N_DEV, wmv[held[buf][0]][0].shape[1], D)

    def update_layer(l, bufs, q, r2):
        pre = jnp.stack([chip, jnp.int32(l)]).astype(jnp.int32)
        for k, buf in enumerate(bufs):
            for pos, name in enumerate(held[buf]):
                w, m, v = wmv[name]
                r, C = w.shape[1], w.shape[2]
                tr = max(t for t in range(16, 513, 16) if r % t == 0)
                nb = r // tr
                b0 = pos * nb
                specs = [pl.BlockSpec((None, tr, C), functools.partial(lambda i, s, b0: (s[0], b0 + i, 0), b0=b0))]
                specs += [pl.BlockSpec((None, tr, C), functools.partial(lambda i, s, j, b0: (j, b0 + i, 0), j=j, b0=b0))
                          for j in range(3)]
                upd_big[name] = _adamw(
                    "adamw_" + name, w.reshape(L * r, C), m.reshape(L * r, C), v.reshape(L * r, C),
                    [q[k], r2[k], r2[k], r2[k]], specs, tr, prefetch=pre, nsteps=nb,
                    row_map=functools.partial(lambda i, s, nb: (s[1] * nb + i, 0), nb=nb), prev=upd_big[name])

    def pair_sums(bufs, grads):
        sends = [send_buffer(n, grads[n]) for n in bufs]
        return [_pair_add(p, r, c_arr) for p, r in zip(sends, _exchange_sibling(sends))]

    def start_reduce(tag, names, qs, after=None):
        s_send, s_recv, q_thru, zones, token = _exchange_chips_start("reduce_start_" + tag, qs, after)
        return (tag, names, (s_send, s_recv), list(q_thru), list(zones)), token

    def finish_reduce(l, handle, after):
        tag, names, sems, q, zones = handle
        r2 = _exchange_chips_wait("reduce_wait_" + tag, q, zones, sems[0], sems[1], after)
        update_layer(l, names, q, r2)

    names_all = tuple(held)
    names_early = names_all[1:]
    token = None
    first_early = []
    for l in range(L - 1, -1, -1):
        if l > 0:
            dx, big, small[l] = _layer_bwd(dx, saved[l], weights(l), params(l), alpha, dep=token)
            if pending is not None:
                finish_reduce(l + 1, pending, dx)
            pending, token = start_reduce(str(l), names_all, pair_sums(names_all, big))
        else:
            def early(grads):
                if pending is not None:
                    finish_reduce(1, pending, grads["w_abo"])
                handle, tok = start_reduce("0_rest", names_early, pair_sums(names_early, grads))
                first_early.append(handle)
                return tok
            dx, big, small[l] = _layer_bwd(dx, saved[l], weights(l), params(l), alpha, dep=token, early=early)
            q_in = pair_sums(("w_in",), big)
    dx0, _, dln0_g, dln0_b = _ln_bwd("ln0_bwd", x2d, dx, ln0_g)
    dlb_logits = _lb_bwd(lb_logits, jnp.concatenate([small[l]["lbs"] for l in range(L)], axis=0))

    small_l = {n: [small[l][n] for l in range(L)] for n in _SMALL if n != "lb_logits"}
    small_l["lb_logits"] = [dlb_logits[l] for l in range(L)]
    loss_pad = jnp.pad(loss_row, ((0, 0), (0, D - LANES)))
    part = jnp.concatenate([_pack_small(small_l, dln0_g, dln0_b, loss_pad, D, L)]
                           + [small[l]["w_dw"] for l in range(L)], axis=0)
    parts_all = _all_gather_small(part)
    n_small = part.shape[0] - L * CONV_HALO

    last, _ = start_reduce("0_in", ("w_in",), q_in, after=parts_all)


    inputs = dict(b_in=(b_in, m_b_in, v_b_in), lb_logits=(lb_logits, m_lb_logits, v_lb_logits),
                  g_norm_w=(g_norm_w, m_g_norm_w, v_g_norm_w), b_dw=(b_dw, m_b_dw, v_b_dw),
                  conv_ln_g=(conv_ln_g, m_conv_ln_g, v_conv_ln_g), conv_ln_b=(conv_ln_b, m_conv_ln_b, v_conv_ln_b),
                  b_b=(b_b, m_b_b, v_b_b), ln1_g=(ln1_g, m_ln1_g, v_ln1_g), ln1_b=(ln1_b, m_ln1_b, v_ln1_b),
                  ln2_g=(ln2_g, m_ln2_g, v_ln2_g), ln2_b=(ln2_b, m_ln2_b, v_ln2_b))
    zero_row = jnp.zeros((1, D), F32)
    packed = [_pack_small({n: [inputs[n][i][l] for l in range(L)] for n in _SMALL},
                          (ln0_g, m_ln0_g, v_ln0_g)[i], (ln0_b, m_ln0_b, v_ln0_b)[i], zero_row, D, L)
              for i in range(3)]
    small_specs = [pl.BlockSpec((None, n_small, D), functools.partial(lambda i, d: (d, 0, 0), d=d))
                   for d in range(N_DEV)]
    s_out = _adamw("adamw_small", packed[0], packed[1], packed[2], [parts_all] * N_DEV, small_specs, n_small)
    s_g, n_rows = _unpack_small(s_out[0], D, L, hv)
    s_d, _ = _unpack_small(s_out[1], D, L, hv)
    s_m, _ = _unpack_small(s_out[2], D, L, hv)
    s_v, _ = _unpack_small(s_out[3], D, L, hv)
    loss = s_out[0][n_rows, 0]

    cw = w_dw.shape[2]
    dev = 4 * my_x + 2 * my_y + my_c
    tap_parts = lax.dynamic_slice_in_dim(parts_all[:, n_small:, :], dev * cw, cw, axis=2)
    tap_specs = [pl.BlockSpec((None, L * CONV_HALO, cw), functools.partial(lambda i, d: (d, 0, 0), d=d))
                 for d in range(N_DEV)]
    pad_t = lambda a: jnp.pad(a, ((0, 0), (0, CONV_HALO - CONV_WIDTH), (0, 0))).reshape(L * CONV_HALO, cw)
    t_out = _adamw("adamw_taps", pad_t(w_dw), pad_t(m_w_dw), pad_t(v_w_dw), [tap_parts] * N_DEV, tap_specs,
                   L * CONV_HALO)
    finish_reduce(0, first_early[0], t_out[0])
    finish_reduce(0, last, upd_big["w_up"][0])
    upd ={n: [o.reshape(wmv[n][0].shape) for o in outs] for n, outs in upd_big.items()}
    upd["w_up"] = [jnp.swapaxes(o, 1, 2) for o in upd["w_up"]]
    upd["w_dw"] = [o.reshape(L, CONV_HALO, cw)[:, :CONV_WIDTH, :] for o in t_out]

    order = ["ln0_g", "ln0_b", "w_in", "b_in", "lb_logits", "g_norm_w", "w_a", "w_dw", "b_dw", "conv_ln_g",
             "conv_ln_b", "w_b", "b_b", "w_o", "ln1_g", "ln1_b", "w_up", "w_down", "ln2_g", "ln2_b"]
    small_sets = (s_g, s_d, s_m, s_v)
    outs = [loss, dx0.reshape(x.shape)]
    for i in range(4):
        for n in order:
            outs.append(upd[n][i] if n in upd else small_sets[i][n])
    return tuple(outs)
```

```python
import functools

import jax
import jax.numpy as jnp
from jax import lax
from jax.experimental import pallas as pl
from jax.experimental.pallas import tpu as pltpu

F32 = jnp.float32
MXU_DTYPE = jnp.bfloat16
ACT_DTYPE = jnp.bfloat16

LANES = 128
SUB = 8
N_DEV = 8
N_SEC = 8
CONV_WIDTH = 31
CONV_HALO = 32
HG_C = 16
LN_EPS = 1e-5
RMS_EPS = 1e-6
F_MIN = 1e-30
LOG2E = 1.4426950408889634
ADAM_LR = 0.001
ADAM_B1 = 0.9
ADAM_B2 = 0.999
ADAM_EPS = 1e-08
ADAM_WD = 0.01
ADAM_STEP = 10
VMEM_LIMIT = 56 * 1024 * 1024
MESH = pl.DeviceIdType.MESH

_NN = (((1,), (0,)), ((), ()))
_NT = (((1,), (1,)), ((), ()))
_TN = (((0,), (0,)), ((), ()))


_ANY = pl.BlockSpec(memory_space=pl.ANY)
_HBM = pl.BlockSpec(memory_space=pltpu.HBM)
_SEM = pl.BlockSpec(memory_space=pltpu.SEMAPHORE)
_EFFECT = pltpu.SideEffectType.DATAFLOW_SIDE_EFFECTING


def _cp(*sem):
    return pltpu.CompilerParams(dimension_semantics=tuple(sem), vmem_limit_bytes=VMEM_LIMIT)


def _pick(n, cands):
    for c in cands:
        if c <= n and n % c == 0:
            return c
    return n


def _silu(x):
    return x * jax.nn.sigmoid(x)


def _dsilu(x):
    s = jax.nn.sigmoid(x)
    return s * (1.0 + x * (1.0 - s))


def _matmul(name, a, b, *, dims, grid, a_spec, b_spec, out_shape, out_spec, acc_shape, nk,
            bias=None, bias_spec=None, add=None, add_spec=None, add_scale=1.0, dep=None):
    has_bias, has_add = bias is not None, add is not None
    kaxis = len(grid) - 1

    def body(*refs):
        a_ref, b_ref = refs[0], refs[1]
        pos = 2
        bias_ref = add_ref = None
        if has_bias:
            bias_ref = refs[pos]
            pos += 1
        if has_add:
            add_ref = refs[pos]
            pos += 1
        if dep is not None:
            pos += 1
        o_ref = refs[pos]
        acc_ref = refs[pos + 1] if nk > 1 else None

        part = lax.dot_general(a_ref[...].astype(MXU_DTYPE), b_ref[...].astype(MXU_DTYPE), dims,
                               preferred_element_type=F32)

        def finish(r):
            if has_bias:
                r = r + bias_ref[...]
            if has_add:
                r = r + add_scale * add_ref[...]
            o_ref[...] = r.astype(o_ref.dtype)

        if nk == 1:
            finish(part)
        else:
            k = pl.program_id(kaxis)

            @pl.when(k == 0)
            def _():
                acc_ref[...] = part

            @pl.when(k > 0)
            def _():
                acc_ref[...] += part

            @pl.when(k == nk - 1)
            def _():
                finish(acc_ref[...])

    ins, specs = [a, b], [a_spec, b_spec]
    if has_bias:
        ins.append(bias)
        specs.append(bias_spec)
    if has_add:
        ins.append(add)
        specs.append(add_spec)
    if dep is not None:
        ins.append(dep)
        specs.append(_ANY)
    sem =("parallel",) * (len(grid) - 1) + ("arbitrary",) if nk > 1 else ("parallel",) * len(grid)
    return pl.pallas_call(
        body, name=name, grid=grid, in_specs=specs, out_specs=out_spec, out_shape=out_shape,
        scratch_shapes=[pltpu.VMEM(acc_shape, F32)] if nk > 1 else [],
        compiler_params=_cp(*sem))(*ins)


def _mm_tn(name, a, b, out_dtype):
    K, M = a.shape
    N = b.shape[1]
    tm = _pick(M, (256, 128))
    return _matmul(
        name, a, b, dims=_TN, grid=(M // tm,),
        a_spec=pl.BlockSpec((K, tm), lambda i: (0, i)),
        b_spec=pl.BlockSpec((K, N), lambda i: (0, 0)),
        out_shape=jax.ShapeDtypeStruct((M, N), out_dtype),
        out_spec=pl.BlockSpec((tm, N), lambda i: (i, 0)),
        acc_shape=(tm, N), nk=1)


def _branch_dw(pairs, rs):
    T, D = pairs[0][0].shape
    nslot = max(1, LANES // rs)
    tm = nslot * rs
    nk = len(pairs)

    def body(*refs):
        o_ref = refs[-1]
        k = pl.program_id(0)
        for kk in range(nk):
            @pl.when(k == kk)
            def _():
                r = lax.dot_general(refs[2 * kk][...].astype(MXU_DTYPE), refs[2 * kk + 1][...].astype(MXU_DTYPE),
                                    _TN, preferred_element_type=F32)
                o_ref[...] = r.astype(o_ref.dtype).reshape(nslot, rs, D)

    in_specs, ins = [], []
    for kk, (a, b) in enumerate(pairs):
        in_specs.append(pl.BlockSpec((T, tm), functools.partial(lambda k, i, kk: (0, jnp.where(k == kk, i, 0)), kk=kk)))
        in_specs.append(pl.BlockSpec((T, D), lambda k, i: (0, 0)))
        ins += [a, b]
    return pl.pallas_call(
        body, name="branch_dw", grid=(nk, N_DEV // nslot), in_specs=in_specs,
        out_specs=pl.BlockSpec((nslot, rs, D), lambda k, i: (i, k, 0)),
        out_shape=jax.ShapeDtypeStruct((N_DEV, nk * rs, D), ACT_DTYPE),
        compiler_params=_cp("arbitrary", "arbitrary"))(*ins)


def _proj_in(x_bf, w_in, b_in, dep=None):
    T, D = x_bf.shape
    tn = _pick(D, (512, 256, 128))
    return _matmul(
        "proj_in", x_bf, w_in, dims=_NN, grid=(N_SEC, D // tn),
        a_spec=pl.BlockSpec((T, D), lambda s, j: (0, 0)),
        b_spec=pl.BlockSpec((None, D, tn), lambda s, j: (s, 0, j)),
        out_shape=jax.ShapeDtypeStruct((N_SEC, T, D), F32),
        out_spec=pl.BlockSpec((None, T, tn), lambda s, j: (s, 0, j)),
        acc_shape=(T, tn), nk=1,
        bias=b_in, bias_spec=pl.BlockSpec((None, 1, tn), lambda s, j: (s, 0, j)), dep=dep)


def _proj_in_dx(dh, w_in, add, add_scale):
    _, T, D = dh.shape
    tm = _pick(T, (256, 128, 64, 32, 16))

    def body(dh_ref, w_ref, add_ref, o_ref):
        acc = add_scale * add_ref[...]
        for s in range(N_SEC):
            acc = acc + lax.dot_general(dh_ref[s].astype(MXU_DTYPE), w_ref[s].astype(MXU_DTYPE), _NT,
                                        preferred_element_type=F32)
        o_ref[...] = acc

    row = pl.BlockSpec((tm, D), lambda i: (i, 0))
    return pl.pallas_call(
        body, name="proj_in_dx", grid=(T // tm,),
        in_specs=[pl.BlockSpec((N_SEC, tm, D), lambda i: (0, i, 0)),
                  pl.BlockSpec((N_SEC, D, D), lambda i: (0, 0, 0), pipeline_mode=pl.Buffered(1)), row],
        out_specs=row, out_shape=jax.ShapeDtypeStruct((T, D), F32),
        compiler_params=_cp("parallel"))(dh, w_in, add)


def _proj_in_dw(x_bf, dh):
    _, T, D = dh.shape
    tn = _pick(D, (512, 256, 128))

    def body(x_ref, dh_ref, dw_ref, db_ref):
        dhv = dh_ref[...]
        dw_ref[...] = lax.dot_general(x_ref[...].astype(MXU_DTYPE), dhv.astype(MXU_DTYPE), _TN,
                                      preferred_element_type=F32).astype(dw_ref.dtype)
        db_ref[...] = jnp.sum(dhv.astype(F32), axis=0, keepdims=True)

    return pl.pallas_call(
        body, name="proj_in_dw", grid=(N_SEC, D // tn),
        in_specs=[pl.BlockSpec((T, D), lambda s, j: (0, 0)), pl.BlockSpec((None, T, tn), lambda s, j: (s, 0, j))],
        out_specs=[pl.BlockSpec((None, D, tn), lambda s, j: (s, 0, j)),
                   pl.BlockSpec((None, 1, tn), lambda s, j: (s, 0, j))],
        out_shape=[jax.ShapeDtypeStruct((N_SEC, D, D), ACT_DTYPE), jax.ShapeDtypeStruct((N_SEC, 1, D), F32)],
        compiler_params=_cp("parallel", "parallel"))(x_bf, dh)


def _ffn_up_dx(dup, w_up_t, add, add_scale):
    _, T, F = dup.shape
    D = w_up_t.shape[1]
    tm = _pick(T, (256, 128, 64, 32, 16))

    def body(dup_ref, w_ref, add_ref, o_ref):
        acc = add_scale * add_ref[...]
        for p in range(2):
            acc = acc + jnp.dot(dup_ref[p].astype(MXU_DTYPE), w_ref[pl.ds(p * F, F), :].astype(MXU_DTYPE),
                                preferred_element_type=F32)
        o_ref[...] = acc

    row = pl.BlockSpec((tm, D), lambda i: (i, 0))
    return pl.pallas_call(
        body, name="ffn_up_dx", grid=(T // tm,),
        in_specs=[pl.BlockSpec((2, tm, F), lambda i: (0, i, 0)),
                  pl.BlockSpec((2 * F, D), lambda i: (0, 0), pipeline_mode=pl.Buffered(1)), row],
        out_specs=row, out_shape=jax.ShapeDtypeStruct((T, D), F32),
        compiler_params=_cp("parallel"))(dup, w_up_t, add)


def _ffn_up_dw(x_bf, dup):
    _, T, F = dup.shape
    D = x_bf.shape[1]
    tm = _pick(F, (1408, 256, 128))
    nb = F // tm
    return _matmul(
        "ffn_up_dw", dup, x_bf, dims=_TN, grid=(2, nb),
        a_spec=pl.BlockSpec((None, T, tm), lambda p, j: (p, 0, j)),
        b_spec=pl.BlockSpec((T, D), lambda p, j: (0, 0)),
        out_shape=jax.ShapeDtypeStruct((2 * F, D), ACT_DTYPE),
        out_spec=pl.BlockSpec((tm, D), lambda p, j: (p * nb + j, 0)),
        acc_shape=(tm, D), nk=1)


def _ln_fwd(name, a, res, alpha, g, b, dep=None):
    T, D = a.shape
    tr = _pick(T, (256, 128, 64, 32, 16))
    has_res = res is not None

    def body(*refs):
        if has_res:
            a_ref, r_ref, g_ref, b_ref = refs[:4]
            y_ref, yb_ref, z_ref = refs[-3:]
            z = alpha * a_ref[...] + r_ref[...]
            z_ref[...] = z
        else:
            a_ref, g_ref, b_ref = refs[:3]
            y_ref, yb_ref = refs[-2:]
            z = a_ref[...]
        mu = jnp.mean(z, axis=-1, keepdims=True)
        zc = z - mu
        var = jnp.mean(zc * zc, axis=-1, keepdims=True)
        y = zc * lax.rsqrt(var + LN_EPS) * g_ref[...] + b_ref[...]
        y_ref[...] = y
        yb_ref[...] = y.astype(ACT_DTYPE)

    row = pl.BlockSpec((tr, D), lambda i: (i, 0))
    vec = pl.BlockSpec((1, D), lambda i: (0, 0))
    ins = [a] + ([res] if has_res else []) + [g.reshape(1, D), b.reshape(1, D)]
    in_specs = [row] + ([row] if has_res else []) + [vec, vec]
    if dep is not None:
        ins.append(dep)
        in_specs.append(_ANY)
    out_shape = [jax.ShapeDtypeStruct((T, D), F32), jax.ShapeDtypeStruct((T, D), ACT_DTYPE)]
    if has_res:
        out_shape.append(jax.ShapeDtypeStruct((T, D), F32))
    return pl.pallas_call(
        body, name=name, grid=(T // tr,), in_specs=in_specs,
        out_specs=[row] * len(out_shape), out_shape=out_shape, compiler_params=_cp("parallel"))(*ins)


def _ln_bwd(name, z, dy, g, dep=None):
    T, D = z.shape
    tr = _pick(T, (256, 128, 64, 32, 16))

    def body(z_ref, dy_ref, g_ref, *rest):
        dz_ref, dzb_ref, dg_ref, db_ref = rest[-4:]

        @pl.when(pl.program_id(0) == 0)
        def _():
            dg_ref[...] = jnp.zeros_like(dg_ref)
            db_ref[...] = jnp.zeros_like(db_ref)

        zv = z_ref[...]
        dy_ = dy_ref[...]
        mu = jnp.mean(zv, axis=-1, keepdims=True)
        zc = zv - mu
        rstd = lax.rsqrt(jnp.mean(zc * zc, axis=-1, keepdims=True) + LN_EPS)
        xhat = zc * rstd
        dxh = dy_ * g_ref[...]
        dz = rstd * (dxh - jnp.mean(dxh, axis=-1, keepdims=True)
                     - xhat * jnp.mean(dxh * xhat, axis=-1, keepdims=True))
        dz_ref[...] = dz
        dzb_ref[...] = dz.astype(ACT_DTYPE)
        dg_ref[...] += jnp.sum(dy_ * xhat, axis=0, keepdims=True)
        db_ref[...] += jnp.sum(dy_, axis=0, keepdims=True)

    row = pl.BlockSpec((tr, D), lambda i: (i, 0))
    vec = pl.BlockSpec((1, D), lambda i: (0, 0))
    ins, in_specs = [z, dy, g.reshape(1, D)], [row, row, vec]
    if dep is not None:
        ins.append(dep)
        in_specs.append(_ANY)
    return pl.pallas_call(
        body, name=name, grid=(T // tr,), in_specs=in_specs, out_specs=[row, row, vec, vec],
        out_shape=[jax.ShapeDtypeStruct((T, D), F32), jax.ShapeDtypeStruct((T, D), ACT_DTYPE),
                   jax.ShapeDtypeStruct((1, D), F32), jax.ShapeDtypeStruct((1, D), F32)],
        compiler_params=_cp("arbitrary"))(*ins)


def _loss_fwd_bwd(y, target):
    T, D = y.shape
    tr = _pick(T, (256, 128, 64, 32, 16))

    def body(y_ref, t_ref, dy_ref, l_ref):
        @pl.when(pl.program_id(0) == 0)
        def _():
            l_ref[...] = jnp.zeros_like(l_ref)

        e = y_ref[...] - t_ref[...]
        dy_ref[...] = e * (1.0 / D)
        row = jnp.sum(e * e, axis=-1, keepdims=True) * (1.0 / D)
        l_ref[...] += 0.5 * jnp.sum(row, axis=0, keepdims=True)

    rowspec = pl.BlockSpec((tr, D), lambda i: (i, 0))
    return pl.pallas_call(
        body, name="loss", grid=(T // tr,), in_specs=[rowspec, rowspec],
        out_specs=[rowspec, pl.BlockSpec((1, LANES), lambda i: (0, 0))],
        out_shape=[jax.ShapeDtypeStruct((T, D), F32), jax.ShapeDtypeStruct((1, LANES), F32)],
        compiler_params=_cp("arbitrary"))(y, target)


def _layer_norm_rows(z, g, b):
    mu = jnp.mean(z, axis=-1, keepdims=True)
    zc = z - mu
    var = jnp.mean(zc * zc, axis=-1, keepdims=True)
    return zc * lax.rsqrt(var + LN_EPS) * g + b


def _mixer_out(y_hg, y_cv, h, xin, w_a, w_b, b_b, w_o, alpha, ln_g, ln_b):
    T, D = xin.shape
    tm = _pick(T, (256, 128, 64, 32, 16))

    def body(yhg_ref, ycv_ref, gh_ref, gc_ref, x_ref, wa_ref, wb_ref, bb_ref, wo_ref, g_ref, b_ref,
             yh_ref, yc_ref, m_ref, x1_ref, x1b_ref, z_ref):
        y_h = jnp.dot(yhg_ref[...].astype(MXU_DTYPE), wa_ref[...].astype(MXU_DTYPE), preferred_element_type=F32)
        y_c = jnp.dot(ycv_ref[...].astype(MXU_DTYPE), wb_ref[...].astype(MXU_DTYPE),
                      preferred_element_type=F32) + bb_ref[...]
        yh_ref[...] = y_h
        yc_ref[...] = y_c
        merged = (jax.nn.sigmoid(gh_ref[...]) * y_h + jax.nn.sigmoid(gc_ref[...]) * y_c).astype(ACT_DTYPE)
        m_ref[...] = merged
        z = alpha * x_ref[...] + jnp.dot(merged.astype(MXU_DTYPE), wo_ref[...].astype(MXU_DTYPE),
                                         preferred_element_type=F32)
        z_ref[...] = z
        x1 = _layer_norm_rows(z, g_ref[...], b_ref[...])
        x1_ref[...] = x1
        x1b_ref[...] = x1.astype(ACT_DTYPE)

    row = pl.BlockSpec((tm, D), lambda i: (i, 0))
    mat = pl.BlockSpec((D, D), lambda i: (0, 0))
    vec = pl.BlockSpec((1, D), lambda i: (0, 0))
    f32, act = jax.ShapeDtypeStruct((T, D), F32), jax.ShapeDtypeStruct((T, D), ACT_DTYPE)
    return pl.pallas_call(
        body, name="mixer_out", grid=(T // tm,),
        in_specs=[row, row, pl.BlockSpec((None, tm, D), lambda i: (6, i, 0)),
                  pl.BlockSpec((None, tm, D), lambda i: (7, i, 0)), row, mat, mat, vec, mat, vec, vec],
        out_specs=[row] * 6, out_shape=[f32, f32, act, f32, act, f32],
        compiler_params=_cp("parallel"))(y_hg, y_cv, h, h, xin, w_a, w_b, b_b, w_o, ln_g.reshape(1, D),
                                         ln_b.reshape(1, D))


def _ffn_up_swiglu(x_bf, w_up_t):
    T, D = x_bf.shape
    F = w_up_t.shape[0] // 2
    tn = _pick(F, (256, 128))
    nb = F // tn

    def body(x_ref, wg_ref, wv_ref, up_ref, act_ref):
        xv = x_ref[...].astype(MXU_DTYPE)
        g = lax.dot_general(xv, wg_ref[...].astype(MXU_DTYPE), _NT, preferred_element_type=F32)
        v = lax.dot_general(xv, wv_ref[...].astype(MXU_DTYPE), _NT, preferred_element_type=F32)
        up_ref[0] = g
        up_ref[1] = v
        act_ref[...] = (_silu(g) * v).astype(ACT_DTYPE)

    return pl.pallas_call(
        body, name="ffn_up", grid=(nb,),
        in_specs=[pl.BlockSpec((T, D), lambda j: (0, 0)), pl.BlockSpec((tn, D), lambda j: (j, 0)),
                  pl.BlockSpec((tn, D), lambda j: (nb + j, 0))],
        out_specs=[pl.BlockSpec((2, T, tn), lambda j: (0, 0, j)), pl.BlockSpec((T, tn), lambda j: (0, j))],
        out_shape=[jax.ShapeDtypeStruct((2, T, F), F32), jax.ShapeDtypeStruct((T, F), ACT_DTYPE)],
        compiler_params=_cp("parallel"))(x_bf, w_up_t, w_up_t)


def _ffn_down_ln2(act, w_down, x1, alpha, ln_g, ln_b):
    T, D = x1.shape
    F = act.shape[1]
    tm = _pick(T, (256, 128, 64, 32, 16))

    def body(a_ref, w_ref, x_ref, g_ref, b_ref, x2_ref, x2b_ref, z_ref):
        z = alpha * x_ref[...] + jnp.dot(a_ref[...].astype(MXU_DTYPE), w_ref[...].astype(MXU_DTYPE),
                                         preferred_element_type=F32)
        z_ref[...] = z
        x2 = _layer_norm_rows(z, g_ref[...], b_ref[...])
        x2_ref[...] = x2
        x2b_ref[...] = x2.astype(ACT_DTYPE)

    row = pl.BlockSpec((tm, D), lambda i: (i, 0))
    vec = pl.BlockSpec((1, D), lambda i: (0, 0))
    f32, actt = jax.ShapeDtypeStruct((T, D), F32), jax.ShapeDtypeStruct((T, D), ACT_DTYPE)
    return pl.pallas_call(
        body, name="ffn_down", grid=(T // tm,),
        in_specs=[pl.BlockSpec((tm, F), lambda i: (i, 0)), pl.BlockSpec((F, D), lambda i: (0, 0)), row, vec, vec],
        out_specs=[row] * 3, out_shape=[f32, actt, f32],
        compiler_params=_cp("parallel"))(act, w_down, x1, ln_g.reshape(1, D), ln_b.reshape(1, D))


def _ln2_ffn_down_bwd(z, dy, ln_g, w_down, up, dep=None):
    T, D = z.shape
    F = w_down.shape[0]
    tm = _pick(T, (256, 128, 64, 32, 16))

    def body(z_ref, dy_ref, g_ref, w_ref, up_ref, *rest):
        dz_ref, dzb_ref, dup_ref, dg_ref, db_ref = rest[-5:]

        @pl.when(pl.program_id(0) == 0)
        def _():
            dg_ref[...] = jnp.zeros_like(dg_ref)
            db_ref[...] = jnp.zeros_like(db_ref)

        zv = z_ref[...]
        dy_ = dy_ref[...]
        mu = jnp.mean(zv, axis=-1, keepdims=True)
        zc = zv - mu
        rstd = lax.rsqrt(jnp.mean(zc * zc, axis=-1, keepdims=True) + LN_EPS)
        xhat = zc * rstd
        dxh = dy_ * g_ref[...]
        dz = rstd * (dxh - jnp.mean(dxh, axis=-1, keepdims=True)
                     - xhat * jnp.mean(dxh * xhat, axis=-1, keepdims=True))
        dz_ref[...] = dz
        dzb = dz.astype(ACT_DTYPE)
        dzb_ref[...] = dzb
        dg_ref[...] += jnp.sum(dy_ * xhat, axis=0, keepdims=True)
        db_ref[...] += jnp.sum(dy_, axis=0, keepdims=True)
        da = lax.dot_general(dzb.astype(MXU_DTYPE), w_ref[...].astype(MXU_DTYPE), _NT, preferred_element_type=F32)
        ug = up_ref[0]
        dup_ref[0] = (da * up_ref[1] * _dsilu(ug)).astype(ACT_DTYPE)
        dup_ref[1] = (da * _silu(ug)).astype(ACT_DTYPE)

    row = pl.BlockSpec((tm, D), lambda i: (i, 0))
    vec = pl.BlockSpec((1, D), lambda i: (0, 0))
    blk = pl.BlockSpec((2, tm, F), lambda i: (0, i, 0))
    ins = [z, dy, ln_g.reshape(1, D), w_down, up]
    in_specs = [row, row, vec, pl.BlockSpec((F, D), lambda i: (0, 0)), blk]
    if dep is not None:
        ins.append(dep)
        in_specs.append(_ANY)
    v32 = jax.ShapeDtypeStruct((1, D), F32)
    return pl.pallas_call(
        body, name="ln2_ffn_down_bwd", grid=(T // tm,), in_specs=in_specs,
        out_specs=[row, row, blk, vec, vec],
        out_shape=[jax.ShapeDtypeStruct((T, D), F32), jax.ShapeDtypeStruct((T, D), ACT_DTYPE),
                   jax.ShapeDtypeStruct((2, T, F), ACT_DTYPE), v32, v32],
        compiler_params=_cp("arbitrary"))(*ins)


def _mixer_out_bwd(z, dx1, y_h, y_c, h, w_a, w_b, w_o, ln_g):
    T, D = z.shape
    tm = _pick(T, (256, 128, 64, 32, 16))

    def body(z_ref, dx_ref, yh_ref, yc_ref, gh_ref, gc_ref, wa_ref, wb_ref, wo_ref, g_ref,
             dz_ref, dzb_ref, dyh_ref, dyc_ref, dyhg_ref, dycv_ref, dh_ref, dg_ref, db_ref, dbb_ref):
        @pl.when(pl.program_id(0) == 0)
        def _():
            dg_ref[...] = jnp.zeros_like(dg_ref)
            db_ref[...] = jnp.zeros_like(db_ref)
            dbb_ref[...] = jnp.zeros_like(dbb_ref)

        zv = z_ref[...]
        dy_ = dx_ref[...]
        mu = jnp.mean(zv, axis=-1, keepdims=True)
        zc = zv - mu
        rstd = lax.rsqrt(jnp.mean(zc * zc, axis=-1, keepdims=True) + LN_EPS)
        xhat = zc * rstd
        dxh = dy_ * g_ref[...]
        dz = rstd * (dxh - jnp.mean(dxh, axis=-1, keepdims=True)
                     - xhat * jnp.mean(dxh * xhat, axis=-1, keepdims=True))
        dz_ref[...] = dz
        dzb = dz.astype(ACT_DTYPE)
        dzb_ref[...] = dzb
        dg_ref[...] += jnp.sum(dy_ * xhat, axis=0, keepdims=True)
        db_ref[...] += jnp.sum(dy_, axis=0, keepdims=True)
        dm_ = lax.dot_general(dzb.astype(MXU_DTYPE), wo_ref[...].astype(MXU_DTYPE), _NT, preferred_element_type=F32)
        sh = jax.nn.sigmoid(gh_ref[...])
        sc = jax.nn.sigmoid(gc_ref[...])
        dyc = dm_ * sc
        dyh_b = (dm_ * sh).astype(ACT_DTYPE)
        dyc_b = dyc.astype(ACT_DTYPE)
        dyh_ref[...] = dyh_b
        dyc_ref[...] = dyc_b
        dbb_ref[...] += jnp.sum(dyc, axis=0, keepdims=True)
        dh_ref[0] = (dm_ * yh_ref[...] * sh * (1.0 - sh)).astype(ACT_DTYPE)
        dh_ref[1] = (dm_ * yc_ref[...] * sc * (1.0 - sc)).astype(ACT_DTYPE)
        dyhg_ref[...] = lax.dot_general(dyh_b.astype(MXU_DTYPE), wa_ref[...].astype(MXU_DTYPE), _NT,
                                        preferred_element_type=F32)
        dycv_ref[...] = lax.dot_general(dyc_b.astype(MXU_DTYPE), wb_ref[...].astype(MXU_DTYPE), _NT,
                                        preferred_element_type=F32)

    row = pl.BlockSpec((tm, D), lambda i: (i, 0))
    mat = pl.BlockSpec((D, D), lambda i: (0, 0))
    vec = pl.BlockSpec((1, D), lambda i: (0, 0))
    f32, act = jax.ShapeDtypeStruct((T, D), F32), jax.ShapeDtypeStruct((T, D), ACT_DTYPE)
    v32 = jax.ShapeDtypeStruct((1, D), F32)
    return pl.pallas_call(
        body, name="mixer_out_bwd", grid=(T // tm,),
        in_specs=[row, row, row, row, pl.BlockSpec((None, tm, D), lambda i: (6, i, 0)),
                  pl.BlockSpec((None, tm, D), lambda i: (7, i, 0)), mat, mat, mat, vec],
        out_specs=[row, row, row, row, row, row, pl.BlockSpec((2, tm, D), lambda i: (3, i, 0)), vec, vec, vec],
        out_shape=[f32, act, act, act, f32, f32, jax.ShapeDtypeStruct((N_SEC, T, D), ACT_DTYPE), v32, v32, v32],
        compiler_params=_cp("arbitrary"))(z, dx1, y_h, y_c, h, h, w_a, w_b, w_o, ln_g.reshape(1, D))


def _lb_softmax(x):
    L = x.shape[0]
    rows = [x[l:l + 1] for l in range(L)]
    m = rows[0]
    for r in rows[1:]:
        m = jnp.maximum(m, r)
    e = [jnp.exp(r - m) for r in rows]
    s = e[0]
    for r in e[1:]:
        s = s + r
    return [r / s for r in e]


def _lb_fwd(lb_logits):
    L, D = lb_logits.shape

    def body(x_ref, o_ref):
        p = _lb_softmax(x_ref[...])
        run = jnp.zeros_like(p[0])
        for l in range(L):
            if l > 0:
                run = run + p[l]
            o_ref[pl.ds(l, 1), :] = run

    return pl.pallas_call(body, name="lb_fwd", out_shape=jax.ShapeDtypeStruct((L, D), F32))(lb_logits)


def _lb_bwd(lb_logits, dlbs):
    L, D = lb_logits.shape

    def body(x_ref, d_ref, o_ref):
        p = _lb_softmax(x_ref[...])
        d = d_ref[...]
        dp = [jnp.zeros_like(p[0]) for _ in range(L)]
        run = jnp.zeros_like(p[0])
        for j in range(L - 1, 0, -1):
            run = run + d[j:j + 1]
            dp[j] = run
        dot = dp[0] * p[0]
        for j in range(1, L):
            dot = dot + dp[j] * p[j]
        for j in range(L):
            o_ref[pl.ds(j, 1), :] = p[j] * (dp[j] - dot)

    return pl.pallas_call(body, name="lb_bwd", out_shape=jax.ShapeDtypeStruct((L, D), F32))(lb_logits, dlbs)


def _blk_cumsum(x, c, reverse=False):
    n = x.shape[0]
    pos = lax.broadcasted_iota(jnp.int32, x.shape, 0) % c
    s = 1
    while s < c:
        if reverse:
            shifted = pltpu.roll(x, n - s, 0)
            x = x + jnp.where(pos + s < c, shifted, 0.0)
        else:
            shifted = pltpu.roll(x, s, 0)
            x = x + jnp.where(pos >= s, shifted, 0.0)
        s *= 2
    return x


def _hgrn_prologue(q_ref, f_ref, lb_ref):
    lbv = lb_ref[...]
    z = f_ref[...]
    sig = jax.nn.sigmoid(z)
    one_m = 1.0 - lbv
    f = lbv + one_m * sig
    logf = jnp.log(jnp.maximum(f, F_MIN))
    k = one_m * jax.nn.sigmoid(-z)
    q = _silu(q_ref[...])
    return q, k, logf, f, sig, one_m


def _hgrn_fwd(h, lbs_l, gw):
    _, T, D = h.shape
    nh = D // LANES
    c = HG_C
    Tt = _pick(T, (512, 256, 128, 64, 32, 16))
    nb = Tt // c
    ng = c // SUB

    def body(q_ref, f_ref, i_ref, g_ref, lb_ref, gw_ref, o_ref, y_ref, sall_ref,
             st_ref, G_s, q_s, k_s, W_s, R_s, dS_s, o_s):
        @pl.when(pl.program_id(1) == 0)
        def _():
            st_ref[...] = jnp.zeros_like(st_ref)

        q, k, logf, _, _, _ = _hgrn_prologue(q_ref, f_ref, lb_ref)
        G_s[...] = _blk_cumsum(logf, c) * LOG2E
        q_s[...] = q
        k_s[...] = k
        ones = jnp.ones((LANES, LANES), MXU_DTYPE)
        rowid = lax.broadcasted_iota(jnp.int32, (SUB, LANES), 0)
        zero = jnp.zeros((SUB, LANES), F32)
        for bi in range(nb):
            r0 = bi * c
            glast = G_s[pl.ds(r0 + c - 1, 1), :]
            kd = k_s[pl.ds(r0, c), :] * jnp.exp2(glast - G_s[pl.ds(r0, c), :])
            dS_s[bi] = lax.dot_general(i_ref[pl.ds(r0, c), :].astype(MXU_DTYPE), kd.astype(MXU_DTYPE), _TN,
                                       preferred_element_type=F32)
        st = st_ref[...]
        for bi in range(nb):
            sall_ref[bi] = st
            st = st * jnp.exp2(G_s[pl.ds(bi * c + c - 1, 1), :]) + dS_s[bi]
        st_ref[...] = st
        for bi in range(nb):
            r0 = bi * c
            qd = q_s[pl.ds(r0, c), :] * jnp.exp2(G_s[pl.ds(r0, c), :])
            o_s[pl.ds(r0, c), :] = lax.dot_general(qd.astype(MXU_DTYPE), sall_ref[bi].astype(MXU_DTYPE), _NT,
                                                   preferred_element_type=F32)
        for bi in range(nb):
            r0 = bi * c
            w0 = bi * c * c
            Gg = [G_s[pl.ds(r0 + gi * SUB, SUB), :] for gi in range(ng)]
            qg = [q_s[pl.ds(r0 + gi * SUB, SUB), :] for gi in range(ng)]
            for s in range(c):
                gs = G_s[pl.ds(r0 + s, 1), :]
                ks = k_s[pl.ds(r0 + s, 1), :]
                parts = []
                for gi in range(ng):
                    if gi < s // SUB:
                        parts.append(zero)
                        continue
                    e = jnp.exp2(jnp.minimum(Gg[gi] - gs, 0.0))
                    if gi == s // SUB:
                        e = jnp.where(rowid >= s - gi * SUB, e, 0.0)
                    parts.append(e * qg[gi] * ks)
                W_s[pl.ds(w0 + s * c, c), :] = jnp.concatenate(parts, axis=0).astype(MXU_DTYPE)
        R_s[...] = jnp.dot(W_s[...], ones, preferred_element_type=F32)
        for bi in range(nb):
            r0 = bi * c
            w0 = bi * c * c
            acc = [o_s[pl.ds(r0 + gi * SUB, SUB), :] for gi in range(ng)]
            for s in range(c):
                vs = i_ref[pl.ds(r0 + s, 1), :]
                for gi in range(s // SUB, ng):
                    acc[gi] = acc[gi] + R_s[pl.ds(w0 + s * c + gi * SUB, SUB), :] * vs
            o_s[pl.ds(r0, c), :] = jnp.concatenate(acc, axis=0)
        o = o_s[...]
        n = o * lax.rsqrt(jnp.mean(o * o, axis=-1, keepdims=True) + RMS_EPS)
        o_ref[...] = o
        y_ref[...] = (n * gw_ref[...] * _silu(g_ref[...])).astype(ACT_DTYPE)

    def sec(s):
        return pl.BlockSpec((None, Tt, LANES), lambda hd, i: (s, i, hd))

    col = pl.BlockSpec((Tt, LANES), lambda hd, i: (i, hd))
    return pl.pallas_call(
        body, name="hgrn_fwd", grid=(nh, T // Tt),
        in_specs=[sec(0), sec(1), sec(2), sec(3), pl.BlockSpec((1, LANES), lambda hd, i: (0, hd)),
                  pl.BlockSpec((1, LANES), lambda hd, i: (0, 0))],
        out_specs=[col, col, pl.BlockSpec((nb, None, LANES, LANES), lambda hd, i: (i, hd, 0, 0))],
        out_shape=[jax.ShapeDtypeStruct((T, D), F32), jax.ShapeDtypeStruct((T, D), ACT_DTYPE),
                   jax.ShapeDtypeStruct((T // c, nh, LANES, LANES), F32)],
        scratch_shapes=[pltpu.VMEM((LANES, LANES), F32), pltpu.VMEM((Tt, LANES), F32),
                        pltpu.VMEM((Tt, LANES), F32), pltpu.VMEM((Tt, LANES), F32),
                        pltpu.VMEM((nb * c * c, LANES), MXU_DTYPE), pltpu.VMEM((nb * c * c, LANES), F32),
                        pltpu.VMEM((nb, LANES, LANES), F32), pltpu.VMEM((Tt, LANES), F32)],
        compiler_params=_cp("parallel", "arbitrary"))(h, h, h, h, lbs_l, gw)


def _hgrn_bwd(h, lbs_l, gw, o_pre, st_all, dy, dh):
    _, T, D = h.shape
    nh = D // LANES
    c = HG_C
    Tt = _pick(T, (512, 256, 128, 64, 32, 16))
    nb = Tt // c
    ng = c // SUB
    nT = T // Tt

    def body(q_ref, f_ref, i_ref, g_ref, lb_ref, gw_ref, o_ref, sall_ref, dy_ref, dh_in_ref,
             dh_ref, dlb_ref, dgw_ref,
             dst_ref, G_s, q_s, k_s, do_s, E_s, WP_s, dq_s, dk_s, dv_s, dG_s,
             R_s, dS_s, dstA_s, dqd_s, dkd_s, dvi_s, da_s):
        del dh_in_ref
        hd, ti = pl.program_id(0), pl.program_id(1)

        @pl.when(ti == 0)
        def _():
            dst_ref[...] = jnp.zeros_like(dst_ref)
            dlb_ref[...] = jnp.zeros_like(dlb_ref)

        @pl.when((ti == 0) & (hd == 0))
        def _():
            dgw_ref[...] = jnp.zeros_like(dgw_ref)

        q, k, logf, f, sig, one_m = _hgrn_prologue(q_ref, f_ref, lb_ref)
        G_s[...] = _blk_cumsum(logf, c) * LOG2E
        q_s[...] = q
        k_s[...] = k

        o = o_ref[...]
        gr = g_ref[...]
        dy_ = dy_ref[...]
        rr = lax.rsqrt(jnp.mean(o * o, axis=-1, keepdims=True) + RMS_EPS)
        n = o * rr
        sg = _silu(gr)
        gwv = gw_ref[...]
        dh_ref[3] = (dy_ * n * gwv * _dsilu(gr)).astype(ACT_DTYPE)
        dgw_ref[...] += jnp.sum(dy_ * n * sg, axis=0, keepdims=True)
        dn = dy_ * gwv * sg
        do_s[...] = rr * (dn - n * jnp.mean(dn * n, axis=-1, keepdims=True))

        ones = jnp.ones((LANES, LANES), MXU_DTYPE)
        rowid = lax.broadcasted_iota(jnp.int32, (SUB, LANES), 0)
        rowid_c = lax.broadcasted_iota(jnp.int32, (c, LANES), 0)
        zero = jnp.zeros((SUB, LANES), F32)
        cc = c * c
        for bi in range(nb):
            r0 = bi * c
            qd = q_s[pl.ds(r0, c), :] * jnp.exp2(G_s[pl.ds(r0, c), :])
            dS_s[bi] = lax.dot_general(do_s[pl.ds(r0, c), :].astype(MXU_DTYPE), qd.astype(MXU_DTYPE), _TN,
                                       preferred_element_type=F32)
        dst = dst_ref[...]
        for bi in range(nb - 1, -1, -1):
            dstA_s[bi] = dst
            dst = dst * jnp.exp2(G_s[pl.ds(bi * c + c - 1, 1), :]) + dS_s[bi]
        dst_ref[...] = dst
        for bi in range(nb):
            r0 = bi * c
            glast = G_s[pl.ds(r0 + c - 1, 1), :]
            kd = k_s[pl.ds(r0, c), :] * jnp.exp2(glast - G_s[pl.ds(r0, c), :])
            st = sall_ref[bi]
            dstb = dstA_s[bi]
            dst_m = dstb.astype(MXU_DTYPE)
            dqd_s[pl.ds(r0, c), :] = lax.dot_general(do_s[pl.ds(r0, c), :].astype(MXU_DTYPE), st.astype(MXU_DTYPE),
                                                     _NN, preferred_element_type=F32)
            dkd_s[pl.ds(r0, c), :] = lax.dot_general(i_ref[pl.ds(r0, c), :].astype(MXU_DTYPE), dst_m, _NN,
                                                     preferred_element_type=F32)
            dvi_s[pl.ds(r0, c), :] = lax.dot_general(kd.astype(MXU_DTYPE), dst_m, _NT,
                                                     preferred_element_type=F32)
            da_s[pl.ds(bi * SUB, 1), :] = jnp.sum(dstb * st, axis=0, keepdims=True)
        for bi in range(nb):
            r0 = bi * c
            e0, w0 = bi * cc, bi * 2 * cc
            Gg = [G_s[pl.ds(r0 + gi * SUB, SUB), :] for gi in range(ng)]
            kg = [k_s[pl.ds(r0 + gi * SUB, SUB), :] for gi in range(ng)]
            vg = [i_ref[pl.ds(r0 + gi * SUB, SUB), :] for gi in range(ng)]
            for t in range(c):
                gt = G_s[pl.ds(r0 + t, 1), :]
                qt = q_s[pl.ds(r0 + t, 1), :]
                dot_ = do_s[pl.ds(r0 + t, 1), :]
                ep, wp, pp = [], [], []
                for gi in range(ng):
                    if gi > t // SUB:
                        ep.append(zero)
                        wp.append(zero)
                        pp.append(zero)
                        continue
                    e = jnp.exp2(jnp.minimum(gt - Gg[gi], 0.0))
                    if gi == t // SUB:
                        e = jnp.where(rowid <= t - gi * SUB, e, 0.0)
                    ep.append(e)
                    wp.append(e * kg[gi] * qt)
                    pp.append(vg[gi] * dot_)
                E_s[pl.ds(e0 + t * c, c), :] = jnp.concatenate(ep, axis=0)
                WP_s[pl.ds(w0 + t * c, c), :] = jnp.concatenate(wp, axis=0).astype(MXU_DTYPE)
                WP_s[pl.ds(w0 + cc + t * c, c), :] = jnp.concatenate(pp, axis=0).astype(MXU_DTYPE)
        R_s[...] = jnp.dot(WP_s[...], ones, preferred_element_type=F32)
        for bi in range(nb):
            r0 = bi * c
            e0, w0 = bi * cc, bi * 2 * cc
            kg = [k_s[pl.ds(r0 + gi * SUB, SUB), :] for gi in range(ng)]
            dk_g = [zero] * ng
            dv_g = [zero] * ng
            dq_g = [zero] * ng
            for t in range(c):
                qt = q_s[pl.ds(r0 + t, 1), :]
                dot_ = do_s[pl.ds(r0 + t, 1), :]
                tot = None
                for gi in range(t // SUB + 1):
                    lo = t * c + gi * SUB
                    dae = R_s[pl.ds(w0 + cc + lo, SUB), :] * E_s[pl.ds(e0 + lo, SUB), :]
                    z = dae * kg[gi]
                    tot = z if tot is None else tot + z
                    dk_g[gi] = dk_g[gi] + dae * qt
                    dv_g[gi] = dv_g[gi] + R_s[pl.ds(w0 + lo, SUB), :] * dot_
                gt_ = t // SUB
                dq_g[gt_] = jnp.where(rowid == t - gt_ * SUB, jnp.sum(tot, axis=0, keepdims=True), dq_g[gt_])
            dq_i = jnp.concatenate(dq_g, axis=0)
            dk_i = jnp.concatenate(dk_g, axis=0)
            dv_i = jnp.concatenate(dv_g, axis=0)
            Gb = G_s[pl.ds(r0, c), :]
            qb = q_s[pl.ds(r0, c), :]
            kb = k_s[pl.ds(r0, c), :]
            glast = G_s[pl.ds(r0 + c - 1, 1), :]
            eg = jnp.exp2(Gb)
            egl = jnp.exp2(glast - Gb)
            dqd = dqd_s[pl.ds(r0, c), :]
            dkd = dkd_s[pl.ds(r0, c), :]
            dq_s[pl.ds(r0, c), :] = dqd * eg + dq_i
            dk_s[pl.ds(r0, c), :] = dkd * egl + dk_i
            dv_s[pl.ds(r0, c), :] = dvi_s[pl.ds(r0, c), :] + dv_i
            dkdkd = dkd * kb * egl
            dG = dqd * qb * eg + qb * dq_i - kb * dk_i - dkdkd
            dglast = jnp.sum(dkdkd, axis=0, keepdims=True) + da_s[pl.ds(bi * SUB, 1), :] * jnp.exp2(glast)
            dG_s[pl.ds(r0, c), :] = dG + jnp.where(rowid_c == c - 1, dglast, 0.0)

        dlogf = _blk_cumsum(dG_s[...], c, reverse=True)
        df = jnp.where(f > F_MIN, dlogf / f, 0.0)
        dk = dk_s[...]
        dh_ref[0] = (dq_s[...] * _dsilu(q_ref[...])).astype(ACT_DTYPE)
        dh_ref[1] = ((df - dk) * one_m * sig * (1.0 - sig)).astype(ACT_DTYPE)
        dh_ref[2] = dv_s[...].astype(ACT_DTYPE)
        dlb_ref[...] += jnp.sum((df - dk) * (1.0 - sig), axis=0, keepdims=True)

    def sec(s):
        return pl.BlockSpec((None, Tt, LANES), lambda hd, i: (s, nT - 1 - i, hd))

    col = pl.BlockSpec((Tt, LANES), lambda hd, i: (nT - 1 - i, hd))
    tile = pltpu.VMEM((Tt, LANES), F32)
    return pl.pallas_call(
        body, name="hgrn_bwd", grid=(nh, nT),
        in_specs=[sec(0), sec(1), sec(2), sec(3), pl.BlockSpec((1, LANES), lambda hd, i: (0, hd)),
                  pl.BlockSpec((1, LANES), lambda hd, i: (0, 0)), col,
                  pl.BlockSpec((nb, None, LANES, LANES), lambda hd, i: (nT - 1 - i, hd, 0, 0)), col,
                  pl.BlockSpec(memory_space=pl.ANY)],
        out_specs=[pl.BlockSpec((4, Tt, LANES), lambda hd, i: (0, nT - 1 - i, hd)),
                   pl.BlockSpec((1, LANES), lambda hd, i: (0, hd)),
                   pl.BlockSpec((1, LANES), lambda hd, i: (0, 0))],
        out_shape=[jax.ShapeDtypeStruct(dh.shape, dh.dtype), jax.ShapeDtypeStruct((1, D), F32),
                   jax.ShapeDtypeStruct((1, LANES), F32)],
        scratch_shapes=[pltpu.VMEM((LANES, LANES), F32), tile, tile, tile, tile,
                        pltpu.VMEM((nb * c * c, LANES), F32), pltpu.VMEM((2 * nb * c * c, LANES), MXU_DTYPE),
                        tile, tile, tile, tile,
                        pltpu.VMEM((2 * nb * c * c, LANES), F32), pltpu.VMEM((nb, LANES, LANES), F32),
                        pltpu.VMEM((nb, LANES, LANES), F32), tile, tile, tile, pltpu.VMEM((nb * SUB, LANES), F32)],
        input_output_aliases={9: 0},
        compiler_params=_cp("arbitrary", "arbitrary"))(h, h, h, h, lbs_l, gw, o_pre, st_all, dy, dh)


def _shifted_copies(src, cs, dst, rows):
    for b in range(1, SUB):
        dst[b - 1] = src[pl.ds(b, rows + CONV_HALO - SUB), cs]


def _shifted(src, cs, copies, shift, rows):
    a8, b = divmod(shift, SUB)
    if b == 0:
        return src[pl.ds(shift, rows), cs]
    return copies[b - 1, pl.ds(a8 * SUB, rows), :]


def _conv_fwd(h, w_dw, b_dw, ln_g, ln_b):
    _, T, D = h.shape
    Tt = _pick(T, (256, 128, 64, 32))
    hb = Tt // CONV_HALO
    off = CONV_HALO - (CONV_WIDTH - 1)

    def body(a_ref, b_ref, ap_ref, bp_ref, w_ref, bd_ref, g_ref, be_ref, yc_ref, y_ref, U_s, Ub_s):
        first = pl.program_id(0) == 0
        up = ap_ref[...] * jax.nn.sigmoid(bp_ref[...])
        U_s[pl.ds(0, CONV_HALO), :] = jnp.where(first, 0.0, up)
        U_s[pl.ds(CONV_HALO, Tt), :] = a_ref[...] * jax.nn.sigmoid(b_ref[...])
        for cb in range(D // LANES):
            cs = pl.ds(cb * LANES, LANES)
            _shifted_copies(U_s, cs, Ub_s, Tt)
            acc = jnp.zeros((Tt, LANES), F32)
            for j in range(CONV_WIDTH):
                acc = acc + w_ref[pl.ds(j, 1), cs] * _shifted(U_s, cs, Ub_s, off + j, Tt)
            yc_ref[:, cs] = acc + bd_ref[:, cs]
        yc = yc_ref[...]
        mu = jnp.mean(yc, axis=-1, keepdims=True)
        zc = yc - mu
        var = jnp.mean(zc * zc, axis=-1, keepdims=True)
        ln = zc * lax.rsqrt(var + LN_EPS) * g_ref[...] + be_ref[...]
        y_ref[...] = _silu(ln).astype(ACT_DTYPE)

    def main(s):
        return pl.BlockSpec((None, Tt, D), lambda i: (s, i, 0))

    def prev(s):
        return pl.BlockSpec((None, CONV_HALO, D), lambda i: (s, jnp.maximum(i * hb - 1, 0), 0))

    row = pl.BlockSpec((Tt, D), lambda i: (i, 0))
    vec = pl.BlockSpec((1, D), lambda i: (0, 0))
    return pl.pallas_call(
        body, name="conv_fwd", grid=(T // Tt,),
        in_specs=[main(4), main(5), prev(4), prev(5), pl.BlockSpec((CONV_HALO, D), lambda i: (0, 0)),
                  vec, vec, vec],
        out_specs=[row, row],
        out_shape=[jax.ShapeDtypeStruct((T, D), F32), jax.ShapeDtypeStruct((T, D), ACT_DTYPE)],
        scratch_shapes=[pltpu.VMEM((CONV_HALO + Tt, D), F32),
                        pltpu.VMEM((SUB - 1, Tt + CONV_HALO - SUB, LANES), F32)],
        compiler_params=_cp("parallel"))(h, h, h, h, w_dw, b_dw, ln_g, ln_b)


def _conv_bwd(h, w_dw, ln_g, ln_b, yc, dy, dh):
    _, T, D = h.shape
    Tt = _pick(T, (256, 128, 64, 32))
    hb = Tt // CONV_HALO
    nT = T // Tt
    nhb = T // CONV_HALO
    off = CONV_HALO - (CONV_WIDTH - 1)

    def body(a_ref, b_ref, ap_ref, bp_ref, w_ref, g_ref, be_ref, yc_ref, ycn_ref, dy_ref, dyn_ref, dh_in_ref,
             dh_ref, dw_ref, dbd_ref, dg_ref, dbe_ref, U_s, DY_s, du_s, Ub_s, DYb_s):
        del dh_in_ref
        i = pl.program_id(0)

        @pl.when(i == 0)
        def _():
            dw_ref[...] = jnp.zeros_like(dw_ref)
            dbd_ref[...] = jnp.zeros_like(dbd_ref)
            dg_ref[...] = jnp.zeros_like(dg_ref)
            dbe_ref[...] = jnp.zeros_like(dbe_ref)

        gv = g_ref[...]
        bev = be_ref[...]

        def ln_silu_bwd(ycv, dyv):
            mu = jnp.mean(ycv, axis=-1, keepdims=True)
            zc = ycv - mu
            rstd = lax.rsqrt(jnp.mean(zc * zc, axis=-1, keepdims=True) + LN_EPS)
            xhat = zc * rstd
            dln = dyv * _dsilu(xhat * gv + bev)
            dxh = dln * gv
            dyc = rstd * (dxh - jnp.mean(dxh, axis=-1, keepdims=True)
                          - xhat * jnp.mean(dxh * xhat, axis=-1, keepdims=True))
            return dyc, dln, xhat

        dyc, dln, xhat = ln_silu_bwd(yc_ref[...], dy_ref[...])
        dg_ref[...] += jnp.sum(dln * xhat, axis=0, keepdims=True)
        dbe_ref[...] += jnp.sum(dln, axis=0, keepdims=True)
        dbd_ref[...] += jnp.sum(dyc, axis=0, keepdims=True)
        DY_s[pl.ds(0, Tt), :] = dyc
        dycn, _, _ = ln_silu_bwd(ycn_ref[...], dyn_ref[...])
        DY_s[pl.ds(Tt, CONV_HALO), :] = jnp.where(i == nT - 1, 0.0, dycn)

        sb = jax.nn.sigmoid(b_ref[...])
        av = a_ref[...]
        up = ap_ref[...] * jax.nn.sigmoid(bp_ref[...])
        U_s[pl.ds(0, CONV_HALO), :] = jnp.where(i == 0, 0.0, up)
        U_s[pl.ds(CONV_HALO, Tt), :] = av * sb

        for cb in range(D // LANES):
            cs = pl.ds(cb * LANES, LANES)
            _shifted_copies(U_s, cs, Ub_s, Tt)
            _shifted_copies(DY_s, cs, DYb_s, Tt)
            dyb = DY_s[pl.ds(0, Tt), cs]
            acc = jnp.zeros((Tt, LANES), F32)
            for j in range(CONV_WIDTH):
                acc = acc + w_ref[pl.ds(j, 1), cs] * _shifted(DY_s, cs, DYb_s, CONV_WIDTH - 1 - j, Tt)
                dw_ref[pl.ds(j, 1), cs] += jnp.sum(dyb * _shifted(U_s, cs, Ub_s, off + j, Tt), axis=0, keepdims=True)
            du_s[:, cs] = acc
        du = du_s[...]
        dh_ref[0] = (du * sb).astype(ACT_DTYPE)
        dh_ref[1] = (du * av * sb * (1.0 - sb)).astype(ACT_DTYPE)

    def main(s):
        return pl.BlockSpec((None, Tt, D), lambda i: (s, i, 0))

    def prev(s):
        return pl.BlockSpec((None, CONV_HALO, D), lambda i: (s, jnp.maximum(i * hb - 1, 0), 0))

    row = pl.BlockSpec((Tt, D), lambda i: (i, 0))
    nxt = pl.BlockSpec((CONV_HALO, D), lambda i: (jnp.minimum((i + 1) * hb, nhb - 1), 0))
    vec = pl.BlockSpec((1, D), lambda i: (0, 0))
    wspec = pl.BlockSpec((CONV_HALO, D), lambda i: (0, 0))
    return pl.pallas_call(
        body, name="conv_bwd", grid=(nT,),
        in_specs=[main(4), main(5), prev(4), prev(5), wspec, vec, vec, row, nxt, row, nxt,
                  pl.BlockSpec(memory_space=pl.ANY)],
        out_specs=[pl.BlockSpec((2, Tt, D), lambda i: (2, i, 0)), wspec, vec, vec, vec],
        out_shape=[jax.ShapeDtypeStruct(dh.shape, dh.dtype), jax.ShapeDtypeStruct((CONV_HALO, D), F32),
                   jax.ShapeDtypeStruct((1, D), F32), jax.ShapeDtypeStruct((1, D), F32),
                   jax.ShapeDtypeStruct((1, D), F32)],
        scratch_shapes=[pltpu.VMEM((CONV_HALO + Tt, D), F32), pltpu.VMEM((Tt + CONV_HALO, D), F32),
                        pltpu.VMEM((Tt, D), F32),
                        pltpu.VMEM((SUB - 1, Tt + CONV_HALO - SUB, LANES), F32),
                        pltpu.VMEM((SUB - 1, Tt + CONV_HALO - SUB, LANES), F32)],
        input_output_aliases={11: 0},
        compiler_params=_cp("arbitrary"))(h, h, h, h, w_dw, ln_g, ln_b, yc, yc, dy, dy, dh)


def _adamw(name, w, m, v, parts, part_specs, tr, prefetch=None, nsteps=None, row_map=None, prev=None):
    R, C = w.shape
    bc1 = 1.0 - ADAM_B1 ** ADAM_STEP
    bc2 = 1.0 - ADAM_B2 ** ADAM_STEP
    npart = len(parts)
    npre = 0 if prefetch is None else 1
    nprev = 0 if prev is None else 4

    def body(*refs):
        refs = refs[npre:]
        w_ref, m_ref, v_ref = refs[:3]
        p_refs = refs[3:3 + npart]
        g_ref, d_ref, mo_ref, vo_ref = refs[3 + npart + nprev:]
        g = p_refs[0][...].astype(F32)
        for p in p_refs[1:]:
            g = g + p[...].astype(F32)
        wv = w_ref[...]
        mn = ADAM_B1 * m_ref[...] + (1.0 - ADAM_B1) * g
        vn = ADAM_B2 * v_ref[...] + (1.0 - ADAM_B2) * (g * g)
        m_hat = mn / bc1
        v_hat = vn / bc2
        g_ref[...] = g
        d_ref[...] = -ADAM_LR * (m_hat / (jnp.sqrt(v_hat) + ADAM_EPS) + ADAM_WD * wv)
        mo_ref[...] = mn
        vo_ref[...] = vn

    if row_map is None:
        row_map = (lambda i: (i, 0)) if prefetch is None else (lambda i, s: (i, 0))
    row = pl.BlockSpec((tr, C), row_map)
    out = jax.ShapeDtypeStruct((R, C), F32)
    gs = pltpu.PrefetchScalarGridSpec(
        num_scalar_prefetch=npre, grid=(R // tr if nsteps is None else nsteps,),
        in_specs=[row, row, row] + list(part_specs) + [_ANY] * nprev, out_specs=[row] * 4)
    args = ([prefetch] if npre else []) + [w, m, v] + list(parts) + (list(prev) if nprev else [])
    first_prev = npre + 3 + npart
    return pl.pallas_call(body, name=name, grid_spec=gs, out_shape=[out] * 4,
                          input_output_aliases={first_prev + i: i for i in range(nprev)},
                          compiler_params=_cp("parallel"))(*args)


def _pair_add(p, r1, my_c):
    _, R, C = r1.shape
    tr = max(t for t in range(16, 1025, 16) if R % t == 0)

    def body(c_ref, p_ref, r_ref, q_ref):
        del c_ref
        q_ref[...] = (p_ref[...].astype(F32) + r_ref[...].astype(F32)).astype(q_ref.dtype)

    gs = pltpu.PrefetchScalarGridSpec(
        num_scalar_prefetch=1, grid=(4, R // tr),
        in_specs=[pl.BlockSpec((None, tr, C), lambda j, i, c: (2 * j + c[0], i, 0)),
                  pl.BlockSpec((None, tr, C), lambda j, i, c: (j, i, 0))],
        out_specs=pl.BlockSpec((None, tr, C), lambda j, i, c: (j, i, 0)))
    return pl.pallas_call(body, name="pair_add", grid_spec=gs, out_shape=jax.ShapeDtypeStruct(r1.shape, r1.dtype),
                          compiler_params=_cp("parallel", "parallel"))(my_c, p, r1)


def _place():
    x, y, c = lax.axis_index("x"), lax.axis_index("y"), lax.axis_index("c")
    chips = [(1 - x, y), (x, 1 - y), (1 - x, 1 - y)]
    return x, y, c, chips


def _hbm(a):
    return pltpu.with_memory_space_constraint(a, pltpu.HBM)


def _gather_targets():
    x, y, c, chips = _place()
    return 4 * x + 2 * y + c, [(x, y, 1 - c)] + [(*chip, c) for chip in chips]


def _gather_start(name, shards, zones, after=None):
    n = len(shards)
    lands = [_hbm(z) for z in zones]
    n_in = 2 * n + (0 if after is None else 1)

    def body(*refs):
        srcs, zones = refs[:n], refs[n:2 * n]
        send, recv, token = refs[n_in], refs[n_in + 1], refs[-1]
        mine, targets = _gather_targets()
        for a in range(n):
            for k, to in enumerate(targets):
                pltpu.make_async_remote_copy(
                    src_ref=srcs[a], dst_ref=zones[a].at[mine], send_sem=send.at[4 * a + k],
                    recv_sem=recv.at[4 * a + k], device_id=to, device_id_type=MESH).start()
        token[...] = jnp.zeros_like(token)

    sem = pltpu.SemaphoreType.DMA((4 * n,))
    out_shape = ([sem, sem] + [pltpu.HBM(s.shape, s.dtype) for s in shards]
                 + [pltpu.HBM(z.shape, z.dtype) for z in lands] + [jax.ShapeDtypeStruct((8, LANES), F32)])
    outs = pl.pallas_call(
        body, name=name, out_shape=out_shape, in_specs=[_HBM] * (2 * n) + ([] if after is None else [_ANY]),
        out_specs=[_SEM, _SEM] + [_HBM] * (2 * n) + [pl.BlockSpec(memory_space=pltpu.VMEM)],
        input_output_aliases={i: 2 + i for i in range(2 * n)},
        compiler_params=pltpu.CompilerParams(has_side_effects=_EFFECT))(
            *[_hbm(s) for s in shards], *lands, *([] if after is None else [after]))
    return outs[0], outs[1], list(outs[2:2 + n]), list(outs[2 + n:2 + 2 * n]), outs[-1]


def _gather_wait(name, shards, zones, send, recv, after):
    per = len(shards)

    def body(*refs):
        srcs, lz = refs[:per], refs[per:2 * per]
        send_s, recv_s = refs[2 * per], refs[2 * per + 1]
        mine, targets = _gather_targets()
        for a in range(per):
            for k, to in enumerate(targets):
                cp = pltpu.make_async_remote_copy(
                    src_ref=srcs[a], dst_ref=lz[a].at[mine], send_sem=send_s.at[4 * a + k],
                    recv_sem=recv_s.at[4 * a + k], device_id=to, device_id_type=MESH)
                cp.wait_send()
                cp.wait_recv()

    outs = pl.pallas_call(
        body, name=name, out_shape=[pltpu.HBM(s.shape, s.dtype) for s in shards + zones],
        in_specs=[_HBM] * (2 * per) + [_SEM, _SEM, _ANY], out_specs=[_HBM] * (2 * per),
        input_output_aliases={i: i for i in range(2 * per)},
        compiler_params=pltpu.CompilerParams(has_side_effects=_EFFECT))(*shards, *zones, send, recv, after)
    return outs[:per], outs[per:]


def _gather_finish(zones):
    n = len(zones)

    def body(*refs):
        lz = refs[n:2 * n]
        send_sems, recv_sems = refs[2 * n:]
        x, y, c, chips = _place()

        def fwd(a, j, pc):
            cx, cy = chips[j]
            blk = lz[a].at[4 * cx + 2 * cy + pc]
            return pltpu.make_async_remote_copy(
                src_ref=blk, dst_ref=blk, send_sem=send_sems.at[3 * a + j], recv_sem=recv_sems.at[3 * a + j],
                device_id=(x, y, 1 - c), device_id_type=MESH)

        sends = [fwd(a, j, c) for a in range(n) for j in range(3)]
        for cp in sends:
            cp.start()
        for a in range(n):
            for j in range(3):
                fwd(a, j, 1 - c).wait_recv()
        for cp in sends:
            cp.wait_send()

    return pl.pallas_call(
        body, name="gather_finish", out_shape=[jax.ShapeDtypeStruct(z.shape, z.dtype) for z in zones],
        in_specs=[_ANY] * n, out_specs=[_ANY] * n, input_output_aliases={a: a for a in range(n)},
        scratch_shapes=[pltpu.SemaphoreType.DMA((3 * n,)), pltpu.SemaphoreType.DMA((3 * n,))])(*zones)


def _place_own(shard, dev):
    R, C = shard.shape
    tr = max(t for t in range(16, 1025, 16) if R % t == 0)

    def body(d_ref, s_ref, z_ref):
        del d_ref
        z_ref[...] = s_ref[...]

    gs = pltpu.PrefetchScalarGridSpec(
        num_scalar_prefetch=1, grid=(R // tr,), in_specs=[pl.BlockSpec((tr, C), lambda i, d: (i, 0))],
        out_specs=pl.BlockSpec((None, tr, C), lambda i, d: (d[0], i, 0)))
    return pl.pallas_call(body, name="place_own", grid_spec=gs,
                          out_shape=jax.ShapeDtypeStruct((N_DEV, R, C), shard.dtype),
                          compiler_params=_cp("parallel"))(dev, shard)


def _exchange_sibling(bufs):
    n_arr = len(bufs)

    def body(*refs):
        srcs, outs = refs[:n_arr], refs[n_arr:2 * n_arr]
        send_sems, recv_sems = refs[2 * n_arr:]
        x, y, c, _ = _place()
        copies = []
        for n in range(n_arr):
            for j in range(4):
                copies.append(pltpu.make_async_remote_copy(
                    src_ref=srcs[n].at[2 * j + 1 - c], dst_ref=outs[n].at[j],
                    send_sem=send_sems.at[4 * n + j], recv_sem=recv_sems.at[4 * n + j],
                    device_id=(x, y, 1 - c), device_id_type=MESH))
        for cp in copies:
            cp.start()
        for cp in copies:
            cp.wait()

    return pl.pallas_call(
        body, name="exchange_sibling",
        out_shape=[jax.ShapeDtypeStruct((4,) + b.shape[1:], b.dtype) for b in bufs],
        in_specs=[_ANY] * n_arr, out_specs=[_ANY] * n_arr,
        scratch_shapes=[pltpu.SemaphoreType.DMA((4 * n_arr,)), pltpu.SemaphoreType.DMA((4 * n_arr,))])(*bufs)


def _chip_copies(srcs, zones, send, recv):
    _, _, c, chips = _place()
    return [pltpu.make_async_remote_copy(
        src_ref=srcs[n].at[2 * cx + cy], dst_ref=zones[n].at[k], send_sem=send.at[3 * n + k],
        recv_sem=recv.at[3 * n + k], device_id=(cx, cy, c), device_id_type=MESH)
        for n in range(len(srcs)) for k, (cx, cy) in enumerate(chips)]


def _exchange_chips_start(name, bufs, after=None):
    n = len(bufs)
    n_in = 2 * n + (0 if after is None else 1)
    lands = [_hbm(lax.empty((3,) + b.shape[1:], b.dtype)) for b in bufs]

    def body(*refs):
        srcs, zones = refs[:n], refs[n:2 * n]
        send, recv, token = refs[n_in], refs[n_in + 1], refs[-1]
        for cp in _chip_copies(srcs, zones, send, recv):
            cp.start()
        token[...] = jnp.zeros_like(token)

    sem = pltpu.SemaphoreType.DMA((3 * n,))
    outs = pl.pallas_call(
        body, name=name,
        out_shape=[sem, sem] + [pltpu.HBM(b.shape, b.dtype) for b in bufs]
        + [pltpu.HBM(z.shape, z.dtype) for z in lands] + [jax.ShapeDtypeStruct((8, LANES), F32)],
        in_specs=[_HBM] * (2 * n) + ([] if after is None else [_ANY]),
        out_specs=[_SEM, _SEM] + [_HBM] * (2 * n) + [pl.BlockSpec(memory_space=pltpu.VMEM)],
        input_output_aliases={i: 2 + i for i in range(2 * n)},
        compiler_params=pltpu.CompilerParams(has_side_effects=_EFFECT))(
            *[_hbm(b) for b in bufs], *lands, *([] if after is None else [after]))
    return outs[0], outs[1], outs[2:2 + n], outs[2 + n:2 + 2 * n], outs[-1]


def _exchange_chips_wait(name, bufs, zones, send, recv, after):
    n = len(bufs)

    def body(*refs):
        for cp in _chip_copies(refs[:n], refs[n:2 * n], refs[2 * n], refs[2 * n + 1]):
            cp.wait_send()
            cp.wait_recv()

    outs = pl.pallas_call(
        body, name=name, out_shape=[pltpu.HBM(a.shape, a.dtype) for a in list(bufs) + list(zones)],
        in_specs=[_HBM] * (2 * n) + [_SEM, _SEM, _ANY], out_specs=[_HBM] * (2 * n),
        input_output_aliases={i: i for i in range(2 * n)},
        compiler_params=pltpu.CompilerParams(has_side_effects=_EFFECT))(*bufs, *zones, send, recv, after)
    return outs[n:]


def _all_gather_small(part):
    def body(src, out, send_sems, recv_sems, local_sem):
        x, y, c, _ = _place()
        mine = pltpu.make_async_copy(src, out.at[4 * x + 2 * y + c], local_sem)
        mine.start()
        copies = []
        for r in range(1, N_DEV):
            dx, dy, dc = (r >> 2) & 1, (r >> 1) & 1, r & 1
            peer = (1 - x if dx else x, 1 - y if dy else y, 1 - c if dc else c)
            copies.append(pltpu.make_async_remote_copy(
                src_ref=src, dst_ref=out.at[4 * x + 2 * y + c],
                send_sem=send_sems.at[r - 1], recv_sem=recv_sems.at[r - 1],
                device_id=peer, device_id_type=MESH))
        for cp in copies:
            cp.start()
        for cp in copies:
            cp.wait()
        mine.wait()

    return pl.pallas_call(
        body, name="all_gather_small",
        out_shape=jax.ShapeDtypeStruct((N_DEV,) + part.shape, part.dtype),
        in_specs=[_ANY], out_specs=_ANY,
        scratch_shapes=[pltpu.SemaphoreType.DMA((N_DEV - 1,)), pltpu.SemaphoreType.DMA((N_DEV - 1,)),
                        pltpu.SemaphoreType.DMA])(part)


def _layer_fwd(xin, xin_bf, w_in, rest, P, alpha, dep=None):
    h = _proj_in(xin_bf, w_in, P["b_in"], dep=dep)
    o_pre, y_hg, st_all = _hgrn_fwd(h, P["lbs"], P["g_norm_w"])
    yc_pre, y_cv = _conv_fwd(h, P["w_dw"], P["b_dw"], P["conv_ln_g"], P["conv_ln_b"])
    W = rest(y_cv)
    y_h, y_c, merged, x1, x1_bf, z1 = _mixer_out(y_hg, y_cv, h, xin, W["w_a"], W["w_b"], P["b_b"], W["w_o"], alpha,
                                                 P["ln1_g"], P["ln1_b"])
    up, act = _ffn_up_swiglu(x1_bf, W["w_up"])
    x2, x2_bf, z2 = _ffn_down_ln2(act, W["w_down"], x1, alpha, P["ln2_g"], P["ln2_b"])
    saved = dict(xin_bf=xin_bf, h=h, o_pre=o_pre, y_hg=y_hg, st_all=st_all, yc_pre=yc_pre, y_cv=y_cv,
                 y_h=y_h, y_c=y_c, merged=merged, z1=z1, x1_bf=x1_bf, up=up, act=act, z2=z2)
    return x2, x2_bf, saved


def _layer_bwd(dx2, S, W, P, alpha, dep=None, early=None):
    dz2, dz2_bf, dup, dln2_g, dln2_b = _ln2_ffn_down_bwd(S["z2"], dx2, P["ln2_g"], W["w_down"], S["up"], dep=dep)
    dw_down = _mm_tn("ffn_down_dw", S["act"], dz2_bf, ACT_DTYPE)
    dx1 = _ffn_up_dx(dup, W["w_up"], dz2, alpha)
    dw_up = _ffn_up_dw(S["x1_bf"], dup)
    dz1, dz1_bf, dy_h, dy_c, dy_hg, dy_cv, dh, dln1_g, dln1_b, db_b = _mixer_out_bwd(
        S["z1"], dx1, S["y_h"], S["y_c"], S["h"], W["w_a"], W["w_b"], W["w_o"], P["ln1_g"])
    dw_abo = _branch_dw([(S["y_hg"], dy_h), (S["y_cv"], dy_c), (S["merged"], dz1_bf)], dz1.shape[1] // N_DEV)
    if early is not None:
        token = early(dict(w_abo=dw_abo, w_down=dw_down, w_up=dw_up))
        dy_cv = dy_cv + token[0, 0]
    dh, dw_dw, db_dw, dcln_g, dcln_b = _conv_bwd(S["h"], P["w_dw"], P["conv_ln_g"], P["conv_ln_b"],
                                                 S["yc_pre"], dy_cv, dh)
    dh, dlbs, dgw = _hgrn_bwd(S["h"], P["lbs"], P["g_norm_w"], S["o_pre"], S["st_all"], dy_hg, dh)
    dxin = _proj_in_dx(dh, W["w_in"], dz1, alpha)
    dw_in, db_in = _proj_in_dw(S["xin_bf"], dh)
    big = dict(w_in=dw_in, w_abo=dw_abo, w_down=dw_down, w_up=dw_up)
    small = dict(b_in=db_in, lbs=dlbs, g_norm_w=dgw, w_dw=dw_dw, b_dw=db_dw, conv_ln_g=dcln_g,
                 conv_ln_b=dcln_b, b_b=db_b, ln1_g=dln1_g, ln1_b=dln1_b, ln2_g=dln2_g, ln2_b=dln2_b)
    return dxin, big, small


_SMALL = ("b_in", "lb_logits", "g_norm_w", "b_dw", "conv_ln_g", "conv_ln_b", "b_b", "ln1_g", "ln1_b", "ln2_g",
          "ln2_b")


def _pack_small(per_layer, ln0_g, ln0_b, extra_row, D, L):
    rows = []
    for l in range(L):
        for n in _SMALL:
            a = per_layer[n][l]
            if n == "b_in":
                rows.append(a.reshape(N_SEC, D))
            elif n == "g_norm_w":
                rows.append(jnp.pad(a.reshape(1, -1), ((0, 0), (0, D - a.size))))
            else:
                rows.append(a.reshape(1, D))
    rows += [ln0_g.reshape(1, D), ln0_b.reshape(1, D), extra_row]
    buf = jnp.concatenate(rows, axis=0)
    pad = (-buf.shape[0]) % 8
    return jnp.pad(buf, ((0, pad), (0, 0)))


def _unpack_small(buf, D, L, hv):
    out = {n: [] for n in _SMALL}
    r = 0
    for l in range(L):
        for n in _SMALL:
            if n == "b_in":
                out[n].append(buf[r:r + N_SEC].reshape(N_SEC * D))
                r += N_SEC
            elif n == "g_norm_w":
                out[n].append(buf[r, :hv])
                r += 1
            else:
                out[n].append(buf[r])
                r += 1
    res = {n: jnp.stack(v) for n, v in out.items()}
    res["ln0_g"] = buf[r]
    res["ln0_b"] = buf[r + 1]
    return res, r + 2


def kernel(x, ln0_g, ln0_b, w_in, b_in, lb_logits, g_norm_w, w_a, w_dw, b_dw, conv_ln_g, conv_ln_b, w_b, b_b, w_o, ln1_g, ln1_b, w_up, w_down, ln2_g, ln2_b, loss_target, m_ln0_g, m_ln0_b, m_w_in, m_b_in, m_lb_logits, m_g_norm_w, m_w_a, m_w_dw, m_b_dw, m_conv_ln_g, m_conv_ln_b, m_w_b, m_b_b, m_w_o, m_ln1_g, m_ln1_b, m_w_up, m_w_down, m_ln2_g, m_ln2_b, v_ln0_g, v_ln0_b, v_w_in, v_b_in, v_lb_logits, v_g_norm_w, v_w_a, v_w_dw, v_b_dw, v_conv_ln_g, v_conv_ln_b, v_w_b, v_b_b, v_w_o, v_ln1_g, v_ln1_b, v_w_up, v_w_down, v_ln2_g, v_ln2_b):
    L, D = w_in.shape[0], w_in.shape[1]
    T = x.shape[0] * x.shape[1]
    Dn = w_in.shape[2]
    rs = w_a.shape[1]
    rd = w_down.shape[1]
    cu = w_up.shape[2]
    F = rd * N_DEV
    hv = g_norm_w.shape[1]
    alpha = (2 * L) ** 0.25
    my_x, my_y, my_c = lax.axis_index("x"), lax.axis_index("y"), lax.axis_index("c")
    dev_arr = jnp.reshape(4 * my_x + 2 * my_y + my_c, (1,)).astype(jnp.int32)

    o_a, o_b, o_o, o_d = D, D + rs, D + 2 * rs, D + 3 * rs
    taps = jnp.pad(w_dw, ((0, 0), (0, CONV_HALO - CONV_WIDTH), (0, 0))).reshape(L * CONV_HALO, w_dw.shape[2])
    taps_all = _all_gather_small(taps)
    w_dw_full = taps_all.transpose(1, 0, 2).reshape(L, CONV_HALO, D)

    started, gathered = {}, {}

    def start_gather(key, after):
        l, part = key
        rest = [w_a[l], w_b[l], w_o[l], w_down[l]]
        rows = dict(all=[w_in[l]] + rest, rest=rest)
        if part == "in":
            shards = [w_in[l].astype(ACT_DTYPE)]
        else:
            shards = [jnp.concatenate(rows[part], axis=0).astype(ACT_DTYPE),
                      jnp.swapaxes(w_up[l], 0, 1).astype(ACT_DTYPE)]
        started[key] = _gather_start("gather_start_%d_%s" % key, shards, [_place_own(s, dev_arr) for s in shards],
                                     after)
        return started[key][4]

    def finish_gather(key, after):
        send, recv, thru, zone, _ = started[key]
        _, zn = _gather_wait("gather_wait_%d_%s" % key, thru, zone, send, recv, after)
        gathered[key] = _gather_finish(zn)

    def w_in_of(l):
        return gathered[(l, "in") if l == 0 else (l, "all")][0]

    def rest_of(l):
        ga, gb = gathered[(l, "rest") if l == 0 else (l, "all")]
        base = 0 if l == 0 else D
        return dict(
            w_a=ga[:, base:base + rs, :].reshape(D, D),
            w_b=ga[:, base + rs:base + 2 * rs, :].reshape(D, D),
            w_o=ga[:, base + 2 * rs:base + 3 * rs, :].reshape(D, D),
            w_down=ga[:, base + 3 * rs:base + 3 * rs + rd, :].reshape(F, D),
            w_up=gb.reshape(2 * F, D))

    def weights(l):
        return dict(w_in=w_in_of(l), **rest_of(l))

    lbs = _lb_fwd(lb_logits)

    def params(l):
        return dict(b_in=b_in[l].reshape(N_SEC, 1, D), lbs=lbs[l].reshape(1, D), g_norm_w=g_norm_w[l].reshape(1, hv),
                    w_dw=w_dw_full[l], b_dw=b_dw[l].reshape(1, D), conv_ln_g=conv_ln_g[l].reshape(1, D),
                    conv_ln_b=conv_ln_b[l].reshape(1, D), b_b=b_b[l].reshape(1, D), ln1_g=ln1_g[l], ln1_b=ln1_b[l],
                    ln2_g=ln2_g[l], ln2_b=ln2_b[l])

    x2d = x.reshape(T, D)
    token = start_gather((0, "in"), taps_all)
    token = start_gather((0, "rest"), token)
    if L > 1:
        token = start_gather((1, "all"), token)
    xc, xc_bf = _ln_fwd("ln0", x2d, None, 1.0, ln0_g, ln0_b, dep=token)
    finish_gather((0, "in"), xc_bf)
    saved = []
    for l in range(L):
        if l == 0:
            def rest(after):
                finish_gather((0, "rest"), after)
                return rest_of(0)
            token = None
        else:
            rest = functools.partial(lambda after, l: rest_of(l), l=l)
            token = start_gather((l + 1, "all"), gathered[(l, "all")][0]) if l + 1 < L else None
        xc, xc_bf, s = _layer_fwd(xc, xc_bf, w_in_of(l), rest, params(l), alpha, dep=token)
        saved.append(s)
        if l + 1 < L:
            finish_gather((l + 1, "all"), xc_bf)

    c_arr = jnp.reshape(my_c, (1,)).astype(jnp.int32)
    chip = 2 * my_x + my_y
    dx, loss_row = _loss_fwd_bwd(xc, loss_target.reshape(T, D))
    small = [None] * L
    pending = None
    upd_big = {n: None for n in ("w_in", "w_a", "w_b", "w_o", "w_down", "w_up")}
    wmv = dict(w_in=(w_in, m_w_in, v_w_in), w_a=(w_a, m_w_a, v_w_a), w_b=(w_b, m_w_b, v_w_b),
               w_o=(w_o, m_w_o, v_w_o), w_down=(w_down, m_w_down, v_w_down),
               w_up=tuple(jnp.swapaxes(a, 1, 2) for a in (w_up, m_w_up, v_w_up)))

    held = dict(w_in=("w_in",), w_abo=("w_a", "w_b", "w_o"), w_down=("w_down",), w_up=("w_up",))

    def send_buffer(buf, g):
        return g if g.ndim == 3 else g.reshape(N_DEV, wmv[held[buf][0]][0].shape[1], D)

    def update_layer(l, bufs, q, r2):
        pre = jnp.stack([chip, jnp.int32(l)]).astype(jnp.int32)
        for k, buf in enumerate(bufs):
            for pos, name in enumerate(held[buf]):
                w, m, v = wmv[name]
                r, C = w.shape[1], w.shape[2]
                tr = max(t for t in range(16, 513, 16) if r % t == 0)
                nb = r // tr
                b0 = pos * nb
                specs = [pl.BlockSpec((None, tr, C), functools.partial(lambda i, s, b0: (s[0], b0 + i, 0), b0=b0))]
                specs += [pl.BlockSpec((None, tr, C), functools.partial(lambda i, s, j, b0: (j, b0 + i, 0), j=j, b0=b0))
                          for j in range(3)]
                upd_big[name] = _adamw(
                    "adamw_" + name, w.reshape(L * r, C), m.reshape(L * r, C), v.reshape(L * r, C),
                    [q[k], r2[k], r2[k], r2[k]], specs, tr, prefetch=pre, nsteps=nb,
                    row_map=functools.partial(lambda i, s, nb: (s[1] * nb + i, 0), nb=nb), prev=upd_big[name])

    def pair_sums(bufs, grads):
        sends = [send_buffer(n, grads[n]) for n in bufs]
        return [_pair_add(p, r, c_arr) for p, r in zip(sends, _exchange_sibling(sends))]

    def start_reduce(tag, names, qs, after=None):
        s_send, s_recv, q_thru, zones, token = _exchange_chips_start("reduce_start_" + tag, qs, after)
        return (tag, names, (s_send, s_recv), list(q_thru), list(zones)), token

    def finish_reduce(l, handle, after):
        tag, names, sems, q, zones = handle
        r2 = _exchange_chips_wait("reduce_wait_" + tag, q, zones, sems[0], sems[1], after)
        update_layer(l, names, q, r2)

    names_all = tuple(held)
    names_early = names_all[1:]
    token = None
    first_early = []
    for l in range(L - 1, -1, -1):
        if l > 0:
            dx, big, small[l] = _layer_bwd(dx, saved[l], weights(l), params(l), alpha, dep=token)
            if pending is not None:
                finish_reduce(l + 1, pending, dx)
            pending, token = start_reduce(str(l), names_all, pair_sums(names_all, big))
        else:
            def early(grads):
                if pending is not None:
                    finish_reduce(1, pending, grads["w_abo"])
                handle, tok = start_reduce("0_rest", names_early, pair_sums(names_early, grads))
                first_early.append(handle)
                return tok
            dx, big, small[l] = _layer_bwd(dx, saved[l], weights(l), params(l), alpha, dep=token, early=early)
            q_in = pair_sums(("w_in",), big)
    dx0, _, dln0_g, dln0_b = _ln_bwd("ln0_bwd", x2d, dx, ln0_g)
    dlb_logits = _lb_bwd(lb_logits, jnp.concatenate([small[l]["lbs"] for l in range(L)], axis=0))

    small_l = {n: [small[l][n] for l in range(L)] for n in _SMALL if n != "lb_logits"}
    small_l["lb_logits"] = [dlb_logits[l] for l in range(L)]
    loss_pad = jnp.pad(loss_row, ((0, 0), (0, D - LANES)))
    part = jnp.concatenate([_pack_small(small_l, dln0_g, dln0_b, loss_pad, D, L)]
                           + [small[l]["w_dw"] for l in range(L)], axis=0)
    parts_all = _all_gather_small(part)
    n_small = part.shape[0] - L * CONV_HALO

    last, _ = start_reduce("0_in", ("w_in",), q_in, after=parts_all)


    inputs = dict(b_in=(b_in, m_b_in, v_b_in), lb_logits=(lb_logits, m_lb_logits, v_lb_logits),
                  g_norm_w=(g_norm_w, m_g_norm_w, v_g_norm_w), b_dw=(b_dw, m_b_dw, v_b_dw),
                  conv_ln_g=(conv_ln_g, m_conv_ln_g, v_conv_ln_g), conv_ln_b=(conv_ln_b, m_conv_ln_b, v_conv_ln_b),
                  b_b=(b_b, m_b_b, v_b_b), ln1_g=(ln1_g, m_ln1_g, v_ln1_g), ln1_b=(ln1_b, m_ln1_b, v_ln1_b),
                  ln2_g=(ln2_g, m_ln2_g, v_ln2_g), ln2_b=(ln2_b, m_ln2_b, v_ln2_b))
    zero_row = jnp.zeros((1, D), F32)
    packed = [_pack_small({n: [inputs[n][i][l] for l in range(L)] for n in _SMALL},
                          (ln0_g, m_ln0_g, v_ln0_g)[i], (ln0_b, m_ln0_b, v_ln0_b)[i], zero_row, D, L)
              for i in range(3)]
    small_specs = [pl.BlockSpec((None, n_small, D), functools.partial(lambda i, d: (d, 0, 0), d=d))
                   for d in range(N_DEV)]
    s_out = _adamw("adamw_small", packed[0], packed[1], packed[2], [parts_all] * N_DEV, small_specs, n_small)
    s_g, n_rows = _unpack_small(s_out[0], D, L, hv)
    s_d, _ = _unpack_small(s_out[1], D, L, hv)
    s_m, _ = _unpack_small(s_out[2], D, L, hv)
    s_v, _ = _unpack_small(s_out[3], D, L, hv)
    loss = s_out[0][n_rows, 0]

    cw = w_dw.shape[2]
    dev = 4 * my_x + 2 * my_y + my_c
    tap_parts = lax.dynamic_slice_in_dim(parts_all[:, n_small:, :], dev * cw, cw, axis=2)
    tap_specs = [pl.BlockSpec((None, L * CONV_HALO, cw), functools.partial(lambda i, d: (d, 0, 0), d=d))
                 for d in range(N_DEV)]
    pad_t = lambda a: jnp.pad(a, ((0, 0), (0, CONV_HALO - CONV_WIDTH), (0, 0))).reshape(L * CONV_HALO, cw)
    t_out = _adamw("adamw_taps", pad_t(w_dw), pad_t(m_w_dw), pad_t(v_w_dw), [tap_parts] * N_DEV, tap_specs,
                   L * CONV_HALO)
    finish_reduce(0, first_early[0], t_out[0])
    finish_reduce(0, last, upd_big["w_up"][0])
    upd ={n: [o.reshape(wmv[n][0].shape) for o in outs] for n, outs in upd_big.items()}
    upd["w_up"] = [jnp.swapaxes(o, 1, 2) for o in upd["w_up"]]
    upd["w_dw"] = [o.reshape(L, CONV_HALO, cw)[:, :CONV_WIDTH, :] for o in t_out]

    order = ["ln0_g", "ln0_b", "w_in", "b_in", "lb_logits", "g_norm_w", "w_a", "w_dw", "b_dw", "conv_ln_g",
             "conv_ln_b", "w_b", "b_b", "w_o", "ln1_g", "ln1_b", "w_up", "w_down", "ln2_g", "ln2_b"]
    small_sets = (s_g, s_d, s_m, s_v)
    outs = [loss, dx0.reshape(x.shape)]
    for i in range(4):
        for n in order:
            outs.append(upd[n][i] if n in upd else small_sets[i][n])
    return tuple(outs)
```

```python
import functools

import jax
import jax.numpy as jnp
from jax import lax
from jax.experimental import pallas as pl
from jax.experimental.pallas import tpu as pltpu

F32 = jnp.float32
MXU_DTYPE = jnp.bfloat16
ACT_DTYPE = jnp.bfloat16

LANES = 128
SUB = 8
N_DEV = 8
N_SEC = 8
CONV_WIDTH = 31
CONV_HALO = 32
HG_C = 16
LN_EPS = 1e-5
RMS_EPS = 1e-6
F_MIN = 1e-30
LOG2E = 1.4426950408889634
ADAM_LR = 0.001
ADAM_B1 = 0.9
ADAM_B2 = 0.999
ADAM_EPS = 1e-08
ADAM_WD = 0.01
ADAM_STEP = 10
VMEM_LIMIT = 56 * 1024 * 1024
MESH = pl.DeviceIdType.MESH

_NN = (((1,), (0,)), ((), ()))
_NT = (((1,), (1,)), ((), ()))
_TN = (((0,), (0,)), ((), ()))


_ANY = pl.BlockSpec(memory_space=pl.ANY)
_HBM = pl.BlockSpec(memory_space=pltpu.HBM)
_SEM = pl.BlockSpec(memory_space=pltpu.SEMAPHORE)
_EFFECT = pltpu.SideEffectType.DATAFLOW_SIDE_EFFECTING


def _cp(*sem):
    return pltpu.CompilerParams(dimension_semantics=tuple(sem), vmem_limit_bytes=VMEM_LIMIT)


def _pick(n, cands):
    for c in cands:
        if c <= n and n % c == 0:
            return c
    return n


def _silu(x):
    return x * jax.nn.sigmoid(x)


def _dsilu(x):
    s = jax.nn.sigmoid(x)
    return s * (1.0 + x * (1.0 - s))


def _matmul(name, a, b, *, dims, grid, a_spec, b_spec, out_shape, out_spec, acc_shape, nk,
            bias=None, bias_spec=None, add=None, add_spec=None, add_scale=1.0, dep=None):
    has_bias, has_add = bias is not None, add is not None
    kaxis = len(grid) - 1

    def body(*refs):
        a_ref, b_ref = refs[0], refs[1]
        pos = 2
        bias_ref = add_ref = None
        if has_bias:
            bias_ref = refs[pos]
            pos += 1
        if has_add:
            add_ref = refs[pos]
            pos += 1
        if dep is not None:
            pos += 1
        o_ref = refs[pos]
        acc_ref = refs[pos + 1] if nk > 1 else None

        part = lax.dot_general(a_ref[...].astype(MXU_DTYPE), b_ref[...].astype(MXU_DTYPE), dims,
                               preferred_element_type=F32)

        def finish(r):
            if has_bias:
                r = r + bias_ref[...]
            if has_add:
                r = r + add_scale * add_ref[...]
            o_ref[...] = r.astype(o_ref.dtype)

        if nk == 1:
            finish(part)
        else:
            k = pl.program_id(kaxis)

            @pl.when(k == 0)
            def _():
                acc_ref[...] = part

            @pl.when(k > 0)
            def _():
                acc_ref[...] += part

            @pl.when(k == nk - 1)
            def _():
                finish(acc_ref[...])

    ins, specs = [a, b], [a_spec, b_spec]
    if has_bias:
        ins.append(bias)
        specs.append(bias_spec)
    if has_add:
        ins.append(add)
        specs.append(add_spec)
    if dep is not None:
        ins.append(dep)
        specs.append(_ANY)
    sem =("parallel",) * (len(grid) - 1) + ("arbitrary",) if nk > 1 else ("parallel",) * len(grid)
    return pl.pallas_call(
        body, name=name, grid=grid, in_specs=specs, out_specs=out_spec, out_shape=out_shape,
        scratch_shapes=[pltpu.VMEM(acc_shape, F32)] if nk > 1 else [],
        compiler_params=_cp(*sem))(*ins)


def _mm_tn(name, a, b, out_dtype):
    K, M = a.shape
    N = b.shape[1]
    tm = _pick(M, (256, 128))
    return _matmul(
        name, a, b, dims=_TN, grid=(M // tm,),
        a_spec=pl.BlockSpec((K, tm), lambda i: (0, i)),
        b_spec=pl.BlockSpec((K, N), lambda i: (0, 0)),
        out_shape=jax.ShapeDtypeStruct((M, N), out_dtype),
        out_spec=pl.BlockSpec((tm, N), lambda i: (i, 0)),
        acc_shape=(tm, N), nk=1)


def _branch_dw(pairs, rs):
    T, D = pairs[0][0].shape
    nslot = max(1, LANES // rs)
    tm = nslot * rs
    nk = len(pairs)

    def body(*refs):
        o_ref = refs[-1]
        k = pl.program_id(0)
        for kk in range(nk):
            @pl.when(k == kk)
            def _():
                r = lax.dot_general(refs[2 * kk][...].astype(MXU_DTYPE), refs[2 * kk + 1][...].astype(MXU_DTYPE),
                                    _TN, preferred_element_type=F32)
                o_ref[...] = r.astype(o_ref.dtype).reshape(nslot, rs, D)

    in_specs, ins = [], []
    for kk, (a, b) in enumerate(pairs):
        in_specs.append(pl.BlockSpec((T, tm), functools.partial(lambda k, i, kk: (0, jnp.where(k == kk, i, 0)), kk=kk)))
        in_specs.append(pl.BlockSpec((T, D), lambda k, i: (0, 0)))
        ins += [a, b]
    return pl.pallas_call(
        body, name="branch_dw", grid=(nk, N_DEV // nslot), in_specs=in_specs,
        out_specs=pl.BlockSpec((nslot, rs, D), lambda k, i: (i, k, 0)),
        out_shape=jax.ShapeDtypeStruct((N_DEV, nk * rs, D), ACT_DTYPE),
        compiler_params=_cp("arbitrary", "arbitrary"))(*ins)


def _proj_in(x_bf, w_in, b_in, dep=None):
    T, D = x_bf.shape
    tn = _pick(D, (512, 256, 128))
    return _matmul(
        "proj_in", x_bf, w_in, dims=_NN, grid=(N_SEC, D // tn),
        a_spec=pl.BlockSpec((T, D), lambda s, j: (0, 0)),
        b_spec=pl.BlockSpec((None, D, tn), lambda s, j: (s, 0, j)),
        out_shape=jax.ShapeDtypeStruct((N_SEC, T, D), F32),
        out_spec=pl.BlockSpec((None, T, tn), lambda s, j: (s, 0, j)),
        acc_shape=(T, tn), nk=1,
        bias=b_in, bias_spec=pl.BlockSpec((None, 1, tn), lambda s, j: (s, 0, j)), dep=dep)


def _proj_in_dx(dh, w_in, add, add_scale):
    _, T, D = dh.shape
    tm = _pick(T, (256, 128, 64, 32, 16))

    def body(dh_ref, w_ref, add_ref, o_ref):
        acc = add_scale * add_ref[...]
        for s in range(N_SEC):
            acc = acc + lax.dot_general(dh_ref[s].astype(MXU_DTYPE), w_ref[s].astype(MXU_DTYPE), _NT,
                                        preferred_element_type=F32)
        o_ref[...] = acc

    row = pl.BlockSpec((tm, D), lambda i: (i, 0))
    return pl.pallas_call(
        body, name="proj_in_dx", grid=(T // tm,),
        in_specs=[pl.BlockSpec((N_SEC, tm, D), lambda i: (0, i, 0)),
                  pl.BlockSpec((N_SEC, D, D), lambda i: (0, 0, 0), pipeline_mode=pl.Buffered(1)), row],
        out_specs=row, out_shape=jax.ShapeDtypeStruct((T, D), F32),
        compiler_params=_cp("parallel"))(dh, w_in, add)


def _proj_in_dw(x_bf, dh):
    _, T, D = dh.shape
    tn = _pick(D, (512, 256, 128))

    def body(x_ref, dh_ref, dw_ref, db_ref):
        dhv = dh_ref[...]
        dw_ref[...] = lax.dot_general(x_ref[...].astype(MXU_DTYPE), dhv.astype(MXU_DTYPE), _TN,
                                      preferred_element_type=F32).astype(dw_ref.dtype)
        db_ref[...] = jnp.sum(dhv.astype(F32), axis=0, keepdims=True)

    return pl.pallas_call(
        body, name="proj_in_dw", grid=(N_SEC, D // tn),
        in_specs=[pl.BlockSpec((T, D), lambda s, j: (0, 0)), pl.BlockSpec((None, T, tn), lambda s, j: (s, 0, j))],
        out_specs=[pl.BlockSpec((None, D, tn), lambda s, j: (s, 0, j)),
                   pl.BlockSpec((None, 1, tn), lambda s, j: (s, 0, j))],
        out_shape=[jax.ShapeDtypeStruct((N_SEC, D, D), ACT_DTYPE), jax.ShapeDtypeStruct((N_SEC, 1, D), F32)],
        compiler_params=_cp("parallel", "parallel"))(x_bf, dh)


def _ffn_up_dx(dup, w_up_t, add, add_scale):
    _, T, F = dup.shape
    D = w_up_t.shape[1]
    tm = _pick(T, (256, 128, 64, 32, 16))

    def body(dup_ref, w_ref, add_ref, o_ref):
        acc = add_scale * add_ref[...]
        for p in range(2):
            acc = acc + jnp.dot(dup_ref[p].astype(MXU_DTYPE), w_ref[pl.ds(p * F, F), :].astype(MXU_DTYPE),
                                preferred_element_type=F32)
        o_ref[...] = acc

    row = pl.BlockSpec((tm, D), lambda i: (i, 0))
    return pl.pallas_call(
        body, name="ffn_up_dx", grid=(T // tm,),
        in_specs=[pl.BlockSpec((2, tm, F), lambda i: (0, i, 0)),
                  pl.BlockSpec((2 * F, D), lambda i: (0, 0), pipeline_mode=pl.Buffered(1)), row],
        out_specs=row, out_shape=jax.ShapeDtypeStruct((T, D), F32),
        compiler_params=_cp("parallel"))(dup, w_up_t, add)


def _ffn_up_dw(x_bf, dup):
    _, T, F = dup.shape
    D = x_bf.shape[1]
    tm = _pick(F, (1408, 256, 128))
    nb = F // tm
    return _matmul(
        "ffn_up_dw", dup, x_bf, dims=_TN, grid=(2, nb),
        a_spec=pl.BlockSpec((None, T, tm), lambda p, j: (p, 0, j)),
        b_spec=pl.BlockSpec((T, D), lambda p, j: (0, 0)),
        out_shape=jax.ShapeDtypeStruct((2 * F, D), ACT_DTYPE),
        out_spec=pl.BlockSpec((tm, D), lambda p, j: (p * nb + j, 0)),
        acc_shape=(tm, D), nk=1)


def _ln_fwd(name, a, res, alpha, g, b, dep=None):
    T, D = a.shape
    tr = _pick(T, (256, 128, 64, 32, 16))
    has_res = res is not None

    def body(*refs):
        if has_res:
            a_ref, r_ref, g_ref, b_ref = refs[:4]
            y_ref, yb_ref, z_ref = refs[-3:]
            z = alpha * a_ref[...] + r_ref[...]
            z_ref[...] = z
        else:
            a_ref, g_ref, b_ref = refs[:3]
            y_ref, yb_ref = refs[-2:]
            z = a_ref[...]
        mu = jnp.mean(z, axis=-1, keepdims=True)
        zc = z - mu
        var = jnp.mean(zc * zc, axis=-1, keepdims=True)
        y = zc * lax.rsqrt(var + LN_EPS) * g_ref[...] + b_ref[...]
        y_ref[...] = y
        yb_ref[...] = y.astype(ACT_DTYPE)

    row = pl.BlockSpec((tr, D), lambda i: (i, 0))
    vec = pl.BlockSpec((1, D), lambda i: (0, 0))
    ins = [a] + ([res] if has_res else []) + [g.reshape(1, D), b.reshape(1, D)]
    in_specs = [row] + ([row] if has_res else []) + [vec, vec]
    if dep is not None:
        ins.append(dep)
        in_specs.append(_ANY)
    out_shape = [jax.ShapeDtypeStruct((T, D), F32), jax.ShapeDtypeStruct((T, D), ACT_DTYPE)]
    if has_res:
        out_shape.append(jax.ShapeDtypeStruct((T, D), F32))
    return pl.pallas_call(
        body, name=name, grid=(T // tr,), in_specs=in_specs,
        out_specs=[row] * len(out_shape), out_shape=out_shape, compiler_params=_cp("parallel"))(*ins)


def _ln_bwd(name, z, dy, g, dep=None):
    T, D = z.shape
    tr = _pick(T, (256, 128, 64, 32, 16))

    def body(z_ref, dy_ref, g_ref, *rest):
        dz_ref, dzb_ref, dg_ref, db_ref = rest[-4:]

        @pl.when(pl.program_id(0) == 0)
        def _():
            dg_ref[...] = jnp.zeros_like(dg_ref)
            db_ref[...] = jnp.zeros_like(db_ref)

        zv = z_ref[...]
        dy_ = dy_ref[...]
        mu = jnp.mean(zv, axis=-1, keepdims=True)
        zc = zv - mu
        rstd = lax.rsqrt(jnp.mean(zc * zc, axis=-1, keepdims=True) + LN_EPS)
        xhat = zc * rstd
        dxh = dy_ * g_ref[...]
        dz = rstd * (dxh - jnp.mean(dxh, axis=-1, keepdims=True)
                     - xhat * jnp.mean(dxh * xhat, axis=-1, keepdims=True))
        dz_ref[...] = dz
        dzb_ref[...] = dz.astype(ACT_DTYPE)
        dg_ref[...] += jnp.sum(dy_ * xhat, axis=0, keepdims=True)
        db_ref[...] += jnp.sum(dy_, axis=0, keepdims=True)

    row = pl.BlockSpec((tr, D), lambda i: (i, 0))
    vec = pl.BlockSpec((1, D), lambda i: (0, 0))
    ins, in_specs = [z, dy, g.reshape(1, D)], [row, row, vec]
    if dep is not None:
        ins.append(dep)
        in_specs.append(_ANY)
    return pl.pallas_call(
        body, name=name, grid=(T // tr,), in_specs=in_specs, out_specs=[row, row, vec, vec],
        out_shape=[jax.ShapeDtypeStruct((T, D), F32), jax.ShapeDtypeStruct((T, D), ACT_DTYPE),
                   jax.ShapeDtypeStruct((1, D), F32), jax.ShapeDtypeStruct((1, D), F32)],
        compiler_params=_cp("arbitrary"))(*ins)


def _loss_fwd_bwd(y, target):
    T, D = y.shape
    tr = _pick(T, (256, 128, 64, 32, 16))

    def body(y_ref, t_ref, dy_ref, l_ref):
        @pl.when(pl.program_id(0) == 0)
        def _():
            l_ref[...] = jnp.zeros_like(l_ref)

        e = y_ref[...] - t_ref[...]
        dy_ref[...] = e * (1.0 / D)
        row = jnp.sum(e * e, axis=-1, keepdims=True) * (1.0 / D)
        l_ref[...] += 0.5 * jnp.sum(row, axis=0, keepdims=True)

    rowspec = pl.BlockSpec((tr, D), lambda i: (i, 0))
    return pl.pallas_call(
        body, name="loss", grid=(T // tr,), in_specs=[rowspec, rowspec],
        out_specs=[rowspec, pl.BlockSpec((1, LANES), lambda i: (0, 0))],
        out_shape=[jax.ShapeDtypeStruct((T, D), F32), jax.ShapeDtypeStruct((1, LANES), F32)],
        compiler_params=_cp("arbitrary"))(y, target)


def _layer_norm_rows(z, g, b):
    mu = jnp.mean(z, axis=-1, keepdims=True)
    zc = z - mu
    var = jnp.mean(zc * zc, axis=-1, keepdims=True)
    return zc * lax.rsqrt(var + LN_EPS) * g + b


def _mixer_out(y_hg, y_cv, h, xin, w_a, w_b, b_b, w_o, alpha, ln_g, ln_b):
    T, D = xin.shape
    tm = _pick(T, (256, 128, 64, 32, 16))

    def body(yhg_ref, ycv_ref, gh_ref, gc_ref, x_ref, wa_ref, wb_ref, bb_ref, wo_ref, g_ref, b_ref,
             yh_ref, yc_ref, m_ref, x1_ref, x1b_ref, z_ref):
        y_h = jnp.dot(yhg_ref[...].astype(MXU_DTYPE), wa_ref[...].astype(MXU_DTYPE), preferred_element_type=F32)
        y_c = jnp.dot(ycv_ref[...].astype(MXU_DTYPE), wb_ref[...].astype(MXU_DTYPE),
                      preferred_element_type=F32) + bb_ref[...]
        yh_ref[...] = y_h
        yc_ref[...] = y_c
        merged = (jax.nn.sigmoid(gh_ref[...]) * y_h + jax.nn.sigmoid(gc_ref[...]) * y_c).astype(ACT_DTYPE)
        m_ref[...] = merged
        z = alpha * x_ref[...] + jnp.dot(merged.astype(MXU_DTYPE), wo_ref[...].astype(MXU_DTYPE),
                                         preferred_element_type=F32)
        z_ref[...] = z
        x1 = _layer_norm_rows(z, g_ref[...], b_ref[...])
        x1_ref[...] = x1
        x1b_ref[...] = x1.astype(ACT_DTYPE)

    row = pl.BlockSpec((tm, D), lambda i: (i, 0))
    mat = pl.BlockSpec((D, D), lambda i: (0, 0))
    vec = pl.BlockSpec((1, D), lambda i: (0, 0))
    f32, act = jax.ShapeDtypeStruct((T, D), F32), jax.ShapeDtypeStruct((T, D), ACT_DTYPE)
    return pl.pallas_call(
        body, name="mixer_out", grid=(T // tm,),
        in_specs=[row, row, pl.BlockSpec((None, tm, D), lambda i: (6, i, 0)),
                  pl.BlockSpec((None, tm, D), lambda i: (7, i, 0)), row, mat, mat, vec, mat, vec, vec],
        out_specs=[row] * 6, out_shape=[f32, f32, act, f32, act, f32],
        compiler_params=_cp("parallel"))(y_hg, y_cv, h, h, xin, w_a, w_b, b_b, w_o, ln_g.reshape(1, D),
                                         ln_b.reshape(1, D))


def _ffn_up_swiglu(x_bf, w_up_t):
    T, D = x_bf.shape
    F = w_up_t.shape[0] // 2
    tn = _pick(F, (256, 128))
    nb = F // tn

    def body(x_ref, wg_ref, wv_ref, up_ref, act_ref):
        xv = x_ref[...].astype(MXU_DTYPE)
        g = lax.dot_general(xv, wg_ref[...].astype(MXU_DTYPE), _NT, preferred_element_type=F32)
        v = lax.dot_general(xv, wv_ref[...].astype(MXU_DTYPE), _NT, preferred_element_type=F32)
        up_ref[0] = g
        up_ref[1] = v
        act_ref[...] = (_silu(g) * v).astype(ACT_DTYPE)

    return pl.pallas_call(
        body, name="ffn_up", grid=(nb,),
        in_specs=[pl.BlockSpec((T, D), lambda j: (0, 0)), pl.BlockSpec((tn, D), lambda j: (j, 0)),
                  pl.BlockSpec((tn, D), lambda j: (nb + j, 0))],
        out_specs=[pl.BlockSpec((2, T, tn), lambda j: (0, 0, j)), pl.BlockSpec((T, tn), lambda j: (0, j))],
        out_shape=[jax.ShapeDtypeStruct((2, T, F), F32), jax.ShapeDtypeStruct((T, F), ACT_DTYPE)],
        compiler_params=_cp("parallel"))(x_bf, w_up_t, w_up_t)


def _ffn_down_ln2(act, w_down, x1, alpha, ln_g, ln_b):
    T, D = x1.shape
    F = act.shape[1]
    tm = _pick(T, (256, 128, 64, 32, 16))

    def body(a_ref, w_ref, x_ref, g_ref, b_ref, x2_ref, x2b_ref, z_ref):
        z = alpha * x_ref[...] + jnp.dot(a_ref[...].astype(MXU_DTYPE), w_ref[...].astype(MXU_DTYPE),
                                         preferred_element_type=F32)
        z_ref[...] = z
        x2 = _layer_norm_rows(z, g_ref[...], b_ref[...])
        x2_ref[...] = x2
        x2b_ref[...] = x2.astype(ACT_DTYPE)

    row = pl.BlockSpec((tm, D), lambda i: (i, 0))
    vec = pl.BlockSpec((1, D), lambda i: (0, 0))
    f32, actt = jax.ShapeDtypeStruct((T, D), F32), jax.ShapeDtypeStruct((T, D), ACT_DTYPE)
    return pl.pallas_call(
        body, name="ffn_down", grid=(T // tm,),
        in_specs=[pl.BlockSpec((tm, F), lambda i: (i, 0)), pl.BlockSpec((F, D), lambda i: (0, 0)), row, vec, vec],
        out_specs=[row] * 3, out_shape=[f32, actt, f32],
        compiler_params=_cp("parallel"))(act, w_down, x1, ln_g.reshape(1, D), ln_b.reshape(1, D))


def _ln2_ffn_down_bwd(z, dy, ln_g, w_down, up, dep=None):
    T, D = z.shape
    F = w_down.shape[0]
    tm = _pick(T, (256, 128, 64, 32, 16))

    def body(z_ref, dy_ref, g_ref, w_ref, up_ref, *rest):
        dz_ref, dzb_ref, dup_ref, dg_ref, db_ref = rest[-5:]

        @pl.when(pl.program_id(0) == 0)
        def _():
            dg_ref[...] = jnp.zeros_like(dg_ref)
            db_ref[...] = jnp.zeros_like(db_ref)

        zv = z_ref[...]
        dy_ = dy_ref[...]
        mu = jnp.mean(zv, axis=-1, keepdims=True)
        zc = zv - mu
        rstd = lax.rsqrt(jnp.mean(zc * zc, axis=-1, keepdims=True) + LN_EPS)
        xhat = zc * rstd
        dxh = dy_ * g_ref[...]
        dz = rstd * (dxh - jnp.mean(dxh, axis=-1, keepdims=True)
                     - xhat * jnp.mean(dxh * xhat, axis=-1, keepdims=True))
        dz_ref[...] = dz
        dzb = dz.astype(ACT_DTYPE)
        dzb_ref[...] = dzb
        dg_ref[...] += jnp.sum(dy_ * xhat, axis=0, keepdims=True)
        db_ref[...] += jnp.sum(dy_, axis=0, keepdims=True)
        da = lax.dot_general(dzb.astype(MXU_DTYPE), w_ref[...].astype(MXU_DTYPE), _NT, preferred_element_type=F32)
        ug = up_ref[0]
        dup_ref[0] = (da * up_ref[1] * _dsilu(ug)).astype(ACT_DTYPE)
        dup_ref[1] = (da * _silu(ug)).astype(ACT_DTYPE)

    row = pl.BlockSpec((tm, D), lambda i: (i, 0))
    vec = pl.BlockSpec((1, D), lambda i: (0, 0))
    blk = pl.BlockSpec((2, tm, F), lambda i: (0, i, 0))
    ins = [z, dy, ln_g.reshape(1, D), w_down, up]
    in_specs = [row, row, vec, pl.BlockSpec((F, D), lambda i: (0, 0)), blk]
    if dep is not None:
        ins.append(dep)
        in_specs.append(_ANY)
    v32 = jax.ShapeDtypeStruct((1, D), F32)
    return pl.pallas_call(
        body, name="ln2_ffn_down_bwd", grid=(T // tm,), in_specs=in_specs,
        out_specs=[row, row, blk, vec, vec],
        out_shape=[jax.ShapeDtypeStruct((T, D), F32), jax.ShapeDtypeStruct((T, D), ACT_DTYPE),
                   jax.ShapeDtypeStruct((2, T, F), ACT_DTYPE), v32, v32],
        compiler_params=_cp("arbitrary"))(*ins)


def _mixer_out_bwd(z, dx1, y_h, y_c, h, w_a, w_b, w_o, ln_g):
    T, D = z.shape
    tm = _pick(T, (256, 128, 64, 32, 16))

    def body(z_ref, dx_ref, yh_ref, yc_ref, gh_ref, gc_ref, wa_ref, wb_ref, wo_ref, g_ref,
             dz_ref, dzb_ref, dyh_ref, dyc_ref, dyhg_ref, dycv_ref, dh_ref, dg_ref, db_ref, dbb_ref):
        @pl.when(pl.program_id(0) == 0)
        def _():
            dg_ref[...] = jnp.zeros_like(dg_ref)
            db_ref[...] = jnp.zeros_like(db_ref)
            dbb_ref[...] = jnp.zeros_like(dbb_ref)

        zv = z_ref[...]
        dy_ = dx_ref[...]
        mu = jnp.mean(zv, axis=-1, keepdims=True)
        zc = zv - mu
        rstd = lax.rsqrt(jnp.mean(zc * zc, axis=-1, keepdims=True) + LN_EPS)
        xhat = zc * rstd
        dxh = dy_ * g_ref[...]
        dz = rstd * (dxh - jnp.mean(dxh, axis=-1, keepdims=True)
                     - xhat * jnp.mean(dxh * xhat, axis=-1, keepdims=True))
        dz_ref[...] = dz
        dzb = dz.astype(ACT_DTYPE)
        dzb_ref[...] = dzb
        dg_ref[...] += jnp.sum(dy_ * xhat, axis=0, keepdims=True)
        db_ref[...] += jnp.sum(dy_, axis=0, keepdims=True)
        dm_ = lax.dot_general(dzb.astype(MXU_DTYPE), wo_ref[...].astype(MXU_DTYPE), _NT, preferred_element_type=F32)
        sh = jax.nn.sigmoid(gh_ref[...])
        sc = jax.nn.sigmoid(gc_ref[...])
        dyc = dm_ * sc
        dyh_b = (dm_ * sh).astype(ACT_DTYPE)
        dyc_b = dyc.astype(ACT_DTYPE)
        dyh_ref[...] = dyh_b
        dyc_ref[...] = dyc_b
        dbb_ref[...] += jnp.sum(dyc, axis=0, keepdims=True)
        dh_ref[0] = (dm_ * yh_ref[...] * sh * (1.0 - sh)).astype(ACT_DTYPE)
        dh_ref[1] = (dm_ * yc_ref[...] * sc * (1.0 - sc)).astype(ACT_DTYPE)
        dyhg_ref[...] = lax.dot_general(dyh_b.astype(MXU_DTYPE), wa_ref[...].astype(MXU_DTYPE), _NT,
                                        preferred_element_type=F32)
        dycv_ref[...] = lax.dot_general(dyc_b.astype(MXU_DTYPE), wb_ref[...].astype(MXU_DTYPE), _NT,
                                        preferred_element_type=F32)

    row = pl.BlockSpec((tm, D), lambda i: (i, 0))
    mat = pl.BlockSpec((D, D), lambda i: (0, 0))
    vec = pl.BlockSpec((1, D), lambda i: (0, 0))
    f32, act = jax.ShapeDtypeStruct((T, D), F32), jax.ShapeDtypeStruct((T, D), ACT_DTYPE)
    v32 = jax.ShapeDtypeStruct((1, D), F32)
    return pl.pallas_call(
        body, name="mixer_out_bwd", grid=(T // tm,),
        in_specs=[row, row, row, row, pl.BlockSpec((None, tm, D), lambda i: (6, i, 0)),
                  pl.BlockSpec((None, tm, D), lambda i: (7, i, 0)), mat, mat, mat, vec],
        out_specs=[row, row, row, row, row, row, pl.BlockSpec((2, tm, D), lambda i: (3, i, 0)), vec, vec, vec],
        out_shape=[f32, act, act, act, f32, f32, jax.ShapeDtypeStruct((N_SEC, T, D), ACT_DTYPE), v32, v32, v32],
        compiler_params=_cp("arbitrary"))(z, dx1, y_h, y_c, h, h, w_a, w_b, w_o, ln_g.reshape(1, D))


def _lb_softmax(x):
    L = x.shape[0]
    rows = [x[l:l + 1] for l in range(L)]
    m = rows[0]
    for r in rows[1:]:
        m = jnp.maximum(m, r)
    e = [jnp.exp(r - m) for r in rows]
    s = e[0]
    for r in e[1:]:
        s = s + r
    return [r / s for r in e]


def _lb_fwd(lb_logits):
    L, D = lb_logits.shape

    def body(x_ref, o_ref):
        p = _lb_softmax(x_ref[...])
        run = jnp.zeros_like(p[0])
        for l in range(L):
            if l > 0:
                run = run + p[l]
            o_ref[pl.ds(l, 1), :] = run

    return pl.pallas_call(body, name="lb_fwd", out_shape=jax.ShapeDtypeStruct((L, D), F32))(lb_logits)


def _lb_bwd(lb_logits, dlbs):
    L, D = lb_logits.shape

    def body(x_ref, d_ref, o_ref):
        p = _lb_softmax(x_ref[...])
        d = d_ref[...]
        dp = [jnp.zeros_like(p[0]) for _ in range(L)]
        run = jnp.zeros_like(p[0])
        for j in range(L - 1, 0, -1):
            run = run + d[j:j + 1]
            dp[j] = run
        dot = dp[0] * p[0]
        for j in range(1, L):
            dot = dot + dp[j] * p[j]
        for j in range(L):
            o_ref[pl.ds(j, 1), :] = p[j] * (dp[j] - dot)

    return pl.pallas_call(body, name="lb_bwd", out_shape=jax.ShapeDtypeStruct((L, D), F32))(lb_logits, dlbs)


def _blk_cumsum(x, c, reverse=False):
    n = x.shape[0]
    pos = lax.broadcasted_iota(jnp.int32, x.shape, 0) % c
    s = 1
    while s < c:
        if reverse:
            shifted = pltpu.roll(x, n - s, 0)
            x = x + jnp.where(pos + s < c, shifted, 0.0)
        else:
            shifted = pltpu.roll(x, s, 0)
            x = x + jnp.where(pos >= s, shifted, 0.0)
        s *= 2
    return x


def _hgrn_prologue(q_ref, f_ref, lb_ref):
    lbv = lb_ref[...]
    z = f_ref[...]
    sig = jax.nn.sigmoid(z)
    one_m = 1.0 - lbv
    f = lbv + one_m * sig
    logf = jnp.log(jnp.maximum(f, F_MIN))
    k = one_m * jax.nn.sigmoid(-z)
    q = _silu(q_ref[...])
    return q, k, logf, f, sig, one_m


def _hgrn_fwd(h, lbs_l, gw):
    _, T, D = h.shape
    nh = D // LANES
    c = HG_C
    Tt = _pick(T, (512, 256, 128, 64, 32, 16))
    nb = Tt // c
    ng = c // SUB

    def body(q_ref, f_ref, i_ref, g_ref, lb_ref, gw_ref, o_ref, y_ref, sall_ref,
             st_ref, G_s, q_s, k_s, W_s, R_s, dS_s, o_s):
        @pl.when(pl.program_id(1) == 0)
        def _():
            st_ref[...] = jnp.zeros_like(st_ref)

        q, k, logf, _, _, _ = _hgrn_prologue(q_ref, f_ref, lb_ref)
        G_s[...] = _blk_cumsum(logf, c) * LOG2E
        q_s[...] = q
        k_s[...] = k
        ones = jnp.ones((LANES, LANES), MXU_DTYPE)
        rowid = lax.broadcasted_iota(jnp.int32, (SUB, LANES), 0)
        zero = jnp.zeros((SUB, LANES), F32)
        for bi in range(nb):
            r0 = bi * c
            glast = G_s[pl.ds(r0 + c - 1, 1), :]
            kd = k_s[pl.ds(r0, c), :] * jnp.exp2(glast - G_s[pl.ds(r0, c), :])
            dS_s[bi] = lax.dot_general(i_ref[pl.ds(r0, c), :].astype(MXU_DTYPE), kd.astype(MXU_DTYPE), _TN,
                                       preferred_element_type=F32)
        st = st_ref[...]
        for bi in range(nb):
            sall_ref[bi] = st
            st = st * jnp.exp2(G_s[pl.ds(bi * c + c - 1, 1), :]) + dS_s[bi]
        st_ref[...] = st
        for bi in range(nb):
            r0 = bi * c
            qd = q_s[pl.ds(r0, c), :] * jnp.exp2(G_s[pl.ds(r0, c), :])
            o_s[pl.ds(r0, c), :] = lax.dot_general(qd.astype(MXU_DTYPE), sall_ref[bi].astype(MXU_DTYPE), _NT,
                                                   preferred_element_type=F32)
        for bi in range(nb):
            r0 = bi * c
            w0 = bi * c * c
            Gg = [G_s[pl.ds(r0 + gi * SUB, SUB), :] for gi in range(ng)]
            qg = [q_s[pl.ds(r0 + gi * SUB, SUB), :] for gi in range(ng)]
            for s in range(c):
                gs = G_s[pl.ds(r0 + s, 1), :]
                ks = k_s[pl.ds(r0 + s, 1), :]
                parts = []
                for gi in range(ng):
                    if gi < s // SUB:
                        parts.append(zero)
                        continue
                    e = jnp.exp2(jnp.minimum(Gg[gi] - gs, 0.0))
                    if gi == s // SUB:
                        e = jnp.where(rowid >= s - gi * SUB, e, 0.0)
                    parts.append(e * qg[gi] * ks)
                W_s[pl.ds(w0 + s * c, c), :] = jnp.concatenate(parts, axis=0).astype(MXU_DTYPE)
        R_s[...] = jnp.dot(W_s[...], ones, preferred_element_type=F32)
        for bi in range(nb):
            r0 = bi * c
            w0 = bi * c * c
            acc = [o_s[pl.ds(r0 + gi * SUB, SUB), :] for gi in range(ng)]
            for s in range(c):
                vs = i_ref[pl.ds(r0 + s, 1), :]
                for gi in range(s // SUB, ng):
                    acc[gi] = acc[gi] + R_s[pl.ds(w0 + s * c + gi * SUB, SUB), :] * vs
            o_s[pl.ds(r0, c), :] = jnp.concatenate(acc, axis=0)
        o = o_s[...]
        n = o * lax.rsqrt(jnp.mean(o * o, axis=-1, keepdims=True) + RMS_EPS)
        o_ref[...] = o
        y_ref[...] = (n * gw_ref[...] * _silu(g_ref[...])).astype(ACT_DTYPE)

    def sec(s):
        return pl.BlockSpec((None, Tt, LANES), lambda hd, i: (s, i, hd))

    col = pl.BlockSpec((Tt, LANES), lambda hd, i: (i, hd))
    return pl.pallas_call(
        body, name="hgrn_fwd", grid=(nh, T // Tt),
        in_specs=[sec(0), sec(1), sec(2), sec(3), pl.BlockSpec((1, LANES), lambda hd, i: (0, hd)),
                  pl.BlockSpec((1, LANES), lambda hd, i: (0, 0))],
        out_specs=[col, col, pl.BlockSpec((nb, None, LANES, LANES), lambda hd, i: (i, hd, 0, 0))],
        out_shape=[jax.ShapeDtypeStruct((T, D), F32), jax.ShapeDtypeStruct((T, D), ACT_DTYPE),
                   jax.ShapeDtypeStruct((T // c, nh, LANES, LANES), F32)],
        scratch_shapes=[pltpu.VMEM((LANES, LANES), F32), pltpu.VMEM((Tt, LANES), F32),
                        pltpu.VMEM((Tt, LANES), F32), pltpu.VMEM((Tt, LANES), F32),
                        pltpu.VMEM((nb * c * c, LANES), MXU_DTYPE), pltpu.VMEM((nb * c * c, LANES), F32),
                        pltpu.VMEM((nb, LANES, LANES), F32), pltpu.VMEM((Tt, LANES), F32)],
        compiler_params=_cp("parallel", "arbitrary"))(h, h, h, h, lbs_l, gw)


def _hgrn_bwd(h, lbs_l, gw, o_pre, st_all, dy, dh):
    _, T, D = h.shape
    nh = D // LANES
    c = HG_C
    Tt = _pick(T, (512, 256, 128, 64, 32, 16))
    nb = Tt // c
    ng = c // SUB
    nT = T // Tt

    def body(q_ref, f_ref, i_ref, g_ref, lb_ref, gw_ref, o_ref, sall_ref, dy_ref, dh_in_ref,
             dh_ref, dlb_ref, dgw_ref,
             dst_ref, G_s, q_s, k_s, do_s, E_s, WP_s, dq_s, dk_s, dv_s, dG_s,
             R_s, dS_s, dstA_s, dqd_s, dkd_s, dvi_s, da_s):
        del dh_in_ref
        hd, ti = pl.program_id(0), pl.program_id(1)

        @pl.when(ti == 0)
        def _():
            dst_ref[...] = jnp.zeros_like(dst_ref)
            dlb_ref[...] = jnp.zeros_like(dlb_ref)

        @pl.when((ti == 0) & (hd == 0))
        def _():
            dgw_ref[...] = jnp.zeros_like(dgw_ref)

        q, k, logf, f, sig, one_m = _hgrn_prologue(q_ref, f_ref, lb_ref)
        G_s[...] = _blk_cumsum(logf, c) * LOG2E
        q_s[...] = q
        k_s[...] = k

        o = o_ref[...]
        gr = g_ref[...]
        dy_ = dy_ref[...]
        rr = lax.rsqrt(jnp.mean(o * o, axis=-1, keepdims=True) + RMS_EPS)
        n = o * rr
        sg = _silu(gr)
        gwv = gw_ref[...]
        dh_ref[3] = (dy_ * n * gwv * _dsilu(gr)).astype(ACT_DTYPE)
        dgw_ref[...] += jnp.sum(dy_ * n * sg, axis=0, keepdims=True)
        dn = dy_ * gwv * sg
        do_s[...] = rr * (dn - n * jnp.mean(dn * n, axis=-1, keepdims=True))

        ones = jnp.ones((LANES, LANES), MXU_DTYPE)
        rowid = lax.broadcasted_iota(jnp.int32, (SUB, LANES), 0)
        rowid_c = lax.broadcasted_iota(jnp.int32, (c, LANES), 0)
        zero = jnp.zeros((SUB, LANES), F32)
        cc = c * c
        for bi in range(nb):
            r0 = bi * c
            qd = q_s[pl.ds(r0, c), :] * jnp.exp2(G_s[pl.ds(r0, c), :])
            dS_s[bi] = lax.dot_general(do_s[pl.ds(r0, c), :].astype(MXU_DTYPE), qd.astype(MXU_DTYPE), _TN,
                                       preferred_element_type=F32)
        dst = dst_ref[...]
        for bi in range(nb - 1, -1, -1):
            dstA_s[bi] = dst
            dst = dst * jnp.exp2(G_s[pl.ds(bi * c + c - 1, 1), :]) + dS_s[bi]
        dst_ref[...] = dst
        for bi in range(nb):
            r0 = bi * c
            glast = G_s[pl.ds(r0 + c - 1, 1), :]
            kd = k_s[pl.ds(r0, c), :] * jnp.exp2(glast - G_s[pl.ds(r0, c), :])
            st = sall_ref[bi]
            dstb = dstA_s[bi]
            dst_m = dstb.astype(MXU_DTYPE)
            dqd_s[pl.ds(r0, c), :] = lax.dot_general(do_s[pl.ds(r0, c), :].astype(MXU_DTYPE), st.astype(MXU_DTYPE),
                                                     _NN, preferred_element_type=F32)
            dkd_s[pl.ds(r0, c), :] = lax.dot_general(i_ref[pl.ds(r0, c), :].astype(MXU_DTYPE), dst_m, _NN,
                                                     preferred_element_type=F32)
            dvi_s[pl.ds(r0, c), :] = lax.dot_general(kd.astype(MXU_DTYPE), dst_m, _NT,
                                                     preferred_element_type=F32)
            da_s[pl.ds(bi * SUB, 1), :] = jnp.sum(dstb * st, axis=0, keepdims=True)
        for bi in range(nb):
            r0 = bi * c
            e0, w0 = bi * cc, bi * 2 * cc
            Gg = [G_s[pl.ds(r0 + gi * SUB, SUB), :] for gi in range(ng)]
            kg = [k_s[pl.ds(r0 + gi * SUB, SUB), :] for gi in range(ng)]
            vg = [i_ref[pl.ds(r0 + gi * SUB, SUB), :] for gi in range(ng)]
            for t in range(c):
                gt = G_s[pl.ds(r0 + t, 1), :]
                qt = q_s[pl.ds(r0 + t, 1), :]
                dot_ = do_s[pl.ds(r0 + t, 1), :]
                ep, wp, pp = [], [], []
                for gi in range(ng):
                    if gi > t // SUB:
                        ep.append(zero)
                        wp.append(zero)
                        pp.append(zero)
                        continue
                    e = jnp.exp2(jnp.minimum(gt - Gg[gi], 0.0))
                    if gi == t // SUB:
                        e = jnp.where(rowid <= t - gi * SUB, e, 0.0)
                    ep.append(e)
                    wp.append(e * kg[gi] * qt)
                    pp.append(vg[gi] * dot_)
                E_s[pl.ds(e0 + t * c, c), :] = jnp.concatenate(ep, axis=0)
                WP_s[pl.ds(w0 + t * c, c), :] = jnp.concatenate(wp, axis=0).astype(MXU_DTYPE)
                WP_s[pl.ds(w0 + cc + t * c, c), :] = jnp.concatenate(pp, axis=0).astype(MXU_DTYPE)
        R_s[...] = jnp.dot(WP_s[...], ones, preferred_element_type=F32)
        for bi in range(nb):
            r0 = bi * c
            e0, w0 = bi * cc, bi * 2 * cc
            kg = [k_s[pl.ds(r0 + gi * SUB, SUB), :] for gi in range(ng)]
            dk_g = [zero] * ng
            dv_g = [zero] * ng
            dq_g = [zero] * ng
            for t in range(c):
                qt = q_s[pl.ds(r0 + t, 1), :]
                dot_ = do_s[pl.ds(r0 + t, 1), :]
                tot = None
                for gi in range(t // SUB + 1):
                    lo = t * c + gi * SUB
                    dae = R_s[pl.ds(w0 + cc + lo, SUB), :] * E_s[pl.ds(e0 + lo, SUB), :]
                    z = dae * kg[gi]
                    tot = z if tot is None else tot + z
                    dk_g[gi] = dk_g[gi] + dae * qt
                    dv_g[gi] = dv_g[gi] + R_s[pl.ds(w0 + lo, SUB), :] * dot_
                gt_ = t // SUB
                dq_g[gt_] = jnp.where(rowid == t - gt_ * SUB, jnp.sum(tot, axis=0, keepdims=True), dq_g[gt_])
            dq_i = jnp.concatenate(dq_g, axis=0)
            dk_i = jnp.concatenate(dk_g, axis=0)
            dv_i = jnp.concatenate(dv_g, axis=0)
            Gb = G_s[pl.ds(r0, c), :]
            qb = q_s[pl.ds(r0, c), :]
            kb = k_s[pl.ds(r0, c), :]
            glast = G_s[pl.ds(r0 + c - 1, 1), :]
            eg = jnp.exp2(Gb)
            egl = jnp.exp2(glast - Gb)
            dqd = dqd_s[pl.ds(r0, c), :]
            dkd = dkd_s[pl.ds(r0, c), :]
            dq_s[pl.ds(r0, c), :] = dqd * eg + dq_i
            dk_s[pl.ds(r0, c), :] = dkd * egl + dk_i
            dv_s[pl.ds(r0, c), :] = dvi_s[pl.ds(r0, c), :] + dv_i
            dkdkd = dkd * kb * egl
            dG = dqd * qb * eg + qb * dq_i - kb * dk_i - dkdkd
            dglast = jnp.sum(dkdkd, axis=0, keepdims=True) + da_s[pl.ds(bi * SUB, 1), :] * jnp.exp2(glast)
            dG_s[pl.ds(r0, c), :] = dG + jnp.where(rowid_c == c - 1, dglast, 0.0)

        dlogf = _blk_cumsum(dG_s[...], c, reverse=True)
        df = jnp.where(f > F_MIN, dlogf / f, 0.0)
        dk = dk_s[...]
        dh_ref[0] = (dq_s[...] * _dsilu(q_ref[...])).astype(ACT_DTYPE)
        dh_ref[1] = ((df - dk) * one_m * sig * (1.0 - sig)).astype(ACT_DTYPE)
        dh_ref[2] = dv_s[...].astype(ACT_DTYPE)
        dlb_ref[...] += jnp.sum((df - dk) * (1.0 - sig), axis=0, keepdims=True)

    def sec(s):
        return pl.BlockSpec((None, Tt, LANES), lambda hd, i: (s, nT - 1 - i, hd))

    col = pl.BlockSpec((Tt, LANES), lambda hd, i: (nT - 1 - i, hd))
    tile = pltpu.VMEM((Tt, LANES), F32)
    return pl.pallas_call(
        body, name="hgrn_bwd", grid=(nh, nT),
        in_specs=[sec(0), sec(1), sec(2), sec(3), pl.BlockSpec((1, LANES), lambda hd, i: (0, hd)),
                  pl.BlockSpec((1, LANES), lambda hd, i: (0, 0)), col,
                  pl.BlockSpec((nb, None, LANES, LANES), lambda hd, i: (nT - 1 - i, hd, 0, 0)), col,
                  pl.BlockSpec(memory_space=pl.ANY)],
        out_specs=[pl.BlockSpec((4, Tt, LANES), lambda hd, i: (0, nT - 1 - i, hd)),
                   pl.BlockSpec((1, LANES), lambda hd, i: (0, hd)),
                   pl.BlockSpec((1, LANES), lambda hd, i: (0, 0))],
        out_shape=[jax.ShapeDtypeStruct(dh.shape, dh.dtype), jax.ShapeDtypeStruct((1, D), F32),
                   jax.ShapeDtypeStruct((1, LANES), F32)],
        scratch_shapes=[pltpu.VMEM((LANES, LANES), F32), tile, tile, tile, tile,
                        pltpu.VMEM((nb * c * c, LANES), F32), pltpu.VMEM((2 * nb * c * c, LANES), MXU_DTYPE),
                        tile, tile, tile, tile,
                        pltpu.VMEM((2 * nb * c * c, LANES), F32), pltpu.VMEM((nb, LANES, LANES), F32),
                        pltpu.VMEM((nb, LANES, LANES), F32), tile, tile, tile, pltpu.VMEM((nb * SUB, LANES), F32)],
        input_output_aliases={9: 0},
        compiler_params=_cp("arbitrary", "arbitrary"))(h, h, h, h, lbs_l, gw, o_pre, st_all, dy, dh)


def _shifted_copies(src, cs, dst, rows):
    for b in range(1, SUB):
        dst[b - 1] = src[pl.ds(b, rows + CONV_HALO - SUB), cs]


def _shifted(src, cs, copies, shift, rows):
    a8, b = divmod(shift, SUB)
    if b == 0:
        return src[pl.ds(shift, rows), cs]
    return copies[b - 1, pl.ds(a8 * SUB, rows), :]


def _conv_fwd(h, w_dw, b_dw, ln_g, ln_b):
    _, T, D = h.shape
    Tt = _pick(T, (256, 128, 64, 32))
    hb = Tt // CONV_HALO
    off = CONV_HALO - (CONV_WIDTH - 1)

    def body(a_ref, b_ref, ap_ref, bp_ref, w_ref, bd_ref, g_ref, be_ref, yc_ref, y_ref, U_s, Ub_s):
        first = pl.program_id(0) == 0
        up = ap_ref[...] * jax.nn.sigmoid(bp_ref[...])
        U_s[pl.ds(0, CONV_HALO), :] = jnp.where(first, 0.0, up)
        U_s[pl.ds(CONV_HALO, Tt), :] = a_ref[...] * jax.nn.sigmoid(b_ref[...])
        for cb in range(D // LANES):
            cs = pl.ds(cb * LANES, LANES)
            _shifted_copies(U_s, cs, Ub_s, Tt)
            acc = jnp.zeros((Tt, LANES), F32)
            for j in range(CONV_WIDTH):
                acc = acc + w_ref[pl.ds(j, 1), cs] * _shifted(U_s, cs, Ub_s, off + j, Tt)
            yc_ref[:, cs] = acc + bd_ref[:, cs]
        yc = yc_ref[...]
        mu = jnp.mean(yc, axis=-1, keepdims=True)
        zc = yc - mu
        var = jnp.mean(zc * zc, axis=-1, keepdims=True)
        ln = zc * lax.rsqrt(var + LN_EPS) * g_ref[...] + be_ref[...]
        y_ref[...] = _silu(ln).astype(ACT_DTYPE)

    def main(s):
        return pl.BlockSpec((None, Tt, D), lambda i: (s, i, 0))

    def prev(s):
        return pl.BlockSpec((None, CONV_HALO, D), lambda i: (s, jnp.maximum(i * hb - 1, 0), 0))

    row = pl.BlockSpec((Tt, D), lambda i: (i, 0))
    vec = pl.BlockSpec((1, D), lambda i: (0, 0))
    return pl.pallas_call(
        body, name="conv_fwd", grid=(T // Tt,),
        in_specs=[main(4), main(5), prev(4), prev(5), pl.BlockSpec((CONV_HALO, D), lambda i: (0, 0)),
                  vec, vec, vec],
        out_specs=[row, row],
        out_shape=[jax.ShapeDtypeStruct((T, D), F32), jax.ShapeDtypeStruct((T, D), ACT_DTYPE)],
        scratch_shapes=[pltpu.VMEM((CONV_HALO + Tt, D), F32),
                        pltpu.VMEM((SUB - 1, Tt + CONV_HALO - SUB, LANES), F32)],
        compiler_params=_cp("parallel"))(h, h, h, h, w_dw, b_dw, ln_g, ln_b)


def _conv_bwd(h, w_dw, ln_g, ln_b, yc, dy, dh):
    _, T, D = h.shape
    Tt = _pick(T, (256, 128, 64, 32))
    hb = Tt // CONV_HALO
    nT = T // Tt
    nhb = T // CONV_HALO
    off = CONV_HALO - (CONV_WIDTH - 1)

    def body(a_ref, b_ref, ap_ref, bp_ref, w_ref, g_ref, be_ref, yc_ref, ycn_ref, dy_ref, dyn_ref, dh_in_ref,
             dh_ref, dw_ref, dbd_ref, dg_ref, dbe_ref, U_s, DY_s, du_s, Ub_s, DYb_s):
        del dh_in_ref
        i = pl.program_id(0)

        @pl.when(i == 0)
        def _():
            dw_ref[...] = jnp.zeros_like(dw_ref)
            dbd_ref[...] = jnp.zeros_like(dbd_ref)
            dg_ref[...] = jnp.zeros_like(dg_ref)
            dbe_ref[...] = jnp.zeros_like(dbe_ref)

        gv = g_ref[...]
        bev = be_ref[...]

        def ln_silu_bwd(ycv, dyv):
            mu = jnp.mean(ycv, axis=-1, keepdims=True)
            zc = ycv - mu
            rstd = lax.rsqrt(jnp.mean(zc * zc, axis=-1, keepdims=True) + LN_EPS)
            xhat = zc * rstd
            dln = dyv * _dsilu(xhat * gv + bev)
            dxh = dln * gv
            dyc = rstd * (dxh - jnp.mean(dxh, axis=-1, keepdims=True)
                          - xhat * jnp.mean(dxh * xhat, axis=-1, keepdims=True))
            return dyc, dln, xhat

        dyc, dln, xhat = ln_silu_bwd(yc_ref[...], dy_ref[...])
        dg_ref[...] += jnp.sum(dln * xhat, axis=0, keepdims=True)
        dbe_ref[...] += jnp.sum(dln, axis=0, keepdims=True)
        dbd_ref[...] += jnp.sum(dyc, axis=0, keepdims=True)
        DY_s[pl.ds(0, Tt), :] = dyc
        dycn, _, _ = ln_silu_bwd(ycn_ref[...], dyn_ref[...])
        DY_s[pl.ds(Tt, CONV_HALO), :] = jnp.where(i == nT - 1, 0.0, dycn)

        sb = jax.nn.sigmoid(b_ref[...])
        av = a_ref[...]
        up = ap_ref[...] * jax.nn.sigmoid(bp_ref[...])
        U_s[pl.ds(0, CONV_HALO), :] = jnp.where(i == 0, 0.0, up)
        U_s[pl.ds(CONV_HALO, Tt), :] = av * sb

        for cb in range(D // LANES):
            cs = pl.ds(cb * LANES, LANES)
            _shifted_copies(U_s, cs, Ub_s, Tt)
            _shifted_copies(DY_s, cs, DYb_s, Tt)
            dyb = DY_s[pl.ds(0, Tt), cs]
            acc = jnp.zeros((Tt, LANES), F32)
            for j in range(CONV_WIDTH):
                acc = acc + w_ref[pl.ds(j, 1), cs] * _shifted(DY_s, cs, DYb_s, CONV_WIDTH - 1 - j, Tt)
                dw_ref[pl.ds(j, 1), cs] += jnp.sum(dyb * _shifted(U_s, cs, Ub_s, off + j, Tt), axis=0, keepdims=True)
            du_s[:, cs] = acc
        du = du_s[...]
        dh_ref[0] = (du * sb).astype(ACT_DTYPE)
        dh_ref[1] = (du * av * sb * (1.0 - sb)).astype(ACT_DTYPE)

    def main(s):
        return pl.BlockSpec((None, Tt, D), lambda i: (s, i, 0))

    def prev(s):
        return pl.BlockSpec((None, CONV_HALO, D), lambda i: (s, jnp.maximum(i * hb - 1, 0), 0))

    row = pl.BlockSpec((Tt, D), lambda i: (i, 0))
    nxt = pl.BlockSpec((CONV_HALO, D), lambda i: (jnp.minimum((i + 1) * hb, nhb - 1), 0))
    vec = pl.BlockSpec((1, D), lambda i: (0, 0))
    wspec = pl.BlockSpec((CONV_HALO, D), lambda i: (0, 0))
    return pl.pallas_call(
        body, name="conv_bwd", grid=(nT,),
        in_specs=[main(4), main(5), prev(4), prev(5), wspec, vec, vec, row, nxt, row, nxt,
                  pl.BlockSpec(memory_space=pl.ANY)],
        out_specs=[pl.BlockSpec((2, Tt, D), lambda i: (2, i, 0)), wspec, vec, vec, vec],
        out_shape=[jax.ShapeDtypeStruct(dh.shape, dh.dtype), jax.ShapeDtypeStruct((CONV_HALO, D), F32),
                   jax.ShapeDtypeStruct((1, D), F32), jax.ShapeDtypeStruct((1, D), F32),
                   jax.ShapeDtypeStruct((1, D), F32)],
        scratch_shapes=[pltpu.VMEM((CONV_HALO + Tt, D), F32), pltpu.VMEM((Tt + CONV_HALO, D), F32),
                        pltpu.VMEM((Tt, D), F32),
                        pltpu.VMEM((SUB - 1, Tt + CONV_HALO - SUB, LANES), F32),
                        pltpu.VMEM((SUB - 1, Tt + CONV_HALO - SUB, LANES), F32)],
        input_output_aliases={11: 0},
        compiler_params=_cp("arbitrary"))(h, h, h, h, w_dw, ln_g, ln_b, yc, yc, dy, dy, dh)


def _adamw(name, w, m, v, parts, part_specs, tr, prefetch=None, nsteps=None, row_map=None, prev=None):
    R, C = w.shape
    bc1 = 1.0 - ADAM_B1 ** ADAM_STEP
    bc2 = 1.0 - ADAM_B2 ** ADAM_STEP
    npart = len(parts)
    npre = 0 if prefetch is None else 1
    nprev = 0 if prev is None else 4

    def body(*refs):
        refs = refs[npre:]
        w_ref, m_ref, v_ref = refs[:3]
        p_refs = refs[3:3 + npart]
        g_ref, d_ref, mo_ref, vo_ref = refs[3 + npart + nprev:]
        g = p_refs[0][...].astype(F32)
        for p in p_refs[1:]:
            g = g + p[...].astype(F32)
        wv = w_ref[...]
        mn = ADAM_B1 * m_ref[...] + (1.0 - ADAM_B1) * g
        vn = ADAM_B2 * v_ref[...] + (1.0 - ADAM_B2) * (g * g)
        m_hat = mn / bc1
        v_hat = vn / bc2
        g_ref[...] = g
        d_ref[...] = -ADAM_LR * (m_hat / (jnp.sqrt(v_hat) + ADAM_EPS) + ADAM_WD * wv)
        mo_ref[...] = mn
        vo_ref[...] = vn

    if row_map is None:
        row_map = (lambda i: (i, 0)) if prefetch is None else (lambda i, s: (i, 0))
    row = pl.BlockSpec((tr, C), row_map)
    out = jax.ShapeDtypeStruct((R, C), F32)
    gs = pltpu.PrefetchScalarGridSpec(
        num_scalar_prefetch=npre, grid=(R // tr if nsteps is None else nsteps,),
        in_specs=[row, row, row] + list(part_specs) + [_ANY] * nprev, out_specs=[row] * 4)
    args = ([prefetch] if npre else []) + [w, m, v] + list(parts) + (list(prev) if nprev else [])
    first_prev = npre + 3 + npart
    return pl.pallas_call(body, name=name, grid_spec=gs, out_shape=[out] * 4,
                          input_output_aliases={first_prev + i: i for i in range(nprev)},
                          compiler_params=_cp("parallel"))(*args)


def _pair_add(p, r1, my_c):
    _, R, C = r1.shape
    tr = max(t for t in range(16, 1025, 16) if R % t == 0)

    def body(c_ref, p_ref, r_ref, q_ref):
        del c_ref
        q_ref[...] = (p_ref[...].astype(F32) + r_ref[...].astype(F32)).astype(q_ref.dtype)

    gs = pltpu.PrefetchScalarGridSpec(
        num_scalar_prefetch=1, grid=(4, R // tr),
        in_specs=[pl.BlockSpec((None, tr, C), lambda j, i, c: (2 * j + c[0], i, 0)),
                  pl.BlockSpec((None, tr, C), lambda j, i, c: (j, i, 0))],
        out_specs=pl.BlockSpec((None, tr, C), lambda j, i, c: (j, i, 0)))
    return pl.pallas_call(body, name="pair_add", grid_spec=gs, out_shape=jax.ShapeDtypeStruct(r1.shape, r1.dtype),
                          compiler_params=_cp("parallel", "parallel"))(my_c, p, r1)


def _place():
    x, y, c = lax.axis_index("x"), lax.axis_index("y"), lax.axis_index("c")
    chips = [(1 - x, y), (x, 1 - y), (1 - x, 1 - y)]
    return x, y, c, chips


def _hbm(a):
    return pltpu.with_memory_space_constraint(a, pltpu.HBM)


def _gather_targets():
    x, y, c, chips = _place()
    return 4 * x + 2 * y + c, [(x, y, 1 - c)] + [(*chip, c) for chip in chips]


def _gather_start(name, shards, zones, after=None):
    n = len(shards)
    lands = [_hbm(z) for z in zones]
    n_in = 2 * n + (0 if after is None else 1)

    def body(*refs):
        srcs, zones = refs[:n], refs[n:2 * n]
        send, recv, token = refs[n_in], refs[n_in + 1], refs[-1]
        mine, targets = _gather_targets()
        for a in range(n):
            for k, to in enumerate(targets):
                pltpu.make_async_remote_copy(
                    src_ref=srcs[a], dst_ref=zones[a].at[mine], send_sem=send.at[4 * a + k],
                    recv_sem=recv.at[4 * a + k], device_id=to, device_id_type=MESH).start()
        token[...] = jnp.zeros_like(token)

    sem = pltpu.SemaphoreType.DMA((4 * n,))
    out_shape = ([sem, sem] + [pltpu.HBM(s.shape, s.dtype) for s in shards]
                 + [pltpu.HBM(z.shape, z.dtype) for z in lands] + [jax.ShapeDtypeStruct((8, LANES), F32)])
    outs = pl.pallas_call(
        body, name=name, out_shape=out_shape, in_specs=[_HBM] * (2 * n) + ([] if after is None else [_ANY]),
        out_specs=[_SEM, _SEM] + [_HBM] * (2 * n) + [pl.BlockSpec(memory_space=pltpu.VMEM)],
        input_output_aliases={i: 2 + i for i in range(2 * n)},
        compiler_params=pltpu.CompilerParams(has_side_effects=_EFFECT))(
            *[_hbm(s) for s in shards], *lands, *([] if after is None else [after]))
    return outs[0], outs[1], list(outs[2:2 + n]), list(outs[2 + n:2 + 2 * n]), outs[-1]


def _gather_wait(name, shards, zones, send, recv, after):
    per = len(shards)

    def body(*refs):
        srcs, lz = refs[:per], refs[per:2 * per]
        send_s, recv_s = refs[2 * per], refs[2 * per + 1]
        mine, targets = _gather_targets()
        for a in range(per):
            for k, to in enumerate(targets):
                cp = pltpu.make_async_remote_copy(
                    src_ref=srcs[a], dst_ref=lz[a].at[mine], send_sem=send_s.at[4 * a + k],
                    recv_sem=recv_s.at[4 * a + k], device_id=to, device_id_type=MESH)
                cp.wait_send()
                cp.wait_recv()

    outs = pl.pallas_call(
        body, name=name, out_shape=[pltpu.HBM(s.shape, s.dtype) for s in shards + zones],
        in_specs=[_HBM] * (2 * per) + [_SEM, _SEM, _ANY], out_specs=[_HBM] * (2 * per),
        input_output_aliases={i: i for i in range(2 * per)},
        compiler_params=pltpu.CompilerParams(has_side_effects=_EFFECT))(*shards, *zones, send, recv, after)
    return outs[:per], outs[per:]


def _gather_finish(zones):
    n = len(zones)

    def body(*refs):
        lz = refs[n:2 * n]
        send_sems, recv_sems = refs[2 * n:]
        x, y, c, chips = _place()

        def fwd(a, j, pc):
            cx, cy = chips[j]
            blk = lz[a].at[4 * cx + 2 * cy + pc]
            return pltpu.make_async_remote_copy(
                src_ref=blk, dst_ref=blk, send_sem=send_sems.at[3 * a + j], recv_sem=recv_sems.at[3 * a + j],
                device_id=(x, y, 1 - c), device_id_type=MESH)

        sends = [fwd(a, j, c) for a in range(n) for j in range(3)]
        for cp in sends:
            cp.start()
        for a in range(n):
            for j in range(3):
                fwd(a, j, 1 - c).wait_recv()
        for cp in sends:
            cp.wait_send()

    return pl.pallas_call(
        body, name="gather_finish", out_shape=[jax.ShapeDtypeStruct(z.shape, z.dtype) for z in zones],
        in_specs=[_ANY] * n, out_specs=[_ANY] * n, input_output_aliases={a: a for a in range(n)},
        scratch_shapes=[pltpu.SemaphoreType.DMA((3 * n,)), pltpu.SemaphoreType.DMA((3 * n,))])(*zones)


def _place_own(shard, dev):
    R, C = shard.shape
    tr = max(t for t in range(16, 1025, 16) if R % t == 0)

    def body(d_ref, s_ref, z_ref):
        del d_ref
        z_ref[...] = s_ref[...]

    gs = pltpu.PrefetchScalarGridSpec(
        num_scalar_prefetch=1, grid=(R // tr,), in_specs=[pl.BlockSpec((tr, C), lambda i, d: (i, 0))],
        out_specs=pl.BlockSpec((None, tr, C), lambda i, d: (d[0], i, 0)))
    return pl.pallas_call(body, name="place_own", grid_spec=gs,
                          out_shape=jax.ShapeDtypeStruct((N_DEV, R, C), shard.dtype),
                          compiler_params=_cp("parallel"))(dev, shard)


def _exchange_sibling(bufs):
    n_arr = len(bufs)

    def body(*refs):
        srcs, outs = refs[:n_arr], refs[n_arr:2 * n_arr]
        send_sems, recv_sems = refs[2 * n_arr:]
        x, y, c, _ = _place()
        copies = []
        for n in range(n_arr):
            for j in range(4):
                copies.append(pltpu.make_async_remote_copy(
                    src_ref=srcs[n].at[2 * j + 1 - c], dst_ref=outs[n].at[j],
                    send_sem=send_sems.at[4 * n + j], recv_sem=recv_sems.at[4 * n + j],
                    device_id=(x, y, 1 - c), device_id_type=MESH))
        for cp in copies:
            cp.start()
        for cp in copies:
            cp.wait()

    return pl.pallas_call(
        body, name="exchange_sibling",
        out_shape=[jax.ShapeDtypeStruct((4,) + b.shape[1:], b.dtype) for b in bufs],
        in_specs=[_HBM] * n_arr, out_specs=[_HBM] * n_arr,
        scratch_shapes=[pltpu.SemaphoreType.DMA((4 * n_arr,)), pltpu.SemaphoreType.DMA((4 * n_arr,))])(
            *[_hbm(b) for b in bufs])


def _chip_copies(srcs, zones, send, recv):
    _, _, c, chips = _place()
    return [pltpu.make_async_remote_copy(
        src_ref=srcs[n].at[2 * cx + cy], dst_ref=zones[n].at[k], send_sem=send.at[3 * n + k],
        recv_sem=recv.at[3 * n + k], device_id=(cx, cy, c), device_id_type=MESH)
        for n in range(len(srcs)) for k, (cx, cy) in enumerate(chips)]


def _exchange_chips_start(name, bufs, after=None):
    n = len(bufs)
    n_in = 2 * n + (0 if after is None else 1)
    lands = [_hbm(lax.empty((3,) + b.shape[1:], b.dtype)) for b in bufs]

    def body(*refs):
        srcs, zones = refs[:n], refs[n:2 * n]
        send, recv, token = refs[n_in], refs[n_in + 1], refs[-1]
        for cp in _chip_copies(srcs, zones, send, recv):
            cp.start()
        token[...] = jnp.zeros_like(token)

    sem = pltpu.SemaphoreType.DMA((3 * n,))
    outs = pl.pallas_call(
        body, name=name,
        out_shape=[sem, sem] + [pltpu.HBM(b.shape, b.dtype) for b in bufs]
        + [pltpu.HBM(z.shape, z.dtype) for z in lands] + [jax.ShapeDtypeStruct((8, LANES), F32)],
        in_specs=[_HBM] * (2 * n) + ([] if after is None else [_ANY]),
        out_specs=[_SEM, _SEM] + [_HBM] * (2 * n) + [pl.BlockSpec(memory_space=pltpu.VMEM)],
        input_output_aliases={i: 2 + i for i in range(2 * n)},
        compiler_params=pltpu.CompilerParams(has_side_effects=_EFFECT))(
            *[_hbm(b) for b in bufs], *lands, *([] if after is None else [after]))
    return outs[0], outs[1], outs[2:2 + n], outs[2 + n:2 + 2 * n], outs[-1]


def _exchange_chips_wait(name, bufs, zones, send, recv, after):
    n = len(bufs)

    def body(*refs):
        for cp in _chip_copies(refs[:n], refs[n:2 * n], refs[2 * n], refs[2 * n + 1]):
            cp.wait_send()
            cp.wait_recv()

    outs = pl.pallas_call(
        body, name=name, out_shape=[pltpu.HBM(a.shape, a.dtype) for a in list(bufs) + list(zones)],
        in_specs=[_HBM] * (2 * n) + [_SEM, _SEM, _ANY], out_specs=[_HBM] * (2 * n),
        input_output_aliases={i: i for i in range(2 * n)},
        compiler_params=pltpu.CompilerParams(has_side_effects=_EFFECT))(*bufs, *zones, send, recv, after)
    return outs[n:]


def _all_gather_small(part):
    def body(src, out, send_sems, recv_sems, local_sem):
        x, y, c, _ = _place()
        mine = pltpu.make_async_copy(src, out.at[4 * x + 2 * y + c], local_sem)
        mine.start()
        copies = []
        for r in range(1, N_DEV):
            dx, dy, dc = (r >> 2) & 1, (r >> 1) & 1, r & 1
            peer = (1 - x if dx else x, 1 - y if dy else y, 1 - c if dc else c)
            copies.append(pltpu.make_async_remote_copy(
                src_ref=src, dst_ref=out.at[4 * x + 2 * y + c],
                send_sem=send_sems.at[r - 1], recv_sem=recv_sems.at[r - 1],
                device_id=peer, device_id_type=MESH))
        for cp in copies:
            cp.start()
        for cp in copies:
            cp.wait()
        mine.wait()

    return pl.pallas_call(
        body, name="all_gather_small",
        out_shape=jax.ShapeDtypeStruct((N_DEV,) + part.shape, part.dtype),
        in_specs=[_ANY], out_specs=_ANY,
        scratch_shapes=[pltpu.SemaphoreType.DMA((N_DEV - 1,)), pltpu.SemaphoreType.DMA((N_DEV - 1,)),
                        pltpu.SemaphoreType.DMA])(part)


def _layer_fwd(xin, xin_bf, w_in, rest, P, alpha, dep=None):
    h = _proj_in(xin_bf, w_in, P["b_in"], dep=dep)
    o_pre, y_hg, st_all = _hgrn_fwd(h, P["lbs"], P["g_norm_w"])
    yc_pre, y_cv = _conv_fwd(h, P["w_dw"], P["b_dw"], P["conv_ln_g"], P["conv_ln_b"])
    W = rest(y_cv)
    y_h, y_c, merged, x1, x1_bf, z1 = _mixer_out(y_hg, y_cv, h, xin, W["w_a"], W["w_b"], P["b_b"], W["w_o"], alpha,
                                                 P["ln1_g"], P["ln1_b"])
    up, act = _ffn_up_swiglu(x1_bf, W["w_up"])
    x2, x2_bf, z2 = _ffn_down_ln2(act, W["w_down"], x1, alpha, P["ln2_g"], P["ln2_b"])
    saved = dict(xin_bf=xin_bf, h=h, o_pre=o_pre, y_hg=y_hg, st_all=st_all, yc_pre=yc_pre, y_cv=y_cv,
                 y_h=y_h, y_c=y_c, merged=merged, z1=z1, x1_bf=x1_bf, up=up, act=act, z2=z2)
    return x2, x2_bf, saved


def _layer_bwd(dx2, S, W, P, alpha, dep=None, early=None):
    dz2, dz2_bf, dup, dln2_g, dln2_b = _ln2_ffn_down_bwd(S["z2"], dx2, P["ln2_g"], W["w_down"], S["up"], dep=dep)
    dw_down = _mm_tn("ffn_down_dw", S["act"], dz2_bf, ACT_DTYPE)
    dx1 = _ffn_up_dx(dup, W["w_up"], dz2, alpha)
    dw_up = _ffn_up_dw(S["x1_bf"], dup)
    dz1, dz1_bf, dy_h, dy_c, dy_hg, dy_cv, dh, dln1_g, dln1_b, db_b = _mixer_out_bwd(
        S["z1"], dx1, S["y_h"], S["y_c"], S["h"], W["w_a"], W["w_b"], W["w_o"], P["ln1_g"])
    dw_abo = _branch_dw([(S["y_hg"], dy_h), (S["y_cv"], dy_c), (S["merged"], dz1_bf)], dz1.shape[1] // N_DEV)
    if early is not None:
        token = early(dict(w_abo=dw_abo, w_down=dw_down, w_up=dw_up))
        dy_cv = dy_cv + token[0, 0]
    dh, dw_dw, db_dw, dcln_g, dcln_b = _conv_bwd(S["h"], P["w_dw"], P["conv_ln_g"], P["conv_ln_b"],
                                                 S["yc_pre"], dy_cv, dh)
    dh, dlbs, dgw = _hgrn_bwd(S["h"], P["lbs"], P["g_norm_w"], S["o_pre"], S["st_all"], dy_hg, dh)
    dxin = _proj_in_dx(dh, W["w_in"], dz1, alpha)
    dw_in, db_in = _proj_in_dw(S["xin_bf"], dh)
    big = dict(w_in=dw_in, w_abo=dw_abo, w_down=dw_down, w_up=dw_up)
    small = dict(b_in=db_in, lbs=dlbs, g_norm_w=dgw, w_dw=dw_dw, b_dw=db_dw, conv_ln_g=dcln_g,
                 conv_ln_b=dcln_b, b_b=db_b, ln1_g=dln1_g, ln1_b=dln1_b, ln2_g=dln2_g, ln2_b=dln2_b)
    return dxin, big, small


_SMALL = ("b_in", "lb_logits", "g_norm_w", "b_dw", "conv_ln_g", "conv_ln_b", "b_b", "ln1_g", "ln1_b", "ln2_g",
          "ln2_b")


def _pack_small(per_layer, ln0_g, ln0_b, extra_row, D, L):
    rows = []
    for l in range(L):
        for n in _SMALL:
            a = per_layer[n][l]
            if n == "b_in":
                rows.append(a.reshape(N_SEC, D))
            elif n == "g_norm_w":
                rows.append(jnp.pad(a.reshape(1, -1), ((0, 0), (0, D - a.size))))
            else:
                rows.append(a.reshape(1, D))
    rows += [ln0_g.reshape(1, D), ln0_b.reshape(1, D), extra_row]
    buf = jnp.concatenate(rows, axis=0)
    pad = (-buf.shape[0]) % 8
    return jnp.pad(buf, ((0, pad), (0, 0)))


def _unpack_small(buf, D, L, hv):
    out = {n: [] for n in _SMALL}
    r = 0
    for l in range(L):
        for n in _SMALL:
            if n == "b_in":
                out[n].append(buf[r:r + N_SEC].reshape(N_SEC * D))
                r += N_SEC
            elif n == "g_norm_w":
                out[n].append(buf[r, :hv])
                r += 1
            else:
                out[n].append(buf[r])
                r += 1
    res = {n: jnp.stack(v) for n, v in out.items()}
    res["ln0_g"] = buf[r]
    res["ln0_b"] = buf[r + 1]
    return res, r + 2


def kernel(x, ln0_g, ln0_b, w_in, b_in, lb_logits, g_norm_w, w_a, w_dw, b_dw, conv_ln_g, conv_ln_b, w_b, b_b, w_o, ln1_g, ln1_b, w_up, w_down, ln2_g, ln2_b, loss_target, m_ln0_g, m_ln0_b, m_w_in, m_b_in, m_lb_logits, m_g_norm_w, m_w_a, m_w_dw, m_b_dw, m_conv_ln_g, m_conv_ln_b, m_w_b, m_b_b, m_w_o, m_ln1_g, m_ln1_b, m_w_up, m_w_down, m_ln2_g, m_ln2_b, v_ln0_g, v_ln0_b, v_w_in, v_b_in, v_lb_logits, v_g_norm_w, v_w_a, v_w_dw, v_b_dw, v_conv_ln_g, v_conv_ln_b, v_w_b, v_b_b, v_w_o, v_ln1_g, v_ln1_b, v_w_up, v_w_down, v_ln2_g, v_ln2_b):
    L, D = w_in.shape[0], w_in.shape[1]
    T = x.shape[0] * x.shape[1]
    Dn = w_in.shape[2]
    rs = w_a.shape[1]
    rd = w_down.shape[1]
    cu = w_up.shape[2]
    F = rd * N_DEV
    hv = g_norm_w.shape[1]
    alpha = (2 * L) ** 0.25
    my_x, my_y, my_c = lax.axis_index("x"), lax.axis_index("y"), lax.axis_index("c")
    dev_arr = jnp.reshape(4 * my_x + 2 * my_y + my_c, (1,)).astype(jnp.int32)

    o_a, o_b, o_o, o_d = D, D + rs, D + 2 * rs, D + 3 * rs
    taps = jnp.pad(w_dw, ((0, 0), (0, CONV_HALO - CONV_WIDTH), (0, 0))).reshape(L * CONV_HALO, w_dw.shape[2])
    taps_all = _all_gather_small(taps)
    w_dw_full = taps_all.transpose(1, 0, 2).reshape(L, CONV_HALO, D)

    started, gathered = {}, {}

    def start_gather(key, after):
        l, part = key
        rest = [w_a[l], w_b[l], w_o[l], w_down[l]]
        rows = dict(all=[w_in[l]] + rest, rest=rest)
        if part == "in":
            shards = [w_in[l].astype(ACT_DTYPE)]
        else:
            shards = [jnp.concatenate(rows[part], axis=0).astype(ACT_DTYPE),
                      jnp.swapaxes(w_up[l], 0, 1).astype(ACT_DTYPE)]
        started[key] = _gather_start("gather_start_%d_%s" % key, shards, [_place_own(s, dev_arr) for s in shards],
                                     after)
        return started[key][4]

    def finish_gather(key, after):
        send, recv, thru, zone, _ = started[key]
        _, zn = _gather_wait("gather_wait_%d_%s" % key, thru, zone, send, recv, after)
        gathered[key] = _gather_finish(zn)

    def w_in_of(l):
        return gathered[(l, "in") if l == 0 else (l, "all")][0]

    def rest_of(l):
        ga, gb = gathered[(l, "rest") if l == 0 else (l, "all")]
        base = 0 if l == 0 else D
        return dict(
            w_a=ga[:, base:base + rs, :].reshape(D, D),
            w_b=ga[:, base + rs:base + 2 * rs, :].reshape(D, D),
            w_o=ga[:, base + 2 * rs:base + 3 * rs, :].reshape(D, D),
            w_down=ga[:, base + 3 * rs:base + 3 * rs + rd, :].reshape(F, D),
            w_up=gb.reshape(2 * F, D))

    def weights(l):
        return dict(w_in=w_in_of(l), **rest_of(l))

    lbs = _lb_fwd(lb_logits)

    def params(l):
        return dict(b_in=b_in[l].reshape(N_SEC, 1, D), lbs=lbs[l].reshape(1, D), g_norm_w=g_norm_w[l].reshape(1, hv),
                    w_dw=w_dw_full[l], b_dw=b_dw[l].reshape(1, D), conv_ln_g=conv_ln_g[l].reshape(1, D),
                    conv_ln_b=conv_ln_b[l].reshape(1, D), b_b=b_b[l].reshape(1, D), ln1_g=ln1_g[l], ln1_b=ln1_b[l],
                    ln2_g=ln2_g[l], ln2_b=ln2_b[l])

    x2d = x.reshape(T, D)
    token = start_gather((0, "in"), taps_all)
    token = start_gather((0, "rest"), token)
    if L > 1:
        token = start_gather((1, "all"), token)
    xc, xc_bf = _ln_fwd("ln0", x2d, None, 1.0, ln0_g, ln0_b, dep=token)
    finish_gather((0, "in"), xc_bf)
    saved = []
    for l in range(L):
        if l == 0:
            def rest(after):
                finish_gather((0, "rest"), after)
                return rest_of(0)
            token = None
        else:
            rest = functools.partial(lambda after, l: rest_of(l), l=l)
            token = start_gather((l + 1, "all"), gathered[(l, "all")][0]) if l + 1 < L else None
        xc, xc_bf, s = _layer_fwd(xc, xc_bf, w_in_of(l), rest, params(l), alpha, dep=token)
        saved.append(s)
        if l + 1 < L:
            finish_gather((l + 1, "all"), xc_bf)

    c_arr = jnp.reshape(my_c, (1,)).astype(jnp.int32)
    chip = 2 * my_x + my_y
    dx, loss_row = _loss_fwd_bwd(xc, loss_target.reshape(T, D))
    small = [None] * L
    pending = None
    upd_big = {n: None for n in ("w_in", "w_a", "w_b", "w_o", "w_down", "w_up")}
    wmv = dict(w_in=(w_in, m_w_in, v_w_in), w_a=(w_a, m_w_a, v_w_a), w_b=(w_b, m_w_b, v_w_b),
               w_o=(w_o, m_w_o, v_w_o), w_down=(w_down, m_w_down, v_w_down),
               w_up=tuple(jnp.swapaxes(a, 1, 2) for a in (w_up, m_w_up, v_w_up)))

    held = dict(w_in=("w_in",), w_abo=("w_a", "w_b", "w_o"), w_down=("w_down",), w_up=("w_up",))

    def send_buffer(buf, g):
        return g if g.ndim == 3 else g.reshape(N_DEV, wmv[held[buf][0]][0].shape[1], D)

    def update_layer(l, bufs, q, r2):
        pre = jnp.stack([chip, jnp.int32(l)]).astype(jnp.int32)
        for k, buf in enumerate(bufs):
            for pos, name in enumerate(held[buf]):
                w, m, v = wmv[name]
                r, C = w.shape[1], w.shape[2]
                tr = max(t for t in range(16, 513, 16) if r % t == 0)
                nb = r // tr
                b0 = pos * nb
                specs = [pl.BlockSpec((None, tr, C), functools.partial(lambda i, s, b0: (s[0], b0 + i, 0), b0=b0))]
                specs += [pl.BlockSpec((None, tr, C), functools.partial(lambda i, s, j, b0: (j, b0 + i, 0), j=j, b0=b0))
                          for j in range(3)]
                upd_big[name] = _adamw(
                    "adamw_" + name, w.reshape(L * r, C), m.reshape(L * r, C), v.reshape(L * r, C),
                    [q[k], r2[k], r2[k], r2[k]], specs, tr, prefetch=pre, nsteps=nb,
                    row_map=functools.partial(lambda i, s, nb: (s[1] * nb + i, 0), nb=nb), prev=upd_big[name])

    def pair_sums(bufs, grads):
        sends = [send_buffer(n, grads[n]) for n in bufs]
        return [_pair_add(p, r, c_arr) for p, r in zip(sends, _exchange_sibling(sends))]

    def start_reduce(tag, names, qs, after=None):
        s_send, s_recv, q_thru, zones, token = _exchange_chips_start("reduce_start_" + tag, qs, after)
        return (tag, names, (s_send, s_recv), list(q_thru), list(zones)), token

    def finish_reduce(l, handle, after):
        tag, names, sems, q, zones = handle
        r2 = _exchange_chips_wait("reduce_wait_" + tag, q, zones, sems[0], sems[1], after)
        update_layer(l, names, q, r2)

    names_all = tuple(held)
    names_early = names_all[1:]
    token = None
    first_early = []
    for l in range(L - 1, -1, -1):
        if l > 0:
            dx, big, small[l] = _layer_bwd(dx, saved[l], weights(l), params(l), alpha, dep=token)
            if pending is not None:
                finish_reduce(l + 1, pending, dx)
            pending, token = start_reduce(str(l), names_all, pair_sums(names_all, big))
        else:
            def early(grads):
                if pending is not None:
                    finish_reduce(1, pending, grads["w_abo"])
                handle, tok = start_reduce("0_rest", names_early, pair_sums(names_early, grads))
                first_early.append(handle)
                return tok
            dx, big, small[l] = _layer_bwd(dx, saved[l], weights(l), params(l), alpha, dep=token, early=early)
            q_in = pair_sums(("w_in",), big)
    dx0, _, dln0_g, dln0_b = _ln_bwd("ln0_bwd", x2d, dx, ln0_g)
    dlb_logits = _lb_bwd(lb_logits, jnp.concatenate([small[l]["lbs"] for l in range(L)], axis=0))

    small_l = {n: [small[l][n] for l in range(L)] for n in _SMALL if n != "lb_logits"}
    small_l["lb_logits"] = [dlb_logits[l] for l in range(L)]
    loss_pad = jnp.pad(loss_row, ((0, 0), (0, D - LANES)))
    part = jnp.concatenate([_pack_small(small_l, dln0_g, dln0_b, loss_pad, D, L)]
                           + [small[l]["w_dw"] for l in range(L)], axis=0)
    parts_all = _all_gather_small(part)
    n_small = part.shape[0] - L * CONV_HALO

    last, _ = start_reduce("0_in", ("w_in",), q_in, after=parts_all)


    inputs = dict(b_in=(b_in, m_b_in, v_b_in), lb_logits=(lb_logits, m_lb_logits, v_lb_logits),
                  g_norm_w=(g_norm_w, m_g_norm_w, v_g_norm_w), b_dw=(b_dw, m_b_dw, v_b_dw),
                  conv_ln_g=(conv_ln_g, m_conv_ln_g, v_conv_ln_g), conv_ln_b=(conv_ln_b, m_conv_ln_b, v_conv_ln_b),
                  b_b=(b_b, m_b_b, v_b_b), ln1_g=(ln1_g, m_ln1_g, v_ln1_g), ln1_b=(ln1_b, m_ln1_b, v_ln1_b),
                  ln2_g=(ln2_g, m_ln2_g, v_ln2_g), ln2_b=(ln2_b, m_ln2_b, v_ln2_b))
    zero_row = jnp.zeros((1, D), F32)
    packed = [_pack_small({n: [inputs[n][i][l] for l in range(L)] for n in _SMALL},
                          (ln0_g, m_ln0_g, v_ln0_g)[i], (ln0_b, m_ln0_b, v_ln0_b)[i], zero_row, D, L)
              for i in range(3)]
    small_specs = [pl.BlockSpec((None, n_small, D), functools.partial(lambda i, d: (d, 0, 0), d=d))
                   for d in range(N_DEV)]
    s_out = _adamw("adamw_small", packed[0], packed[1], packed[2], [parts_all] * N_DEV, small_specs, n_small)
    s_g, n_rows = _unpack_small(s_out[0], D, L, hv)
    s_d, _ = _unpack_small(s_out[1], D, L, hv)
    s_m, _ = _unpack_small(s_out[2], D, L, hv)
    s_v, _ = _unpack_small(s_out[3], D, L, hv)
    loss = s_out[0][n_rows, 0]

    cw = w_dw.shape[2]
    dev = 4 * my_x + 2 * my_y + my_c
    tap_parts = lax.dynamic_slice_in_dim(parts_all[:, n_small:, :], dev * cw, cw, axis=2)
    tap_specs = [pl.BlockSpec((None, L * CONV_HALO, cw), functools.partial(lambda i, d: (d, 0, 0), d=d))
                 for d in range(N_DEV)]
    pad_t = lambda a: jnp.pad(a, ((0, 0), (0, CONV_HALO - CONV_WIDTH), (0, 0))).reshape(L * CONV_HALO, cw)
    t_out = _adamw("adamw_taps", pad_t(w_dw), pad_t(m_w_dw), pad_t(v_w_dw), [tap_parts] * N_DEV, tap_specs,
                   L * CONV_HALO)
    finish_reduce(0, first_early[0], t_out[0])
    finish_reduce(0, last, upd_big["w_up"][0])
    upd ={n: [o.reshape(wmv[n][0].shape) for o in outs] for n, outs in upd_big.items()}
    upd["w_up"] = [jnp.swapaxes(o, 1, 2) for o in upd["w_up"]]
    upd["w_dw"] = [o.reshape(L, CONV_HALO, cw)[:, :CONV_WIDTH, :] for o in t_out]

    order = ["ln0_g", "ln0_b", "w_in", "b_in", "lb_logits", "g_norm_w", "w_a", "w_dw", "b_dw", "conv_ln_g",
             "conv_ln_b", "w_b", "b_b", "w_o", "ln1_g", "ln1_b", "w_up", "w_down", "ln2_g", "ln2_b"]
    small_sets = (s_g, s_d, s_m, s_v)
    outs = [loss, dx0.reshape(x.shape)]
    for i in range(4):
        for n in order:
            outs.append(upd[n][i] if n in upd else small_sets[i][n])
    return tuple(outs)
```

```python
import functools

import jax
import jax.numpy as jnp
from jax import lax
from jax.experimental import pallas as pl
from jax.experimental.pallas import tpu as pltpu

F32 = jnp.float32
MXU_DTYPE = jnp.bfloat16
ACT_DTYPE = jnp.bfloat16

LANES = 128
SUB = 8
N_DEV = 8
N_SEC = 8
CONV_WIDTH = 31
CONV_HALO = 32
HG_C = 16
LN_EPS = 1e-5
RMS_EPS = 1e-6
F_MIN = 1e-30
LOG2E = 1.4426950408889634
ADAM_LR = 0.001
ADAM_B1 = 0.9
ADAM_B2 = 0.999
ADAM_EPS = 1e-08
ADAM_WD = 0.01
ADAM_STEP = 10
VMEM_LIMIT = 56 * 1024 * 1024
MESH = pl.DeviceIdType.MESH

_NN = (((1,), (0,)), ((), ()))
_NT = (((1,), (1,)), ((), ()))
_TN = (((0,), (0,)), ((), ()))


_ANY = pl.BlockSpec(memory_space=pl.ANY)
_HBM = pl.BlockSpec(memory_space=pltpu.HBM)
_SEM = pl.BlockSpec(memory_space=pltpu.SEMAPHORE)
_EFFECT = pltpu.SideEffectType.DATAFLOW_SIDE_EFFECTING


def _cp(*sem):
    return pltpu.CompilerParams(dimension_semantics=tuple(sem), vmem_limit_bytes=VMEM_LIMIT)


def _pick(n, cands):
    for c in cands:
        if c <= n and n % c == 0:
            return c
    return n


def _silu(x):
    return x * jax.nn.sigmoid(x)


def _dsilu(x):
    s = jax.nn.sigmoid(x)
    return s * (1.0 + x * (1.0 - s))


def _matmul(name, a, b, *, dims, grid, a_spec, b_spec, out_shape, out_spec, acc_shape, nk,
            bias=None, bias_spec=None, add=None, add_spec=None, add_scale=1.0, dep=None):
    has_bias, has_add = bias is not None, add is not None
    kaxis = len(grid) - 1

    def body(*refs):
        a_ref, b_ref = refs[0], refs[1]
        pos = 2
        bias_ref = add_ref = None
        if has_bias:
            bias_ref = refs[pos]
            pos += 1
        if has_add:
            add_ref = refs[pos]
            pos += 1
        if dep is not None:
            pos += 1
        o_ref = refs[pos]
        acc_ref = refs[pos + 1] if nk > 1 else None

        part = lax.dot_general(a_ref[...].astype(MXU_DTYPE), b_ref[...].astype(MXU_DTYPE), dims,
                               preferred_element_type=F32)

        def finish(r):
            if has_bias:
                r = r + bias_ref[...]
            if has_add:
                r = r + add_scale * add_ref[...]
            o_ref[...] = r.astype(o_ref.dtype)

        if nk == 1:
            finish(part)
        else:
            k = pl.program_id(kaxis)

            @pl.when(k == 0)
            def _():
                acc_ref[...] = part

            @pl.when(k > 0)
            def _():
                acc_ref[...] += part

            @pl.when(k == nk - 1)
            def _():
                finish(acc_ref[...])

    ins, specs = [a, b], [a_spec, b_spec]
    if has_bias:
        ins.append(bias)
        specs.append(bias_spec)
    if has_add:
        ins.append(add)
        specs.append(add_spec)
    if dep is not None:
        ins.append(dep)
        specs.append(_ANY)
    sem =("parallel",) * (len(grid) - 1) + ("arbitrary",) if nk > 1 else ("parallel",) * len(grid)
    return pl.pallas_call(
        body, name=name, grid=grid, in_specs=specs, out_specs=out_spec, out_shape=out_shape,
        scratch_shapes=[pltpu.VMEM(acc_shape, F32)] if nk > 1 else [],
        compiler_params=_cp(*sem))(*ins)


def _mm_tn(name, a, b, out_dtype):
    K, M = a.shape
    N = b.shape[1]
    tm = _pick(M, (256, 128))
    return _matmul(
        name, a, b, dims=_TN, grid=(M // tm,),
        a_spec=pl.BlockSpec((K, tm), lambda i: (0, i)),
        b_spec=pl.BlockSpec((K, N), lambda i: (0, 0)),
        out_shape=jax.ShapeDtypeStruct((M, N), out_dtype),
        out_spec=pl.BlockSpec((tm, N), lambda i: (i, 0)),
        acc_shape=(tm, N), nk=1)


def _branch_dw(pairs, rs):
    T, D = pairs[0][0].shape
    nslot = max(1, LANES // rs)
    tm = nslot * rs
    nk = len(pairs)

    def body(*refs):
        o_ref = refs[-1]
        k = pl.program_id(0)
        for kk in range(nk):
            @pl.when(k == kk)
            def _():
                r = lax.dot_general(refs[2 * kk][...].astype(MXU_DTYPE), refs[2 * kk + 1][...].astype(MXU_DTYPE),
                                    _TN, preferred_element_type=F32)
                o_ref[...] = r.astype(o_ref.dtype).reshape(nslot, rs, D)

    in_specs, ins = [], []
    for kk, (a, b) in enumerate(pairs):
        in_specs.append(pl.BlockSpec((T, tm), functools.partial(lambda k, i, kk: (0, jnp.where(k == kk, i, 0)), kk=kk)))
        in_specs.append(pl.BlockSpec((T, D), lambda k, i: (0, 0)))
        ins += [a, b]
    return pl.pallas_call(
        body, name="branch_dw", grid=(nk, N_DEV // nslot), in_specs=in_specs,
        out_specs=pl.BlockSpec((nslot, rs, D), lambda k, i: (i, k, 0)),
        out_shape=jax.ShapeDtypeStruct((N_DEV, nk * rs, D), ACT_DTYPE),
        compiler_params=_cp("arbitrary", "arbitrary"))(*ins)


def _proj_in(x_bf, w_in, b_in, dep=None):
    T, D = x_bf.shape
    tn = _pick(D, (512, 256, 128))
    return _matmul(
        "proj_in", x_bf, w_in, dims=_NN, grid=(N_SEC, D // tn),
        a_spec=pl.BlockSpec((T, D), lambda s, j: (0, 0)),
        b_spec=pl.BlockSpec((None, D, tn), lambda s, j: (s, 0, j)),
        out_shape=jax.ShapeDtypeStruct((N_SEC, T, D), F32),
        out_spec=pl.BlockSpec((None, T, tn), lambda s, j: (s, 0, j)),
        acc_shape=(T, tn), nk=1,
        bias=b_in, bias_spec=pl.BlockSpec((None, 1, tn), lambda s, j: (s, 0, j)), dep=dep)


def _proj_in_dx(dh, w_in, add, add_scale):
    _, T, D = dh.shape
    tm = _pick(T, (256, 128, 64, 32, 16))

    def body(dh_ref, w_ref, add_ref, o_ref):
        acc = add_scale * add_ref[...]
        for s in range(N_SEC):
            acc = acc + lax.dot_general(dh_ref[s].astype(MXU_DTYPE), w_ref[s].astype(MXU_DTYPE), _NT,
                                        preferred_element_type=F32)
        o_ref[...] = acc

    row = pl.BlockSpec((tm, D), lambda i: (i, 0))
    return pl.pallas_call(
        body, name="proj_in_dx", grid=(T // tm,),
        in_specs=[pl.BlockSpec((N_SEC, tm, D), lambda i: (0, i, 0)),
                  pl.BlockSpec((N_SEC, D, D), lambda i: (0, 0, 0), pipeline_mode=pl.Buffered(1)), row],
        out_specs=row, out_shape=jax.ShapeDtypeStruct((T, D), F32),
        compiler_params=_cp("parallel"))(dh, w_in, add)


def _proj_in_dw(x_bf, dh):
    _, T, D = dh.shape
    tn = _pick(D, (512, 256, 128))

    def body(x_ref, dh_ref, dw_ref, db_ref):
        dhv = dh_ref[...]
        dw_ref[...] = lax.dot_general(x_ref[...].astype(MXU_DTYPE), dhv.astype(MXU_DTYPE), _TN,
                                      preferred_element_type=F32).astype(dw_ref.dtype)
        db_ref[...] = jnp.sum(dhv.astype(F32), axis=0, keepdims=True)

    return pl.pallas_call(
        body, name="proj_in_dw", grid=(N_SEC, D // tn),
        in_specs=[pl.BlockSpec((T, D), lambda s, j: (0, 0)), pl.BlockSpec((None, T, tn), lambda s, j: (s, 0, j))],
        out_specs=[pl.BlockSpec((None, D, tn), lambda s, j: (s, 0, j)),
                   pl.BlockSpec((None, 1, tn), lambda s, j: (s, 0, j))],
        out_shape=[jax.ShapeDtypeStruct((N_SEC, D, D), ACT_DTYPE), jax.ShapeDtypeStruct((N_SEC, 1, D), F32)],
        compiler_params=_cp("parallel", "parallel"))(x_bf, dh)


def _ffn_up_dx(dup, w_up_t, add, add_scale):
    _, T, F = dup.shape
    D = w_up_t.shape[1]
    tm = _pick(T, (256, 128, 64, 32, 16))

    def body(dup_ref, w_ref, add_ref, o_ref):
        acc = add_scale * add_ref[...]
        for p in range(2):
            acc = acc + jnp.dot(dup_ref[p].astype(MXU_DTYPE), w_ref[pl.ds(p * F, F), :].astype(MXU_DTYPE),
                                preferred_element_type=F32)
        o_ref[...] = acc

    row = pl.BlockSpec((tm, D), lambda i: (i, 0))
    return pl.pallas_call(
        body, name="ffn_up_dx", grid=(T // tm,),
        in_specs=[pl.BlockSpec((2, tm, F), lambda i: (0, i, 0)),
                  pl.BlockSpec((2 * F, D), lambda i: (0, 0), pipeline_mode=pl.Buffered(1)), row],
        out_specs=row, out_shape=jax.ShapeDtypeStruct((T, D), F32),
        compiler_params=_cp("parallel"))(dup, w_up_t, add)


def _ffn_up_dw(x_bf, dup):
    _, T, F = dup.shape
    D = x_bf.shape[1]
    tm = _pick(F, (1408, 256, 128))
    nb = F // tm
    return _matmul(
        "ffn_up_dw", dup, x_bf, dims=_TN, grid=(2, nb),
        a_spec=pl.BlockSpec((None, T, tm), lambda p, j: (p, 0, j)),
        b_spec=pl.BlockSpec((T, D), lambda p, j: (0, 0)),
        out_shape=jax.ShapeDtypeStruct((2 * F, D), ACT_DTYPE),
        out_spec=pl.BlockSpec((tm, D), lambda p, j: (p * nb + j, 0)),
        acc_shape=(tm, D), nk=1)


def _ln_fwd(name, a, res, alpha, g, b, dep=None):
    T, D = a.shape
    tr = _pick(T, (256, 128, 64, 32, 16))
    has_res = res is not None

    def body(*refs):
        if has_res:
            a_ref, r_ref, g_ref, b_ref = refs[:4]
            y_ref, yb_ref, z_ref = refs[-3:]
            z = alpha * a_ref[...] + r_ref[...]
            z_ref[...] = z
        else:
            a_ref, g_ref, b_ref = refs[:3]
            y_ref, yb_ref = refs[-2:]
            z = a_ref[...]
        mu = jnp.mean(z, axis=-1, keepdims=True)
        zc = z - mu
        var = jnp.mean(zc * zc, axis=-1, keepdims=True)
        y = zc * lax.rsqrt(var + LN_EPS) * g_ref[...] + b_ref[...]
        y_ref[...] = y
        yb_ref[...] = y.astype(ACT_DTYPE)

    row = pl.BlockSpec((tr, D), lambda i: (i, 0))
    vec = pl.BlockSpec((1, D), lambda i: (0, 0))
    ins = [a] + ([res] if has_res else []) + [g.reshape(1, D), b.reshape(1, D)]
    in_specs = [row] + ([row] if has_res else []) + [vec, vec]
    if dep is not None:
        ins.append(dep)
        in_specs.append(_ANY)
    out_shape = [jax.ShapeDtypeStruct((T, D), F32), jax.ShapeDtypeStruct((T, D), ACT_DTYPE)]
    if has_res:
        out_shape.append(jax.ShapeDtypeStruct((T, D), F32))
    return pl.pallas_call(
        body, name=name, grid=(T // tr,), in_specs=in_specs,
        out_specs=[row] * len(out_shape), out_shape=out_shape, compiler_params=_cp("parallel"))(*ins)


def _ln_bwd(name, z, dy, g, dep=None):
    T, D = z.shape
    tr = _pick(T, (256, 128, 64, 32, 16))

    def body(z_ref, dy_ref, g_ref, *rest):
        dz_ref, dzb_ref, dg_ref, db_ref = rest[-4:]

        @pl.when(pl.program_id(0) == 0)
        def _():
            dg_ref[...] = jnp.zeros_like(dg_ref)
            db_ref[...] = jnp.zeros_like(db_ref)

        zv = z_ref[...]
        dy_ = dy_ref[...]
        mu = jnp.mean(zv, axis=-1, keepdims=True)
        zc = zv - mu
        rstd = lax.rsqrt(jnp.mean(zc * zc, axis=-1, keepdims=True) + LN_EPS)
        xhat = zc * rstd
        dxh = dy_ * g_ref[...]
        dz = rstd * (dxh - jnp.mean(dxh, axis=-1, keepdims=True)
                     - xhat * jnp.mean(dxh * xhat, axis=-1, keepdims=True))
        dz_ref[...] = dz
        dzb_ref[...] = dz.astype(ACT_DTYPE)
        dg_ref[...] += jnp.sum(dy_ * xhat, axis=0, keepdims=True)
        db_ref[...] += jnp.sum(dy_, axis=0, keepdims=True)

    row = pl.BlockSpec((tr, D), lambda i: (i, 0))
    vec = pl.BlockSpec((1, D), lambda i: (0, 0))
    ins, in_specs = [z, dy, g.reshape(1, D)], [row, row, vec]
    if dep is not None:
        ins.append(dep)
        in_specs.append(_ANY)
    return pl.pallas_call(
        body, name=name, grid=(T // tr,), in_specs=in_specs, out_specs=[row, row, vec, vec],
        out_shape=[jax.ShapeDtypeStruct((T, D), F32), jax.ShapeDtypeStruct((T, D), ACT_DTYPE),
                   jax.ShapeDtypeStruct((1, D), F32), jax.ShapeDtypeStruct((1, D), F32)],
        compiler_params=_cp("arbitrary"))(*ins)


def _loss_fwd_bwd(y, target):
    T, D = y.shape
    tr = _pick(T, (256, 128, 64, 32, 16))

    def body(y_ref, t_ref, dy_ref, l_ref):
        @pl.when(pl.program_id(0) == 0)
        def _():
            l_ref[...] = jnp.zeros_like(l_ref)

        e = y_ref[...] - t_ref[...]
        dy_ref[...] = e * (1.0 / D)
        row = jnp.sum(e * e, axis=-1, keepdims=True) * (1.0 / D)
        l_ref[...] += 0.5 * jnp.sum(row, axis=0, keepdims=True)

    rowspec = pl.BlockSpec((tr, D), lambda i: (i, 0))
    return pl.pallas_call(
        body, name="loss", grid=(T // tr,), in_specs=[rowspec, rowspec],
        out_specs=[rowspec, pl.BlockSpec((1, LANES), lambda i: (0, 0))],
        out_shape=[jax.ShapeDtypeStruct((T, D), F32), jax.ShapeDtypeStruct((1, LANES), F32)],
        compiler_params=_cp("arbitrary"))(y, target)


def _layer_norm_rows(z, g, b):
    mu = jnp.mean(z, axis=-1, keepdims=True)
    zc = z - mu
    var = jnp.mean(zc * zc, axis=-1, keepdims=True)
    return zc * lax.rsqrt(var + LN_EPS) * g + b


def _mixer_out(y_hg, y_cv, h, xin, w_a, w_b, b_b, w_o, alpha, ln_g, ln_b):
    T, D = xin.shape
    tm = _pick(T, (256, 128, 64, 32, 16))

    def body(yhg_ref, ycv_ref, gh_ref, gc_ref, x_ref, wa_ref, wb_ref, bb_ref, wo_ref, g_ref, b_ref,
             yh_ref, yc_ref, m_ref, x1_ref, x1b_ref, z_ref):
        y_h = jnp.dot(yhg_ref[...].astype(MXU_DTYPE), wa_ref[...].astype(MXU_DTYPE), preferred_element_type=F32)
        y_c = jnp.dot(ycv_ref[...].astype(MXU_DTYPE), wb_ref[...].astype(MXU_DTYPE),
                      preferred_element_type=F32) + bb_ref[...]
        yh_ref[...] = y_h
        yc_ref[...] = y_c
        merged = (jax.nn.sigmoid(gh_ref[...]) * y_h + jax.nn.sigmoid(gc_ref[...]) * y_c).astype(ACT_DTYPE)
        m_ref[...] = merged
        z = alpha * x_ref[...] + jnp.dot(merged.astype(MXU_DTYPE), wo_ref[...].astype(MXU_DTYPE),
                                         preferred_element_type=F32)
        z_ref[...] = z
        x1 = _layer_norm_rows(z, g_ref[...], b_ref[...])
        x1_ref[...] = x1
        x1b_ref[...] = x1.astype(ACT_DTYPE)

    row = pl.BlockSpec((tm, D), lambda i: (i, 0))
    mat = pl.BlockSpec((D, D), lambda i: (0, 0))
    vec = pl.BlockSpec((1, D), lambda i: (0, 0))
    f32, act = jax.ShapeDtypeStruct((T, D), F32), jax.ShapeDtypeStruct((T, D), ACT_DTYPE)
    return pl.pallas_call(
        body, name="mixer_out", grid=(T // tm,),
        in_specs=[row, row, pl.BlockSpec((None, tm, D), lambda i: (6, i, 0)),
                  pl.BlockSpec((None, tm, D), lambda i: (7, i, 0)), row, mat, mat, vec, mat, vec, vec],
        out_specs=[row] * 6, out_shape=[f32, f32, act, f32, act, f32],
        compiler_params=_cp("parallel"))(y_hg, y_cv, h, h, xin, w_a, w_b, b_b, w_o, ln_g.reshape(1, D),
                                         ln_b.reshape(1, D))


def _ffn_up_swiglu(x_bf, w_up_t):
    T, D = x_bf.shape
    F = w_up_t.shape[0] // 2
    tn = _pick(F, (256, 128))
    nb = F // tn

    def body(x_ref, wg_ref, wv_ref, up_ref, act_ref):
        xv = x_ref[...].astype(MXU_DTYPE)
        g = lax.dot_general(xv, wg_ref[...].astype(MXU_DTYPE), _NT, preferred_element_type=F32)
        v = lax.dot_general(xv, wv_ref[...].astype(MXU_DTYPE), _NT, preferred_element_type=F32)
        up_ref[0] = g
        up_ref[1] = v
        act_ref[...] = (_silu(g) * v).astype(ACT_DTYPE)

    return pl.pallas_call(
        body, name="ffn_up", grid=(nb,),
        in_specs=[pl.BlockSpec((T, D), lambda j: (0, 0)), pl.BlockSpec((tn, D), lambda j: (j, 0)),
                  pl.BlockSpec((tn, D), lambda j: (nb + j, 0))],
        out_specs=[pl.BlockSpec((2, T, tn), lambda j: (0, 0, j)), pl.BlockSpec((T, tn), lambda j: (0, j))],
        out_shape=[jax.ShapeDtypeStruct((2, T, F), F32), jax.ShapeDtypeStruct((T, F), ACT_DTYPE)],
        compiler_params=_cp("parallel"))(x_bf, w_up_t, w_up_t)


def _ffn_down_ln2(act, w_down, x1, alpha, ln_g, ln_b):
    T, D = x1.shape
    F = act.shape[1]
    tm = _pick(T, (256, 128, 64, 32, 16))

    def body(a_ref, w_ref, x_ref, g_ref, b_ref, x2_ref, x2b_ref, z_ref):
        z = alpha * x_ref[...] + jnp.dot(a_ref[...].astype(MXU_DTYPE), w_ref[...].astype(MXU_DTYPE),
                                         preferred_element_type=F32)
        z_ref[...] = z
        x2 = _layer_norm_rows(z, g_ref[...], b_ref[...])
        x2_ref[...] = x2
        x2b_ref[...] = x2.astype(ACT_DTYPE)

    row = pl.BlockSpec((tm, D), lambda i: (i, 0))
    vec = pl.BlockSpec((1, D), lambda i: (0, 0))
    f32, actt = jax.ShapeDtypeStruct((T, D), F32), jax.ShapeDtypeStruct((T, D), ACT_DTYPE)
    return pl.pallas_call(
        body, name="ffn_down", grid=(T // tm,),
        in_specs=[pl.BlockSpec((tm, F), lambda i: (i, 0)), pl.BlockSpec((F, D), lambda i: (0, 0)), row, vec, vec],
        out_specs=[row] * 3, out_shape=[f32, actt, f32],
        compiler_params=_cp("parallel"))(act, w_down, x1, ln_g.reshape(1, D), ln_b.reshape(1, D))


def _ln2_ffn_down_bwd(z, dy, ln_g, w_down, up, dep=None):
    T, D = z.shape
    F = w_down.shape[0]
    tm = _pick(T, (256, 128, 64, 32, 16))

    def body(z_ref, dy_ref, g_ref, w_ref, up_ref, *rest):
        dz_ref, dzb_ref, dup_ref, dg_ref, db_ref = rest[-5:]

        @pl.when(pl.program_id(0) == 0)
        def _():
            dg_ref[...] = jnp.zeros_like(dg_ref)
            db_ref[...] = jnp.zeros_like(db_ref)

        zv = z_ref[...]
        dy_ = dy_ref[...]
        mu = jnp.mean(zv, axis=-1, keepdims=True)
        zc = zv - mu
        rstd = lax.rsqrt(jnp.mean(zc * zc, axis=-1, keepdims=True) + LN_EPS)
        xhat = zc * rstd
        dxh = dy_ * g_ref[...]
        dz = rstd * (dxh - jnp.mean(dxh, axis=-1, keepdims=True)
                     - xhat * jnp.mean(dxh * xhat, axis=-1, keepdims=True))
        dz_ref[...] = dz
        dzb = dz.astype(ACT_DTYPE)
        dzb_ref[...] = dzb
        dg_ref[...] += jnp.sum(dy_ * xhat, axis=0, keepdims=True)
        db_ref[...] += jnp.sum(dy_, axis=0, keepdims=True)
        da = lax.dot_general(dzb.astype(MXU_DTYPE), w_ref[...].astype(MXU_DTYPE), _NT, preferred_element_type=F32)
        ug = up_ref[0]
        dup_ref[0] = (da * up_ref[1] * _dsilu(ug)).astype(ACT_DTYPE)
        dup_ref[1] = (da * _silu(ug)).astype(ACT_DTYPE)

    row = pl.BlockSpec((tm, D), lambda i: (i, 0))
    vec = pl.BlockSpec((1, D), lambda i: (0, 0))
    blk = pl.BlockSpec((2, tm, F), lambda i: (0, i, 0))
    ins = [z, dy, ln_g.reshape(1, D), w_down, up]
    in_specs = [row, row, vec, pl.BlockSpec((F, D), lambda i: (0, 0)), blk]
    if dep is not None:
        ins.append(dep)
        in_specs.append(_ANY)
    v32 = jax.ShapeDtypeStruct((1, D), F32)
    return pl.pallas_call(
        body, name="ln2_ffn_down_bwd", grid=(T // tm,), in_specs=in_specs,
        out_specs=[row, row, blk, vec, vec],
        out_shape=[jax.ShapeDtypeStruct((T, D), F32), jax.ShapeDtypeStruct((T, D), ACT_DTYPE),
                   jax.ShapeDtypeStruct((2, T, F), ACT_DTYPE), v32, v32],
        compiler_params=_cp("arbitrary"))(*ins)


def _mixer_out_bwd(z, dx1, y_h, y_c, h, w_a, w_b, w_o, ln_g):
    T, D = z.shape
    tm = _pick(T, (256, 128, 64, 32, 16))

    def body(z_ref, dx_ref, yh_ref, yc_ref, gh_ref, gc_ref, wa_ref, wb_ref, wo_ref, g_ref,
             dz_ref, dzb_ref, dyh_ref, dyc_ref, dyhg_ref, dycv_ref, dh_ref, dg_ref, db_ref, dbb_ref):
        @pl.when(pl.program_id(0) == 0)
        def _():
            dg_ref[...] = jnp.zeros_like(dg_ref)
            db_ref[...] = jnp.zeros_like(db_ref)
            dbb_ref[...] = jnp.zeros_like(dbb_ref)

        zv = z_ref[...]
        dy_ = dx_ref[...]
        mu = jnp.mean(zv, axis=-1, keepdims=True)
        zc = zv - mu
        rstd = lax.rsqrt(jnp.mean(zc * zc, axis=-1, keepdims=True) + LN_EPS)
        xhat = zc * rstd
        dxh = dy_ * g_ref[...]
        dz = rstd * (dxh - jnp.mean(dxh, axis=-1, keepdims=True)
                     - xhat * jnp.mean(dxh * xhat, axis=-1, keepdims=True))
        dz_ref[...] = dz
        dzb = dz.astype(ACT_DTYPE)
        dzb_ref[...] = dzb
        dg_ref[...] += jnp.sum(dy_ * xhat, axis=0, keepdims=True)
        db_ref[...] += jnp.sum(dy_, axis=0, keepdims=True)
        dm_ = lax.dot_general(dzb.astype(MXU_DTYPE), wo_ref[...].astype(MXU_DTYPE), _NT, preferred_element_type=F32)
        sh = jax.nn.sigmoid(gh_ref[...])
        sc = jax.nn.sigmoid(gc_ref[...])
        dyc = dm_ * sc
        dyh_b = (dm_ * sh).astype(ACT_DTYPE)
        dyc_b = dyc.astype(ACT_DTYPE)
        dyh_ref[...] = dyh_b
        dyc_ref[...] = dyc_b
        dbb_ref[...] += jnp.sum(dyc, axis=0, keepdims=True)
        dh_ref[0] = (dm_ * yh_ref[...] * sh * (1.0 - sh)).astype(ACT_DTYPE)
        dh_ref[1] = (dm_ * yc_ref[...] * sc * (1.0 - sc)).astype(ACT_DTYPE)
        dyhg_ref[...] = lax.dot_general(dyh_b.astype(MXU_DTYPE), wa_ref[...].astype(MXU_DTYPE), _NT,
                                        preferred_element_type=F32)
        dycv_ref[...] = lax.dot_general(dyc_b.astype(MXU_DTYPE), wb_ref[...].astype(MXU_DTYPE), _NT,
                                        preferred_element_type=F32)

    row = pl.BlockSpec((tm, D), lambda i: (i, 0))
    mat = pl.BlockSpec((D, D), lambda i: (0, 0))
    vec = pl.BlockSpec((1, D), lambda i: (0, 0))
    f32, act = jax.ShapeDtypeStruct((T, D), F32), jax.ShapeDtypeStruct((T, D), ACT_DTYPE)
    v32 = jax.ShapeDtypeStruct((1, D), F32)
    return pl.pallas_call(
        body, name="mixer_out_bwd", grid=(T // tm,),
        in_specs=[row, row, row, row, pl.BlockSpec((None, tm, D), lambda i: (6, i, 0)),
                  pl.BlockSpec((None, tm, D), lambda i: (7, i, 0)), mat, mat, mat, vec],
        out_specs=[row, row, row, row, row, row, pl.BlockSpec((2, tm, D), lambda i: (3, i, 0)), vec, vec, vec],
        out_shape=[f32, act, act, act, f32, f32, jax.ShapeDtypeStruct((N_SEC, T, D), ACT_DTYPE), v32, v32, v32],
        compiler_params=_cp("arbitrary"))(z, dx1, y_h, y_c, h, h, w_a, w_b, w_o, ln_g.reshape(1, D))


def _lb_softmax(x):
    L = x.shape[0]
    rows = [x[l:l + 1] for l in range(L)]
    m = rows[0]
    for r in rows[1:]:
        m = jnp.maximum(m, r)
    e = [jnp.exp(r - m) for r in rows]
    s = e[0]
    for r in e[1:]:
        s = s + r
    return [r / s for r in e]


def _lb_fwd(lb_logits):
    L, D = lb_logits.shape

    def body(x_ref, o_ref):
        p = _lb_softmax(x_ref[...])
        run = jnp.zeros_like(p[0])
        for l in range(L):
            if l > 0:
                run = run + p[l]
            o_ref[pl.ds(l, 1), :] = run

    return pl.pallas_call(body, name="lb_fwd", out_shape=jax.ShapeDtypeStruct((L, D), F32))(lb_logits)


def _lb_bwd(lb_logits, dlbs):
    L, D = lb_logits.shape

    def body(x_ref, d_ref, o_ref):
        p = _lb_softmax(x_ref[...])
        d = d_ref[...]
        dp = [jnp.zeros_like(p[0]) for _ in range(L)]
        run = jnp.zeros_like(p[0])
        for j in range(L - 1, 0, -1):
            run = run + d[j:j + 1]
            dp[j] = run
        dot = dp[0] * p[0]
        for j in range(1, L):
            dot = dot + dp[j] * p[j]
        for j in range(L):
            o_ref[pl.ds(j, 1), :] = p[j] * (dp[j] - dot)

    return pl.pallas_call(body, name="lb_bwd", out_shape=jax.ShapeDtypeStruct((L, D), F32))(lb_logits, dlbs)


def _blk_cumsum(x, c, reverse=False):
    n = x.shape[0]
    pos = lax.broadcasted_iota(jnp.int32, x.shape, 0) % c
    s = 1
    while s < c:
        if reverse:
            shifted = pltpu.roll(x, n - s, 0)
            x = x + jnp.where(pos + s < c, shifted, 0.0)
        else:
            shifted = pltpu.roll(x, s, 0)
            x = x + jnp.where(pos >= s, shifted, 0.0)
        s *= 2
    return x


def _hgrn_prologue(q_ref, f_ref, lb_ref):
    lbv = lb_ref[...]
    z = f_ref[...]
    sig = jax.nn.sigmoid(z)
    one_m = 1.0 - lbv
    f = lbv + one_m * sig
    logf = jnp.log(jnp.maximum(f, F_MIN))
    k = one_m * jax.nn.sigmoid(-z)
    q = _silu(q_ref[...])
    return q, k, logf, f, sig, one_m


def _hgrn_fwd(h, lbs_l, gw):
    _, T, D = h.shape
    nh = D // LANES
    c = HG_C
    Tt = _pick(T, (512, 256, 128, 64, 32, 16))
    nb = Tt // c
    ng = c // SUB

    def body(q_ref, f_ref, i_ref, g_ref, lb_ref, gw_ref, o_ref, y_ref, sall_ref,
             st_ref, G_s, q_s, k_s, W_s, R_s, dS_s, o_s):
        @pl.when(pl.program_id(1) == 0)
        def _():
            st_ref[...] = jnp.zeros_like(st_ref)

        q, k, logf, _, _, _ = _hgrn_prologue(q_ref, f_ref, lb_ref)
        G_s[...] = _blk_cumsum(logf, c) * LOG2E
        q_s[...] = q
        k_s[...] = k
        ones = jnp.ones((LANES, LANES), MXU_DTYPE)
        rowid = lax.broadcasted_iota(jnp.int32, (SUB, LANES), 0)
        zero = jnp.zeros((SUB, LANES), F32)
        for bi in range(nb):
            r0 = bi * c
            glast = G_s[pl.ds(r0 + c - 1, 1), :]
            kd = k_s[pl.ds(r0, c), :] * jnp.exp2(glast - G_s[pl.ds(r0, c), :])
            dS_s[bi] = lax.dot_general(i_ref[pl.ds(r0, c), :].astype(MXU_DTYPE), kd.astype(MXU_DTYPE), _TN,
                                       preferred_element_type=F32)
        st = st_ref[...]
        for bi in range(nb):
            sall_ref[bi] = st
            st = st * jnp.exp2(G_s[pl.ds(bi * c + c - 1, 1), :]) + dS_s[bi]
        st_ref[...] = st
        for bi in range(nb):
            r0 = bi * c
            qd = q_s[pl.ds(r0, c), :] * jnp.exp2(G_s[pl.ds(r0, c), :])
            o_s[pl.ds(r0, c), :] = lax.dot_general(qd.astype(MXU_DTYPE), sall_ref[bi].astype(MXU_DTYPE), _NT,
                                                   preferred_element_type=F32)
        for bi in range(nb):
            r0 = bi * c
            w0 = bi * c * c
            Gg = [G_s[pl.ds(r0 + gi * SUB, SUB), :] for gi in range(ng)]
            qg = [q_s[pl.ds(r0 + gi * SUB, SUB), :] for gi in range(ng)]
            for s in range(c):
                gs = G_s[pl.ds(r0 + s, 1), :]
                ks = k_s[pl.ds(r0 + s, 1), :]
                parts = []
                for gi in range(ng):
                    if gi < s // SUB:
                        parts.append(zero)
                        continue
                    e = jnp.exp2(jnp.minimum(Gg[gi] - gs, 0.0))
                    if gi == s // SUB:
                        e = jnp.where(rowid >= s - gi * SUB, e, 0.0)
                    parts.append(e * qg[gi] * ks)
                W_s[pl.ds(w0 + s * c, c), :] = jnp.concatenate(parts, axis=0).astype(MXU_DTYPE)
        R_s[...] = jnp.dot(W_s[...], ones, preferred_element_type=F32)
        for bi in range(nb):
            r0 = bi * c
            w0 = bi * c * c
            acc = [o_s[pl.ds(r0 + gi * SUB, SUB), :] for gi in range(ng)]
            for s in range(c):
                vs = i_ref[pl.ds(r0 + s, 1), :]
                for gi in range(s // SUB, ng):
                    acc[gi] = acc[gi] + R_s[pl.ds(w0 + s * c + gi * SUB, SUB), :] * vs
            o_s[pl.ds(r0, c), :] = jnp.concatenate(acc, axis=0)
        o = o_s[...]
        n = o * lax.rsqrt(jnp.mean(o * o, axis=-1, keepdims=True) + RMS_EPS)
        o_ref[...] = o
        y_ref[...] = (n * gw_ref[...] * _silu(g_ref[...])).astype(ACT_DTYPE)

    def sec(s):
        return pl.BlockSpec((None, Tt, LANES), lambda hd, i: (s, i, hd))

    col = pl.BlockSpec((Tt, LANES), lambda hd, i: (i, hd))
    return pl.pallas_call(
        body, name="hgrn_fwd", grid=(nh, T // Tt),
        in_specs=[sec(0), sec(1), sec(2), sec(3), pl.BlockSpec((1, LANES), lambda hd, i: (0, hd)),
                  pl.BlockSpec((1, LANES), lambda hd, i: (0, 0))],
        out_specs=[col, col, pl.BlockSpec((nb, None, LANES, LANES), lambda hd, i: (i, hd, 0, 0))],
        out_shape=[jax.ShapeDtypeStruct((T, D), F32), jax.ShapeDtypeStruct((T, D), ACT_DTYPE),
                   jax.ShapeDtypeStruct((T // c, nh, LANES, LANES), F32)],
        scratch_shapes=[pltpu.VMEM((LANES, LANES), F32), pltpu.VMEM((Tt, LANES), F32),
                        pltpu.VMEM((Tt, LANES), F32), pltpu.VMEM((Tt, LANES), F32),
                        pltpu.VMEM((nb * c * c, LANES), MXU_DTYPE), pltpu.VMEM((nb * c * c, LANES), F32),
                        pltpu.VMEM((nb, LANES, LANES), F32), pltpu.VMEM((Tt, LANES), F32)],
        compiler_params=_cp("parallel", "arbitrary"))(h, h, h, h, lbs_l, gw)


def _hgrn_bwd(h, lbs_l, gw, o_pre, st_all, dy, dh):
    _, T, D = h.shape
    nh = D // LANES
    c = HG_C
    Tt = _pick(T, (512, 256, 128, 64, 32, 16))
    nb = Tt // c
    ng = c // SUB
    nT = T // Tt

    def body(q_ref, f_ref, i_ref, g_ref, lb_ref, gw_ref, o_ref, sall_ref, dy_ref, dh_in_ref,
             dh_ref, dlb_ref, dgw_ref,
             dst_ref, G_s, q_s, k_s, do_s, E_s, WP_s, dq_s, dk_s, dv_s, dG_s,
             R_s, dS_s, dstA_s, dqd_s, dkd_s, dvi_s, da_s):
        del dh_in_ref
        hd, ti = pl.program_id(0), pl.program_id(1)

        @pl.when(ti == 0)
        def _():
            dst_ref[...] = jnp.zeros_like(dst_ref)
            dlb_ref[...] = jnp.zeros_like(dlb_ref)

        @pl.when((ti == 0) & (hd == 0))
        def _():
            dgw_ref[...] = jnp.zeros_like(dgw_ref)

        q, k, logf, f, sig, one_m = _hgrn_prologue(q_ref, f_ref, lb_ref)
        G_s[...] = _blk_cumsum(logf, c) * LOG2E
        q_s[...] = q
        k_s[...] = k

        o = o_ref[...]
        gr = g_ref[...]
        dy_ = dy_ref[...]
        rr = lax.rsqrt(jnp.mean(o * o, axis=-1, keepdims=True) + RMS_EPS)
        n = o * rr
        sg = _silu(gr)
        gwv = gw_ref[...]
        dh_ref[3] = (dy_ * n * gwv * _dsilu(gr)).astype(ACT_DTYPE)
        dgw_ref[...] += jnp.sum(dy_ * n * sg, axis=0, keepdims=True)
        dn = dy_ * gwv * sg
        do_s[...] = rr * (dn - n * jnp.mean(dn * n, axis=-1, keepdims=True))

        ones = jnp.ones((LANES, LANES), MXU_DTYPE)
        rowid = lax.broadcasted_iota(jnp.int32, (SUB, LANES), 0)
        rowid_c = lax.broadcasted_iota(jnp.int32, (c, LANES), 0)
        zero = jnp.zeros((SUB, LANES), F32)
        cc = c * c
        for bi in range(nb):
            r0 = bi * c
            qd = q_s[pl.ds(r0, c), :] * jnp.exp2(G_s[pl.ds(r0, c), :])
            dS_s[bi] = lax.dot_general(do_s[pl.ds(r0, c), :].astype(MXU_DTYPE), qd.astype(MXU_DTYPE), _TN,
                                       preferred_element_type=F32)
        dst = dst_ref[...]
        for bi in range(nb - 1, -1, -1):
            dstA_s[bi] = dst
            dst = dst * jnp.exp2(G_s[pl.ds(bi * c + c - 1, 1), :]) + dS_s[bi]
        dst_ref[...] = dst
        for bi in range(nb):
            r0 = bi * c
            glast = G_s[pl.ds(r0 + c - 1, 1), :]
            kd = k_s[pl.ds(r0, c), :] * jnp.exp2(glast - G_s[pl.ds(r0, c), :])
            st = sall_ref[bi]
            dstb = dstA_s[bi]
            dst_m = dstb.astype(MXU_DTYPE)
            dqd_s[pl.ds(r0, c), :] = lax.dot_general(do_s[pl.ds(r0, c), :].astype(MXU_DTYPE), st.astype(MXU_DTYPE),
                                                     _NN, preferred_element_type=F32)
            dkd_s[pl.ds(r0, c), :] = lax.dot_general(i_ref[pl.ds(r0, c), :].astype(MXU_DTYPE), dst_m, _NN,
                                                     preferred_element_type=F32)
            dvi_s[pl.ds(r0, c), :] = lax.dot_general(kd.astype(MXU_DTYPE), dst_m, _NT,
                                                     preferred_element_type=F32)
            da_s[pl.ds(bi * SUB, 1), :] = jnp.sum(dstb * st, axis=0, keepdims=True)
        for bi in range(nb):
            r0 = bi * c
            e0, w0 = bi * cc, bi * 2 * cc
            Gg = [G_s[pl.ds(r0 + gi * SUB, SUB), :] for gi in range(ng)]
            kg = [k_s[pl.ds(r0 + gi * SUB, SUB), :] for gi in range(ng)]
            vg = [i_ref[pl.ds(r0 + gi * SUB, SUB), :] for gi in range(ng)]
            for t in range(c):
                gt = G_s[pl.ds(r0 + t, 1), :]
                qt = q_s[pl.ds(r0 + t, 1), :]
                dot_ = do_s[pl.ds(r0 + t, 1), :]
                ep, wp, pp = [], [], []
                for gi in range(ng):
                    if gi > t // SUB:
                        ep.append(zero)
                        wp.append(zero)
                        pp.append(zero)
                        continue
                    e = jnp.exp2(jnp.minimum(gt - Gg[gi], 0.0))
                    if gi == t // SUB:
                        e = jnp.where(rowid <= t - gi * SUB, e, 0.0)
                    ep.append(e)
                    wp.append(e * kg[gi] * qt)
                    pp.append(vg[gi] * dot_)
                E_s[pl.ds(e0 + t * c, c), :] = jnp.concatenate(ep, axis=0)
                WP_s[pl.ds(w0 + t * c, c), :] = jnp.concatenate(wp, axis=0).astype(MXU_DTYPE)
                WP_s[pl.ds(w0 + cc + t * c, c), :] = jnp.concatenate(pp, axis=0).astype(MXU_DTYPE)
        R_s[...] = jnp.dot(WP_s[...], ones, preferred_element_type=F32)
        for bi in range(nb):
            r0 = bi * c
            e0, w0 = bi * cc, bi * 2 * cc
            kg = [k_s[pl.ds(r0 + gi * SUB, SUB), :] for gi in range(ng)]
            dk_g = [zero] * ng
            dv_g = [zero] * ng
            dq_g = [zero] * ng
            for t in range(c):
                qt = q_s[pl.ds(r0 + t, 1), :]
                dot_ = do_s[pl.ds(r0 + t, 1), :]
                tot = None
                for gi in range(t // SUB + 1):
                    lo = t * c + gi * SUB
                    dae = R_s[pl.ds(w0 + cc + lo, SUB), :] * E_s[pl.ds(e0 + lo, SUB), :]
                    z = dae * kg[gi]
                    tot = z if tot is None else tot + z
                    dk_g[gi] = dk_g[gi] + dae * qt
                    dv_g[gi] = dv_g[gi] + R_s[pl.ds(w0 + lo, SUB), :] * dot_
                gt_ = t // SUB
                dq_g[gt_] = jnp.where(rowid == t - gt_ * SUB, jnp.sum(tot, axis=0, keepdims=True), dq_g[gt_])
            dq_i = jnp.concatenate(dq_g, axis=0)
            dk_i = jnp.concatenate(dk_g, axis=0)
            dv_i = jnp.concatenate(dv_g, axis=0)
            Gb = G_s[pl.ds(r0, c), :]
            qb = q_s[pl.ds(r0, c), :]
            kb = k_s[pl.ds(r0, c), :]
            glast = G_s[pl.ds(r0 + c - 1, 1), :]
            eg = jnp.exp2(Gb)
            egl = jnp.exp2(glast - Gb)
            dqd = dqd_s[pl.ds(r0, c), :]
            dkd = dkd_s[pl.ds(r0, c), :]
            dq_s[pl.ds(r0, c), :] = dqd * eg + dq_i
            dk_s[pl.ds(r0, c), :] = dkd * egl + dk_i
            dv_s[pl.ds(r0, c), :] = dvi_s[pl.ds(r0, c), :] + dv_i
            dkdkd = dkd * kb * egl
            dG = dqd * qb * eg + qb * dq_i - kb * dk_i - dkdkd
            dglast = jnp.sum(dkdkd, axis=0, keepdims=True) + da_s[pl.ds(bi * SUB, 1), :] * jnp.exp2(glast)
            dG_s[pl.ds(r0, c), :] = dG + jnp.where(rowid_c == c - 1, dglast, 0.0)

        dlogf = _blk_cumsum(dG_s[...], c, reverse=True)
        df = jnp.where(f > F_MIN, dlogf / f, 0.0)
        dk = dk_s[...]
        dh_ref[0] = (dq_s[...] * _dsilu(q_ref[...])).astype(ACT_DTYPE)
        dh_ref[1] = ((df - dk) * one_m * sig * (1.0 - sig)).astype(ACT_DTYPE)
        dh_ref[2] = dv_s[...].astype(ACT_DTYPE)
        dlb_ref[...] += jnp.sum((df - dk) * (1.0 - sig), axis=0, keepdims=True)

    def sec(s):
        return pl.BlockSpec((None, Tt, LANES), lambda hd, i: (s, nT - 1 - i, hd))

    col = pl.BlockSpec((Tt, LANES), lambda hd, i: (nT - 1 - i, hd))
    tile = pltpu.VMEM((Tt, LANES), F32)
    return pl.pallas_call(
        body, name="hgrn_bwd", grid=(nh, nT),
        in_specs=[sec(0), sec(1), sec(2), sec(3), pl.BlockSpec((1, LANES), lambda hd, i: (0, hd)),
                  pl.BlockSpec((1, LANES), lambda hd, i: (0, 0)), col,
                  pl.BlockSpec((nb, None, LANES, LANES), lambda hd, i: (nT - 1 - i, hd, 0, 0)), col,
                  pl.BlockSpec(memory_space=pl.ANY)],
        out_specs=[pl.BlockSpec((4, Tt, LANES), lambda hd, i: (0, nT - 1 - i, hd)),
                   pl.BlockSpec((1, LANES), lambda hd, i: (0, hd)),
                   pl.BlockSpec((1, LANES), lambda hd, i: (0, 0))],
        out_shape=[jax.ShapeDtypeStruct(dh.shape, dh.dtype), jax.ShapeDtypeStruct((1, D), F32),
                   jax.ShapeDtypeStruct((1, LANES), F32)],
        scratch_shapes=[pltpu.VMEM((LANES, LANES), F32), tile, tile, tile, tile,
                        pltpu.VMEM((nb * c * c, LANES), F32), pltpu.VMEM((2 * nb * c * c, LANES), MXU_DTYPE),
                        tile, tile, tile, tile,
                        pltpu.VMEM((2 * nb * c * c, LANES), F32), pltpu.VMEM((nb, LANES, LANES), F32),
                        pltpu.VMEM((nb, LANES, LANES), F32), tile, tile, tile, pltpu.VMEM((nb * SUB, LANES), F32)],
        input_output_aliases={9: 0},
        compiler_params=_cp("arbitrary", "arbitrary"))(h, h, h, h, lbs_l, gw, o_pre, st_all, dy, dh)


def _shifted_copies(src, cs, dst, rows):
    for b in range(1, SUB):
        dst[b - 1] = src[pl.ds(b, rows + CONV_HALO - SUB), cs]


def _shifted(src, cs, copies, shift, rows):
    a8, b = divmod(shift, SUB)
    if b == 0:
        return src[pl.ds(shift, rows), cs]
    return copies[b - 1, pl.ds(a8 * SUB, rows), :]


def _conv_fwd(h, w_dw, b_dw, ln_g, ln_b):
    _, T, D = h.shape
    Tt = _pick(T, (256, 128, 64, 32))
    hb = Tt // CONV_HALO
    off = CONV_HALO - (CONV_WIDTH - 1)

    def body(a_ref, b_ref, ap_ref, bp_ref, w_ref, bd_ref, g_ref, be_ref, yc_ref, y_ref, U_s, Ub_s):
        first = pl.program_id(0) == 0
        up = ap_ref[...] * jax.nn.sigmoid(bp_ref[...])
        U_s[pl.ds(0, CONV_HALO), :] = jnp.where(first, 0.0, up)
        U_s[pl.ds(CONV_HALO, Tt), :] = a_ref[...] * jax.nn.sigmoid(b_ref[...])
        for cb in range(D // LANES):
            cs = pl.ds(cb * LANES, LANES)
            _shifted_copies(U_s, cs, Ub_s, Tt)
            acc = jnp.zeros((Tt, LANES), F32)
            for j in range(CONV_WIDTH):
                acc = acc + w_ref[pl.ds(j, 1), cs] * _shifted(U_s, cs, Ub_s, off + j, Tt)
            yc_ref[:, cs] = acc + bd_ref[:, cs]
        yc = yc_ref[...]
        mu = jnp.mean(yc, axis=-1, keepdims=True)
        zc = yc - mu
        var = jnp.mean(zc * zc, axis=-1, keepdims=True)
        ln = zc * lax.rsqrt(var + LN_EPS) * g_ref[...] + be_ref[...]
        y_ref[...] = _silu(ln).astype(ACT_DTYPE)

    def main(s):
        return pl.BlockSpec((None, Tt, D), lambda i: (s, i, 0))

    def prev(s):
        return pl.BlockSpec((None, CONV_HALO, D), lambda i: (s, jnp.maximum(i * hb - 1, 0), 0))

    row = pl.BlockSpec((Tt, D), lambda i: (i, 0))
    vec = pl.BlockSpec((1, D), lambda i: (0, 0))
    return pl.pallas_call(
        body, name="conv_fwd", grid=(T // Tt,),
        in_specs=[main(4), main(5), prev(4), prev(5), pl.BlockSpec((CONV_HALO, D), lambda i: (0, 0)),
                  vec, vec, vec],
        out_specs=[row, row],
        out_shape=[jax.ShapeDtypeStruct((T, D), F32), jax.ShapeDtypeStruct((T, D), ACT_DTYPE)],
        scratch_shapes=[pltpu.VMEM((CONV_HALO + Tt, D), F32),
                        pltpu.VMEM((SUB - 1, Tt + CONV_HALO - SUB, LANES), F32)],
        compiler_params=_cp("parallel"))(h, h, h, h, w_dw, b_dw, ln_g, ln_b)


def _conv_bwd(h, w_dw, ln_g, ln_b, yc, dy, dh):
    _, T, D = h.shape
    Tt = _pick(T, (256, 128, 64, 32))
    hb = Tt // CONV_HALO
    nT = T // Tt
    nhb = T // CONV_HALO
    off = CONV_HALO - (CONV_WIDTH - 1)

    def body(a_ref, b_ref, ap_ref, bp_ref, w_ref, g_ref, be_ref, yc_ref, ycn_ref, dy_ref, dyn_ref, dh_in_ref,
             dh_ref, dw_ref, dbd_ref, dg_ref, dbe_ref, U_s, DY_s, du_s, Ub_s, DYb_s):
        del dh_in_ref
        i = pl.program_id(0)

        @pl.when(i == 0)
        def _():
            dw_ref[...] = jnp.zeros_like(dw_ref)
            dbd_ref[...] = jnp.zeros_like(dbd_ref)
            dg_ref[...] = jnp.zeros_like(dg_ref)
            dbe_ref[...] = jnp.zeros_like(dbe_ref)

        gv = g_ref[...]
        bev = be_ref[...]

        def ln_silu_bwd(ycv, dyv):
            mu = jnp.mean(ycv, axis=-1, keepdims=True)
            zc = ycv - mu
            rstd = lax.rsqrt(jnp.mean(zc * zc, axis=-1, keepdims=True) + LN_EPS)
            xhat = zc * rstd
            dln = dyv * _dsilu(xhat * gv + bev)
            dxh = dln * gv
            dyc = rstd * (dxh - jnp.mean(dxh, axis=-1, keepdims=True)
                          - xhat * jnp.mean(dxh * xhat, axis=-1, keepdims=True))
            return dyc, dln, xhat

        dyc, dln, xhat = ln_silu_bwd(yc_ref[...], dy_ref[...])
        dg_ref[...] += jnp.sum(dln * xhat, axis=0, keepdims=True)
        dbe_ref[...] += jnp.sum(dln, axis=0, keepdims=True)
        dbd_ref[...] += jnp.sum(dyc, axis=0, keepdims=True)
        DY_s[pl.ds(0, Tt), :] = dyc
        dycn, _, _ = ln_silu_bwd(ycn_ref[...], dyn_ref[...])
        DY_s[pl.ds(Tt, CONV_HALO), :] = jnp.where(i == nT - 1, 0.0, dycn)

        sb = jax.nn.sigmoid(b_ref[...])
        av = a_ref[...]
        up = ap_ref[...] * jax.nn.sigmoid(bp_ref[...])
        U_s[pl.ds(0, CONV_HALO), :] = jnp.where(i == 0, 0.0, up)
        U_s[pl.ds(CONV_HALO, Tt), :] = av * sb

        for cb in range(D // LANES):
            cs = pl.ds(cb * LANES, LANES)
            _shifted_copies(U_s, cs, Ub_s, Tt)
            _shifted_copies(DY_s, cs, DYb_s, Tt)
            dyb = DY_s[pl.ds(0, Tt), cs]
            acc = jnp.zeros((Tt, LANES), F32)
            for j in range(CONV_WIDTH):
                acc = acc + w_ref[pl.ds(j, 1), cs] * _shifted(DY_s, cs, DYb_s, CONV_WIDTH - 1 - j, Tt)
                dw_ref[pl.ds(j, 1), cs] += jnp.sum(dyb * _shifted(U_s, cs, Ub_s, off + j, Tt), axis=0, keepdims=True)
            du_s[:, cs] = acc
        du = du_s[...]
        dh_ref[0] = (du * sb).astype(ACT_DTYPE)
        dh_ref[1] = (du * av * sb * (1.0 - sb)).astype(ACT_DTYPE)

    def main(s):
        return pl.BlockSpec((None, Tt, D), lambda i: (s, i, 0))

    def prev(s):
        return pl.BlockSpec((None, CONV_HALO, D), lambda i: (s, jnp.maximum(i * hb - 1, 0), 0))

    row = pl.BlockSpec((Tt, D), lambda i: (i, 0))
    nxt = pl.BlockSpec((CONV_HALO, D), lambda i: (jnp.minimum((i + 1) * hb, nhb - 1), 0))
    vec = pl.BlockSpec((1, D), lambda i: (0, 0))
    wspec = pl.BlockSpec((CONV_HALO, D), lambda i: (0, 0))
    return pl.pallas_call(
        body, name="conv_bwd", grid=(nT,),
        in_specs=[main(4), main(5), prev(4), prev(5), wspec, vec, vec, row, nxt, row, nxt,
                  pl.BlockSpec(memory_space=pl.ANY)],
        out_specs=[pl.BlockSpec((2, Tt, D), lambda i: (2, i, 0)), wspec, vec, vec, vec],
        out_shape=[jax.ShapeDtypeStruct(dh.shape, dh.dtype), jax.ShapeDtypeStruct((CONV_HALO, D), F32),
                   jax.ShapeDtypeStruct((1, D), F32), jax.ShapeDtypeStruct((1, D), F32),
                   jax.ShapeDtypeStruct((1, D), F32)],
        scratch_shapes=[pltpu.VMEM((CONV_HALO + Tt, D), F32), pltpu.VMEM((Tt + CONV_HALO, D), F32),
                        pltpu.VMEM((Tt, D), F32),
                        pltpu.VMEM((SUB - 1, Tt + CONV_HALO - SUB, LANES), F32),
                        pltpu.VMEM((SUB - 1, Tt + CONV_HALO - SUB, LANES), F32)],
        input_output_aliases={11: 0},
        compiler_params=_cp("arbitrary"))(h, h, h, h, w_dw, ln_g, ln_b, yc, yc, dy, dy, dh)


def _adamw(name, w, m, v, parts, part_specs, tr, prefetch=None, nsteps=None, row_map=None, prev=None):
    R, C = w.shape
    bc1 = 1.0 - ADAM_B1 ** ADAM_STEP
    bc2 = 1.0 - ADAM_B2 ** ADAM_STEP
    npart = len(parts)
    npre = 0 if prefetch is None else 1
    nprev = 0 if prev is None else 4

    def body(*refs):
        refs = refs[npre:]
        w_ref, m_ref, v_ref = refs[:3]
        p_refs = refs[3:3 + npart]
        g_ref, d_ref, mo_ref, vo_ref = refs[3 + npart + nprev:]
        g = p_refs[0][...].astype(F32)
        for p in p_refs[1:]:
            g = g + p[...].astype(F32)
        wv = w_ref[...]
        mn = ADAM_B1 * m_ref[...] + (1.0 - ADAM_B1) * g
        vn = ADAM_B2 * v_ref[...] + (1.0 - ADAM_B2) * (g * g)
        m_hat = mn / bc1
        v_hat = vn / bc2
        g_ref[...] = g
        d_ref[...] = -ADAM_LR * (m_hat / (jnp.sqrt(v_hat) + ADAM_EPS) + ADAM_WD * wv)
        mo_ref[...] = mn
        vo_ref[...] = vn

    if row_map is None:
        row_map = (lambda i: (i, 0)) if prefetch is None else (lambda i, s: (i, 0))
    row = pl.BlockSpec((tr, C), row_map)
    out = jax.ShapeDtypeStruct((R, C), F32)
    gs = pltpu.PrefetchScalarGridSpec(
        num_scalar_prefetch=npre, grid=(R // tr if nsteps is None else nsteps,),
        in_specs=[row, row, row] + list(part_specs) + [_ANY] * nprev, out_specs=[row] * 4)
    args = ([prefetch] if npre else []) + [w, m, v] + list(parts) + (list(prev) if nprev else [])
    first_prev = npre + 3 + npart
    return pl.pallas_call(body, name=name, grid_spec=gs, out_shape=[out] * 4,
                          input_output_aliases={first_prev + i: i for i in range(nprev)},
                          compiler_params=_cp("parallel"))(*args)


def _pair_add(p, r1, my_c):
    _, R, C = r1.shape
    tr = max(t for t in range(16, 1025, 16) if R % t == 0)

    def body(c_ref, p_ref, r_ref, q_ref):
        del c_ref
        q_ref[...] = (p_ref[...].astype(F32) + r_ref[...].astype(F32)).astype(q_ref.dtype)

    gs = pltpu.PrefetchScalarGridSpec(
        num_scalar_prefetch=1, grid=(4, R // tr),
        in_specs=[pl.BlockSpec((None, tr, C), lambda j, i, c: (2 * j + c[0], i, 0)),
                  pl.BlockSpec((None, tr, C), lambda j, i, c: (j, i, 0))],
        out_specs=pl.BlockSpec((None, tr, C), lambda j, i, c: (j, i, 0)))
    return pl.pallas_call(body, name="pair_add", grid_spec=gs, out_shape=jax.ShapeDtypeStruct(r1.shape, r1.dtype),
                          compiler_params=_cp("parallel", "parallel"))(my_c, p, r1)


def _place():
    x, y, c = lax.axis_index("x"), lax.axis_index("y"), lax.axis_index("c")
    chips = [(1 - x, y), (x, 1 - y), (1 - x, 1 - y)]
    return x, y, c, chips


def _hbm(a):
    return pltpu.with_memory_space_constraint(a, pltpu.HBM)


def _gather_targets():
    x, y, c, chips = _place()
    return 4 * x + 2 * y + c, [(x, y, 1 - c)] + [(*chip, c) for chip in chips]


def _gather_start(name, shards, zones, after=None):
    n = len(shards)
    lands = [_hbm(z) for z in zones]
    n_in = 2 * n + (0 if after is None else 1)

    def body(*refs):
        srcs, zones = refs[:n], refs[n:2 * n]
        send, recv, token = refs[n_in], refs[n_in + 1], refs[-1]
        mine, targets = _gather_targets()
        for a in range(n):
            for k, to in enumerate(targets):
                pltpu.make_async_remote_copy(
                    src_ref=srcs[a], dst_ref=zones[a].at[mine], send_sem=send.at[4 * a + k],
                    recv_sem=recv.at[4 * a + k], device_id=to, device_id_type=MESH).start()
        token[...] = jnp.zeros_like(token)

    sem = pltpu.SemaphoreType.DMA((4 * n,))
    out_shape = ([sem, sem] + [pltpu.HBM(s.shape, s.dtype) for s in shards]
                 + [pltpu.HBM(z.shape, z.dtype) for z in lands] + [jax.ShapeDtypeStruct((8, LANES), F32)])
    outs = pl.pallas_call(
        body, name=name, out_shape=out_shape, in_specs=[_HBM] * (2 * n) + ([] if after is None else [_ANY]),
        out_specs=[_SEM, _SEM] + [_HBM] * (2 * n) + [pl.BlockSpec(memory_space=pltpu.VMEM)],
        input_output_aliases={i: 2 + i for i in range(2 * n)},
        compiler_params=pltpu.CompilerParams(has_side_effects=_EFFECT))(
            *[_hbm(s) for s in shards], *lands, *([] if after is None else [after]))
    return outs[0], outs[1], list(outs[2:2 + n]), list(outs[2 + n:2 + 2 * n]), outs[-1]


def _gather_wait(name, shards, zones, send, recv, after):
    per = len(shards)

    def body(*refs):
        srcs, lz = refs[:per], refs[per:2 * per]
        send_s, recv_s = refs[2 * per], refs[2 * per + 1]
        mine, targets = _gather_targets()
        for a in range(per):
            for k, to in enumerate(targets):
                cp = pltpu.make_async_remote_copy(
                    src_ref=srcs[a], dst_ref=lz[a].at[mine], send_sem=send_s.at[4 * a + k],
                    recv_sem=recv_s.at[4 * a + k], device_id=to, device_id_type=MESH)
                cp.wait_send()
                cp.wait_recv()

    outs = pl.pallas_call(
        body, name=name, out_shape=[pltpu.HBM(s.shape, s.dtype) for s in shards + zones],
        in_specs=[_HBM] * (2 * per) + [_SEM, _SEM, _ANY], out_specs=[_HBM] * (2 * per),
        input_output_aliases={i: i for i in range(2 * per)},
        compiler_params=pltpu.CompilerParams(has_side_effects=_EFFECT))(*shards, *zones, send, recv, after)
    return outs[:per], outs[per:]


def _gather_finish(zones):
    n = len(zones)

    def body(*refs):
        lz = refs[n:2 * n]
        send_sems, recv_sems = refs[2 * n:]
        x, y, c, chips = _place()

        def fwd(a, j, pc):
            cx, cy = chips[j]
            blk = lz[a].at[4 * cx + 2 * cy + pc]
            return pltpu.make_async_remote_copy(
                src_ref=blk, dst_ref=blk, send_sem=send_sems.at[3 * a + j], recv_sem=recv_sems.at[3 * a + j],
                device_id=(x, y, 1 - c), device_id_type=MESH)

        sends = [fwd(a, j, c) for a in range(n) for j in range(3)]
        for cp in sends:
            cp.start()
        for a in range(n):
            for j in range(3):
                fwd(a, j, 1 - c).wait_recv()
        for cp in sends:
            cp.wait_send()

    return pl.pallas_call(
        body, name="gather_finish", out_shape=[jax.ShapeDtypeStruct(z.shape, z.dtype) for z in zones],
        in_specs=[_ANY] * n, out_specs=[_ANY] * n, input_output_aliases={a: a for a in range(n)},
        scratch_shapes=[pltpu.SemaphoreType.DMA((3 * n,)), pltpu.SemaphoreType.DMA((3 * n,))])(*zones)


def _place_own(shard, dev):
    R, C = shard.shape
    tr = max(t for t in range(16, 1025, 16) if R % t == 0)

    def body(d_ref, s_ref, z_ref):
        del d_ref
        z_ref[...] = s_ref[...]

    gs = pltpu.PrefetchScalarGridSpec(
        num_scalar_prefetch=1, grid=(R // tr,), in_specs=[pl.BlockSpec((tr, C), lambda i, d: (i, 0))],
        out_specs=pl.BlockSpec((None, tr, C), lambda i, d: (d[0], i, 0)))
    return pl.pallas_call(body, name="place_own", grid_spec=gs,
                          out_shape=jax.ShapeDtypeStruct((N_DEV, R, C), shard.dtype),
                          compiler_params=_cp("parallel"))(dev, shard)


def _exchange_sibling(bufs):
    n_arr = len(bufs)

    def body(*refs):
        srcs, outs = refs[:n_arr], refs[n_arr:2 * n_arr]
        send_sems, recv_sems = refs[2 * n_arr:]
        x, y, c, _ = _place()
        copies = []
        for n in range(n_arr):
            for j in range(4):
                copies.append(pltpu.make_async_remote_copy(
                    src_ref=srcs[n].at[2 * j + 1 - c], dst_ref=outs[n].at[j],
                    send_sem=send_sems.at[4 * n + j], recv_sem=recv_sems.at[4 * n + j],
                    device_id=(x, y, 1 - c), device_id_type=MESH))
        for cp in copies:
            cp.start()
        for cp in copies:
            cp.wait()

    return pl.pallas_call(
        body, name="exchange_sibling",
        out_shape=[jax.ShapeDtypeStruct((4,) + b.shape[1:], b.dtype) for b in bufs],
        in_specs=[_ANY] * n_arr, out_specs=[_ANY] * n_arr,
        scratch_shapes=[pltpu.SemaphoreType.DMA((4 * n_arr,)), pltpu.SemaphoreType.DMA((4 * n_arr,))])(*bufs)


def _chip_copies(srcs, zones, send, recv):
    _, _, c, chips = _place()
    return [pltpu.make_async_remote_copy(
        src_ref=srcs[n].at[2 * cx + cy], dst_ref=zones[n].at[k], send_sem=send.at[3 * n + k],
        recv_sem=recv.at[3 * n + k], device_id=(cx, cy, c), device_id_type=MESH)
        for n in range(len(srcs)) for k, (cx, cy) in enumerate(chips)]


def _exchange_chips_start(name, bufs, after=None):
    n = len(bufs)
    n_in = 2 * n + (0 if after is None else 1)
    lands = [_hbm(lax.empty((3,) + b.shape[1:], b.dtype)) for b in bufs]

    def body(*refs):
        srcs, zones = refs[:n], refs[n:2 * n]
        send, recv, token = refs[n_in], refs[n_in + 1], refs[-1]
        for cp in _chip_copies(srcs, zones, send, recv):
            cp.start()
        token[...] = jnp.zeros_like(token)

    sem = pltpu.SemaphoreType.DMA((3 * n,))
    outs = pl.pallas_call(
        body, name=name,
        out_shape=[sem, sem] + [pltpu.HBM(b.shape, b.dtype) for b in bufs]
        + [pltpu.HBM(z.shape, z.dtype) for z in lands] + [jax.ShapeDtypeStruct((8, LANES), F32)],
        in_specs=[_HBM] * (2 * n) + ([] if after is None else [_ANY]),
        out_specs=[_SEM, _SEM] + [_HBM] * (2 * n) + [pl.BlockSpec(memory_space=pltpu.VMEM)],
        input_output_aliases={i: 2 + i for i in range(2 * n)},
        compiler_params=pltpu.CompilerParams(has_side_effects=_EFFECT))(
            *[_hbm(b) for b in bufs], *lands, *([] if after is None else [after]))
    return outs[0], outs[1], outs[2:2 + n], outs[2 + n:2 + 2 * n], outs[-1]


def _exchange_chips_wait(name, bufs, zones, send, recv, after):
    n = len(bufs)

    def body(*refs):
        for cp in _chip_copies(refs[:n], refs[n:2 * n], refs[2 * n], refs[2 * n + 1]):
            cp.wait_send()
            cp.wait_recv()

    outs = pl.pallas_call(
        body, name=name, out_shape=[pltpu.HBM(a.shape, a.dtype) for a in list(bufs) + list(zones)],
        in_specs=[_HBM] * (2 * n) + [_SEM, _SEM, _ANY], out_specs=[_HBM] * (2 * n),
        input_output_aliases={i: i for i in range(2 * n)},
        compiler_params=pltpu.CompilerParams(has_side_effects=_EFFECT))(*bufs, *zones, send, recv, after)
    return outs[n:]


def _all_gather_small(part, per_peer=False, dep=None):
    block = part.shape[1:] if per_peer else part.shape

    def body(src, *rest):
        out, send_sems, recv_sems, local_sem = rest[-4:]
        x, y, c, _ = _place()
        me = 4 * x + 2 * y + c
        mine = pltpu.make_async_copy(src.at[me] if per_peer else src, out.at[me], local_sem)
        mine.start()
        copies = []
        for r in range(1, N_DEV):
            dx, dy, dc = (r >> 2) & 1, (r >> 1) & 1, r & 1
            peer = (1 - x if dx else x, 1 - y if dy else y, 1 - c if dc else c)
            copies.append(pltpu.make_async_remote_copy(
                src_ref=src.at[4 * peer[0] + 2 * peer[1] + peer[2]] if per_peer else src, dst_ref=out.at[me],
                send_sem=send_sems.at[r - 1], recv_sem=recv_sems.at[r - 1],
                device_id=peer, device_id_type=MESH))
        for cp in copies:
            cp.start()
        for cp in copies:
            cp.wait()
        mine.wait()

    ins = [part] + ([] if dep is None else [dep])
    return pl.pallas_call(
        body, name="exchange_small" if per_peer else "all_gather_small",
        out_shape=jax.ShapeDtypeStruct((N_DEV,) + block, part.dtype),
        in_specs=[_ANY] * len(ins), out_specs=_ANY,
        scratch_shapes=[pltpu.SemaphoreType.DMA((N_DEV - 1,)), pltpu.SemaphoreType.DMA((N_DEV - 1,)),
                        pltpu.SemaphoreType.DMA])(*ins)


def _layer_fwd(xin, xin_bf, w_in, rest, P, alpha, dep=None):
    h = _proj_in(xin_bf, w_in, P["b_in"], dep=dep)
    o_pre, y_hg, st_all = _hgrn_fwd(h, P["lbs"], P["g_norm_w"])
    yc_pre, y_cv = _conv_fwd(h, P["w_dw"], P["b_dw"], P["conv_ln_g"], P["conv_ln_b"])
    W = rest(y_cv)
    y_h, y_c, merged, x1, x1_bf, z1 = _mixer_out(y_hg, y_cv, h, xin, W["w_a"], W["w_b"], P["b_b"], W["w_o"], alpha,
                                                 P["ln1_g"], P["ln1_b"])
    up, act = _ffn_up_swiglu(x1_bf, W["w_up"])
    x2, x2_bf, z2 = _ffn_down_ln2(act, W["w_down"], x1, alpha, P["ln2_g"], P["ln2_b"])
    saved = dict(xin_bf=xin_bf, h=h, o_pre=o_pre, y_hg=y_hg, st_all=st_all, yc_pre=yc_pre, y_cv=y_cv,
                 y_h=y_h, y_c=y_c, merged=merged, z1=z1, x1_bf=x1_bf, up=up, act=act, z2=z2)
    return x2, x2_bf, saved


def _layer_bwd(dx2, S, W, P, alpha, dep=None, early=None):
    dz2, dz2_bf, dup, dln2_g, dln2_b = _ln2_ffn_down_bwd(S["z2"], dx2, P["ln2_g"], W["w_down"], S["up"], dep=dep)
    dw_down = _mm_tn("ffn_down_dw", S["act"], dz2_bf, ACT_DTYPE)
    dx1 = _ffn_up_dx(dup, W["w_up"], dz2, alpha)
    dw_up = _ffn_up_dw(S["x1_bf"], dup)
    dz1, dz1_bf, dy_h, dy_c, dy_hg, dy_cv, dh, dln1_g, dln1_b, db_b = _mixer_out_bwd(
        S["z1"], dx1, S["y_h"], S["y_c"], S["h"], W["w_a"], W["w_b"], W["w_o"], P["ln1_g"])
    dw_abo = _branch_dw([(S["y_hg"], dy_h), (S["y_cv"], dy_c), (S["merged"], dz1_bf)], dz1.shape[1] // N_DEV)
    if early is not None:
        token = early(dict(w_abo=dw_abo, w_down=dw_down, w_up=dw_up))
        dy_cv = dy_cv + token[0, 0]
    dh, dw_dw, db_dw, dcln_g, dcln_b = _conv_bwd(S["h"], P["w_dw"], P["conv_ln_g"], P["conv_ln_b"],
                                                 S["yc_pre"], dy_cv, dh)
    dh, dlbs, dgw = _hgrn_bwd(S["h"], P["lbs"], P["g_norm_w"], S["o_pre"], S["st_all"], dy_hg, dh)
    dxin = _proj_in_dx(dh, W["w_in"], dz1, alpha)
    dw_in, db_in = _proj_in_dw(S["xin_bf"], dh)
    big = dict(w_in=dw_in, w_abo=dw_abo, w_down=dw_down, w_up=dw_up)
    small = dict(b_in=db_in, lbs=dlbs, g_norm_w=dgw, w_dw=dw_dw, b_dw=db_dw, conv_ln_g=dcln_g,
                 conv_ln_b=dcln_b, b_b=db_b, ln1_g=dln1_g, ln1_b=dln1_b, ln2_g=dln2_g, ln2_b=dln2_b)
    return dxin, big, small


_SMALL = ("b_in", "lb_logits", "g_norm_w", "b_dw", "conv_ln_g", "conv_ln_b", "b_b", "ln1_g", "ln1_b", "ln2_g",
          "ln2_b")


def _pack_small(per_layer, ln0_g, ln0_b, extra_row, D, L):
    rows = []
    for l in range(L):
        for n in _SMALL:
            a = per_layer[n][l]
            if n == "b_in":
                rows.append(a.reshape(N_SEC, D))
            elif n == "g_norm_w":
                rows.append(jnp.pad(a.reshape(1, -1), ((0, 0), (0, D - a.size))))
            else:
                rows.append(a.reshape(1, D))
    rows += [ln0_g.reshape(1, D), ln0_b.reshape(1, D), extra_row]
    buf = jnp.concatenate(rows, axis=0)
    pad = (-buf.shape[0]) % 8
    return jnp.pad(buf, ((0, pad), (0, 0)))


def _unpack_small(buf, D, L, hv):
    out = {n: [] for n in _SMALL}
    r = 0
    for l in range(L):
        for n in _SMALL:
            if n == "b_in":
                out[n].append(buf[r:r + N_SEC].reshape(N_SEC * D))
                r += N_SEC
            elif n == "g_norm_w":
                out[n].append(buf[r, :hv])
                r += 1
            else:
                out[n].append(buf[r])
                r += 1
    res = {n: jnp.stack(v) for n, v in out.items()}
    res["ln0_g"] = buf[r]
    res["ln0_b"] = buf[r + 1]
    return res, r + 2


def kernel(x, ln0_g, ln0_b, w_in, b_in, lb_logits, g_norm_w, w_a, w_dw, b_dw, conv_ln_g, conv_ln_b, w_b, b_b, w_o, ln1_g, ln1_b, w_up, w_down, ln2_g, ln2_b, loss_target, m_ln0_g, m_ln0_b, m_w_in, m_b_in, m_lb_logits, m_g_norm_w, m_w_a, m_w_dw, m_b_dw, m_conv_ln_g, m_conv_ln_b, m_w_b, m_b_b, m_w_o, m_ln1_g, m_ln1_b, m_w_up, m_w_down, m_ln2_g, m_ln2_b, v_ln0_g, v_ln0_b, v_w_in, v_b_in, v_lb_logits, v_g_norm_w, v_w_a, v_w_dw, v_b_dw, v_conv_ln_g, v_conv_ln_b, v_w_b, v_b_b, v_w_o, v_ln1_g, v_ln1_b, v_w_up, v_w_down, v_ln2_g, v_ln2_b):
    L, D = w_in.shape[0], w_in.shape[1]
    T = x.shape[0] * x.shape[1]
    Dn = w_in.shape[2]
    rs = w_a.shape[1]
    rd = w_down.shape[1]
    cu = w_up.shape[2]
    F = rd * N_DEV
    hv = g_norm_w.shape[1]
    alpha = (2 * L) ** 0.25
    my_x, my_y, my_c = lax.axis_index("x"), lax.axis_index("y"), lax.axis_index("c")
    dev_arr = jnp.reshape(4 * my_x + 2 * my_y + my_c, (1,)).astype(jnp.int32)

    o_a, o_b, o_o, o_d = D, D + rs, D + 2 * rs, D + 3 * rs
    taps = jnp.pad(w_dw, ((0, 0), (0, CONV_HALO - CONV_WIDTH), (0, 0))).reshape(L * CONV_HALO, w_dw.shape[2])
    taps_all = _all_gather_small(taps)
    w_dw_full = taps_all.transpose(1, 0, 2).reshape(L, CONV_HALO, D)

    started, gathered = {}, {}

    def start_gather(key, after):
        l, part = key
        rest = [w_a[l], w_b[l], w_o[l], w_down[l]]
        rows = dict(all=[w_in[l]] + rest, rest=rest)
        if part == "in":
            shards = [w_in[l].astype(ACT_DTYPE)]
        else:
            shards = [jnp.concatenate(rows[part], axis=0).astype(ACT_DTYPE),
                      jnp.swapaxes(w_up[l], 0, 1).astype(ACT_DTYPE)]
        started[key] = _gather_start("gather_start_%d_%s" % key, shards, [_place_own(s, dev_arr) for s in shards],
                                     after)
        return started[key][4]

    def finish_gather(key, after):
        send, recv, thru, zone, _ = started[key]
        _, zn = _gather_wait("gather_wait_%d_%s" % key, thru, zone, send, recv, after)
        gathered[key] = _gather_finish(zn)

    def w_in_of(l):
        return gathered[(l, "in") if l == 0 else (l, "all")][0]

    def rest_of(l):
        ga, gb = gathered[(l, "rest") if l == 0 else (l, "all")]
        base = 0 if l == 0 else D
        return dict(
            w_a=ga[:, base:base + rs, :].reshape(D, D),
            w_b=ga[:, base + rs:base + 2 * rs, :].reshape(D, D),
            w_o=ga[:, base + 2 * rs:base + 3 * rs, :].reshape(D, D),
            w_down=ga[:, base + 3 * rs:base + 3 * rs + rd, :].reshape(F, D),
            w_up=gb.reshape(2 * F, D))

    def weights(l):
        return dict(w_in=w_in_of(l), **rest_of(l))

    lbs = _lb_fwd(lb_logits)

    def params(l):
        return dict(b_in=b_in[l].reshape(N_SEC, 1, D), lbs=lbs[l].reshape(1, D), g_norm_w=g_norm_w[l].reshape(1, hv),
                    w_dw=w_dw_full[l], b_dw=b_dw[l].reshape(1, D), conv_ln_g=conv_ln_g[l].reshape(1, D),
                    conv_ln_b=conv_ln_b[l].reshape(1, D), b_b=b_b[l].reshape(1, D), ln1_g=ln1_g[l], ln1_b=ln1_b[l],
                    ln2_g=ln2_g[l], ln2_b=ln2_b[l])

    x2d = x.reshape(T, D)
    token = start_gather((0, "in"), taps_all)
    token = start_gather((0, "rest"), token)
    if L > 1:
        token = start_gather((1, "all"), token)
    xc, xc_bf = _ln_fwd("ln0", x2d, None, 1.0, ln0_g, ln0_b, dep=token)
    finish_gather((0, "in"), xc_bf)
    saved = []
    for l in range(L):
        if l == 0:
            def rest(after):
                finish_gather((0, "rest"), after)
                return rest_of(0)
            token = None
        else:
            rest = functools.partial(lambda after, l: rest_of(l), l=l)
            token = start_gather((l + 1, "all"), gathered[(l, "all")][0]) if l + 1 < L else None
        xc, xc_bf, s = _layer_fwd(xc, xc_bf, w_in_of(l), rest, params(l), alpha, dep=token)
        saved.append(s)
        if l + 1 < L:
            finish_gather((l + 1, "all"), xc_bf)

    c_arr = jnp.reshape(my_c, (1,)).astype(jnp.int32)
    chip = 2 * my_x + my_y
    dx, loss_row = _loss_fwd_bwd(xc, loss_target.reshape(T, D))
    small = [None] * L
    pending = None
    upd_big = {n: None for n in ("w_in", "w_a", "w_b", "w_o", "w_down", "w_up")}
    wmv = dict(w_in=(w_in, m_w_in, v_w_in), w_a=(w_a, m_w_a, v_w_a), w_b=(w_b, m_w_b, v_w_b),
               w_o=(w_o, m_w_o, v_w_o), w_down=(w_down, m_w_down, v_w_down),
               w_up=tuple(jnp.swapaxes(a, 1, 2) for a in (w_up, m_w_up, v_w_up)))

    held = dict(w_in=("w_in",), w_abo=("w_a", "w_b", "w_o"), w_down=("w_down",), w_up=("w_up",))

    def send_buffer(buf, g):
        return g if g.ndim == 3 else g.reshape(N_DEV, wmv[held[buf][0]][0].shape[1], D)

    def update_layer(l, bufs, q, r2):
        pre = jnp.stack([chip, jnp.int32(l)]).astype(jnp.int32)
        for k, buf in enumerate(bufs):
            for pos, name in enumerate(held[buf]):
                w, m, v = wmv[name]
                r, C = w.shape[1], w.shape[2]
                tr = max(t for t in range(16, 513, 16) if r % t == 0)
                nb = r // tr
                b0 = pos * nb
                specs = [pl.BlockSpec((None, tr, C), functools.partial(lambda i, s, b0: (s[0], b0 + i, 0), b0=b0))]
                specs += [pl.BlockSpec((None, tr, C), functools.partial(lambda i, s, j, b0: (j, b0 + i, 0), j=j, b0=b0))
                          for j in range(3)]
                upd_big[name] = _adamw(
                    "adamw_" + name, w.reshape(L * r, C), m.reshape(L * r, C), v.reshape(L * r, C),
                    [q[k], r2[k], r2[k], r2[k]], specs, tr, prefetch=pre, nsteps=nb,
                    row_map=functools.partial(lambda i, s, nb: (s[1] * nb + i, 0), nb=nb), prev=upd_big[name])

    def pair_sums(bufs, grads):
        sends = [send_buffer(n, grads[n]) for n in bufs]
        return [_pair_add(p, r, c_arr) for p, r in zip(sends, _exchange_sibling(sends))]

    def start_reduce(tag, names, qs, after=None):
        s_send, s_recv, q_thru, zones, token = _exchange_chips_start("reduce_start_" + tag, qs, after)
        return (tag, names, (s_send, s_recv), list(q_thru), list(zones)), token

    def finish_reduce(l, handle, after):
        tag, names, sems, q, zones = handle
        r2 = _exchange_chips_wait("reduce_wait_" + tag, q, zones, sems[0], sems[1], after)
        update_layer(l, names, q, r2)

    names_all = tuple(held)
    names_early = names_all[1:]
    token = None
    first_early = []
    for l in range(L - 1, -1, -1):
        if l > 0:
            dx, big, small[l] = _layer_bwd(dx, saved[l], weights(l), params(l), alpha, dep=token)
            if pending is not None:
                finish_reduce(l + 1, pending, dx)
            pending, token = start_reduce(str(l), names_all, pair_sums(names_all, big))
        else:
            def early(grads):
                if pending is not None:
                    finish_reduce(1, pending, grads["w_abo"])
                handle, tok = start_reduce("0_rest", names_early, pair_sums(names_early, grads))
                first_early.append(handle)
                return tok
            dx, big, small[l] = _layer_bwd(dx, saved[l], weights(l), params(l), alpha, dep=token, early=early)
            q_in = pair_sums(("w_in",), big)
    dx0, _, dln0_g, dln0_b = _ln_bwd("ln0_bwd", x2d, dx, ln0_g)
    dlb_logits = _lb_bwd(lb_logits, jnp.concatenate([small[l]["lbs"] for l in range(L)], axis=0))

    small_l = {n: [small[l][n] for l in range(L)] for n in _SMALL if n != "lb_logits"}
    small_l["lb_logits"] = [dlb_logits[l] for l in range(L)]
    loss_pad = jnp.pad(loss_row, ((0, 0), (0, D - LANES)))
    cw = w_dw.shape[2]
    taps_send = jnp.concatenate([small[l]["w_dw"] for l in range(L)], axis=0)
    taps_send = taps_send.reshape(L * CONV_HALO, N_DEV, cw).transpose(1, 0, 2)
    tap_parts = _all_gather_small(taps_send, per_peer=True)
    part = _pack_small(small_l, dln0_g, dln0_b, loss_pad, D, L)
    parts_all = _all_gather_small(part, dep=tap_parts)
    n_small = part.shape[0]

    last, _ = start_reduce("0_in", ("w_in",), q_in, after=parts_all)


    inputs = dict(b_in=(b_in, m_b_in, v_b_in), lb_logits=(lb_logits, m_lb_logits, v_lb_logits),
                  g_norm_w=(g_norm_w, m_g_norm_w, v_g_norm_w), b_dw=(b_dw, m_b_dw, v_b_dw),
                  conv_ln_g=(conv_ln_g, m_conv_ln_g, v_conv_ln_g), conv_ln_b=(conv_ln_b, m_conv_ln_b, v_conv_ln_b),
                  b_b=(b_b, m_b_b, v_b_b), ln1_g=(ln1_g, m_ln1_g, v_ln1_g), ln1_b=(ln1_b, m_ln1_b, v_ln1_b),
                  ln2_g=(ln2_g, m_ln2_g, v_ln2_g), ln2_b=(ln2_b, m_ln2_b, v_ln2_b))
    zero_row = jnp.zeros((1, D), F32)
    packed = [_pack_small({n: [inputs[n][i][l] for l in range(L)] for n in _SMALL},
                          (ln0_g, m_ln0_g, v_ln0_g)[i], (ln0_b, m_ln0_b, v_ln0_b)[i], zero_row, D, L)
              for i in range(3)]
    small_specs = [pl.BlockSpec((None, n_small, D), functools.partial(lambda i, d: (d, 0, 0), d=d))
                   for d in range(N_DEV)]
    s_out = _adamw("adamw_small", packed[0], packed[1], packed[2], [parts_all] * N_DEV, small_specs, n_small)
    s_g, n_rows = _unpack_small(s_out[0], D, L, hv)
    s_d, _ = _unpack_small(s_out[1], D, L, hv)
    s_m, _ = _unpack_small(s_out[2], D, L, hv)
    s_v, _ = _unpack_small(s_out[3], D, L, hv)
    loss = s_out[0][n_rows, 0]

    tap_specs = [pl.BlockSpec((None, L * CONV_HALO, cw), functools.partial(lambda i, d: (d, 0, 0), d=d))
                 for d in range(N_DEV)]
    pad_t = lambda a: jnp.pad(a, ((0, 0), (0, CONV_HALO - CONV_WIDTH), (0, 0))).reshape(L * CONV_HALO, cw)
    t_out = _adamw("adamw_taps", pad_t(w_dw), pad_t(m_w_dw), pad_t(v_w_dw), [tap_parts] * N_DEV, tap_specs,
                   L * CONV_HALO)
    finish_reduce(0, first_early[0], t_out[0])
    finish_reduce(0, last, upd_big["w_up"][0])
    upd ={n: [o.reshape(wmv[n][0].shape) for o in outs] for n, outs in upd_big.items()}
    upd["w_up"] = [jnp.swapaxes(o, 1, 2) for o in upd["w_up"]]
    upd["w_dw"] = [o.reshape(L, CONV_HALO, cw)[:, :CONV_WIDTH, :] for o in t_out]

    order = ["ln0_g", "ln0_b", "w_in", "b_in", "lb_logits", "g_norm_w", "w_a", "w_dw", "b_dw", "conv_ln_g",
             "conv_ln_b", "w_b", "b_b", "w_o", "ln1_g", "ln1_b", "w_up", "w_down", "ln2_g", "ln2_b"]
    small_sets = (s_g, s_d, s_m, s_v)
    outs = [loss, dx0.reshape(x.shape)]
    for i in range(4):
        for n in order:
            outs.append(upd[n][i] if n in upd else small_sets[i][n])
    return tuple(outs)
```

```python
import functools

import jax
import jax.numpy as jnp
from jax import lax
from jax.experimental import pallas as pl
from jax.experimental.pallas import tpu as pltpu

F32 = jnp.float32
MXU_DTYPE = jnp.bfloat16
ACT_DTYPE = jnp.bfloat16

LANES = 128
SUB = 8
N_DEV = 8
N_SEC = 8
CONV_WIDTH = 31
CONV_HALO = 32
HG_C = 16
LN_EPS = 1e-5
RMS_EPS = 1e-6
F_MIN = 1e-30
LOG2E = 1.4426950408889634
ADAM_LR = 0.001
ADAM_B1 = 0.9
ADAM_B2 = 0.999
ADAM_EPS = 1e-08
ADAM_WD = 0.01
ADAM_STEP = 10
VMEM_LIMIT = 56 * 1024 * 1024
MESH = pl.DeviceIdType.MESH

_NN = (((1,), (0,)), ((), ()))
_NT = (((1,), (1,)), ((), ()))
_TN = (((0,), (0,)), ((), ()))


_ANY = pl.BlockSpec(memory_space=pl.ANY)
_HBM = pl.BlockSpec(memory_space=pltpu.HBM)
_SEM = pl.BlockSpec(memory_space=pltpu.SEMAPHORE)
_EFFECT = pltpu.SideEffectType.DATAFLOW_SIDE_EFFECTING


def _cp(*sem):
    return pltpu.CompilerParams(dimension_semantics=tuple(sem), vmem_limit_bytes=VMEM_LIMIT)


def _pick(n, cands):
    for c in cands:
        if c <= n and n % c == 0:
            return c
    return n


def _silu(x):
    return x * jax.nn.sigmoid(x)


def _dsilu(x):
    s = jax.nn.sigmoid(x)
    return s * (1.0 + x * (1.0 - s))


def _matmul(name, a, b, *, dims, grid, a_spec, b_spec, out_shape, out_spec, acc_shape, nk,
            bias=None, bias_spec=None, add=None, add_spec=None, add_scale=1.0, dep=None):
    has_bias, has_add = bias is not None, add is not None
    kaxis = len(grid) - 1

    def body(*refs):
        a_ref, b_ref = refs[0], refs[1]
        pos = 2
        bias_ref = add_ref = None
        if has_bias:
            bias_ref = refs[pos]
            pos += 1
        if has_add:
            add_ref = refs[pos]
            pos += 1
        if dep is not None:
            pos += 1
        o_ref = refs[pos]
        acc_ref = refs[pos + 1] if nk > 1 else None

        part = lax.dot_general(a_ref[...].astype(MXU_DTYPE), b_ref[...].astype(MXU_DTYPE), dims,
                               preferred_element_type=F32)

        def finish(r):
            if has_bias:
                r = r + bias_ref[...]
            if has_add:
                r = r + add_scale * add_ref[...]
            o_ref[...] = r.astype(o_ref.dtype)

        if nk == 1:
            finish(part)
        else:
            k = pl.program_id(kaxis)

            @pl.when(k == 0)
            def _():
                acc_ref[...] = part

            @pl.when(k > 0)
            def _():
                acc_ref[...] += part

            @pl.when(k == nk - 1)
            def _():
                finish(acc_ref[...])

    ins, specs = [a, b], [a_spec, b_spec]
    if has_bias:
        ins.append(bias)
        specs.append(bias_spec)
    if has_add:
        ins.append(add)
        specs.append(add_spec)
    if dep is not None:
        ins.append(dep)
        specs.append(_ANY)
    sem =("parallel",) * (len(grid) - 1) + ("arbitrary",) if nk > 1 else ("parallel",) * len(grid)
    return pl.pallas_call(
        body, name=name, grid=grid, in_specs=specs, out_specs=out_spec, out_shape=out_shape,
        scratch_shapes=[pltpu.VMEM(acc_shape, F32)] if nk > 1 else [],
        compiler_params=_cp(*sem))(*ins)


def _mm_tn(name, a, b, out_dtype):
    K, M = a.shape
    N = b.shape[1]
    tm = _pick(M, (256, 128))
    return _matmul(
        name, a, b, dims=_TN, grid=(M // tm,),
        a_spec=pl.BlockSpec((K, tm), lambda i: (0, i)),
        b_spec=pl.BlockSpec((K, N), lambda i: (0, 0)),
        out_shape=jax.ShapeDtypeStruct((M, N), out_dtype),
        out_spec=pl.BlockSpec((tm, N), lambda i: (i, 0)),
        acc_shape=(tm, N), nk=1)


def _branch_dw(pairs, rs):
    T, D = pairs[0][0].shape
    nslot = max(1, LANES // rs)
    tm = nslot * rs
    nk = len(pairs)

    def body(*refs):
        o_ref = refs[-1]
        k = pl.program_id(0)
        for kk in range(nk):
            @pl.when(k == kk)
            def _():
                r = lax.dot_general(refs[2 * kk][...].astype(MXU_DTYPE), refs[2 * kk + 1][...].astype(MXU_DTYPE),
                                    _TN, preferred_element_type=F32)
                o_ref[...] = r.astype(o_ref.dtype).reshape(nslot, rs, D)

    in_specs, ins = [], []
    for kk, (a, b) in enumerate(pairs):
        in_specs.append(pl.BlockSpec((T, tm), functools.partial(lambda k, i, kk: (0, jnp.where(k == kk, i, 0)), kk=kk)))
        in_specs.append(pl.BlockSpec((T, D), lambda k, i: (0, 0)))
        ins += [a, b]
    return pl.pallas_call(
        body, name="branch_dw", grid=(nk, N_DEV // nslot), in_specs=in_specs,
        out_specs=pl.BlockSpec((nslot, rs, D), lambda k, i: (i, k, 0)),
        out_shape=jax.ShapeDtypeStruct((N_DEV, nk * rs, D), ACT_DTYPE),
        compiler_params=_cp("arbitrary", "arbitrary"))(*ins)


def _proj_in(x_bf, w_in, b_in, dep=None):
    T, D = x_bf.shape
    tn = _pick(D, (512, 256, 128))
    return _matmul(
        "proj_in", x_bf, w_in, dims=_NN, grid=(N_SEC, D // tn),
        a_spec=pl.BlockSpec((T, D), lambda s, j: (0, 0)),
        b_spec=pl.BlockSpec((None, D, tn), lambda s, j: (s, 0, j)),
        out_shape=jax.ShapeDtypeStruct((N_SEC, T, D), F32),
        out_spec=pl.BlockSpec((None, T, tn), lambda s, j: (s, 0, j)),
        acc_shape=(T, tn), nk=1,
        bias=b_in, bias_spec=pl.BlockSpec((None, 1, tn), lambda s, j: (s, 0, j)), dep=dep)


def _proj_in_dx(dh, w_in, add, add_scale):
    _, T, D = dh.shape
    tm = _pick(T, (256, 128, 64, 32, 16))

    def body(dh_ref, w_ref, add_ref, o_ref):
        acc = add_scale * add_ref[...]
        for s in range(N_SEC):
            acc = acc + lax.dot_general(dh_ref[s].astype(MXU_DTYPE), w_ref[s].astype(MXU_DTYPE), _NT,
                                        preferred_element_type=F32)
        o_ref[...] = acc

    row = pl.BlockSpec((tm, D), lambda i: (i, 0))
    return pl.pallas_call(
        body, name="proj_in_dx", grid=(T // tm,),
        in_specs=[pl.BlockSpec((N_SEC, tm, D), lambda i: (0, i, 0)),
                  pl.BlockSpec((N_SEC, D, D), lambda i: (0, 0, 0), pipeline_mode=pl.Buffered(1)), row],
        out_specs=row, out_shape=jax.ShapeDtypeStruct((T, D), F32),
        compiler_params=_cp("parallel"))(dh, w_in, add)


def _proj_in_dw(x_bf, dh):
    _, T, D = dh.shape
    tn = _pick(D, (512, 256, 128))

    def body(x_ref, dh_ref, dw_ref, db_ref):
        dhv = dh_ref[...]
        dw_ref[...] = lax.dot_general(x_ref[...].astype(MXU_DTYPE), dhv.astype(MXU_DTYPE), _TN,
                                      preferred_element_type=F32).astype(dw_ref.dtype)
        db_ref[...] = jnp.sum(dhv.astype(F32), axis=0, keepdims=True)

    return pl.pallas_call(
        body, name="proj_in_dw", grid=(N_SEC, D // tn),
        in_specs=[pl.BlockSpec((T, D), lambda s, j: (0, 0)), pl.BlockSpec((None, T, tn), lambda s, j: (s, 0, j))],
        out_specs=[pl.BlockSpec((None, D, tn), lambda s, j: (s, 0, j)),
                   pl.BlockSpec((None, 1, tn), lambda s, j: (s, 0, j))],
        out_shape=[jax.ShapeDtypeStruct((N_SEC, D, D), ACT_DTYPE), jax.ShapeDtypeStruct((N_SEC, 1, D), F32)],
        compiler_params=_cp("parallel", "parallel"))(x_bf, dh)


def _ffn_up_dx(dup, w_up_t, add, add_scale):
    _, T, F = dup.shape
    D = w_up_t.shape[1]
    tm = _pick(T, (256, 128, 64, 32, 16))

    def body(dup_ref, w_ref, add_ref, o_ref):
        acc = add_scale * add_ref[...]
        for p in range(2):
            acc = acc + jnp.dot(dup_ref[p].astype(MXU_DTYPE), w_ref[pl.ds(p * F, F), :].astype(MXU_DTYPE),
                                preferred_element_type=F32)
        o_ref[...] = acc

    row = pl.BlockSpec((tm, D), lambda i: (i, 0))
    return pl.pallas_call(
        body, name="ffn_up_dx", grid=(T // tm,),
        in_specs=[pl.BlockSpec((2, tm, F), lambda i: (0, i, 0)),
                  pl.BlockSpec((2 * F, D), lambda i: (0, 0), pipeline_mode=pl.Buffered(1)), row],
        out_specs=row, out_shape=jax.ShapeDtypeStruct((T, D), F32),
        compiler_params=_cp("parallel"))(dup, w_up_t, add)


def _ffn_up_dw(x_bf, dup):
    _, T, F = dup.shape
    D = x_bf.shape[1]
    tm = _pick(F, (1408, 256, 128))
    nb = F // tm
    return _matmul(
        "ffn_up_dw", dup, x_bf, dims=_TN, grid=(2, nb),
        a_spec=pl.BlockSpec((None, T, tm), lambda p, j: (p, 0, j)),
        b_spec=pl.BlockSpec((T, D), lambda p, j: (0, 0)),
        out_shape=jax.ShapeDtypeStruct((2 * F, D), ACT_DTYPE),
        out_spec=pl.BlockSpec((tm, D), lambda p, j: (p * nb + j, 0)),
        acc_shape=(tm, D), nk=1)


def _ln_fwd(name, a, res, alpha, g, b, dep=None):
    T, D = a.shape
    tr = _pick(T, (256, 128, 64, 32, 16))
    has_res = res is not None

    def body(*refs):
        if has_res:
            a_ref, r_ref, g_ref, b_ref = refs[:4]
            y_ref, yb_ref, z_ref = refs[-3:]
            z = alpha * a_ref[...] + r_ref[...]
            z_ref[...] = z
        else:
            a_ref, g_ref, b_ref = refs[:3]
            y_ref, yb_ref = refs[-2:]
            z = a_ref[...]
        mu = jnp.mean(z, axis=-1, keepdims=True)
        zc = z - mu
        var = jnp.mean(zc * zc, axis=-1, keepdims=True)
        y = zc * lax.rsqrt(var + LN_EPS) * g_ref[...] + b_ref[...]
        y_ref[...] = y
        yb_ref[...] = y.astype(ACT_DTYPE)

    row = pl.BlockSpec((tr, D), lambda i: (i, 0))
    vec = pl.BlockSpec((1, D), lambda i: (0, 0))
    ins = [a] + ([res] if has_res else []) + [g.reshape(1, D), b.reshape(1, D)]
    in_specs = [row] + ([row] if has_res else []) + [vec, vec]
    if dep is not None:
        ins.append(dep)
        in_specs.append(_ANY)
    out_shape = [jax.ShapeDtypeStruct((T, D), F32), jax.ShapeDtypeStruct((T, D), ACT_DTYPE)]
    if has_res:
        out_shape.append(jax.ShapeDtypeStruct((T, D), F32))
    return pl.pallas_call(
        body, name=name, grid=(T // tr,), in_specs=in_specs,
        out_specs=[row] * len(out_shape), out_shape=out_shape, compiler_params=_cp("parallel"))(*ins)


def _ln_bwd(name, z, dy, g, dep=None):
    T, D = z.shape
    tr = _pick(T, (256, 128, 64, 32, 16))

    def body(z_ref, dy_ref, g_ref, *rest):
        dz_ref, dzb_ref, dg_ref, db_ref = rest[-4:]

        @pl.when(pl.program_id(0) == 0)
        def _():
            dg_ref[...] = jnp.zeros_like(dg_ref)
            db_ref[...] = jnp.zeros_like(db_ref)

        zv = z_ref[...]
        dy_ = dy_ref[...]
        mu = jnp.mean(zv, axis=-1, keepdims=True)
        zc = zv - mu
        rstd = lax.rsqrt(jnp.mean(zc * zc, axis=-1, keepdims=True) + LN_EPS)
        xhat = zc * rstd
        dxh = dy_ * g_ref[...]
        dz = rstd * (dxh - jnp.mean(dxh, axis=-1, keepdims=True)
                     - xhat * jnp.mean(dxh * xhat, axis=-1, keepdims=True))
        dz_ref[...] = dz
        dzb_ref[...] = dz.astype(ACT_DTYPE)
        dg_ref[...] += jnp.sum(dy_ * xhat, axis=0, keepdims=True)
        db_ref[...] += jnp.sum(dy_, axis=0, keepdims=True)

    row = pl.BlockSpec((tr, D), lambda i: (i, 0))
    vec = pl.BlockSpec((1, D), lambda i: (0, 0))
    ins, in_specs = [z, dy, g.reshape(1, D)], [row, row, vec]
    if dep is not None:
        ins.append(dep)
        in_specs.append(_ANY)
    return pl.pallas_call(
        body, name=name, grid=(T // tr,), in_specs=in_specs, out_specs=[row, row, vec, vec],
        out_shape=[jax.ShapeDtypeStruct((T, D), F32), jax.ShapeDtypeStruct((T, D), ACT_DTYPE),
                   jax.ShapeDtypeStruct((1, D), F32), jax.ShapeDtypeStruct((1, D), F32)],
        compiler_params=_cp("arbitrary"))(*ins)


def _loss_fwd_bwd(y, target):
    T, D = y.shape
    tr = _pick(T, (256, 128, 64, 32, 16))

    def body(y_ref, t_ref, dy_ref, l_ref):
        @pl.when(pl.program_id(0) == 0)
        def _():
            l_ref[...] = jnp.zeros_like(l_ref)

        e = y_ref[...] - t_ref[...]
        dy_ref[...] = e * (1.0 / D)
        row = jnp.sum(e * e, axis=-1, keepdims=True) * (1.0 / D)
        l_ref[...] += 0.5 * jnp.sum(row, axis=0, keepdims=True)

    rowspec = pl.BlockSpec((tr, D), lambda i: (i, 0))
    return pl.pallas_call(
        body, name="loss", grid=(T // tr,), in_specs=[rowspec, rowspec],
        out_specs=[rowspec, pl.BlockSpec((1, LANES), lambda i: (0, 0))],
        out_shape=[jax.ShapeDtypeStruct((T, D), F32), jax.ShapeDtypeStruct((1, LANES), F32)],
        compiler_params=_cp("arbitrary"))(y, target)


def _layer_norm_rows(z, g, b):
    mu = jnp.mean(z, axis=-1, keepdims=True)
    zc = z - mu
    var = jnp.mean(zc * zc, axis=-1, keepdims=True)
    return zc * lax.rsqrt(var + LN_EPS) * g + b


def _mixer_out(y_hg, y_cv, h, xin, w_a, w_b, b_b, w_o, alpha, ln_g, ln_b):
    T, D = xin.shape
    tm = _pick(T, (256, 128, 64, 32, 16))

    def body(yhg_ref, ycv_ref, gh_ref, gc_ref, x_ref, wa_ref, wb_ref, bb_ref, wo_ref, g_ref, b_ref,
             yh_ref, yc_ref, m_ref, x1_ref, x1b_ref, z_ref):
        y_h = jnp.dot(yhg_ref[...].astype(MXU_DTYPE), wa_ref[...].astype(MXU_DTYPE), preferred_element_type=F32)
        y_c = jnp.dot(ycv_ref[...].astype(MXU_DTYPE), wb_ref[...].astype(MXU_DTYPE),
                      preferred_element_type=F32) + bb_ref[...]
        yh_ref[...] = y_h
        yc_ref[...] = y_c
        merged = (jax.nn.sigmoid(gh_ref[...]) * y_h + jax.nn.sigmoid(gc_ref[...]) * y_c).astype(ACT_DTYPE)
        m_ref[...] = merged
        z = alpha * x_ref[...] + jnp.dot(merged.astype(MXU_DTYPE), wo_ref[...].astype(MXU_DTYPE),
                                         preferred_element_type=F32)
        z_ref[...] = z
        x1 = _layer_norm_rows(z, g_ref[...], b_ref[...])
        x1_ref[...] = x1
        x1b_ref[...] = x1.astype(ACT_DTYPE)

    row = pl.BlockSpec((tm, D), lambda i: (i, 0))
    mat = pl.BlockSpec((D, D), lambda i: (0, 0))
    vec = pl.BlockSpec((1, D), lambda i: (0, 0))
    f32, act = jax.ShapeDtypeStruct((T, D), F32), jax.ShapeDtypeStruct((T, D), ACT_DTYPE)
    return pl.pallas_call(
        body, name="mixer_out", grid=(T // tm,),
        in_specs=[row, row, pl.BlockSpec((None, tm, D), lambda i: (6, i, 0)),
                  pl.BlockSpec((None, tm, D), lambda i: (7, i, 0)), row, mat, mat, vec, mat, vec, vec],
        out_specs=[row] * 6, out_shape=[f32, f32, act, f32, act, f32],
        compiler_params=_cp("parallel"))(y_hg, y_cv, h, h, xin, w_a, w_b, b_b, w_o, ln_g.reshape(1, D),
                                         ln_b.reshape(1, D))


def _ffn_up_swiglu(x_bf, w_up_t):
    T, D = x_bf.shape
    F = w_up_t.shape[0] // 2
    tn = _pick(F, (256, 128))
    nb = F // tn

    def body(x_ref, wg_ref, wv_ref, up_ref, act_ref):
        xv = x_ref[...].astype(MXU_DTYPE)
        g = lax.dot_general(xv, wg_ref[...].astype(MXU_DTYPE), _NT, preferred_element_type=F32)
        v = lax.dot_general(xv, wv_ref[...].astype(MXU_DTYPE), _NT, preferred_element_type=F32)
        up_ref[0] = g
        up_ref[1] = v
        act_ref[...] = (_silu(g) * v).astype(ACT_DTYPE)

    return pl.pallas_call(
        body, name="ffn_up", grid=(nb,),
        in_specs=[pl.BlockSpec((T, D), lambda j: (0, 0)), pl.BlockSpec((tn, D), lambda j: (j, 0)),
                  pl.BlockSpec((tn, D), lambda j: (nb + j, 0))],
        out_specs=[pl.BlockSpec((2, T, tn), lambda j: (0, 0, j)), pl.BlockSpec((T, tn), lambda j: (0, j))],
        out_shape=[jax.ShapeDtypeStruct((2, T, F), F32), jax.ShapeDtypeStruct((T, F), ACT_DTYPE)],
        compiler_params=_cp("parallel"))(x_bf, w_up_t, w_up_t)


def _ffn_down_ln2(act, w_down, x1, alpha, ln_g, ln_b):
    T, D = x1.shape
    F = act.shape[1]
    tm = _pick(T, (256, 128, 64, 32, 16))

    def body(a_ref, w_ref, x_ref, g_ref, b_ref, x2_ref, x2b_ref, z_ref):
        z = alpha * x_ref[...] + jnp.dot(a_ref[...].astype(MXU_DTYPE), w_ref[...].astype(MXU_DTYPE),
                                         preferred_element_type=F32)
        z_ref[...] = z
        x2 = _layer_norm_rows(z, g_ref[...], b_ref[...])
        x2_ref[...] = x2
        x2b_ref[...] = x2.astype(ACT_DTYPE)

    row = pl.BlockSpec((tm, D), lambda i: (i, 0))
    vec = pl.BlockSpec((1, D), lambda i: (0, 0))
    f32, actt = jax.ShapeDtypeStruct((T, D), F32), jax.ShapeDtypeStruct((T, D), ACT_DTYPE)
    return pl.pallas_call(
        body, name="ffn_down", grid=(T // tm,),
        in_specs=[pl.BlockSpec((tm, F), lambda i: (i, 0)), pl.BlockSpec((F, D), lambda i: (0, 0)), row, vec, vec],
        out_specs=[row] * 3, out_shape=[f32, actt, f32],
        compiler_params=_cp("parallel"))(act, w_down, x1, ln_g.reshape(1, D), ln_b.reshape(1, D))


def _ln2_ffn_down_bwd(z, dy, ln_g, w_down, up, dep=None):
    T, D = z.shape
    F = w_down.shape[0]
    tm = _pick(T, (256, 128, 64, 32, 16))

    def body(z_ref, dy_ref, g_ref, w_ref, up_ref, *rest):
        dz_ref, dzb_ref, dup_ref, dg_ref, db_ref = rest[-5:]

        @pl.when(pl.program_id(0) == 0)
        def _():
            dg_ref[...] = jnp.zeros_like(dg_ref)
            db_ref[...] = jnp.zeros_like(db_ref)

        zv = z_ref[...]
        dy_ = dy_ref[...]
        mu = jnp.mean(zv, axis=-1, keepdims=True)
        zc = zv - mu
        rstd = lax.rsqrt(jnp.mean(zc * zc, axis=-1, keepdims=True) + LN_EPS)
        xhat = zc * rstd
        dxh = dy_ * g_ref[...]
        dz = rstd * (dxh - jnp.mean(dxh, axis=-1, keepdims=True)
                     - xhat * jnp.mean(dxh * xhat, axis=-1, keepdims=True))
        dz_ref[...] = dz
        dzb = dz.astype(ACT_DTYPE)
        dzb_ref[...] = dzb
        dg_ref[...] += jnp.sum(dy_ * xhat, axis=0, keepdims=True)
        db_ref[...] += jnp.sum(dy_, axis=0, keepdims=True)
        da = lax.dot_general(dzb.astype(MXU_DTYPE), w_ref[...].astype(MXU_DTYPE), _NT, preferred_element_type=F32)
        ug = up_ref[0]
        dup_ref[0] = (da * up_ref[1] * _dsilu(ug)).astype(ACT_DTYPE)
        dup_ref[1] = (da * _silu(ug)).astype(ACT_DTYPE)

    row = pl.BlockSpec((tm, D), lambda i: (i, 0))
    vec = pl.BlockSpec((1, D), lambda i: (0, 0))
    blk = pl.BlockSpec((2, tm, F), lambda i: (0, i, 0))
    ins = [z, dy, ln_g.reshape(1, D), w_down, up]
    in_specs = [row, row, vec, pl.BlockSpec((F, D), lambda i: (0, 0)), blk]
    if dep is not None:
        ins.append(dep)
        in_specs.append(_ANY)
    v32 = jax.ShapeDtypeStruct((1, D), F32)
    return pl.pallas_call(
        body, name="ln2_ffn_down_bwd", grid=(T // tm,), in_specs=in_specs,
        out_specs=[row, row, blk, vec, vec],
        out_shape=[jax.ShapeDtypeStruct((T, D), F32), jax.ShapeDtypeStruct((T, D), ACT_DTYPE),
                   jax.ShapeDtypeStruct((2, T, F), ACT_DTYPE), v32, v32],
        compiler_params=_cp("arbitrary"))(*ins)


def _mixer_out_bwd(z, dx1, y_h, y_c, h, w_a, w_b, w_o, ln_g):
    T, D = z.shape
    tm = _pick(T, (256, 128, 64, 32, 16))

    def body(z_ref, dx_ref, yh_ref, yc_ref, gh_ref, gc_ref, wa_ref, wb_ref, wo_ref, g_ref,
             dz_ref, dzb_ref, dyh_ref, dyc_ref, dyhg_ref, dycv_ref, dh_ref, dg_ref, db_ref, dbb_ref):
        @pl.when(pl.program_id(0) == 0)
        def _():
            dg_ref[...] = jnp.zeros_like(dg_ref)
            db_ref[...] = jnp.zeros_like(db_ref)
            dbb_ref[...] = jnp.zeros_like(dbb_ref)

        zv = z_ref[...]
        dy_ = dx_ref[...]
        mu = jnp.mean(zv, axis=-1, keepdims=True)
        zc = zv - mu
        rstd = lax.rsqrt(jnp.mean(zc * zc, axis=-1, keepdims=True) + LN_EPS)
        xhat = zc * rstd
        dxh = dy_ * g_ref[...]
        dz = rstd * (dxh - jnp.mean(dxh, axis=-1, keepdims=True)
                     - xhat * jnp.mean(dxh * xhat, axis=-1, keepdims=True))
        dz_ref[...] = dz
        dzb = dz.astype(ACT_DTYPE)
        dzb_ref[...] = dzb
        dg_ref[...] += jnp.sum(dy_ * xhat, axis=0, keepdims=True)
        db_ref[...] += jnp.sum(dy_, axis=0, keepdims=True)
        dm_ = lax.dot_general(dzb.astype(MXU_DTYPE), wo_ref[...].astype(MXU_DTYPE), _NT, preferred_element_type=F32)
        sh = jax.nn.sigmoid(gh_ref[...])
        sc = jax.nn.sigmoid(gc_ref[...])
        dyc = dm_ * sc
        dyh_b = (dm_ * sh).astype(ACT_DTYPE)
        dyc_b = dyc.astype(ACT_DTYPE)
        dyh_ref[...] = dyh_b
        dyc_ref[...] = dyc_b
        dbb_ref[...] += jnp.sum(dyc, axis=0, keepdims=True)
        dh_ref[0] = (dm_ * yh_ref[...] * sh * (1.0 - sh)).astype(ACT_DTYPE)
        dh_ref[1] = (dm_ * yc_ref[...] * sc * (1.0 - sc)).astype(ACT_DTYPE)
        dyhg_ref[...] = lax.dot_general(dyh_b.astype(MXU_DTYPE), wa_ref[...].astype(MXU_DTYPE), _NT,
                                        preferred_element_type=F32)
        dycv_ref[...] = lax.dot_general(dyc_b.astype(MXU_DTYPE), wb_ref[...].astype(MXU_DTYPE), _NT,
                                        preferred_element_type=F32)

    row = pl.BlockSpec((tm, D), lambda i: (i, 0))
    mat = pl.BlockSpec((D, D), lambda i: (0, 0))
    vec = pl.BlockSpec((1, D), lambda i: (0, 0))
    f32, act = jax.ShapeDtypeStruct((T, D), F32), jax.ShapeDtypeStruct((T, D), ACT_DTYPE)
    v32 = jax.ShapeDtypeStruct((1, D), F32)
    return pl.pallas_call(
        body, name="mixer_out_bwd", grid=(T // tm,),
        in_specs=[row, row, row, row, pl.BlockSpec((None, tm, D), lambda i: (6, i, 0)),
                  pl.BlockSpec((None, tm, D), lambda i: (7, i, 0)), mat, mat, mat, vec],
        out_specs=[row, row, row, row, row, row, pl.BlockSpec((2, tm, D), lambda i: (3, i, 0)), vec, vec, vec],
        out_shape=[f32, act, act, act, f32, f32, jax.ShapeDtypeStruct((N_SEC, T, D), ACT_DTYPE), v32, v32, v32],
        compiler_params=_cp("arbitrary"))(z, dx1, y_h, y_c, h, h, w_a, w_b, w_o, ln_g.reshape(1, D))


def _lb_softmax(x):
    L = x.shape[0]
    rows = [x[l:l + 1] for l in range(L)]
    m = rows[0]
    for r in rows[1:]:
        m = jnp.maximum(m, r)
    e = [jnp.exp(r - m) for r in rows]
    s = e[0]
    for r in e[1:]:
        s = s + r
    return [r / s for r in e]


def _lb_fwd(lb_logits):
    L, D = lb_logits.shape

    def body(x_ref, o_ref):
        p = _lb_softmax(x_ref[...])
        run = jnp.zeros_like(p[0])
        for l in range(L):
            if l > 0:
                run = run + p[l]
            o_ref[pl.ds(l, 1), :] = run

    return pl.pallas_call(body, name="lb_fwd", out_shape=jax.ShapeDtypeStruct((L, D), F32))(lb_logits)


def _lb_bwd(lb_logits, dlbs):
    L, D = lb_logits.shape

    def body(x_ref, d_ref, o_ref):
        p = _lb_softmax(x_ref[...])
        d = d_ref[...]
        dp = [jnp.zeros_like(p[0]) for _ in range(L)]
        run = jnp.zeros_like(p[0])
        for j in range(L - 1, 0, -1):
            run = run + d[j:j + 1]
            dp[j] = run
        dot = dp[0] * p[0]
        for j in range(1, L):
            dot = dot + dp[j] * p[j]
        for j in range(L):
            o_ref[pl.ds(j, 1), :] = p[j] * (dp[j] - dot)

    return pl.pallas_call(body, name="lb_bwd", out_shape=jax.ShapeDtypeStruct((L, D), F32))(lb_logits, dlbs)


def _blk_cumsum(x, c, reverse=False):
    n = x.shape[0]
    pos = lax.broadcasted_iota(jnp.int32, x.shape, 0) % c
    s = 1
    while s < c:
        if reverse:
            shifted = pltpu.roll(x, n - s, 0)
            x = x + jnp.where(pos + s < c, shifted, 0.0)
        else:
            shifted = pltpu.roll(x, s, 0)
            x = x + jnp.where(pos >= s, shifted, 0.0)
        s *= 2
    return x


def _hgrn_prologue(q_ref, f_ref, lb_ref):
    lbv = lb_ref[...]
    z = f_ref[...]
    sig = jax.nn.sigmoid(z)
    one_m = 1.0 - lbv
    f = lbv + one_m * sig
    logf = jnp.log(jnp.maximum(f, F_MIN))
    k = one_m * jax.nn.sigmoid(-z)
    q = _silu(q_ref[...])
    return q, k, logf, f, sig, one_m


def _hgrn_fwd(h, lbs_l, gw):
    _, T, D = h.shape
    nh = D // LANES
    c = HG_C
    Tt = _pick(T, (512, 256, 128, 64, 32, 16))
    nb = Tt // c
    ng = c // SUB

    def body(q_ref, f_ref, i_ref, g_ref, lb_ref, gw_ref, o_ref, y_ref, sall_ref,
             st_ref, G_s, q_s, k_s, W_s, R_s, dS_s, o_s):
        @pl.when(pl.program_id(1) == 0)
        def _():
            st_ref[...] = jnp.zeros_like(st_ref)

        q, k, logf, _, _, _ = _hgrn_prologue(q_ref, f_ref, lb_ref)
        G_s[...] = _blk_cumsum(logf, c) * LOG2E
        q_s[...] = q
        k_s[...] = k
        ones = jnp.ones((LANES, LANES), MXU_DTYPE)
        rowid = lax.broadcasted_iota(jnp.int32, (SUB, LANES), 0)
        zero = jnp.zeros((SUB, LANES), F32)
        for bi in range(nb):
            r0 = bi * c
            glast = G_s[pl.ds(r0 + c - 1, 1), :]
            kd = k_s[pl.ds(r0, c), :] * jnp.exp2(glast - G_s[pl.ds(r0, c), :])
            dS_s[bi] = lax.dot_general(i_ref[pl.ds(r0, c), :].astype(MXU_DTYPE), kd.astype(MXU_DTYPE), _TN,
                                       preferred_element_type=F32)
        st = st_ref[...]
        for bi in range(nb):
            sall_ref[bi] = st
            st = st * jnp.exp2(G_s[pl.ds(bi * c + c - 1, 1), :]) + dS_s[bi]
        st_ref[...] = st
        for bi in range(nb):
            r0 = bi * c
            qd = q_s[pl.ds(r0, c), :] * jnp.exp2(G_s[pl.ds(r0, c), :])
            o_s[pl.ds(r0, c), :] = lax.dot_general(qd.astype(MXU_DTYPE), sall_ref[bi].astype(MXU_DTYPE), _NT,
                                                   preferred_element_type=F32)
        for bi in range(nb):
            r0 = bi * c
            w0 = bi * c * c
            Gg = [G_s[pl.ds(r0 + gi * SUB, SUB), :] for gi in range(ng)]
            qg = [q_s[pl.ds(r0 + gi * SUB, SUB), :] for gi in range(ng)]
            for s in range(c):
                gs = G_s[pl.ds(r0 + s, 1), :]
                ks = k_s[pl.ds(r0 + s, 1), :]
                parts = []
                for gi in range(ng):
                    if gi < s // SUB:
                        parts.append(zero)
                        continue
                    e = jnp.exp2(jnp.minimum(Gg[gi] - gs, 0.0))
                    if gi == s // SUB:
                        e = jnp.where(rowid >= s - gi * SUB, e, 0.0)
                    parts.append(e * qg[gi] * ks)
                W_s[pl.ds(w0 + s * c, c), :] = jnp.concatenate(parts, axis=0).astype(MXU_DTYPE)
        R_s[...] = jnp.dot(W_s[...], ones, preferred_element_type=F32)
        for bi in range(nb):
            r0 = bi * c
            w0 = bi * c * c
            acc = [o_s[pl.ds(r0 + gi * SUB, SUB), :] for gi in range(ng)]
            for s in range(c):
                vs = i_ref[pl.ds(r0 + s, 1), :]
                for gi in range(s // SUB, ng):
                    acc[gi] = acc[gi] + R_s[pl.ds(w0 + s * c + gi * SUB, SUB), :] * vs
            o_s[pl.ds(r0, c), :] = jnp.concatenate(acc, axis=0)
        o = o_s[...]
        n = o * lax.rsqrt(jnp.mean(o * o, axis=-1, keepdims=True) + RMS_EPS)
        o_ref[...] = o
        y_ref[...] = (n * gw_ref[...] * _silu(g_ref[...])).astype(ACT_DTYPE)

    def sec(s):
        return pl.BlockSpec((None, Tt, LANES), lambda hd, i: (s, i, hd))

    col = pl.BlockSpec((Tt, LANES), lambda hd, i: (i, hd))
    return pl.pallas_call(
        body, name="hgrn_fwd", grid=(nh, T // Tt),
        in_specs=[sec(0), sec(1), sec(2), sec(3), pl.BlockSpec((1, LANES), lambda hd, i: (0, hd)),
                  pl.BlockSpec((1, LANES), lambda hd, i: (0, 0))],
        out_specs=[col, col, pl.BlockSpec((nb, None, LANES, LANES), lambda hd, i: (i, hd, 0, 0))],
        out_shape=[jax.ShapeDtypeStruct((T, D), F32), jax.ShapeDtypeStruct((T, D), ACT_DTYPE),
                   jax.ShapeDtypeStruct((T // c, nh, LANES, LANES), F32)],
        scratch_shapes=[pltpu.VMEM((LANES, LANES), F32), pltpu.VMEM((Tt, LANES), F32),
                        pltpu.VMEM((Tt, LANES), F32), pltpu.VMEM((Tt, LANES), F32),
                        pltpu.VMEM((nb * c * c, LANES), MXU_DTYPE), pltpu.VMEM((nb * c * c, LANES), F32),
                        pltpu.VMEM((nb, LANES, LANES), F32), pltpu.VMEM((Tt, LANES), F32)],
        compiler_params=_cp("parallel", "arbitrary"))(h, h, h, h, lbs_l, gw)


def _hgrn_bwd(h, lbs_l, gw, o_pre, st_all, dy, dh):
    _, T, D = h.shape
    nh = D // LANES
    c = HG_C
    Tt = _pick(T, (512, 256, 128, 64, 32, 16))
    nb = Tt // c
    ng = c // SUB
    nT = T // Tt

    def body(q_ref, f_ref, i_ref, g_ref, lb_ref, gw_ref, o_ref, sall_ref, dy_ref, dh_in_ref,
             dh_ref, dlb_ref, dgw_ref,
             dst_ref, G_s, q_s, k_s, do_s, E_s, WP_s, dq_s, dk_s, dv_s, dG_s,
             R_s, dS_s, dstA_s, dqd_s, dkd_s, dvi_s, da_s):
        del dh_in_ref
        hd, ti = pl.program_id(0), pl.program_id(1)

        @pl.when(ti == 0)
        def _():
            dst_ref[...] = jnp.zeros_like(dst_ref)
            dlb_ref[...] = jnp.zeros_like(dlb_ref)

        @pl.when((ti == 0) & (hd == 0))
        def _():
            dgw_ref[...] = jnp.zeros_like(dgw_ref)

        q, k, logf, f, sig, one_m = _hgrn_prologue(q_ref, f_ref, lb_ref)
        G_s[...] = _blk_cumsum(logf, c) * LOG2E
        q_s[...] = q
        k_s[...] = k

        o = o_ref[...]
        gr = g_ref[...]
        dy_ = dy_ref[...]
        rr = lax.rsqrt(jnp.mean(o * o, axis=-1, keepdims=True) + RMS_EPS)
        n = o * rr
        sg = _silu(gr)
        gwv = gw_ref[...]
        dh_ref[3] = (dy_ * n * gwv * _dsilu(gr)).astype(ACT_DTYPE)
        dgw_ref[...] += jnp.sum(dy_ * n * sg, axis=0, keepdims=True)
        dn = dy_ * gwv * sg
        do_s[...] = rr * (dn - n * jnp.mean(dn * n, axis=-1, keepdims=True))

        ones = jnp.ones((LANES, LANES), MXU_DTYPE)
        rowid = lax.broadcasted_iota(jnp.int32, (SUB, LANES), 0)
        rowid_c = lax.broadcasted_iota(jnp.int32, (c, LANES), 0)
        zero = jnp.zeros((SUB, LANES), F32)
        cc = c * c
        for bi in range(nb):
            r0 = bi * c
            qd = q_s[pl.ds(r0, c), :] * jnp.exp2(G_s[pl.ds(r0, c), :])
            dS_s[bi] = lax.dot_general(do_s[pl.ds(r0, c), :].astype(MXU_DTYPE), qd.astype(MXU_DTYPE), _TN,
                                       preferred_element_type=F32)
        dst = dst_ref[...]
        for bi in range(nb - 1, -1, -1):
            dstA_s[bi] = dst
            dst = dst * jnp.exp2(G_s[pl.ds(bi * c + c - 1, 1), :]) + dS_s[bi]
        dst_ref[...] = dst
        for bi in range(nb):
            r0 = bi * c
            glast = G_s[pl.ds(r0 + c - 1, 1), :]
            kd = k_s[pl.ds(r0, c), :] * jnp.exp2(glast - G_s[pl.ds(r0, c), :])
            st = sall_ref[bi]
            dstb = dstA_s[bi]
            dst_m = dstb.astype(MXU_DTYPE)
            dqd_s[pl.ds(r0, c), :] = lax.dot_general(do_s[pl.ds(r0, c), :].astype(MXU_DTYPE), st.astype(MXU_DTYPE),
                                                     _NN, preferred_element_type=F32)
            dkd_s[pl.ds(r0, c), :] = lax.dot_general(i_ref[pl.ds(r0, c), :].astype(MXU_DTYPE), dst_m, _NN,
                                                     preferred_element_type=F32)
            dvi_s[pl.ds(r0, c), :] = lax.dot_general(kd.astype(MXU_DTYPE), dst_m, _NT,
                                                     preferred_element_type=F32)
            da_s[pl.ds(bi * SUB, 1), :] = jnp.sum(dstb * st, axis=0, keepdims=True)
        for bi in range(nb):
            r0 = bi * c
            e0, w0 = bi * cc, bi * 2 * cc
            Gg = [G_s[pl.ds(r0 + gi * SUB, SUB), :] for gi in range(ng)]
            kg = [k_s[pl.ds(r0 + gi * SUB, SUB), :] for gi in range(ng)]
            vg = [i_ref[pl.ds(r0 + gi * SUB, SUB), :] for gi in range(ng)]
            for t in range(c):
                gt = G_s[pl.ds(r0 + t, 1), :]
                qt = q_s[pl.ds(r0 + t, 1), :]
                dot_ = do_s[pl.ds(r0 + t, 1), :]
                ep, wp, pp = [], [], []
                for gi in range(ng):
                    if gi > t // SUB:
                        ep.append(zero)
                        wp.append(zero)
                        pp.append(zero)
                        continue
                    e = jnp.exp2(jnp.minimum(gt - Gg[gi], 0.0))
                    if gi == t // SUB:
                        e = jnp.where(rowid <= t - gi * SUB, e, 0.0)
                    ep.append(e)
                    wp.append(e * kg[gi] * qt)
                    pp.append(vg[gi] * dot_)
                E_s[pl.ds(e0 + t * c, c), :] = jnp.concatenate(ep, axis=0)
                WP_s[pl.ds(w0 + t * c, c), :] = jnp.concatenate(wp, axis=0).astype(MXU_DTYPE)
                WP_s[pl.ds(w0 + cc + t * c, c), :] = jnp.concatenate(pp, axis=0).astype(MXU_DTYPE)
        R_s[...] = jnp.dot(WP_s[...], ones, preferred_element_type=F32)
        for bi in range(nb):
            r0 = bi * c
            e0, w0 = bi * cc, bi * 2 * cc
            kg = [k_s[pl.ds(r0 + gi * SUB, SUB), :] for gi in range(ng)]
            dk_g = [zero] * ng
            dv_g = [zero] * ng
            dq_g = [zero] * ng
            for t in range(c):
                qt = q_s[pl.ds(r0 + t, 1), :]
                dot_ = do_s[pl.ds(r0 + t, 1), :]
                tot = None
                for gi in range(t // SUB + 1):
                    lo = t * c + gi * SUB
                    dae = R_s[pl.ds(w0 + cc + lo, SUB), :] * E_s[pl.ds(e0 + lo, SUB), :]
                    z = dae * kg[gi]
                    tot = z if tot is None else tot + z
                    dk_g[gi] = dk_g[gi] + dae * qt
                    dv_g[gi] = dv_g[gi] + R_s[pl.ds(w0 + lo, SUB), :] * dot_
                gt_ = t // SUB
                dq_g[gt_] = jnp.where(rowid == t - gt_ * SUB, jnp.sum(tot, axis=0, keepdims=True), dq_g[gt_])
            dq_i = jnp.concatenate(dq_g, axis=0)
            dk_i = jnp.concatenate(dk_g, axis=0)
            dv_i = jnp.concatenate(dv_g, axis=0)
            Gb = G_s[pl.ds(r0, c), :]
            qb = q_s[pl.ds(r0, c), :]
            kb = k_s[pl.ds(r0, c), :]
            glast = G_s[pl.ds(r0 + c - 1, 1), :]
            eg = jnp.exp2(Gb)
            egl = jnp.exp2(glast - Gb)
            dqd = dqd_s[pl.ds(r0, c), :]
            dkd = dkd_s[pl.ds(r0, c), :]
            dq_s[pl.ds(r0, c), :] = dqd * eg + dq_i
            dk_s[pl.ds(r0, c), :] = dkd * egl + dk_i
            dv_s[pl.ds(r0, c), :] = dvi_s[pl.ds(r0, c), :] + dv_i
            dkdkd = dkd * kb * egl
            dG = dqd * qb * eg + qb * dq_i - kb * dk_i - dkdkd
            dglast = jnp.sum(dkdkd, axis=0, keepdims=True) + da_s[pl.ds(bi * SUB, 1), :] * jnp.exp2(glast)
            dG_s[pl.ds(r0, c), :] = dG + jnp.where(rowid_c == c - 1, dglast, 0.0)

        dlogf = _blk_cumsum(dG_s[...], c, reverse=True)
        df = jnp.where(f > F_MIN, dlogf / f, 0.0)
        dk = dk_s[...]
        dh_ref[0] = (dq_s[...] * _dsilu(q_ref[...])).astype(ACT_DTYPE)
        dh_ref[1] = ((df - dk) * one_m * sig * (1.0 - sig)).astype(ACT_DTYPE)
        dh_ref[2] = dv_s[...].astype(ACT_DTYPE)
        dlb_ref[...] += jnp.sum((df - dk) * (1.0 - sig), axis=0, keepdims=True)

    def sec(s):
        return pl.BlockSpec((None, Tt, LANES), lambda hd, i: (s, nT - 1 - i, hd))

    col = pl.BlockSpec((Tt, LANES), lambda hd, i: (nT - 1 - i, hd))
    tile = pltpu.VMEM((Tt, LANES), F32)
    return pl.pallas_call(
        body, name="hgrn_bwd", grid=(nh, nT),
        in_specs=[sec(0), sec(1), sec(2), sec(3), pl.BlockSpec((1, LANES), lambda hd, i: (0, hd)),
                  pl.BlockSpec((1, LANES), lambda hd, i: (0, 0)), col,
                  pl.BlockSpec((nb, None, LANES, LANES), lambda hd, i: (nT - 1 - i, hd, 0, 0)), col,
                  pl.BlockSpec(memory_space=pl.ANY)],
        out_specs=[pl.BlockSpec((4, Tt, LANES), lambda hd, i: (0, nT - 1 - i, hd)),
                   pl.BlockSpec((1, LANES), lambda hd, i: (0, hd)),
                   pl.BlockSpec((1, LANES), lambda hd, i: (0, 0))],
        out_shape=[jax.ShapeDtypeStruct(dh.shape, dh.dtype), jax.ShapeDtypeStruct((1, D), F32),
                   jax.ShapeDtypeStruct((1, LANES), F32)],
        scratch_shapes=[pltpu.VMEM((LANES, LANES), F32), tile, tile, tile, tile,
                        pltpu.VMEM((nb * c * c, LANES), F32), pltpu.VMEM((2 * nb * c * c, LANES), MXU_DTYPE),
                        tile, tile, tile, tile,
                        pltpu.VMEM((2 * nb * c * c, LANES), F32), pltpu.VMEM((nb, LANES, LANES), F32),
                        pltpu.VMEM((nb, LANES, LANES), F32), tile, tile, tile, pltpu.VMEM((nb * SUB, LANES), F32)],
        input_output_aliases={9: 0},
        compiler_params=_cp("arbitrary", "arbitrary"))(h, h, h, h, lbs_l, gw, o_pre, st_all, dy, dh)


def _shifted_copies(src, cs, dst, rows):
    for b in range(1, SUB):
        dst[b - 1] = src[pl.ds(b, rows + CONV_HALO - SUB), cs]


def _shifted(src, cs, copies, shift, rows):
    a8, b = divmod(shift, SUB)
    if b == 0:
        return src[pl.ds(shift, rows), cs]
    return copies[b - 1, pl.ds(a8 * SUB, rows), :]


def _conv_fwd(h, w_dw, b_dw, ln_g, ln_b):
    _, T, D = h.shape
    Tt = _pick(T, (256, 128, 64, 32))
    hb = Tt // CONV_HALO
    off = CONV_HALO - (CONV_WIDTH - 1)

    def body(a_ref, b_ref, ap_ref, bp_ref, w_ref, bd_ref, g_ref, be_ref, yc_ref, y_ref, U_s, Ub_s):
        first = pl.program_id(0) == 0
        up = ap_ref[...] * jax.nn.sigmoid(bp_ref[...])
        U_s[pl.ds(0, CONV_HALO), :] = jnp.where(first, 0.0, up)
        U_s[pl.ds(CONV_HALO, Tt), :] = a_ref[...] * jax.nn.sigmoid(b_ref[...])
        for cb in range(D // LANES):
            cs = pl.ds(cb * LANES, LANES)
            _shifted_copies(U_s, cs, Ub_s, Tt)
            acc = jnp.zeros((Tt, LANES), F32)
            for j in range(CONV_WIDTH):
                acc = acc + w_ref[pl.ds(j, 1), cs] * _shifted(U_s, cs, Ub_s, off + j, Tt)
            yc_ref[:, cs] = acc + bd_ref[:, cs]
        yc = yc_ref[...]
        mu = jnp.mean(yc, axis=-1, keepdims=True)
        zc = yc - mu
        var = jnp.mean(zc * zc, axis=-1, keepdims=True)
        ln = zc * lax.rsqrt(var + LN_EPS) * g_ref[...] + be_ref[...]
        y_ref[...] = _silu(ln).astype(ACT_DTYPE)

    def main(s):
        return pl.BlockSpec((None, Tt, D), lambda i: (s, i, 0))

    def prev(s):
        return pl.BlockSpec((None, CONV_HALO, D), lambda i: (s, jnp.maximum(i * hb - 1, 0), 0))

    row = pl.BlockSpec((Tt, D), lambda i: (i, 0))
    vec = pl.BlockSpec((1, D), lambda i: (0, 0))
    return pl.pallas_call(
        body, name="conv_fwd", grid=(T // Tt,),
        in_specs=[main(4), main(5), prev(4), prev(5), pl.BlockSpec((CONV_HALO, D), lambda i: (0, 0)),
                  vec, vec, vec],
        out_specs=[row, row],
        out_shape=[jax.ShapeDtypeStruct((T, D), F32), jax.ShapeDtypeStruct((T, D), ACT_DTYPE)],
        scratch_shapes=[pltpu.VMEM((CONV_HALO + Tt, D), F32),
                        pltpu.VMEM((SUB - 1, Tt + CONV_HALO - SUB, LANES), F32)],
        compiler_params=_cp("parallel"))(h, h, h, h, w_dw, b_dw, ln_g, ln_b)


def _conv_bwd(h, w_dw, ln_g, ln_b, yc, dy, dh):
    _, T, D = h.shape
    Tt = _pick(T, (256, 128, 64, 32))
    hb = Tt // CONV_HALO
    nT = T // Tt
    nhb = T // CONV_HALO
    off = CONV_HALO - (CONV_WIDTH - 1)

    def body(a_ref, b_ref, ap_ref, bp_ref, w_ref, g_ref, be_ref, yc_ref, ycn_ref, dy_ref, dyn_ref, dh_in_ref,
             dh_ref, dw_ref, dbd_ref, dg_ref, dbe_ref, U_s, DY_s, du_s, Ub_s, DYb_s):
        del dh_in_ref
        i = pl.program_id(0)

        @pl.when(i == 0)
        def _():
            dw_ref[...] = jnp.zeros_like(dw_ref)
            dbd_ref[...] = jnp.zeros_like(dbd_ref)
            dg_ref[...] = jnp.zeros_like(dg_ref)
            dbe_ref[...] = jnp.zeros_like(dbe_ref)

        gv = g_ref[...]
        bev = be_ref[...]

        def ln_silu_bwd(ycv, dyv):
            mu = jnp.mean(ycv, axis=-1, keepdims=True)
            zc = ycv - mu
            rstd = lax.rsqrt(jnp.mean(zc * zc, axis=-1, keepdims=True) + LN_EPS)
            xhat = zc * rstd
            dln = dyv * _dsilu(xhat * gv + bev)
            dxh = dln * gv
            dyc = rstd * (dxh - jnp.mean(dxh, axis=-1, keepdims=True)
                          - xhat * jnp.mean(dxh * xhat, axis=-1, keepdims=True))
            return dyc, dln, xhat

        dyc, dln, xhat = ln_silu_bwd(yc_ref[...], dy_ref[...])
        dg_ref[...] += jnp.sum(dln * xhat, axis=0, keepdims=True)
        dbe_ref[...] += jnp.sum(dln, axis=0, keepdims=True)
        dbd_ref[...] += jnp.sum(dyc, axis=0, keepdims=True)
        DY_s[pl.ds(0, Tt), :] = dyc
        dycn, _, _ = ln_silu_bwd(ycn_ref[...], dyn_ref[...])
        DY_s[pl.ds(Tt, CONV_HALO), :] = jnp.where(i == nT - 1, 0.0, dycn)

        sb = jax.nn.sigmoid(b_ref[...])
        av = a_ref[...]
        up = ap_ref[...] * jax.nn.sigmoid(bp_ref[...])
        U_s[pl.ds(0, CONV_HALO), :] = jnp.where(i == 0, 0.0, up)
        U_s[pl.ds(CONV_HALO, Tt), :] = av * sb

        for cb in range(D // LANES):
            cs = pl.ds(cb * LANES, LANES)
            _shifted_copies(U_s, cs, Ub_s, Tt)
            _shifted_copies(DY_s, cs, DYb_s, Tt)
            dyb = DY_s[pl.ds(0, Tt), cs]
            acc = jnp.zeros((Tt, LANES), F32)
            for j in range(CONV_WIDTH):
                acc = acc + w_ref[pl.ds(j, 1), cs] * _shifted(DY_s, cs, DYb_s, CONV_WIDTH - 1 - j, Tt)
                dw_ref[pl.ds(j, 1), cs] += jnp.sum(dyb * _shifted(U_s, cs, Ub_s, off + j, Tt), axis=0, keepdims=True)
            du_s[:, cs] = acc
        du = du_s[...]
        dh_ref[0] = (du * sb).astype(ACT_DTYPE)
        dh_ref[1] = (du * av * sb * (1.0 - sb)).astype(ACT_DTYPE)

    def main(s):
        return pl.BlockSpec((None, Tt, D), lambda i: (s, i, 0))

    def prev(s):
        return pl.BlockSpec((None, CONV_HALO, D), lambda i: (s, jnp.maximum(i * hb - 1, 0), 0))

    row = pl.BlockSpec((Tt, D), lambda i: (i, 0))
    nxt = pl.BlockSpec((CONV_HALO, D), lambda i: (jnp.minimum((i + 1) * hb, nhb - 1), 0))
    vec = pl.BlockSpec((1, D), lambda i: (0, 0))
    wspec = pl.BlockSpec((CONV_HALO, D), lambda i: (0, 0))
    return pl.pallas_call(
        body, name="conv_bwd", grid=(nT,),
        in_specs=[main(4), main(5), prev(4), prev(5), wspec, vec, vec, row, nxt, row, nxt,
                  pl.BlockSpec(memory_space=pl.ANY)],
        out_specs=[pl.BlockSpec((2, Tt, D), lambda i: (2, i, 0)), wspec, vec, vec, vec],
        out_shape=[jax.ShapeDtypeStruct(dh.shape, dh.dtype), jax.ShapeDtypeStruct((CONV_HALO, D), F32),
                   jax.ShapeDtypeStruct((1, D), F32), jax.ShapeDtypeStruct((1, D), F32),
                   jax.ShapeDtypeStruct((1, D), F32)],
        scratch_shapes=[pltpu.VMEM((CONV_HALO + Tt, D), F32), pltpu.VMEM((Tt + CONV_HALO, D), F32),
                        pltpu.VMEM((Tt, D), F32),
                        pltpu.VMEM((SUB - 1, Tt + CONV_HALO - SUB, LANES), F32),
                        pltpu.VMEM((SUB - 1, Tt + CONV_HALO - SUB, LANES), F32)],
        input_output_aliases={11: 0},
        compiler_params=_cp("arbitrary"))(h, h, h, h, w_dw, ln_g, ln_b, yc, yc, dy, dy, dh)


def _adamw(name, w, m, v, parts, part_specs, tr, prefetch=None, nsteps=None, row_map=None, prev=None):
    R, C = w.shape
    bc1 = 1.0 - ADAM_B1 ** ADAM_STEP
    bc2 = 1.0 - ADAM_B2 ** ADAM_STEP
    npart = len(parts)
    npre = 0 if prefetch is None else 1
    nprev = 0 if prev is None else 4

    def body(*refs):
        refs = refs[npre:]
        w_ref, m_ref, v_ref = refs[:3]
        p_refs = refs[3:3 + npart]
        g_ref, d_ref, mo_ref, vo_ref = refs[3 + npart + nprev:]
        g = p_refs[0][...].astype(F32)
        for p in p_refs[1:]:
            g = g + p[...].astype(F32)
        wv = w_ref[...]
        mn = ADAM_B1 * m_ref[...] + (1.0 - ADAM_B1) * g
        vn = ADAM_B2 * v_ref[...] + (1.0 - ADAM_B2) * (g * g)
        m_hat = mn / bc1
        v_hat = vn / bc2
        g_ref[...] = g
        d_ref[...] = -ADAM_LR * (m_hat / (jnp.sqrt(v_hat) + ADAM_EPS) + ADAM_WD * wv)
        mo_ref[...] = mn
        vo_ref[...] = vn

    if row_map is None:
        row_map = (lambda i: (i, 0)) if prefetch is None else (lambda i, s: (i, 0))
    row = pl.BlockSpec((tr, C), row_map)
    out = jax.ShapeDtypeStruct((R, C), F32)
    gs = pltpu.PrefetchScalarGridSpec(
        num_scalar_prefetch=npre, grid=(R // tr if nsteps is None else nsteps,),
        in_specs=[row, row, row] + list(part_specs) + [_ANY] * nprev, out_specs=[row] * 4)
    args = ([prefetch] if npre else []) + [w, m, v] + list(parts) + (list(prev) if nprev else [])
    first_prev = npre + 3 + npart
    return pl.pallas_call(body, name=name, grid_spec=gs, out_shape=[out] * 4,
                          input_output_aliases={first_prev + i: i for i in range(nprev)},
                          compiler_params=_cp("parallel"))(*args)


def _pair_add(p, r1, my_c):
    _, R, C = r1.shape
    tr = max(t for t in range(16, 1025, 16) if R % t == 0)

    def body(c_ref, p_ref, r_ref, q_ref):
        del c_ref
        q_ref[...] = (p_ref[...].astype(F32) + r_ref[...].astype(F32)).astype(q_ref.dtype)

    gs = pltpu.PrefetchScalarGridSpec(
        num_scalar_prefetch=1, grid=(4, R // tr),
        in_specs=[pl.BlockSpec((None, tr, C), lambda j, i, c: (2 * j + c[0], i, 0)),
                  pl.BlockSpec((None, tr, C), lambda j, i, c: (j, i, 0))],
        out_specs=pl.BlockSpec((None, tr, C), lambda j, i, c: (j, i, 0)))
    return pl.pallas_call(body, name="pair_add", grid_spec=gs, out_shape=jax.ShapeDtypeStruct(r1.shape, r1.dtype),
                          compiler_params=_cp("parallel", "parallel"))(my_c, p, r1)


def _place():
    x, y, c = lax.axis_index("x"), lax.axis_index("y"), lax.axis_index("c")
    chips = [(1 - x, y), (x, 1 - y), (1 - x, 1 - y)]
    return x, y, c, chips


def _hbm(a):
    return pltpu.with_memory_space_constraint(a, pltpu.HBM)


def _gather_targets():
    x, y, c, chips = _place()
    return 4 * x + 2 * y + c, [(x, y, 1 - c)] + [(*chip, c) for chip in chips]


def _gather_start(name, shards, zones, after=None):
    n = len(shards)
    lands = [_hbm(z) for z in zones]
    n_in = 2 * n + (0 if after is None else 1)

    def body(*refs):
        srcs, zones = refs[:n], refs[n:2 * n]
        send, recv, token = refs[n_in], refs[n_in + 1], refs[-1]
        mine, targets = _gather_targets()
        for a in range(n):
            for k, to in enumerate(targets):
                pltpu.make_async_remote_copy(
                    src_ref=srcs[a], dst_ref=zones[a].at[mine], send_sem=send.at[4 * a + k],
                    recv_sem=recv.at[4 * a + k], device_id=to, device_id_type=MESH).start()
        token[...] = jnp.zeros_like(token)

    sem = pltpu.SemaphoreType.DMA((4 * n,))
    out_shape = ([sem, sem] + [pltpu.HBM(s.shape, s.dtype) for s in shards]
                 + [pltpu.HBM(z.shape, z.dtype) for z in lands] + [jax.ShapeDtypeStruct((8, LANES), F32)])
    outs = pl.pallas_call(
        body, name=name, out_shape=out_shape, in_specs=[_HBM] * (2 * n) + ([] if after is None else [_ANY]),
        out_specs=[_SEM, _SEM] + [_HBM] * (2 * n) + [pl.BlockSpec(memory_space=pltpu.VMEM)],
        input_output_aliases={i: 2 + i for i in range(2 * n)},
        compiler_params=pltpu.CompilerParams(has_side_effects=_EFFECT))(
            *[_hbm(s) for s in shards], *lands, *([] if after is None else [after]))
    return outs[0], outs[1], list(outs[2:2 + n]), list(outs[2 + n:2 + 2 * n]), outs[-1]


def _gather_wait(name, shards, zones, send, recv, after):
    per = len(shards)

    def body(*refs):
        srcs, lz = refs[:per], refs[per:2 * per]
        send_s, recv_s = refs[2 * per], refs[2 * per + 1]
        mine, targets = _gather_targets()
        for a in range(per):
            for k, to in enumerate(targets):
                cp = pltpu.make_async_remote_copy(
                    src_ref=srcs[a], dst_ref=lz[a].at[mine], send_sem=send_s.at[4 * a + k],
                    recv_sem=recv_s.at[4 * a + k], device_id=to, device_id_type=MESH)
                cp.wait_send()
                cp.wait_recv()

    outs = pl.pallas_call(
        body, name=name, out_shape=[pltpu.HBM(s.shape, s.dtype) for s in shards + zones],
        in_specs=[_HBM] * (2 * per) + [_SEM, _SEM, _ANY], out_specs=[_HBM] * (2 * per),
        input_output_aliases={i: i for i in range(2 * per)},
        compiler_params=pltpu.CompilerParams(has_side_effects=_EFFECT))(*shards, *zones, send, recv, after)
    return outs[:per], outs[per:]


def _gather_finish(zones):
    n = len(zones)

    def body(*refs):
        lz = refs[n:2 * n]
        send_sems, recv_sems = refs[2 * n:]
        x, y, c, chips = _place()

        def fwd(a, j, pc):
            cx, cy = chips[j]
            blk = lz[a].at[4 * cx + 2 * cy + pc]
            return pltpu.make_async_remote_copy(
                src_ref=blk, dst_ref=blk, send_sem=send_sems.at[3 * a + j], recv_sem=recv_sems.at[3 * a + j],
                device_id=(x, y, 1 - c), device_id_type=MESH)

        sends = [fwd(a, j, c) for a in range(n) for j in range(3)]
        for cp in sends:
            cp.start()
        for a in range(n):
            for j in range(3):
                fwd(a, j, 1 - c).wait_recv()
        for cp in sends:
            cp.wait_send()

    return pl.pallas_call(
        body, name="gather_finish", out_shape=[jax.ShapeDtypeStruct(z.shape, z.dtype) for z in zones],
        in_specs=[_ANY] * n, out_specs=[_ANY] * n, input_output_aliases={a: a for a in range(n)},
        scratch_shapes=[pltpu.SemaphoreType.DMA((3 * n,)), pltpu.SemaphoreType.DMA((3 * n,))])(*zones)


def _place_own(shard, dev):
    R, C = shard.shape
    tr = max(t for t in range(16, 1025, 16) if R % t == 0)

    def body(d_ref, s_ref, z_ref):
        del d_ref
        z_ref[...] = s_ref[...]

    gs = pltpu.PrefetchScalarGridSpec(
        num_scalar_prefetch=1, grid=(R // tr,), in_specs=[pl.BlockSpec((tr, C), lambda i, d: (i, 0))],
        out_specs=pl.BlockSpec((None, tr, C), lambda i, d: (d[0], i, 0)))
    return pl.pallas_call(body, name="place_own", grid_spec=gs,
                          out_shape=jax.ShapeDtypeStruct((N_DEV, R, C), shard.dtype),
                          compiler_params=_cp("parallel"))(dev, shard)


def _exchange_sibling(bufs):
    n_arr = len(bufs)

    def body(*refs):
        srcs, outs = refs[:n_arr], refs[n_arr:2 * n_arr]
        send_sems, recv_sems = refs[2 * n_arr:]
        x, y, c, _ = _place()
        copies = []
        for n in range(n_arr):
            for j in range(4):
                copies.append(pltpu.make_async_remote_copy(
                    src_ref=srcs[n].at[2 * j + 1 - c], dst_ref=outs[n].at[j],
                    send_sem=send_sems.at[4 * n + j], recv_sem=recv_sems.at[4 * n + j],
                    device_id=(x, y, 1 - c), device_id_type=MESH))
        for cp in copies:
            cp.start()
        for cp in copies:
            cp.wait()

    return pl.pallas_call(
        body, name="exchange_sibling",
        out_shape=[jax.ShapeDtypeStruct((4,) + b.shape[1:], b.dtype) for b in bufs],
        in_specs=[_ANY] * n_arr, out_specs=[_ANY] * n_arr,
        scratch_shapes=[pltpu.SemaphoreType.DMA((4 * n_arr,)), pltpu.SemaphoreType.DMA((4 * n_arr,))])(*bufs)


def _chip_copies(srcs, zones, send, recv):
    _, _, c, chips = _place()
    return [pltpu.make_async_remote_copy(
        src_ref=srcs[n].at[2 * cx + cy], dst_ref=zones[n].at[k], send_sem=send.at[3 * n + k],
        recv_sem=recv.at[3 * n + k], device_id=(cx, cy, c), device_id_type=MESH)
        for n in range(len(srcs)) for k, (cx, cy) in enumerate(chips)]


def _exchange_chips_start(name, bufs, after=None):
    n = len(bufs)
    n_in = 2 * n + (0 if after is None else 1)
    lands = [_hbm(lax.empty((3,) + b.shape[1:], b.dtype)) for b in bufs]

    def body(*refs):
        srcs, zones = refs[:n], refs[n:2 * n]
        send, recv, token = refs[n_in], refs[n_in + 1], refs[-1]
        for cp in _chip_copies(srcs, zones, send, recv):
            cp.start()
        token[...] = jnp.zeros_like(token)

    sem = pltpu.SemaphoreType.DMA((3 * n,))
    outs = pl.pallas_call(
        body, name=name,
        out_shape=[sem, sem] + [pltpu.HBM(b.shape, b.dtype) for b in bufs]
        + [pltpu.HBM(z.shape, z.dtype) for z in lands] + [jax.ShapeDtypeStruct((8, LANES), F32)],
        in_specs=[_HBM] * (2 * n) + ([] if after is None else [_ANY]),
        out_specs=[_SEM, _SEM] + [_HBM] * (2 * n) + [pl.BlockSpec(memory_space=pltpu.VMEM)],
        input_output_aliases={i: 2 + i for i in range(2 * n)},
        compiler_params=pltpu.CompilerParams(has_side_effects=_EFFECT))(
            *[_hbm(b) for b in bufs], *lands, *([] if after is None else [after]))
    return outs[0], outs[1], outs[2:2 + n], outs[2 + n:2 + 2 * n], outs[-1]


def _exchange_chips_wait(name, bufs, zones, send, recv, after):
    n = len(bufs)

    def body(*refs):
        for cp in _chip_copies(refs[:n], refs[n:2 * n], refs[2 * n], refs[2 * n + 1]):
            cp.wait_send()
            cp.wait_recv()

    outs = pl.pallas_call(
        body, name=name, out_shape=[pltpu.HBM(a.shape, a.dtype) for a in list(bufs) + list(zones)],
        in_specs=[_HBM] * (2 * n) + [_SEM, _SEM, _ANY], out_specs=[_HBM] * (2 * n),
        input_output_aliases={i: i for i in range(2 * n)},
        compiler_params=pltpu.CompilerParams(has_side_effects=_EFFECT))(*bufs, *zones, send, recv, after)
    return outs[n:]


def _all_gather_small(part, per_peer=False, dep=None):
    block = part.shape[1:] if per_peer else part.shape

    def body(src, *rest):
        out, send_sems, recv_sems, local_sem = rest[-4:]
        x, y, c, _ = _place()
        me = 4 * x + 2 * y + c
        mine = pltpu.make_async_copy(src.at[me] if per_peer else src, out.at[me], local_sem)
        mine.start()
        copies = []
        for r in range(1, N_DEV):
            dx, dy, dc = (r >> 2) & 1, (r >> 1) & 1, r & 1
            peer = (1 - x if dx else x, 1 - y if dy else y, 1 - c if dc else c)
            copies.append(pltpu.make_async_remote_copy(
                src_ref=src.at[4 * peer[0] + 2 * peer[1] + peer[2]] if per_peer else src, dst_ref=out.at[me],
                send_sem=send_sems.at[r - 1], recv_sem=recv_sems.at[r - 1],
                device_id=peer, device_id_type=MESH))
        for cp in copies:
            cp.start()
        for cp in copies:
            cp.wait()
        mine.wait()

    ins = [part] + ([] if dep is None else [dep])
    return pl.pallas_call(
        body, name="exchange_small" if per_peer else "all_gather_small",
        out_shape=jax.ShapeDtypeStruct((N_DEV,) + block, part.dtype),
        in_specs=[_ANY] * len(ins), out_specs=_ANY,
        scratch_shapes=[pltpu.SemaphoreType.DMA((N_DEV - 1,)), pltpu.SemaphoreType.DMA((N_DEV - 1,)),
                        pltpu.SemaphoreType.DMA])(*ins)


def _small_exchanges(part, per_peer):
    def body(src, pp, out, out2, send_sems, recv_sems, local_sems):
        x, y, c, _ = _place()
        me = 4 * x + 2 * y + c
        mine = [pltpu.make_async_copy(src, out.at[me], local_sems.at[0]),
                pltpu.make_async_copy(pp.at[me], out2.at[me], local_sems.at[1])]
        copies = []
        for r in range(1, N_DEV):
            dx, dy, dc = (r >> 2) & 1, (r >> 1) & 1, r & 1
            peer = (1 - x if dx else x, 1 - y if dy else y, 1 - c if dc else c)
            copies.append(pltpu.make_async_remote_copy(
                src_ref=src, dst_ref=out.at[me], send_sem=send_sems.at[r - 1], recv_sem=recv_sems.at[r - 1],
                device_id=peer, device_id_type=MESH))
            copies.append(pltpu.make_async_remote_copy(
                src_ref=pp.at[4 * peer[0] + 2 * peer[1] + peer[2]], dst_ref=out2.at[me],
                send_sem=send_sems.at[N_DEV - 2 + r], recv_sem=recv_sems.at[N_DEV - 2 + r],
                device_id=peer, device_id_type=MESH))
        for cp in mine + copies:
            cp.start()
        for cp in copies:
            cp.wait()
        for cp in mine:
            cp.wait()

    n_sem = 2 * (N_DEV - 1)
    return pl.pallas_call(
        body, name="small_exchanges",
        out_shape=[jax.ShapeDtypeStruct((N_DEV,) + part.shape, part.dtype),
                   jax.ShapeDtypeStruct(per_peer.shape, per_peer.dtype)],
        in_specs=[_ANY, _ANY], out_specs=[_ANY, _ANY],
        scratch_shapes=[pltpu.SemaphoreType.DMA((n_sem,)), pltpu.SemaphoreType.DMA((n_sem,)),
                        pltpu.SemaphoreType.DMA((2,))])(part, per_peer)


def _layer_fwd(xin, xin_bf, w_in, rest, P, alpha, dep=None):
    h = _proj_in(xin_bf, w_in, P["b_in"], dep=dep)
    o_pre, y_hg, st_all = _hgrn_fwd(h, P["lbs"], P["g_norm_w"])
    yc_pre, y_cv = _conv_fwd(h, P["w_dw"], P["b_dw"], P["conv_ln_g"], P["conv_ln_b"])
    W = rest(y_cv)
    y_h, y_c, merged, x1, x1_bf, z1 = _mixer_out(y_hg, y_cv, h, xin, W["w_a"], W["w_b"], P["b_b"], W["w_o"], alpha,
                                                 P["ln1_g"], P["ln1_b"])
    up, act = _ffn_up_swiglu(x1_bf, W["w_up"])
    x2, x2_bf, z2 = _ffn_down_ln2(act, W["w_down"], x1, alpha, P["ln2_g"], P["ln2_b"])
    saved = dict(xin_bf=xin_bf, h=h, o_pre=o_pre, y_hg=y_hg, st_all=st_all, yc_pre=yc_pre, y_cv=y_cv,
                 y_h=y_h, y_c=y_c, merged=merged, z1=z1, x1_bf=x1_bf, up=up, act=act, z2=z2)
    return x2, x2_bf, saved


def _layer_bwd(dx2, S, W, P, alpha, dep=None, early=None):
    dz2, dz2_bf, dup, dln2_g, dln2_b = _ln2_ffn_down_bwd(S["z2"], dx2, P["ln2_g"], W["w_down"], S["up"], dep=dep)
    dw_down = _mm_tn("ffn_down_dw", S["act"], dz2_bf, ACT_DTYPE)
    dx1 = _ffn_up_dx(dup, W["w_up"], dz2, alpha)
    dw_up = _ffn_up_dw(S["x1_bf"], dup)
    dz1, dz1_bf, dy_h, dy_c, dy_hg, dy_cv, dh, dln1_g, dln1_b, db_b = _mixer_out_bwd(
        S["z1"], dx1, S["y_h"], S["y_c"], S["h"], W["w_a"], W["w_b"], W["w_o"], P["ln1_g"])
    dw_abo = _branch_dw([(S["y_hg"], dy_h), (S["y_cv"], dy_c), (S["merged"], dz1_bf)], dz1.shape[1] // N_DEV)
    if early is not None:
        token = early(dict(w_abo=dw_abo, w_down=dw_down, w_up=dw_up))
        dy_cv = dy_cv + token[0, 0]
    dh, dw_dw, db_dw, dcln_g, dcln_b = _conv_bwd(S["h"], P["w_dw"], P["conv_ln_g"], P["conv_ln_b"],
                                                 S["yc_pre"], dy_cv, dh)
    dh, dlbs, dgw = _hgrn_bwd(S["h"], P["lbs"], P["g_norm_w"], S["o_pre"], S["st_all"], dy_hg, dh)
    dxin = _proj_in_dx(dh, W["w_in"], dz1, alpha)
    dw_in, db_in = _proj_in_dw(S["xin_bf"], dh)
    big = dict(w_in=dw_in, w_abo=dw_abo, w_down=dw_down, w_up=dw_up)
    small = dict(b_in=db_in, lbs=dlbs, g_norm_w=dgw, w_dw=dw_dw, b_dw=db_dw, conv_ln_g=dcln_g,
                 conv_ln_b=dcln_b, b_b=db_b, ln1_g=dln1_g, ln1_b=dln1_b, ln2_g=dln2_g, ln2_b=dln2_b)
    return dxin, big, small


_SMALL = ("b_in", "lb_logits", "g_norm_w", "b_dw", "conv_ln_g", "conv_ln_b", "b_b", "ln1_g", "ln1_b", "ln2_g",
          "ln2_b")


def _pack_small(per_layer, ln0_g, ln0_b, extra_row, D, L):
    rows = []
    for l in range(L):
        for n in _SMALL:
            a = per_layer[n][l]
            if n == "b_in":
                rows.append(a.reshape(N_SEC, D))
            elif n == "g_norm_w":
                rows.append(jnp.pad(a.reshape(1, -1), ((0, 0), (0, D - a.size))))
            else:
                rows.append(a.reshape(1, D))
    rows += [ln0_g.reshape(1, D), ln0_b.reshape(1, D), extra_row]
    buf = jnp.concatenate(rows, axis=0)
    pad = (-buf.shape[0]) % 8
    return jnp.pad(buf, ((0, pad), (0, 0)))


def _unpack_small(buf, D, L, hv):
    out = {n: [] for n in _SMALL}
    r = 0
    for l in range(L):
        for n in _SMALL:
            if n == "b_in":
                out[n].append(buf[r:r + N_SEC].reshape(N_SEC * D))
                r += N_SEC
            elif n == "g_norm_w":
                out[n].append(buf[r, :hv])
                r += 1
            else:
                out[n].append(buf[r])
                r += 1
    res = {n: jnp.stack(v) for n, v in out.items()}
    res["ln0_g"] = buf[r]
    res["ln0_b"] = buf[r + 1]
    return res, r + 2


def kernel(x, ln0_g, ln0_b, w_in, b_in, lb_logits, g_norm_w, w_a, w_dw, b_dw, conv_ln_g, conv_ln_b, w_b, b_b, w_o, ln1_g, ln1_b, w_up, w_down, ln2_g, ln2_b, loss_target, m_ln0_g, m_ln0_b, m_w_in, m_b_in, m_lb_logits, m_g_norm_w, m_w_a, m_w_dw, m_b_dw, m_conv_ln_g, m_conv_ln_b, m_w_b, m_b_b, m_w_o, m_ln1_g, m_ln1_b, m_w_up, m_w_down, m_ln2_g, m_ln2_b, v_ln0_g, v_ln0_b, v_w_in, v_b_in, v_lb_logits, v_g_norm_w, v_w_a, v_w_dw, v_b_dw, v_conv_ln_g, v_conv_ln_b, v_w_b, v_b_b, v_w_o, v_ln1_g, v_ln1_b, v_w_up, v_w_down, v_ln2_g, v_ln2_b):
    L, D = w_in.shape[0], w_in.shape[1]
    T = x.shape[0] * x.shape[1]
    Dn = w_in.shape[2]
    rs = w_a.shape[1]
    rd = w_down.shape[1]
    cu = w_up.shape[2]
    F = rd * N_DEV
    hv = g_norm_w.shape[1]
    alpha = (2 * L) ** 0.25
    my_x, my_y, my_c = lax.axis_index("x"), lax.axis_index("y"), lax.axis_index("c")
    dev_arr = jnp.reshape(4 * my_x + 2 * my_y + my_c, (1,)).astype(jnp.int32)

    o_a, o_b, o_o, o_d = D, D + rs, D + 2 * rs, D + 3 * rs
    taps = jnp.pad(w_dw, ((0, 0), (0, CONV_HALO - CONV_WIDTH), (0, 0))).reshape(L * CONV_HALO, w_dw.shape[2])
    taps_all = _all_gather_small(taps)
    w_dw_full = taps_all.transpose(1, 0, 2).reshape(L, CONV_HALO, D)

    started, gathered = {}, {}

    def start_gather(key, after):
        l, part = key
        rest = [w_a[l], w_b[l], w_o[l], w_down[l]]
        rows = dict(all=[w_in[l]] + rest, rest=rest)
        if part == "in":
            shards = [w_in[l].astype(ACT_DTYPE)]
        else:
            shards = [jnp.concatenate(rows[part], axis=0).astype(ACT_DTYPE),
                      jnp.swapaxes(w_up[l], 0, 1).astype(ACT_DTYPE)]
        started[key] = _gather_start("gather_start_%d_%s" % key, shards, [_place_own(s, dev_arr) for s in shards],
                                     after)
        return started[key][4]

    def finish_gather(key, after):
        send, recv, thru, zone, _ = started[key]
        _, zn = _gather_wait("gather_wait_%d_%s" % key, thru, zone, send, recv, after)
        gathered[key] = _gather_finish(zn)

    def w_in_of(l):
        return gathered[(l, "in") if l == 0 else (l, "all")][0]

    def rest_of(l):
        ga, gb = gathered[(l, "rest") if l == 0 else (l, "all")]
        base = 0 if l == 0 else D
        return dict(
            w_a=ga[:, base:base + rs, :].reshape(D, D),
            w_b=ga[:, base + rs:base + 2 * rs, :].reshape(D, D),
            w_o=ga[:, base + 2 * rs:base + 3 * rs, :].reshape(D, D),
            w_down=ga[:, base + 3 * rs:base + 3 * rs + rd, :].reshape(F, D),
            w_up=gb.reshape(2 * F, D))

    def weights(l):
        return dict(w_in=w_in_of(l), **rest_of(l))

    lbs = _lb_fwd(lb_logits)

    def params(l):
        return dict(b_in=b_in[l].reshape(N_SEC, 1, D), lbs=lbs[l].reshape(1, D), g_norm_w=g_norm_w[l].reshape(1, hv),
                    w_dw=w_dw_full[l], b_dw=b_dw[l].reshape(1, D), conv_ln_g=conv_ln_g[l].reshape(1, D),
                    conv_ln_b=conv_ln_b[l].reshape(1, D), b_b=b_b[l].reshape(1, D), ln1_g=ln1_g[l], ln1_b=ln1_b[l],
                    ln2_g=ln2_g[l], ln2_b=ln2_b[l])

    x2d = x.reshape(T, D)
    token = start_gather((0, "in"), taps_all)
    token = start_gather((0, "rest"), token)
    if L > 1:
        token = start_gather((1, "all"), token)
    xc, xc_bf = _ln_fwd("ln0", x2d, None, 1.0, ln0_g, ln0_b, dep=token)
    finish_gather((0, "in"), xc_bf)
    saved = []
    for l in range(L):
        if l == 0:
            def rest(after):
                finish_gather((0, "rest"), after)
                return rest_of(0)
            token = None
        else:
            rest = functools.partial(lambda after, l: rest_of(l), l=l)
            token = start_gather((l + 1, "all"), gathered[(l, "all")][0]) if l + 1 < L else None
        xc, xc_bf, s = _layer_fwd(xc, xc_bf, w_in_of(l), rest, params(l), alpha, dep=token)
        saved.append(s)
        if l + 1 < L:
            finish_gather((l + 1, "all"), xc_bf)

    c_arr = jnp.reshape(my_c, (1,)).astype(jnp.int32)
    chip = 2 * my_x + my_y
    dx, loss_row = _loss_fwd_bwd(xc, loss_target.reshape(T, D))
    small = [None] * L
    pending = None
    upd_big = {n: None for n in ("w_in", "w_a", "w_b", "w_o", "w_down", "w_up")}
    wmv = dict(w_in=(w_in, m_w_in, v_w_in), w_a=(w_a, m_w_a, v_w_a), w_b=(w_b, m_w_b, v_w_b),
               w_o=(w_o, m_w_o, v_w_o), w_down=(w_down, m_w_down, v_w_down),
               w_up=tuple(jnp.swapaxes(a, 1, 2) for a in (w_up, m_w_up, v_w_up)))

    held = dict(w_in=("w_in",), w_abo=("w_a", "w_b", "w_o"), w_down=("w_down",), w_up=("w_up",))

    def send_buffer(buf, g):
        return g if g.ndim == 3 else g.reshape(N_DEV, wmv[held[buf][0]][0].shape[1], D)

    def update_layer(l, bufs, q, r2):
        pre = jnp.stack([chip, jnp.int32(l)]).astype(jnp.int32)
        for k, buf in enumerate(bufs):
            for pos, name in enumerate(held[buf]):
                w, m, v = wmv[name]
                r, C = w.shape[1], w.shape[2]
                tr = max(t for t in range(16, 513, 16) if r % t == 0)
                nb = r // tr
                b0 = pos * nb
                specs = [pl.BlockSpec((None, tr, C), functools.partial(lambda i, s, b0: (s[0], b0 + i, 0), b0=b0))]
                specs += [pl.BlockSpec((None, tr, C), functools.partial(lambda i, s, j, b0: (j, b0 + i, 0), j=j, b0=b0))
                          for j in range(3)]
                upd_big[name] = _adamw(
                    "adamw_" + name, w.reshape(L * r, C), m.reshape(L * r, C), v.reshape(L * r, C),
                    [q[k], r2[k], r2[k], r2[k]], specs, tr, prefetch=pre, nsteps=nb,
                    row_map=functools.partial(lambda i, s, nb: (s[1] * nb + i, 0), nb=nb), prev=upd_big[name])

    def pair_sums(bufs, grads):
        sends = [send_buffer(n, grads[n]) for n in bufs]
        return [_pair_add(p, r, c_arr) for p, r in zip(sends, _exchange_sibling(sends))]

    def start_reduce(tag, names, qs, after=None):
        s_send, s_recv, q_thru, zones, token = _exchange_chips_start("reduce_start_" + tag, qs, after)
        return (tag, names, (s_send, s_recv), list(q_thru), list(zones)), token

    def finish_reduce(l, handle, after):
        tag, names, sems, q, zones = handle
        r2 = _exchange_chips_wait("reduce_wait_" + tag, q, zones, sems[0], sems[1], after)
        update_layer(l, names, q, r2)

    names_all = tuple(held)
    names_early = names_all[1:]
    token = None
    first_early = []
    for l in range(L - 1, -1, -1):
        if l > 0:
            dx, big, small[l] = _layer_bwd(dx, saved[l], weights(l), params(l), alpha, dep=token)
            if pending is not None:
                finish_reduce(l + 1, pending, dx)
            pending, token = start_reduce(str(l), names_all, pair_sums(names_all, big))
        else:
            def early(grads):
                if pending is not None:
                    finish_reduce(1, pending, grads["w_abo"])
                handle, tok = start_reduce("0_rest", names_early, pair_sums(names_early, grads))
                first_early.append(handle)
                return tok
            dx, big, small[l] = _layer_bwd(dx, saved[l], weights(l), params(l), alpha, dep=token, early=early)
            q_in = pair_sums(("w_in",), big)
    dx0, _, dln0_g, dln0_b = _ln_bwd("ln0_bwd", x2d, dx, ln0_g)
    dlb_logits = _lb_bwd(lb_logits, jnp.concatenate([small[l]["lbs"] for l in range(L)], axis=0))

    small_l = {n: [small[l][n] for l in range(L)] for n in _SMALL if n != "lb_logits"}
    small_l["lb_logits"] = [dlb_logits[l] for l in range(L)]
    loss_pad = jnp.pad(loss_row, ((0, 0), (0, D - LANES)))
    cw = w_dw.shape[2]
    taps_send = jnp.concatenate([small[l]["w_dw"] for l in range(L)], axis=0)
    taps_send = taps_send.reshape(L * CONV_HALO, N_DEV, cw).transpose(1, 0, 2)
    part = _pack_small(small_l, dln0_g, dln0_b, loss_pad, D, L)
    parts_all, tap_parts = _small_exchanges(part, taps_send)
    n_small = part.shape[0]

    last, _ = start_reduce("0_in", ("w_in",), q_in, after=parts_all)


    inputs = dict(b_in=(b_in, m_b_in, v_b_in), lb_logits=(lb_logits, m_lb_logits, v_lb_logits),
                  g_norm_w=(g_norm_w, m_g_norm_w, v_g_norm_w), b_dw=(b_dw, m_b_dw, v_b_dw),
                  conv_ln_g=(conv_ln_g, m_conv_ln_g, v_conv_ln_g), conv_ln_b=(conv_ln_b, m_conv_ln_b, v_conv_ln_b),
                  b_b=(b_b, m_b_b, v_b_b), ln1_g=(ln1_g, m_ln1_g, v_ln1_g), ln1_b=(ln1_b, m_ln1_b, v_ln1_b),
                  ln2_g=(ln2_g, m_ln2_g, v_ln2_g), ln2_b=(ln2_b, m_ln2_b, v_ln2_b))
    zero_row = jnp.zeros((1, D), F32)
    packed = [_pack_small({n: [inputs[n][i][l] for l in range(L)] for n in _SMALL},
                          (ln0_g, m_ln0_g, v_ln0_g)[i], (ln0_b, m_ln0_b, v_ln0_b)[i], zero_row, D, L)
              for i in range(3)]
    small_specs = [pl.BlockSpec((None, n_small, D), functools.partial(lambda i, d: (d, 0, 0), d=d))
                   for d in range(N_DEV)]
    s_out = _adamw("adamw_small", packed[0], packed[1], packed[2], [parts_all] * N_DEV, small_specs, n_small)
    s_g, n_rows = _unpack_small(s_out[0], D, L, hv)
    s_d, _ = _unpack_small(s_out[1], D, L, hv)
    s_m, _ = _unpack_small(s_out[2], D, L, hv)
    s_v, _ = _unpack_small(s_out[3], D, L, hv)
    loss = s_out[0][n_rows, 0]

    tap_specs = [pl.BlockSpec((None, L * CONV_HALO, cw), functools.partial(lambda i, d: (d, 0, 0), d=d))
                 for d in range(N_DEV)]
    pad_t = lambda a: jnp.pad(a, ((0, 0), (0, CONV_HALO - CONV_WIDTH), (0, 0))).reshape(L * CONV_HALO, cw)
    t_out = _adamw("adamw_taps", pad_t(w_dw), pad_t(m_w_dw), pad_t(v_w_dw), [tap_parts] * N_DEV, tap_specs,
                   L * CONV_HALO)
    finish_reduce(0, first_early[0], t_out[0])
    finish_reduce(0, last, upd_big["w_up"][0])
    upd ={n: [o.reshape(wmv[n][0].shape) for o in outs] for n, outs in upd_big.items()}
    upd["w_up"] = [jnp.swapaxes(o, 1, 2) for o in upd["w_up"]]
    upd["w_dw"] = [o.reshape(L, CONV_HALO, cw)[:, :CONV_WIDTH, :] for o in t_out]

    order = ["ln0_g", "ln0_b", "w_in", "b_in", "lb_logits", "g_norm_w", "w_a", "w_dw", "b_dw", "conv_ln_g",
             "conv_ln_b", "w_b", "b_b", "w_o", "ln1_g", "ln1_b", "w_up", "w_down", "ln2_g", "ln2_b"]
    small_sets = (s_g, s_d, s_m, s_v)
    outs = [loss, dx0.reshape(x.shape)]
    for i in range(4):
        for n in order:
            outs.append(upd[n][i] if n in upd else small_sets[i][n])
    return tuple(outs)
```
